```python
import jax, jax.numpy as jnp
from jax import lax
import numpy as np

D_MODEL = 1024
BATCH = 8
SEQ = 16384
DEPTH = 1

HEAD_DIM = 64
N_FOX_HEADS = 8
N_DIL_HEADS = 8
FOX_WIDTH = N_FOX_HEADS * HEAD_DIM
DIL_WIDTH = N_DIL_HEADS * HEAD_DIM
MIX_WIDTH = FOX_WIDTH + DIL_WIDTH
DILATED_PATTERNS = ((128, 1), (512, 4), (2048, 16))
Q_BLOCK = 128
N_MEM = 256
N_MEM_HEADS = 4
MEM_WIDTH = N_MEM_HEADS * HEAD_DIM
N_BUCKETS = 32
MAX_DISTANCE = 2048
D_FF = -(-8 * D_MODEL // (3 * 256)) * 256
RMS_EPS = 1e-6
IN_SIZES = (FOX_WIDTH, FOX_WIDTH, FOX_WIDTH, N_FOX_HEADS, DIL_WIDTH, DIL_WIDTH, DIL_WIDTH)
IN_COLS = sum(IN_SIZES)

kernel_name = "hybrid_fox_dilated_memxattn_swiglu"


def rmsnorm(x, g):
    xf = x.astype(jnp.float32)
    y = xf * lax.rsqrt(jnp.mean(xf * xf, axis=-1, keepdims=True) + RMS_EPS)
    return (y * g.astype(jnp.float32)).astype(x.dtype)


def split_heads(t, n):
    B, S, _ = t.shape
    return t.reshape(B, S, n, HEAD_DIM).transpose(0, 2, 1, 3)


def merge_heads(t):
    B, H, S, Dh = t.shape
    return t.transpose(0, 2, 1, 3).reshape(B, S, H * Dh)


def t5_causal_bucket(dist):
    max_exact = N_BUCKETS // 2
    d = np.maximum(dist, 1).astype(np.float32)
    large = max_exact + (np.log(d / max_exact) / np.log(MAX_DISTANCE / max_exact)
                         * (N_BUCKETS - max_exact)).astype(np.int32)
    large = np.minimum(large, N_BUCKETS - 1)
    return np.where(dist < max_exact, dist, large).astype(np.int32)


def forgetting_attention(q, k, v, log_f):
    B, H, S, Dh = q.shape
    nb = S // Q_BLOCK
    c = jnp.cumsum(log_f, axis=-1)
    qb = q.reshape(B, H, nb, Q_BLOCK, Dh).transpose(2, 0, 1, 3, 4)
    cb = c.reshape(B, H, nb, Q_BLOCK).transpose(2, 0, 1, 3)
    key_pos = jnp.arange(S)
    scale = Dh ** -0.5

    def block(args):
        qi, ci, i = args
        s = jnp.einsum('bhqd,bhkd->bhqk', qi, k).astype(jnp.float32) * scale
        s = s + ci[..., :, None] - c[..., None, :]
        q_pos = i * Q_BLOCK + jnp.arange(Q_BLOCK)
        s = jnp.where(key_pos[None, :] <= q_pos[:, None], s, -jnp.inf)
        p = jax.nn.softmax(s, axis=-1)
        return jnp.einsum('bhqk,bhkd->bhqd', p.astype(v.dtype), v)

    o = lax.map(block, (qb, cb, jnp.arange(nb)))
    return o.transpose(1, 2, 0, 3, 4).reshape(B, H, S, Dh)


def dilated_branch(q, k, v, rel_bias, window, dilation):
    B, H, S, Dh = q.shape
    w = window // dilation
    span = w * dilation
    S_pad = -(-S // span) * span
    L = S_pad // dilation
    nb = L // w

    def split(t):
        t = jnp.pad(t, ((0, 0), (0, 0), (0, S_pad - S), (0, 0)))
        t = t.reshape(B, H, L, dilation, Dh).transpose(0, 1, 3, 2, 4)
        return t.reshape(B, H, dilation, nb, w, Dh)

    def with_prev(t):
        prev = jnp.pad(t[:, :, :, :-1], ((0, 0), (0, 0), (0, 0), (1, 0), (0, 0), (0, 0)))
        return jnp.concatenate([prev, t], axis=4)

    qs = split(q)
    kk = with_prev(split(k))
    vv = with_prev(split(v))
    s = jnp.einsum('bhrnqd,bhrnkd->bhrnqk', qs, kk).astype(jnp.float32) * (Dh ** -0.5)

    qi = np.arange(w)[:, None]
    kj = np.arange(2 * w)[None, :]
    sub_dist = qi + w - kj
    band = (sub_dist >= 0) & (sub_dist <= w)
    bucket = t5_causal_bucket(np.clip(sub_dist, 0, w) * dilation)
    bias = rel_bias.astype(jnp.float32)[bucket]
    s = s + jnp.transpose(bias, (2, 0, 1))[None, :, None, None]
    not_first = (np.arange(nb)[:, None, None] > 0) | (kj[None] >= w)
    mask = jnp.asarray(band[None] & not_first)
    s = jnp.where(mask, s, -jnp.inf)

    m = jnp.max(s, axis=-1, keepdims=True)
    e = jnp.exp(s - m)
    l = jnp.sum(e, axis=-1, keepdims=True)
    o = jnp.einsum('bhrnqk,bhrnkd->bhrnqd', (e / l).astype(v.dtype), vv)
    lse = (m + jnp.log(l))[..., 0]

    def merge(t):
        t = t.reshape(B, H, dilation, L, *t.shape[5:])
        t = jnp.moveaxis(t, 2, 3)
        t = t.reshape(B, H, S_pad, *t.shape[4:])
        return t[:, :, :S]

    return merge(o), merge(lse)


def dilated_attention(q, k, v, rel_bias):
    outs, lses = [], []
    for window, dilation in DILATED_PATTERNS:
        o, lse = dilated_branch(q, k, v, rel_bias, window, dilation)
        outs.append(o)
        lses.append(lse)
    alpha = jax.nn.softmax(jnp.stack(lses, axis=0), axis=0)
    return jnp.einsum('pbhs,pbhsd->bhsd', alpha.astype(v.dtype), jnp.stack(outs, axis=0))


def memory_cross_attention(h, hm, w_xq, w_xk, w_xv, w_xo):
    q = split_heads(h @ w_xq, N_MEM_HEADS)
    k = split_heads(hm @ w_xk, N_MEM_HEADS)
    v = split_heads(hm @ w_xv, N_MEM_HEADS)
    s = jnp.einsum('bhqd,bhkd->bhqk', q, k).astype(jnp.float32) * (HEAD_DIM ** -0.5)
    p = jax.nn.softmax(s, axis=-1)
    o = jnp.einsum('bhqk,bhkd->bhqd', p.astype(v.dtype), v)
    return merge_heads(o) @ w_xo


def _fwd_setup_inputs(seed: int = 0) -> dict:
    key = jax.random.key(seed)
    ks = jax.random.split(key, 24)
    nrm = jax.random.normal

    def w(k, shape, fan_in):
        return nrm(k, shape, jnp.float32) * fan_in ** -0.5

    def gain(k):
        return 1.0 + 0.1 * nrm(k, (DEPTH, D_MODEL), jnp.float32)

    return {
        "x": nrm(ks[0], (BATCH, SEQ, D_MODEL), jnp.float32),
        "mem": nrm(ks[1], (BATCH, N_MEM, D_MODEL), jnp.float32),
        "g_mix_pre": gain(ks[2]),
        "w_in": w(ks[3], (DEPTH, D_MODEL, IN_COLS), D_MODEL),
        "b_f": 2.0 + 0.5 * nrm(ks[4], (DEPTH, N_FOX_HEADS), jnp.float32),
        "rel_bias": 0.5 * nrm(ks[5], (N_BUCKETS, N_DIL_HEADS), jnp.float32),
        "w_out": w(ks[6], (DEPTH, MIX_WIDTH, D_MODEL), MIX_WIDTH),
        "g_mix_post": gain(ks[7]),
        "g_xattn_pre": gain(ks[8]),
        "g_mem": gain(ks[9]),
        "w_xq": w(ks[10], (DEPTH, D_MODEL, MEM_WIDTH), D_MODEL),
        "w_xk": w(ks[11], (DEPTH, D_MODEL, MEM_WIDTH), D_MODEL),
        "w_xv": w(ks[12], (DEPTH, D_MODEL, MEM_WIDTH), D_MODEL),
        "w_xo": w(ks[13], (DEPTH, MEM_WIDTH, D_MODEL), MEM_WIDTH),
        "g_xattn_post": gain(ks[14]),
        "g_ffn_pre": gain(ks[15]),
        "w_gate": w(ks[16], (DEPTH, D_MODEL, D_FF), D_MODEL),
        "w_up": w(ks[17], (DEPTH, D_MODEL, D_FF), D_MODEL),
        "w_down": w(ks[18], (DEPTH, D_FF, D_MODEL), D_FF),
        "g_ffn_post": gain(ks[19]),
    }


def _fwd_reference(x, mem, g_mix_pre, w_in, b_f, rel_bias, w_out, g_mix_post,
              g_xattn_pre, g_mem, w_xq, w_xk, w_xv, w_xo, g_xattn_post,
              g_ffn_pre, w_gate, w_up, w_down, g_ffn_post):
    split_points = [int(p) for p in np.cumsum(IN_SIZES)[:-1]]
    for layer in range(DEPTH):
        h = rmsnorm(x, g_mix_pre[layer])
        proj = h @ w_in[layer]
        fq, fk, fv, fgate, dq, dk, dv = jnp.split(proj, split_points, axis=-1)
        log_f = jax.nn.log_sigmoid((fgate + b_f[layer]).astype(jnp.float32))
        log_f = log_f.transpose(0, 2, 1)
        o_fox = forgetting_attention(split_heads(fq, N_FOX_HEADS), split_heads(fk, N_FOX_HEADS),
                                     split_heads(fv, N_FOX_HEADS), log_f)
        o_dil = dilated_attention(split_heads(dq, N_DIL_HEADS), split_heads(dk, N_DIL_HEADS),
                                  split_heads(dv, N_DIL_HEADS), rel_bias)
        o = merge_heads(jnp.concatenate([o_fox, o_dil], axis=1))
        x = x + rmsnorm(o @ w_out[layer], g_mix_post[layer])

        h = rmsnorm(x, g_xattn_pre[layer])
        hm = rmsnorm(mem, g_mem[layer])
        y = memory_cross_attention(h, hm, w_xq[layer], w_xk[layer], w_xv[layer], w_xo[layer])
        x = x + rmsnorm(y, g_xattn_post[layer])

        h = rmsnorm(x, g_ffn_pre[layer])
        y = (jax.nn.silu(h @ w_gate[layer]) * (h @ w_up[layer])) @ w_down[layer]
        x = x + rmsnorm(y, g_ffn_post[layer])
    return x


import jax as _jax
import jax.numpy as _jnp

TWIN_FORMAT = 'train_step'
FWD_PARAMS = ['x', 'mem', 'g_mix_pre', 'w_in', 'b_f', 'rel_bias', 'w_out', 'g_mix_post', 'g_xattn_pre', 'g_mem', 'w_xq', 'w_xk', 'w_xv', 'w_xo', 'g_xattn_post', 'g_ffn_pre', 'w_gate', 'w_up', 'w_down', 'g_ffn_post']
TWIN_WEIGHTS = ['g_mix_pre', 'w_in', 'b_f', 'rel_bias', 'w_out', 'g_mix_post', 'g_xattn_pre', 'g_mem', 'w_xq', 'w_xk', 'w_xv', 'w_xo', 'g_xattn_post', 'g_ffn_pre', 'w_gate', 'w_up', 'w_down', 'g_ffn_post']
TWIN_DIFF_INPUT = 'x'
TWIN_INPUTS = ['x', 'mem', 'g_mix_pre', 'w_in', 'b_f', 'rel_bias', 'w_out', 'g_mix_post', 'g_xattn_pre', 'g_mem', 'w_xq', 'w_xk', 'w_xv', 'w_xo', 'g_xattn_post', 'g_ffn_pre', 'w_gate', 'w_up', 'w_down', 'g_ffn_post', 'loss_target', 'm_g_mix_pre', 'm_w_in', 'm_b_f', 'm_rel_bias', 'm_w_out', 'm_g_mix_post', 'm_g_xattn_pre', 'm_g_mem', 'm_w_xq', 'm_w_xk', 'm_w_xv', 'm_w_xo', 'm_g_xattn_post', 'm_g_ffn_pre', 'm_w_gate', 'm_w_up', 'm_w_down', 'm_g_ffn_post', 'v_g_mix_pre', 'v_w_in', 'v_b_f', 'v_rel_bias', 'v_w_out', 'v_g_mix_post', 'v_g_xattn_pre', 'v_g_mem', 'v_w_xq', 'v_w_xk', 'v_w_xv', 'v_w_xo', 'v_g_xattn_post', 'v_g_ffn_pre', 'v_w_gate', 'v_w_up', 'v_w_down', 'v_g_ffn_post']
TWIN_OUTPUTS = ['loss', 'grad_x', 'grad_g_mix_pre', 'grad_w_in', 'grad_b_f', 'grad_rel_bias', 'grad_w_out', 'grad_g_mix_post', 'grad_g_xattn_pre', 'grad_g_mem', 'grad_w_xq', 'grad_w_xk', 'grad_w_xv', 'grad_w_xo', 'grad_g_xattn_post', 'grad_g_ffn_pre', 'grad_w_gate', 'grad_w_up', 'grad_w_down', 'grad_g_ffn_post', 'delta_g_mix_pre', 'delta_w_in', 'delta_b_f', 'delta_rel_bias', 'delta_w_out', 'delta_g_mix_post', 'delta_g_xattn_pre', 'delta_g_mem', 'delta_w_xq', 'delta_w_xk', 'delta_w_xv', 'delta_w_xo', 'delta_g_xattn_post', 'delta_g_ffn_pre', 'delta_w_gate', 'delta_w_up', 'delta_w_down', 'delta_g_ffn_post', 'new_m_g_mix_pre', 'new_m_w_in', 'new_m_b_f', 'new_m_rel_bias', 'new_m_w_out', 'new_m_g_mix_post', 'new_m_g_xattn_pre', 'new_m_g_mem', 'new_m_w_xq', 'new_m_w_xk', 'new_m_w_xv', 'new_m_w_xo', 'new_m_g_xattn_post', 'new_m_g_ffn_pre', 'new_m_w_gate', 'new_m_w_up', 'new_m_w_down', 'new_m_g_ffn_post', 'new_v_g_mix_pre', 'new_v_w_in', 'new_v_b_f', 'new_v_rel_bias', 'new_v_w_out', 'new_v_g_mix_post', 'new_v_g_xattn_pre', 'new_v_g_mem', 'new_v_w_xq', 'new_v_w_xk', 'new_v_w_xv', 'new_v_w_xo', 'new_v_g_xattn_post', 'new_v_g_ffn_pre', 'new_v_w_gate', 'new_v_w_up', 'new_v_w_down', 'new_v_g_ffn_post']
TWIN_LEAF_KINDS = {'loss': 'loss', 'grad_x': 'grad_x', 'grad_g_mix_pre': 'grad_w', 'grad_w_in': 'grad_w', 'grad_b_f': 'grad_w', 'grad_rel_bias': 'grad_w', 'grad_w_out': 'grad_w', 'grad_g_mix_post': 'grad_w', 'grad_g_xattn_pre': 'grad_w', 'grad_g_mem': 'grad_w', 'grad_w_xq': 'grad_w', 'grad_w_xk': 'grad_w', 'grad_w_xv': 'grad_w', 'grad_w_xo': 'grad_w', 'grad_g_xattn_post': 'grad_w', 'grad_g_ffn_pre': 'grad_w', 'grad_w_gate': 'grad_w', 'grad_w_up': 'grad_w', 'grad_w_down': 'grad_w', 'grad_g_ffn_post': 'grad_w', 'delta_g_mix_pre': 'delta_w', 'delta_w_in': 'delta_w', 'delta_b_f': 'delta_w', 'delta_rel_bias': 'delta_w', 'delta_w_out': 'delta_w', 'delta_g_mix_post': 'delta_w', 'delta_g_xattn_pre': 'delta_w', 'delta_g_mem': 'delta_w', 'delta_w_xq': 'delta_w', 'delta_w_xk': 'delta_w', 'delta_w_xv': 'delta_w', 'delta_w_xo': 'delta_w', 'delta_g_xattn_post': 'delta_w', 'delta_g_ffn_pre': 'delta_w', 'delta_w_gate': 'delta_w', 'delta_w_up': 'delta_w', 'delta_w_down': 'delta_w', 'delta_g_ffn_post': 'delta_w', 'new_m_g_mix_pre': 'new_m', 'new_m_w_in': 'new_m', 'new_m_b_f': 'new_m', 'new_m_rel_bias': 'new_m', 'new_m_w_out': 'new_m', 'new_m_g_mix_post': 'new_m', 'new_m_g_xattn_pre': 'new_m', 'new_m_g_mem': 'new_m', 'new_m_w_xq': 'new_m', 'new_m_w_xk': 'new_m', 'new_m_w_xv': 'new_m', 'new_m_w_xo': 'new_m', 'new_m_g_xattn_post': 'new_m', 'new_m_g_ffn_pre': 'new_m', 'new_m_w_gate': 'new_m', 'new_m_w_up': 'new_m', 'new_m_w_down': 'new_m', 'new_m_g_ffn_post': 'new_m', 'new_v_g_mix_pre': 'new_v', 'new_v_w_in': 'new_v', 'new_v_b_f': 'new_v', 'new_v_rel_bias': 'new_v', 'new_v_w_out': 'new_v', 'new_v_g_mix_post': 'new_v', 'new_v_g_xattn_pre': 'new_v', 'new_v_g_mem': 'new_v', 'new_v_w_xq': 'new_v', 'new_v_w_xk': 'new_v', 'new_v_w_xv': 'new_v', 'new_v_w_xo': 'new_v', 'new_v_g_xattn_post': 'new_v', 'new_v_g_ffn_pre': 'new_v', 'new_v_w_gate': 'new_v', 'new_v_w_up': 'new_v', 'new_v_w_down': 'new_v', 'new_v_g_ffn_post': 'new_v'}


def _forward(args):
    return _fwd_reference(*[args[k] for k in FWD_PARAMS])


def _output_shape():
    def fwd():
        inp = _fwd_setup_inputs(0)
        return _fwd_reference(*[inp[k] for k in FWD_PARAMS])
    out = _jax.eval_shape(fwd)
    return out.shape, out.dtype

N_MICROBATCH = 1
ADAM_LR = 0.001
ADAM_B1 = 0.9
ADAM_B2 = 0.999
ADAM_EPS = 1e-08
ADAM_WD = 0.01
ADAM_STEP = 10
PER_EXAMPLE_BATCH_AXIS = {'x': 0, 'mem': 0, 'loss_target': 0}
SHARED_INPUTS = []
_WEIGHT_DTYPES = {'g_mix_pre': _jnp.float32, 'w_in': _jnp.float32, 'b_f': _jnp.float32, 'rel_bias': _jnp.float32, 'w_out': _jnp.float32, 'g_mix_post': _jnp.float32, 'g_xattn_pre': _jnp.float32, 'g_mem': _jnp.float32, 'w_xq': _jnp.float32, 'w_xk': _jnp.float32, 'w_xv': _jnp.float32, 'w_xo': _jnp.float32, 'g_xattn_post': _jnp.float32, 'g_ffn_pre': _jnp.float32, 'w_gate': _jnp.float32, 'w_up': _jnp.float32, 'w_down': _jnp.float32, 'g_ffn_post': _jnp.float32}
MOMENT_SCALE = {'g_mix_pre': 2.410155e+00, 'w_in': 1.246375e+00, 'b_f': 4.837556e+01, 'rel_bias': 9.366870e-01, 'w_out': 2.133831e+00, 'g_mix_post': 1.280289e+02, 'g_xattn_pre': 1.491418e+00, 'g_mem': 5.154830e+00, 'w_xq': 3.237131e+00, 'w_xk': 3.564887e+00, 'w_xv': 9.669805e+00, 'w_xo': 5.303300e+00, 'g_xattn_post': 1.303144e+02, 'g_ffn_pre': 3.431132e+00, 'w_gate': 1.063824e+00, 'w_up': 2.104623e+00, 'w_down': 3.593211e+00, 'g_ffn_post': 1.283477e+02}


def _to_microbatches(a, axis):
    t = _jnp.moveaxis(a, axis, 0)
    t = t.reshape((N_MICROBATCH, t.shape[0] // N_MICROBATCH) + t.shape[1:])
    return _jnp.moveaxis(t, 1, axis + 1)


def setup_inputs(seed: int = 0) -> dict:
    inp = _fwd_setup_inputs(seed)
    key = _jax.random.fold_in(_jax.random.key(seed), 7919)
    shape, _ = _output_shape()
    out = dict(inp)
    out["loss_target"] = _jax.random.normal(_jax.random.fold_in(key, 0), shape, _jnp.float32)
    for i, name in enumerate(TWIN_WEIGHTS):
        w = inp[name].astype(_jnp.float32)
        if MOMENT_SCALE is None:
            s = _jnp.sqrt(_jnp.mean(_jnp.square(w)) + 1e-30)
        else:
            s = MOMENT_SCALE[name]
        km, kv = _jax.random.split(_jax.random.fold_in(key, i + 1))
        out[name] = w
        out["m_" + name] = s * _jax.random.normal(km, w.shape, _jnp.float32)
        out["v_" + name] = (s * s) * _jax.random.uniform(kv, w.shape, _jnp.float32, 0.5, 1.5)
    if N_MICROBATCH > 1:
        for name, axis in PER_EXAMPLE_BATCH_AXIS.items():
            out[name] = _to_microbatches(out[name], axis)
    return {'x': out['x'], 'mem': out['mem'], 'g_mix_pre': out['g_mix_pre'], 'w_in': out['w_in'], 'b_f': out['b_f'], 'rel_bias': out['rel_bias'], 'w_out': out['w_out'], 'g_mix_post': out['g_mix_post'], 'g_xattn_pre': out['g_xattn_pre'], 'g_mem': out['g_mem'], 'w_xq': out['w_xq'], 'w_xk': out['w_xk'], 'w_xv': out['w_xv'], 'w_xo': out['w_xo'], 'g_xattn_post': out['g_xattn_post'], 'g_ffn_pre': out['g_ffn_pre'], 'w_gate': out['w_gate'], 'w_up': out['w_up'], 'w_down': out['w_down'], 'g_ffn_post': out['g_ffn_post'], 'loss_target': out['loss_target'], 'm_g_mix_pre': out['m_g_mix_pre'], 'm_w_in': out['m_w_in'], 'm_b_f': out['m_b_f'], 'm_rel_bias': out['m_rel_bias'], 'm_w_out': out['m_w_out'], 'm_g_mix_post': out['m_g_mix_post'], 'm_g_xattn_pre': out['m_g_xattn_pre'], 'm_g_mem': out['m_g_mem'], 'm_w_xq': out['m_w_xq'], 'm_w_xk': out['m_w_xk'], 'm_w_xv': out['m_w_xv'], 'm_w_xo': out['m_w_xo'], 'm_g_xattn_post': out['m_g_xattn_post'], 'm_g_ffn_pre': out['m_g_ffn_pre'], 'm_w_gate': out['m_w_gate'], 'm_w_up': out['m_w_up'], 'm_w_down': out['m_w_down'], 'm_g_ffn_post': out['m_g_ffn_post'], 'v_g_mix_pre': out['v_g_mix_pre'], 'v_w_in': out['v_w_in'], 'v_b_f': out['v_b_f'], 'v_rel_bias': out['v_rel_bias'], 'v_w_out': out['v_w_out'], 'v_g_mix_post': out['v_g_mix_post'], 'v_g_xattn_pre': out['v_g_xattn_pre'], 'v_g_mem': out['v_g_mem'], 'v_w_xq': out['v_w_xq'], 'v_w_xk': out['v_w_xk'], 'v_w_xv': out['v_w_xv'], 'v_w_xo': out['v_w_xo'], 'v_g_xattn_post': out['v_g_xattn_post'], 'v_g_ffn_pre': out['v_g_ffn_pre'], 'v_w_gate': out['v_w_gate'], 'v_w_up': out['v_w_up'], 'v_w_down': out['v_w_down'], 'v_g_ffn_post': out['v_g_ffn_post']}


def _loss(weights, diff, rest, loss_target):
    with _jax.named_scope("forward"):
        args = {**rest, TWIN_DIFF_INPUT: diff, **{k: w.astype(_WEIGHT_DTYPES[k]) for k, w in weights.items()}}
        y = _forward(args)
    with _jax.named_scope("loss_head"):
        err = _jnp.square(y.astype(_jnp.float32) - loss_target)
        return 0.5 * _jnp.sum(_jnp.mean(err, axis=-1)) if err.ndim else 0.5 * err


def _adamw(w, g, m, v):
    m = ADAM_B1 * m + (1.0 - ADAM_B1) * g
    v = ADAM_B2 * v + (1.0 - ADAM_B2) * _jnp.square(g)
    m_hat = m / (1.0 - ADAM_B1 ** ADAM_STEP)
    v_hat = v / (1.0 - ADAM_B2 ** ADAM_STEP)
    delta = -ADAM_LR * (m_hat / (_jnp.sqrt(v_hat) + ADAM_EPS) + ADAM_WD * w)
    return delta, m, v


def reference(x, mem, g_mix_pre, w_in, b_f, rel_bias, w_out, g_mix_post, g_xattn_pre, g_mem, w_xq, w_xk, w_xv, w_xo, g_xattn_post, g_ffn_pre, w_gate, w_up, w_down, g_ffn_post, loss_target, m_g_mix_pre, m_w_in, m_b_f, m_rel_bias, m_w_out, m_g_mix_post, m_g_xattn_pre, m_g_mem, m_w_xq, m_w_xk, m_w_xv, m_w_xo, m_g_xattn_post, m_g_ffn_pre, m_w_gate, m_w_up, m_w_down, m_g_ffn_post, v_g_mix_pre, v_w_in, v_b_f, v_rel_bias, v_w_out, v_g_mix_post, v_g_xattn_pre, v_g_mem, v_w_xq, v_w_xk, v_w_xv, v_w_xo, v_g_xattn_post, v_g_ffn_pre, v_w_gate, v_w_up, v_w_down, v_g_ffn_post):
    given = dict(x=x, mem=mem, g_mix_pre=g_mix_pre, w_in=w_in, b_f=b_f, rel_bias=rel_bias, w_out=w_out, g_mix_post=g_mix_post, g_xattn_pre=g_xattn_pre, g_mem=g_mem, w_xq=w_xq, w_xk=w_xk, w_xv=w_xv, w_xo=w_xo, g_xattn_post=g_xattn_post, g_ffn_pre=g_ffn_pre, w_gate=w_gate, w_up=w_up, w_down=w_down, g_ffn_post=g_ffn_post, loss_target=loss_target, m_g_mix_pre=m_g_mix_pre, m_w_in=m_w_in, m_b_f=m_b_f, m_rel_bias=m_rel_bias, m_w_out=m_w_out, m_g_mix_post=m_g_mix_post, m_g_xattn_pre=m_g_xattn_pre, m_g_mem=m_g_mem, m_w_xq=m_w_xq, m_w_xk=m_w_xk, m_w_xv=m_w_xv, m_w_xo=m_w_xo, m_g_xattn_post=m_g_xattn_post, m_g_ffn_pre=m_g_ffn_pre, m_w_gate=m_w_gate, m_w_up=m_w_up, m_w_down=m_w_down, m_g_ffn_post=m_g_ffn_post, v_g_mix_pre=v_g_mix_pre, v_w_in=v_w_in, v_b_f=v_b_f, v_rel_bias=v_rel_bias, v_w_out=v_w_out, v_g_mix_post=v_g_mix_post, v_g_xattn_pre=v_g_xattn_pre, v_g_mem=v_g_mem, v_w_xq=v_w_xq, v_w_xk=v_w_xk, v_w_xv=v_w_xv, v_w_xo=v_w_xo, v_g_xattn_post=v_g_xattn_post, v_g_ffn_pre=v_g_ffn_pre, v_w_gate=v_w_gate, v_w_up=v_w_up, v_w_down=v_w_down, v_g_ffn_post=v_g_ffn_post)
    weights = {n: given[n] for n in TWIN_WEIGHTS}
    shared = {n: given[n] for n in SHARED_INPUTS}
    per_example = {n: given[n] for n in ['x', 'mem']}
    grad_fn = _jax.value_and_grad(_loss, argnums=(0, 1))

    def one_microbatch(ex, loss_target):
        ex = dict(ex)
        diff = ex.pop(TWIN_DIFF_INPUT)
        return grad_fn(weights, diff, {**shared, **ex}, loss_target)

    if N_MICROBATCH == 1:
        loss, (grad_w, grad_x) = one_microbatch(per_example, given["loss_target"])
    else:
        def body(carry, xs):
            loss_sum, grad_sum = carry
            l_k, (gw_k, gx_k) = one_microbatch(xs[0], xs[1])
            with _jax.named_scope("update"):
                return (loss_sum + l_k, _jax.tree.map(_jnp.add, grad_sum, gw_k)), gx_k

        init = (_jnp.zeros((), _jnp.float32), _jax.tree.map(_jnp.zeros_like, weights))
        (loss, grad_w), grad_x = _jax.lax.scan(body, init, (per_example, given["loss_target"]))
    with _jax.named_scope("update"):
        delta_w, new_m, new_v = {}, {}, {}
        for n in TWIN_WEIGHTS:
            delta_w[n], new_m[n], new_v[n] = _adamw(weights[n], grad_w[n], given["m_" + n], given["v_" + n])
    return (loss, grad_x, *[grad_w[n] for n in TWIN_WEIGHTS], *[delta_w[n] for n in TWIN_WEIGHTS],
            *[new_m[n] for n in TWIN_WEIGHTS], *[new_v[n] for n in TWIN_WEIGHTS])
```

```python
import functools

import numpy as np
import jax
import jax.numpy as jnp
from jax import lax
from jax.experimental import pallas as pl
from jax.experimental.pallas import tpu as pltpu

F32 = jnp.float32
BF16 = jnp.bfloat16
HIGHEST = lax.Precision.HIGHEST

RMS_EPS = 1e-6
HEAD_DIM = 64
N_HEADS = 8
HEAD_WIDTH = N_HEADS * HEAD_DIM
PAIR = 2 * HEAD_DIM
N_PAIRS = N_HEADS // 2
DIL_BLOCK = 128
DILATIONS = (1, 4, 16)
N_BUCKETS = 32
MAX_DISTANCE = 2048
N_MEM_HEADS = 4
QK_SCALE = HEAD_DIM ** -0.5
NEG = -1e30
N_DEV = 8

ADAM_LR = 0.001
ADAM_B1 = 0.9
ADAM_B2 = 0.999
ADAM_EPS = 1e-08
ADAM_WD = 0.01
ADAM_STEP = 10

V7X_VMEM_LIMIT_BYTES = 56 * 2 ** 20
ROW_TILE = 256
ATT_BLOCK = 512
SCAN_TILE = 256


def _call(body, *, name, grid, in_specs, out_specs, out_shape, scratch=()):
    return pl.pallas_call(
        body, name=name, grid=grid, in_specs=in_specs, out_specs=out_specs, out_shape=out_shape,
        scratch_shapes=list(scratch),
        compiler_params=pltpu.CompilerParams(
            dimension_semantics=("arbitrary",) * len(grid), vmem_limit_bytes=V7X_VMEM_LIMIT_BYTES))


def _rows(tm, n):
    return pl.BlockSpec((tm, n), lambda i: (i, 0))


def _resident(shape):
    zeros = (0,) * len(shape)
    return pl.BlockSpec(shape, lambda i: zeros, pipeline_mode=pl.Buffered(1))


def _acc_out(shape):
    zeros = (0,) * len(shape)
    return pl.BlockSpec(shape, lambda i: zeros)


def _sds(shape, dtype):
    return jax.ShapeDtypeStruct(shape, dtype)


def _dot(a, b):
    return jnp.dot(a, b, preferred_element_type=F32)


def _dot_nt(a, b):
    return lax.dot_general(a, b, (((1,), (1,)), ((), ())), preferred_element_type=F32)


def _dot_tn(a, b):
    return lax.dot_general(a, b, (((0,), (0,)), ((), ())), preferred_element_type=F32)


def _rms_fwd(x, g):
    r = lax.rsqrt(jnp.mean(x * x, axis=-1, keepdims=True) + RMS_EPS)
    return (x * r) * g


def _rms_bwd(xin, g, dy):
    r = lax.rsqrt(jnp.mean(xin * xin, axis=-1, keepdims=True) + RMS_EPS)
    xhat = xin * r
    dg = jnp.sum(dy * xhat, axis=0, keepdims=True)
    dxh = dy * g
    dx = r * (dxh - xhat * jnp.mean(dxh * xhat, axis=-1, keepdims=True))
    return dx, dg


def _first_head_lanes():
    return lax.broadcasted_iota(jnp.int32, (1, PAIR), 1) < HEAD_DIM


def _pick(mask, a, b):
    return jnp.where(mask, a, b)


def _zero_other(mask, v):
    return jnp.where(mask, v, jnp.zeros_like(v))


def _in_proj(x, g, w):
    s_len, d = x.shape
    tm = ROW_TILE
    hw = HEAD_WIDTH

    def body(x_ref, g_ref, w_ref, h_ref, fqkv_ref, gx_ref, dqkv_ref):
        h = _rms_fwd(x_ref[...], g_ref[...]).astype(BF16)
        h_ref[...] = h
        proj = _dot(h, w_ref[...])
        fqkv_ref[:, 0:hw] = (proj[:, 0:hw] * QK_SCALE).astype(BF16)
        fqkv_ref[:, hw:3 * hw] = proj[:, hw:3 * hw].astype(BF16)
        gx_ref[...] = proj[:, 3 * hw:4 * hw]
        dqkv_ref[:, 0:hw] = (proj[:, 4 * hw:5 * hw] * QK_SCALE).astype(BF16)
        dqkv_ref[:, hw:3 * hw] = proj[:, 5 * hw:7 * hw].astype(BF16)

    return _call(
        body, name="in_proj", grid=(s_len // tm,),
        in_specs=[_rows(tm, d), _resident((1, d)), _resident(w.shape)],
        out_specs=[_rows(tm, d), _rows(tm, 3 * hw), _rows(tm, hw), _rows(tm, 3 * hw)],
        out_shape=[_sds((s_len, d), BF16), _sds((s_len, 3 * hw), BF16), _sds((s_len, hw), F32), _sds((s_len, 3 * hw), BF16)],
    )(x, g, w)


def _gate_scan(gx, b_exp):
    s_len, hw = gx.shape
    t = min(SCAN_TILE, s_len)
    tri = jnp.asarray(np.tril(np.ones((t, t), np.float32)))
    sel = np.zeros((N_HEADS, hw), np.float32)
    sel[np.arange(N_HEADS), np.arange(N_HEADS) * HEAD_DIM] = 1.0
    sel = jnp.asarray(sel)

    def body(gx_ref, b_ref, tri_ref, sel_ref, c_ref, crow_ref, carry):
        @pl.when(pl.program_id(0) == 0)
        def _():
            carry[...] = jnp.zeros_like(carry)

        z = gx_ref[...] + b_ref[...]
        lf = jnp.minimum(z, 0.0) - jnp.log1p(jnp.exp(-jnp.abs(z)))
        c = jnp.dot(tri_ref[...], lf, precision=HIGHEST, preferred_element_type=F32) + carry[...]
        c_ref[...] = c
        carry[...] = c[t - 1:t, :]
        crow_ref[...] = lax.dot_general(sel_ref[...], c, (((1,), (1,)), ((), ())), precision=HIGHEST,
                                        preferred_element_type=F32)

    return _call(
        body, name="gate_scan", grid=(s_len // t,),
        in_specs=[_rows(t, hw), _resident((1, hw)), _resident((t, t)), _resident((N_HEADS, hw))],
        out_specs=[_rows(t, hw), pl.BlockSpec((N_HEADS, t), lambda i: (0, i))],
        out_shape=[_sds((s_len, hw), F32), _sds((N_HEADS, s_len), F32)],
        scratch=[pltpu.VMEM((1, hw), F32)],
    )(gx, b_exp, tri, sel)


def _fox_scores(qm, k2, cc_ref, cr_ref, a, causal):
    s = _dot_nt(qm, k2)
    s = s + (cc_ref[:, a * HEAD_DIM:a * HEAD_DIM + 1] - cr_ref[a:a + 1, :])
    return jnp.where(causal, s, NEG)


def _causal_mask(qi, kj, bq, bk):
    row = lax.broadcasted_iota(jnp.int32, (bq, bk), 0) + qi * bq
    col = lax.broadcasted_iota(jnp.int32, (bq, bk), 1) + kj * bk
    return col <= row


def _fox_fwd(fqkv, c_col, c_row):
    s_len = fqkv.shape[0]
    bq = bk = min(ATT_BLOCK, s_len)
    nq, nk = s_len // bq, s_len // bk

    def body(q_ref, k_ref, v_ref, cc_ref, cr_ref, o_ref, lse_ref, qm_ref, m_ref, l_ref, acc_ref):
        qi, kj = pl.program_id(1), pl.program_id(2)
        in0 = _first_head_lanes()

        @pl.when(kj == 0)
        def _():
            q2 = q_ref[...]
            qm_ref[0] = _zero_other(in0, q2)
            qm_ref[1] = _zero_other(jnp.logical_not(in0), q2)
            m_ref[...] = jnp.full_like(m_ref, NEG)
            l_ref[...] = jnp.zeros_like(l_ref)
            acc_ref[...] = jnp.zeros_like(acc_ref)

        @pl.when(kj <= qi)
        def _():
            k2, v2 = k_ref[...], v_ref[...]
            causal = _causal_mask(qi, kj, bq, bk)
            for a in (0, 1):
                s = _fox_scores(qm_ref[a], k2, cc_ref, cr_ref, a, causal)
                m_old = m_ref[a]
                m_new = jnp.maximum(m_old, jnp.max(s, axis=1, keepdims=True))
                alpha = jnp.exp(m_old - m_new)
                p = jnp.exp(s - m_new)
                l_ref[a] = alpha * l_ref[a] + jnp.sum(p, axis=1, keepdims=True)
                m_ref[a] = m_new
                pv = _dot(p.astype(BF16), v2)
                mine = in0 if a == 0 else jnp.logical_not(in0)
                acc_ref[...] = jnp.where(mine, alpha * acc_ref[...] + pv, acc_ref[...])

        @pl.when(kj == qi)
        def _():
            l0, l1 = l_ref[0], l_ref[1]
            o_ref[...] = acc_ref[...] * _pick(in0, 1.0 / l0, 1.0 / l1)
            lse_ref[...] = _pick(in0, m_ref[0] + jnp.log(l0), m_ref[1] + jnp.log(l1))

    blk = lambda rows: (rows, PAIR)
    return pl.pallas_call(
        body, name="fox_fwd", grid=(N_PAIRS, nq, nk),
        in_specs=[
            pl.BlockSpec(blk(bq), lambda p, i, j: (i, p)),
            pl.BlockSpec(blk(bk), lambda p, i, j: (jnp.minimum(j, i), N_PAIRS + p)),
            pl.BlockSpec(blk(bk), lambda p, i, j: (jnp.minimum(j, i), 2 * N_PAIRS + p)),
            pl.BlockSpec(blk(bq), lambda p, i, j: (i, p)),
            pl.BlockSpec((None, 2, bk), lambda p, i, j: (p, 0, jnp.minimum(j, i))),
        ],
        out_specs=[pl.BlockSpec(blk(bq), lambda p, i, j: (i, p)), pl.BlockSpec(blk(bq), lambda p, i, j: (i, p))],
        out_shape=[_sds((s_len, HEAD_WIDTH), F32), _sds((s_len, HEAD_WIDTH), F32)],
        scratch_shapes=[pltpu.VMEM((2, bq, PAIR), BF16), pltpu.VMEM((2, bq, 1), F32), pltpu.VMEM((2, bq, 1), F32),
                        pltpu.VMEM((bq, PAIR), F32)],
        compiler_params=pltpu.CompilerParams(dimension_semantics=("arbitrary",) * 3, vmem_limit_bytes=V7X_VMEM_LIMIT_BYTES),
    )(fqkv, fqkv, fqkv, c_col, c_row)


def _t5_bucket(dist):
    max_exact = N_BUCKETS // 2
    d = np.maximum(dist, 1).astype(np.float32)
    large = max_exact + (np.log(d / max_exact) / np.log(MAX_DISTANCE / max_exact) * (N_BUCKETS - max_exact)).astype(np.int32)
    large = np.minimum(large, N_BUCKETS - 1)
    return np.where(dist < max_exact, dist, large).astype(np.int32)


def _dil_buckets():
    w = DIL_BLOCK
    qi = np.arange(w)[:, None]
    kj = np.arange(2 * w)[None, :]
    sub = qi + w - kj
    band = (sub >= 0) & (sub <= w)
    out = [np.where(band, _t5_bucket(np.clip(sub, 0, w) * dil), -1) for dil in DILATIONS]
    return np.stack(out).astype(np.int32)


def _dil_bias(rel_bias, buckets):
    w = DIL_BLOCK

    def body(rb_ref, bk_ref, o_ref):
        for p in range(len(DILATIONS)):
            bk = bk_ref[p]
            for h in range(N_HEADS):
                def add(b, acc):
                    return acc + jnp.where(bk == b, rb_ref[b, h], 0.0)
                acc = lax.fori_loop(0, N_BUCKETS, add, jnp.zeros((w, 2 * w), F32))
                o_ref[p, h] = jnp.where(bk < 0, NEG, acc)

    return pl.pallas_call(
        body, name="dil_bias",
        in_specs=[pl.BlockSpec(memory_space=pltpu.SMEM), pl.BlockSpec(memory_space=pltpu.VMEM)],
        out_specs=pl.BlockSpec(memory_space=pltpu.VMEM),
        out_shape=_sds((len(DILATIONS), N_HEADS, w, 2 * w), F32),
    )(rel_bias, buckets)


def _dil_fwd(dqkv, bias, branch):
    dil = DILATIONS[branch]
    s_len = dqkv.shape[0]
    w = DIL_BLOCK
    hw = HEAD_WIDTH
    length = s_len // dil
    nb = length // w
    view = dqkv.reshape(length, dil * 3 * hw)

    def body(q_ref, kc_ref, kp_ref, vc_ref, vp_ref, b_ref, o_ref, lse_ref):
        n = pl.program_id(1)
        in0 = _first_head_lanes()
        not0 = jnp.logical_not(in0)
        prev_half = lax.broadcasted_iota(jnp.int32, (w, 2 * w), 1) < w
        no_prev = jnp.logical_and(n == 0, prev_half)
        for pr in range(N_PAIRS):
            sl = slice(pr * PAIR, (pr + 1) * PAIR)
            q2, kc2, kp2, vc2, vp2 = q_ref[:, sl], kc_ref[:, sl], kp_ref[:, sl], vc_ref[:, sl], vp_ref[:, sl]
            res = []
            for a, mine in ((0, in0), (1, not0)):
                qa = _zero_other(mine, q2)
                s = jnp.concatenate([_dot_nt(qa, kp2), _dot_nt(qa, kc2)], axis=1) + b_ref[2 * pr + a]
                s = jnp.where(no_prev, NEG, s)
                m = jnp.max(s, axis=1, keepdims=True)
                e = jnp.exp(s - m)
                l = jnp.sum(e, axis=1, keepdims=True)
                p = (e / l).astype(BF16)
                res.append((_dot(p[:, :w], vp2) + _dot(p[:, w:], vc2), m + jnp.log(l)))
            o_ref[:, sl] = _pick(in0, res[0][0], res[1][0])
            lse_ref[:, sl] = _pick(in0, res[0][1], res[1][1])

    prev = lambda n: jnp.maximum(n - 1, 0)
    out = pl.pallas_call(
        body, name=f"dil_fwd_{dil}", grid=(dil, nb),
        in_specs=[
            pl.BlockSpec((w, hw), lambda r, n: (n, 3 * r)),
            pl.BlockSpec((w, hw), lambda r, n: (n, 3 * r + 1)),
            pl.BlockSpec((w, hw), lambda r, n: (prev(n), 3 * r + 1)),
            pl.BlockSpec((w, hw), lambda r, n: (n, 3 * r + 2)),
            pl.BlockSpec((w, hw), lambda r, n: (prev(n), 3 * r + 2)),
            pl.BlockSpec((None, N_HEADS, w, 2 * w), lambda r, n: (branch, 0, 0, 0)),
        ],
        out_specs=[pl.BlockSpec((w, hw), lambda r, n: (n, r)), pl.BlockSpec((w, hw), lambda r, n: (n, r))],
        out_shape=[_sds((length, dil * hw), F32), _sds((length, dil * hw), F32)],
        compiler_params=pltpu.CompilerParams(dimension_semantics=("arbitrary",) * 2, vmem_limit_bytes=V7X_VMEM_LIMIT_BYTES),
    )(view, view, view, view, view, bias)
    return out[0].reshape(s_len, hw), out[1].reshape(s_len, hw)


def _mix_out(o_fox, o_br, lse_br, w_out, x, g_post):
    s_len, d = x.shape
    hw = HEAD_WIDTH
    tm = ROW_TILE

    def body(of_ref, o1, o2, o3, l1, l2, l3, w_ref, x_ref, g_ref, x1_ref, y1_ref, od_ref, lj_ref):
        la, lb, lc = l1[...], l2[...], l3[...]
        m = jnp.maximum(jnp.maximum(la, lb), lc)
        ea, eb, ec = jnp.exp(la - m), jnp.exp(lb - m), jnp.exp(lc - m)
        tot = ea + eb + ec
        o_dil = (ea / tot) * o1[...] + (eb / tot) * o2[...] + (ec / tot) * o3[...]
        od_ref[...] = o_dil
        lj_ref[...] = m + jnp.log(tot)
        y = _dot(of_ref[...].astype(BF16), w_ref[0:hw, :]) + _dot(o_dil.astype(BF16), w_ref[hw:2 * hw, :])
        y1_ref[...] = y
        x1_ref[...] = x_ref[...] + _rms_fwd(y, g_ref[...])

    half = _rows(tm, hw)
    return _call(
        body, name="mix_out", grid=(s_len // tm,),
        in_specs=[half] * 7 + [_resident(w_out.shape), _rows(tm, d), _resident((1, d))],
        out_specs=[_rows(tm, d), _rows(tm, d), half, half],
        out_shape=[_sds((s_len, d), F32), _sds((s_len, d), F32), _sds((s_len, hw), F32), _sds((s_len, hw), F32)],
    )(o_fox, *o_br, *lse_br, w_out, x, g_post)


def _mem_fwd(mem, g_mem, w_xk, w_xv):
    n_mem, d = mem.shape
    mw = w_xk.shape[1]

    def body(mem_ref, g_ref, wk_ref, wv_ref, hm_ref, k_ref, v_ref):
        hm = _rms_fwd(mem_ref[...], g_ref[...]).astype(BF16)
        hm_ref[...] = hm
        k_ref[...] = _dot(hm, wk_ref[...]).astype(BF16)
        v_ref[...] = _dot(hm, wv_ref[...]).astype(BF16)

    return pl.pallas_call(
        body, name="mem_fwd",
        out_shape=[_sds((n_mem, d), BF16), _sds((n_mem, mw), BF16), _sds((n_mem, mw), BF16)],
    )(mem, g_mem, w_xk, w_xv)


def _xattn_softmax(qa, k2):
    s = _dot_nt(qa, k2)
    m = jnp.max(s, axis=1, keepdims=True)
    e = jnp.exp(s - m)
    return e / jnp.sum(e, axis=1, keepdims=True)


def _xattn_fwd(x1, g_pre, w_xq, kx, vx, w_xo, g_post):
    s_len, d = x1.shape
    mw = w_xq.shape[1]
    n_mem = kx.shape[0]
    tm = ROW_TILE

    def body(x_ref, gp_ref, wq_ref, k_ref, v_ref, wo_ref, go_ref, x2_ref, y2_ref, h2_ref, q_ref, o_ref):
        x = x_ref[...]
        h = _rms_fwd(x, gp_ref[...]).astype(BF16)
        h2_ref[...] = h
        q = (_dot(h, wq_ref[...]) * QK_SCALE).astype(BF16)
        q_ref[...] = q
        in0 = _first_head_lanes()
        not0 = jnp.logical_not(in0)
        for pr in range(mw // PAIR):
            sl = slice(pr * PAIR, (pr + 1) * PAIR)
            q2, k2, v2 = q[:, sl], k_ref[:, sl], v_ref[:, sl]
            oa = [_dot(_xattn_softmax(_zero_other(mine, q2), k2).astype(BF16), v2) for mine in (in0, not0)]
            o_ref[:, sl] = _pick(in0, oa[0], oa[1]).astype(BF16)
        y = _dot(o_ref[...], wo_ref[...])
        y2_ref[...] = y
        x2_ref[...] = x + _rms_fwd(y, go_ref[...])

    return _call(
        body, name="xattn_fwd", grid=(s_len // tm,),
        in_specs=[_rows(tm, d), _resident((1, d)), _resident(w_xq.shape), _resident((n_mem, mw)), _resident((n_mem, mw)),
                  _resident(w_xo.shape), _resident((1, d))],
        out_specs=[_rows(tm, d), _rows(tm, d), _rows(tm, d), _rows(tm, mw), _rows(tm, mw)],
        out_shape=[_sds((s_len, d), F32), _sds((s_len, d), F32), _sds((s_len, d), BF16), _sds((s_len, mw), BF16),
                   _sds((s_len, mw), BF16)],
    )(x1, g_pre, w_xq, kx, vx, w_xo, g_post)


def _ffn_up(x2, g_pre, w_gate, w_up):
    s_len, d = x2.shape
    dff = w_gate.shape[1]
    tm = ROW_TILE

    def body(x_ref, g_ref, wg_ref, wu_ref, h_ref, a_ref, u_ref, z_ref):
        h = _rms_fwd(x_ref[...], g_ref[...]).astype(BF16)
        h_ref[...] = h
        a = _dot(h, wg_ref[...])
        u = _dot(h, wu_ref[...])
        a_ref[...] = a.astype(BF16)
        u_ref[...] = u.astype(BF16)
        z_ref[...] = ((a * jax.nn.sigmoid(a)) * u).astype(BF16)

    return _call(
        body, name="ffn_up", grid=(s_len // tm,),
        in_specs=[_rows(tm, d), _resident((1, d)), _resident(w_gate.shape), _resident(w_up.shape)],
        out_specs=[_rows(tm, d), _rows(tm, dff), _rows(tm, dff), _rows(tm, dff)],
        out_shape=[_sds((s_len, d), BF16)] + [_sds((s_len, dff), BF16)] * 3,
    )(x2, g_pre, w_gate, w_up)


def _ffn_down_loss(z, w_down, x2, g_post, target):
    s_len, d = x2.shape
    dff = z.shape[1]
    tm = ROW_TILE

    def body(z_ref, w_ref, x_ref, g_ref, t_ref, y_ref, dx_ref, sq_ref):
        @pl.when(pl.program_id(0) == 0)
        def _():
            sq_ref[...] = jnp.zeros_like(sq_ref)

        y = _dot(z_ref[...], w_ref[...])
        y_ref[...] = y
        err = (x_ref[...] + _rms_fwd(y, g_ref[...])) - t_ref[...]
        sq_ref[...] += jnp.sum(err * err, axis=0, keepdims=True)
        dx_ref[...] = err * (1.0 / d)

    return _call(
        body, name="ffn_down_loss", grid=(s_len // tm,),
        in_specs=[_rows(tm, dff), _resident(w_down.shape), _rows(tm, d), _resident((1, d)), _rows(tm, d)],
        out_specs=[_rows(tm, d), _rows(tm, d), _acc_out((1, d))],
        out_shape=[_sds((s_len, d), F32), _sds((s_len, d), F32), _sds((1, d), F32)],
    )(z, w_down, x2, g_post, target)


def _weight_grad(a, b, name):
    s_len, k = a.shape
    n = b.shape[1]
    ts = 512 if s_len % 512 == 0 else s_len
    tn = n
    while k * tn * 4 > 8 * 2 ** 20 and tn % 256 == 0:
        tn //= 2

    def body(a_ref, b_ref, o_ref):
        @pl.when(pl.program_id(1) == 0)
        def _():
            o_ref[...] = jnp.zeros_like(o_ref)

        o_ref[...] += _dot_tn(a_ref[...].astype(BF16), b_ref[...].astype(BF16))

    return pl.pallas_call(
        body, name=name, grid=(n // tn, s_len // ts),
        in_specs=[pl.BlockSpec((ts, k), lambda j, i: (i, 0)), pl.BlockSpec((ts, tn), lambda j, i: (i, j))],
        out_specs=pl.BlockSpec((k, tn), lambda j, i: (0, j)),
        out_shape=_sds((k, n), F32),
        compiler_params=pltpu.CompilerParams(dimension_semantics=("arbitrary",) * 2, vmem_limit_bytes=V7X_VMEM_LIMIT_BYTES),
    )(a, b)


def _ffn_bwd_a(dx3, y3, g_post, w_down_t, a, u):
    s_len, d = dx3.shape
    dff = a.shape[1]
    tm = ROW_TILE

    def body(dx_ref, y_ref, g_ref, w_ref, a_ref, u_ref, dy_ref, da_ref, du_ref, dg_ref):
        @pl.when(pl.program_id(0) == 0)
        def _():
            dg_ref[...] = jnp.zeros_like(dg_ref)

        dy, dg = _rms_bwd(y_ref[...], g_ref[...], dx_ref[...])
        dg_ref[...] += dg
        dyb = dy.astype(BF16)
        dy_ref[...] = dyb
        dz = _dot(dyb, w_ref[...])
        av = a_ref[...].astype(F32)
        uv = u_ref[...].astype(F32)
        sg = jax.nn.sigmoid(av)
        da_ref[...] = (dz * uv * (sg * (1.0 + av * (1.0 - sg)))).astype(BF16)
        du_ref[...] = (dz * (av * sg)).astype(BF16)

    return _call(
        body, name="ffn_bwd_a", grid=(s_len // tm,),
        in_specs=[_rows(tm, d), _rows(tm, d), _resident((1, d)), _resident(w_down_t.shape), _rows(tm, dff), _rows(tm, dff)],
        out_specs=[_rows(tm, d), _rows(tm, dff), _rows(tm, dff), _acc_out((1, d))],
        out_shape=[_sds((s_len, d), BF16), _sds((s_len, dff), BF16), _sds((s_len, dff), BF16), _sds((1, d), F32)],
    )(dx3, y3, g_post, w_down_t, a, u)


def _ffn_bwd_b(da, du, w_gate_t, w_up_t, dx3, x2, g_pre):
    s_len, d = x2.shape
    dff = da.shape[1]
    tm = ROW_TILE

    def body(da_ref, du_ref, wg_ref, wu_ref, dx_ref, x_ref, g_ref, o_ref, dg_ref):
        @pl.when(pl.program_id(0) == 0)
        def _():
            dg_ref[...] = jnp.zeros_like(dg_ref)

        dh = _dot(da_ref[...], wg_ref[...]) + _dot(du_ref[...], wu_ref[...])
        dx, dg = _rms_bwd(x_ref[...], g_ref[...], dh)
        dg_ref[...] += dg
        o_ref[...] = dx_ref[...] + dx

    return _call(
        body, name="ffn_bwd_b", grid=(s_len // tm,),
        in_specs=[_rows(tm, dff), _rows(tm, dff), _resident(w_gate_t.shape), _resident(w_up_t.shape), _rows(tm, d),
                  _rows(tm, d), _resident((1, d))],
        out_specs=[_rows(tm, d), _acc_out((1, d))],
        out_shape=[_sds((s_len, d), F32), _sds((1, d), F32)],
    )(da, du, w_gate_t, w_up_t, dx3, x2, g_pre)


def _xattn_bwd(dx2, y2, g_post, w_xo_t, q, kx, vx, w_xq_t, x1, g_pre):
    s_len, d = x1.shape
    mw = q.shape[1]
    n_mem = kx.shape[0]
    tm = ROW_TILE

    def body(dx_ref, y_ref, go_ref, wo_ref, q_ref, k_ref, v_ref, wq_ref, x_ref, gp_ref,
             dx1_ref, dy_ref, dq_ref, dk_ref, dv_ref, dgo_ref, dgp_ref):
        @pl.when(pl.program_id(0) == 0)
        def _():
            dk_ref[...] = jnp.zeros_like(dk_ref)
            dv_ref[...] = jnp.zeros_like(dv_ref)
            dgo_ref[...] = jnp.zeros_like(dgo_ref)
            dgp_ref[...] = jnp.zeros_like(dgp_ref)

        dxin = dx_ref[...]
        dy, dgo = _rms_bwd(y_ref[...], go_ref[...], dxin)
        dgo_ref[...] += dgo
        dyb = dy.astype(BF16)
        dy_ref[...] = dyb
        do = _dot(dyb, wo_ref[...]).astype(BF16)
        in0 = _first_head_lanes()
        not0 = jnp.logical_not(in0)
        for pr in range(mw // PAIR):
            sl = slice(pr * PAIR, (pr + 1) * PAIR)
            q2, k2, v2, do2 = q_ref[:, sl], k_ref[:, sl], v_ref[:, sl], do[:, sl]
            dqs = []
            dk2 = jnp.zeros((n_mem, PAIR), F32)
            dv2 = jnp.zeros((n_mem, PAIR), F32)
            for mine in (in0, not0):
                qa = _zero_other(mine, q2)
                doa = _zero_other(mine, do2)
                p = _xattn_softmax(qa, k2)
                dp = _dot_nt(doa, v2)
                ds = (p * (dp - jnp.sum(p * dp, axis=1, keepdims=True))).astype(BF16)
                dqs.append(_dot(ds, k2))
                dk2 = dk2 + _dot_tn(ds, qa)
                dv2 = dv2 + _dot_tn(p.astype(BF16), doa)
            dq_ref[:, sl] = (_pick(in0, dqs[0], dqs[1]) * QK_SCALE).astype(BF16)
            dk_ref[:, sl] += dk2
            dv_ref[:, sl] += dv2
        dh = _dot(dq_ref[...], wq_ref[...])
        dx, dgp = _rms_bwd(x_ref[...], gp_ref[...], dh)
        dgp_ref[...] += dgp
        dx1_ref[...] = dxin + dx

    return _call(
        body, name="xattn_bwd", grid=(s_len // tm,),
        in_specs=[_rows(tm, d), _rows(tm, d), _resident((1, d)), _resident(w_xo_t.shape), _rows(tm, mw),
                  _resident((n_mem, mw)), _resident((n_mem, mw)), _resident(w_xq_t.shape), _rows(tm, d), _resident((1, d))],
        out_specs=[_rows(tm, d), _rows(tm, d), _rows(tm, mw), _acc_out((n_mem, mw)), _acc_out((n_mem, mw)),
                   _acc_out((1, d)), _acc_out((1, d))],
        out_shape=[_sds((s_len, d), F32), _sds((s_len, d), BF16), _sds((s_len, mw), BF16), _sds((n_mem, mw), F32),
                   _sds((n_mem, mw), F32), _sds((1, d), F32), _sds((1, d), F32)],
    )(dx2, y2, g_post, w_xo_t, q, kx, vx, w_xq_t, x1, g_pre)


def _mem_bwd(dk, dv, w_xk_t, w_xv_t, hm, mem, g_mem):
    n_mem, d = mem.shape
    mw = dk.shape[1]

    def body(dk_ref, dv_ref, wk_ref, wv_ref, hm_ref, mem_ref, g_ref, dwk_ref, dwv_ref, dg_ref):
        dkb = dk_ref[...].astype(BF16)
        dvb = dv_ref[...].astype(BF16)
        dhm = _dot(dkb, wk_ref[...]) + _dot(dvb, wv_ref[...])
        _, dg = _rms_bwd(mem_ref[...], g_ref[...], dhm)
        dg_ref[...] = dg
        dwk_ref[...] = _dot_tn(hm_ref[...], dkb)
        dwv_ref[...] = _dot_tn(hm_ref[...], dvb)

    return pl.pallas_call(
        body, name="mem_bwd",
        out_shape=[_sds((d, mw), F32), _sds((d, mw), F32), _sds((1, d), F32)],
    )(dk, dv, w_xk_t, w_xv_t, hm, mem, g_mem)


def _mix_out_bwd(dx1, y1, g_post, w_out_t, o_fox, o_dil):
    s_len, d = dx1.shape
    hw = HEAD_WIDTH
    tm = ROW_TILE
    head_of = np.arange(hw) // HEAD_DIM
    ones = jnp.asarray((head_of[:, None] == head_of[None, :]).astype(np.float32))

    def body(dx_ref, y_ref, g_ref, w_ref, of_ref, od_ref, ones_ref, dy_ref, do_ref, dl_ref, dg_ref):
        @pl.when(pl.program_id(0) == 0)
        def _():
            dg_ref[...] = jnp.zeros_like(dg_ref)

        dy, dg = _rms_bwd(y_ref[...], g_ref[...], dx_ref[...])
        dg_ref[...] += dg
        dyb = dy.astype(BF16)
        dy_ref[...] = dyb
        do = _dot(dyb, w_ref[...])
        do_ref[...] = do.astype(BF16)
        dl_ref[:, 0:hw] = jnp.dot(do[:, 0:hw] * of_ref[...], ones_ref[...], precision=HIGHEST, preferred_element_type=F32)
        dl_ref[:, hw:2 * hw] = jnp.dot(do[:, hw:2 * hw] * od_ref[...], ones_ref[...], precision=HIGHEST,
                                       preferred_element_type=F32)

    return _call(
        body, name="mix_out_bwd", grid=(s_len // tm,),
        in_specs=[_rows(tm, d), _rows(tm, d), _resident((1, d)), _resident(w_out_t.shape), _rows(tm, hw), _rows(tm, hw),
                  _resident((hw, hw))],
        out_specs=[_rows(tm, d), _rows(tm, 2 * hw), _rows(tm, 2 * hw), _acc_out((1, d))],
        out_shape=[_sds((s_len, d), BF16), _sds((s_len, 2 * hw), BF16), _sds((s_len, 2 * hw), F32), _sds((1, d), F32)],
    )(dx1, y1, g_post, w_out_t, o_fox, o_dil, ones)


def _fox_bwd_dq(fqkv, do, lse, delta, c_col, c_row):
    s_len = fqkv.shape[0]
    bq = bk = min(ATT_BLOCK, s_len)
    nq, nk = s_len // bq, s_len // bk

    def body(q_ref, k_ref, v_ref, do_ref, lse_ref, dl_ref, cc_ref, cr_ref, dq_ref, rs_ref, qm_ref, dom_ref, acc_ref):
        qi, kj = pl.program_id(1), pl.program_id(2)
        in0 = _first_head_lanes()
        not0 = jnp.logical_not(in0)

        @pl.when(kj == 0)
        def _():
            q2, do2 = q_ref[...], do_ref[...]
            qm_ref[0] = _zero_other(in0, q2)
            qm_ref[1] = _zero_other(not0, q2)
            dom_ref[0] = _zero_other(in0, do2)
            dom_ref[1] = _zero_other(not0, do2)
            acc_ref[...] = jnp.zeros_like(acc_ref)

        @pl.when(kj <= qi)
        def _():
            k2, v2 = k_ref[...], v_ref[...]
            causal = _causal_mask(qi, kj, bq, bk)
            for a, mine in ((0, in0), (1, not0)):
                col = slice(a * HEAD_DIM, a * HEAD_DIM + 1)
                s = _fox_scores(qm_ref[a], k2, cc_ref, cr_ref, a, causal)
                p = jnp.exp(s - lse_ref[:, col])
                dp = _dot_nt(dom_ref[a], v2)
                ds = (p * (dp - dl_ref[:, col])).astype(BF16)
                acc_ref[a] += _dot(ds, jnp.where(mine, k2, jnp.ones_like(k2)))

        @pl.when(kj == qi)
        def _():
            dq_ref[...] = (_pick(in0, acc_ref[0], acc_ref[1]) * QK_SCALE).astype(BF16)
            rs_ref[...] = pltpu.roll(_pick(in0, acc_ref[1], acc_ref[0]), HEAD_DIM, 1)

    blk = lambda rows: (rows, PAIR)
    qmap = lambda p, i, j: (i, p)
    return pl.pallas_call(
        body, name="fox_bwd_dq", grid=(N_PAIRS, nq, nk),
        in_specs=[
            pl.BlockSpec(blk(bq), qmap),
            pl.BlockSpec(blk(bk), lambda p, i, j: (jnp.minimum(j, i), N_PAIRS + p)),
            pl.BlockSpec(blk(bk), lambda p, i, j: (jnp.minimum(j, i), 2 * N_PAIRS + p)),
            pl.BlockSpec(blk(bq), qmap), pl.BlockSpec(blk(bq), qmap), pl.BlockSpec(blk(bq), qmap), pl.BlockSpec(blk(bq), qmap),
            pl.BlockSpec((None, 2, bk), lambda p, i, j: (p, 0, jnp.minimum(j, i))),
        ],
        out_specs=[pl.BlockSpec(blk(bq), qmap), pl.BlockSpec(blk(bq), qmap)],
        out_shape=[_sds((s_len, HEAD_WIDTH), BF16), _sds((s_len, HEAD_WIDTH), F32)],
        scratch_shapes=[pltpu.VMEM((2, bq, PAIR), BF16), pltpu.VMEM((2, bq, PAIR), BF16), pltpu.VMEM((2, bq, PAIR), F32)],
        compiler_params=pltpu.CompilerParams(dimension_semantics=("arbitrary",) * 3, vmem_limit_bytes=V7X_VMEM_LIMIT_BYTES),
    )(fqkv, fqkv, fqkv, do, lse, delta, c_col, c_row)


def _fox_bwd_dkv(fqkv, do, lse, delta, c_col, c_row):
    s_len = fqkv.shape[0]
    bq = bk = min(ATT_BLOCK, s_len)
    nq, nk = s_len // bq, s_len // bk

    def body(q_ref, k_ref, v_ref, do_ref, lse_ref, dl_ref, cc_ref, cr_ref, dk_ref, dv_ref, dc_ref, r_ref, dvacc_ref):
        kj, qi = pl.program_id(1), pl.program_id(2)
        in0 = _first_head_lanes()
        not0 = jnp.logical_not(in0)

        @pl.when(qi == 0)
        def _():
            r_ref[...] = jnp.zeros_like(r_ref)
            dvacc_ref[...] = jnp.zeros_like(dvacc_ref)

        @pl.when(qi >= kj)
        def _():
            q2, do2, k2, v2 = q_ref[...], do_ref[...], k_ref[...], v_ref[...]
            causal = _causal_mask(qi, kj, bq, bk)
            for a, mine in ((0, in0), (1, not0)):
                col = slice(a * HEAD_DIM, a * HEAD_DIM + 1)
                doa = _zero_other(mine, do2)
                s = _fox_scores(_zero_other(mine, q2), k2, cc_ref, cr_ref, a, causal)
                p = jnp.exp(s - lse_ref[:, col])
                dp = _dot_nt(doa, v2)
                ds = (p * (dp - dl_ref[:, col])).astype(BF16)
                dvacc_ref[...] += _dot_tn(p.astype(BF16), doa)
                r_ref[a] += _dot_tn(ds, jnp.where(mine, q2, jnp.ones_like(q2)))

        @pl.when(qi == nq - 1)
        def _():
            dk_ref[...] = _pick(in0, r_ref[0], r_ref[1]).astype(BF16)
            dv_ref[...] = dvacc_ref[...].astype(BF16)
            dc_ref[...] = -pltpu.roll(_pick(in0, r_ref[1], r_ref[0]), HEAD_DIM, 1)

    blk = lambda rows: (rows, PAIR)
    qmap = lambda p, j, i: (jnp.maximum(i, j), p)
    kvout = lambda p, j, i: (j, p)
    return pl.pallas_call(
        body, name="fox_bwd_dkv", grid=(N_PAIRS, nk, nq),
        in_specs=[
            pl.BlockSpec(blk(bq), qmap),
            pl.BlockSpec(blk(bk), lambda p, j, i: (j, N_PAIRS + p)),
            pl.BlockSpec(blk(bk), lambda p, j, i: (j, 2 * N_PAIRS + p)),
            pl.BlockSpec(blk(bq), qmap), pl.BlockSpec(blk(bq), qmap), pl.BlockSpec(blk(bq), qmap), pl.BlockSpec(blk(bq), qmap),
            pl.BlockSpec((None, 2, bk), lambda p, j, i: (p, 0, j)),
        ],
        out_specs=[pl.BlockSpec(blk(bk), kvout)] * 3,
        out_shape=[_sds((s_len, HEAD_WIDTH), BF16), _sds((s_len, HEAD_WIDTH), BF16), _sds((s_len, HEAD_WIDTH), F32)],
        scratch_shapes=[pltpu.VMEM((2, bk, PAIR), F32), pltpu.VMEM((bk, PAIR), F32)],
        compiler_params=pltpu.CompilerParams(dimension_semantics=("arbitrary",) * 3, vmem_limit_bytes=V7X_VMEM_LIMIT_BYTES),
    )(fqkv, fqkv, fqkv, do, lse, delta, c_col, c_row)


def _gate_bwd(rs, dc, gx, b_exp):
    s_len, hw = gx.shape
    t = min(SCAN_TILE, s_len)
    nt = s_len // t
    tri = jnp.asarray(np.triu(np.ones((t, t), np.float32)))

    def body(rs_ref, dc_ref, gx_ref, b_ref, tri_ref, dgx_ref, db_ref, carry):
        @pl.when(pl.program_id(0) == 0)
        def _():
            carry[...] = jnp.zeros_like(carry)
            db_ref[...] = jnp.zeros_like(db_ref)

        dlf = jnp.dot(tri_ref[...], rs_ref[...] + dc_ref[...], precision=HIGHEST, preferred_element_type=F32) + carry[...]
        carry[...] = dlf[0:1, :]
        dgate = dlf * jax.nn.sigmoid(-(gx_ref[...] + b_ref[...]))
        db_ref[...] += jnp.sum(dgate, axis=0, keepdims=True)
        lane = lax.broadcasted_iota(jnp.int32, (1, hw), 1)
        dgx_ref[...] = jnp.where(lane % HEAD_DIM == 0, dgate, 0.0).astype(BF16)

    rev = lambda i: (nt - 1 - i, 0)
    return _call(
        body, name="gate_bwd", grid=(nt,),
        in_specs=[pl.BlockSpec((t, hw), rev)] * 3 + [_resident((1, hw)), _resident((t, t))],
        out_specs=[pl.BlockSpec((t, hw), rev), _acc_out((1, hw))],
        out_shape=[_sds((s_len, hw), BF16), _sds((1, hw), F32)],
        scratch=[pltpu.VMEM((1, hw), F32)],
    )(rs, dc, gx, b_exp, tri)


def _dil_bwd(dqkv, do, lj, delta, bias, branch):
    dil = DILATIONS[branch]
    s_len = dqkv.shape[0]
    w = DIL_BLOCK
    hw = HEAD_WIDTH
    length = s_len // dil
    nb = length // w
    qkv_v = dqkv.reshape(length, dil * 3 * hw)
    do_v = do.reshape(length, dil * 2 * hw)
    dl_v = delta.reshape(length, dil * 2 * hw)
    lj_v = lj.reshape(length, dil * hw)

    def body(q0_ref, q1_ref, kp_ref, kc_ref, vp_ref, vc_ref, do0_ref, do1_ref, l0_ref, l1_ref, d0_ref, d1_ref, b_ref,
             dq_ref, dk_ref, dv_ref, dsum_ref):
        r, n = pl.program_id(0), pl.program_id(1)
        in0 = _first_head_lanes()
        not0 = jnp.logical_not(in0)
        first = n == 0
        last = n == nb - 1

        @pl.when(jnp.logical_and(r == 0, n == 0))
        def _():
            dsum_ref[...] = jnp.zeros_like(dsum_ref)

        for pr in range(N_PAIRS):
            sl = slice(pr * PAIR, (pr + 1) * PAIR)
            q0, q1 = q0_ref[:, sl], q1_ref[:, sl]
            kp2, kc2, vp2, vc2 = kp_ref[:, sl], kc_ref[:, sl], vp_ref[:, sl], vc_ref[:, sl]
            do0, do1 = do0_ref[:, sl], do1_ref[:, sl]
            dqs = []
            dk2 = jnp.zeros((w, PAIR), F32)
            dv2 = jnp.zeros((w, PAIR), F32)
            for a, mine in ((0, in0), (1, not0)):
                h = 2 * pr + a
                col = slice(pr * PAIR + a * HEAD_DIM, pr * PAIR + a * HEAD_DIM + 1)
                qa0, qa1 = _zero_other(mine, q0), _zero_other(mine, q1)
                doa0, doa1 = _zero_other(mine, do0), _zero_other(mine, do1)
                b_prev, b_cur = b_ref[h, :, 0:w], b_ref[h, :, w:2 * w]
                s_a = jnp.where(first, NEG, _dot_nt(qa0, kp2) + b_prev)
                s_b = _dot_nt(qa0, kc2) + b_cur
                s_c = jnp.where(last, NEG, _dot_nt(qa1, kc2) + b_prev)
                p_a = jnp.exp(s_a - l0_ref[:, col])
                p_b = jnp.exp(s_b - l0_ref[:, col])
                p_c = jnp.exp(s_c - l1_ref[:, col])
                ds_a = p_a * (_dot_nt(doa0, vp2) - d0_ref[:, col])
                ds_b = p_b * (_dot_nt(doa0, vc2) - d0_ref[:, col])
                ds_c = p_c * (_dot_nt(doa1, vc2) - d1_ref[:, col])
                dsum_ref[h, :, 0:w] += ds_a
                dsum_ref[h, :, w:2 * w] += ds_b
                ds_a, ds_b, ds_c = ds_a.astype(BF16), ds_b.astype(BF16), ds_c.astype(BF16)
                dqs.append(_dot(ds_a, kp2) + _dot(ds_b, kc2))
                dk2 = dk2 + _dot_tn(ds_b, qa0) + _dot_tn(ds_c, qa1)
                dv2 = dv2 + _dot_tn(p_b.astype(BF16), doa0) + _dot_tn(p_c.astype(BF16), doa1)
            dq_ref[:, sl] = _pick(in0, dqs[0], dqs[1])
            dk_ref[:, sl] = dk2
            dv_ref[:, sl] = dv2

    prev = lambda n: jnp.maximum(n - 1, 0)
    nxt = lambda n: jnp.minimum(n + 1, nb - 1)
    blk = (w, hw)
    outs = pl.pallas_call(
        body, name=f"dil_bwd_{dil}", grid=(dil, nb),
        in_specs=[
            pl.BlockSpec(blk, lambda r, n: (n, 3 * r)),
            pl.BlockSpec(blk, lambda r, n: (nxt(n), 3 * r)),
            pl.BlockSpec(blk, lambda r, n: (prev(n), 3 * r + 1)),
            pl.BlockSpec(blk, lambda r, n: (n, 3 * r + 1)),
            pl.BlockSpec(blk, lambda r, n: (prev(n), 3 * r + 2)),
            pl.BlockSpec(blk, lambda r, n: (n, 3 * r + 2)),
            pl.BlockSpec(blk, lambda r, n: (n, 2 * r + 1)),
            pl.BlockSpec(blk, lambda r, n: (nxt(n), 2 * r + 1)),
            pl.BlockSpec(blk, lambda r, n: (n, r)),
            pl.BlockSpec(blk, lambda r, n: (nxt(n), r)),
            pl.BlockSpec(blk, lambda r, n: (n, 2 * r + 1)),
            pl.BlockSpec(blk, lambda r, n: (nxt(n), 2 * r + 1)),
            pl.BlockSpec((None, N_HEADS, w, 2 * w), lambda r, n: (branch, 0, 0, 0)),
        ],
        out_specs=[pl.BlockSpec(blk, lambda r, n: (n, r))] * 3 + [pl.BlockSpec((N_HEADS, w, 2 * w), lambda r, n: (0, 0, 0))],
        out_shape=[_sds((length, dil * hw), F32)] * 3 + [_sds((N_HEADS, w, 2 * w), F32)],
        compiler_params=pltpu.CompilerParams(dimension_semantics=("arbitrary",) * 2, vmem_limit_bytes=V7X_VMEM_LIMIT_BYTES),
    )(qkv_v, qkv_v, qkv_v, qkv_v, qkv_v, qkv_v, do_v, do_v, lj_v, lj_v, dl_v, dl_v, bias)
    return [o.reshape(s_len, hw) for o in outs[:3]] + [outs[3]]


def _rel_bias_grad(dsum, buckets):
    w = DIL_BLOCK

    def body(ds_ref, bk_ref, o_ref):
        row = lax.broadcasted_iota(jnp.int32, (N_BUCKETS, PAIR), 0)
        lane = lax.broadcasted_iota(jnp.int32, (N_BUCKETS, PAIR), 1)

        def per_bucket(b, acc):
            for p in range(len(DILATIONS)):
                hit = bk_ref[p] == b
                for h in range(N_HEADS):
                    part = jnp.where(hit, ds_ref[p, h], 0.0)
                    tot = jnp.sum(jnp.sum(part, axis=1, keepdims=True), axis=0, keepdims=True)
                    acc = acc + jnp.where(jnp.logical_and(row == b, lane == h), tot, 0.0)
            return acc

        o_ref[...] = lax.fori_loop(0, N_BUCKETS, per_bucket, jnp.zeros((N_BUCKETS, PAIR), F32))

    return pl.pallas_call(body, name="rel_bias_grad", out_shape=_sds((N_BUCKETS, PAIR), F32))(dsum, buckets)


def _in_proj_bwd(dfq, dfk, dfv, dgx, ddq, ddk, ddv, w_in_t, dx1, x, g_pre):
    s_len, d = x.shape
    hw = HEAD_WIDTH
    tm = ROW_TILE

    def body(fq, fk, fv, gx, q1, q2, q3, k1, k2, k3, v1, v2, v3, w_ref, dx_ref, x_ref, g_ref, o_ref, dp_ref, dg_ref):
        @pl.when(pl.program_id(0) == 0)
        def _():
            dg_ref[...] = jnp.zeros_like(dg_ref)

        dp_ref[:, 0:hw] = fq[...]
        dp_ref[:, hw:2 * hw] = fk[...]
        dp_ref[:, 2 * hw:3 * hw] = fv[...]
        dp_ref[:, 3 * hw:4 * hw] = gx[...]
        dp_ref[:, 4 * hw:5 * hw] = (((q1[...] + q2[...]) + q3[...]) * QK_SCALE).astype(BF16)
        dp_ref[:, 5 * hw:6 * hw] = ((k1[...] + k2[...]) + k3[...]).astype(BF16)
        dp_ref[:, 6 * hw:7 * hw] = ((v1[...] + v2[...]) + v3[...]).astype(BF16)
        dh = _dot(dp_ref[...], w_ref[...])
        dx, dg = _rms_bwd(x_ref[...], g_ref[...], dh)
        dg_ref[...] += dg
        o_ref[...] = dx_ref[...] + dx

    half = _rows(tm, hw)
    return _call(
        body, name="in_proj_bwd", grid=(s_len // tm,),
        in_specs=[half] * 13 + [_resident(w_in_t.shape), _rows(tm, d), _rows(tm, d), _resident((1, d))],
        out_specs=[_rows(tm, d), _rows(tm, 7 * hw), _acc_out((1, d))],
        out_shape=[_sds((s_len, d), F32), _sds((s_len, 7 * hw), BF16), _sds((1, d), F32)],
    )(dfq, dfk, dfv, dgx, *ddq, *ddk, *ddv, w_in_t, dx1, x, g_pre)


def _expand_w_in(w_in):
    hw = HEAD_WIDTH
    gate = jnp.repeat(w_in[:, 3 * hw:3 * hw + N_HEADS], HEAD_DIM, axis=1)
    return jnp.concatenate([w_in[:, :3 * hw], gate, w_in[:, 3 * hw + N_HEADS:]], axis=1)


def _local_step(x, mem, target, g, w_bf):
    hw = HEAD_WIDTH
    w_in_e = _expand_w_in(w_bf["w_in"])
    b_exp = jnp.repeat(g["b_f"], HEAD_DIM, axis=1)
    buckets = jnp.asarray(_dil_buckets())

    h1, fqkv, gx, dqkv = _in_proj(x, g["g_mix_pre"], w_in_e)
    c_col, c_row8 = _gate_scan(gx, b_exp)
    c_row = c_row8.reshape(N_PAIRS, 2, -1)
    o_fox, lse_fox = _fox_fwd(fqkv, c_col, c_row)
    bias = _dil_bias(g["rel_bias"], buckets)
    branches = [_dil_fwd(dqkv, bias, p) for p in range(len(DILATIONS))]
    x1, y1, o_dil, lj = _mix_out(o_fox, [b[0] for b in branches], [b[1] for b in branches], w_bf["w_out"], x, g["g_mix_post"])
    hm, kx, vx = _mem_fwd(mem, g["g_mem"], w_bf["w_xk"], w_bf["w_xv"])
    x2, y2, h2, qx, ox = _xattn_fwd(x1, g["g_xattn_pre"], w_bf["w_xq"], kx, vx, w_bf["w_xo"], g["g_xattn_post"])
    h3, a, u, z = _ffn_up(x2, g["g_ffn_pre"], w_bf["w_gate"], w_bf["w_up"])
    y3, dx3, sq = _ffn_down_loss(z, w_bf["w_down"], x2, g["g_ffn_post"], target)

    grads = {}
    dy3, da, du, grads["g_ffn_post"] = _ffn_bwd_a(dx3, y3, g["g_ffn_post"], w_bf["w_down"].T, a, u)
    dx2, grads["g_ffn_pre"] = _ffn_bwd_b(da, du, w_bf["w_gate"].T, w_bf["w_up"].T, dx3, x2, g["g_ffn_pre"])
    grads["w_down"] = _weight_grad(z, dy3, "dw_down")
    grads["w_gate"] = _weight_grad(h3, da, "dw_gate")
    grads["w_up"] = _weight_grad(h3, du, "dw_up")
    dx1, dy2, dqx, dkx, dvx, grads["g_xattn_post"], grads["g_xattn_pre"] = _xattn_bwd(
        dx2, y2, g["g_xattn_post"], w_bf["w_xo"].T, qx, kx, vx, w_bf["w_xq"].T, x1, g["g_xattn_pre"])
    grads["w_xo"] = _weight_grad(ox, dy2, "dw_xo")
    grads["w_xq"] = _weight_grad(h2, dqx, "dw_xq")
    grads["w_xk"], grads["w_xv"], grads["g_mem"] = _mem_bwd(dkx, dvx, w_bf["w_xk"].T, w_bf["w_xv"].T, hm, mem, g["g_mem"])
    dy1, do, delta, grads["g_mix_post"] = _mix_out_bwd(dx1, y1, g["g_mix_post"], w_bf["w_out"].T, o_fox, o_dil)
    grads["w_out"] = jnp.concatenate([_weight_grad(o_fox, dy1, "dw_out_fox"), _weight_grad(o_dil, dy1, "dw_out_dil")], axis=0)
    dfq, rs = _fox_bwd_dq(fqkv, do, lse_fox, delta, c_col, c_row)
    dfk, dfv, dc = _fox_bwd_dkv(fqkv, do, lse_fox, delta, c_col, c_row)
    dgx, db = _gate_bwd(rs, dc, gx, b_exp)
    grads["b_f"] = db[:, ::HEAD_DIM]
    dil = [_dil_bwd(dqkv, do, lj, delta, bias, p) for p in range(len(DILATIONS))]
    grads["rel_bias"] = _rel_bias_grad(jnp.stack([t[3] for t in dil]), buckets)[:, :N_HEADS]
    grad_x, dproj, grads["g_mix_pre"] = _in_proj_bwd(
        dfq, dfk, dfv, dgx, [t[0] for t in dil], [t[1] for t in dil], [t[2] for t in dil], w_in_e.T, dx1, x, g["g_mix_pre"])
    dw_in_e = _weight_grad(h1, dproj, "dw_in")
    grads["w_in"] = jnp.concatenate(
        [dw_in_e[:, :3 * hw], dw_in_e[:, 3 * hw:4 * hw:HEAD_DIM], dw_in_e[:, 4 * hw:]], axis=1)
    return sq, grad_x, grads


MESH = pl.DeviceIdType.MESH


def _my_place():
    return lax.axis_index("x"), lax.axis_index("y"), lax.axis_index("c")


def _all_gather(x, name):
    rows, lanes = x.shape

    def body(x_ref, out_ref, send_sems, recv_sems, local_sem):
        mx, my, mc = _my_place()
        me, sibling = (mx, my, mc), (mx, my, 1 - mc)
        chips = [(1 - mx, my), (mx, 1 - my), (1 - mx, 1 - my)]

        def slot(px, py, pc):
            return out_ref.at[4 * px + 2 * py + pc]

        def copy(k, block, to, src=None):
            return pltpu.make_async_remote_copy(
                src_ref=slot(*block) if src is None else src, dst_ref=slot(*block),
                send_sem=send_sems.at[k], recv_sem=recv_sems.at[k], device_id=to, device_id_type=MESH)

        mine = pltpu.make_async_copy(x_ref, slot(*me), local_sem)
        mine.start()
        first = [copy(0, me, sibling, src=x_ref)]
        first += [copy(1 + j, me, (*chip, mc), src=x_ref) for j, chip in enumerate(chips)]
        for cp in first:
            cp.start()
        passed = [copy(4 + j, (*chip, mc), sibling) for j, chip in enumerate(chips)]
        for j, chip in enumerate(chips):
            copy(1 + j, (*chip, mc), me).wait_recv()
            passed[j].start()
        copy(0, sibling, me).wait_recv()
        for j, chip in enumerate(chips):
            copy(4 + j, (*chip, 1 - mc), me).wait_recv()
        for cp in first + passed:
            cp.wait_send()
        mine.wait()

    return pl.pallas_call(
        body, name=name, out_shape=_sds((N_DEV, rows, lanes), x.dtype),
        in_specs=[pl.BlockSpec(memory_space=pl.ANY)], out_specs=pl.BlockSpec(memory_space=pl.ANY),
        scratch_shapes=[pltpu.SemaphoreType.DMA((N_DEV - 1,)), pltpu.SemaphoreType.DMA((N_DEV - 1,)), pltpu.SemaphoreType.DMA],
    )(x)


def _exchange(g, name):
    _, rows, lanes = g.shape

    def body(g_ref, land_ref, send_sems, recv_sems, local_sem):
        mx, my, mc = _my_place()
        me = 4 * mx + 2 * my + mc
        mine = pltpu.make_async_copy(g_ref.at[me], land_ref.at[me], local_sem)
        mine.start()
        sent, arriving = [], []
        for k in (1, 2, 3, 4, 5, 6, 7):
            px = 1 - mx if k & 4 else mx
            py = 1 - my if k & 2 else my
            pc = 1 - mc if k & 1 else mc
            peer = 4 * px + 2 * py + pc
            cp = pltpu.make_async_remote_copy(
                src_ref=g_ref.at[peer], dst_ref=land_ref.at[me], send_sem=send_sems.at[k - 1], recv_sem=recv_sems.at[k - 1],
                device_id=(px, py, pc), device_id_type=MESH)
            cp.start()
            sent.append(cp)
            arriving.append(pltpu.make_async_remote_copy(
                src_ref=g_ref.at[me], dst_ref=land_ref.at[peer], send_sem=send_sems.at[k - 1], recv_sem=recv_sems.at[k - 1],
                device_id=(px, py, pc), device_id_type=MESH))
        for cp in arriving:
            cp.wait_recv()
        for cp in sent:
            cp.wait_send()
        mine.wait()

    return pl.pallas_call(
        body, name=name, out_shape=_sds(g.shape, g.dtype),
        in_specs=[pl.BlockSpec(memory_space=pl.ANY)], out_specs=pl.BlockSpec(memory_space=pl.ANY),
        scratch_shapes=[pltpu.SemaphoreType.DMA((N_DEV - 1,)), pltpu.SemaphoreType.DMA((N_DEV - 1,)), pltpu.SemaphoreType.DMA],
    )(g)


def _sum_slots(parts, name):
    n, rows, lanes = parts.shape
    tr = 512 if rows % 512 == 0 else rows

    def body(p_ref, o_ref):
        acc = p_ref[0]
        for j in range(1, n):
            acc = acc + p_ref[j]
        o_ref[...] = acc

    return _call(
        body, name=name, grid=(rows // tr,),
        in_specs=[pl.BlockSpec((n, tr, lanes), lambda i: (0, i, 0))], out_specs=_rows(tr, lanes),
        out_shape=_sds((rows, lanes), parts.dtype),
    )(parts)


def _adamw(w, g, m, v, name):
    def body(w_ref, g_ref, m_ref, v_ref, d_ref, nm_ref, nv_ref):
        gv = g_ref[...]
        m_new = ADAM_B1 * m_ref[...] + (1.0 - ADAM_B1) * gv
        v_new = ADAM_B2 * v_ref[...] + (1.0 - ADAM_B2) * (gv * gv)
        nm_ref[...] = m_new
        nv_ref[...] = v_new
        m_hat = m_new / (1.0 - ADAM_B1 ** ADAM_STEP)
        v_hat = v_new / (1.0 - ADAM_B2 ** ADAM_STEP)
        d_ref[...] = -ADAM_LR * (m_hat / (jnp.sqrt(v_hat) + ADAM_EPS) + ADAM_WD * w_ref[...])

    out = _sds(w.shape, F32)
    return pl.pallas_call(
        body, name=name, out_shape=[out, out, out],
        compiler_params=pltpu.CompilerParams(vmem_limit_bytes=V7X_VMEM_LIMIT_BYTES),
    )(w, g, m, v)


def _loss_head(sq, d_model):
    def body(sq_ref, o_ref):
        tot = jnp.sum(jnp.sum(sq_ref[...], axis=1, keepdims=True), axis=0, keepdims=True)
        o_ref[...] = 0.5 * (tot / d_model)

    return pl.pallas_call(body, name="loss_head", out_shape=_sds((1, 1), F32))(sq)


_BIG = (("w_in", 1), ("w_out", 0), ("w_xq", 0), ("w_xk", 0), ("w_xv", 0), ("w_xo", 1), ("w_gate", 1), ("w_up", 1), ("w_down", 0))
_SMALL = ("g_mix_pre", "b_f", "rel_bias", "g_mix_post", "g_xattn_pre", "g_mem", "g_xattn_post", "g_ffn_pre", "g_ffn_post")
LANES = 128
BIG_ROW_ALIGN = 512


def _round_up(n, k):
    return -(-n // k) * k


def _pack_rows(flat_parts, row_align, dtype):
    starts, rows, padded = [], 0, []
    for p in flat_parts:
        n = _round_up(p.shape[0], LANES)
        starts.append(rows)
        rows += n // LANES
        padded.append(jnp.pad(p.astype(dtype), (0, n - p.shape[0])))
    total = _round_up(rows, row_align)
    padded.append(jnp.zeros(((total - rows) * LANES,), dtype))
    return jnp.concatenate(padded).reshape(total, LANES), starts


def _unpack_rows(buf, starts, shapes):
    lead = buf.shape[:-2]
    flat = buf.reshape(lead + (-1,))
    out = []
    for st, shp in zip(starts, shapes):
        n = int(np.prod(shp))
        out.append(flat[..., st * LANES:st * LANES + n].reshape(lead + tuple(shp)))
    return out


def kernel(x, mem, g_mix_pre, w_in, b_f, rel_bias, w_out, g_mix_post, g_xattn_pre, g_mem, w_xq, w_xk, w_xv, w_xo, g_xattn_post, g_ffn_pre, w_gate, w_up, w_down, g_ffn_post, loss_target, m_g_mix_pre, m_w_in, m_b_f, m_rel_bias, m_w_out, m_g_mix_post, m_g_xattn_pre, m_g_mem, m_w_xq, m_w_xk, m_w_xv, m_w_xo, m_g_xattn_post, m_g_ffn_pre, m_w_gate, m_w_up, m_w_down, m_g_ffn_post, v_g_mix_pre, v_w_in, v_b_f, v_rel_bias, v_w_out, v_g_mix_post, v_g_xattn_pre, v_g_mem, v_w_xq, v_w_xk, v_w_xv, v_w_xo, v_g_xattn_post, v_g_ffn_pre, v_w_gate, v_w_up, v_w_down, v_g_ffn_post):
    given = dict(locals())
    order = ("g_mix_pre", "w_in", "b_f", "rel_bias", "w_out", "g_mix_post", "g_xattn_pre", "g_mem", "w_xq", "w_xk", "w_xv",
             "w_xo", "g_xattn_post", "g_ffn_pre", "w_gate", "w_up", "w_down", "g_ffn_post")
    two_d = lambda a: a.reshape(a.shape[-2:])
    w_loc = {n: two_d(given[n]) for n in order}
    m_loc = {n: two_d(given["m_" + n]) for n in order}
    v_loc = {n: two_d(given["v_" + n]) for n in order}
    d_model = x.shape[-1]

    shard_shapes = [w_loc[n].shape for n, _ in _BIG]
    packed, starts = _pack_rows([w_loc[n].reshape(-1) for n, _ in _BIG], BIG_ROW_ALIGN, BF16)
    gathered = _all_gather(packed, "gather_weights")
    w_bf = {}
    for (n, axis), part in zip(_BIG, _unpack_rows(gathered, starts, shard_shapes)):
        r, c = part.shape[1:]
        w_bf[n] = part.reshape(N_DEV * r, c) if axis == 0 else part.transpose(1, 0, 2).reshape(r, N_DEV * c)

    small = {n: w_loc[n] for n in _SMALL}
    sq, grad_x, grads = _local_step(two_d(x), two_d(mem), two_d(loss_target), small, w_bf)

    per_owner = []
    for (n, axis), shp in zip(_BIG, shard_shapes):
        r, c = shp
        gfull = grads[n]
        per_owner.append(gfull.reshape(N_DEV, r * c) if axis == 0 else gfull.reshape(r, N_DEV, c).transpose(1, 0, 2).reshape(N_DEV, r * c))
    rows_big = packed.shape[0]
    slots = []
    for j in range(N_DEV):
        buf, _ = _pack_rows([p[j] for p in per_owner], BIG_ROW_ALIGN, F32)
        slots.append(buf)
    landed = _exchange(jnp.stack(slots), "exchange_grads")
    g_big = dict(zip([n for n, _ in _BIG], _unpack_rows(_sum_slots(landed, "sum_grads"), starts, shard_shapes)))
    assert landed.shape[1] == rows_big

    small_parts = [grads[n].reshape(-1) for n in _SMALL] + [sq.reshape(-1)]
    small_shapes = [w_loc[n].shape for n in _SMALL] + [sq.shape]
    spacked, sstarts = _pack_rows(small_parts, 8, F32)
    ssum = _sum_slots(_all_gather(spacked, "gather_small"), "sum_small")
    g_small = dict(zip(_SMALL, _unpack_rows(ssum, sstarts, small_shapes)[:-1]))
    sq_rows = sq.size // LANES
    loss = _loss_head(ssum[sstarts[-1]:sstarts[-1] + sq_rows], d_model).reshape(())

    g_loc, delta, new_m, new_v = {}, {}, {}, {}
    for n, _ in _BIG:
        g_loc[n] = g_big[n]
        delta[n], new_m[n], new_v[n] = _adamw(w_loc[n], g_big[n], m_loc[n], v_loc[n], "adamw_" + n)
    pk = lambda d: _pack_rows([d[n].reshape(-1) for n in _SMALL], 8, F32)[0]
    pstarts = _pack_rows([w_loc[n].reshape(-1) for n in _SMALL], 8, F32)[1]
    d_s, m_s, v_s = _adamw(pk(w_loc), pk(g_small), pk(m_loc), pk(v_loc), "adamw_small")
    shapes_s = [w_loc[n].shape for n in _SMALL]
    for n, dd, mm, vv in zip(_SMALL, _unpack_rows(d_s, pstarts, shapes_s), _unpack_rows(m_s, pstarts, shapes_s),
                             _unpack_rows(v_s, pstarts, shapes_s)):
        g_loc[n], delta[n], new_m[n], new_v[n] = g_small[n], dd, mm, vv

    like = lambda d: [d[n].reshape(given[n].shape) for n in order]
    return (loss, grad_x.reshape(x.shape), *like(g_loc), *like(delta), *like(new_m), *like(new_v))
```

```python
import functools

import numpy as np
import jax
import jax.numpy as jnp
from jax import lax
from jax.experimental import pallas as pl
from jax.experimental.pallas import tpu as pltpu

F32 = jnp.float32
BF16 = jnp.bfloat16
HIGHEST = lax.Precision.HIGHEST

RMS_EPS = 1e-6
HEAD_DIM = 64
N_HEADS = 8
HEAD_WIDTH = N_HEADS * HEAD_DIM
PAIR = 2 * HEAD_DIM
N_PAIRS = N_HEADS // 2
DIL_BLOCK = 128
DILATIONS = (1, 4, 16)
N_BUCKETS = 32
MAX_DISTANCE = 2048
N_MEM_HEADS = 4
QK_SCALE = HEAD_DIM ** -0.5
NEG = -1e30
N_DEV = 8

ADAM_LR = 0.001
ADAM_B1 = 0.9
ADAM_B2 = 0.999
ADAM_EPS = 1e-08
ADAM_WD = 0.01
ADAM_STEP = 10

V7X_VMEM_LIMIT_BYTES = 56 * 2 ** 20
ROW_TILE = 256
ATT_BLOCK = 512
SCAN_TILE = 256


def _call(body, *, name, grid, in_specs, out_specs, out_shape, scratch=()):
    return pl.pallas_call(
        body, name=name, grid=grid, in_specs=in_specs, out_specs=out_specs, out_shape=out_shape,
        scratch_shapes=list(scratch),
        compiler_params=pltpu.CompilerParams(
            dimension_semantics=("arbitrary",) * len(grid), vmem_limit_bytes=V7X_VMEM_LIMIT_BYTES))


def _rows(tm, n):
    return pl.BlockSpec((tm, n), lambda i: (i, 0))


def _resident(shape):
    zeros = (0,) * len(shape)
    return pl.BlockSpec(shape, lambda i: zeros, pipeline_mode=pl.Buffered(1))


def _acc_out(shape):
    zeros = (0,) * len(shape)
    return pl.BlockSpec(shape, lambda i: zeros)


def _sds(shape, dtype):
    return jax.ShapeDtypeStruct(shape, dtype)


def _dot(a, b):
    return jnp.dot(a, b, preferred_element_type=F32)


def _dot_nt(a, b):
    return lax.dot_general(a, b, (((1,), (1,)), ((), ())), preferred_element_type=F32)


def _dot_tn(a, b):
    return lax.dot_general(a, b, (((0,), (0,)), ((), ())), preferred_element_type=F32)


def _rms_fwd(x, g):
    r = lax.rsqrt(jnp.mean(x * x, axis=-1, keepdims=True) + RMS_EPS)
    return (x * r) * g


def _rms_bwd(xin, g, dy):
    r = lax.rsqrt(jnp.mean(xin * xin, axis=-1, keepdims=True) + RMS_EPS)
    xhat = xin * r
    dg = jnp.sum(dy * xhat, axis=0, keepdims=True)
    dxh = dy * g
    dx = r * (dxh - xhat * jnp.mean(dxh * xhat, axis=-1, keepdims=True))
    return dx, dg


def _first_head_lanes():
    return lax.broadcasted_iota(jnp.int32, (1, PAIR), 1) < HEAD_DIM


def _pick(mask, a, b):
    return jnp.where(mask, a, b)


def _zero_other(mask, v):
    return jnp.where(mask, v, jnp.zeros_like(v))


def _in_proj(x, g, w):
    s_len, d = x.shape
    tm = ROW_TILE
    hw = HEAD_WIDTH

    def body(x_ref, g_ref, w_ref, h_ref, fqkv_ref, gx_ref, dqkv_ref):
        h = _rms_fwd(x_ref[...], g_ref[...]).astype(BF16)
        h_ref[...] = h
        proj = _dot(h, w_ref[...])
        fqkv_ref[:, 0:hw] = (proj[:, 0:hw] * QK_SCALE).astype(BF16)
        fqkv_ref[:, hw:3 * hw] = proj[:, hw:3 * hw].astype(BF16)
        gx_ref[...] = proj[:, 3 * hw:4 * hw]
        dqkv_ref[:, 0:hw] = (proj[:, 4 * hw:5 * hw] * QK_SCALE).astype(BF16)
        dqkv_ref[:, hw:3 * hw] = proj[:, 5 * hw:7 * hw].astype(BF16)

    return _call(
        body, name="in_proj", grid=(s_len // tm,),
        in_specs=[_rows(tm, d), _resident((1, d)), _resident(w.shape)],
        out_specs=[_rows(tm, d), _rows(tm, 3 * hw), _rows(tm, hw), _rows(tm, 3 * hw)],
        out_shape=[_sds((s_len, d), BF16), _sds((s_len, 3 * hw), BF16), _sds((s_len, hw), F32), _sds((s_len, 3 * hw), BF16)],
    )(x, g, w)


def _swap_halves(x):
    return jnp.concatenate([pltpu.roll(x[:, i * PAIR:(i + 1) * PAIR], HEAD_DIM, 1) for i in range(x.shape[1] // PAIR)], axis=1)


def _split3(x):
    hi = x.astype(BF16)
    r = x - hi.astype(F32)
    mid = r.astype(BF16)
    lo = (r - mid.astype(F32)).astype(BF16)
    return hi, mid, lo


def _lane_in_head(width):
    return lax.broadcasted_iota(jnp.int32, (1, width), 1) % HEAD_DIM


def _place3(jj, first, pieces, base):
    out = base
    for i, p in enumerate(pieces):
        out = jnp.where(jj == first + i, p, out)
    return out


def _gate_scan(gx, b_exp):
    s_len, hw = gx.shape
    t = min(SCAN_TILE, s_len)
    tri = jnp.asarray(np.tril(np.ones((t, t), np.float32)))

    def body(gx_ref, b_ref, tri_ref, aq_ref, ak_ref, carry):
        @pl.when(pl.program_id(0) == 0)
        def _():
            carry[...] = jnp.zeros_like(carry)

        z = gx_ref[...] + b_ref[...]
        lf = jnp.minimum(z, 0.0) - jnp.log1p(jnp.exp(-jnp.abs(z)))
        c = jnp.dot(tri_ref[...], lf, precision=HIGHEST, preferred_element_type=F32) + carry[...]
        carry[...] = c[t - 1:t, :]
        hi, mid, lo = _split3(_swap_halves(c))
        jj = _lane_in_head(hw)
        zero = jnp.zeros_like(hi)
        one = jnp.ones_like(hi)
        aq_ref[...] = _place3(jj, 0, (hi, mid, lo), jnp.where(jj < 6, one, zero))
        ak_ref[...] = _place3(jj, 3, (-hi, -mid, -lo), jnp.where(jj < 9, one, zero))

    return _call(
        body, name="gate_scan", grid=(s_len // t,),
        in_specs=[_rows(t, hw), _resident((1, hw)), _resident((t, t))],
        out_specs=[_rows(t, hw), _rows(t, hw)],
        out_shape=[_sds((s_len, hw), BF16), _sds((s_len, hw), BF16)],
        scratch=[pltpu.VMEM((1, hw), F32)],
    )(gx, b_exp, tri)


def _lower_triangle(n):
    return lax.broadcasted_iota(jnp.int32, (n, n), 1) <= lax.broadcasted_iota(jnp.int32, (n, n), 0)


def _on_blocks(qi, kj, step):
    @pl.when(kj < qi)
    def _():
        step(False)

    @pl.when(kj == qi)
    def _():
        step(True)


def _fox_fwd(fqkv, aq, ak):
    s_len = fqkv.shape[0]
    bq = bk = min(ATT_BLOCK, s_len)
    nq, nk = s_len // bq, s_len // bk

    def body(q_ref, k_ref, v_ref, aq_ref, ak_ref, o_ref, lse_ref, qa_ref, m_ref, acc_ref):
        qi, kj = pl.program_id(1), pl.program_id(2)
        in0 = _first_head_lanes()
        not0 = jnp.logical_not(in0)

        @pl.when(kj == 0)
        def _():
            q2, a2 = q_ref[...], aq_ref[...]
            qa_ref[0] = jnp.where(in0, q2, a2)
            qa_ref[1] = jnp.where(in0, a2, q2)
            m_ref[...] = jnp.full_like(m_ref, NEG)
            acc_ref[...] = jnp.zeros_like(acc_ref)

        def step(masked):
            k2, v2, a2 = k_ref[...], v_ref[...], ak_ref[...]
            one = jnp.ones_like(v2)
            for a, mine in ((0, in0), (1, not0)):
                s = _dot_nt(qa_ref[a], jnp.where(mine, k2, a2))
                if masked:
                    s = jnp.where(_lower_triangle(bq), s, NEG)
                m_old = m_ref[a]
                m_new = jnp.maximum(m_old, jnp.max(s, axis=1, keepdims=True))
                p = jnp.exp(s - m_new).astype(BF16)
                acc_ref[a] = jnp.exp(m_old - m_new) * acc_ref[a] + _dot(p, jnp.where(mine, v2, one))
                m_ref[a] = m_new

        _on_blocks(qi, kj, step)

        @pl.when(kj == qi)
        def _():
            acc0, acc1 = acc_ref[0], acc_ref[1]
            l2 = pltpu.roll(_pick(in0, acc1, acc0), HEAD_DIM, 1)
            o_ref[...] = _pick(in0, acc0, acc1) / l2
            lse_ref[...] = _pick(in0, m_ref[0], m_ref[1]) + jnp.log(l2)

    blk = lambda rows: (rows, PAIR)
    qmap = lambda p, i, j: (i, p)
    return pl.pallas_call(
        body, name="fox_fwd", grid=(N_PAIRS, nq, nk),
        in_specs=[
            pl.BlockSpec(blk(bq), qmap),
            pl.BlockSpec(blk(bk), lambda p, i, j: (jnp.minimum(j, i), N_PAIRS + p)),
            pl.BlockSpec(blk(bk), lambda p, i, j: (jnp.minimum(j, i), 2 * N_PAIRS + p)),
            pl.BlockSpec(blk(bq), qmap),
            pl.BlockSpec(blk(bk), lambda p, i, j: (jnp.minimum(j, i), p)),
        ],
        out_specs=[pl.BlockSpec(blk(bq), qmap), pl.BlockSpec(blk(bq), qmap)],
        out_shape=[_sds((s_len, HEAD_WIDTH), F32), _sds((s_len, HEAD_WIDTH), F32)],
        scratch_shapes=[pltpu.VMEM((2, bq, PAIR), BF16), pltpu.VMEM((2, bq, 1), F32), pltpu.VMEM((2, bq, PAIR), F32)],
        compiler_params=pltpu.CompilerParams(dimension_semantics=("arbitrary",) * 3, vmem_limit_bytes=V7X_VMEM_LIMIT_BYTES),
    )(fqkv, fqkv, fqkv, aq, ak)


def _t5_bucket(dist):
    max_exact = N_BUCKETS // 2
    d = np.maximum(dist, 1).astype(np.float32)
    large = max_exact + (np.log(d / max_exact) / np.log(MAX_DISTANCE / max_exact) * (N_BUCKETS - max_exact)).astype(np.int32)
    large = np.minimum(large, N_BUCKETS - 1)
    return np.where(dist < max_exact, dist, large).astype(np.int32)


def _dil_buckets():
    w = DIL_BLOCK
    qi = np.arange(w)[:, None]
    kj = np.arange(2 * w)[None, :]
    sub = qi + w - kj
    band = (sub >= 0) & (sub <= w)
    out = [np.where(band, _t5_bucket(np.clip(sub, 0, w) * dil), -1) for dil in DILATIONS]
    return np.stack(out).astype(np.int32)


def _dil_bias(rel_bias, buckets):
    w = DIL_BLOCK

    def body(rb_ref, bk_ref, o_ref):
        for p in range(len(DILATIONS)):
            bk = bk_ref[p]
            for h in range(N_HEADS):
                def add(b, acc):
                    return acc + jnp.where(bk == b, rb_ref[b, h], 0.0)
                acc = lax.fori_loop(0, N_BUCKETS, add, jnp.zeros((w, 2 * w), F32))
                o_ref[p, h] = jnp.where(bk < 0, NEG, acc)

    return pl.pallas_call(
        body, name="dil_bias",
        in_specs=[pl.BlockSpec(memory_space=pltpu.SMEM), pl.BlockSpec(memory_space=pltpu.VMEM)],
        out_specs=pl.BlockSpec(memory_space=pltpu.VMEM),
        out_shape=_sds((len(DILATIONS), N_HEADS, w, 2 * w), F32),
    )(rel_bias, buckets)


def _dil_fwd(dqkv, bias, branch):
    dil = DILATIONS[branch]
    s_len = dqkv.shape[0]
    w = DIL_BLOCK
    hw = HEAD_WIDTH
    length = s_len // dil
    nb = length // w
    view = dqkv.reshape(length, dil * 3 * hw)

    def body(q_ref, kc_ref, kp_ref, vc_ref, vp_ref, b_ref, o_ref, lse_ref):
        n = pl.program_id(1)
        in0 = _first_head_lanes()
        not0 = jnp.logical_not(in0)
        prev_half = lax.broadcasted_iota(jnp.int32, (w, 2 * w), 1) < w
        no_prev = jnp.logical_and(n == 0, prev_half)
        for pr in range(N_PAIRS):
            sl = slice(pr * PAIR, (pr + 1) * PAIR)
            q2, kc2, kp2, vc2, vp2 = q_ref[:, sl], kc_ref[:, sl], kp_ref[:, sl], vc_ref[:, sl], vp_ref[:, sl]
            res = []
            for a, mine in ((0, in0), (1, not0)):
                qa = _zero_other(mine, q2)
                s = jnp.concatenate([_dot_nt(qa, kp2), _dot_nt(qa, kc2)], axis=1) + b_ref[2 * pr + a]
                s = jnp.where(no_prev, NEG, s)
                m = jnp.max(s, axis=1, keepdims=True)
                e = jnp.exp(s - m)
                l = jnp.sum(e, axis=1, keepdims=True)
                p = (e / l).astype(BF16)
                res.append((_dot(p[:, :w], vp2) + _dot(p[:, w:], vc2), m + jnp.log(l)))
            o_ref[:, sl] = _pick(in0, res[0][0], res[1][0])
            lse_ref[:, sl] = _pick(in0, res[0][1], res[1][1])

    prev = lambda n: jnp.maximum(n - 1, 0)
    out = pl.pallas_call(
        body, name=f"dil_fwd_{dil}", grid=(dil, nb),
        in_specs=[
            pl.BlockSpec((w, hw), lambda r, n: (n, 3 * r)),
            pl.BlockSpec((w, hw), lambda r, n: (n, 3 * r + 1)),
            pl.BlockSpec((w, hw), lambda r, n: (prev(n), 3 * r + 1)),
            pl.BlockSpec((w, hw), lambda r, n: (n, 3 * r + 2)),
            pl.BlockSpec((w, hw), lambda r, n: (prev(n), 3 * r + 2)),
            pl.BlockSpec((None, N_HEADS, w, 2 * w), lambda r, n: (branch, 0, 0, 0)),
        ],
        out_specs=[pl.BlockSpec((w, hw), lambda r, n: (n, r)), pl.BlockSpec((w, hw), lambda r, n: (n, r))],
        out_shape=[_sds((length, dil * hw), F32), _sds((length, dil * hw), F32)],
        compiler_params=pltpu.CompilerParams(dimension_semantics=("arbitrary",) * 2, vmem_limit_bytes=V7X_VMEM_LIMIT_BYTES),
    )(view, view, view, view, view, bias)
    return out[0].reshape(s_len, hw), out[1].reshape(s_len, hw)


def _mix_out(o_fox, o_br, lse_br, w_out, x, g_post):
    s_len, d = x.shape
    hw = HEAD_WIDTH
    tm = ROW_TILE

    def body(of_ref, o1, o2, o3, l1, l2, l3, w_ref, x_ref, g_ref, x1_ref, y1_ref, od_ref, lj_ref):
        la, lb, lc = l1[...], l2[...], l3[...]
        m = jnp.maximum(jnp.maximum(la, lb), lc)
        ea, eb, ec = jnp.exp(la - m), jnp.exp(lb - m), jnp.exp(lc - m)
        tot = ea + eb + ec
        o_dil = (ea / tot) * o1[...] + (eb / tot) * o2[...] + (ec / tot) * o3[...]
        od_ref[...] = o_dil
        lj_ref[...] = m + jnp.log(tot)
        y = _dot(of_ref[...].astype(BF16), w_ref[0:hw, :]) + _dot(o_dil.astype(BF16), w_ref[hw:2 * hw, :])
        y1_ref[...] = y
        x1_ref[...] = x_ref[...] + _rms_fwd(y, g_ref[...])

    half = _rows(tm, hw)
    return _call(
        body, name="mix_out", grid=(s_len // tm,),
        in_specs=[half] * 7 + [_resident(w_out.shape), _rows(tm, d), _resident((1, d))],
        out_specs=[_rows(tm, d), _rows(tm, d), half, half],
        out_shape=[_sds((s_len, d), F32), _sds((s_len, d), F32), _sds((s_len, hw), F32), _sds((s_len, hw), F32)],
    )(o_fox, *o_br, *lse_br, w_out, x, g_post)


def _mem_fwd(mem, g_mem, w_xk, w_xv):
    n_mem, d = mem.shape
    mw = w_xk.shape[1]

    def body(mem_ref, g_ref, wk_ref, wv_ref, hm_ref, k_ref, v_ref):
        hm = _rms_fwd(mem_ref[...], g_ref[...]).astype(BF16)
        hm_ref[...] = hm
        k_ref[...] = _dot(hm, wk_ref[...]).astype(BF16)
        v_ref[...] = _dot(hm, wv_ref[...]).astype(BF16)

    return pl.pallas_call(
        body, name="mem_fwd",
        out_shape=[_sds((n_mem, d), BF16), _sds((n_mem, mw), BF16), _sds((n_mem, mw), BF16)],
    )(mem, g_mem, w_xk, w_xv)


def _xattn_softmax(qa, k2):
    s = _dot_nt(qa, k2)
    m = jnp.max(s, axis=1, keepdims=True)
    e = jnp.exp(s - m)
    return e / jnp.sum(e, axis=1, keepdims=True)


def _xattn_fwd(x1, g_pre, w_xq, kx, vx, w_xo, g_post):
    s_len, d = x1.shape
    mw = w_xq.shape[1]
    n_mem = kx.shape[0]
    tm = ROW_TILE

    def body(x_ref, gp_ref, wq_ref, k_ref, v_ref, wo_ref, go_ref, x2_ref, y2_ref, h2_ref, q_ref, o_ref):
        x = x_ref[...]
        h = _rms_fwd(x, gp_ref[...]).astype(BF16)
        h2_ref[...] = h
        q = (_dot(h, wq_ref[...]) * QK_SCALE).astype(BF16)
        q_ref[...] = q
        in0 = _first_head_lanes()
        not0 = jnp.logical_not(in0)
        for pr in range(mw // PAIR):
            sl = slice(pr * PAIR, (pr + 1) * PAIR)
            q2, k2, v2 = q[:, sl], k_ref[:, sl], v_ref[:, sl]
            oa = [_dot(_xattn_softmax(_zero_other(mine, q2), k2).astype(BF16), v2) for mine in (in0, not0)]
            o_ref[:, sl] = _pick(in0, oa[0], oa[1]).astype(BF16)
        y = _dot(o_ref[...], wo_ref[...])
        y2_ref[...] = y
        x2_ref[...] = x + _rms_fwd(y, go_ref[...])

    return _call(
        body, name="xattn_fwd", grid=(s_len // tm,),
        in_specs=[_rows(tm, d), _resident((1, d)), _resident(w_xq.shape), _resident((n_mem, mw)), _resident((n_mem, mw)),
                  _resident(w_xo.shape), _resident((1, d))],
        out_specs=[_rows(tm, d), _rows(tm, d), _rows(tm, d), _rows(tm, mw), _rows(tm, mw)],
        out_shape=[_sds((s_len, d), F32), _sds((s_len, d), F32), _sds((s_len, d), BF16), _sds((s_len, mw), BF16),
                   _sds((s_len, mw), BF16)],
    )(x1, g_pre, w_xq, kx, vx, w_xo, g_post)


def _ffn_up(x2, g_pre, w_gate, w_up):
    s_len, d = x2.shape
    dff = w_gate.shape[1]
    tm = ROW_TILE

    def body(x_ref, g_ref, wg_ref, wu_ref, h_ref, a_ref, u_ref, z_ref):
        h = _rms_fwd(x_ref[...], g_ref[...]).astype(BF16)
        h_ref[...] = h
        a = _dot(h, wg_ref[...])
        u = _dot(h, wu_ref[...])
        a_ref[...] = a.astype(BF16)
        u_ref[...] = u.astype(BF16)
        z_ref[...] = ((a * jax.nn.sigmoid(a)) * u).astype(BF16)

    return _call(
        body, name="ffn_up", grid=(s_len // tm,),
        in_specs=[_rows(tm, d), _resident((1, d)), _resident(w_gate.shape), _resident(w_up.shape)],
        out_specs=[_rows(tm, d), _rows(tm, dff), _rows(tm, dff), _rows(tm, dff)],
        out_shape=[_sds((s_len, d), BF16)] + [_sds((s_len, dff), BF16)] * 3,
    )(x2, g_pre, w_gate, w_up)


def _ffn_down_loss(z, w_down, x2, g_post, target):
    s_len, d = x2.shape
    dff = z.shape[1]
    tm = ROW_TILE

    def body(z_ref, w_ref, x_ref, g_ref, t_ref, y_ref, dx_ref, sq_ref):
        @pl.when(pl.program_id(0) == 0)
        def _():
            sq_ref[...] = jnp.zeros_like(sq_ref)

        y = _dot(z_ref[...], w_ref[...])
        y_ref[...] = y
        err = (x_ref[...] + _rms_fwd(y, g_ref[...])) - t_ref[...]
        sq_ref[...] += jnp.sum(err * err, axis=0, keepdims=True)
        dx_ref[...] = err * (1.0 / d)

    return _call(
        body, name="ffn_down_loss", grid=(s_len // tm,),
        in_specs=[_rows(tm, dff), _resident(w_down.shape), _rows(tm, d), _resident((1, d)), _rows(tm, d)],
        out_specs=[_rows(tm, d), _rows(tm, d), _acc_out((1, d))],
        out_shape=[_sds((s_len, d), F32), _sds((s_len, d), F32), _sds((1, d), F32)],
    )(z, w_down, x2, g_post, target)


def _weight_grad(a, b, name):
    s_len, k = a.shape
    n = b.shape[1]
    ts = 512 if s_len % 512 == 0 else s_len
    tn = n
    while k * tn * 4 > 8 * 2 ** 20 and tn % 256 == 0:
        tn //= 2

    def body(a_ref, b_ref, o_ref):
        @pl.when(pl.program_id(1) == 0)
        def _():
            o_ref[...] = jnp.zeros_like(o_ref)

        o_ref[...] += _dot_tn(a_ref[...].astype(BF16), b_ref[...].astype(BF16))

    return pl.pallas_call(
        body, name=name, grid=(n // tn, s_len // ts),
        in_specs=[pl.BlockSpec((ts, k), lambda j, i: (i, 0)), pl.BlockSpec((ts, tn), lambda j, i: (i, j))],
        out_specs=pl.BlockSpec((k, tn), lambda j, i: (0, j)),
        out_shape=_sds((k, n), F32),
        compiler_params=pltpu.CompilerParams(dimension_semantics=("arbitrary",) * 2, vmem_limit_bytes=V7X_VMEM_LIMIT_BYTES),
    )(a, b)


def _ffn_bwd_a(dx3, y3, g_post, w_down_t, a, u):
    s_len, d = dx3.shape
    dff = a.shape[1]
    tm = ROW_TILE

    def body(dx_ref, y_ref, g_ref, w_ref, a_ref, u_ref, dy_ref, da_ref, du_ref, dg_ref):
        @pl.when(pl.program_id(0) == 0)
        def _():
            dg_ref[...] = jnp.zeros_like(dg_ref)

        dy, dg = _rms_bwd(y_ref[...], g_ref[...], dx_ref[...])
        dg_ref[...] += dg
        dyb = dy.astype(BF16)
        dy_ref[...] = dyb
        dz = _dot(dyb, w_ref[...])
        av = a_ref[...].astype(F32)
        uv = u_ref[...].astype(F32)
        sg = jax.nn.sigmoid(av)
        da_ref[...] = (dz * uv * (sg * (1.0 + av * (1.0 - sg)))).astype(BF16)
        du_ref[...] = (dz * (av * sg)).astype(BF16)

    return _call(
        body, name="ffn_bwd_a", grid=(s_len // tm,),
        in_specs=[_rows(tm, d), _rows(tm, d), _resident((1, d)), _resident(w_down_t.shape), _rows(tm, dff), _rows(tm, dff)],
        out_specs=[_rows(tm, d), _rows(tm, dff), _rows(tm, dff), _acc_out((1, d))],
        out_shape=[_sds((s_len, d), BF16), _sds((s_len, dff), BF16), _sds((s_len, dff), BF16), _sds((1, d), F32)],
    )(dx3, y3, g_post, w_down_t, a, u)


def _ffn_bwd_b(da, du, w_gate_t, w_up_t, dx3, x2, g_pre):
    s_len, d = x2.shape
    dff = da.shape[1]
    tm = ROW_TILE

    def body(da_ref, du_ref, wg_ref, wu_ref, dx_ref, x_ref, g_ref, o_ref, dg_ref):
        @pl.when(pl.program_id(0) == 0)
        def _():
            dg_ref[...] = jnp.zeros_like(dg_ref)

        dh = _dot(da_ref[...], wg_ref[...]) + _dot(du_ref[...], wu_ref[...])
        dx, dg = _rms_bwd(x_ref[...], g_ref[...], dh)
        dg_ref[...] += dg
        o_ref[...] = dx_ref[...] + dx

    return _call(
        body, name="ffn_bwd_b", grid=(s_len // tm,),
        in_specs=[_rows(tm, dff), _rows(tm, dff), _resident(w_gate_t.shape), _resident(w_up_t.shape), _rows(tm, d),
                  _rows(tm, d), _resident((1, d))],
        out_specs=[_rows(tm, d), _acc_out((1, d))],
        out_shape=[_sds((s_len, d), F32), _sds((1, d), F32)],
    )(da, du, w_gate_t, w_up_t, dx3, x2, g_pre)


def _xattn_bwd(dx2, y2, g_post, w_xo_t, q, kx, vx, w_xq_t, x1, g_pre):
    s_len, d = x1.shape
    mw = q.shape[1]
    n_mem = kx.shape[0]
    tm = ROW_TILE

    def body(dx_ref, y_ref, go_ref, wo_ref, q_ref, k_ref, v_ref, wq_ref, x_ref, gp_ref,
             dx1_ref, dy_ref, dq_ref, dk_ref, dv_ref, dgo_ref, dgp_ref):
        @pl.when(pl.program_id(0) == 0)
        def _():
            dk_ref[...] = jnp.zeros_like(dk_ref)
            dv_ref[...] = jnp.zeros_like(dv_ref)
            dgo_ref[...] = jnp.zeros_like(dgo_ref)
            dgp_ref[...] = jnp.zeros_like(dgp_ref)

        dxin = dx_ref[...]
        dy, dgo = _rms_bwd(y_ref[...], go_ref[...], dxin)
        dgo_ref[...] += dgo
        dyb = dy.astype(BF16)
        dy_ref[...] = dyb
        do = _dot(dyb, wo_ref[...]).astype(BF16)
        in0 = _first_head_lanes()
        not0 = jnp.logical_not(in0)
        for pr in range(mw // PAIR):
            sl = slice(pr * PAIR, (pr + 1) * PAIR)
            q2, k2, v2, do2 = q_ref[:, sl], k_ref[:, sl], v_ref[:, sl], do[:, sl]
            dqs = []
            dk2 = jnp.zeros((n_mem, PAIR), F32)
            dv2 = jnp.zeros((n_mem, PAIR), F32)
            for mine in (in0, not0):
                qa = _zero_other(mine, q2)
                doa = _zero_other(mine, do2)
                p = _xattn_softmax(qa, k2)
                dp = _dot_nt(doa, v2)
                ds = (p * (dp - jnp.sum(p * dp, axis=1, keepdims=True))).astype(BF16)
                dqs.append(_dot(ds, k2))
                dk2 = dk2 + _dot_tn(ds, qa)
                dv2 = dv2 + _dot_tn(p.astype(BF16), doa)
            dq_ref[:, sl] = (_pick(in0, dqs[0], dqs[1]) * QK_SCALE).astype(BF16)
            dk_ref[:, sl] += dk2
            dv_ref[:, sl] += dv2
        dh = _dot(dq_ref[...], wq_ref[...])
        dx, dgp = _rms_bwd(x_ref[...], gp_ref[...], dh)
        dgp_ref[...] += dgp
        dx1_ref[...] = dxin + dx

    return _call(
        body, name="xattn_bwd", grid=(s_len // tm,),
        in_specs=[_rows(tm, d), _rows(tm, d), _resident((1, d)), _resident(w_xo_t.shape), _rows(tm, mw),
                  _resident((n_mem, mw)), _resident((n_mem, mw)), _resident(w_xq_t.shape), _rows(tm, d), _resident((1, d))],
        out_specs=[_rows(tm, d), _rows(tm, d), _rows(tm, mw), _acc_out((n_mem, mw)), _acc_out((n_mem, mw)),
                   _acc_out((1, d)), _acc_out((1, d))],
        out_shape=[_sds((s_len, d), F32), _sds((s_len, d), BF16), _sds((s_len, mw), BF16), _sds((n_mem, mw), F32),
                   _sds((n_mem, mw), F32), _sds((1, d), F32), _sds((1, d), F32)],
    )(dx2, y2, g_post, w_xo_t, q, kx, vx, w_xq_t, x1, g_pre)


def _mem_bwd(dk, dv, w_xk_t, w_xv_t, hm, mem, g_mem):
    n_mem, d = mem.shape
    mw = dk.shape[1]

    def body(dk_ref, dv_ref, wk_ref, wv_ref, hm_ref, mem_ref, g_ref, dwk_ref, dwv_ref, dg_ref):
        dkb = dk_ref[...].astype(BF16)
        dvb = dv_ref[...].astype(BF16)
        dhm = _dot(dkb, wk_ref[...]) + _dot(dvb, wv_ref[...])
        _, dg = _rms_bwd(mem_ref[...], g_ref[...], dhm)
        dg_ref[...] = dg
        dwk_ref[...] = _dot_tn(hm_ref[...], dkb)
        dwv_ref[...] = _dot_tn(hm_ref[...], dvb)

    return pl.pallas_call(
        body, name="mem_bwd",
        out_shape=[_sds((d, mw), F32), _sds((d, mw), F32), _sds((1, d), F32)],
    )(dk, dv, w_xk_t, w_xv_t, hm, mem, g_mem)


def _mix_out_bwd(dx1, y1, g_post, w_out_t, o_fox, o_dil):
    s_len, d = dx1.shape
    hw = HEAD_WIDTH
    tm = ROW_TILE
    head_of = np.arange(hw) // HEAD_DIM
    ones = jnp.asarray((head_of[:, None] == head_of[None, :]).astype(np.float32))

    def body(dx_ref, y_ref, g_ref, w_ref, of_ref, od_ref, ones_ref, dy_ref, do_ref, dl_ref, dg_ref):
        @pl.when(pl.program_id(0) == 0)
        def _():
            dg_ref[...] = jnp.zeros_like(dg_ref)

        dy, dg = _rms_bwd(y_ref[...], g_ref[...], dx_ref[...])
        dg_ref[...] += dg
        dyb = dy.astype(BF16)
        dy_ref[...] = dyb
        do = _dot(dyb, w_ref[...])
        do_ref[...] = do.astype(BF16)
        dl_ref[:, 0:hw] = jnp.dot(do[:, 0:hw] * of_ref[...], ones_ref[...], precision=HIGHEST, preferred_element_type=F32)
        dl_ref[:, hw:2 * hw] = jnp.dot(do[:, hw:2 * hw] * od_ref[...], ones_ref[...], precision=HIGHEST,
                                       preferred_element_type=F32)

    return _call(
        body, name="mix_out_bwd", grid=(s_len // tm,),
        in_specs=[_rows(tm, d), _rows(tm, d), _resident((1, d)), _resident(w_out_t.shape), _rows(tm, hw), _rows(tm, hw),
                  _resident((hw, hw))],
        out_specs=[_rows(tm, d), _rows(tm, 2 * hw), _rows(tm, 2 * hw), _acc_out((1, d))],
        out_shape=[_sds((s_len, d), BF16), _sds((s_len, 2 * hw), BF16), _sds((s_len, 2 * hw), F32), _sds((1, d), F32)],
    )(dx1, y1, g_post, w_out_t, o_fox, o_dil, ones)


def _fox_bwd_prep(aq, lse, delta):
    s_len, hw = aq.shape
    tm = ROW_TILE

    def body(aq_ref, lse_ref, dl_ref, aql_ref, ad_ref):
        jj = _lane_in_head(hw)
        l3 = _split3(_swap_halves(lse_ref[...]))
        aql_ref[...] = _place3(jj, 6, [-p for p in l3], aq_ref[...])
        d3 = _split3(_swap_halves(dl_ref[...]))
        ad_ref[...] = _place3(jj, 0, [-p for p in d3], jnp.zeros((tm, hw), BF16))

    half = _rows(tm, hw)
    return _call(
        body, name="fox_bwd_prep", grid=(s_len // tm,),
        in_specs=[half, half, half], out_specs=[half, half],
        out_shape=[_sds((s_len, hw), BF16), _sds((s_len, hw), BF16)],
    )(aq, lse, delta)


def _ones_on_first3(shape):
    jj = lax.broadcasted_iota(jnp.int32, shape, 1) % HEAD_DIM
    return jnp.where(jj < 3, 1.0, 0.0).astype(BF16)


def _fox_bwd_dq(fqkv, do, aql, ak, ad):
    s_len = fqkv.shape[0]
    bq = bk = min(ATT_BLOCK, s_len)
    nq, nk = s_len // bq, s_len // bk

    def body(q_ref, k_ref, v_ref, do_ref, aql_ref, ak_ref, ad_ref, dq_ref, rs_ref, qa_ref, doa_ref, acc_ref):
        qi, kj = pl.program_id(1), pl.program_id(2)
        in0 = _first_head_lanes()
        not0 = jnp.logical_not(in0)

        @pl.when(kj == 0)
        def _():
            q2, do2, a2, d2 = q_ref[...], do_ref[...], aql_ref[...], ad_ref[...]
            qa_ref[0] = jnp.where(in0, q2, a2)
            qa_ref[1] = jnp.where(in0, a2, q2)
            doa_ref[0] = jnp.where(in0, do2, d2)
            doa_ref[1] = jnp.where(in0, d2, do2)
            acc_ref[...] = jnp.zeros_like(acc_ref)

        def step(masked):
            k2, v2, a2 = k_ref[...], v_ref[...], ak_ref[...]
            one = jnp.ones_like(k2)
            one3 = _ones_on_first3(v2.shape)
            for a, mine in ((0, in0), (1, not0)):
                s = _dot_nt(qa_ref[a], jnp.where(mine, k2, a2))
                if masked:
                    s = jnp.where(_lower_triangle(bq), s, NEG)
                dp = _dot_nt(doa_ref[a], jnp.where(mine, v2, one3))
                ds = (jnp.exp(s) * dp).astype(BF16)
                acc_ref[a] += _dot(ds, jnp.where(mine, k2, one))

        _on_blocks(qi, kj, step)

        @pl.when(kj == qi)
        def _():
            dq_ref[...] = (_pick(in0, acc_ref[0], acc_ref[1]) * QK_SCALE).astype(BF16)
            rs_ref[...] = pltpu.roll(_pick(in0, acc_ref[1], acc_ref[0]), HEAD_DIM, 1)

    blk = lambda rows: (rows, PAIR)
    qmap = lambda p, i, j: (i, p)
    return pl.pallas_call(
        body, name="fox_bwd_dq", grid=(N_PAIRS, nq, nk),
        in_specs=[
            pl.BlockSpec(blk(bq), qmap),
            pl.BlockSpec(blk(bk), lambda p, i, j: (jnp.minimum(j, i), N_PAIRS + p)),
            pl.BlockSpec(blk(bk), lambda p, i, j: (jnp.minimum(j, i), 2 * N_PAIRS + p)),
            pl.BlockSpec(blk(bq), qmap), pl.BlockSpec(blk(bq), qmap),
            pl.BlockSpec(blk(bk), lambda p, i, j: (jnp.minimum(j, i), p)),
            pl.BlockSpec(blk(bq), qmap),
        ],
        out_specs=[pl.BlockSpec(blk(bq), qmap), pl.BlockSpec(blk(bq), qmap)],
        out_shape=[_sds((s_len, HEAD_WIDTH), BF16), _sds((s_len, HEAD_WIDTH), F32)],
        scratch_shapes=[pltpu.VMEM((2, bq, PAIR), BF16), pltpu.VMEM((2, bq, PAIR), BF16), pltpu.VMEM((2, bq, PAIR), F32)],
        compiler_params=pltpu.CompilerParams(dimension_semantics=("arbitrary",) * 3, vmem_limit_bytes=V7X_VMEM_LIMIT_BYTES),
    )(fqkv, fqkv, fqkv, do, aql, ak, ad)


def _fox_bwd_dkv(fqkv, do, aql, ak, ad):
    s_len = fqkv.shape[0]
    bq = bk = min(ATT_BLOCK, s_len)
    nq, nk = s_len // bq, s_len // bk

    def body(q_ref, k_ref, v_ref, do_ref, aql_ref, ak_ref, ad_ref, dk_ref, dv_ref, dc_ref, ka_ref, va_ref, r_ref, dvacc_ref):
        kj, qi = pl.program_id(1), pl.program_id(2)
        in0 = _first_head_lanes()
        not0 = jnp.logical_not(in0)

        @pl.when(qi == 0)
        def _():
            k2, v2, a2 = k_ref[...], v_ref[...], ak_ref[...]
            one3 = _ones_on_first3(v2.shape)
            ka_ref[0] = jnp.where(in0, k2, a2)
            ka_ref[1] = jnp.where(in0, a2, k2)
            va_ref[0] = jnp.where(in0, v2, one3)
            va_ref[1] = jnp.where(in0, one3, v2)
            r_ref[...] = jnp.zeros_like(r_ref)
            dvacc_ref[...] = jnp.zeros_like(dvacc_ref)

        def step(masked):
            q2, do2, a2, d2 = q_ref[...], do_ref[...], aql_ref[...], ad_ref[...]
            one = jnp.ones_like(q2)
            for a, mine in ((0, in0), (1, not0)):
                doa = jnp.where(mine, do2, d2)
                s = _dot_nt(jnp.where(mine, q2, a2), ka_ref[a])
                if masked:
                    s = jnp.where(_lower_triangle(bq), s, NEG)
                p = jnp.exp(s)
                ds = (p * _dot_nt(doa, va_ref[a])).astype(BF16)
                dvacc_ref[a] += _dot_tn(p.astype(BF16), doa)
                r_ref[a] += _dot_tn(ds, jnp.where(mine, q2, one))

        _on_blocks(qi, kj, step)

        @pl.when(qi == nq - 1)
        def _():
            dk_ref[...] = _pick(in0, r_ref[0], r_ref[1]).astype(BF16)
            dv_ref[...] = _pick(in0, dvacc_ref[0], dvacc_ref[1]).astype(BF16)
            dc_ref[...] = -pltpu.roll(_pick(in0, r_ref[1], r_ref[0]), HEAD_DIM, 1)

    blk = lambda rows: (rows, PAIR)
    qmap = lambda p, j, i: (jnp.maximum(i, j), p)
    kvout = lambda p, j, i: (j, p)
    return pl.pallas_call(
        body, name="fox_bwd_dkv", grid=(N_PAIRS, nk, nq),
        in_specs=[
            pl.BlockSpec(blk(bq), qmap),
            pl.BlockSpec(blk(bk), lambda p, j, i: (j, N_PAIRS + p)),
            pl.BlockSpec(blk(bk), lambda p, j, i: (j, 2 * N_PAIRS + p)),
            pl.BlockSpec(blk(bq), qmap), pl.BlockSpec(blk(bq), qmap),
            pl.BlockSpec(blk(bk), kvout),
            pl.BlockSpec(blk(bq), qmap),
        ],
        out_specs=[pl.BlockSpec(blk(bk), kvout)] * 3,
        out_shape=[_sds((s_len, HEAD_WIDTH), BF16), _sds((s_len, HEAD_WIDTH), BF16), _sds((s_len, HEAD_WIDTH), F32)],
        scratch_shapes=[pltpu.VMEM((2, bk, PAIR), BF16), pltpu.VMEM((2, bk, PAIR), BF16), pltpu.VMEM((2, bk, PAIR), F32),
                        pltpu.VMEM((2, bk, PAIR), F32)],
        compiler_params=pltpu.CompilerParams(dimension_semantics=("arbitrary",) * 3, vmem_limit_bytes=V7X_VMEM_LIMIT_BYTES),
    )(fqkv, fqkv, fqkv, do, aql, ak, ad)


def _gate_bwd(rs, dc, gx, b_exp):
    s_len, hw = gx.shape
    t = min(SCAN_TILE, s_len)
    nt = s_len // t
    tri = jnp.asarray(np.triu(np.ones((t, t), np.float32)))

    def body(rs_ref, dc_ref, gx_ref, b_ref, tri_ref, dgx_ref, db_ref, carry):
        @pl.when(pl.program_id(0) == 0)
        def _():
            carry[...] = jnp.zeros_like(carry)
            db_ref[...] = jnp.zeros_like(db_ref)

        dlf = jnp.dot(tri_ref[...], rs_ref[...] + dc_ref[...], precision=HIGHEST, preferred_element_type=F32) + carry[...]
        carry[...] = dlf[0:1, :]
        dgate = dlf * jax.nn.sigmoid(-(gx_ref[...] + b_ref[...]))
        db_ref[...] += jnp.sum(dgate, axis=0, keepdims=True)
        lane = lax.broadcasted_iota(jnp.int32, (1, hw), 1)
        dgx_ref[...] = jnp.where(lane % HEAD_DIM == 0, dgate, 0.0).astype(BF16)

    rev = lambda i: (nt - 1 - i, 0)
    return _call(
        body, name="gate_bwd", grid=(nt,),
        in_specs=[pl.BlockSpec((t, hw), rev)] * 3 + [_resident((1, hw)), _resident((t, t))],
        out_specs=[pl.BlockSpec((t, hw), rev), _acc_out((1, hw))],
        out_shape=[_sds((s_len, hw), BF16), _sds((1, hw), F32)],
        scratch=[pltpu.VMEM((1, hw), F32)],
    )(rs, dc, gx, b_exp, tri)


def _dil_bwd(dqkv, do, lj, delta, bias, branch):
    dil = DILATIONS[branch]
    s_len = dqkv.shape[0]
    w = DIL_BLOCK
    hw = HEAD_WIDTH
    length = s_len // dil
    nb = length // w
    qkv_v = dqkv.reshape(length, dil * 3 * hw)
    do_v = do.reshape(length, dil * 2 * hw)
    dl_v = delta.reshape(length, dil * 2 * hw)
    lj_v = lj.reshape(length, dil * hw)

    def body(q0_ref, q1_ref, kp_ref, kc_ref, vp_ref, vc_ref, do0_ref, do1_ref, l0_ref, l1_ref, d0_ref, d1_ref, b_ref,
             dq_ref, dk_ref, dv_ref, dsum_ref):
        r, n = pl.program_id(0), pl.program_id(1)
        in0 = _first_head_lanes()
        not0 = jnp.logical_not(in0)
        first = n == 0
        last = n == nb - 1

        @pl.when(jnp.logical_and(r == 0, n == 0))
        def _():
            dsum_ref[...] = jnp.zeros_like(dsum_ref)

        for pr in range(N_PAIRS):
            sl = slice(pr * PAIR, (pr + 1) * PAIR)
            q0, q1 = q0_ref[:, sl], q1_ref[:, sl]
            kp2, kc2, vp2, vc2 = kp_ref[:, sl], kc_ref[:, sl], vp_ref[:, sl], vc_ref[:, sl]
            do0, do1 = do0_ref[:, sl], do1_ref[:, sl]
            dqs = []
            dk2 = jnp.zeros((w, PAIR), F32)
            dv2 = jnp.zeros((w, PAIR), F32)
            for a, mine in ((0, in0), (1, not0)):
                h = 2 * pr + a
                col = slice(pr * PAIR + a * HEAD_DIM, pr * PAIR + a * HEAD_DIM + 1)
                qa0, qa1 = _zero_other(mine, q0), _zero_other(mine, q1)
                doa0, doa1 = _zero_other(mine, do0), _zero_other(mine, do1)
                b_prev, b_cur = b_ref[h, :, 0:w], b_ref[h, :, w:2 * w]
                s_a = jnp.where(first, NEG, _dot_nt(qa0, kp2) + b_prev)
                s_b = _dot_nt(qa0, kc2) + b_cur
                s_c = jnp.where(last, NEG, _dot_nt(qa1, kc2) + b_prev)
                p_a = jnp.exp(s_a - l0_ref[:, col])
                p_b = jnp.exp(s_b - l0_ref[:, col])
                p_c = jnp.exp(s_c - l1_ref[:, col])
                ds_a = p_a * (_dot_nt(doa0, vp2) - d0_ref[:, col])
                ds_b = p_b * (_dot_nt(doa0, vc2) - d0_ref[:, col])
                ds_c = p_c * (_dot_nt(doa1, vc2) - d1_ref[:, col])
                dsum_ref[h, :, 0:w] += ds_a
                dsum_ref[h, :, w:2 * w] += ds_b
                ds_a, ds_b, ds_c = ds_a.astype(BF16), ds_b.astype(BF16), ds_c.astype(BF16)
                dqs.append(_dot(ds_a, kp2) + _dot(ds_b, kc2))
                dk2 = dk2 + _dot_tn(ds_b, qa0) + _dot_tn(ds_c, qa1)
                dv2 = dv2 + _dot_tn(p_b.astype(BF16), doa0) + _dot_tn(p_c.astype(BF16), doa1)
            dq_ref[:, sl] = _pick(in0, dqs[0], dqs[1])
            dk_ref[:, sl] = dk2
            dv_ref[:, sl] = dv2

    prev = lambda n: jnp.maximum(n - 1, 0)
    nxt = lambda n: jnp.minimum(n + 1, nb - 1)
    blk = (w, hw)
    outs = pl.pallas_call(
        body, name=f"dil_bwd_{dil}", grid=(dil, nb),
        in_specs=[
            pl.BlockSpec(blk, lambda r, n: (n, 3 * r)),
            pl.BlockSpec(blk, lambda r, n: (nxt(n), 3 * r)),
            pl.BlockSpec(blk, lambda r, n: (prev(n), 3 * r + 1)),
            pl.BlockSpec(blk, lambda r, n: (n, 3 * r + 1)),
            pl.BlockSpec(blk, lambda r, n: (prev(n), 3 * r + 2)),
            pl.BlockSpec(blk, lambda r, n: (n, 3 * r + 2)),
            pl.BlockSpec(blk, lambda r, n: (n, 2 * r + 1)),
            pl.BlockSpec(blk, lambda r, n: (nxt(n), 2 * r + 1)),
            pl.BlockSpec(blk, lambda r, n: (n, r)),
            pl.BlockSpec(blk, lambda r, n: (nxt(n), r)),
            pl.BlockSpec(blk, lambda r, n: (n, 2 * r + 1)),
            pl.BlockSpec(blk, lambda r, n: (nxt(n), 2 * r + 1)),
            pl.BlockSpec((None, N_HEADS, w, 2 * w), lambda r, n: (branch, 0, 0, 0)),
        ],
        out_specs=[pl.BlockSpec(blk, lambda r, n: (n, r))] * 3 + [pl.BlockSpec((N_HEADS, w, 2 * w), lambda r, n: (0, 0, 0))],
        out_shape=[_sds((length, dil * hw), F32)] * 3 + [_sds((N_HEADS, w, 2 * w), F32)],
        compiler_params=pltpu.CompilerParams(dimension_semantics=("arbitrary",) * 2, vmem_limit_bytes=V7X_VMEM_LIMIT_BYTES),
    )(qkv_v, qkv_v, qkv_v, qkv_v, qkv_v, qkv_v, do_v, do_v, lj_v, lj_v, dl_v, dl_v, bias)
    return [o.reshape(s_len, hw) for o in outs[:3]] + [outs[3]]


def _rel_bias_grad(dsum, buckets):
    w = DIL_BLOCK

    def body(ds_ref, bk_ref, o_ref):
        row = lax.broadcasted_iota(jnp.int32, (N_BUCKETS, PAIR), 0)
        lane = lax.broadcasted_iota(jnp.int32, (N_BUCKETS, PAIR), 1)

        def per_bucket(b, acc):
            for p in range(len(DILATIONS)):
                hit = bk_ref[p] == b
                for h in range(N_HEADS):
                    part = jnp.where(hit, ds_ref[p, h], 0.0)
                    tot = jnp.sum(jnp.sum(part, axis=1, keepdims=True), axis=0, keepdims=True)
                    acc = acc + jnp.where(jnp.logical_and(row == b, lane == h), tot, 0.0)
            return acc

        o_ref[...] = lax.fori_loop(0, N_BUCKETS, per_bucket, jnp.zeros((N_BUCKETS, PAIR), F32))

    return pl.pallas_call(body, name="rel_bias_grad", out_shape=_sds((N_BUCKETS, PAIR), F32))(dsum, buckets)


def _in_proj_bwd(dfq, dfk, dfv, dgx, ddq, ddk, ddv, w_in_t, dx1, x, g_pre):
    s_len, d = x.shape
    hw = HEAD_WIDTH
    tm = ROW_TILE

    def body(fq, fk, fv, gx, q1, q2, q3, k1, k2, k3, v1, v2, v3, w_ref, dx_ref, x_ref, g_ref, o_ref, dp_ref, dg_ref):
        @pl.when(pl.program_id(0) == 0)
        def _():
            dg_ref[...] = jnp.zeros_like(dg_ref)

        dp_ref[:, 0:hw] = fq[...]
        dp_ref[:, hw:2 * hw] = fk[...]
        dp_ref[:, 2 * hw:3 * hw] = fv[...]
        dp_ref[:, 3 * hw:4 * hw] = gx[...]
        dp_ref[:, 4 * hw:5 * hw] = (((q1[...] + q2[...]) + q3[...]) * QK_SCALE).astype(BF16)
        dp_ref[:, 5 * hw:6 * hw] = ((k1[...] + k2[...]) + k3[...]).astype(BF16)
        dp_ref[:, 6 * hw:7 * hw] = ((v1[...] + v2[...]) + v3[...]).astype(BF16)
        dh = _dot(dp_ref[...], w_ref[...])
        dx, dg = _rms_bwd(x_ref[...], g_ref[...], dh)
        dg_ref[...] += dg
        o_ref[...] = dx_ref[...] + dx

    half = _rows(tm, hw)
    return _call(
        body, name="in_proj_bwd", grid=(s_len // tm,),
        in_specs=[half] * 13 + [_resident(w_in_t.shape), _rows(tm, d), _rows(tm, d), _resident((1, d))],
        out_specs=[_rows(tm, d), _rows(tm, 7 * hw), _acc_out((1, d))],
        out_shape=[_sds((s_len, d), F32), _sds((s_len, 7 * hw), BF16), _sds((1, d), F32)],
    )(dfq, dfk, dfv, dgx, *ddq, *ddk, *ddv, w_in_t, dx1, x, g_pre)


def _expand_w_in(w_in):
    hw = HEAD_WIDTH
    gate = jnp.repeat(w_in[:, 3 * hw:3 * hw + N_HEADS], HEAD_DIM, axis=1)
    return jnp.concatenate([w_in[:, :3 * hw], gate, w_in[:, 3 * hw + N_HEADS:]], axis=1)


def _local_step(x, mem, target, g, w_bf):
    hw = HEAD_WIDTH
    w_in_e = _expand_w_in(w_bf["w_in"])
    b_exp = jnp.repeat(g["b_f"], HEAD_DIM, axis=1)
    buckets = jnp.asarray(_dil_buckets())

    h1, fqkv, gx, dqkv = _in_proj(x, g["g_mix_pre"], w_in_e)
    aq, ak = _gate_scan(gx, b_exp)
    o_fox, lse_fox = _fox_fwd(fqkv, aq, ak)
    bias = _dil_bias(g["rel_bias"], buckets)
    branches = [_dil_fwd(dqkv, bias, p) for p in range(len(DILATIONS))]
    x1, y1, o_dil, lj = _mix_out(o_fox, [b[0] for b in branches], [b[1] for b in branches], w_bf["w_out"], x, g["g_mix_post"])
    hm, kx, vx = _mem_fwd(mem, g["g_mem"], w_bf["w_xk"], w_bf["w_xv"])
    x2, y2, h2, qx, ox = _xattn_fwd(x1, g["g_xattn_pre"], w_bf["w_xq"], kx, vx, w_bf["w_xo"], g["g_xattn_post"])
    h3, a, u, z = _ffn_up(x2, g["g_ffn_pre"], w_bf["w_gate"], w_bf["w_up"])
    y3, dx3, sq = _ffn_down_loss(z, w_bf["w_down"], x2, g["g_ffn_post"], target)

    grads = {}
    dy3, da, du, grads["g_ffn_post"] = _ffn_bwd_a(dx3, y3, g["g_ffn_post"], w_bf["w_down"].T, a, u)
    dx2, grads["g_ffn_pre"] = _ffn_bwd_b(da, du, w_bf["w_gate"].T, w_bf["w_up"].T, dx3, x2, g["g_ffn_pre"])
    grads["w_down"] = _weight_grad(z, dy3, "dw_down")
    grads["w_gate"] = _weight_grad(h3, da, "dw_gate")
    grads["w_up"] = _weight_grad(h3, du, "dw_up")
    dx1, dy2, dqx, dkx, dvx, grads["g_xattn_post"], grads["g_xattn_pre"] = _xattn_bwd(
        dx2, y2, g["g_xattn_post"], w_bf["w_xo"].T, qx, kx, vx, w_bf["w_xq"].T, x1, g["g_xattn_pre"])
    grads["w_xo"] = _weight_grad(ox, dy2, "dw_xo")
    grads["w_xq"] = _weight_grad(h2, dqx, "dw_xq")
    grads["w_xk"], grads["w_xv"], grads["g_mem"] = _mem_bwd(dkx, dvx, w_bf["w_xk"].T, w_bf["w_xv"].T, hm, mem, g["g_mem"])
    dy1, do, delta, grads["g_mix_post"] = _mix_out_bwd(dx1, y1, g["g_mix_post"], w_bf["w_out"].T, o_fox, o_dil)
    grads["w_out"] = jnp.concatenate([_weight_grad(o_fox, dy1, "dw_out_fox"), _weight_grad(o_dil, dy1, "dw_out_dil")], axis=0)
    aql, ad = _fox_bwd_prep(aq, lse_fox, delta)
    dfq, rs = _fox_bwd_dq(fqkv, do, aql, ak, ad)
    dfk, dfv, dc = _fox_bwd_dkv(fqkv, do, aql, ak, ad)
    dgx, db = _gate_bwd(rs, dc, gx, b_exp)
    grads["b_f"] = db[:, ::HEAD_DIM]
    dil = [_dil_bwd(dqkv, do, lj, delta, bias, p) for p in range(len(DILATIONS))]
    grads["rel_bias"] = _rel_bias_grad(jnp.stack([t[3] for t in dil]), buckets)[:, :N_HEADS]
    grad_x, dproj, grads["g_mix_pre"] = _in_proj_bwd(
        dfq, dfk, dfv, dgx, [t[0] for t in dil], [t[1] for t in dil], [t[2] for t in dil], w_in_e.T, dx1, x, g["g_mix_pre"])
    dw_in_e = _weight_grad(h1, dproj, "dw_in")
    grads["w_in"] = jnp.concatenate(
        [dw_in_e[:, :3 * hw], dw_in_e[:, 3 * hw:4 * hw:HEAD_DIM], dw_in_e[:, 4 * hw:]], axis=1)
    return sq, grad_x, grads


MESH = pl.DeviceIdType.MESH


def _my_place():
    return lax.axis_index("x"), lax.axis_index("y"), lax.axis_index("c")


def _all_gather(x, name):
    rows, lanes = x.shape

    def body(x_ref, out_ref, send_sems, recv_sems, local_sem):
        mx, my, mc = _my_place()
        me, sibling = (mx, my, mc), (mx, my, 1 - mc)
        chips = [(1 - mx, my), (mx, 1 - my), (1 - mx, 1 - my)]

        def slot(px, py, pc):
            return out_ref.at[4 * px + 2 * py + pc]

        def copy(k, block, to, src=None):
            return pltpu.make_async_remote_copy(
                src_ref=slot(*block) if src is None else src, dst_ref=slot(*block),
                send_sem=send_sems.at[k], recv_sem=recv_sems.at[k], device_id=to, device_id_type=MESH)

        mine = pltpu.make_async_copy(x_ref, slot(*me), local_sem)
        mine.start()
        first = [copy(0, me, sibling, src=x_ref)]
        first += [copy(1 + j, me, (*chip, mc), src=x_ref) for j, chip in enumerate(chips)]
        for cp in first:
            cp.start()
        passed = [copy(4 + j, (*chip, mc), sibling) for j, chip in enumerate(chips)]
        for j, chip in enumerate(chips):
            copy(1 + j, (*chip, mc), me).wait_recv()
            passed[j].start()
        copy(0, sibling, me).wait_recv()
        for j, chip in enumerate(chips):
            copy(4 + j, (*chip, 1 - mc), me).wait_recv()
        for cp in first + passed:
            cp.wait_send()
        mine.wait()

    return pl.pallas_call(
        body, name=name, out_shape=_sds((N_DEV, rows, lanes), x.dtype),
        in_specs=[pl.BlockSpec(memory_space=pl.ANY)], out_specs=pl.BlockSpec(memory_space=pl.ANY),
        scratch_shapes=[pltpu.SemaphoreType.DMA((N_DEV - 1,)), pltpu.SemaphoreType.DMA((N_DEV - 1,)), pltpu.SemaphoreType.DMA],
    )(x)


def _exchange(g, name):
    _, rows, lanes = g.shape

    def body(g_ref, land_ref, send_sems, recv_sems, local_sem):
        mx, my, mc = _my_place()
        me = 4 * mx + 2 * my + mc
        mine = pltpu.make_async_copy(g_ref.at[me], land_ref.at[me], local_sem)
        mine.start()
        sent, arriving = [], []
        for k in (1, 2, 3, 4, 5, 6, 7):
            px = 1 - mx if k & 4 else mx
            py = 1 - my if k & 2 else my
            pc = 1 - mc if k & 1 else mc
            peer = 4 * px + 2 * py + pc
            cp = pltpu.make_async_remote_copy(
                src_ref=g_ref.at[peer], dst_ref=land_ref.at[me], send_sem=send_sems.at[k - 1], recv_sem=recv_sems.at[k - 1],
                device_id=(px, py, pc), device_id_type=MESH)
            cp.start()
            sent.append(cp)
            arriving.append(pltpu.make_async_remote_copy(
                src_ref=g_ref.at[me], dst_ref=land_ref.at[peer], send_sem=send_sems.at[k - 1], recv_sem=recv_sems.at[k - 1],
                device_id=(px, py, pc), device_id_type=MESH))
        for cp in arriving:
            cp.wait_recv()
        for cp in sent:
            cp.wait_send()
        mine.wait()

    return pl.pallas_call(
        body, name=name, out_shape=_sds(g.shape, g.dtype),
        in_specs=[pl.BlockSpec(memory_space=pl.ANY)], out_specs=pl.BlockSpec(memory_space=pl.ANY),
        scratch_shapes=[pltpu.SemaphoreType.DMA((N_DEV - 1,)), pltpu.SemaphoreType.DMA((N_DEV - 1,)), pltpu.SemaphoreType.DMA],
    )(g)


def _sum_slots(parts, name):
    n, rows, lanes = parts.shape
    tr = 512 if rows % 512 == 0 else rows

    def body(p_ref, o_ref):
        acc = p_ref[0]
        for j in range(1, n):
            acc = acc + p_ref[j]
        o_ref[...] = acc

    return _call(
        body, name=name, grid=(rows // tr,),
        in_specs=[pl.BlockSpec((n, tr, lanes), lambda i: (0, i, 0))], out_specs=_rows(tr, lanes),
        out_shape=_sds((rows, lanes), parts.dtype),
    )(parts)


def _adamw(w, g, m, v, name):
    def body(w_ref, g_ref, m_ref, v_ref, d_ref, nm_ref, nv_ref):
        gv = g_ref[...]
        m_new = ADAM_B1 * m_ref[...] + (1.0 - ADAM_B1) * gv
        v_new = ADAM_B2 * v_ref[...] + (1.0 - ADAM_B2) * (gv * gv)
        nm_ref[...] = m_new
        nv_ref[...] = v_new
        m_hat = m_new / (1.0 - ADAM_B1 ** ADAM_STEP)
        v_hat = v_new / (1.0 - ADAM_B2 ** ADAM_STEP)
        d_ref[...] = -ADAM_LR * (m_hat / (jnp.sqrt(v_hat) + ADAM_EPS) + ADAM_WD * w_ref[...])

    out = _sds(w.shape, F32)
    return pl.pallas_call(
        body, name=name, out_shape=[out, out, out],
        compiler_params=pltpu.CompilerParams(vmem_limit_bytes=V7X_VMEM_LIMIT_BYTES),
    )(w, g, m, v)


def _loss_head(sq, d_model):
    def body(sq_ref, o_ref):
        tot = jnp.sum(jnp.sum(sq_ref[...], axis=1, keepdims=True), axis=0, keepdims=True)
        o_ref[...] = 0.5 * (tot / d_model)

    return pl.pallas_call(body, name="loss_head", out_shape=_sds((1, 1), F32))(sq)


_BIG = (("w_in", 1), ("w_out", 0), ("w_xq", 0), ("w_xk", 0), ("w_xv", 0), ("w_xo", 1), ("w_gate", 1), ("w_up", 1), ("w_down", 0))
_SMALL = ("g_mix_pre", "b_f", "rel_bias", "g_mix_post", "g_xattn_pre", "g_mem", "g_xattn_post", "g_ffn_pre", "g_ffn_post")
LANES = 128
BIG_ROW_ALIGN = 512


def _round_up(n, k):
    return -(-n // k) * k


def _pack_rows(flat_parts, row_align, dtype):
    starts, rows, padded = [], 0, []
    for p in flat_parts:
        n = _round_up(p.shape[0], LANES)
        starts.append(rows)
        rows += n // LANES
        padded.append(jnp.pad(p.astype(dtype), (0, n - p.shape[0])))
    total = _round_up(rows, row_align)
    padded.append(jnp.zeros(((total - rows) * LANES,), dtype))
    return jnp.concatenate(padded).reshape(total, LANES), starts


def _unpack_rows(buf, starts, shapes):
    lead = buf.shape[:-2]
    flat = buf.reshape(lead + (-1,))
    out = []
    for st, shp in zip(starts, shapes):
        n = int(np.prod(shp))
        out.append(flat[..., st * LANES:st * LANES + n].reshape(lead + tuple(shp)))
    return out


def kernel(x, mem, g_mix_pre, w_in, b_f, rel_bias, w_out, g_mix_post, g_xattn_pre, g_mem, w_xq, w_xk, w_xv, w_xo, g_xattn_post, g_ffn_pre, w_gate, w_up, w_down, g_ffn_post, loss_target, m_g_mix_pre, m_w_in, m_b_f, m_rel_bias, m_w_out, m_g_mix_post, m_g_xattn_pre, m_g_mem, m_w_xq, m_w_xk, m_w_xv, m_w_xo, m_g_xattn_post, m_g_ffn_pre, m_w_gate, m_w_up, m_w_down, m_g_ffn_post, v_g_mix_pre, v_w_in, v_b_f, v_rel_bias, v_w_out, v_g_mix_post, v_g_xattn_pre, v_g_mem, v_w_xq, v_w_xk, v_w_xv, v_w_xo, v_g_xattn_post, v_g_ffn_pre, v_w_gate, v_w_up, v_w_down, v_g_ffn_post):
    given = dict(locals())
    order = ("g_mix_pre", "w_in", "b_f", "rel_bias", "w_out", "g_mix_post", "g_xattn_pre", "g_mem", "w_xq", "w_xk", "w_xv",
             "w_xo", "g_xattn_post", "g_ffn_pre", "w_gate", "w_up", "w_down", "g_ffn_post")
    two_d = lambda a: a.reshape(a.shape[-2:])
    w_loc = {n: two_d(given[n]) for n in order}
    m_loc = {n: two_d(given["m_" + n]) for n in order}
    v_loc = {n: two_d(given["v_" + n]) for n in order}
    d_model = x.shape[-1]

    shard_shapes = [w_loc[n].shape for n, _ in _BIG]
    packed, starts = _pack_rows([w_loc[n].reshape(-1) for n, _ in _BIG], BIG_ROW_ALIGN, BF16)
    gathered = _all_gather(packed, "gather_weights")
    w_bf = {}
    for (n, axis), part in zip(_BIG, _unpack_rows(gathered, starts, shard_shapes)):
        r, c = part.shape[1:]
        w_bf[n] = part.reshape(N_DEV * r, c) if axis == 0 else part.transpose(1, 0, 2).reshape(r, N_DEV * c)

    small = {n: w_loc[n] for n in _SMALL}
    sq, grad_x, grads = _local_step(two_d(x), two_d(mem), two_d(loss_target), small, w_bf)

    per_owner = []
    for (n, axis), shp in zip(_BIG, shard_shapes):
        r, c = shp
        gfull = grads[n]
        per_owner.append(gfull.reshape(N_DEV, r * c) if axis == 0 else gfull.reshape(r, N_DEV, c).transpose(1, 0, 2).reshape(N_DEV, r * c))
    rows_big = packed.shape[0]
    slots = []
    for j in range(N_DEV):
        buf, _ = _pack_rows([p[j] for p in per_owner], BIG_ROW_ALIGN, F32)
        slots.append(buf)
    landed = _exchange(jnp.stack(slots), "exchange_grads")
    g_big = dict(zip([n for n, _ in _BIG], _unpack_rows(_sum_slots(landed, "sum_grads"), starts, shard_shapes)))
    assert landed.shape[1] == rows_big

    small_parts = [grads[n].reshape(-1) for n in _SMALL] + [sq.reshape(-1)]
    small_shapes = [w_loc[n].shape for n in _SMALL] + [sq.shape]
    spacked, sstarts = _pack_rows(small_parts, 8, F32)
    ssum = _sum_slots(_all_gather(spacked, "gather_small"), "sum_small")
    g_small = dict(zip(_SMALL, _unpack_rows(ssum, sstarts, small_shapes)[:-1]))
    sq_rows = sq.size // LANES
    loss = _loss_head(ssum[sstarts[-1]:sstarts[-1] + sq_rows], d_model).reshape(())

    g_loc, delta, new_m, new_v = {}, {}, {}, {}
    for n, _ in _BIG:
        g_loc[n] = g_big[n]
        delta[n], new_m[n], new_v[n] = _adamw(w_loc[n], g_big[n], m_loc[n], v_loc[n], "adamw_" + n)
    pk = lambda d: _pack_rows([d[n].reshape(-1) for n in _SMALL], 8, F32)[0]
    pstarts = _pack_rows([w_loc[n].reshape(-1) for n in _SMALL], 8, F32)[1]
    d_s, m_s, v_s = _adamw(pk(w_loc), pk(g_small), pk(m_loc), pk(v_loc), "adamw_small")
    shapes_s = [w_loc[n].shape for n in _SMALL]
    for n, dd, mm, vv in zip(_SMALL, _unpack_rows(d_s, pstarts, shapes_s), _unpack_rows(m_s, pstarts, shapes_s),
                             _unpack_rows(v_s, pstarts, shapes_s)):
        g_loc[n], delta[n], new_m[n], new_v[n] = g_small[n], dd, mm, vv

    like = lambda d: [d[n].reshape(given[n].shape) for n in order]
    return (loss, grad_x.reshape(x.shape), *like(g_loc), *like(delta), *like(new_m), *like(new_v))
```

```python
import functools

import numpy as np
import jax
import jax.numpy as jnp
from jax import lax
from jax.experimental import pallas as pl
from jax.experimental.pallas import tpu as pltpu

F32 = jnp.float32
BF16 = jnp.bfloat16
HIGHEST = lax.Precision.HIGHEST

RMS_EPS = 1e-6
HEAD_DIM = 64
N_HEADS = 8
HEAD_WIDTH = N_HEADS * HEAD_DIM
PAIR = 2 * HEAD_DIM
N_PAIRS = N_HEADS // 2
DIL_BLOCK = 128
DILATIONS = (1, 4, 16)
N_BUCKETS = 32
MAX_DISTANCE = 2048
N_MEM_HEADS = 4
QK_SCALE = HEAD_DIM ** -0.5
NEG = -1e30
N_DEV = 8

ADAM_LR = 0.001
ADAM_B1 = 0.9
ADAM_B2 = 0.999
ADAM_EPS = 1e-08
ADAM_WD = 0.01
ADAM_STEP = 10

V7X_VMEM_LIMIT_BYTES = 56 * 2 ** 20
ROW_TILE = 256
ATT_BLOCK = 512
SCAN_TILE = 256


def _call(body, *, name, grid, in_specs, out_specs, out_shape, scratch=()):
    return pl.pallas_call(
        body, name=name, grid=grid, in_specs=in_specs, out_specs=out_specs, out_shape=out_shape,
        scratch_shapes=list(scratch),
        compiler_params=pltpu.CompilerParams(
            dimension_semantics=("arbitrary",) * len(grid), vmem_limit_bytes=V7X_VMEM_LIMIT_BYTES))


def _rows(tm, n):
    return pl.BlockSpec((tm, n), lambda i: (i, 0))


def _resident(shape):
    zeros = (0,) * len(shape)
    return pl.BlockSpec(shape, lambda i: zeros, pipeline_mode=pl.Buffered(1))


def _acc_out(shape):
    zeros = (0,) * len(shape)
    return pl.BlockSpec(shape, lambda i: zeros)


def _sds(shape, dtype):
    return jax.ShapeDtypeStruct(shape, dtype)


def _dot(a, b):
    return jnp.dot(a, b, preferred_element_type=F32)


def _dot_nt(a, b):
    return lax.dot_general(a, b, (((1,), (1,)), ((), ())), preferred_element_type=F32)


def _dot_tn(a, b):
    return lax.dot_general(a, b, (((0,), (0,)), ((), ())), preferred_element_type=F32)


def _rms_fwd(x, g):
    r = lax.rsqrt(jnp.mean(x * x, axis=-1, keepdims=True) + RMS_EPS)
    return (x * r) * g


def _rms_bwd(xin, g, dy):
    r = lax.rsqrt(jnp.mean(xin * xin, axis=-1, keepdims=True) + RMS_EPS)
    xhat = xin * r
    dg = jnp.sum(dy * xhat, axis=0, keepdims=True)
    dxh = dy * g
    dx = r * (dxh - xhat * jnp.mean(dxh * xhat, axis=-1, keepdims=True))
    return dx, dg


def _first_head_lanes():
    return lax.broadcasted_iota(jnp.int32, (1, PAIR), 1) < HEAD_DIM


def _pick(mask, a, b):
    return jnp.where(mask, a, b)


def _zero_other(mask, v):
    return jnp.where(mask, v, jnp.zeros_like(v))


def _in_proj(x, g, w):
    s_len, d = x.shape
    tm = ROW_TILE
    hw = HEAD_WIDTH

    def body(x_ref, g_ref, w_ref, h_ref, fqkv_ref, gx_ref, dqkv_ref):
        h = _rms_fwd(x_ref[...], g_ref[...]).astype(BF16)
        h_ref[...] = h
        proj = _dot(h, w_ref[...])
        fqkv_ref[:, 0:hw] = (proj[:, 0:hw] * QK_SCALE).astype(BF16)
        fqkv_ref[:, hw:3 * hw] = proj[:, hw:3 * hw].astype(BF16)
        gx_ref[...] = proj[:, 3 * hw:4 * hw]
        dqkv_ref[:, 0:hw] = (proj[:, 4 * hw:5 * hw] * QK_SCALE).astype(BF16)
        dqkv_ref[:, hw:3 * hw] = proj[:, 5 * hw:7 * hw].astype(BF16)

    return _call(
        body, name="in_proj", grid=(s_len // tm,),
        in_specs=[_rows(tm, d), _resident((1, d)), _resident(w.shape)],
        out_specs=[_rows(tm, d), _rows(tm, 3 * hw), _rows(tm, hw), _rows(tm, 3 * hw)],
        out_shape=[_sds((s_len, d), BF16), _sds((s_len, 3 * hw), BF16), _sds((s_len, hw), F32), _sds((s_len, 3 * hw), BF16)],
    )(x, g, w)


def _swap_halves(x):
    return jnp.concatenate([pltpu.roll(x[:, i * PAIR:(i + 1) * PAIR], HEAD_DIM, 1) for i in range(x.shape[1] // PAIR)], axis=1)


def _split3(x):
    hi = x.astype(BF16)
    r = x - hi.astype(F32)
    mid = r.astype(BF16)
    lo = (r - mid.astype(F32)).astype(BF16)
    return hi, mid, lo


def _lane_in_head(width):
    return lax.broadcasted_iota(jnp.int32, (1, width), 1) % HEAD_DIM


def _place3(jj, first, pieces, base):
    out = base
    for i, p in enumerate(pieces):
        out = jnp.where(jj == first + i, p, out)
    return out


def _gate_scan(gx, b_exp):
    s_len, hw = gx.shape
    t = min(SCAN_TILE, s_len)
    tri = jnp.asarray(np.tril(np.ones((t, t), np.float32)))

    def body(gx_ref, b_ref, tri_ref, aq_ref, ak_ref, carry):
        @pl.when(pl.program_id(0) == 0)
        def _():
            carry[...] = jnp.zeros_like(carry)

        z = gx_ref[...] + b_ref[...]
        lf = jnp.minimum(z, 0.0) - jnp.log1p(jnp.exp(-jnp.abs(z)))
        c = jnp.dot(tri_ref[...], lf, precision=HIGHEST, preferred_element_type=F32) + carry[...]
        carry[...] = c[t - 1:t, :]
        hi, mid, lo = _split3(_swap_halves(c))
        jj = _lane_in_head(hw)
        zero = jnp.zeros_like(hi)
        one = jnp.ones_like(hi)
        aq_ref[...] = _place3(jj, 0, (hi, mid, lo), jnp.where(jj < 6, one, zero))
        ak_ref[...] = _place3(jj, 3, (-hi, -mid, -lo), jnp.where(jj < 9, one, zero))

    return _call(
        body, name="gate_scan", grid=(s_len // t,),
        in_specs=[_rows(t, hw), _resident((1, hw)), _resident((t, t))],
        out_specs=[_rows(t, hw), _rows(t, hw)],
        out_shape=[_sds((s_len, hw), BF16), _sds((s_len, hw), BF16)],
        scratch=[pltpu.VMEM((1, hw), F32)],
    )(gx, b_exp, tri)


def _lower_triangle(n):
    return lax.broadcasted_iota(jnp.int32, (n, n), 1) <= lax.broadcasted_iota(jnp.int32, (n, n), 0)


def _on_blocks(qi, kj, step):
    @pl.when(kj < qi)
    def _():
        step(False)

    @pl.when(kj == qi)
    def _():
        step(True)


def _fox_fwd(fqkv, aq, ak):
    s_len = fqkv.shape[0]
    bq = bk = min(ATT_BLOCK, s_len)
    nq, nk = s_len // bq, s_len // bk

    def body(q_ref, k_ref, v_ref, aq_ref, ak_ref, o_ref, lse_ref, qa_ref, m_ref, acc_ref):
        qi, kj = pl.program_id(1), pl.program_id(2)
        in0 = _first_head_lanes()
        not0 = jnp.logical_not(in0)

        @pl.when(kj == 0)
        def _():
            q2, a2 = q_ref[...], aq_ref[...]
            qa_ref[0] = jnp.where(in0, q2, a2)
            qa_ref[1] = jnp.where(in0, a2, q2)
            m_ref[...] = jnp.full_like(m_ref, NEG)
            acc_ref[...] = jnp.zeros_like(acc_ref)

        def step(masked):
            k2, v2, a2 = k_ref[...], v_ref[...], ak_ref[...]
            one = jnp.ones_like(v2)
            for a, mine in ((0, in0), (1, not0)):
                s = _dot_nt(qa_ref[a], jnp.where(mine, k2, a2))
                if masked:
                    s = jnp.where(_lower_triangle(bq), s, NEG)
                m_old = m_ref[a]
                m_new = jnp.maximum(m_old, jnp.max(s, axis=1, keepdims=True))
                p = jnp.exp(s - m_new).astype(BF16)
                acc_ref[a] = jnp.exp(m_old - m_new) * acc_ref[a] + _dot(p, jnp.where(mine, v2, one))
                m_ref[a] = m_new

        _on_blocks(qi, kj, step)

        @pl.when(kj == qi)
        def _():
            acc0, acc1 = acc_ref[0], acc_ref[1]
            l2 = pltpu.roll(_pick(in0, acc1, acc0), HEAD_DIM, 1)
            o_ref[...] = _pick(in0, acc0, acc1) / l2
            lse_ref[...] = _pick(in0, m_ref[0], m_ref[1]) + jnp.log(l2)

    blk = lambda rows: (rows, PAIR)
    qmap = lambda p, i, j: (i, p)
    return pl.pallas_call(
        body, name="fox_fwd", grid=(N_PAIRS, nq, nk),
        in_specs=[
            pl.BlockSpec(blk(bq), qmap),
            pl.BlockSpec(blk(bk), lambda p, i, j: (jnp.minimum(j, i), N_PAIRS + p)),
            pl.BlockSpec(blk(bk), lambda p, i, j: (jnp.minimum(j, i), 2 * N_PAIRS + p)),
            pl.BlockSpec(blk(bq), qmap),
            pl.BlockSpec(blk(bk), lambda p, i, j: (jnp.minimum(j, i), p)),
        ],
        out_specs=[pl.BlockSpec(blk(bq), qmap), pl.BlockSpec(blk(bq), qmap)],
        out_shape=[_sds((s_len, HEAD_WIDTH), F32), _sds((s_len, HEAD_WIDTH), F32)],
        scratch_shapes=[pltpu.VMEM((2, bq, PAIR), BF16), pltpu.VMEM((2, bq, 1), F32), pltpu.VMEM((2, bq, PAIR), F32)],
        compiler_params=pltpu.CompilerParams(dimension_semantics=("arbitrary",) * 3, vmem_limit_bytes=V7X_VMEM_LIMIT_BYTES),
    )(fqkv, fqkv, fqkv, aq, ak)


def _t5_bucket(dist):
    max_exact = N_BUCKETS // 2
    d = np.maximum(dist, 1).astype(np.float32)
    large = max_exact + (np.log(d / max_exact) / np.log(MAX_DISTANCE / max_exact) * (N_BUCKETS - max_exact)).astype(np.int32)
    large = np.minimum(large, N_BUCKETS - 1)
    return np.where(dist < max_exact, dist, large).astype(np.int32)


def _dil_buckets():
    w = DIL_BLOCK
    qi = np.arange(w)[:, None]
    kj = np.arange(2 * w)[None, :]
    sub = qi + w - kj
    band = (sub >= 0) & (sub <= w)
    out = [np.where(band, _t5_bucket(np.clip(sub, 0, w) * dil), -1) for dil in DILATIONS]
    return np.stack(out).astype(np.int32)


def _dil_bias(rel_bias, buckets):
    w = DIL_BLOCK

    def body(rb_ref, bk_ref, o_ref):
        for p in range(len(DILATIONS)):
            bk = bk_ref[p]
            for h in range(N_HEADS):
                def add(b, acc):
                    return acc + jnp.where(bk == b, rb_ref[b, h], 0.0)
                acc = lax.fori_loop(0, N_BUCKETS, add, jnp.zeros((w, 2 * w), F32))
                o_ref[p, h] = jnp.where(bk < 0, NEG, acc)

    return pl.pallas_call(
        body, name="dil_bias",
        in_specs=[pl.BlockSpec(memory_space=pltpu.SMEM), pl.BlockSpec(memory_space=pltpu.VMEM)],
        out_specs=pl.BlockSpec(memory_space=pltpu.VMEM),
        out_shape=_sds((len(DILATIONS), N_HEADS, w, 2 * w), F32),
    )(rel_bias, buckets)


def _dil_fwd(dqkv, bias, branch):
    dil = DILATIONS[branch]
    s_len = dqkv.shape[0]
    w = DIL_BLOCK
    hw = HEAD_WIDTH
    length = s_len // dil
    nb = length // w
    view = dqkv.reshape(length, dil * 3 * hw)

    def body(q_ref, kc_ref, kp_ref, vc_ref, vp_ref, b_ref, o_ref, lse_ref):
        n = pl.program_id(1)
        in0 = _first_head_lanes()
        not0 = jnp.logical_not(in0)
        pairs = [slice(pr * PAIR, (pr + 1) * PAIR) for pr in range(N_PAIRS)]
        tiles = []
        for sl in pairs:
            q2 = q_ref[:, sl]
            qq = jnp.concatenate([_zero_other(in0, q2), _zero_other(not0, q2)], axis=0)
            tiles.append(jnp.concatenate([_dot_nt(qq, kp_ref[:, sl]), _dot_nt(qq, kc_ref[:, sl])], axis=1))
        s = jnp.concatenate(tiles, axis=0) + b_ref[...].reshape(N_HEADS * w, 2 * w)
        prev_half = lax.broadcasted_iota(jnp.int32, (1, 2 * w), 1) < w
        s = jnp.where(jnp.logical_and(n == 0, prev_half), NEG, s)
        m = jnp.max(s, axis=1, keepdims=True)
        e = jnp.exp(s - m)
        l = jnp.sum(e, axis=1, keepdims=True)
        p = (e / l).astype(BF16)
        lse = m + jnp.log(l)
        for pr, sl in enumerate(pairs):
            pp = p[2 * pr * w:(2 * pr + 2) * w]
            o2 = _dot(pp[:, :w], vp_ref[:, sl]) + _dot(pp[:, w:], vc_ref[:, sl])
            o_ref[:, sl] = _pick(in0, o2[:w], o2[w:])
            lse_ref[:, sl] = _pick(in0, lse[2 * pr * w:(2 * pr + 1) * w], lse[(2 * pr + 1) * w:(2 * pr + 2) * w])

    prev = lambda n: jnp.maximum(n - 1, 0)
    out = pl.pallas_call(
        body, name=f"dil_fwd_{dil}", grid=(dil, nb),
        in_specs=[
            pl.BlockSpec((w, hw), lambda r, n: (n, 3 * r)),
            pl.BlockSpec((w, hw), lambda r, n: (n, 3 * r + 1)),
            pl.BlockSpec((w, hw), lambda r, n: (prev(n), 3 * r + 1)),
            pl.BlockSpec((w, hw), lambda r, n: (n, 3 * r + 2)),
            pl.BlockSpec((w, hw), lambda r, n: (prev(n), 3 * r + 2)),
            pl.BlockSpec((None, N_HEADS, w, 2 * w), lambda r, n: (branch, 0, 0, 0)),
        ],
        out_specs=[pl.BlockSpec((w, hw), lambda r, n: (n, r)), pl.BlockSpec((w, hw), lambda r, n: (n, r))],
        out_shape=[_sds((length, dil * hw), F32), _sds((length, dil * hw), F32)],
        compiler_params=pltpu.CompilerParams(dimension_semantics=("arbitrary",) * 2, vmem_limit_bytes=V7X_VMEM_LIMIT_BYTES),
    )(view, view, view, view, view, bias)
    return out[0].reshape(s_len, hw), out[1].reshape(s_len, hw)


def _mix_out(o_fox, o_br, lse_br, w_out, x, g_post):
    s_len, d = x.shape
    hw = HEAD_WIDTH
    tm = ROW_TILE

    def body(of_ref, o1, o2, o3, l1, l2, l3, w_ref, x_ref, g_ref, x1_ref, y1_ref, od_ref, lj_ref):
        la, lb, lc = l1[...], l2[...], l3[...]
        m = jnp.maximum(jnp.maximum(la, lb), lc)
        ea, eb, ec = jnp.exp(la - m), jnp.exp(lb - m), jnp.exp(lc - m)
        tot = ea + eb + ec
        o_dil = (ea / tot) * o1[...] + (eb / tot) * o2[...] + (ec / tot) * o3[...]
        od_ref[...] = o_dil
        lj_ref[...] = m + jnp.log(tot)
        y = _dot(of_ref[...].astype(BF16), w_ref[0:hw, :]) + _dot(o_dil.astype(BF16), w_ref[hw:2 * hw, :])
        y1_ref[...] = y
        x1_ref[...] = x_ref[...] + _rms_fwd(y, g_ref[...])

    half = _rows(tm, hw)
    return _call(
        body, name="mix_out", grid=(s_len // tm,),
        in_specs=[half] * 7 + [_resident(w_out.shape), _rows(tm, d), _resident((1, d))],
        out_specs=[_rows(tm, d), _rows(tm, d), half, half],
        out_shape=[_sds((s_len, d), F32), _sds((s_len, d), F32), _sds((s_len, hw), F32), _sds((s_len, hw), F32)],
    )(o_fox, *o_br, *lse_br, w_out, x, g_post)


def _mem_fwd(mem, g_mem, w_xk, w_xv):
    n_mem, d = mem.shape
    mw = w_xk.shape[1]

    def body(mem_ref, g_ref, wk_ref, wv_ref, hm_ref, k_ref, v_ref):
        hm = _rms_fwd(mem_ref[...], g_ref[...]).astype(BF16)
        hm_ref[...] = hm
        k_ref[...] = _dot(hm, wk_ref[...]).astype(BF16)
        v_ref[...] = _dot(hm, wv_ref[...]).astype(BF16)

    return pl.pallas_call(
        body, name="mem_fwd",
        out_shape=[_sds((n_mem, d), BF16), _sds((n_mem, mw), BF16), _sds((n_mem, mw), BF16)],
    )(mem, g_mem, w_xk, w_xv)


def _xattn_softmax(qa, k2):
    s = _dot_nt(qa, k2)
    m = jnp.max(s, axis=1, keepdims=True)
    e = jnp.exp(s - m)
    return e / jnp.sum(e, axis=1, keepdims=True)


def _xattn_fwd(x1, g_pre, w_xq, kx, vx, w_xo, g_post):
    s_len, d = x1.shape
    mw = w_xq.shape[1]
    n_mem = kx.shape[0]
    tm = ROW_TILE

    def body(x_ref, gp_ref, wq_ref, k_ref, v_ref, wo_ref, go_ref, x2_ref, y2_ref, h2_ref, q_ref, o_ref):
        x = x_ref[...]
        h = _rms_fwd(x, gp_ref[...]).astype(BF16)
        h2_ref[...] = h
        q = (_dot(h, wq_ref[...]) * QK_SCALE).astype(BF16)
        q_ref[...] = q
        in0 = _first_head_lanes()
        not0 = jnp.logical_not(in0)
        for pr in range(mw // PAIR):
            sl = slice(pr * PAIR, (pr + 1) * PAIR)
            q2, k2, v2 = q[:, sl], k_ref[:, sl], v_ref[:, sl]
            oa = [_dot(_xattn_softmax(_zero_other(mine, q2), k2).astype(BF16), v2) for mine in (in0, not0)]
            o_ref[:, sl] = _pick(in0, oa[0], oa[1]).astype(BF16)
        y = _dot(o_ref[...], wo_ref[...])
        y2_ref[...] = y
        x2_ref[...] = x + _rms_fwd(y, go_ref[...])

    return _call(
        body, name="xattn_fwd", grid=(s_len // tm,),
        in_specs=[_rows(tm, d), _resident((1, d)), _resident(w_xq.shape), _resident((n_mem, mw)), _resident((n_mem, mw)),
                  _resident(w_xo.shape), _resident((1, d))],
        out_specs=[_rows(tm, d), _rows(tm, d), _rows(tm, d), _rows(tm, mw), _rows(tm, mw)],
        out_shape=[_sds((s_len, d), F32), _sds((s_len, d), F32), _sds((s_len, d), BF16), _sds((s_len, mw), BF16),
                   _sds((s_len, mw), BF16)],
    )(x1, g_pre, w_xq, kx, vx, w_xo, g_post)


def _ffn_up(x2, g_pre, w_gate, w_up):
    s_len, d = x2.shape
    dff = w_gate.shape[1]
    tm = ROW_TILE

    def body(x_ref, g_ref, wg_ref, wu_ref, h_ref, a_ref, u_ref, z_ref):
        h = _rms_fwd(x_ref[...], g_ref[...]).astype(BF16)
        h_ref[...] = h
        a = _dot(h, wg_ref[...])
        u = _dot(h, wu_ref[...])
        a_ref[...] = a.astype(BF16)
        u_ref[...] = u.astype(BF16)
        z_ref[...] = ((a * jax.nn.sigmoid(a)) * u).astype(BF16)

    return _call(
        body, name="ffn_up", grid=(s_len // tm,),
        in_specs=[_rows(tm, d), _resident((1, d)), _resident(w_gate.shape), _resident(w_up.shape)],
        out_specs=[_rows(tm, d), _rows(tm, dff), _rows(tm, dff), _rows(tm, dff)],
        out_shape=[_sds((s_len, d), BF16)] + [_sds((s_len, dff), BF16)] * 3,
    )(x2, g_pre, w_gate, w_up)


def _ffn_down_loss(z, w_down, x2, g_post, target):
    s_len, d = x2.shape
    dff = z.shape[1]
    tm = ROW_TILE

    def body(z_ref, w_ref, x_ref, g_ref, t_ref, y_ref, dx_ref, sq_ref):
        @pl.when(pl.program_id(0) == 0)
        def _():
            sq_ref[...] = jnp.zeros_like(sq_ref)

        y = _dot(z_ref[...], w_ref[...])
        y_ref[...] = y
        err = (x_ref[...] + _rms_fwd(y, g_ref[...])) - t_ref[...]
        sq_ref[...] += jnp.sum(err * err, axis=0, keepdims=True)
        dx_ref[...] = err * (1.0 / d)

    return _call(
        body, name="ffn_down_loss", grid=(s_len // tm,),
        in_specs=[_rows(tm, dff), _resident(w_down.shape), _rows(tm, d), _resident((1, d)), _rows(tm, d)],
        out_specs=[_rows(tm, d), _rows(tm, d), _acc_out((1, d))],
        out_shape=[_sds((s_len, d), F32), _sds((s_len, d), F32), _sds((1, d), F32)],
    )(z, w_down, x2, g_post, target)


def _weight_grad(a, b, name):
    s_len, k = a.shape
    n = b.shape[1]
    ts = 512 if s_len % 512 == 0 else s_len
    tn = n
    while k * tn * 4 > 8 * 2 ** 20 and tn % 256 == 0:
        tn //= 2

    def body(a_ref, b_ref, o_ref):
        @pl.when(pl.program_id(1) == 0)
        def _():
            o_ref[...] = jnp.zeros_like(o_ref)

        o_ref[...] += _dot_tn(a_ref[...].astype(BF16), b_ref[...].astype(BF16))

    return pl.pallas_call(
        body, name=name, grid=(n // tn, s_len // ts),
        in_specs=[pl.BlockSpec((ts, k), lambda j, i: (i, 0)), pl.BlockSpec((ts, tn), lambda j, i: (i, j))],
        out_specs=pl.BlockSpec((k, tn), lambda j, i: (0, j)),
        out_shape=_sds((k, n), F32),
        compiler_params=pltpu.CompilerParams(dimension_semantics=("arbitrary",) * 2, vmem_limit_bytes=V7X_VMEM_LIMIT_BYTES),
    )(a, b)


def _ffn_bwd_a(dx3, y3, g_post, w_down_t, a, u):
    s_len, d = dx3.shape
    dff = a.shape[1]
    tm = ROW_TILE

    def body(dx_ref, y_ref, g_ref, w_ref, a_ref, u_ref, dy_ref, da_ref, du_ref, dg_ref):
        @pl.when(pl.program_id(0) == 0)
        def _():
            dg_ref[...] = jnp.zeros_like(dg_ref)

        dy, dg = _rms_bwd(y_ref[...], g_ref[...], dx_ref[...])
        dg_ref[...] += dg
        dyb = dy.astype(BF16)
        dy_ref[...] = dyb
        dz = _dot(dyb, w_ref[...])
        av = a_ref[...].astype(F32)
        uv = u_ref[...].astype(F32)
        sg = jax.nn.sigmoid(av)
        da_ref[...] = (dz * uv * (sg * (1.0 + av * (1.0 - sg)))).astype(BF16)
        du_ref[...] = (dz * (av * sg)).astype(BF16)

    return _call(
        body, name="ffn_bwd_a", grid=(s_len // tm,),
        in_specs=[_rows(tm, d), _rows(tm, d), _resident((1, d)), _resident(w_down_t.shape), _rows(tm, dff), _rows(tm, dff)],
        out_specs=[_rows(tm, d), _rows(tm, dff), _rows(tm, dff), _acc_out((1, d))],
        out_shape=[_sds((s_len, d), BF16), _sds((s_len, dff), BF16), _sds((s_len, dff), BF16), _sds((1, d), F32)],
    )(dx3, y3, g_post, w_down_t, a, u)


def _ffn_bwd_b(da, du, w_gate_t, w_up_t, dx3, x2, g_pre):
    s_len, d = x2.shape
    dff = da.shape[1]
    tm = ROW_TILE

    def body(da_ref, du_ref, wg_ref, wu_ref, dx_ref, x_ref, g_ref, o_ref, dg_ref):
        @pl.when(pl.program_id(0) == 0)
        def _():
            dg_ref[...] = jnp.zeros_like(dg_ref)

        dh = _dot(da_ref[...], wg_ref[...]) + _dot(du_ref[...], wu_ref[...])
        dx, dg = _rms_bwd(x_ref[...], g_ref[...], dh)
        dg_ref[...] += dg
        o_ref[...] = dx_ref[...] + dx

    return _call(
        body, name="ffn_bwd_b", grid=(s_len // tm,),
        in_specs=[_rows(tm, dff), _rows(tm, dff), _resident(w_gate_t.shape), _resident(w_up_t.shape), _rows(tm, d),
                  _rows(tm, d), _resident((1, d))],
        out_specs=[_rows(tm, d), _acc_out((1, d))],
        out_shape=[_sds((s_len, d), F32), _sds((1, d), F32)],
    )(da, du, w_gate_t, w_up_t, dx3, x2, g_pre)


def _xattn_bwd(dx2, y2, g_post, w_xo_t, q, kx, vx, w_xq_t, x1, g_pre):
    s_len, d = x1.shape
    mw = q.shape[1]
    n_mem = kx.shape[0]
    tm = ROW_TILE

    def body(dx_ref, y_ref, go_ref, wo_ref, q_ref, k_ref, v_ref, wq_ref, x_ref, gp_ref,
             dx1_ref, dy_ref, dq_ref, dk_ref, dv_ref, dgo_ref, dgp_ref):
        @pl.when(pl.program_id(0) == 0)
        def _():
            dk_ref[...] = jnp.zeros_like(dk_ref)
            dv_ref[...] = jnp.zeros_like(dv_ref)
            dgo_ref[...] = jnp.zeros_like(dgo_ref)
            dgp_ref[...] = jnp.zeros_like(dgp_ref)

        dxin = dx_ref[...]
        dy, dgo = _rms_bwd(y_ref[...], go_ref[...], dxin)
        dgo_ref[...] += dgo
        dyb = dy.astype(BF16)
        dy_ref[...] = dyb
        do = _dot(dyb, wo_ref[...]).astype(BF16)
        in0 = _first_head_lanes()
        not0 = jnp.logical_not(in0)
        for pr in range(mw // PAIR):
            sl = slice(pr * PAIR, (pr + 1) * PAIR)
            q2, k2, v2, do2 = q_ref[:, sl], k_ref[:, sl], v_ref[:, sl], do[:, sl]
            dqs = []
            dk2 = jnp.zeros((n_mem, PAIR), F32)
            dv2 = jnp.zeros((n_mem, PAIR), F32)
            for mine in (in0, not0):
                qa = _zero_other(mine, q2)
                doa = _zero_other(mine, do2)
                p = _xattn_softmax(qa, k2)
                dp = _dot_nt(doa, v2)
                ds = (p * (dp - jnp.sum(p * dp, axis=1, keepdims=True))).astype(BF16)
                dqs.append(_dot(ds, k2))
                dk2 = dk2 + _dot_tn(ds, qa)
                dv2 = dv2 + _dot_tn(p.astype(BF16), doa)
            dq_ref[:, sl] = (_pick(in0, dqs[0], dqs[1]) * QK_SCALE).astype(BF16)
            dk_ref[:, sl] += dk2
            dv_ref[:, sl] += dv2
        dh = _dot(dq_ref[...], wq_ref[...])
        dx, dgp = _rms_bwd(x_ref[...], gp_ref[...], dh)
        dgp_ref[...] += dgp
        dx1_ref[...] = dxin + dx

    return _call(
        body, name="xattn_bwd", grid=(s_len // tm,),
        in_specs=[_rows(tm, d), _rows(tm, d), _resident((1, d)), _resident(w_xo_t.shape), _rows(tm, mw),
                  _resident((n_mem, mw)), _resident((n_mem, mw)), _resident(w_xq_t.shape), _rows(tm, d), _resident((1, d))],
        out_specs=[_rows(tm, d), _rows(tm, d), _rows(tm, mw), _acc_out((n_mem, mw)), _acc_out((n_mem, mw)),
                   _acc_out((1, d)), _acc_out((1, d))],
        out_shape=[_sds((s_len, d), F32), _sds((s_len, d), BF16), _sds((s_len, mw), BF16), _sds((n_mem, mw), F32),
                   _sds((n_mem, mw), F32), _sds((1, d), F32), _sds((1, d), F32)],
    )(dx2, y2, g_post, w_xo_t, q, kx, vx, w_xq_t, x1, g_pre)


def _mem_bwd(dk, dv, w_xk_t, w_xv_t, hm, mem, g_mem):
    n_mem, d = mem.shape
    mw = dk.shape[1]

    def body(dk_ref, dv_ref, wk_ref, wv_ref, hm_ref, mem_ref, g_ref, dwk_ref, dwv_ref, dg_ref):
        dkb = dk_ref[...].astype(BF16)
        dvb = dv_ref[...].astype(BF16)
        dhm = _dot(dkb, wk_ref[...]) + _dot(dvb, wv_ref[...])
        _, dg = _rms_bwd(mem_ref[...], g_ref[...], dhm)
        dg_ref[...] = dg
        dwk_ref[...] = _dot_tn(hm_ref[...], dkb)
        dwv_ref[...] = _dot_tn(hm_ref[...], dvb)

    return pl.pallas_call(
        body, name="mem_bwd",
        out_shape=[_sds((d, mw), F32), _sds((d, mw), F32), _sds((1, d), F32)],
    )(dk, dv, w_xk_t, w_xv_t, hm, mem, g_mem)


def _mix_out_bwd(dx1, y1, g_post, w_out_t, o_fox, o_dil):
    s_len, d = dx1.shape
    hw = HEAD_WIDTH
    tm = ROW_TILE
    head_of = np.arange(hw) // HEAD_DIM
    ones = jnp.asarray((head_of[:, None] == head_of[None, :]).astype(np.float32))

    def body(dx_ref, y_ref, g_ref, w_ref, of_ref, od_ref, ones_ref, dy_ref, dof_ref, dod_ref, dlf_ref, dld_ref, dg_ref):
        @pl.when(pl.program_id(0) == 0)
        def _():
            dg_ref[...] = jnp.zeros_like(dg_ref)

        dy, dg = _rms_bwd(y_ref[...], g_ref[...], dx_ref[...])
        dg_ref[...] += dg
        dyb = dy.astype(BF16)
        dy_ref[...] = dyb
        do = _dot(dyb, w_ref[...])
        dof_ref[...] = do[:, 0:hw].astype(BF16)
        dod_ref[...] = do[:, hw:2 * hw].astype(BF16)
        dlf_ref[...] = jnp.dot(do[:, 0:hw] * of_ref[...], ones_ref[...], precision=HIGHEST, preferred_element_type=F32)
        dld_ref[...] = jnp.dot(do[:, hw:2 * hw] * od_ref[...], ones_ref[...], precision=HIGHEST, preferred_element_type=F32)

    half = _rows(tm, hw)
    return _call(
        body, name="mix_out_bwd", grid=(s_len // tm,),
        in_specs=[_rows(tm, d), _rows(tm, d), _resident((1, d)), _resident(w_out_t.shape), half, half, _resident((hw, hw))],
        out_specs=[_rows(tm, d), half, half, half, half, _acc_out((1, d))],
        out_shape=[_sds((s_len, d), BF16), _sds((s_len, hw), BF16), _sds((s_len, hw), BF16), _sds((s_len, hw), F32),
                   _sds((s_len, hw), F32), _sds((1, d), F32)],
    )(dx1, y1, g_post, w_out_t, o_fox, o_dil, ones)


def _fox_bwd_prep(aq, lse, delta):
    s_len, hw = aq.shape
    tm = ROW_TILE

    def body(aq_ref, lse_ref, dl_ref, aql_ref, ad_ref):
        jj = _lane_in_head(hw)
        l3 = _split3(_swap_halves(lse_ref[...]))
        aql_ref[...] = _place3(jj, 6, [-p for p in l3], aq_ref[...])
        d3 = _split3(_swap_halves(dl_ref[...]))
        ad_ref[...] = _place3(jj, 0, [-p for p in d3], jnp.zeros((tm, hw), BF16))

    half = _rows(tm, hw)
    return _call(
        body, name="fox_bwd_prep", grid=(s_len // tm,),
        in_specs=[half, half, half], out_specs=[half, half],
        out_shape=[_sds((s_len, hw), BF16), _sds((s_len, hw), BF16)],
    )(aq, lse, delta)


def _ones_on_first3(shape):
    jj = lax.broadcasted_iota(jnp.int32, shape, 1) % HEAD_DIM
    return jnp.where(jj < 3, 1.0, 0.0).astype(BF16)


def _fox_bwd_dq(fqkv, do, aql, ak, ad):
    s_len = fqkv.shape[0]
    bq = bk = min(ATT_BLOCK, s_len)
    nq, nk = s_len // bq, s_len // bk

    def body(q_ref, k_ref, v_ref, do_ref, aql_ref, ak_ref, ad_ref, dq_ref, rs_ref, qa_ref, doa_ref, acc_ref):
        qi, kj = pl.program_id(1), pl.program_id(2)
        in0 = _first_head_lanes()
        not0 = jnp.logical_not(in0)

        @pl.when(kj == 0)
        def _():
            q2, do2, a2, d2 = q_ref[...], do_ref[...], aql_ref[...], ad_ref[...]
            qa_ref[0] = jnp.where(in0, q2, a2)
            qa_ref[1] = jnp.where(in0, a2, q2)
            doa_ref[0] = jnp.where(in0, do2, d2)
            doa_ref[1] = jnp.where(in0, d2, do2)
            acc_ref[...] = jnp.zeros_like(acc_ref)

        def step(masked):
            k2, v2, a2 = k_ref[...], v_ref[...], ak_ref[...]
            one = jnp.ones_like(k2)
            one3 = _ones_on_first3(v2.shape)
            for a, mine in ((0, in0), (1, not0)):
                s = _dot_nt(qa_ref[a], jnp.where(mine, k2, a2))
                if masked:
                    s = jnp.where(_lower_triangle(bq), s, NEG)
                dp = _dot_nt(doa_ref[a], jnp.where(mine, v2, one3))
                ds = (jnp.exp(s) * dp).astype(BF16)
                acc_ref[a] += _dot(ds, jnp.where(mine, k2, one))

        _on_blocks(qi, kj, step)

        @pl.when(kj == qi)
        def _():
            dq_ref[...] = (_pick(in0, acc_ref[0], acc_ref[1]) * QK_SCALE).astype(BF16)
            rs_ref[...] = pltpu.roll(_pick(in0, acc_ref[1], acc_ref[0]), HEAD_DIM, 1)

    blk = lambda rows: (rows, PAIR)
    qmap = lambda p, i, j: (i, p)
    return pl.pallas_call(
        body, name="fox_bwd_dq", grid=(N_PAIRS, nq, nk),
        in_specs=[
            pl.BlockSpec(blk(bq), qmap),
            pl.BlockSpec(blk(bk), lambda p, i, j: (jnp.minimum(j, i), N_PAIRS + p)),
            pl.BlockSpec(blk(bk), lambda p, i, j: (jnp.minimum(j, i), 2 * N_PAIRS + p)),
            pl.BlockSpec(blk(bq), qmap), pl.BlockSpec(blk(bq), qmap),
            pl.BlockSpec(blk(bk), lambda p, i, j: (jnp.minimum(j, i), p)),
            pl.BlockSpec(blk(bq), qmap),
        ],
        out_specs=[pl.BlockSpec(blk(bq), qmap), pl.BlockSpec(blk(bq), qmap)],
        out_shape=[_sds((s_len, HEAD_WIDTH), BF16), _sds((s_len, HEAD_WIDTH), F32)],
        scratch_shapes=[pltpu.VMEM((2, bq, PAIR), BF16), pltpu.VMEM((2, bq, PAIR), BF16), pltpu.VMEM((2, bq, PAIR), F32)],
        compiler_params=pltpu.CompilerParams(dimension_semantics=("arbitrary",) * 3, vmem_limit_bytes=V7X_VMEM_LIMIT_BYTES),
    )(fqkv, fqkv, fqkv, do, aql, ak, ad)


def _fox_bwd_dkv(fqkv, do, aql, ak, ad):
    s_len = fqkv.shape[0]
    bq = bk = min(ATT_BLOCK, s_len)
    nq, nk = s_len // bq, s_len // bk

    def body(q_ref, k_ref, v_ref, do_ref, aql_ref, ak_ref, ad_ref, dk_ref, dv_ref, dc_ref, ka_ref, va_ref, r_ref, dvacc_ref):
        kj, qi = pl.program_id(1), pl.program_id(2)
        in0 = _first_head_lanes()
        not0 = jnp.logical_not(in0)

        @pl.when(qi == 0)
        def _():
            k2, v2, a2 = k_ref[...], v_ref[...], ak_ref[...]
            one3 = _ones_on_first3(v2.shape)
            ka_ref[0] = jnp.where(in0, k2, a2)
            ka_ref[1] = jnp.where(in0, a2, k2)
            va_ref[0] = jnp.where(in0, v2, one3)
            va_ref[1] = jnp.where(in0, one3, v2)
            r_ref[...] = jnp.zeros_like(r_ref)
            dvacc_ref[...] = jnp.zeros_like(dvacc_ref)

        def step(masked):
            q2, do2, a2, d2 = q_ref[...], do_ref[...], aql_ref[...], ad_ref[...]
            one = jnp.ones_like(q2)
            for a, mine in ((0, in0), (1, not0)):
                doa = jnp.where(mine, do2, d2)
                s = _dot_nt(jnp.where(mine, q2, a2), ka_ref[a])
                if masked:
                    s = jnp.where(_lower_triangle(bq), s, NEG)
                p = jnp.exp(s)
                ds = (p * _dot_nt(doa, va_ref[a])).astype(BF16)
                dvacc_ref[a] += _dot_tn(p.astype(BF16), doa)
                r_ref[a] += _dot_tn(ds, jnp.where(mine, q2, one))

        _on_blocks(qi, kj, step)

        @pl.when(qi == nq - 1)
        def _():
            dk_ref[...] = _pick(in0, r_ref[0], r_ref[1]).astype(BF16)
            dv_ref[...] = _pick(in0, dvacc_ref[0], dvacc_ref[1]).astype(BF16)
            dc_ref[...] = -pltpu.roll(_pick(in0, r_ref[1], r_ref[0]), HEAD_DIM, 1)

    blk = lambda rows: (rows, PAIR)
    qmap = lambda p, j, i: (jnp.maximum(i, j), p)
    kvout = lambda p, j, i: (j, p)
    return pl.pallas_call(
        body, name="fox_bwd_dkv", grid=(N_PAIRS, nk, nq),
        in_specs=[
            pl.BlockSpec(blk(bq), qmap),
            pl.BlockSpec(blk(bk), lambda p, j, i: (j, N_PAIRS + p)),
            pl.BlockSpec(blk(bk), lambda p, j, i: (j, 2 * N_PAIRS + p)),
            pl.BlockSpec(blk(bq), qmap), pl.BlockSpec(blk(bq), qmap),
            pl.BlockSpec(blk(bk), kvout),
            pl.BlockSpec(blk(bq), qmap),
        ],
        out_specs=[pl.BlockSpec(blk(bk), kvout)] * 3,
        out_shape=[_sds((s_len, HEAD_WIDTH), BF16), _sds((s_len, HEAD_WIDTH), BF16), _sds((s_len, HEAD_WIDTH), F32)],
        scratch_shapes=[pltpu.VMEM((2, bk, PAIR), BF16), pltpu.VMEM((2, bk, PAIR), BF16), pltpu.VMEM((2, bk, PAIR), F32),
                        pltpu.VMEM((2, bk, PAIR), F32)],
        compiler_params=pltpu.CompilerParams(dimension_semantics=("arbitrary",) * 3, vmem_limit_bytes=V7X_VMEM_LIMIT_BYTES),
    )(fqkv, fqkv, fqkv, do, aql, ak, ad)


def _gate_bwd(rs, dc, gx, b_exp):
    s_len, hw = gx.shape
    t = min(SCAN_TILE, s_len)
    nt = s_len // t
    tri = jnp.asarray(np.triu(np.ones((t, t), np.float32)))

    def body(rs_ref, dc_ref, gx_ref, b_ref, tri_ref, dgx_ref, db_ref, carry):
        @pl.when(pl.program_id(0) == 0)
        def _():
            carry[...] = jnp.zeros_like(carry)
            db_ref[...] = jnp.zeros_like(db_ref)

        dlf = jnp.dot(tri_ref[...], rs_ref[...] + dc_ref[...], precision=HIGHEST, preferred_element_type=F32) + carry[...]
        carry[...] = dlf[0:1, :]
        dgate = dlf * jax.nn.sigmoid(-(gx_ref[...] + b_ref[...]))
        db_ref[...] += jnp.sum(dgate, axis=0, keepdims=True)
        lane = lax.broadcasted_iota(jnp.int32, (1, hw), 1)
        dgx_ref[...] = jnp.where(lane % HEAD_DIM == 0, dgate, 0.0).astype(BF16)

    rev = lambda i: (nt - 1 - i, 0)
    return _call(
        body, name="gate_bwd", grid=(nt,),
        in_specs=[pl.BlockSpec((t, hw), rev)] * 3 + [_resident((1, hw)), _resident((t, t))],
        out_specs=[pl.BlockSpec((t, hw), rev), _acc_out((1, hw))],
        out_shape=[_sds((s_len, hw), BF16), _sds((1, hw), F32)],
        scratch=[pltpu.VMEM((1, hw), F32)],
    )(rs, dc, gx, b_exp, tri)


def _dil_bwd(dqkv, do, lj, delta, bias, branch):
    dil = DILATIONS[branch]
    s_len = dqkv.shape[0]
    w = DIL_BLOCK
    hw = HEAD_WIDTH
    length = s_len // dil
    nb = length // w
    qkv_v = dqkv.reshape(length, dil * 3 * hw)
    do_v = do.reshape(length, dil * hw)
    dl_v = delta.reshape(length, dil * hw)
    lj_v = lj.reshape(length, dil * hw)

    def body(q0_ref, q1_ref, kp_ref, kc_ref, vp_ref, vc_ref, do0_ref, do1_ref, l0_ref, l1_ref, d0_ref, d1_ref, b_ref,
             dq_ref, dk_ref, dv_ref, dsum_ref):
        r, n = pl.program_id(0), pl.program_id(1)
        in0 = _first_head_lanes()
        not0 = jnp.logical_not(in0)
        first = n == 0
        last = n == nb - 1

        @pl.when(jnp.logical_and(r == 0, n == 0))
        def _():
            dsum_ref[...] = jnp.zeros_like(dsum_ref)

        pairs = [slice(pr * PAIR, (pr + 1) * PAIR) for pr in range(N_PAIRS)]

        def both_heads(ref, sl):
            v = ref[:, sl]
            return jnp.concatenate([_zero_other(in0, v), _zero_other(not0, v)], axis=0)

        def head_columns(ref):
            return jnp.concatenate([ref[:, h * HEAD_DIM:h * HEAD_DIM + 1] for h in range(N_HEADS)], axis=0)

        qq0 = [both_heads(q0_ref, sl) for sl in pairs]
        qq1 = [both_heads(q1_ref, sl) for sl in pairs]
        dd0 = [both_heads(do0_ref, sl) for sl in pairs]
        dd1 = [both_heads(do1_ref, sl) for sl in pairs]
        stack = lambda tiles: jnp.concatenate(tiles, axis=0)
        s_a = stack([_dot_nt(qq0[i], kp_ref[:, sl]) for i, sl in enumerate(pairs)])
        s_b = stack([_dot_nt(qq0[i], kc_ref[:, sl]) for i, sl in enumerate(pairs)])
        s_c = stack([_dot_nt(qq1[i], kc_ref[:, sl]) for i, sl in enumerate(pairs)])
        dp_a = stack([_dot_nt(dd0[i], vp_ref[:, sl]) for i, sl in enumerate(pairs)])
        dp_b = stack([_dot_nt(dd0[i], vc_ref[:, sl]) for i, sl in enumerate(pairs)])
        dp_c = stack([_dot_nt(dd1[i], vc_ref[:, sl]) for i, sl in enumerate(pairs)])
        bias2 = b_ref[...].reshape(N_HEADS * w, 2 * w)
        b_prev, b_cur = bias2[:, 0:w], bias2[:, w:2 * w]
        lse0, lse1 = head_columns(l0_ref), head_columns(l1_ref)
        dl0, dl1 = head_columns(d0_ref), head_columns(d1_ref)
        p_a = jnp.exp(jnp.where(first, NEG, s_a + b_prev) - lse0)
        p_b = jnp.exp((s_b + b_cur) - lse0)
        p_c = jnp.exp(jnp.where(last, NEG, s_c + b_prev) - lse1)
        ds_a = p_a * (dp_a - dl0)
        ds_b = p_b * (dp_b - dl0)
        ds_c = p_c * (dp_c - dl1)
        dsum_ref[...] += jnp.concatenate([ds_a, ds_b], axis=1).reshape(N_HEADS, w, 2 * w)
        ds_a, ds_b, ds_c = ds_a.astype(BF16), ds_b.astype(BF16), ds_c.astype(BF16)
        p_b, p_c = p_b.astype(BF16), p_c.astype(BF16)
        for i, sl in enumerate(pairs):
            rows = slice(2 * i * w, (2 * i + 2) * w)
            dq2 = _dot(ds_a[rows], kp_ref[:, sl]) + _dot(ds_b[rows], kc_ref[:, sl])
            dq_ref[:, sl] = _pick(in0, dq2[:w], dq2[w:])
            dk_ref[:, sl] = _dot_tn(ds_b[rows], qq0[i]) + _dot_tn(ds_c[rows], qq1[i])
            dv_ref[:, sl] = _dot_tn(p_b[rows], dd0[i]) + _dot_tn(p_c[rows], dd1[i])

    prev = lambda n: jnp.maximum(n - 1, 0)
    nxt = lambda n: jnp.minimum(n + 1, nb - 1)
    blk = (w, hw)
    outs = pl.pallas_call(
        body, name=f"dil_bwd_{dil}", grid=(dil, nb),
        in_specs=[
            pl.BlockSpec(blk, lambda r, n: (n, 3 * r)),
            pl.BlockSpec(blk, lambda r, n: (nxt(n), 3 * r)),
            pl.BlockSpec(blk, lambda r, n: (prev(n), 3 * r + 1)),
            pl.BlockSpec(blk, lambda r, n: (n, 3 * r + 1)),
            pl.BlockSpec(blk, lambda r, n: (prev(n), 3 * r + 2)),
            pl.BlockSpec(blk, lambda r, n: (n, 3 * r + 2)),
            pl.BlockSpec(blk, lambda r, n: (n, r)),
            pl.BlockSpec(blk, lambda r, n: (nxt(n), r)),
            pl.BlockSpec(blk, lambda r, n: (n, r)),
            pl.BlockSpec(blk, lambda r, n: (nxt(n), r)),
            pl.BlockSpec(blk, lambda r, n: (n, r)),
            pl.BlockSpec(blk, lambda r, n: (nxt(n), r)),
            pl.BlockSpec((None, N_HEADS, w, 2 * w), lambda r, n: (branch, 0, 0, 0)),
        ],
        out_specs=[pl.BlockSpec(blk, lambda r, n: (n, r))] * 3 + [pl.BlockSpec((N_HEADS, w, 2 * w), lambda r, n: (0, 0, 0))],
        out_shape=[_sds((length, dil * hw), F32)] * 3 + [_sds((N_HEADS, w, 2 * w), F32)],
        compiler_params=pltpu.CompilerParams(dimension_semantics=("arbitrary",) * 2, vmem_limit_bytes=V7X_VMEM_LIMIT_BYTES),
    )(qkv_v, qkv_v, qkv_v, qkv_v, qkv_v, qkv_v, do_v, do_v, lj_v, lj_v, dl_v, dl_v, bias)
    return [o.reshape(s_len, hw) for o in outs[:3]] + [outs[3]]


def _rel_bias_grad(dsum, buckets):
    w = DIL_BLOCK

    def body(ds_ref, bk_ref, o_ref):
        row = lax.broadcasted_iota(jnp.int32, (N_BUCKETS, PAIR), 0)
        lane = lax.broadcasted_iota(jnp.int32, (N_BUCKETS, PAIR), 1)

        def per_bucket(b, acc):
            for p in range(len(DILATIONS)):
                hit = bk_ref[p] == b
                for h in range(N_HEADS):
                    part = jnp.where(hit, ds_ref[p, h], 0.0)
                    tot = jnp.sum(jnp.sum(part, axis=1, keepdims=True), axis=0, keepdims=True)
                    acc = acc + jnp.where(jnp.logical_and(row == b, lane == h), tot, 0.0)
            return acc

        o_ref[...] = lax.fori_loop(0, N_BUCKETS, per_bucket, jnp.zeros((N_BUCKETS, PAIR), F32))

    return pl.pallas_call(body, name="rel_bias_grad", out_shape=_sds((N_BUCKETS, PAIR), F32))(dsum, buckets)


def _in_proj_bwd(dfq, dfk, dfv, dgx, ddq, ddk, ddv, w_in_t, dx1, x, g_pre):
    s_len, d = x.shape
    hw = HEAD_WIDTH
    tm = ROW_TILE

    def body(fq, fk, fv, gx, q1, q2, q3, k1, k2, k3, v1, v2, v3, w_ref, dx_ref, x_ref, g_ref, o_ref, dp_ref, dg_ref):
        @pl.when(pl.program_id(0) == 0)
        def _():
            dg_ref[...] = jnp.zeros_like(dg_ref)

        dp_ref[:, 0:hw] = fq[...]
        dp_ref[:, hw:2 * hw] = fk[...]
        dp_ref[:, 2 * hw:3 * hw] = fv[...]
        dp_ref[:, 3 * hw:4 * hw] = gx[...]
        dp_ref[:, 4 * hw:5 * hw] = (((q1[...] + q2[...]) + q3[...]) * QK_SCALE).astype(BF16)
        dp_ref[:, 5 * hw:6 * hw] = ((k1[...] + k2[...]) + k3[...]).astype(BF16)
        dp_ref[:, 6 * hw:7 * hw] = ((v1[...] + v2[...]) + v3[...]).astype(BF16)
        dh = _dot(dp_ref[...], w_ref[...])
        dx, dg = _rms_bwd(x_ref[...], g_ref[...], dh)
        dg_ref[...] += dg
        o_ref[...] = dx_ref[...] + dx

    half = _rows(tm, hw)
    return _call(
        body, name="in_proj_bwd", grid=(s_len // tm,),
        in_specs=[half] * 13 + [_resident(w_in_t.shape), _rows(tm, d), _rows(tm, d), _resident((1, d))],
        out_specs=[_rows(tm, d), _rows(tm, 7 * hw), _acc_out((1, d))],
        out_shape=[_sds((s_len, d), F32), _sds((s_len, 7 * hw), BF16), _sds((1, d), F32)],
    )(dfq, dfk, dfv, dgx, *ddq, *ddk, *ddv, w_in_t, dx1, x, g_pre)


def _expand_w_in(w_in):
    hw = HEAD_WIDTH
    gate = jnp.repeat(w_in[:, 3 * hw:3 * hw + N_HEADS], HEAD_DIM, axis=1)
    return jnp.concatenate([w_in[:, :3 * hw], gate, w_in[:, 3 * hw + N_HEADS:]], axis=1)


def _local_step(x, mem, target, g, w_bf):
    hw = HEAD_WIDTH
    w_in_e = _expand_w_in(w_bf["w_in"])
    b_exp = jnp.repeat(g["b_f"], HEAD_DIM, axis=1)
    buckets = jnp.asarray(_dil_buckets())

    h1, fqkv, gx, dqkv = _in_proj(x, g["g_mix_pre"], w_in_e)
    aq, ak = _gate_scan(gx, b_exp)
    o_fox, lse_fox = _fox_fwd(fqkv, aq, ak)
    bias = _dil_bias(g["rel_bias"], buckets)
    branches = [_dil_fwd(dqkv, bias, p) for p in range(len(DILATIONS))]
    x1, y1, o_dil, lj = _mix_out(o_fox, [b[0] for b in branches], [b[1] for b in branches], w_bf["w_out"], x, g["g_mix_post"])
    hm, kx, vx = _mem_fwd(mem, g["g_mem"], w_bf["w_xk"], w_bf["w_xv"])
    x2, y2, h2, qx, ox = _xattn_fwd(x1, g["g_xattn_pre"], w_bf["w_xq"], kx, vx, w_bf["w_xo"], g["g_xattn_post"])
    h3, a, u, z = _ffn_up(x2, g["g_ffn_pre"], w_bf["w_gate"], w_bf["w_up"])
    y3, dx3, sq = _ffn_down_loss(z, w_bf["w_down"], x2, g["g_ffn_post"], target)

    grads = {}
    dy3, da, du, grads["g_ffn_post"] = _ffn_bwd_a(dx3, y3, g["g_ffn_post"], w_bf["w_down"].T, a, u)
    dx2, grads["g_ffn_pre"] = _ffn_bwd_b(da, du, w_bf["w_gate"].T, w_bf["w_up"].T, dx3, x2, g["g_ffn_pre"])
    grads["w_down"] = _weight_grad(z, dy3, "dw_down")
    grads["w_gate"] = _weight_grad(h3, da, "dw_gate")
    grads["w_up"] = _weight_grad(h3, du, "dw_up")
    dx1, dy2, dqx, dkx, dvx, grads["g_xattn_post"], grads["g_xattn_pre"] = _xattn_bwd(
        dx2, y2, g["g_xattn_post"], w_bf["w_xo"].T, qx, kx, vx, w_bf["w_xq"].T, x1, g["g_xattn_pre"])
    grads["w_xo"] = _weight_grad(ox, dy2, "dw_xo")
    grads["w_xq"] = _weight_grad(h2, dqx, "dw_xq")
    grads["w_xk"], grads["w_xv"], grads["g_mem"] = _mem_bwd(dkx, dvx, w_bf["w_xk"].T, w_bf["w_xv"].T, hm, mem, g["g_mem"])
    dy1, do_fox, do_dil, delta_fox, delta_dil, grads["g_mix_post"] = _mix_out_bwd(
        dx1, y1, g["g_mix_post"], w_bf["w_out"].T, o_fox, o_dil)
    grads["w_out"] = jnp.concatenate([_weight_grad(o_fox, dy1, "dw_out_fox"), _weight_grad(o_dil, dy1, "dw_out_dil")], axis=0)
    aql, ad = _fox_bwd_prep(aq, lse_fox, delta_fox)
    dfq, rs = _fox_bwd_dq(fqkv, do_fox, aql, ak, ad)
    dfk, dfv, dc = _fox_bwd_dkv(fqkv, do_fox, aql, ak, ad)
    dgx, db = _gate_bwd(rs, dc, gx, b_exp)
    grads["b_f"] = db[:, ::HEAD_DIM]
    dil = [_dil_bwd(dqkv, do_dil, lj, delta_dil, bias, p) for p in range(len(DILATIONS))]
    grads["rel_bias"] = _rel_bias_grad(jnp.stack([t[3] for t in dil]), buckets)[:, :N_HEADS]
    grad_x, dproj, grads["g_mix_pre"] = _in_proj_bwd(
        dfq, dfk, dfv, dgx, [t[0] for t in dil], [t[1] for t in dil], [t[2] for t in dil], w_in_e.T, dx1, x, g["g_mix_pre"])
    dw_in_e = _weight_grad(h1, dproj, "dw_in")
    grads["w_in"] = jnp.concatenate(
        [dw_in_e[:, :3 * hw], dw_in_e[:, 3 * hw:4 * hw:HEAD_DIM], dw_in_e[:, 4 * hw:]], axis=1)
    return sq, grad_x, grads


MESH = pl.DeviceIdType.MESH


def _my_place():
    return lax.axis_index("x"), lax.axis_index("y"), lax.axis_index("c")


def _all_gather(x, name):
    rows, lanes = x.shape

    def body(x_ref, out_ref, send_sems, recv_sems, local_sem):
        mx, my, mc = _my_place()
        me, sibling = (mx, my, mc), (mx, my, 1 - mc)
        chips = [(1 - mx, my), (mx, 1 - my), (1 - mx, 1 - my)]

        def slot(px, py, pc):
            return out_ref.at[4 * px + 2 * py + pc]

        def copy(k, block, to, src=None):
            return pltpu.make_async_remote_copy(
                src_ref=slot(*block) if src is None else src, dst_ref=slot(*block),
                send_sem=send_sems.at[k], recv_sem=recv_sems.at[k], device_id=to, device_id_type=MESH)

        mine = pltpu.make_async_copy(x_ref, slot(*me), local_sem)
        mine.start()
        first = [copy(0, me, sibling, src=x_ref)]
        first += [copy(1 + j, me, (*chip, mc), src=x_ref) for j, chip in enumerate(chips)]
        for cp in first:
            cp.start()
        passed = [copy(4 + j, (*chip, mc), sibling) for j, chip in enumerate(chips)]
        for j, chip in enumerate(chips):
            copy(1 + j, (*chip, mc), me).wait_recv()
            passed[j].start()
        copy(0, sibling, me).wait_recv()
        for j, chip in enumerate(chips):
            copy(4 + j, (*chip, 1 - mc), me).wait_recv()
        for cp in first + passed:
            cp.wait_send()
        mine.wait()

    return pl.pallas_call(
        body, name=name, out_shape=_sds((N_DEV, rows, lanes), x.dtype),
        in_specs=[pl.BlockSpec(memory_space=pl.ANY)], out_specs=pl.BlockSpec(memory_space=pl.ANY),
        scratch_shapes=[pltpu.SemaphoreType.DMA((N_DEV - 1,)), pltpu.SemaphoreType.DMA((N_DEV - 1,)), pltpu.SemaphoreType.DMA],
    )(x)


def _exchange(g, name):
    _, rows, lanes = g.shape

    def body(g_ref, land_ref, send_sems, recv_sems, local_sem):
        mx, my, mc = _my_place()
        me = 4 * mx + 2 * my + mc
        mine = pltpu.make_async_copy(g_ref.at[me], land_ref.at[me], local_sem)
        mine.start()
        sent, arriving = [], []
        for k in (1, 2, 3, 4, 5, 6, 7):
            px = 1 - mx if k & 4 else mx
            py = 1 - my if k & 2 else my
            pc = 1 - mc if k & 1 else mc
            peer = 4 * px + 2 * py + pc
            cp = pltpu.make_async_remote_copy(
                src_ref=g_ref.at[peer], dst_ref=land_ref.at[me], send_sem=send_sems.at[k - 1], recv_sem=recv_sems.at[k - 1],
                device_id=(px, py, pc), device_id_type=MESH)
            cp.start()
            sent.append(cp)
            arriving.append(pltpu.make_async_remote_copy(
                src_ref=g_ref.at[me], dst_ref=land_ref.at[peer], send_sem=send_sems.at[k - 1], recv_sem=recv_sems.at[k - 1],
                device_id=(px, py, pc), device_id_type=MESH))
        for cp in arriving:
            cp.wait_recv()
        for cp in sent:
            cp.wait_send()
        mine.wait()

    return pl.pallas_call(
        body, name=name, out_shape=_sds(g.shape, g.dtype),
        in_specs=[pl.BlockSpec(memory_space=pl.ANY)], out_specs=pl.BlockSpec(memory_space=pl.ANY),
        scratch_shapes=[pltpu.SemaphoreType.DMA((N_DEV - 1,)), pltpu.SemaphoreType.DMA((N_DEV - 1,)), pltpu.SemaphoreType.DMA],
    )(g)


def _sum_slots(parts, name):
    n, rows, lanes = parts.shape
    tr = 512 if rows % 512 == 0 else rows

    def body(p_ref, o_ref):
        acc = p_ref[0]
        for j in range(1, n):
            acc = acc + p_ref[j]
        o_ref[...] = acc

    return _call(
        body, name=name, grid=(rows // tr,),
        in_specs=[pl.BlockSpec((n, tr, lanes), lambda i: (0, i, 0))], out_specs=_rows(tr, lanes),
        out_shape=_sds((rows, lanes), parts.dtype),
    )(parts)


def _adamw(w, g, m, v, name):
    def body(w_ref, g_ref, m_ref, v_ref, d_ref, nm_ref, nv_ref):
        gv = g_ref[...]
        m_new = ADAM_B1 * m_ref[...] + (1.0 - ADAM_B1) * gv
        v_new = ADAM_B2 * v_ref[...] + (1.0 - ADAM_B2) * (gv * gv)
        nm_ref[...] = m_new
        nv_ref[...] = v_new
        m_hat = m_new / (1.0 - ADAM_B1 ** ADAM_STEP)
        v_hat = v_new / (1.0 - ADAM_B2 ** ADAM_STEP)
        d_ref[...] = -ADAM_LR * (m_hat / (jnp.sqrt(v_hat) + ADAM_EPS) + ADAM_WD * w_ref[...])

    out = _sds(w.shape, F32)
    return pl.pallas_call(
        body, name=name, out_shape=[out, out, out],
        compiler_params=pltpu.CompilerParams(vmem_limit_bytes=V7X_VMEM_LIMIT_BYTES),
    )(w, g, m, v)


def _loss_head(sq, d_model):
    def body(sq_ref, o_ref):
        tot = jnp.sum(jnp.sum(sq_ref[...], axis=1, keepdims=True), axis=0, keepdims=True)
        o_ref[...] = 0.5 * (tot / d_model)

    return pl.pallas_call(body, name="loss_head", out_shape=_sds((1, 1), F32))(sq)


_BIG = (("w_in", 1), ("w_out", 0), ("w_xq", 0), ("w_xk", 0), ("w_xv", 0), ("w_xo", 1), ("w_gate", 1), ("w_up", 1), ("w_down", 0))
_SMALL = ("g_mix_pre", "b_f", "rel_bias", "g_mix_post", "g_xattn_pre", "g_mem", "g_xattn_post", "g_ffn_pre", "g_ffn_post")
LANES = 128
BIG_ROW_ALIGN = 512


def _round_up(n, k):
    return -(-n // k) * k


def _pack_rows(flat_parts, row_align, dtype):
    starts, rows, padded = [], 0, []
    for p in flat_parts:
        n = _round_up(p.shape[0], LANES)
        starts.append(rows)
        rows += n // LANES
        padded.append(jnp.pad(p.astype(dtype), (0, n - p.shape[0])))
    total = _round_up(rows, row_align)
    padded.append(jnp.zeros(((total - rows) * LANES,), dtype))
    return jnp.concatenate(padded).reshape(total, LANES), starts


def _unpack_rows(buf, starts, shapes):
    lead = buf.shape[:-2]
    flat = buf.reshape(lead + (-1,))
    out = []
    for st, shp in zip(starts, shapes):
        n = int(np.prod(shp))
        out.append(flat[..., st * LANES:st * LANES + n].reshape(lead + tuple(shp)))
    return out


def kernel(x, mem, g_mix_pre, w_in, b_f, rel_bias, w_out, g_mix_post, g_xattn_pre, g_mem, w_xq, w_xk, w_xv, w_xo, g_xattn_post, g_ffn_pre, w_gate, w_up, w_down, g_ffn_post, loss_target, m_g_mix_pre, m_w_in, m_b_f, m_rel_bias, m_w_out, m_g_mix_post, m_g_xattn_pre, m_g_mem, m_w_xq, m_w_xk, m_w_xv, m_w_xo, m_g_xattn_post, m_g_ffn_pre, m_w_gate, m_w_up, m_w_down, m_g_ffn_post, v_g_mix_pre, v_w_in, v_b_f, v_rel_bias, v_w_out, v_g_mix_post, v_g_xattn_pre, v_g_mem, v_w_xq, v_w_xk, v_w_xv, v_w_xo, v_g_xattn_post, v_g_ffn_pre, v_w_gate, v_w_up, v_w_down, v_g_ffn_post):
    given = dict(locals())
    order = ("g_mix_pre", "w_in", "b_f", "rel_bias", "w_out", "g_mix_post", "g_xattn_pre", "g_mem", "w_xq", "w_xk", "w_xv",
             "w_xo", "g_xattn_post", "g_ffn_pre", "w_gate", "w_up", "w_down", "g_ffn_post")
    two_d = lambda a: a.reshape(a.shape[-2:])
    w_loc = {n: two_d(given[n]) for n in order}
    m_loc = {n: two_d(given["m_" + n]) for n in order}
    v_loc = {n: two_d(given["v_" + n]) for n in order}
    d_model = x.shape[-1]

    shard_shapes = [w_loc[n].shape for n, _ in _BIG]
    packed, starts = _pack_rows([w_loc[n].reshape(-1) for n, _ in _BIG], BIG_ROW_ALIGN, BF16)
    gathered = _all_gather(packed, "gather_weights")
    w_bf = {}
    for (n, axis), part in zip(_BIG, _unpack_rows(gathered, starts, shard_shapes)):
        r, c = part.shape[1:]
        w_bf[n] = part.reshape(N_DEV * r, c) if axis == 0 else part.transpose(1, 0, 2).reshape(r, N_DEV * c)

    small = {n: w_loc[n] for n in _SMALL}
    sq, grad_x, grads = _local_step(two_d(x), two_d(mem), two_d(loss_target), small, w_bf)

    per_owner = []
    for (n, axis), shp in zip(_BIG, shard_shapes):
        r, c = shp
        gfull = grads[n]
        per_owner.append(gfull.reshape(N_DEV, r * c) if axis == 0 else gfull.reshape(r, N_DEV, c).transpose(1, 0, 2).reshape(N_DEV, r * c))
    rows_big = packed.shape[0]
    slots = []
    for j in range(N_DEV):
        buf, _ = _pack_rows([p[j] for p in per_owner], BIG_ROW_ALIGN, F32)
        slots.append(buf)
    landed = _exchange(jnp.stack(slots), "exchange_grads")
    g_big = dict(zip([n for n, _ in _BIG], _unpack_rows(_sum_slots(landed, "sum_grads"), starts, shard_shapes)))
    assert landed.shape[1] == rows_big

    small_parts = [grads[n].reshape(-1) for n in _SMALL] + [sq.reshape(-1)]
    small_shapes = [w_loc[n].shape for n in _SMALL] + [sq.shape]
    spacked, sstarts = _pack_rows(small_parts, 8, F32)
    ssum = _sum_slots(_all_gather(spacked, "gather_small"), "sum_small")
    g_small = dict(zip(_SMALL, _unpack_rows(ssum, sstarts, small_shapes)[:-1]))
    sq_rows = sq.size // LANES
    loss = _loss_head(ssum[sstarts[-1]:sstarts[-1] + sq_rows], d_model).reshape(())

    g_loc, delta, new_m, new_v = {}, {}, {}, {}
    for n, _ in _BIG:
        g_loc[n] = g_big[n]
        delta[n], new_m[n], new_v[n] = _adamw(w_loc[n], g_big[n], m_loc[n], v_loc[n], "adamw_" + n)
    pk = lambda d: _pack_rows([d[n].reshape(-1) for n in _SMALL], 8, F32)[0]
    pstarts = _pack_rows([w_loc[n].reshape(-1) for n in _SMALL], 8, F32)[1]
    d_s, m_s, v_s = _adamw(pk(w_loc), pk(g_small), pk(m_loc), pk(v_loc), "adamw_small")
    shapes_s = [w_loc[n].shape for n in _SMALL]
    for n, dd, mm, vv in zip(_SMALL, _unpack_rows(d_s, pstarts, shapes_s), _unpack_rows(m_s, pstarts, shapes_s),
                             _unpack_rows(v_s, pstarts, shapes_s)):
        g_loc[n], delta[n], new_m[n], new_v[n] = g_small[n], dd, mm, vv

    like = lambda d: [d[n].reshape(given[n].shape) for n in order]
    return (loss, grad_x.reshape(x.shape), *like(g_loc), *like(delta), *like(new_m), *like(new_v))
```

```python
import functools

import numpy as np
import jax
import jax.numpy as jnp
from jax import lax
from jax.experimental import pallas as pl
from jax.experimental.pallas import tpu as pltpu

F32 = jnp.float32
BF16 = jnp.bfloat16
HIGHEST = lax.Precision.HIGHEST

RMS_EPS = 1e-6
HEAD_DIM = 64
N_HEADS = 8
HEAD_WIDTH = N_HEADS * HEAD_DIM
PAIR = 2 * HEAD_DIM
N_PAIRS = N_HEADS // 2
DIL_BLOCK = 128
DILATIONS = (1, 4, 16)
N_BUCKETS = 32
MAX_DISTANCE = 2048
N_MEM_HEADS = 4
QK_SCALE = HEAD_DIM ** -0.5
NEG = -1e30
N_DEV = 8

ADAM_LR = 0.001
ADAM_B1 = 0.9
ADAM_B2 = 0.999
ADAM_EPS = 1e-08
ADAM_WD = 0.01
ADAM_STEP = 10

V7X_VMEM_LIMIT_BYTES = 56 * 2 ** 20
ROW_TILE = 256
ATT_BLOCK = 512
SCAN_TILE = 256


def _call(body, *, name, grid, in_specs, out_specs, out_shape, scratch=()):
    return pl.pallas_call(
        body, name=name, grid=grid, in_specs=in_specs, out_specs=out_specs, out_shape=out_shape,
        scratch_shapes=list(scratch),
        compiler_params=pltpu.CompilerParams(
            dimension_semantics=("arbitrary",) * len(grid), vmem_limit_bytes=V7X_VMEM_LIMIT_BYTES))


def _rows(tm, n):
    return pl.BlockSpec((tm, n), lambda i: (i, 0))


def _resident(shape):
    zeros = (0,) * len(shape)
    return pl.BlockSpec(shape, lambda i: zeros, pipeline_mode=pl.Buffered(1))


def _acc_out(shape):
    zeros = (0,) * len(shape)
    return pl.BlockSpec(shape, lambda i: zeros)


def _sds(shape, dtype):
    return jax.ShapeDtypeStruct(shape, dtype)


def _dot(a, b):
    return jnp.dot(a, b, preferred_element_type=F32)


def _dot_nt(a, b):
    return lax.dot_general(a, b, (((1,), (1,)), ((), ())), preferred_element_type=F32)


def _dot_tn(a, b):
    return lax.dot_general(a, b, (((0,), (0,)), ((), ())), preferred_element_type=F32)


def _rms_fwd(x, g):
    r = lax.rsqrt(jnp.mean(x * x, axis=-1, keepdims=True) + RMS_EPS)
    return (x * r) * g


def _rms_bwd(xin, g, dy):
    r = lax.rsqrt(jnp.mean(xin * xin, axis=-1, keepdims=True) + RMS_EPS)
    xhat = xin * r
    dg = jnp.sum(dy * xhat, axis=0, keepdims=True)
    dxh = dy * g
    dx = r * (dxh - xhat * jnp.mean(dxh * xhat, axis=-1, keepdims=True))
    return dx, dg


def _first_head_lanes():
    return lax.broadcasted_iota(jnp.int32, (1, PAIR), 1) < HEAD_DIM


def _pick(mask, a, b):
    return jnp.where(mask, a, b)


def _zero_other(mask, v):
    return jnp.where(mask, v, jnp.zeros_like(v))


def _in_proj(x, g, w):
    s_len, d = x.shape
    tm = ROW_TILE
    hw = HEAD_WIDTH

    def body(x_ref, g_ref, w_ref, h_ref, fqkv_ref, gx_ref, dqkv_ref):
        h = _rms_fwd(x_ref[...], g_ref[...]).astype(BF16)
        h_ref[...] = h
        proj = _dot(h, w_ref[...])
        fqkv_ref[:, 0:hw] = (proj[:, 0:hw] * QK_SCALE).astype(BF16)
        fqkv_ref[:, hw:3 * hw] = proj[:, hw:3 * hw].astype(BF16)
        gx_ref[...] = proj[:, 3 * hw:4 * hw]
        dqkv_ref[:, 0:hw] = (proj[:, 4 * hw:5 * hw] * QK_SCALE).astype(BF16)
        dqkv_ref[:, hw:3 * hw] = proj[:, 5 * hw:7 * hw].astype(BF16)

    return _call(
        body, name="in_proj", grid=(s_len // tm,),
        in_specs=[_rows(tm, d), _resident((1, d)), _resident(w.shape)],
        out_specs=[_rows(tm, d), _rows(tm, 3 * hw), _rows(tm, hw), _rows(tm, 3 * hw)],
        out_shape=[_sds((s_len, d), BF16), _sds((s_len, 3 * hw), BF16), _sds((s_len, hw), F32), _sds((s_len, 3 * hw), BF16)],
    )(x, g, w)


def _swap_halves(x):
    return jnp.concatenate([pltpu.roll(x[:, i * PAIR:(i + 1) * PAIR], HEAD_DIM, 1) for i in range(x.shape[1] // PAIR)], axis=1)


def _split3(x):
    hi = x.astype(BF16)
    r = x - hi.astype(F32)
    mid = r.astype(BF16)
    lo = (r - mid.astype(F32)).astype(BF16)
    return hi, mid, lo


def _lane_in_head(width):
    return lax.broadcasted_iota(jnp.int32, (1, width), 1) % HEAD_DIM


def _place3(jj, first, pieces, base):
    out = base
    for i, p in enumerate(pieces):
        out = jnp.where(jj == first + i, p, out)
    return out


def _gate_scan(gx, b_exp):
    s_len, hw = gx.shape
    t = min(SCAN_TILE, s_len)
    tri = jnp.asarray(np.tril(np.ones((t, t), np.float32)))

    def body(gx_ref, b_ref, tri_ref, aq_ref, ak_ref, carry):
        @pl.when(pl.program_id(0) == 0)
        def _():
            carry[...] = jnp.zeros_like(carry)

        z = gx_ref[...] + b_ref[...]
        lf = jnp.minimum(z, 0.0) - jnp.log1p(jnp.exp(-jnp.abs(z)))
        c = jnp.dot(tri_ref[...], lf, precision=HIGHEST, preferred_element_type=F32) + carry[...]
        carry[...] = c[t - 1:t, :]
        hi, mid, lo = _split3(_swap_halves(c))
        jj = _lane_in_head(hw)
        zero = jnp.zeros_like(hi)
        one = jnp.ones_like(hi)
        aq_ref[...] = _place3(jj, 0, (hi, mid, lo), jnp.where(jj < 6, one, zero))
        ak_ref[...] = _place3(jj, 3, (-hi, -mid, -lo), jnp.where(jj < 9, one, zero))

    return _call(
        body, name="gate_scan", grid=(s_len // t,),
        in_specs=[_rows(t, hw), _resident((1, hw)), _resident((t, t))],
        out_specs=[_rows(t, hw), _rows(t, hw)],
        out_shape=[_sds((s_len, hw), BF16), _sds((s_len, hw), BF16)],
        scratch=[pltpu.VMEM((1, hw), F32)],
    )(gx, b_exp, tri)


def _lower_triangle(n):
    return lax.broadcasted_iota(jnp.int32, (n, n), 1) <= lax.broadcasted_iota(jnp.int32, (n, n), 0)


def _on_blocks(qi, kj, step):
    @pl.when(kj < qi)
    def _():
        step(False)

    @pl.when(kj == qi)
    def _():
        step(True)


def _fox_fwd(fqkv, aq, ak):
    s_len = fqkv.shape[0]
    bq = bk = min(ATT_BLOCK, s_len)
    nq, nk = s_len // bq, s_len // bk

    def body(q_ref, k_ref, v_ref, aq_ref, ak_ref, o_ref, lse_ref, qa_ref, m_ref, acc_ref):
        qi, kj = pl.program_id(1), pl.program_id(2)
        in0 = _first_head_lanes()
        not0 = jnp.logical_not(in0)

        @pl.when(kj == 0)
        def _():
            q2, a2 = q_ref[...], aq_ref[...]
            qa_ref[0] = jnp.where(in0, q2, a2)
            qa_ref[1] = jnp.where(in0, a2, q2)
            m_ref[...] = jnp.full_like(m_ref, NEG)
            acc_ref[...] = jnp.zeros_like(acc_ref)

        def step(masked):
            k2, v2, a2 = k_ref[...], v_ref[...], ak_ref[...]
            one = jnp.ones_like(v2)
            for a, mine in ((0, in0), (1, not0)):
                s = _dot_nt(qa_ref[a], jnp.where(mine, k2, a2))
                if masked:
                    s = jnp.where(_lower_triangle(bq), s, NEG)
                m_old = m_ref[a]
                m_new = jnp.maximum(m_old, jnp.max(s, axis=1, keepdims=True))
                p = jnp.exp(s - jnp.tile(m_new, (1, bk // PAIR))).astype(BF16)
                acc_ref[a] = jnp.exp(m_old - m_new) * acc_ref[a] + _dot(p, jnp.where(mine, v2, one))
                m_ref[a] = m_new

        _on_blocks(qi, kj, step)

        @pl.when(kj == qi)
        def _():
            acc0, acc1 = acc_ref[0], acc_ref[1]
            l2 = pltpu.roll(_pick(in0, acc1, acc0), HEAD_DIM, 1)
            o_ref[...] = _pick(in0, acc0, acc1) / l2
            lse_ref[...] = _pick(in0, m_ref[0], m_ref[1]) + jnp.log(l2)

    blk = lambda rows: (rows, PAIR)
    qmap = lambda p, i, j: (i, p)
    return pl.pallas_call(
        body, name="fox_fwd", grid=(N_PAIRS, nq, nk),
        in_specs=[
            pl.BlockSpec(blk(bq), qmap),
            pl.BlockSpec(blk(bk), lambda p, i, j: (jnp.minimum(j, i), N_PAIRS + p)),
            pl.BlockSpec(blk(bk), lambda p, i, j: (jnp.minimum(j, i), 2 * N_PAIRS + p)),
            pl.BlockSpec(blk(bq), qmap),
            pl.BlockSpec(blk(bk), lambda p, i, j: (jnp.minimum(j, i), p)),
        ],
        out_specs=[pl.BlockSpec(blk(bq), qmap), pl.BlockSpec(blk(bq), qmap)],
        out_shape=[_sds((s_len, HEAD_WIDTH), F32), _sds((s_len, HEAD_WIDTH), F32)],
        scratch_shapes=[pltpu.VMEM((2, bq, PAIR), BF16), pltpu.VMEM((2, bq, PAIR), F32), pltpu.VMEM((2, bq, PAIR), F32)],
        compiler_params=pltpu.CompilerParams(dimension_semantics=("arbitrary",) * 3, vmem_limit_bytes=V7X_VMEM_LIMIT_BYTES),
    )(fqkv, fqkv, fqkv, aq, ak)


def _t5_bucket(dist):
    max_exact = N_BUCKETS // 2
    d = np.maximum(dist, 1).astype(np.float32)
    large = max_exact + (np.log(d / max_exact) / np.log(MAX_DISTANCE / max_exact) * (N_BUCKETS - max_exact)).astype(np.int32)
    large = np.minimum(large, N_BUCKETS - 1)
    return np.where(dist < max_exact, dist, large).astype(np.int32)


def _dil_buckets():
    w = DIL_BLOCK
    qi = np.arange(w)[:, None]
    kj = np.arange(2 * w)[None, :]
    sub = qi + w - kj
    band = (sub >= 0) & (sub <= w)
    out = [np.where(band, _t5_bucket(np.clip(sub, 0, w) * dil), -1) for dil in DILATIONS]
    return np.stack(out).astype(np.int32)


def _dil_bias(rel_bias, buckets):
    w = DIL_BLOCK

    def body(rb_ref, bk_ref, o_ref):
        for p in range(len(DILATIONS)):
            bk = bk_ref[p]
            for h in range(N_HEADS):
                def add(b, acc):
                    return acc + jnp.where(bk == b, rb_ref[b, h], 0.0)
                acc = lax.fori_loop(0, N_BUCKETS, add, jnp.zeros((w, 2 * w), F32))
                o_ref[p, h] = jnp.where(bk < 0, NEG, acc)

    return pl.pallas_call(
        body, name="dil_bias",
        in_specs=[pl.BlockSpec(memory_space=pltpu.SMEM), pl.BlockSpec(memory_space=pltpu.VMEM)],
        out_specs=pl.BlockSpec(memory_space=pltpu.VMEM),
        out_shape=_sds((len(DILATIONS), N_HEADS, w, 2 * w), F32),
    )(rel_bias, buckets)


def _dil_fwd(dqkv, bias, branch):
    dil = DILATIONS[branch]
    s_len = dqkv.shape[0]
    w = DIL_BLOCK
    hw = HEAD_WIDTH
    length = s_len // dil
    nb = length // w
    view = dqkv.reshape(length, dil * 3 * hw)

    def body(q_ref, kc_ref, kp_ref, vc_ref, vp_ref, b_ref, o_ref, lse_ref):
        n = pl.program_id(1)
        in0 = _first_head_lanes()
        not0 = jnp.logical_not(in0)
        pairs = [slice(pr * PAIR, (pr + 1) * PAIR) for pr in range(N_PAIRS)]
        tiles = []
        for sl in pairs:
            q2 = q_ref[:, sl]
            qq = jnp.concatenate([_zero_other(in0, q2), _zero_other(not0, q2)], axis=0)
            tiles.append(jnp.concatenate([_dot_nt(qq, kp_ref[:, sl]), _dot_nt(qq, kc_ref[:, sl])], axis=1))
        s = jnp.concatenate(tiles, axis=0) + b_ref[...].reshape(N_HEADS * w, 2 * w)
        prev_half = lax.broadcasted_iota(jnp.int32, (1, 2 * w), 1) < w
        s = jnp.where(jnp.logical_and(n == 0, prev_half), NEG, s)
        m = jnp.max(s, axis=1, keepdims=True)
        e = jnp.exp(s - m)
        l = jnp.sum(e, axis=1, keepdims=True)
        p = (e / l).astype(BF16)
        lse = m + jnp.log(l)
        for pr, sl in enumerate(pairs):
            pp = p[2 * pr * w:(2 * pr + 2) * w]
            o2 = _dot(pp[:, :w], vp_ref[:, sl]) + _dot(pp[:, w:], vc_ref[:, sl])
            o_ref[:, sl] = _pick(in0, o2[:w], o2[w:])
            lse_ref[:, sl] = _pick(in0, lse[2 * pr * w:(2 * pr + 1) * w], lse[(2 * pr + 1) * w:(2 * pr + 2) * w])

    prev = lambda n: jnp.maximum(n - 1, 0)
    out = pl.pallas_call(
        body, name=f"dil_fwd_{dil}", grid=(dil, nb),
        in_specs=[
            pl.BlockSpec((w, hw), lambda r, n: (n, 3 * r)),
            pl.BlockSpec((w, hw), lambda r, n: (n, 3 * r + 1)),
            pl.BlockSpec((w, hw), lambda r, n: (prev(n), 3 * r + 1)),
            pl.BlockSpec((w, hw), lambda r, n: (n, 3 * r + 2)),
            pl.BlockSpec((w, hw), lambda r, n: (prev(n), 3 * r + 2)),
            pl.BlockSpec((None, N_HEADS, w, 2 * w), lambda r, n: (branch, 0, 0, 0)),
        ],
        out_specs=[pl.BlockSpec((w, hw), lambda r, n: (n, r)), pl.BlockSpec((w, hw), lambda r, n: (n, r))],
        out_shape=[_sds((length, dil * hw), F32), _sds((length, dil * hw), F32)],
        compiler_params=pltpu.CompilerParams(dimension_semantics=("arbitrary",) * 2, vmem_limit_bytes=V7X_VMEM_LIMIT_BYTES),
    )(view, view, view, view, view, bias)
    return out[0].reshape(s_len, hw), out[1].reshape(s_len, hw)


def _mix_out(o_fox, o_br, lse_br, w_out, x, g_post):
    s_len, d = x.shape
    hw = HEAD_WIDTH
    tm = ROW_TILE

    def body(of_ref, o1, o2, o3, l1, l2, l3, w_ref, x_ref, g_ref, x1_ref, y1_ref, od_ref, lj_ref):
        la, lb, lc = l1[...], l2[...], l3[...]
        m = jnp.maximum(jnp.maximum(la, lb), lc)
        ea, eb, ec = jnp.exp(la - m), jnp.exp(lb - m), jnp.exp(lc - m)
        tot = ea + eb + ec
        o_dil = (ea / tot) * o1[...] + (eb / tot) * o2[...] + (ec / tot) * o3[...]
        od_ref[...] = o_dil
        lj_ref[...] = m + jnp.log(tot)
        y = _dot(of_ref[...].astype(BF16), w_ref[0:hw, :]) + _dot(o_dil.astype(BF16), w_ref[hw:2 * hw, :])
        y1_ref[...] = y
        x1_ref[...] = x_ref[...] + _rms_fwd(y, g_ref[...])

    half = _rows(tm, hw)
    return _call(
        body, name="mix_out", grid=(s_len // tm,),
        in_specs=[half] * 7 + [_resident(w_out.shape), _rows(tm, d), _resident((1, d))],
        out_specs=[_rows(tm, d), _rows(tm, d), half, half],
        out_shape=[_sds((s_len, d), F32), _sds((s_len, d), F32), _sds((s_len, hw), F32), _sds((s_len, hw), F32)],
    )(o_fox, *o_br, *lse_br, w_out, x, g_post)


def _mem_fwd(mem, g_mem, w_xk, w_xv):
    n_mem, d = mem.shape
    mw = w_xk.shape[1]

    def body(mem_ref, g_ref, wk_ref, wv_ref, hm_ref, k_ref, v_ref):
        hm = _rms_fwd(mem_ref[...], g_ref[...]).astype(BF16)
        hm_ref[...] = hm
        k_ref[...] = _dot(hm, wk_ref[...]).astype(BF16)
        v_ref[...] = _dot(hm, wv_ref[...]).astype(BF16)

    return pl.pallas_call(
        body, name="mem_fwd",
        out_shape=[_sds((n_mem, d), BF16), _sds((n_mem, mw), BF16), _sds((n_mem, mw), BF16)],
    )(mem, g_mem, w_xk, w_xv)


def _xattn_softmax(qa, k2):
    s = _dot_nt(qa, k2)
    m = jnp.max(s, axis=1, keepdims=True)
    e = jnp.exp(s - m)
    return e / jnp.sum(e, axis=1, keepdims=True)


def _xattn_fwd(x1, g_pre, w_xq, kx, vx, w_xo, g_post):
    s_len, d = x1.shape
    mw = w_xq.shape[1]
    n_mem = kx.shape[0]
    tm = ROW_TILE

    def body(x_ref, gp_ref, wq_ref, k_ref, v_ref, wo_ref, go_ref, x2_ref, y2_ref, h2_ref, q_ref, o_ref):
        x = x_ref[...]
        h = _rms_fwd(x, gp_ref[...]).astype(BF16)
        h2_ref[...] = h
        q = (_dot(h, wq_ref[...]) * QK_SCALE).astype(BF16)
        q_ref[...] = q
        in0 = _first_head_lanes()
        not0 = jnp.logical_not(in0)
        for pr in range(mw // PAIR):
            sl = slice(pr * PAIR, (pr + 1) * PAIR)
            q2, k2, v2 = q[:, sl], k_ref[:, sl], v_ref[:, sl]
            oa = [_dot(_xattn_softmax(_zero_other(mine, q2), k2).astype(BF16), v2) for mine in (in0, not0)]
            o_ref[:, sl] = _pick(in0, oa[0], oa[1]).astype(BF16)
        y = _dot(o_ref[...], wo_ref[...])
        y2_ref[...] = y
        x2_ref[...] = x + _rms_fwd(y, go_ref[...])

    return _call(
        body, name="xattn_fwd", grid=(s_len // tm,),
        in_specs=[_rows(tm, d), _resident((1, d)), _resident(w_xq.shape), _resident((n_mem, mw)), _resident((n_mem, mw)),
                  _resident(w_xo.shape), _resident((1, d))],
        out_specs=[_rows(tm, d), _rows(tm, d), _rows(tm, d), _rows(tm, mw), _rows(tm, mw)],
        out_shape=[_sds((s_len, d), F32), _sds((s_len, d), F32), _sds((s_len, d), BF16), _sds((s_len, mw), BF16),
                   _sds((s_len, mw), BF16)],
    )(x1, g_pre, w_xq, kx, vx, w_xo, g_post)


def _ffn_up(x2, g_pre, w_gate, w_up):
    s_len, d = x2.shape
    dff = w_gate.shape[1]
    tm = ROW_TILE

    def body(x_ref, g_ref, wg_ref, wu_ref, h_ref, a_ref, u_ref, z_ref):
        h = _rms_fwd(x_ref[...], g_ref[...]).astype(BF16)
        h_ref[...] = h
        a = _dot(h, wg_ref[...])
        u = _dot(h, wu_ref[...])
        a_ref[...] = a.astype(BF16)
        u_ref[...] = u.astype(BF16)
        z_ref[...] = ((a * jax.nn.sigmoid(a)) * u).astype(BF16)

    return _call(
        body, name="ffn_up", grid=(s_len // tm,),
        in_specs=[_rows(tm, d), _resident((1, d)), _resident(w_gate.shape), _resident(w_up.shape)],
        out_specs=[_rows(tm, d), _rows(tm, dff), _rows(tm, dff), _rows(tm, dff)],
        out_shape=[_sds((s_len, d), BF16)] + [_sds((s_len, dff), BF16)] * 3,
    )(x2, g_pre, w_gate, w_up)


def _ffn_down_loss(z, w_down, x2, g_post, target):
    s_len, d = x2.shape
    dff = z.shape[1]
    tm = ROW_TILE

    def body(z_ref, w_ref, x_ref, g_ref, t_ref, y_ref, dx_ref, sq_ref):
        @pl.when(pl.program_id(0) == 0)
        def _():
            sq_ref[...] = jnp.zeros_like(sq_ref)

        y = _dot(z_ref[...], w_ref[...])
        y_ref[...] = y
        err = (x_ref[...] + _rms_fwd(y, g_ref[...])) - t_ref[...]
        sq_ref[...] += jnp.sum(err * err, axis=0, keepdims=True)
        dx_ref[...] = err * (1.0 / d)

    return _call(
        body, name="ffn_down_loss", grid=(s_len // tm,),
        in_specs=[_rows(tm, dff), _resident(w_down.shape), _rows(tm, d), _resident((1, d)), _rows(tm, d)],
        out_specs=[_rows(tm, d), _rows(tm, d), _acc_out((1, d))],
        out_shape=[_sds((s_len, d), F32), _sds((s_len, d), F32), _sds((1, d), F32)],
    )(z, w_down, x2, g_post, target)


def _weight_grad(a, b, name):
    s_len, k = a.shape
    n = b.shape[1]
    ts = 512 if s_len % 512 == 0 else s_len
    tn = n
    while k * tn * 4 > 8 * 2 ** 20 and tn % 256 == 0:
        tn //= 2

    def body(a_ref, b_ref, o_ref):
        @pl.when(pl.program_id(1) == 0)
        def _():
            o_ref[...] = jnp.zeros_like(o_ref)

        o_ref[...] += _dot_tn(a_ref[...].astype(BF16), b_ref[...].astype(BF16))

    return pl.pallas_call(
        body, name=name, grid=(n // tn, s_len // ts),
        in_specs=[pl.BlockSpec((ts, k), lambda j, i: (i, 0)), pl.BlockSpec((ts, tn), lambda j, i: (i, j))],
        out_specs=pl.BlockSpec((k, tn), lambda j, i: (0, j)),
        out_shape=_sds((k, n), F32),
        compiler_params=pltpu.CompilerParams(dimension_semantics=("arbitrary",) * 2, vmem_limit_bytes=V7X_VMEM_LIMIT_BYTES),
    )(a, b)


def _ffn_bwd_a(dx3, y3, g_post, w_down_t, a, u):
    s_len, d = dx3.shape
    dff = a.shape[1]
    tm = ROW_TILE

    def body(dx_ref, y_ref, g_ref, w_ref, a_ref, u_ref, dy_ref, da_ref, du_ref, dg_ref):
        @pl.when(pl.program_id(0) == 0)
        def _():
            dg_ref[...] = jnp.zeros_like(dg_ref)

        dy, dg = _rms_bwd(y_ref[...], g_ref[...], dx_ref[...])
        dg_ref[...] += dg
        dyb = dy.astype(BF16)
        dy_ref[...] = dyb
        dz = _dot(dyb, w_ref[...])
        av = a_ref[...].astype(F32)
        uv = u_ref[...].astype(F32)
        sg = jax.nn.sigmoid(av)
        da_ref[...] = (dz * uv * (sg * (1.0 + av * (1.0 - sg)))).astype(BF16)
        du_ref[...] = (dz * (av * sg)).astype(BF16)

    return _call(
        body, name="ffn_bwd_a", grid=(s_len // tm,),
        in_specs=[_rows(tm, d), _rows(tm, d), _resident((1, d)), _resident(w_down_t.shape), _rows(tm, dff), _rows(tm, dff)],
        out_specs=[_rows(tm, d), _rows(tm, dff), _rows(tm, dff), _acc_out((1, d))],
        out_shape=[_sds((s_len, d), BF16), _sds((s_len, dff), BF16), _sds((s_len, dff), BF16), _sds((1, d), F32)],
    )(dx3, y3, g_post, w_down_t, a, u)


def _ffn_bwd_b(da, du, w_gate_t, w_up_t, dx3, x2, g_pre):
    s_len, d = x2.shape
    dff = da.shape[1]
    tm = ROW_TILE

    def body(da_ref, du_ref, wg_ref, wu_ref, dx_ref, x_ref, g_ref, o_ref, dg_ref):
        @pl.when(pl.program_id(0) == 0)
        def _():
            dg_ref[...] = jnp.zeros_like(dg_ref)

        dh = _dot(da_ref[...], wg_ref[...]) + _dot(du_ref[...], wu_ref[...])
        dx, dg = _rms_bwd(x_ref[...], g_ref[...], dh)
        dg_ref[...] += dg
        o_ref[...] = dx_ref[...] + dx

    return _call(
        body, name="ffn_bwd_b", grid=(s_len // tm,),
        in_specs=[_rows(tm, dff), _rows(tm, dff), _resident(w_gate_t.shape), _resident(w_up_t.shape), _rows(tm, d),
                  _rows(tm, d), _resident((1, d))],
        out_specs=[_rows(tm, d), _acc_out((1, d))],
        out_shape=[_sds((s_len, d), F32), _sds((1, d), F32)],
    )(da, du, w_gate_t, w_up_t, dx3, x2, g_pre)


def _xattn_bwd(dx2, y2, g_post, w_xo_t, q, kx, vx, w_xq_t, x1, g_pre):
    s_len, d = x1.shape
    mw = q.shape[1]
    n_mem = kx.shape[0]
    tm = ROW_TILE

    def body(dx_ref, y_ref, go_ref, wo_ref, q_ref, k_ref, v_ref, wq_ref, x_ref, gp_ref,
             dx1_ref, dy_ref, dq_ref, dk_ref, dv_ref, dgo_ref, dgp_ref):
        @pl.when(pl.program_id(0) == 0)
        def _():
            dk_ref[...] = jnp.zeros_like(dk_ref)
            dv_ref[...] = jnp.zeros_like(dv_ref)
            dgo_ref[...] = jnp.zeros_like(dgo_ref)
            dgp_ref[...] = jnp.zeros_like(dgp_ref)

        dxin = dx_ref[...]
        dy, dgo = _rms_bwd(y_ref[...], go_ref[...], dxin)
        dgo_ref[...] += dgo
        dyb = dy.astype(BF16)
        dy_ref[...] = dyb
        do = _dot(dyb, wo_ref[...]).astype(BF16)
        in0 = _first_head_lanes()
        not0 = jnp.logical_not(in0)
        for pr in range(mw // PAIR):
            sl = slice(pr * PAIR, (pr + 1) * PAIR)
            q2, k2, v2, do2 = q_ref[:, sl], k_ref[:, sl], v_ref[:, sl], do[:, sl]
            dqs = []
            dk2 = jnp.zeros((n_mem, PAIR), F32)
            dv2 = jnp.zeros((n_mem, PAIR), F32)
            for mine in (in0, not0):
                qa = _zero_other(mine, q2)
                doa = _zero_other(mine, do2)
                p = _xattn_softmax(qa, k2)
                dp = _dot_nt(doa, v2)
                ds = (p * (dp - jnp.sum(p * dp, axis=1, keepdims=True))).astype(BF16)
                dqs.append(_dot(ds, k2))
                dk2 = dk2 + _dot_tn(ds, qa)
                dv2 = dv2 + _dot_tn(p.astype(BF16), doa)
            dq_ref[:, sl] = (_pick(in0, dqs[0], dqs[1]) * QK_SCALE).astype(BF16)
            dk_ref[:, sl] += dk2
            dv_ref[:, sl] += dv2
        dh = _dot(dq_ref[...], wq_ref[...])
        dx, dgp = _rms_bwd(x_ref[...], gp_ref[...], dh)
        dgp_ref[...] += dgp
        dx1_ref[...] = dxin + dx

    return _call(
        body, name="xattn_bwd", grid=(s_len // tm,),
        in_specs=[_rows(tm, d), _rows(tm, d), _resident((1, d)), _resident(w_xo_t.shape), _rows(tm, mw),
                  _resident((n_mem, mw)), _resident((n_mem, mw)), _resident(w_xq_t.shape), _rows(tm, d), _resident((1, d))],
        out_specs=[_rows(tm, d), _rows(tm, d), _rows(tm, mw), _acc_out((n_mem, mw)), _acc_out((n_mem, mw)),
                   _acc_out((1, d)), _acc_out((1, d))],
        out_shape=[_sds((s_len, d), F32), _sds((s_len, d), BF16), _sds((s_len, mw), BF16), _sds((n_mem, mw), F32),
                   _sds((n_mem, mw), F32), _sds((1, d), F32), _sds((1, d), F32)],
    )(dx2, y2, g_post, w_xo_t, q, kx, vx, w_xq_t, x1, g_pre)


def _mem_bwd(dk, dv, w_xk_t, w_xv_t, hm, mem, g_mem):
    n_mem, d = mem.shape
    mw = dk.shape[1]

    def body(dk_ref, dv_ref, wk_ref, wv_ref, hm_ref, mem_ref, g_ref, dwk_ref, dwv_ref, dg_ref):
        dkb = dk_ref[...].astype(BF16)
        dvb = dv_ref[...].astype(BF16)
        dhm = _dot(dkb, wk_ref[...]) + _dot(dvb, wv_ref[...])
        _, dg = _rms_bwd(mem_ref[...], g_ref[...], dhm)
        dg_ref[...] = dg
        dwk_ref[...] = _dot_tn(hm_ref[...], dkb)
        dwv_ref[...] = _dot_tn(hm_ref[...], dvb)

    return pl.pallas_call(
        body, name="mem_bwd",
        out_shape=[_sds((d, mw), F32), _sds((d, mw), F32), _sds((1, d), F32)],
    )(dk, dv, w_xk_t, w_xv_t, hm, mem, g_mem)


def _mix_out_bwd(dx1, y1, g_post, w_out_t, o_fox, o_dil):
    s_len, d = dx1.shape
    hw = HEAD_WIDTH
    tm = ROW_TILE
    head_of = np.arange(hw) // HEAD_DIM
    ones = jnp.asarray((head_of[:, None] == head_of[None, :]).astype(np.float32))

    def body(dx_ref, y_ref, g_ref, w_ref, of_ref, od_ref, ones_ref, dy_ref, dof_ref, dod_ref, dlf_ref, dld_ref, dg_ref):
        @pl.when(pl.program_id(0) == 0)
        def _():
            dg_ref[...] = jnp.zeros_like(dg_ref)

        dy, dg = _rms_bwd(y_ref[...], g_ref[...], dx_ref[...])
        dg_ref[...] += dg
        dyb = dy.astype(BF16)
        dy_ref[...] = dyb
        do = _dot(dyb, w_ref[...])
        dof_ref[...] = do[:, 0:hw].astype(BF16)
        dod_ref[...] = do[:, hw:2 * hw].astype(BF16)
        dlf_ref[...] = jnp.dot(do[:, 0:hw] * of_ref[...], ones_ref[...], precision=HIGHEST, preferred_element_type=F32)
        dld_ref[...] = jnp.dot(do[:, hw:2 * hw] * od_ref[...], ones_ref[...], precision=HIGHEST, preferred_element_type=F32)

    half = _rows(tm, hw)
    return _call(
        body, name="mix_out_bwd", grid=(s_len // tm,),
        in_specs=[_rows(tm, d), _rows(tm, d), _resident((1, d)), _resident(w_out_t.shape), half, half, _resident((hw, hw))],
        out_specs=[_rows(tm, d), half, half, half, half, _acc_out((1, d))],
        out_shape=[_sds((s_len, d), BF16), _sds((s_len, hw), BF16), _sds((s_len, hw), BF16), _sds((s_len, hw), F32),
                   _sds((s_len, hw), F32), _sds((1, d), F32)],
    )(dx1, y1, g_post, w_out_t, o_fox, o_dil, ones)


def _fox_bwd_prep(aq, lse, delta):
    s_len, hw = aq.shape
    tm = ROW_TILE

    def body(aq_ref, lse_ref, dl_ref, aql_ref, ad_ref):
        jj = _lane_in_head(hw)
        l3 = _split3(_swap_halves(lse_ref[...]))
        aql_ref[...] = _place3(jj, 6, [-p for p in l3], aq_ref[...])
        d3 = _split3(_swap_halves(dl_ref[...]))
        ad_ref[...] = _place3(jj, 0, [-p for p in d3], jnp.zeros((tm, hw), BF16))

    half = _rows(tm, hw)
    return _call(
        body, name="fox_bwd_prep", grid=(s_len // tm,),
        in_specs=[half, half, half], out_specs=[half, half],
        out_shape=[_sds((s_len, hw), BF16), _sds((s_len, hw), BF16)],
    )(aq, lse, delta)


def _ones_on_first3(shape):
    jj = lax.broadcasted_iota(jnp.int32, shape, 1) % HEAD_DIM
    return jnp.where(jj < 3, 1.0, 0.0).astype(BF16)


def _fox_bwd(fqkv, do, aql, ak, ad):
    s_len = fqkv.shape[0]
    bq = bk = min(ATT_BLOCK, s_len)
    nq, nk = s_len // bq, s_len // bk

    def body(q_ref, k_ref, v_ref, do_ref, aql_ref, ak_ref, ad_ref, dq_ref, rs_ref, dk_ref, dv_ref, dc_ref,
             ka_ref, va_ref, kone_ref, r_ref, dvacc_ref, dqacc_ref):
        kj, qi = pl.program_id(1), pl.program_id(2)
        in0 = _first_head_lanes()
        not0 = jnp.logical_not(in0)
        heads = ((0, in0), (1, not0))
        rows = pl.ds(pl.multiple_of(qi * bq, bq), bq)

        @pl.when(qi == 0)
        def _():
            k2, v2, a2 = k_ref[...], v_ref[...], ak_ref[...]
            one = jnp.ones_like(k2)
            one3 = _ones_on_first3(v2.shape)
            for a, mine in heads:
                ka_ref[a] = jnp.where(mine, k2, a2)
                va_ref[a] = jnp.where(mine, v2, one3)
                kone_ref[a] = jnp.where(mine, k2, one)
            r_ref[...] = jnp.zeros_like(r_ref)
            dvacc_ref[...] = jnp.zeros_like(dvacc_ref)

        @pl.when(kj == 0)
        def _():
            for a, _ in heads:
                dqacc_ref[a, rows, :] = jnp.zeros((bq, PAIR), F32)

        def step(masked):
            q2, do2, a2, d2 = q_ref[...], do_ref[...], aql_ref[...], ad_ref[...]
            one = jnp.ones_like(q2)
            for a, mine in heads:
                doa = jnp.where(mine, do2, d2)
                s = _dot_nt(jnp.where(mine, q2, a2), ka_ref[a])
                if masked:
                    s = jnp.where(_lower_triangle(bq), s, NEG)
                p = jnp.exp(s)
                ds = (p * _dot_nt(doa, va_ref[a])).astype(BF16)
                dvacc_ref[a] += _dot_tn(p.astype(BF16), doa)
                r_ref[a] += _dot_tn(ds, jnp.where(mine, q2, one))
                dqacc_ref[a, rows, :] += _dot(ds, kone_ref[a])

        _on_blocks(qi, kj, step)

        @pl.when(qi == kj)
        def _():
            acc0, acc1 = dqacc_ref[0, rows, :], dqacc_ref[1, rows, :]
            dq_ref[...] = (_pick(in0, acc0, acc1) * QK_SCALE).astype(BF16)
            rs_ref[...] = pltpu.roll(_pick(in0, acc1, acc0), HEAD_DIM, 1)

        @pl.when(qi == nq - 1)
        def _():
            dk_ref[...] = _pick(in0, r_ref[0], r_ref[1]).astype(BF16)
            dv_ref[...] = _pick(in0, dvacc_ref[0], dvacc_ref[1]).astype(BF16)
            dc_ref[...] = -pltpu.roll(_pick(in0, r_ref[1], r_ref[0]), HEAD_DIM, 1)

    blk = lambda n: (n, PAIR)
    qmap = lambda p, j, i: (jnp.maximum(i, j), p)
    kvmap = lambda p, j, i: (j, p)
    return pl.pallas_call(
        body, name="fox_bwd", grid=(N_PAIRS, nk, nq),
        in_specs=[
            pl.BlockSpec(blk(bq), qmap),
            pl.BlockSpec(blk(bk), lambda p, j, i: (j, N_PAIRS + p)),
            pl.BlockSpec(blk(bk), lambda p, j, i: (j, 2 * N_PAIRS + p)),
            pl.BlockSpec(blk(bq), qmap), pl.BlockSpec(blk(bq), qmap),
            pl.BlockSpec(blk(bk), kvmap),
            pl.BlockSpec(blk(bq), qmap),
        ],
        out_specs=[pl.BlockSpec(blk(bk), kvmap)] * 5,
        out_shape=[_sds((s_len, HEAD_WIDTH), BF16), _sds((s_len, HEAD_WIDTH), F32), _sds((s_len, HEAD_WIDTH), BF16),
                   _sds((s_len, HEAD_WIDTH), BF16), _sds((s_len, HEAD_WIDTH), F32)],
        scratch_shapes=[pltpu.VMEM((2, bk, PAIR), BF16), pltpu.VMEM((2, bk, PAIR), BF16), pltpu.VMEM((2, bk, PAIR), BF16),
                        pltpu.VMEM((2, bk, PAIR), F32), pltpu.VMEM((2, bk, PAIR), F32), pltpu.VMEM((2, s_len, PAIR), F32)],
        compiler_params=pltpu.CompilerParams(dimension_semantics=("arbitrary",) * 3, vmem_limit_bytes=V7X_VMEM_LIMIT_BYTES),
    )(fqkv, fqkv, fqkv, do, aql, ak, ad)


def _gate_bwd(rs, dc, gx, b_exp):
    s_len, hw = gx.shape
    t = min(SCAN_TILE, s_len)
    nt = s_len // t
    tri = jnp.asarray(np.triu(np.ones((t, t), np.float32)))

    def body(rs_ref, dc_ref, gx_ref, b_ref, tri_ref, dgx_ref, db_ref, carry):
        @pl.when(pl.program_id(0) == 0)
        def _():
            carry[...] = jnp.zeros_like(carry)
            db_ref[...] = jnp.zeros_like(db_ref)

        dlf = jnp.dot(tri_ref[...], rs_ref[...] + dc_ref[...], precision=HIGHEST, preferred_element_type=F32) + carry[...]
        carry[...] = dlf[0:1, :]
        dgate = dlf * jax.nn.sigmoid(-(gx_ref[...] + b_ref[...]))
        db_ref[...] += jnp.sum(dgate, axis=0, keepdims=True)
        lane = lax.broadcasted_iota(jnp.int32, (1, hw), 1)
        dgx_ref[...] = jnp.where(lane % HEAD_DIM == 0, dgate, 0.0).astype(BF16)

    rev = lambda i: (nt - 1 - i, 0)
    return _call(
        body, name="gate_bwd", grid=(nt,),
        in_specs=[pl.BlockSpec((t, hw), rev)] * 3 + [_resident((1, hw)), _resident((t, t))],
        out_specs=[pl.BlockSpec((t, hw), rev), _acc_out((1, hw))],
        out_shape=[_sds((s_len, hw), BF16), _sds((1, hw), F32)],
        scratch=[pltpu.VMEM((1, hw), F32)],
    )(rs, dc, gx, b_exp, tri)


def _dil_bwd(dqkv, do, lj, delta, bias, branch):
    dil = DILATIONS[branch]
    s_len = dqkv.shape[0]
    w = DIL_BLOCK
    hw = HEAD_WIDTH
    length = s_len // dil
    nb = length // w
    qkv_v = dqkv.reshape(length, dil * 3 * hw)
    do_v = do.reshape(length, dil * hw)
    dl_v = delta.reshape(length, dil * hw)
    lj_v = lj.reshape(length, dil * hw)

    def body(q0_ref, q1_ref, kp_ref, kc_ref, vp_ref, vc_ref, do0_ref, do1_ref, l0_ref, l1_ref, d0_ref, d1_ref, b_ref,
             dq_ref, dk_ref, dv_ref, dsum_ref):
        r, n = pl.program_id(0), pl.program_id(1)
        in0 = _first_head_lanes()
        not0 = jnp.logical_not(in0)
        first = n == 0
        last = n == nb - 1

        @pl.when(jnp.logical_and(r == 0, n == 0))
        def _():
            dsum_ref[...] = jnp.zeros_like(dsum_ref)

        pairs = [slice(pr * PAIR, (pr + 1) * PAIR) for pr in range(N_PAIRS)]

        def both_heads(ref, sl):
            v = ref[:, sl]
            return jnp.concatenate([_zero_other(in0, v), _zero_other(not0, v)], axis=0)

        def head_columns(ref):
            return jnp.concatenate([ref[:, h * HEAD_DIM:h * HEAD_DIM + 1] for h in range(N_HEADS)], axis=0)

        qq0 = [both_heads(q0_ref, sl) for sl in pairs]
        qq1 = [both_heads(q1_ref, sl) for sl in pairs]
        dd0 = [both_heads(do0_ref, sl) for sl in pairs]
        dd1 = [both_heads(do1_ref, sl) for sl in pairs]
        stack = lambda tiles: jnp.concatenate(tiles, axis=0)
        s_a = stack([_dot_nt(qq0[i], kp_ref[:, sl]) for i, sl in enumerate(pairs)])
        s_b = stack([_dot_nt(qq0[i], kc_ref[:, sl]) for i, sl in enumerate(pairs)])
        s_c = stack([_dot_nt(qq1[i], kc_ref[:, sl]) for i, sl in enumerate(pairs)])
        dp_a = stack([_dot_nt(dd0[i], vp_ref[:, sl]) for i, sl in enumerate(pairs)])
        dp_b = stack([_dot_nt(dd0[i], vc_ref[:, sl]) for i, sl in enumerate(pairs)])
        dp_c = stack([_dot_nt(dd1[i], vc_ref[:, sl]) for i, sl in enumerate(pairs)])
        bias2 = b_ref[...].reshape(N_HEADS * w, 2 * w)
        b_prev, b_cur = bias2[:, 0:w], bias2[:, w:2 * w]
        lse0, lse1 = head_columns(l0_ref), head_columns(l1_ref)
        dl0, dl1 = head_columns(d0_ref), head_columns(d1_ref)
        p_a = jnp.exp(jnp.where(first, NEG, s_a + b_prev) - lse0)
        p_b = jnp.exp((s_b + b_cur) - lse0)
        p_c = jnp.exp(jnp.where(last, NEG, s_c + b_prev) - lse1)
        ds_a = p_a * (dp_a - dl0)
        ds_b = p_b * (dp_b - dl0)
        ds_c = p_c * (dp_c - dl1)
        dsum_ref[...] += jnp.concatenate([ds_a, ds_b], axis=1).reshape(N_HEADS, w, 2 * w)
        ds_a, ds_b, ds_c = ds_a.astype(BF16), ds_b.astype(BF16), ds_c.astype(BF16)
        p_b, p_c = p_b.astype(BF16), p_c.astype(BF16)
        for i, sl in enumerate(pairs):
            rows = slice(2 * i * w, (2 * i + 2) * w)
            dq2 = _dot(ds_a[rows], kp_ref[:, sl]) + _dot(ds_b[rows], kc_ref[:, sl])
            dq_ref[:, sl] = _pick(in0, dq2[:w], dq2[w:])
            dk_ref[:, sl] = _dot_tn(ds_b[rows], qq0[i]) + _dot_tn(ds_c[rows], qq1[i])
            dv_ref[:, sl] = _dot_tn(p_b[rows], dd0[i]) + _dot_tn(p_c[rows], dd1[i])

    prev = lambda n: jnp.maximum(n - 1, 0)
    nxt = lambda n: jnp.minimum(n + 1, nb - 1)
    blk = (w, hw)
    outs = pl.pallas_call(
        body, name=f"dil_bwd_{dil}", grid=(dil, nb),
        in_specs=[
            pl.BlockSpec(blk, lambda r, n: (n, 3 * r)),
            pl.BlockSpec(blk, lambda r, n: (nxt(n), 3 * r)),
            pl.BlockSpec(blk, lambda r, n: (prev(n), 3 * r + 1)),
            pl.BlockSpec(blk, lambda r, n: (n, 3 * r + 1)),
            pl.BlockSpec(blk, lambda r, n: (prev(n), 3 * r + 2)),
            pl.BlockSpec(blk, lambda r, n: (n, 3 * r + 2)),
            pl.BlockSpec(blk, lambda r, n: (n, r)),
            pl.BlockSpec(blk, lambda r, n: (nxt(n), r)),
            pl.BlockSpec(blk, lambda r, n: (n, r)),
            pl.BlockSpec(blk, lambda r, n: (nxt(n), r)),
            pl.BlockSpec(blk, lambda r, n: (n, r)),
            pl.BlockSpec(blk, lambda r, n: (nxt(n), r)),
            pl.BlockSpec((None, N_HEADS, w, 2 * w), lambda r, n: (branch, 0, 0, 0)),
        ],
        out_specs=[pl.BlockSpec(blk, lambda r, n: (n, r))] * 3 + [pl.BlockSpec((N_HEADS, w, 2 * w), lambda r, n: (0, 0, 0))],
        out_shape=[_sds((length, dil * hw), F32)] * 3 + [_sds((N_HEADS, w, 2 * w), F32)],
        compiler_params=pltpu.CompilerParams(dimension_semantics=("arbitrary",) * 2, vmem_limit_bytes=V7X_VMEM_LIMIT_BYTES),
    )(qkv_v, qkv_v, qkv_v, qkv_v, qkv_v, qkv_v, do_v, do_v, lj_v, lj_v, dl_v, dl_v, bias)
    return [o.reshape(s_len, hw) for o in outs[:3]] + [outs[3]]


def _rel_bias_grad(dsum, buckets):
    w = DIL_BLOCK

    def body(ds_ref, bk_ref, o_ref):
        row = lax.broadcasted_iota(jnp.int32, (N_BUCKETS, PAIR), 0)
        lane = lax.broadcasted_iota(jnp.int32, (N_BUCKETS, PAIR), 1)

        def per_bucket(b, acc):
            for p in range(len(DILATIONS)):
                hit = bk_ref[p] == b
                for h in range(N_HEADS):
                    part = jnp.where(hit, ds_ref[p, h], 0.0)
                    tot = jnp.sum(jnp.sum(part, axis=1, keepdims=True), axis=0, keepdims=True)
                    acc = acc + jnp.where(jnp.logical_and(row == b, lane == h), tot, 0.0)
            return acc

        o_ref[...] = lax.fori_loop(0, N_BUCKETS, per_bucket, jnp.zeros((N_BUCKETS, PAIR), F32))

    return pl.pallas_call(body, name="rel_bias_grad", out_shape=_sds((N_BUCKETS, PAIR), F32))(dsum, buckets)


def _in_proj_bwd(dfq, dfk, dfv, dgx, ddq, ddk, ddv, w_in_t, dx1, x, g_pre):
    s_len, d = x.shape
    hw = HEAD_WIDTH
    tm = ROW_TILE

    def body(fq, fk, fv, gx, q1, q2, q3, k1, k2, k3, v1, v2, v3, w_ref, dx_ref, x_ref, g_ref, o_ref, dp_ref, dg_ref):
        @pl.when(pl.program_id(0) == 0)
        def _():
            dg_ref[...] = jnp.zeros_like(dg_ref)

        dp_ref[:, 0:hw] = fq[...]
        dp_ref[:, hw:2 * hw] = fk[...]
        dp_ref[:, 2 * hw:3 * hw] = fv[...]
        dp_ref[:, 3 * hw:4 * hw] = gx[...]
        dp_ref[:, 4 * hw:5 * hw] = (((q1[...] + q2[...]) + q3[...]) * QK_SCALE).astype(BF16)
        dp_ref[:, 5 * hw:6 * hw] = ((k1[...] + k2[...]) + k3[...]).astype(BF16)
        dp_ref[:, 6 * hw:7 * hw] = ((v1[...] + v2[...]) + v3[...]).astype(BF16)
        dh = _dot(dp_ref[...], w_ref[...])
        dx, dg = _rms_bwd(x_ref[...], g_ref[...], dh)
        dg_ref[...] += dg
        o_ref[...] = dx_ref[...] + dx

    half = _rows(tm, hw)
    return _call(
        body, name="in_proj_bwd", grid=(s_len // tm,),
        in_specs=[half] * 13 + [_resident(w_in_t.shape), _rows(tm, d), _rows(tm, d), _resident((1, d))],
        out_specs=[_rows(tm, d), _rows(tm, 7 * hw), _acc_out((1, d))],
        out_shape=[_sds((s_len, d), F32), _sds((s_len, 7 * hw), BF16), _sds((1, d), F32)],
    )(dfq, dfk, dfv, dgx, *ddq, *ddk, *ddv, w_in_t, dx1, x, g_pre)


def _expand_w_in(w_in):
    hw = HEAD_WIDTH
    gate = jnp.repeat(w_in[:, 3 * hw:3 * hw + N_HEADS], HEAD_DIM, axis=1)
    return jnp.concatenate([w_in[:, :3 * hw], gate, w_in[:, 3 * hw + N_HEADS:]], axis=1)


def _local_step(x, mem, target, g, w_bf):
    hw = HEAD_WIDTH
    w_in_e = _expand_w_in(w_bf["w_in"])
    b_exp = jnp.repeat(g["b_f"], HEAD_DIM, axis=1)
    buckets = jnp.asarray(_dil_buckets())

    h1, fqkv, gx, dqkv = _in_proj(x, g["g_mix_pre"], w_in_e)
    aq, ak = _gate_scan(gx, b_exp)
    o_fox, lse_fox = _fox_fwd(fqkv, aq, ak)
    bias = _dil_bias(g["rel_bias"], buckets)
    branches = [_dil_fwd(dqkv, bias, p) for p in range(len(DILATIONS))]
    x1, y1, o_dil, lj = _mix_out(o_fox, [b[0] for b in branches], [b[1] for b in branches], w_bf["w_out"], x, g["g_mix_post"])
    hm, kx, vx = _mem_fwd(mem, g["g_mem"], w_bf["w_xk"], w_bf["w_xv"])
    x2, y2, h2, qx, ox = _xattn_fwd(x1, g["g_xattn_pre"], w_bf["w_xq"], kx, vx, w_bf["w_xo"], g["g_xattn_post"])
    h3, a, u, z = _ffn_up(x2, g["g_ffn_pre"], w_bf["w_gate"], w_bf["w_up"])
    y3, dx3, sq = _ffn_down_loss(z, w_bf["w_down"], x2, g["g_ffn_post"], target)

    grads = {}
    dy3, da, du, grads["g_ffn_post"] = _ffn_bwd_a(dx3, y3, g["g_ffn_post"], w_bf["w_down"].T, a, u)
    dx2, grads["g_ffn_pre"] = _ffn_bwd_b(da, du, w_bf["w_gate"].T, w_bf["w_up"].T, dx3, x2, g["g_ffn_pre"])
    grads["w_down"] = _weight_grad(z, dy3, "dw_down")
    grads["w_gate"] = _weight_grad(h3, da, "dw_gate")
    grads["w_up"] = _weight_grad(h3, du, "dw_up")
    dx1, dy2, dqx, dkx, dvx, grads["g_xattn_post"], grads["g_xattn_pre"] = _xattn_bwd(
        dx2, y2, g["g_xattn_post"], w_bf["w_xo"].T, qx, kx, vx, w_bf["w_xq"].T, x1, g["g_xattn_pre"])
    grads["w_xo"] = _weight_grad(ox, dy2, "dw_xo")
    grads["w_xq"] = _weight_grad(h2, dqx, "dw_xq")
    grads["w_xk"], grads["w_xv"], grads["g_mem"] = _mem_bwd(dkx, dvx, w_bf["w_xk"].T, w_bf["w_xv"].T, hm, mem, g["g_mem"])
    dy1, do_fox, do_dil, delta_fox, delta_dil, grads["g_mix_post"] = _mix_out_bwd(
        dx1, y1, g["g_mix_post"], w_bf["w_out"].T, o_fox, o_dil)
    grads["w_out"] = jnp.concatenate([_weight_grad(o_fox, dy1, "dw_out_fox"), _weight_grad(o_dil, dy1, "dw_out_dil")], axis=0)
    aql, ad = _fox_bwd_prep(aq, lse_fox, delta_fox)
    dfq, rs, dfk, dfv, dc = _fox_bwd(fqkv, do_fox, aql, ak, ad)
    dgx, db = _gate_bwd(rs, dc, gx, b_exp)
    grads["b_f"] = db[:, ::HEAD_DIM]
    dil = [_dil_bwd(dqkv, do_dil, lj, delta_dil, bias, p) for p in range(len(DILATIONS))]
    grads["rel_bias"] = _rel_bias_grad(jnp.stack([t[3] for t in dil]), buckets)[:, :N_HEADS]
    grad_x, dproj, grads["g_mix_pre"] = _in_proj_bwd(
        dfq, dfk, dfv, dgx, [t[0] for t in dil], [t[1] for t in dil], [t[2] for t in dil], w_in_e.T, dx1, x, g["g_mix_pre"])
    dw_in_e = _weight_grad(h1, dproj, "dw_in")
    grads["w_in"] = jnp.concatenate(
        [dw_in_e[:, :3 * hw], dw_in_e[:, 3 * hw:4 * hw:HEAD_DIM], dw_in_e[:, 4 * hw:]], axis=1)
    return sq, grad_x, grads


MESH = pl.DeviceIdType.MESH


def _my_place():
    return lax.axis_index("x"), lax.axis_index("y"), lax.axis_index("c")


def _all_gather(x, name):
    rows, lanes = x.shape

    def body(x_ref, out_ref, send_sems, recv_sems, local_sem):
        mx, my, mc = _my_place()
        me, sibling = (mx, my, mc), (mx, my, 1 - mc)
        chips = [(1 - mx, my), (mx, 1 - my), (1 - mx, 1 - my)]

        def slot(px, py, pc):
            return out_ref.at[4 * px + 2 * py + pc]

        def copy(k, block, to, src=None):
            return pltpu.make_async_remote_copy(
                src_ref=slot(*block) if src is None else src, dst_ref=slot(*block),
                send_sem=send_sems.at[k], recv_sem=recv_sems.at[k], device_id=to, device_id_type=MESH)

        mine = pltpu.make_async_copy(x_ref, slot(*me), local_sem)
        mine.start()
        first = [copy(0, me, sibling, src=x_ref)]
        first += [copy(1 + j, me, (*chip, mc), src=x_ref) for j, chip in enumerate(chips)]
        for cp in first:
            cp.start()
        passed = [copy(4 + j, (*chip, mc), sibling) for j, chip in enumerate(chips)]
        for j, chip in enumerate(chips):
            copy(1 + j, (*chip, mc), me).wait_recv()
            passed[j].start()
        copy(0, sibling, me).wait_recv()
        for j, chip in enumerate(chips):
            copy(4 + j, (*chip, 1 - mc), me).wait_recv()
        for cp in first + passed:
            cp.wait_send()
        mine.wait()

    return pl.pallas_call(
        body, name=name, out_shape=_sds((N_DEV, rows, lanes), x.dtype),
        in_specs=[pl.BlockSpec(memory_space=pl.ANY)], out_specs=pl.BlockSpec(memory_space=pl.ANY),
        scratch_shapes=[pltpu.SemaphoreType.DMA((N_DEV - 1,)), pltpu.SemaphoreType.DMA((N_DEV - 1,)), pltpu.SemaphoreType.DMA],
    )(x)


def _exchange(g, name):
    _, rows, lanes = g.shape

    def body(g_ref, land_ref, send_sems, recv_sems, local_sem):
        mx, my, mc = _my_place()
        me = 4 * mx + 2 * my + mc
        mine = pltpu.make_async_copy(g_ref.at[me], land_ref.at[me], local_sem)
        mine.start()
        sent, arriving = [], []
        for k in (1, 2, 3, 4, 5, 6, 7):
            px = 1 - mx if k & 4 else mx
            py = 1 - my if k & 2 else my
            pc = 1 - mc if k & 1 else mc
            peer = 4 * px + 2 * py + pc
            cp = pltpu.make_async_remote_copy(
                src_ref=g_ref.at[peer], dst_ref=land_ref.at[me], send_sem=send_sems.at[k - 1], recv_sem=recv_sems.at[k - 1],
                device_id=(px, py, pc), device_id_type=MESH)
            cp.start()
            sent.append(cp)
            arriving.append(pltpu.make_async_remote_copy(
                src_ref=g_ref.at[me], dst_ref=land_ref.at[peer], send_sem=send_sems.at[k - 1], recv_sem=recv_sems.at[k - 1],
                device_id=(px, py, pc), device_id_type=MESH))
        for cp in arriving:
            cp.wait_recv()
        for cp in sent:
            cp.wait_send()
        mine.wait()

    return pl.pallas_call(
        body, name=name, out_shape=_sds(g.shape, g.dtype),
        in_specs=[pl.BlockSpec(memory_space=pl.ANY)], out_specs=pl.BlockSpec(memory_space=pl.ANY),
        scratch_shapes=[pltpu.SemaphoreType.DMA((N_DEV - 1,)), pltpu.SemaphoreType.DMA((N_DEV - 1,)), pltpu.SemaphoreType.DMA],
    )(g)


def _sum_slots(parts, name):
    n, rows, lanes = parts.shape
    tr = 512 if rows % 512 == 0 else rows

    def body(p_ref, o_ref):
        acc = p_ref[0].astype(F32)
        for j in range(1, n):
            acc = acc + p_ref[j].astype(F32)
        o_ref[...] = acc

    return _call(
        body, name=name, grid=(rows // tr,),
        in_specs=[pl.BlockSpec((n, tr, lanes), lambda i: (0, i, 0))], out_specs=_rows(tr, lanes),
        out_shape=_sds((rows, lanes), F32),
    )(parts)


def _adamw(w, g, m, v, name):
    def body(w_ref, g_ref, m_ref, v_ref, d_ref, nm_ref, nv_ref):
        gv = g_ref[...]
        m_new = ADAM_B1 * m_ref[...] + (1.0 - ADAM_B1) * gv
        v_new = ADAM_B2 * v_ref[...] + (1.0 - ADAM_B2) * (gv * gv)
        nm_ref[...] = m_new
        nv_ref[...] = v_new
        m_hat = m_new / (1.0 - ADAM_B1 ** ADAM_STEP)
        v_hat = v_new / (1.0 - ADAM_B2 ** ADAM_STEP)
        d_ref[...] = -ADAM_LR * (m_hat / (jnp.sqrt(v_hat) + ADAM_EPS) + ADAM_WD * w_ref[...])

    out = _sds(w.shape, F32)
    return pl.pallas_call(
        body, name=name, out_shape=[out, out, out],
        compiler_params=pltpu.CompilerParams(vmem_limit_bytes=V7X_VMEM_LIMIT_BYTES),
    )(w, g, m, v)


def _loss_head(sq, d_model):
    def body(sq_ref, o_ref):
        tot = jnp.sum(jnp.sum(sq_ref[...], axis=1, keepdims=True), axis=0, keepdims=True)
        o_ref[...] = 0.5 * (tot / d_model)

    return pl.pallas_call(body, name="loss_head", out_shape=_sds((1, 1), F32))(sq)


_BIG = (("w_in", 1), ("w_out", 0), ("w_xq", 0), ("w_xk", 0), ("w_xv", 0), ("w_xo", 1), ("w_gate", 1), ("w_up", 1), ("w_down", 0))
_SMALL = ("g_mix_pre", "b_f", "rel_bias", "g_mix_post", "g_xattn_pre", "g_mem", "g_xattn_post", "g_ffn_pre", "g_ffn_post")
LANES = 128
BIG_ROW_ALIGN = 512


def _round_up(n, k):
    return -(-n // k) * k


def _pack_rows(flat_parts, row_align, dtype):
    starts, rows, padded = [], 0, []
    for p in flat_parts:
        n = _round_up(p.shape[0], LANES)
        starts.append(rows)
        rows += n // LANES
        padded.append(jnp.pad(p.astype(dtype), (0, n - p.shape[0])))
    total = _round_up(rows, row_align)
    padded.append(jnp.zeros(((total - rows) * LANES,), dtype))
    return jnp.concatenate(padded).reshape(total, LANES), starts


def _unpack_rows(buf, starts, shapes):
    lead = buf.shape[:-2]
    flat = buf.reshape(lead + (-1,))
    out = []
    for st, shp in zip(starts, shapes):
        n = int(np.prod(shp))
        out.append(flat[..., st * LANES:st * LANES + n].reshape(lead + tuple(shp)))
    return out


def kernel(x, mem, g_mix_pre, w_in, b_f, rel_bias, w_out, g_mix_post, g_xattn_pre, g_mem, w_xq, w_xk, w_xv, w_xo, g_xattn_post, g_ffn_pre, w_gate, w_up, w_down, g_ffn_post, loss_target, m_g_mix_pre, m_w_in, m_b_f, m_rel_bias, m_w_out, m_g_mix_post, m_g_xattn_pre, m_g_mem, m_w_xq, m_w_xk, m_w_xv, m_w_xo, m_g_xattn_post, m_g_ffn_pre, m_w_gate, m_w_up, m_w_down, m_g_ffn_post, v_g_mix_pre, v_w_in, v_b_f, v_rel_bias, v_w_out, v_g_mix_post, v_g_xattn_pre, v_g_mem, v_w_xq, v_w_xk, v_w_xv, v_w_xo, v_g_xattn_post, v_g_ffn_pre, v_w_gate, v_w_up, v_w_down, v_g_ffn_post):
    given = dict(locals())
    order = ("g_mix_pre", "w_in", "b_f", "rel_bias", "w_out", "g_mix_post", "g_xattn_pre", "g_mem", "w_xq", "w_xk", "w_xv",
             "w_xo", "g_xattn_post", "g_ffn_pre", "w_gate", "w_up", "w_down", "g_ffn_post")
    two_d = lambda a: a.reshape(a.shape[-2:])
    w_loc = {n: two_d(given[n]) for n in order}
    m_loc = {n: two_d(given["m_" + n]) for n in order}
    v_loc = {n: two_d(given["v_" + n]) for n in order}
    d_model = x.shape[-1]

    shard_shapes = [w_loc[n].shape for n, _ in _BIG]
    packed, starts = _pack_rows([w_loc[n].reshape(-1) for n, _ in _BIG], BIG_ROW_ALIGN, BF16)
    gathered = _all_gather(packed, "gather_weights")
    w_bf = {}
    for (n, axis), part in zip(_BIG, _unpack_rows(gathered, starts, shard_shapes)):
        r, c = part.shape[1:]
        w_bf[n] = part.reshape(N_DEV * r, c) if axis == 0 else part.transpose(1, 0, 2).reshape(r, N_DEV * c)

    small = {n: w_loc[n] for n in _SMALL}
    sq, grad_x, grads = _local_step(two_d(x), two_d(mem), two_d(loss_target), small, w_bf)

    per_owner = []
    for (n, axis), shp in zip(_BIG, shard_shapes):
        r, c = shp
        gfull = grads[n]
        per_owner.append(gfull.reshape(N_DEV, r * c) if axis == 0 else gfull.reshape(r, N_DEV, c).transpose(1, 0, 2).reshape(N_DEV, r * c))
    rows_big = packed.shape[0]
    slots = []
    for j in range(N_DEV):
        buf, _ = _pack_rows([p[j] for p in per_owner], BIG_ROW_ALIGN, BF16)
        slots.append(buf)
    landed = _exchange(jnp.stack(slots), "exchange_grads")
    g_big = dict(zip([n for n, _ in _BIG], _unpack_rows(_sum_slots(landed, "sum_grads"), starts, shard_shapes)))
    assert landed.shape[1] == rows_big

    small_parts = [grads[n].reshape(-1) for n in _SMALL] + [sq.reshape(-1)]
    small_shapes = [w_loc[n].shape for n in _SMALL] + [sq.shape]
    spacked, sstarts = _pack_rows(small_parts, 8, F32)
    ssum = _sum_slots(_all_gather(spacked, "gather_small"), "sum_small")
    g_small = dict(zip(_SMALL, _unpack_rows(ssum, sstarts, small_shapes)[:-1]))
    sq_rows = sq.size // LANES
    loss = _loss_head(ssum[sstarts[-1]:sstarts[-1] + sq_rows], d_model).reshape(())

    g_loc, delta, new_m, new_v = {}, {}, {}, {}
    for n, _ in _BIG:
        g_loc[n] = g_big[n]
        delta[n], new_m[n], new_v[n] = _adamw(w_loc[n], g_big[n], m_loc[n], v_loc[n], "adamw_" + n)
    pk = lambda d: _pack_rows([d[n].reshape(-1) for n in _SMALL], 8, F32)[0]
    pstarts = _pack_rows([w_loc[n].reshape(-1) for n in _SMALL], 8, F32)[1]
    d_s, m_s, v_s = _adamw(pk(w_loc), pk(g_small), pk(m_loc), pk(v_loc), "adamw_small")
    shapes_s = [w_loc[n].shape for n in _SMALL]
    for n, dd, mm, vv in zip(_SMALL, _unpack_rows(d_s, pstarts, shapes_s), _unpack_rows(m_s, pstarts, shapes_s),
                             _unpack_rows(v_s, pstarts, shapes_s)):
        g_loc[n], delta[n], new_m[n], new_v[n] = g_small[n], dd, mm, vv

    like = lambda d: [d[n].reshape(given[n].shape) for n in order]
    return (loss, grad_x.reshape(x.shape), *like(g_loc), *like(delta), *like(new_m), *like(new_v))
```

```python
import functools

import numpy as np
import jax
import jax.numpy as jnp
from jax import lax
from jax.experimental import pallas as pl
from jax.experimental.pallas import tpu as pltpu

F32 = jnp.float32
BF16 = jnp.bfloat16
HIGHEST = lax.Precision.HIGHEST

RMS_EPS = 1e-6
HEAD_DIM = 64
N_HEADS = 8
HEAD_WIDTH = N_HEADS * HEAD_DIM
PAIR = 2 * HEAD_DIM
N_PAIRS = N_HEADS // 2
DIL_BLOCK = 128
DILATIONS = (1, 4, 16)
N_BUCKETS = 32
MAX_DISTANCE = 2048
N_MEM_HEADS = 4
QK_SCALE = HEAD_DIM ** -0.5
NEG = -1e30
FOX_SKIP_MARGIN = 110.0
N_DEV = 8

ADAM_LR = 0.001
ADAM_B1 = 0.9
ADAM_B2 = 0.999
ADAM_EPS = 1e-08
ADAM_WD = 0.01
ADAM_STEP = 10

V7X_VMEM_LIMIT_BYTES = 56 * 2 ** 20
ROW_TILE = 256
ATT_BLOCK = 512
SCAN_TILE = 256


def _call(body, *, name, grid, in_specs, out_specs, out_shape, scratch=()):
    return pl.pallas_call(
        body, name=name, grid=grid, in_specs=in_specs, out_specs=out_specs, out_shape=out_shape,
        scratch_shapes=list(scratch),
        compiler_params=pltpu.CompilerParams(
            dimension_semantics=("arbitrary",) * len(grid), vmem_limit_bytes=V7X_VMEM_LIMIT_BYTES))


def _rows(tm, n):
    return pl.BlockSpec((tm, n), lambda i: (i, 0))


def _resident(shape):
    zeros = (0,) * len(shape)
    return pl.BlockSpec(shape, lambda i: zeros, pipeline_mode=pl.Buffered(1))


def _acc_out(shape):
    zeros = (0,) * len(shape)
    return pl.BlockSpec(shape, lambda i: zeros)


def _sds(shape, dtype):
    return jax.ShapeDtypeStruct(shape, dtype)


def _dot(a, b):
    return jnp.dot(a, b, preferred_element_type=F32)


def _dot_nt(a, b):
    return lax.dot_general(a, b, (((1,), (1,)), ((), ())), preferred_element_type=F32)


def _dot_tn(a, b):
    return lax.dot_general(a, b, (((0,), (0,)), ((), ())), preferred_element_type=F32)


def _rms_fwd(x, g):
    r = lax.rsqrt(jnp.mean(x * x, axis=-1, keepdims=True) + RMS_EPS)
    return (x * r) * g


def _rms_bwd(xin, g, dy):
    r = lax.rsqrt(jnp.mean(xin * xin, axis=-1, keepdims=True) + RMS_EPS)
    xhat = xin * r
    dg = jnp.sum(dy * xhat, axis=0, keepdims=True)
    dxh = dy * g
    dx = r * (dxh - xhat * jnp.mean(dxh * xhat, axis=-1, keepdims=True))
    return dx, dg


def _first_head_lanes():
    return lax.broadcasted_iota(jnp.int32, (1, PAIR), 1) < HEAD_DIM


def _pick(mask, a, b):
    return jnp.where(mask, a, b)


def _zero_other(mask, v):
    return jnp.where(mask, v, jnp.zeros_like(v))


def _in_proj(x, g, w):
    s_len, d = x.shape
    tm = ROW_TILE
    hw = HEAD_WIDTH

    def body(x_ref, g_ref, w_ref, h_ref, fqkv_ref, gx_ref, dqkv_ref):
        h = _rms_fwd(x_ref[...], g_ref[...]).astype(BF16)
        h_ref[...] = h
        proj = _dot(h, w_ref[...])
        fqkv_ref[:, 0:hw] = (proj[:, 0:hw] * QK_SCALE).astype(BF16)
        fqkv_ref[:, hw:3 * hw] = proj[:, hw:3 * hw].astype(BF16)
        gx_ref[...] = proj[:, 3 * hw:4 * hw]
        dqkv_ref[:, 0:hw] = (proj[:, 4 * hw:5 * hw] * QK_SCALE).astype(BF16)
        dqkv_ref[:, hw:3 * hw] = proj[:, 5 * hw:7 * hw].astype(BF16)

    return _call(
        body, name="in_proj", grid=(s_len // tm,),
        in_specs=[_rows(tm, d), _resident((1, d)), _resident(w.shape)],
        out_specs=[_rows(tm, d), _rows(tm, 3 * hw), _rows(tm, hw), _rows(tm, 3 * hw)],
        out_shape=[_sds((s_len, d), BF16), _sds((s_len, 3 * hw), BF16), _sds((s_len, hw), F32), _sds((s_len, 3 * hw), BF16)],
    )(x, g, w)


def _swap_halves(x):
    return jnp.concatenate([pltpu.roll(x[:, i * PAIR:(i + 1) * PAIR], HEAD_DIM, 1) for i in range(x.shape[1] // PAIR)], axis=1)


def _split3(x):
    hi = x.astype(BF16)
    r = x - hi.astype(F32)
    mid = r.astype(BF16)
    lo = (r - mid.astype(F32)).astype(BF16)
    return hi, mid, lo


def _lane_in_head(width):
    return lax.broadcasted_iota(jnp.int32, (1, width), 1) % HEAD_DIM


def _place3(jj, first, pieces, base):
    out = base
    for i, p in enumerate(pieces):
        out = jnp.where(jj == first + i, p, out)
    return out


def _gate_scan(gx, b_exp):
    s_len, hw = gx.shape
    t = min(SCAN_TILE, s_len)
    tri = jnp.asarray(np.tril(np.ones((t, t), np.float32)))

    def body(gx_ref, b_ref, tri_ref, aq_ref, ak_ref, c_ref, carry):
        @pl.when(pl.program_id(0) == 0)
        def _():
            carry[...] = jnp.zeros_like(carry)

        z = gx_ref[...] + b_ref[...]
        lf = jnp.minimum(z, 0.0) - jnp.log1p(jnp.exp(-jnp.abs(z)))
        c = jnp.dot(tri_ref[...], lf, precision=HIGHEST, preferred_element_type=F32) + carry[...]
        carry[...] = c[t - 1:t, :]
        c_ref[...] = c
        hi, mid, lo = _split3(_swap_halves(c))
        jj = _lane_in_head(hw)
        zero = jnp.zeros_like(hi)
        one = jnp.ones_like(hi)
        aq_ref[...] = _place3(jj, 0, (hi, mid, lo), jnp.where(jj < 6, one, zero))
        ak_ref[...] = _place3(jj, 3, (-hi, -mid, -lo), jnp.where(jj < 9, one, zero))

    return _call(
        body, name="gate_scan", grid=(s_len // t,),
        in_specs=[_rows(t, hw), _resident((1, hw)), _resident((t, t))],
        out_specs=[_rows(t, hw), _rows(t, hw), _rows(t, hw)],
        out_shape=[_sds((s_len, hw), BF16), _sds((s_len, hw), BF16), _sds((s_len, hw), F32)],
        scratch=[pltpu.VMEM((1, hw), F32)],
    )(gx, b_exp, tri)


def _head_block_ones():
    head_of = np.arange(HEAD_WIDTH) // HEAD_DIM
    return jnp.asarray((head_of[:, None] == head_of[None, :]).astype(np.float32))


def _fox_block_stats(fqkv, c):
    s_len = fqkv.shape[0]
    hw = HEAD_WIDTH
    b = min(ATT_BLOCK, s_len)
    nb = s_len // b

    def body(q_ref, k_ref, c_ref, ones_ref, o_ref):
        q, k, cv = q_ref[...].astype(F32), k_ref[...].astype(F32), c_ref[...]
        seg = lambda x: jnp.dot(x, ones_ref[...], precision=HIGHEST, preferred_element_type=F32)
        col_max = lambda x: jnp.max(x, axis=0, keepdims=True)
        col_min = lambda x: jnp.min(x, axis=0, keepdims=True)
        o_ref[0] = jnp.concatenate(
            [jnp.sqrt(col_max(seg(q * q))), col_max(cv) - col_min(seg(q * k)), jnp.sqrt(col_max(seg(k * k))), col_min(cv),
             jnp.zeros((4, hw), F32)], axis=0)

    stats = _call(
        body, name="fox_block_stats", grid=(nb,),
        in_specs=[pl.BlockSpec((b, hw), lambda i: (i, 0)), pl.BlockSpec((b, hw), lambda i: (i, 1)), _rows(b, hw),
                  _resident((hw, hw))],
        out_specs=pl.BlockSpec((1, 8, hw), lambda i: (i, 0, 0)),
        out_shape=_sds((nb, 8, hw), F32),
    )(fqkv, fqkv, c, _head_block_ones())
    return jnp.transpose(stats[:, :4, ::HEAD_DIM], (1, 2, 0)).reshape(4 * N_HEADS, nb)


def _tile_needed(st_ref, pair, i, j):
    need = None
    for a in (0, 1):
        h = 2 * pair + a
        bound = st_ref[h, i] * st_ref[2 * N_HEADS + h, j] + st_ref[N_HEADS + h, i] - st_ref[3 * N_HEADS + h, j]
        need_a = jnp.logical_not(bound < -FOX_SKIP_MARGIN)
        need = need_a if need is None else jnp.logical_or(need, need_a)
    return need


def _lower_triangle(n):
    return lax.broadcasted_iota(jnp.int32, (n, n), 1) <= lax.broadcasted_iota(jnp.int32, (n, n), 0)


def _fox_fwd(fqkv, aq, ak, stats):
    s_len = fqkv.shape[0]
    bq = bk = min(ATT_BLOCK, s_len)
    nq, nk = s_len // bq, s_len // bk

    def body(st_ref, q_ref, k_ref, v_ref, aq_ref, ak_ref, o_ref, lse_ref, qa_ref, m_ref, acc_ref):
        pair, qi, back = pl.program_id(0), pl.program_id(1), pl.program_id(2)
        kj = jnp.maximum(qi - back, 0)
        in0 = _first_head_lanes()
        not0 = jnp.logical_not(in0)

        @pl.when(back == 0)
        def _():
            q2, a2 = q_ref[...], aq_ref[...]
            qa_ref[0] = jnp.where(in0, q2, a2)
            qa_ref[1] = jnp.where(in0, a2, q2)
            m_ref[...] = jnp.full_like(m_ref, NEG)
            acc_ref[...] = jnp.zeros_like(acc_ref)

        def step(masked):
            k2, v2, a2 = k_ref[...], v_ref[...], ak_ref[...]
            one = jnp.ones_like(v2)
            for a, mine in ((0, in0), (1, not0)):
                s = _dot_nt(qa_ref[a], jnp.where(mine, k2, a2))
                if masked:
                    s = jnp.where(_lower_triangle(bq), s, NEG)
                m_old = m_ref[a]
                m_new = jnp.maximum(m_old, jnp.max(s, axis=1, keepdims=True))
                p = jnp.exp(s - jnp.tile(m_new, (1, bk // PAIR))).astype(BF16)
                acc_ref[a] = jnp.exp(m_old - m_new) * acc_ref[a] + _dot(p, jnp.where(mine, v2, one))
                m_ref[a] = m_new

        @pl.when(back == 0)
        def _():
            step(True)

        @pl.when(jnp.logical_and(jnp.logical_and(back > 0, back <= qi), _tile_needed(st_ref, pair, qi, kj)))
        def _():
            step(False)

        @pl.when(back == qi)
        def _():
            acc0, acc1 = acc_ref[0], acc_ref[1]
            l2 = pltpu.roll(_pick(in0, acc1, acc0), HEAD_DIM, 1)
            o_ref[...] = _pick(in0, acc0, acc1) / l2
            lse_ref[...] = _pick(in0, m_ref[0], m_ref[1]) + jnp.log(l2)

    blk = lambda rows: (rows, PAIR)
    qmap = lambda p, i, b: (i, p)
    key_block = lambda i, b: jnp.maximum(i - b, 0)
    return pl.pallas_call(
        body, name="fox_fwd", grid=(N_PAIRS, nq, nk),
        in_specs=[
            pl.BlockSpec(memory_space=pltpu.SMEM),
            pl.BlockSpec(blk(bq), qmap),
            pl.BlockSpec(blk(bk), lambda p, i, b: (key_block(i, b), N_PAIRS + p)),
            pl.BlockSpec(blk(bk), lambda p, i, b: (key_block(i, b), 2 * N_PAIRS + p)),
            pl.BlockSpec(blk(bq), qmap),
            pl.BlockSpec(blk(bk), lambda p, i, b: (key_block(i, b), p)),
        ],
        out_specs=[pl.BlockSpec(blk(bq), qmap), pl.BlockSpec(blk(bq), qmap)],
        out_shape=[_sds((s_len, HEAD_WIDTH), F32), _sds((s_len, HEAD_WIDTH), F32)],
        scratch_shapes=[pltpu.VMEM((2, bq, PAIR), BF16), pltpu.VMEM((2, bq, PAIR), F32), pltpu.VMEM((2, bq, PAIR), F32)],
        compiler_params=pltpu.CompilerParams(dimension_semantics=("arbitrary",) * 3, vmem_limit_bytes=V7X_VMEM_LIMIT_BYTES),
    )(stats, fqkv, fqkv, fqkv, aq, ak)


def _t5_bucket(dist):
    max_exact = N_BUCKETS // 2
    d = np.maximum(dist, 1).astype(np.float32)
    large = max_exact + (np.log(d / max_exact) / np.log(MAX_DISTANCE / max_exact) * (N_BUCKETS - max_exact)).astype(np.int32)
    large = np.minimum(large, N_BUCKETS - 1)
    return np.where(dist < max_exact, dist, large).astype(np.int32)


def _dil_buckets():
    w = DIL_BLOCK
    qi = np.arange(w)[:, None]
    kj = np.arange(2 * w)[None, :]
    sub = qi + w - kj
    band = (sub >= 0) & (sub <= w)
    out = [np.where(band, _t5_bucket(np.clip(sub, 0, w) * dil), -1) for dil in DILATIONS]
    return np.stack(out).astype(np.int32)


def _dil_bias(rel_bias, buckets):
    w = DIL_BLOCK

    def body(rb_ref, bk_ref, o_ref):
        for p in range(len(DILATIONS)):
            bk = bk_ref[p]
            for h in range(N_HEADS):
                def add(b, acc):
                    return acc + jnp.where(bk == b, rb_ref[b, h], 0.0)
                acc = lax.fori_loop(0, N_BUCKETS, add, jnp.zeros((w, 2 * w), F32))
                o_ref[p, h] = jnp.where(bk < 0, NEG, acc)

    return pl.pallas_call(
        body, name="dil_bias",
        in_specs=[pl.BlockSpec(memory_space=pltpu.SMEM), pl.BlockSpec(memory_space=pltpu.VMEM)],
        out_specs=pl.BlockSpec(memory_space=pltpu.VMEM),
        out_shape=_sds((len(DILATIONS), N_HEADS, w, 2 * w), F32),
    )(rel_bias, buckets)


def _dil_fwd(dqkv, bias, branch):
    dil = DILATIONS[branch]
    s_len = dqkv.shape[0]
    w = DIL_BLOCK
    hw = HEAD_WIDTH
    length = s_len // dil
    nb = length // w
    view = dqkv.reshape(length, dil * 3 * hw)

    def body(q_ref, kc_ref, kp_ref, vc_ref, vp_ref, b_ref, o_ref, lse_ref):
        n = pl.program_id(1)
        in0 = _first_head_lanes()
        not0 = jnp.logical_not(in0)
        pairs = [slice(pr * PAIR, (pr + 1) * PAIR) for pr in range(N_PAIRS)]
        tiles = []
        for sl in pairs:
            q2 = q_ref[:, sl]
            qq = jnp.concatenate([_zero_other(in0, q2), _zero_other(not0, q2)], axis=0)
            tiles.append(jnp.concatenate([_dot_nt(qq, kp_ref[:, sl]), _dot_nt(qq, kc_ref[:, sl])], axis=1))
        s = jnp.concatenate(tiles, axis=0) + b_ref[...].reshape(N_HEADS * w, 2 * w)
        prev_half = lax.broadcasted_iota(jnp.int32, (1, 2 * w), 1) < w
        s = jnp.where(jnp.logical_and(n == 0, prev_half), NEG, s)
        m = jnp.max(s, axis=1, keepdims=True)
        e = jnp.exp(s - m)
        l = jnp.sum(e, axis=1, keepdims=True)
        p = (e / l).astype(BF16)
        lse = m + jnp.log(l)
        for pr, sl in enumerate(pairs):
            pp = p[2 * pr * w:(2 * pr + 2) * w]
            o2 = _dot(pp[:, :w], vp_ref[:, sl]) + _dot(pp[:, w:], vc_ref[:, sl])
            o_ref[:, sl] = _pick(in0, o2[:w], o2[w:])
            lse_ref[:, sl] = _pick(in0, lse[2 * pr * w:(2 * pr + 1) * w], lse[(2 * pr + 1) * w:(2 * pr + 2) * w])

    prev = lambda n: jnp.maximum(n - 1, 0)
    out = pl.pallas_call(
        body, name=f"dil_fwd_{dil}", grid=(dil, nb),
        in_specs=[
            pl.BlockSpec((w, hw), lambda r, n: (n, 3 * r)),
            pl.BlockSpec((w, hw), lambda r, n: (n, 3 * r + 1)),
            pl.BlockSpec((w, hw), lambda r, n: (prev(n), 3 * r + 1)),
            pl.BlockSpec((w, hw), lambda r, n: (n, 3 * r + 2)),
            pl.BlockSpec((w, hw), lambda r, n: (prev(n), 3 * r + 2)),
            pl.BlockSpec((None, N_HEADS, w, 2 * w), lambda r, n: (branch, 0, 0, 0)),
        ],
        out_specs=[pl.BlockSpec((w, hw), lambda r, n: (n, r)), pl.BlockSpec((w, hw), lambda r, n: (n, r))],
        out_shape=[_sds((length, dil * hw), F32), _sds((length, dil * hw), F32)],
        compiler_params=pltpu.CompilerParams(dimension_semantics=("arbitrary",) * 2, vmem_limit_bytes=V7X_VMEM_LIMIT_BYTES),
    )(view, view, view, view, view, bias)
    return out[0].reshape(s_len, hw), out[1].reshape(s_len, hw)


def _mix_out(o_fox, o_br, lse_br, w_out, x, g_post):
    s_len, d = x.shape
    hw = HEAD_WIDTH
    tm = ROW_TILE

    def body(of_ref, o1, o2, o3, l1, l2, l3, w_ref, x_ref, g_ref, x1_ref, y1_ref, od_ref, lj_ref):
        la, lb, lc = l1[...], l2[...], l3[...]
        m = jnp.maximum(jnp.maximum(la, lb), lc)
        ea, eb, ec = jnp.exp(la - m), jnp.exp(lb - m), jnp.exp(lc - m)
        tot = ea + eb + ec
        o_dil = (ea / tot) * o1[...] + (eb / tot) * o2[...] + (ec / tot) * o3[...]
        od_ref[...] = o_dil
        lj_ref[...] = m + jnp.log(tot)
        y = _dot(of_ref[...].astype(BF16), w_ref[0:hw, :]) + _dot(o_dil.astype(BF16), w_ref[hw:2 * hw, :])
        y1_ref[...] = y
        x1_ref[...] = x_ref[...] + _rms_fwd(y, g_ref[...])

    half = _rows(tm, hw)
    return _call(
        body, name="mix_out", grid=(s_len // tm,),
        in_specs=[half] * 7 + [_resident(w_out.shape), _rows(tm, d), _resident((1, d))],
        out_specs=[_rows(tm, d), _rows(tm, d), half, half],
        out_shape=[_sds((s_len, d), F32), _sds((s_len, d), F32), _sds((s_len, hw), F32), _sds((s_len, hw), F32)],
    )(o_fox, *o_br, *lse_br, w_out, x, g_post)


def _mem_fwd(mem, g_mem, w_xk, w_xv):
    n_mem, d = mem.shape
    mw = w_xk.shape[1]

    def body(mem_ref, g_ref, wk_ref, wv_ref, hm_ref, k_ref, v_ref):
        hm = _rms_fwd(mem_ref[...], g_ref[...]).astype(BF16)
        hm_ref[...] = hm
        k_ref[...] = _dot(hm, wk_ref[...]).astype(BF16)
        v_ref[...] = _dot(hm, wv_ref[...]).astype(BF16)

    return pl.pallas_call(
        body, name="mem_fwd",
        out_shape=[_sds((n_mem, d), BF16), _sds((n_mem, mw), BF16), _sds((n_mem, mw), BF16)],
    )(mem, g_mem, w_xk, w_xv)


def _xattn_softmax(qa, k2):
    s = _dot_nt(qa, k2)
    m = jnp.max(s, axis=1, keepdims=True)
    e = jnp.exp(s - m)
    return e / jnp.sum(e, axis=1, keepdims=True)


def _xattn_fwd(x1, g_pre, w_xq, kx, vx, w_xo, g_post):
    s_len, d = x1.shape
    mw = w_xq.shape[1]
    n_mem = kx.shape[0]
    tm = ROW_TILE

    def body(x_ref, gp_ref, wq_ref, k_ref, v_ref, wo_ref, go_ref, x2_ref, y2_ref, h2_ref, q_ref, o_ref):
        x = x_ref[...]
        h = _rms_fwd(x, gp_ref[...]).astype(BF16)
        h2_ref[...] = h
        q = (_dot(h, wq_ref[...]) * QK_SCALE).astype(BF16)
        q_ref[...] = q
        in0 = _first_head_lanes()
        not0 = jnp.logical_not(in0)
        for pr in range(mw // PAIR):
            sl = slice(pr * PAIR, (pr + 1) * PAIR)
            q2, k2, v2 = q[:, sl], k_ref[:, sl], v_ref[:, sl]
            oa = [_dot(_xattn_softmax(_zero_other(mine, q2), k2).astype(BF16), v2) for mine in (in0, not0)]
            o_ref[:, sl] = _pick(in0, oa[0], oa[1]).astype(BF16)
        y = _dot(o_ref[...], wo_ref[...])
        y2_ref[...] = y
        x2_ref[...] = x + _rms_fwd(y, go_ref[...])

    return _call(
        body, name="xattn_fwd", grid=(s_len // tm,),
        in_specs=[_rows(tm, d), _resident((1, d)), _resident(w_xq.shape), _resident((n_mem, mw)), _resident((n_mem, mw)),
                  _resident(w_xo.shape), _resident((1, d))],
        out_specs=[_rows(tm, d), _rows(tm, d), _rows(tm, d), _rows(tm, mw), _rows(tm, mw)],
        out_shape=[_sds((s_len, d), F32), _sds((s_len, d), F32), _sds((s_len, d), BF16), _sds((s_len, mw), BF16),
                   _sds((s_len, mw), BF16)],
    )(x1, g_pre, w_xq, kx, vx, w_xo, g_post)


def _ffn_up(x2, g_pre, w_gate, w_up):
    s_len, d = x2.shape
    dff = w_gate.shape[1]
    tm = ROW_TILE

    def body(x_ref, g_ref, wg_ref, wu_ref, h_ref, a_ref, u_ref, z_ref):
        h = _rms_fwd(x_ref[...], g_ref[...]).astype(BF16)
        h_ref[...] = h
        a = _dot(h, wg_ref[...])
        u = _dot(h, wu_ref[...])
        a_ref[...] = a.astype(BF16)
        u_ref[...] = u.astype(BF16)
        z_ref[...] = ((a * jax.nn.sigmoid(a)) * u).astype(BF16)

    return _call(
        body, name="ffn_up", grid=(s_len // tm,),
        in_specs=[_rows(tm, d), _resident((1, d)), _resident(w_gate.shape), _resident(w_up.shape)],
        out_specs=[_rows(tm, d), _rows(tm, dff), _rows(tm, dff), _rows(tm, dff)],
        out_shape=[_sds((s_len, d), BF16)] + [_sds((s_len, dff), BF16)] * 3,
    )(x2, g_pre, w_gate, w_up)


def _ffn_down_loss(z, w_down, x2, g_post, target):
    s_len, d = x2.shape
    dff = z.shape[1]
    tm = ROW_TILE

    def body(z_ref, w_ref, x_ref, g_ref, t_ref, y_ref, dx_ref, sq_ref):
        @pl.when(pl.program_id(0) == 0)
        def _():
            sq_ref[...] = jnp.zeros_like(sq_ref)

        y = _dot(z_ref[...], w_ref[...])
        y_ref[...] = y
        err = (x_ref[...] + _rms_fwd(y, g_ref[...])) - t_ref[...]
        sq_ref[...] += jnp.sum(err * err, axis=0, keepdims=True)
        dx_ref[...] = err * (1.0 / d)

    return _call(
        body, name="ffn_down_loss", grid=(s_len // tm,),
        in_specs=[_rows(tm, dff), _resident(w_down.shape), _rows(tm, d), _resident((1, d)), _rows(tm, d)],
        out_specs=[_rows(tm, d), _rows(tm, d), _acc_out((1, d))],
        out_shape=[_sds((s_len, d), F32), _sds((s_len, d), F32), _sds((1, d), F32)],
    )(z, w_down, x2, g_post, target)


def _weight_grad(a, b, name):
    s_len, k = a.shape
    n = b.shape[1]
    ts = 512 if s_len % 512 == 0 else s_len
    tn = n
    while k * tn * 4 > 8 * 2 ** 20 and tn % 256 == 0:
        tn //= 2

    def body(a_ref, b_ref, o_ref):
        @pl.when(pl.program_id(1) == 0)
        def _():
            o_ref[...] = jnp.zeros_like(o_ref)

        o_ref[...] += _dot_tn(a_ref[...].astype(BF16), b_ref[...].astype(BF16))

    return pl.pallas_call(
        body, name=name, grid=(n // tn, s_len // ts),
        in_specs=[pl.BlockSpec((ts, k), lambda j, i: (i, 0)), pl.BlockSpec((ts, tn), lambda j, i: (i, j))],
        out_specs=pl.BlockSpec((k, tn), lambda j, i: (0, j)),
        out_shape=_sds((k, n), F32),
        compiler_params=pltpu.CompilerParams(dimension_semantics=("arbitrary",) * 2, vmem_limit_bytes=V7X_VMEM_LIMIT_BYTES),
    )(a, b)


def _ffn_bwd_a(dx3, y3, g_post, w_down_t, a, u):
    s_len, d = dx3.shape
    dff = a.shape[1]
    tm = ROW_TILE

    def body(dx_ref, y_ref, g_ref, w_ref, a_ref, u_ref, dy_ref, da_ref, du_ref, dg_ref):
        @pl.when(pl.program_id(0) == 0)
        def _():
            dg_ref[...] = jnp.zeros_like(dg_ref)

        dy, dg = _rms_bwd(y_ref[...], g_ref[...], dx_ref[...])
        dg_ref[...] += dg
        dyb = dy.astype(BF16)
        dy_ref[...] = dyb
        dz = _dot(dyb, w_ref[...])
        av = a_ref[...].astype(F32)
        uv = u_ref[...].astype(F32)
        sg = jax.nn.sigmoid(av)
        da_ref[...] = (dz * uv * (sg * (1.0 + av * (1.0 - sg)))).astype(BF16)
        du_ref[...] = (dz * (av * sg)).astype(BF16)

    return _call(
        body, name="ffn_bwd_a", grid=(s_len // tm,),
        in_specs=[_rows(tm, d), _rows(tm, d), _resident((1, d)), _resident(w_down_t.shape), _rows(tm, dff), _rows(tm, dff)],
        out_specs=[_rows(tm, d), _rows(tm, dff), _rows(tm, dff), _acc_out((1, d))],
        out_shape=[_sds((s_len, d), BF16), _sds((s_len, dff), BF16), _sds((s_len, dff), BF16), _sds((1, d), F32)],
    )(dx3, y3, g_post, w_down_t, a, u)


def _ffn_bwd_b(da, du, w_gate_t, w_up_t, dx3, x2, g_pre):
    s_len, d = x2.shape
    dff = da.shape[1]
    tm = ROW_TILE

    def body(da_ref, du_ref, wg_ref, wu_ref, dx_ref, x_ref, g_ref, o_ref, dg_ref):
        @pl.when(pl.program_id(0) == 0)
        def _():
            dg_ref[...] = jnp.zeros_like(dg_ref)

        dh = _dot(da_ref[...], wg_ref[...]) + _dot(du_ref[...], wu_ref[...])
        dx, dg = _rms_bwd(x_ref[...], g_ref[...], dh)
        dg_ref[...] += dg
        o_ref[...] = dx_ref[...] + dx

    return _call(
        body, name="ffn_bwd_b", grid=(s_len // tm,),
        in_specs=[_rows(tm, dff), _rows(tm, dff), _resident(w_gate_t.shape), _resident(w_up_t.shape), _rows(tm, d),
                  _rows(tm, d), _resident((1, d))],
        out_specs=[_rows(tm, d), _acc_out((1, d))],
        out_shape=[_sds((s_len, d), F32), _sds((1, d), F32)],
    )(da, du, w_gate_t, w_up_t, dx3, x2, g_pre)


def _xattn_bwd(dx2, y2, g_post, w_xo_t, q, kx, vx, w_xq_t, x1, g_pre):
    s_len, d = x1.shape
    mw = q.shape[1]
    n_mem = kx.shape[0]
    tm = ROW_TILE

    def body(dx_ref, y_ref, go_ref, wo_ref, q_ref, k_ref, v_ref, wq_ref, x_ref, gp_ref,
             dx1_ref, dy_ref, dq_ref, dk_ref, dv_ref, dgo_ref, dgp_ref):
        @pl.when(pl.program_id(0) == 0)
        def _():
            dk_ref[...] = jnp.zeros_like(dk_ref)
            dv_ref[...] = jnp.zeros_like(dv_ref)
            dgo_ref[...] = jnp.zeros_like(dgo_ref)
            dgp_ref[...] = jnp.zeros_like(dgp_ref)

        dxin = dx_ref[...]
        dy, dgo = _rms_bwd(y_ref[...], go_ref[...], dxin)
        dgo_ref[...] += dgo
        dyb = dy.astype(BF16)
        dy_ref[...] = dyb
        do = _dot(dyb, wo_ref[...]).astype(BF16)
        in0 = _first_head_lanes()
        not0 = jnp.logical_not(in0)
        for pr in range(mw // PAIR):
            sl = slice(pr * PAIR, (pr + 1) * PAIR)
            q2, k2, v2, do2 = q_ref[:, sl], k_ref[:, sl], v_ref[:, sl], do[:, sl]
            dqs = []
            dk2 = jnp.zeros((n_mem, PAIR), F32)
            dv2 = jnp.zeros((n_mem, PAIR), F32)
            for mine in (in0, not0):
                qa = _zero_other(mine, q2)
                doa = _zero_other(mine, do2)
                p = _xattn_softmax(qa, k2)
                dp = _dot_nt(doa, v2)
                ds = (p * (dp - jnp.sum(p * dp, axis=1, keepdims=True))).astype(BF16)
                dqs.append(_dot(ds, k2))
                dk2 = dk2 + _dot_tn(ds, qa)
                dv2 = dv2 + _dot_tn(p.astype(BF16), doa)
            dq_ref[:, sl] = (_pick(in0, dqs[0], dqs[1]) * QK_SCALE).astype(BF16)
            dk_ref[:, sl] += dk2
            dv_ref[:, sl] += dv2
        dh = _dot(dq_ref[...], wq_ref[...])
        dx, dgp = _rms_bwd(x_ref[...], gp_ref[...], dh)
        dgp_ref[...] += dgp
        dx1_ref[...] = dxin + dx

    return _call(
        body, name="xattn_bwd", grid=(s_len // tm,),
        in_specs=[_rows(tm, d), _rows(tm, d), _resident((1, d)), _resident(w_xo_t.shape), _rows(tm, mw),
                  _resident((n_mem, mw)), _resident((n_mem, mw)), _resident(w_xq_t.shape), _rows(tm, d), _resident((1, d))],
        out_specs=[_rows(tm, d), _rows(tm, d), _rows(tm, mw), _acc_out((n_mem, mw)), _acc_out((n_mem, mw)),
                   _acc_out((1, d)), _acc_out((1, d))],
        out_shape=[_sds((s_len, d), F32), _sds((s_len, d), BF16), _sds((s_len, mw), BF16), _sds((n_mem, mw), F32),
                   _sds((n_mem, mw), F32), _sds((1, d), F32), _sds((1, d), F32)],
    )(dx2, y2, g_post, w_xo_t, q, kx, vx, w_xq_t, x1, g_pre)


def _mem_bwd(dk, dv, w_xk_t, w_xv_t, hm, mem, g_mem):
    n_mem, d = mem.shape
    mw = dk.shape[1]

    def body(dk_ref, dv_ref, wk_ref, wv_ref, hm_ref, mem_ref, g_ref, dwk_ref, dwv_ref, dg_ref):
        dkb = dk_ref[...].astype(BF16)
        dvb = dv_ref[...].astype(BF16)
        dhm = _dot(dkb, wk_ref[...]) + _dot(dvb, wv_ref[...])
        _, dg = _rms_bwd(mem_ref[...], g_ref[...], dhm)
        dg_ref[...] = dg
        dwk_ref[...] = _dot_tn(hm_ref[...], dkb)
        dwv_ref[...] = _dot_tn(hm_ref[...], dvb)

    return pl.pallas_call(
        body, name="mem_bwd",
        out_shape=[_sds((d, mw), F32), _sds((d, mw), F32), _sds((1, d), F32)],
    )(dk, dv, w_xk_t, w_xv_t, hm, mem, g_mem)


def _mix_out_bwd(dx1, y1, g_post, w_out_t, o_fox, o_dil):
    s_len, d = dx1.shape
    hw = HEAD_WIDTH
    tm = ROW_TILE
    head_of = np.arange(hw) // HEAD_DIM
    ones = jnp.asarray((head_of[:, None] == head_of[None, :]).astype(np.float32))

    def body(dx_ref, y_ref, g_ref, w_ref, of_ref, od_ref, ones_ref, dy_ref, dof_ref, dod_ref, dlf_ref, dld_ref, dg_ref):
        @pl.when(pl.program_id(0) == 0)
        def _():
            dg_ref[...] = jnp.zeros_like(dg_ref)

        dy, dg = _rms_bwd(y_ref[...], g_ref[...], dx_ref[...])
        dg_ref[...] += dg
        dyb = dy.astype(BF16)
        dy_ref[...] = dyb
        do = _dot(dyb, w_ref[...])
        dof_ref[...] = do[:, 0:hw].astype(BF16)
        dod_ref[...] = do[:, hw:2 * hw].astype(BF16)
        dlf_ref[...] = jnp.dot(do[:, 0:hw] * of_ref[...], ones_ref[...], precision=HIGHEST, preferred_element_type=F32)
        dld_ref[...] = jnp.dot(do[:, hw:2 * hw] * od_ref[...], ones_ref[...], precision=HIGHEST, preferred_element_type=F32)

    half = _rows(tm, hw)
    return _call(
        body, name="mix_out_bwd", grid=(s_len // tm,),
        in_specs=[_rows(tm, d), _rows(tm, d), _resident((1, d)), _resident(w_out_t.shape), half, half, _resident((hw, hw))],
        out_specs=[_rows(tm, d), half, half, half, half, _acc_out((1, d))],
        out_shape=[_sds((s_len, d), BF16), _sds((s_len, hw), BF16), _sds((s_len, hw), BF16), _sds((s_len, hw), F32),
                   _sds((s_len, hw), F32), _sds((1, d), F32)],
    )(dx1, y1, g_post, w_out_t, o_fox, o_dil, ones)


def _fox_bwd_prep(aq, lse, delta):
    s_len, hw = aq.shape
    tm = ROW_TILE

    def body(aq_ref, lse_ref, dl_ref, aql_ref, ad_ref):
        jj = _lane_in_head(hw)
        l3 = _split3(_swap_halves(lse_ref[...]))
        aql_ref[...] = _place3(jj, 6, [-p for p in l3], aq_ref[...])
        d3 = _split3(_swap_halves(dl_ref[...]))
        ad_ref[...] = _place3(jj, 0, [-p for p in d3], jnp.zeros((tm, hw), BF16))

    half = _rows(tm, hw)
    return _call(
        body, name="fox_bwd_prep", grid=(s_len // tm,),
        in_specs=[half, half, half], out_specs=[half, half],
        out_shape=[_sds((s_len, hw), BF16), _sds((s_len, hw), BF16)],
    )(aq, lse, delta)


def _ones_on_first3(shape):
    jj = lax.broadcasted_iota(jnp.int32, shape, 1) % HEAD_DIM
    return jnp.where(jj < 3, 1.0, 0.0).astype(BF16)


def _fox_bwd(fqkv, do, aql, ak, ad, stats):
    s_len = fqkv.shape[0]
    bq = bk = min(ATT_BLOCK, s_len)
    nq, nk = s_len // bq, s_len // bk

    def body(st_ref, q_ref, k_ref, v_ref, do_ref, aql_ref, ak_ref, ad_ref, dq_ref, rs_ref, dk_ref, dv_ref, dc_ref,
             ka_ref, va_ref, kone_ref, r_ref, dvacc_ref, dqacc_ref):
        pair, kj, qi = pl.program_id(0), pl.program_id(1), pl.program_id(2)
        in0 = _first_head_lanes()
        not0 = jnp.logical_not(in0)
        heads = ((0, in0), (1, not0))
        rows = pl.ds(pl.multiple_of(qi * bq, bq), bq)

        @pl.when(qi == 0)
        def _():
            k2, v2, a2 = k_ref[...], v_ref[...], ak_ref[...]
            one = jnp.ones_like(k2)
            one3 = _ones_on_first3(v2.shape)
            for a, mine in heads:
                ka_ref[a] = jnp.where(mine, k2, a2)
                va_ref[a] = jnp.where(mine, v2, one3)
                kone_ref[a] = jnp.where(mine, k2, one)
            r_ref[...] = jnp.zeros_like(r_ref)
            dvacc_ref[...] = jnp.zeros_like(dvacc_ref)

        @pl.when(kj == 0)
        def _():
            for a, _ in heads:
                dqacc_ref[a, rows, :] = jnp.zeros((bq, PAIR), F32)

        def step(masked):
            q2, do2, a2, d2 = q_ref[...], do_ref[...], aql_ref[...], ad_ref[...]
            one = jnp.ones_like(q2)
            for a, mine in heads:
                doa = jnp.where(mine, do2, d2)
                s = _dot_nt(jnp.where(mine, q2, a2), ka_ref[a])
                if masked:
                    s = jnp.where(_lower_triangle(bq), s, NEG)
                p = jnp.exp(s)
                ds = (p * _dot_nt(doa, va_ref[a])).astype(BF16)
                dvacc_ref[a] += _dot_tn(p.astype(BF16), doa)
                r_ref[a] += _dot_tn(ds, jnp.where(mine, q2, one))
                dqacc_ref[a, rows, :] += _dot(ds, kone_ref[a])

        @pl.when(jnp.logical_and(qi > kj, _tile_needed(st_ref, pair, qi, kj)))
        def _():
            step(False)

        @pl.when(qi == kj)
        def _():
            step(True)
            acc0, acc1 = dqacc_ref[0, rows, :], dqacc_ref[1, rows, :]
            dq_ref[...] = (_pick(in0, acc0, acc1) * QK_SCALE).astype(BF16)
            rs_ref[...] = pltpu.roll(_pick(in0, acc1, acc0), HEAD_DIM, 1)

        @pl.when(qi == nq - 1)
        def _():
            dk_ref[...] = _pick(in0, r_ref[0], r_ref[1]).astype(BF16)
            dv_ref[...] = _pick(in0, dvacc_ref[0], dvacc_ref[1]).astype(BF16)
            dc_ref[...] = -pltpu.roll(_pick(in0, r_ref[1], r_ref[0]), HEAD_DIM, 1)

    blk = lambda n: (n, PAIR)
    qmap = lambda p, j, i: (jnp.maximum(i, j), p)
    kvmap = lambda p, j, i: (j, p)
    return pl.pallas_call(
        body, name="fox_bwd", grid=(N_PAIRS, nk, nq),
        in_specs=[
            pl.BlockSpec(memory_space=pltpu.SMEM),
            pl.BlockSpec(blk(bq), qmap),
            pl.BlockSpec(blk(bk), lambda p, j, i: (j, N_PAIRS + p)),
            pl.BlockSpec(blk(bk), lambda p, j, i: (j, 2 * N_PAIRS + p)),
            pl.BlockSpec(blk(bq), qmap), pl.BlockSpec(blk(bq), qmap),
            pl.BlockSpec(blk(bk), kvmap),
            pl.BlockSpec(blk(bq), qmap),
        ],
        out_specs=[pl.BlockSpec(blk(bk), kvmap)] * 5,
        out_shape=[_sds((s_len, HEAD_WIDTH), BF16), _sds((s_len, HEAD_WIDTH), F32), _sds((s_len, HEAD_WIDTH), BF16),
                   _sds((s_len, HEAD_WIDTH), BF16), _sds((s_len, HEAD_WIDTH), F32)],
        scratch_shapes=[pltpu.VMEM((2, bk, PAIR), BF16), pltpu.VMEM((2, bk, PAIR), BF16), pltpu.VMEM((2, bk, PAIR), BF16),
                        pltpu.VMEM((2, bk, PAIR), F32), pltpu.VMEM((2, bk, PAIR), F32), pltpu.VMEM((2, s_len, PAIR), F32)],
        compiler_params=pltpu.CompilerParams(dimension_semantics=("arbitrary",) * 3, vmem_limit_bytes=V7X_VMEM_LIMIT_BYTES),
    )(stats, fqkv, fqkv, fqkv, do, aql, ak, ad)


def _gate_bwd(rs, dc, gx, b_exp):
    s_len, hw = gx.shape
    t = min(SCAN_TILE, s_len)
    nt = s_len // t
    tri = jnp.asarray(np.triu(np.ones((t, t), np.float32)))

    def body(rs_ref, dc_ref, gx_ref, b_ref, tri_ref, dgx_ref, db_ref, carry):
        @pl.when(pl.program_id(0) == 0)
        def _():
            carry[...] = jnp.zeros_like(carry)
            db_ref[...] = jnp.zeros_like(db_ref)

        dlf = jnp.dot(tri_ref[...], rs_ref[...] + dc_ref[...], precision=HIGHEST, preferred_element_type=F32) + carry[...]
        carry[...] = dlf[0:1, :]
        dgate = dlf * jax.nn.sigmoid(-(gx_ref[...] + b_ref[...]))
        db_ref[...] += jnp.sum(dgate, axis=0, keepdims=True)
        lane = lax.broadcasted_iota(jnp.int32, (1, hw), 1)
        dgx_ref[...] = jnp.where(lane % HEAD_DIM == 0, dgate, 0.0).astype(BF16)

    rev = lambda i: (nt - 1 - i, 0)
    return _call(
        body, name="gate_bwd", grid=(nt,),
        in_specs=[pl.BlockSpec((t, hw), rev)] * 3 + [_resident((1, hw)), _resident((t, t))],
        out_specs=[pl.BlockSpec((t, hw), rev), _acc_out((1, hw))],
        out_shape=[_sds((s_len, hw), BF16), _sds((1, hw), F32)],
        scratch=[pltpu.VMEM((1, hw), F32)],
    )(rs, dc, gx, b_exp, tri)


def _dil_bwd(dqkv, do, lj, delta, bias, branch):
    dil = DILATIONS[branch]
    s_len = dqkv.shape[0]
    w = DIL_BLOCK
    hw = HEAD_WIDTH
    length = s_len // dil
    nb = length // w
    qkv_v = dqkv.reshape(length, dil * 3 * hw)
    do_v = do.reshape(length, dil * hw)
    dl_v = delta.reshape(length, dil * hw)
    lj_v = lj.reshape(length, dil * hw)

    def body(q0_ref, q1_ref, kp_ref, kc_ref, vp_ref, vc_ref, do0_ref, do1_ref, l0_ref, l1_ref, d0_ref, d1_ref, b_ref,
             dq_ref, dk_ref, dv_ref, dsum_ref):
        r, n = pl.program_id(0), pl.program_id(1)
        in0 = _first_head_lanes()
        not0 = jnp.logical_not(in0)
        first = n == 0
        last = n == nb - 1

        @pl.when(jnp.logical_and(r == 0, n == 0))
        def _():
            dsum_ref[...] = jnp.zeros_like(dsum_ref)

        pairs = [slice(pr * PAIR, (pr + 1) * PAIR) for pr in range(N_PAIRS)]

        def both_heads(ref, sl):
            v = ref[:, sl]
            return jnp.concatenate([_zero_other(in0, v), _zero_other(not0, v)], axis=0)

        def head_columns(ref):
            return jnp.concatenate([ref[:, h * HEAD_DIM:h * HEAD_DIM + 1] for h in range(N_HEADS)], axis=0)

        qq0 = [both_heads(q0_ref, sl) for sl in pairs]
        qq1 = [both_heads(q1_ref, sl) for sl in pairs]
        dd0 = [both_heads(do0_ref, sl) for sl in pairs]
        dd1 = [both_heads(do1_ref, sl) for sl in pairs]
        stack = lambda tiles: jnp.concatenate(tiles, axis=0)
        s_a = stack([_dot_nt(qq0[i], kp_ref[:, sl]) for i, sl in enumerate(pairs)])
        s_b = stack([_dot_nt(qq0[i], kc_ref[:, sl]) for i, sl in enumerate(pairs)])
        s_c = stack([_dot_nt(qq1[i], kc_ref[:, sl]) for i, sl in enumerate(pairs)])
        dp_a = stack([_dot_nt(dd0[i], vp_ref[:, sl]) for i, sl in enumerate(pairs)])
        dp_b = stack([_dot_nt(dd0[i], vc_ref[:, sl]) for i, sl in enumerate(pairs)])
        dp_c = stack([_dot_nt(dd1[i], vc_ref[:, sl]) for i, sl in enumerate(pairs)])
        bias2 = b_ref[...].reshape(N_HEADS * w, 2 * w)
        b_prev, b_cur = bias2[:, 0:w], bias2[:, w:2 * w]
        lse0, lse1 = head_columns(l0_ref), head_columns(l1_ref)
        dl0, dl1 = head_columns(d0_ref), head_columns(d1_ref)
        p_a = jnp.exp(jnp.where(first, NEG, s_a + b_prev) - lse0)
        p_b = jnp.exp((s_b + b_cur) - lse0)
        p_c = jnp.exp(jnp.where(last, NEG, s_c + b_prev) - lse1)
        ds_a = p_a * (dp_a - dl0)
        ds_b = p_b * (dp_b - dl0)
        ds_c = p_c * (dp_c - dl1)
        dsum_ref[...] += jnp.concatenate([ds_a, ds_b], axis=1).reshape(N_HEADS, w, 2 * w)
        ds_a, ds_b, ds_c = ds_a.astype(BF16), ds_b.astype(BF16), ds_c.astype(BF16)
        p_b, p_c = p_b.astype(BF16), p_c.astype(BF16)
        for i, sl in enumerate(pairs):
            rows = slice(2 * i * w, (2 * i + 2) * w)
            dq2 = _dot(ds_a[rows], kp_ref[:, sl]) + _dot(ds_b[rows], kc_ref[:, sl])
            dq_ref[:, sl] = _pick(in0, dq2[:w], dq2[w:])
            dk_ref[:, sl] = _dot_tn(ds_b[rows], qq0[i]) + _dot_tn(ds_c[rows], qq1[i])
            dv_ref[:, sl] = _dot_tn(p_b[rows], dd0[i]) + _dot_tn(p_c[rows], dd1[i])

    prev = lambda n: jnp.maximum(n - 1, 0)
    nxt = lambda n: jnp.minimum(n + 1, nb - 1)
    blk = (w, hw)
    outs = pl.pallas_call(
        body, name=f"dil_bwd_{dil}", grid=(dil, nb),
        in_specs=[
            pl.BlockSpec(blk, lambda r, n: (n, 3 * r)),
            pl.BlockSpec(blk, lambda r, n: (nxt(n), 3 * r)),
            pl.BlockSpec(blk, lambda r, n: (prev(n), 3 * r + 1)),
            pl.BlockSpec(blk, lambda r, n: (n, 3 * r + 1)),
            pl.BlockSpec(blk, lambda r, n: (prev(n), 3 * r + 2)),
            pl.BlockSpec(blk, lambda r, n: (n, 3 * r + 2)),
            pl.BlockSpec(blk, lambda r, n: (n, r)),
            pl.BlockSpec(blk, lambda r, n: (nxt(n), r)),
            pl.BlockSpec(blk, lambda r, n: (n, r)),
            pl.BlockSpec(blk, lambda r, n: (nxt(n), r)),
            pl.BlockSpec(blk, lambda r, n: (n, r)),
            pl.BlockSpec(blk, lambda r, n: (nxt(n), r)),
            pl.BlockSpec((None, N_HEADS, w, 2 * w), lambda r, n: (branch, 0, 0, 0)),
        ],
        out_specs=[pl.BlockSpec(blk, lambda r, n: (n, r))] * 3 + [pl.BlockSpec((N_HEADS, w, 2 * w), lambda r, n: (0, 0, 0))],
        out_shape=[_sds((length, dil * hw), F32)] * 3 + [_sds((N_HEADS, w, 2 * w), F32)],
        compiler_params=pltpu.CompilerParams(dimension_semantics=("arbitrary",) * 2, vmem_limit_bytes=V7X_VMEM_LIMIT_BYTES),
    )(qkv_v, qkv_v, qkv_v, qkv_v, qkv_v, qkv_v, do_v, do_v, lj_v, lj_v, dl_v, dl_v, bias)
    return [o.reshape(s_len, hw) for o in outs[:3]] + [outs[3]]


def _rel_bias_grad(dsum, buckets):
    w = DIL_BLOCK

    def body(ds_ref, bk_ref, o_ref):
        row = lax.broadcasted_iota(jnp.int32, (N_BUCKETS, PAIR), 0)
        lane = lax.broadcasted_iota(jnp.int32, (N_BUCKETS, PAIR), 1)

        def per_bucket(b, acc):
            for p in range(len(DILATIONS)):
                hit = bk_ref[p] == b
                for h in range(N_HEADS):
                    part = jnp.where(hit, ds_ref[p, h], 0.0)
                    tot = jnp.sum(jnp.sum(part, axis=1, keepdims=True), axis=0, keepdims=True)
                    acc = acc + jnp.where(jnp.logical_and(row == b, lane == h), tot, 0.0)
            return acc

        o_ref[...] = lax.fori_loop(0, N_BUCKETS, per_bucket, jnp.zeros((N_BUCKETS, PAIR), F32))

    return pl.pallas_call(body, name="rel_bias_grad", out_shape=_sds((N_BUCKETS, PAIR), F32))(dsum, buckets)


def _in_proj_bwd(dfq, dfk, dfv, dgx, ddq, ddk, ddv, w_in_t, dx1, x, g_pre):
    s_len, d = x.shape
    hw = HEAD_WIDTH
    tm = ROW_TILE

    def body(fq, fk, fv, gx, q1, q2, q3, k1, k2, k3, v1, v2, v3, w_ref, dx_ref, x_ref, g_ref, o_ref, dp_ref, dg_ref):
        @pl.when(pl.program_id(0) == 0)
        def _():
            dg_ref[...] = jnp.zeros_like(dg_ref)

        dp_ref[:, 0:hw] = fq[...]
        dp_ref[:, hw:2 * hw] = fk[...]
        dp_ref[:, 2 * hw:3 * hw] = fv[...]
        dp_ref[:, 3 * hw:4 * hw] = gx[...]
        dp_ref[:, 4 * hw:5 * hw] = (((q1[...] + q2[...]) + q3[...]) * QK_SCALE).astype(BF16)
        dp_ref[:, 5 * hw:6 * hw] = ((k1[...] + k2[...]) + k3[...]).astype(BF16)
        dp_ref[:, 6 * hw:7 * hw] = ((v1[...] + v2[...]) + v3[...]).astype(BF16)
        dh = _dot(dp_ref[...], w_ref[...])
        dx, dg = _rms_bwd(x_ref[...], g_ref[...], dh)
        dg_ref[...] += dg
        o_ref[...] = dx_ref[...] + dx

    half = _rows(tm, hw)
    return _call(
        body, name="in_proj_bwd", grid=(s_len // tm,),
        in_specs=[half] * 13 + [_resident(w_in_t.shape), _rows(tm, d), _rows(tm, d), _resident((1, d))],
        out_specs=[_rows(tm, d), _rows(tm, 7 * hw), _acc_out((1, d))],
        out_shape=[_sds((s_len, d), F32), _sds((s_len, 7 * hw), BF16), _sds((1, d), F32)],
    )(dfq, dfk, dfv, dgx, *ddq, *ddk, *ddv, w_in_t, dx1, x, g_pre)


def _expand_w_in(w_in):
    hw = HEAD_WIDTH
    gate = jnp.repeat(w_in[:, 3 * hw:3 * hw + N_HEADS], HEAD_DIM, axis=1)
    return jnp.concatenate([w_in[:, :3 * hw], gate, w_in[:, 3 * hw + N_HEADS:]], axis=1)


def _local_step(x, mem, target, g, w_bf):
    hw = HEAD_WIDTH
    w_in_e = _expand_w_in(w_bf["w_in"])
    b_exp = jnp.repeat(g["b_f"], HEAD_DIM, axis=1)
    buckets = jnp.asarray(_dil_buckets())

    h1, fqkv, gx, dqkv = _in_proj(x, g["g_mix_pre"], w_in_e)
    aq, ak, c = _gate_scan(gx, b_exp)
    stats = _fox_block_stats(fqkv, c)
    o_fox, lse_fox = _fox_fwd(fqkv, aq, ak, stats)
    bias = _dil_bias(g["rel_bias"], buckets)
    branches = [_dil_fwd(dqkv, bias, p) for p in range(len(DILATIONS))]
    x1, y1, o_dil, lj = _mix_out(o_fox, [b[0] for b in branches], [b[1] for b in branches], w_bf["w_out"], x, g["g_mix_post"])
    hm, kx, vx = _mem_fwd(mem, g["g_mem"], w_bf["w_xk"], w_bf["w_xv"])
    x2, y2, h2, qx, ox = _xattn_fwd(x1, g["g_xattn_pre"], w_bf["w_xq"], kx, vx, w_bf["w_xo"], g["g_xattn_post"])
    h3, a, u, z = _ffn_up(x2, g["g_ffn_pre"], w_bf["w_gate"], w_bf["w_up"])
    y3, dx3, sq = _ffn_down_loss(z, w_bf["w_down"], x2, g["g_ffn_post"], target)

    grads = {}
    dy3, da, du, grads["g_ffn_post"] = _ffn_bwd_a(dx3, y3, g["g_ffn_post"], w_bf["w_down"].T, a, u)
    dx2, grads["g_ffn_pre"] = _ffn_bwd_b(da, du, w_bf["w_gate"].T, w_bf["w_up"].T, dx3, x2, g["g_ffn_pre"])
    grads["w_down"] = _weight_grad(z, dy3, "dw_down")
    grads["w_gate"] = _weight_grad(h3, da, "dw_gate")
    grads["w_up"] = _weight_grad(h3, du, "dw_up")
    dx1, dy2, dqx, dkx, dvx, grads["g_xattn_post"], grads["g_xattn_pre"] = _xattn_bwd(
        dx2, y2, g["g_xattn_post"], w_bf["w_xo"].T, qx, kx, vx, w_bf["w_xq"].T, x1, g["g_xattn_pre"])
    grads["w_xo"] = _weight_grad(ox, dy2, "dw_xo")
    grads["w_xq"] = _weight_grad(h2, dqx, "dw_xq")
    grads["w_xk"], grads["w_xv"], grads["g_mem"] = _mem_bwd(dkx, dvx, w_bf["w_xk"].T, w_bf["w_xv"].T, hm, mem, g["g_mem"])
    dy1, do_fox, do_dil, delta_fox, delta_dil, grads["g_mix_post"] = _mix_out_bwd(
        dx1, y1, g["g_mix_post"], w_bf["w_out"].T, o_fox, o_dil)
    grads["w_out"] = jnp.concatenate([_weight_grad(o_fox, dy1, "dw_out_fox"), _weight_grad(o_dil, dy1, "dw_out_dil")], axis=0)
    aql, ad = _fox_bwd_prep(aq, lse_fox, delta_fox)
    dfq, rs, dfk, dfv, dc = _fox_bwd(fqkv, do_fox, aql, ak, ad, stats)
    dgx, db = _gate_bwd(rs, dc, gx, b_exp)
    grads["b_f"] = db[:, ::HEAD_DIM]
    dil = [_dil_bwd(dqkv, do_dil, lj, delta_dil, bias, p) for p in range(len(DILATIONS))]
    grads["rel_bias"] = _rel_bias_grad(jnp.stack([t[3] for t in dil]), buckets)[:, :N_HEADS]
    grad_x, dproj, grads["g_mix_pre"] = _in_proj_bwd(
        dfq, dfk, dfv, dgx, [t[0] for t in dil], [t[1] for t in dil], [t[2] for t in dil], w_in_e.T, dx1, x, g["g_mix_pre"])
    dw_in_e = _weight_grad(h1, dproj, "dw_in")
    grads["w_in"] = jnp.concatenate(
        [dw_in_e[:, :3 * hw], dw_in_e[:, 3 * hw:4 * hw:HEAD_DIM], dw_in_e[:, 4 * hw:]], axis=1)
    return sq, grad_x, grads


MESH = pl.DeviceIdType.MESH


def _my_place():
    return lax.axis_index("x"), lax.axis_index("y"), lax.axis_index("c")


def _all_gather(x, name):
    rows, lanes = x.shape

    def body(x_ref, out_ref, send_sems, recv_sems, local_sem):
        mx, my, mc = _my_place()
        me, sibling = (mx, my, mc), (mx, my, 1 - mc)
        chips = [(1 - mx, my), (mx, 1 - my), (1 - mx, 1 - my)]

        def slot(px, py, pc):
            return out_ref.at[4 * px + 2 * py + pc]

        def copy(k, block, to, src=None):
            return pltpu.make_async_remote_copy(
                src_ref=slot(*block) if src is None else src, dst_ref=slot(*block),
                send_sem=send_sems.at[k], recv_sem=recv_sems.at[k], device_id=to, device_id_type=MESH)

        mine = pltpu.make_async_copy(x_ref, slot(*me), local_sem)
        mine.start()
        first = [copy(0, me, sibling, src=x_ref)]
        first += [copy(1 + j, me, (*chip, mc), src=x_ref) for j, chip in enumerate(chips)]
        for cp in first:
            cp.start()
        passed = [copy(4 + j, (*chip, mc), sibling) for j, chip in enumerate(chips)]
        for j, chip in enumerate(chips):
            copy(1 + j, (*chip, mc), me).wait_recv()
            passed[j].start()
        copy(0, sibling, me).wait_recv()
        for j, chip in enumerate(chips):
            copy(4 + j, (*chip, 1 - mc), me).wait_recv()
        for cp in first + passed:
            cp.wait_send()
        mine.wait()

    return pl.pallas_call(
        body, name=name, out_shape=_sds((N_DEV, rows, lanes), x.dtype),
        in_specs=[pl.BlockSpec(memory_space=pl.ANY)], out_specs=pl.BlockSpec(memory_space=pl.ANY),
        scratch_shapes=[pltpu.SemaphoreType.DMA((N_DEV - 1,)), pltpu.SemaphoreType.DMA((N_DEV - 1,)), pltpu.SemaphoreType.DMA],
    )(x)


def _exchange(g, name):
    _, rows, lanes = g.shape

    def body(g_ref, land_ref, send_sems, recv_sems, local_sem):
        mx, my, mc = _my_place()
        me = 4 * mx + 2 * my + mc
        mine = pltpu.make_async_copy(g_ref.at[me], land_ref.at[me], local_sem)
        mine.start()
        sent, arriving = [], []
        for k in (1, 2, 3, 4, 5, 6, 7):
            px = 1 - mx if k & 4 else mx
            py = 1 - my if k & 2 else my
            pc = 1 - mc if k & 1 else mc
            peer = 4 * px + 2 * py + pc
            cp = pltpu.make_async_remote_copy(
                src_ref=g_ref.at[peer], dst_ref=land_ref.at[me], send_sem=send_sems.at[k - 1], recv_sem=recv_sems.at[k - 1],
                device_id=(px, py, pc), device_id_type=MESH)
            cp.start()
            sent.append(cp)
            arriving.append(pltpu.make_async_remote_copy(
                src_ref=g_ref.at[me], dst_ref=land_ref.at[peer], send_sem=send_sems.at[k - 1], recv_sem=recv_sems.at[k - 1],
                device_id=(px, py, pc), device_id_type=MESH))
        for cp in arriving:
            cp.wait_recv()
        for cp in sent:
            cp.wait_send()
        mine.wait()

    return pl.pallas_call(
        body, name=name, out_shape=_sds(g.shape, g.dtype),
        in_specs=[pl.BlockSpec(memory_space=pl.ANY)], out_specs=pl.BlockSpec(memory_space=pl.ANY),
        scratch_shapes=[pltpu.SemaphoreType.DMA((N_DEV - 1,)), pltpu.SemaphoreType.DMA((N_DEV - 1,)), pltpu.SemaphoreType.DMA],
    )(g)


def _sum_slots(parts, name):
    n, rows, lanes = parts.shape
    tr = 512 if rows % 512 == 0 else rows

    def body(p_ref, o_ref):
        acc = p_ref[0].astype(F32)
        for j in range(1, n):
            acc = acc + p_ref[j].astype(F32)
        o_ref[...] = acc

    return _call(
        body, name=name, grid=(rows // tr,),
        in_specs=[pl.BlockSpec((n, tr, lanes), lambda i: (0, i, 0))], out_specs=_rows(tr, lanes),
        out_shape=_sds((rows, lanes), F32),
    )(parts)


def _adamw(w, g, m, v, name):
    def body(w_ref, g_ref, m_ref, v_ref, d_ref, nm_ref, nv_ref):
        gv = g_ref[...]
        m_new = ADAM_B1 * m_ref[...] + (1.0 - ADAM_B1) * gv
        v_new = ADAM_B2 * v_ref[...] + (1.0 - ADAM_B2) * (gv * gv)
        nm_ref[...] = m_new
        nv_ref[...] = v_new
        m_hat = m_new / (1.0 - ADAM_B1 ** ADAM_STEP)
        v_hat = v_new / (1.0 - ADAM_B2 ** ADAM_STEP)
        d_ref[...] = -ADAM_LR * (m_hat / (jnp.sqrt(v_hat) + ADAM_EPS) + ADAM_WD * w_ref[...])

    out = _sds(w.shape, F32)
    return pl.pallas_call(
        body, name=name, out_shape=[out, out, out],
        compiler_params=pltpu.CompilerParams(vmem_limit_bytes=V7X_VMEM_LIMIT_BYTES),
    )(w, g, m, v)


def _loss_head(sq, d_model):
    def body(sq_ref, o_ref):
        tot = jnp.sum(jnp.sum(sq_ref[...], axis=1, keepdims=True), axis=0, keepdims=True)
        o_ref[...] = 0.5 * (tot / d_model)

    return pl.pallas_call(body, name="loss_head", out_shape=_sds((1, 1), F32))(sq)


_BIG = (("w_in", 1), ("w_out", 0), ("w_xq", 0), ("w_xk", 0), ("w_xv", 0), ("w_xo", 1), ("w_gate", 1), ("w_up", 1), ("w_down", 0))
_SMALL = ("g_mix_pre", "b_f", "rel_bias", "g_mix_post", "g_xattn_pre", "g_mem", "g_xattn_post", "g_ffn_pre", "g_ffn_post")
LANES = 128
BIG_ROW_ALIGN = 512


def _round_up(n, k):
    return -(-n // k) * k


def _pack_rows(flat_parts, row_align, dtype):
    starts, rows, padded = [], 0, []
    for p in flat_parts:
        n = _round_up(p.shape[0], LANES)
        starts.append(rows)
        rows += n // LANES
        padded.append(jnp.pad(p.astype(dtype), (0, n - p.shape[0])))
    total = _round_up(rows, row_align)
    padded.append(jnp.zeros(((total - rows) * LANES,), dtype))
    return jnp.concatenate(padded).reshape(total, LANES), starts


def _unpack_rows(buf, starts, shapes):
    lead = buf.shape[:-2]
    flat = buf.reshape(lead + (-1,))
    out = []
    for st, shp in zip(starts, shapes):
        n = int(np.prod(shp))
        out.append(flat[..., st * LANES:st * LANES + n].reshape(lead + tuple(shp)))
    return out


def kernel(x, mem, g_mix_pre, w_in, b_f, rel_bias, w_out, g_mix_post, g_xattn_pre, g_mem, w_xq, w_xk, w_xv, w_xo, g_xattn_post, g_ffn_pre, w_gate, w_up, w_down, g_ffn_post, loss_target, m_g_mix_pre, m_w_in, m_b_f, m_rel_bias, m_w_out, m_g_mix_post, m_g_xattn_pre, m_g_mem, m_w_xq, m_w_xk, m_w_xv, m_w_xo, m_g_xattn_post, m_g_ffn_pre, m_w_gate, m_w_up, m_w_down, m_g_ffn_post, v_g_mix_pre, v_w_in, v_b_f, v_rel_bias, v_w_out, v_g_mix_post, v_g_xattn_pre, v_g_mem, v_w_xq, v_w_xk, v_w_xv, v_w_xo, v_g_xattn_post, v_g_ffn_pre, v_w_gate, v_w_up, v_w_down, v_g_ffn_post):
    given = dict(locals())
    order = ("g_mix_pre", "w_in", "b_f", "rel_bias", "w_out", "g_mix_post", "g_xattn_pre", "g_mem", "w_xq", "w_xk", "w_xv",
             "w_xo", "g_xattn_post", "g_ffn_pre", "w_gate", "w_up", "w_down", "g_ffn_post")
    two_d = lambda a: a.reshape(a.shape[-2:])
    w_loc = {n: two_d(given[n]) for n in order}
    m_loc = {n: two_d(given["m_" + n]) for n in order}
    v_loc = {n: two_d(given["v_" + n]) for n in order}
    d_model = x.shape[-1]

    shard_shapes = [w_loc[n].shape for n, _ in _BIG]
    packed, starts = _pack_rows([w_loc[n].reshape(-1) for n, _ in _BIG], BIG_ROW_ALIGN, BF16)
    gathered = _all_gather(packed, "gather_weights")
    w_bf = {}
    for (n, axis), part in zip(_BIG, _unpack_rows(gathered, starts, shard_shapes)):
        r, c = part.shape[1:]
        w_bf[n] = part.reshape(N_DEV * r, c) if axis == 0 else part.transpose(1, 0, 2).reshape(r, N_DEV * c)

    small = {n: w_loc[n] for n in _SMALL}
    sq, grad_x, grads = _local_step(two_d(x), two_d(mem), two_d(loss_target), small, w_bf)

    per_owner = []
    for (n, axis), shp in zip(_BIG, shard_shapes):
        r, c = shp
        gfull = grads[n]
        per_owner.append(gfull.reshape(N_DEV, r * c) if axis == 0 else gfull.reshape(r, N_DEV, c).transpose(1, 0, 2).reshape(N_DEV, r * c))
    rows_big = packed.shape[0]
    slots = []
    for j in range(N_DEV):
        buf, _ = _pack_rows([p[j] for p in per_owner], BIG_ROW_ALIGN, BF16)
        slots.append(buf)
    landed = _exchange(jnp.stack(slots), "exchange_grads")
    g_big = dict(zip([n for n, _ in _BIG], _unpack_rows(_sum_slots(landed, "sum_grads"), starts, shard_shapes)))
    assert landed.shape[1] == rows_big

    small_parts = [grads[n].reshape(-1) for n in _SMALL] + [sq.reshape(-1)]
    small_shapes = [w_loc[n].shape for n in _SMALL] + [sq.shape]
    spacked, sstarts = _pack_rows(small_parts, 8, F32)
    ssum = _sum_slots(_all_gather(spacked, "gather_small"), "sum_small")
    g_small = dict(zip(_SMALL, _unpack_rows(ssum, sstarts, small_shapes)[:-1]))
    sq_rows = sq.size // LANES
    loss = _loss_head(ssum[sstarts[-1]:sstarts[-1] + sq_rows], d_model).reshape(())

    g_loc, delta, new_m, new_v = {}, {}, {}, {}
    for n, _ in _BIG:
        g_loc[n] = g_big[n]
        delta[n], new_m[n], new_v[n] = _adamw(w_loc[n], g_big[n], m_loc[n], v_loc[n], "adamw_" + n)
    pk = lambda d: _pack_rows([d[n].reshape(-1) for n in _SMALL], 8, F32)[0]
    pstarts = _pack_rows([w_loc[n].reshape(-1) for n in _SMALL], 8, F32)[1]
    d_s, m_s, v_s = _adamw(pk(w_loc), pk(g_small), pk(m_loc), pk(v_loc), "adamw_small")
    shapes_s = [w_loc[n].shape for n in _SMALL]
    for n, dd, mm, vv in zip(_SMALL, _unpack_rows(d_s, pstarts, shapes_s), _unpack_rows(m_s, pstarts, shapes_s),
                             _unpack_rows(v_s, pstarts, shapes_s)):
        g_loc[n], delta[n], new_m[n], new_v[n] = g_small[n], dd, mm, vv

    like = lambda d: [d[n].reshape(given[n].shape) for n in order]
    return (loss, grad_x.reshape(x.shape), *like(g_loc), *like(delta), *like(new_m), *like(new_v))
```

```python
import functools

import numpy as np
import jax
import jax.numpy as jnp
from jax import lax
from jax.experimental import pallas as pl
from jax.experimental.pallas import tpu as pltpu

F32 = jnp.float32
BF16 = jnp.bfloat16
HIGHEST = lax.Precision.HIGHEST

RMS_EPS = 1e-6
HEAD_DIM = 64
N_HEADS = 8
HEAD_WIDTH = N_HEADS * HEAD_DIM
PAIR = 2 * HEAD_DIM
N_PAIRS = N_HEADS // 2
DIL_BLOCK = 128
DILATIONS = (1, 4, 16)
N_BUCKETS = 32
MAX_DISTANCE = 2048
N_MEM_HEADS = 4
QK_SCALE = HEAD_DIM ** -0.5
NEG = -1e30
FOX_SKIP_MARGIN = 110.0
N_DEV = 8

ADAM_LR = 0.001
ADAM_B1 = 0.9
ADAM_B2 = 0.999
ADAM_EPS = 1e-08
ADAM_WD = 0.01
ADAM_STEP = 10

V7X_VMEM_LIMIT_BYTES = 56 * 2 ** 20
ROW_TILE = 256
ATT_BLOCK = 512
SCAN_TILE = 256


def _call(body, *, name, grid, in_specs, out_specs, out_shape, scratch=()):
    return pl.pallas_call(
        body, name=name, grid=grid, in_specs=in_specs, out_specs=out_specs, out_shape=out_shape,
        scratch_shapes=list(scratch),
        compiler_params=pltpu.CompilerParams(
            dimension_semantics=("arbitrary",) * len(grid), vmem_limit_bytes=V7X_VMEM_LIMIT_BYTES))


def _rows(tm, n):
    return pl.BlockSpec((tm, n), lambda i: (i, 0))


def _resident(shape):
    zeros = (0,) * len(shape)
    return pl.BlockSpec(shape, lambda i: zeros, pipeline_mode=pl.Buffered(1))


def _acc_out(shape):
    zeros = (0,) * len(shape)
    return pl.BlockSpec(shape, lambda i: zeros)


def _sds(shape, dtype):
    return jax.ShapeDtypeStruct(shape, dtype)


def _dot(a, b):
    return jnp.dot(a, b, preferred_element_type=F32)


def _dot_nt(a, b):
    return lax.dot_general(a, b, (((1,), (1,)), ((), ())), preferred_element_type=F32)


def _dot_tn(a, b):
    return lax.dot_general(a, b, (((0,), (0,)), ((), ())), preferred_element_type=F32)


def _rms_fwd(x, g):
    r = lax.rsqrt(jnp.mean(x * x, axis=-1, keepdims=True) + RMS_EPS)
    return (x * r) * g


def _rms_bwd(xin, g, dy):
    r = lax.rsqrt(jnp.mean(xin * xin, axis=-1, keepdims=True) + RMS_EPS)
    xhat = xin * r
    dg = jnp.sum(dy * xhat, axis=0, keepdims=True)
    dxh = dy * g
    dx = r * (dxh - xhat * jnp.mean(dxh * xhat, axis=-1, keepdims=True))
    return dx, dg


def _first_head_lanes():
    return lax.broadcasted_iota(jnp.int32, (1, PAIR), 1) < HEAD_DIM


def _pick(mask, a, b):
    return jnp.where(mask, a, b)


def _zero_other(mask, v):
    return jnp.where(mask, v, jnp.zeros_like(v))


def _in_proj(x, g, w):
    s_len, d = x.shape
    tm = ROW_TILE
    hw = HEAD_WIDTH

    def body(x_ref, g_ref, w_ref, h_ref, fqkv_ref, gx_ref, dqkv_ref):
        h = _rms_fwd(x_ref[...], g_ref[...]).astype(BF16)
        h_ref[...] = h
        proj = _dot(h, w_ref[...])
        fqkv_ref[:, 0:hw] = (proj[:, 0:hw] * QK_SCALE).astype(BF16)
        fqkv_ref[:, hw:3 * hw] = proj[:, hw:3 * hw].astype(BF16)
        gx_ref[...] = proj[:, 3 * hw:4 * hw]
        dqkv_ref[:, 0:hw] = (proj[:, 4 * hw:5 * hw] * QK_SCALE).astype(BF16)
        dqkv_ref[:, hw:3 * hw] = proj[:, 5 * hw:7 * hw].astype(BF16)

    return _call(
        body, name="in_proj", grid=(s_len // tm,),
        in_specs=[_rows(tm, d), _resident((1, d)), _resident(w.shape)],
        out_specs=[_rows(tm, d), _rows(tm, 3 * hw), _rows(tm, hw), _rows(tm, 3 * hw)],
        out_shape=[_sds((s_len, d), BF16), _sds((s_len, 3 * hw), BF16), _sds((s_len, hw), F32), _sds((s_len, 3 * hw), BF16)],
    )(x, g, w)


def _swap_halves(x):
    return jnp.concatenate([pltpu.roll(x[:, i * PAIR:(i + 1) * PAIR], HEAD_DIM, 1) for i in range(x.shape[1] // PAIR)], axis=1)


def _split3(x):
    hi = x.astype(BF16)
    r = x - hi.astype(F32)
    mid = r.astype(BF16)
    lo = (r - mid.astype(F32)).astype(BF16)
    return hi, mid, lo


def _lane_in_head(width):
    return lax.broadcasted_iota(jnp.int32, (1, width), 1) % HEAD_DIM


def _place3(jj, first, pieces, base):
    out = base
    for i, p in enumerate(pieces):
        out = jnp.where(jj == first + i, p, out)
    return out


def _gate_scan(gx, b_exp):
    s_len, hw = gx.shape
    t = min(SCAN_TILE, s_len)
    tri = jnp.asarray(np.tril(np.ones((t, t), np.float32)))

    def body(gx_ref, b_ref, tri_ref, aq_ref, ak_ref, c_ref, carry):
        @pl.when(pl.program_id(0) == 0)
        def _():
            carry[...] = jnp.zeros_like(carry)

        z = gx_ref[...] + b_ref[...]
        lf = jnp.minimum(z, 0.0) - jnp.log1p(jnp.exp(-jnp.abs(z)))
        c = jnp.dot(tri_ref[...], lf, precision=HIGHEST, preferred_element_type=F32) + carry[...]
        carry[...] = c[t - 1:t, :]
        c_ref[...] = c
        hi, mid, lo = _split3(_swap_halves(c))
        jj = _lane_in_head(hw)
        zero = jnp.zeros_like(hi)
        one = jnp.ones_like(hi)
        aq_ref[...] = _place3(jj, 0, (hi, mid, lo), jnp.where(jj < 6, one, zero))
        ak_ref[...] = _place3(jj, 3, (-hi, -mid, -lo), jnp.where(jj < 9, one, zero))

    return _call(
        body, name="gate_scan", grid=(s_len // t,),
        in_specs=[_rows(t, hw), _resident((1, hw)), _resident((t, t))],
        out_specs=[_rows(t, hw), _rows(t, hw), _rows(t, hw)],
        out_shape=[_sds((s_len, hw), BF16), _sds((s_len, hw), BF16), _sds((s_len, hw), F32)],
        scratch=[pltpu.VMEM((1, hw), F32)],
    )(gx, b_exp, tri)


def _head_block_ones():
    head_of = np.arange(HEAD_WIDTH) // HEAD_DIM
    return jnp.asarray((head_of[:, None] == head_of[None, :]).astype(np.float32))


def _fox_block_stats(fqkv, c):
    s_len = fqkv.shape[0]
    hw = HEAD_WIDTH
    b = min(ATT_BLOCK, s_len)
    nb = s_len // b

    def body(q_ref, k_ref, c_ref, ones_ref, o_ref):
        q, k, cv = q_ref[...].astype(F32), k_ref[...].astype(F32), c_ref[...]
        seg = lambda x: jnp.dot(x, ones_ref[...], precision=HIGHEST, preferred_element_type=F32)
        col_max = lambda x: jnp.max(x, axis=0, keepdims=True)
        col_min = lambda x: jnp.min(x, axis=0, keepdims=True)
        o_ref[0] = jnp.concatenate(
            [jnp.sqrt(col_max(seg(q * q))), col_max(cv) - col_min(seg(q * k)), jnp.sqrt(col_max(seg(k * k))), col_min(cv),
             jnp.zeros((4, hw), F32)], axis=0)

    stats = _call(
        body, name="fox_block_stats", grid=(nb,),
        in_specs=[pl.BlockSpec((b, hw), lambda i: (i, 0)), pl.BlockSpec((b, hw), lambda i: (i, 1)), _rows(b, hw),
                  _resident((hw, hw))],
        out_specs=pl.BlockSpec((1, 8, hw), lambda i: (i, 0, 0)),
        out_shape=_sds((nb, 8, hw), F32),
    )(fqkv, fqkv, c, _head_block_ones())
    st = jnp.transpose(stats[:, :4, ::HEAD_DIM], (1, 2, 0))
    bound = st[0][:, :, None] * st[2][:, None, :] + st[1][:, :, None] - st[3][:, None, :]
    need_h = jnp.logical_not(bound < -FOX_SKIP_MARGIN)
    need = jnp.logical_or(need_h[0::2], need_h[1::2])
    ii = lax.broadcasted_iota(jnp.int32, (1, nb, nb), 1)
    jj = lax.broadcasted_iota(jnp.int32, (1, nb, nb), 2)
    first_needed = jnp.min(jnp.where(jnp.logical_or(jnp.logical_and(need, jj < ii), jj == ii), jj, nb), axis=2)
    window = ii[:, :, 0] - first_needed + 1
    in_window = jnp.logical_and(jj >= first_needed[:, :, None], jj <= ii)
    last_query = jnp.max(jnp.where(in_window, ii, 0), axis=1)
    return jnp.concatenate([st.reshape(4 * N_HEADS, nb), window.astype(F32), last_query.astype(F32)], axis=0)


FOX_WINDOW_ROW = 4 * N_HEADS
FOX_LAST_QUERY_ROW = 4 * N_HEADS + N_PAIRS


def _tile_needed(st_ref, pair, i, j):
    need = None
    for a in (0, 1):
        h = 2 * pair + a
        bound = st_ref[h, i] * st_ref[2 * N_HEADS + h, j] + st_ref[N_HEADS + h, i] - st_ref[3 * N_HEADS + h, j]
        need_a = jnp.logical_not(bound < -FOX_SKIP_MARGIN)
        need = need_a if need is None else jnp.logical_or(need, need_a)
    return need


def _lower_triangle(n):
    return lax.broadcasted_iota(jnp.int32, (n, n), 1) <= lax.broadcasted_iota(jnp.int32, (n, n), 0)


def _fox_fwd(fqkv, aq, ak, stats):
    s_len = fqkv.shape[0]
    bq = bk = min(ATT_BLOCK, s_len)
    nq, nk = s_len // bq, s_len // bk

    def body(st_ref, q_ref, k_ref, v_ref, aq_ref, ak_ref, o_ref, lse_ref, qa_ref, m_ref, acc_ref):
        pair, qi, back = pl.program_id(0), pl.program_id(1), pl.program_id(2)
        kj = jnp.maximum(qi - back, 0)
        in0 = _first_head_lanes()
        not0 = jnp.logical_not(in0)

        @pl.when(back == 0)
        def _():
            q2, a2 = q_ref[...], aq_ref[...]
            qa_ref[0] = jnp.where(in0, q2, a2)
            qa_ref[1] = jnp.where(in0, a2, q2)
            m_ref[...] = jnp.full_like(m_ref, NEG)
            acc_ref[...] = jnp.zeros_like(acc_ref)

        def step(masked):
            k2, v2, a2 = k_ref[...], v_ref[...], ak_ref[...]
            one = jnp.ones_like(v2)
            for a, mine in ((0, in0), (1, not0)):
                s = _dot_nt(qa_ref[a], jnp.where(mine, k2, a2))
                if masked:
                    s = jnp.where(_lower_triangle(bq), s, NEG)
                m_old = m_ref[a]
                m_new = jnp.maximum(m_old, jnp.max(s, axis=1, keepdims=True))
                p = jnp.exp(s - jnp.tile(m_new, (1, bk // PAIR))).astype(BF16)
                acc_ref[a] = jnp.exp(m_old - m_new) * acc_ref[a] + _dot(p, jnp.where(mine, v2, one))
                m_ref[a] = m_new

        @pl.when(back == 0)
        def _():
            step(True)

        @pl.when(jnp.logical_and(jnp.logical_and(back > 0, back <= qi), _tile_needed(st_ref, pair, qi, kj)))
        def _():
            step(False)

        @pl.when(back == qi)
        def _():
            acc0, acc1 = acc_ref[0], acc_ref[1]
            l2 = pltpu.roll(_pick(in0, acc1, acc0), HEAD_DIM, 1)
            o_ref[...] = _pick(in0, acc0, acc1) / l2
            lse_ref[...] = _pick(in0, m_ref[0], m_ref[1]) + jnp.log(l2)

    blk = lambda rows: (rows, PAIR)
    qmap = lambda p, i, b, st: (i, p)

    def key_block(p, i, b, st):
        window = jnp.clip(st[FOX_WINDOW_ROW + p, i].astype(jnp.int32), 1, i + 1)
        return i - jnp.minimum(b, window - 1)

    return pl.pallas_call(
        body, name="fox_fwd",
        grid_spec=pltpu.PrefetchScalarGridSpec(
            num_scalar_prefetch=1, grid=(N_PAIRS, nq, nk),
            in_specs=[
                pl.BlockSpec(blk(bq), qmap),
                pl.BlockSpec(blk(bk), lambda p, i, b, st: (key_block(p, i, b, st), N_PAIRS + p)),
                pl.BlockSpec(blk(bk), lambda p, i, b, st: (key_block(p, i, b, st), 2 * N_PAIRS + p)),
                pl.BlockSpec(blk(bq), qmap),
                pl.BlockSpec(blk(bk), lambda p, i, b, st: (key_block(p, i, b, st), p)),
            ],
            out_specs=[pl.BlockSpec(blk(bq), qmap), pl.BlockSpec(blk(bq), qmap)],
            scratch_shapes=[pltpu.VMEM((2, bq, PAIR), BF16), pltpu.VMEM((2, bq, PAIR), F32), pltpu.VMEM((2, bq, PAIR), F32)]),
        out_shape=[_sds((s_len, HEAD_WIDTH), F32), _sds((s_len, HEAD_WIDTH), F32)],
        compiler_params=pltpu.CompilerParams(dimension_semantics=("arbitrary",) * 3, vmem_limit_bytes=V7X_VMEM_LIMIT_BYTES),
    )(stats, fqkv, fqkv, fqkv, aq, ak)


def _t5_bucket(dist):
    max_exact = N_BUCKETS // 2
    d = np.maximum(dist, 1).astype(np.float32)
    large = max_exact + (np.log(d / max_exact) / np.log(MAX_DISTANCE / max_exact) * (N_BUCKETS - max_exact)).astype(np.int32)
    large = np.minimum(large, N_BUCKETS - 1)
    return np.where(dist < max_exact, dist, large).astype(np.int32)


def _dil_buckets():
    w = DIL_BLOCK
    qi = np.arange(w)[:, None]
    kj = np.arange(2 * w)[None, :]
    sub = qi + w - kj
    band = (sub >= 0) & (sub <= w)
    out = [np.where(band, _t5_bucket(np.clip(sub, 0, w) * dil), -1) for dil in DILATIONS]
    return np.stack(out).astype(np.int32)


def _dil_bias(rel_bias, buckets):
    w = DIL_BLOCK

    def body(rb_ref, bk_ref, o_ref):
        for p in range(len(DILATIONS)):
            bk = bk_ref[p]
            for h in range(N_HEADS):
                def add(b, acc):
                    return acc + jnp.where(bk == b, rb_ref[b, h], 0.0)
                acc = lax.fori_loop(0, N_BUCKETS, add, jnp.zeros((w, 2 * w), F32))
                o_ref[p, h] = jnp.where(bk < 0, NEG, acc)

    return pl.pallas_call(
        body, name="dil_bias",
        in_specs=[pl.BlockSpec(memory_space=pltpu.SMEM), pl.BlockSpec(memory_space=pltpu.VMEM)],
        out_specs=pl.BlockSpec(memory_space=pltpu.VMEM),
        out_shape=_sds((len(DILATIONS), N_HEADS, w, 2 * w), F32),
    )(rel_bias, buckets)


def _dil_fwd(dqkv, bias, branch):
    dil = DILATIONS[branch]
    s_len = dqkv.shape[0]
    w = DIL_BLOCK
    hw = HEAD_WIDTH
    length = s_len // dil
    nb = length // w
    view = dqkv.reshape(length, dil * 3 * hw)

    def body(q_ref, kc_ref, kp_ref, vc_ref, vp_ref, b_ref, o_ref, lse_ref):
        n = pl.program_id(1)
        in0 = _first_head_lanes()
        not0 = jnp.logical_not(in0)
        pairs = [slice(pr * PAIR, (pr + 1) * PAIR) for pr in range(N_PAIRS)]
        tiles = []
        for sl in pairs:
            q2 = q_ref[:, sl]
            qq = jnp.concatenate([_zero_other(in0, q2), _zero_other(not0, q2)], axis=0)
            tiles.append(jnp.concatenate([_dot_nt(qq, kp_ref[:, sl]), _dot_nt(qq, kc_ref[:, sl])], axis=1))
        s = jnp.concatenate(tiles, axis=0) + b_ref[...].reshape(N_HEADS * w, 2 * w)
        prev_half = lax.broadcasted_iota(jnp.int32, (1, 2 * w), 1) < w
        s = jnp.where(jnp.logical_and(n == 0, prev_half), NEG, s)
        m = jnp.max(s, axis=1, keepdims=True)
        e = jnp.exp(s - m)
        l = jnp.sum(e, axis=1, keepdims=True)
        p = (e / l).astype(BF16)
        lse = m + jnp.log(l)
        for pr, sl in enumerate(pairs):
            pp = p[2 * pr * w:(2 * pr + 2) * w]
            o2 = _dot(pp[:, :w], vp_ref[:, sl]) + _dot(pp[:, w:], vc_ref[:, sl])
            o_ref[:, sl] = _pick(in0, o2[:w], o2[w:])
            lse_ref[:, sl] = _pick(in0, lse[2 * pr * w:(2 * pr + 1) * w], lse[(2 * pr + 1) * w:(2 * pr + 2) * w])

    prev = lambda n: jnp.maximum(n - 1, 0)
    out = pl.pallas_call(
        body, name=f"dil_fwd_{dil}", grid=(dil, nb),
        in_specs=[
            pl.BlockSpec((w, hw), lambda r, n: (n, 3 * r)),
            pl.BlockSpec((w, hw), lambda r, n: (n, 3 * r + 1)),
            pl.BlockSpec((w, hw), lambda r, n: (prev(n), 3 * r + 1)),
            pl.BlockSpec((w, hw), lambda r, n: (n, 3 * r + 2)),
            pl.BlockSpec((w, hw), lambda r, n: (prev(n), 3 * r + 2)),
            pl.BlockSpec((None, N_HEADS, w, 2 * w), lambda r, n: (branch, 0, 0, 0)),
        ],
        out_specs=[pl.BlockSpec((w, hw), lambda r, n: (n, r)), pl.BlockSpec((w, hw), lambda r, n: (n, r))],
        out_shape=[_sds((length, dil * hw), F32), _sds((length, dil * hw), F32)],
        compiler_params=pltpu.CompilerParams(dimension_semantics=("arbitrary",) * 2, vmem_limit_bytes=V7X_VMEM_LIMIT_BYTES),
    )(view, view, view, view, view, bias)
    return out[0].reshape(s_len, hw), out[1].reshape(s_len, hw)


def _mix_out(o_fox, o_br, lse_br, w_out, x, g_post):
    s_len, d = x.shape
    hw = HEAD_WIDTH
    tm = ROW_TILE

    def body(of_ref, o1, o2, o3, l1, l2, l3, w_ref, x_ref, g_ref, x1_ref, y1_ref, od_ref, lj_ref):
        la, lb, lc = l1[...], l2[...], l3[...]
        m = jnp.maximum(jnp.maximum(la, lb), lc)
        ea, eb, ec = jnp.exp(la - m), jnp.exp(lb - m), jnp.exp(lc - m)
        tot = ea + eb + ec
        o_dil = (ea / tot) * o1[...] + (eb / tot) * o2[...] + (ec / tot) * o3[...]
        od_ref[...] = o_dil
        lj_ref[...] = m + jnp.log(tot)
        y = _dot(of_ref[...].astype(BF16), w_ref[0:hw, :]) + _dot(o_dil.astype(BF16), w_ref[hw:2 * hw, :])
        y1_ref[...] = y
        x1_ref[...] = x_ref[...] + _rms_fwd(y, g_ref[...])

    half = _rows(tm, hw)
    return _call(
        body, name="mix_out", grid=(s_len // tm,),
        in_specs=[half] * 7 + [_resident(w_out.shape), _rows(tm, d), _resident((1, d))],
        out_specs=[_rows(tm, d), _rows(tm, d), half, half],
        out_shape=[_sds((s_len, d), F32), _sds((s_len, d), F32), _sds((s_len, hw), F32), _sds((s_len, hw), F32)],
    )(o_fox, *o_br, *lse_br, w_out, x, g_post)


def _mem_fwd(mem, g_mem, w_xk, w_xv):
    n_mem, d = mem.shape
    mw = w_xk.shape[1]

    def body(mem_ref, g_ref, wk_ref, wv_ref, hm_ref, k_ref, v_ref):
        hm = _rms_fwd(mem_ref[...], g_ref[...]).astype(BF16)
        hm_ref[...] = hm
        k_ref[...] = _dot(hm, wk_ref[...]).astype(BF16)
        v_ref[...] = _dot(hm, wv_ref[...]).astype(BF16)

    return pl.pallas_call(
        body, name="mem_fwd",
        out_shape=[_sds((n_mem, d), BF16), _sds((n_mem, mw), BF16), _sds((n_mem, mw), BF16)],
    )(mem, g_mem, w_xk, w_xv)


def _xattn_softmax(qa, k2):
    s = _dot_nt(qa, k2)
    m = jnp.max(s, axis=1, keepdims=True)
    e = jnp.exp(s - m)
    return e / jnp.sum(e, axis=1, keepdims=True)


def _xattn_fwd(x1, g_pre, w_xq, kx, vx, w_xo, g_post):
    s_len, d = x1.shape
    mw = w_xq.shape[1]
    n_mem = kx.shape[0]
    tm = ROW_TILE

    def body(x_ref, gp_ref, wq_ref, k_ref, v_ref, wo_ref, go_ref, x2_ref, y2_ref, h2_ref, q_ref, o_ref):
        x = x_ref[...]
        h = _rms_fwd(x, gp_ref[...]).astype(BF16)
        h2_ref[...] = h
        q = (_dot(h, wq_ref[...]) * QK_SCALE).astype(BF16)
        q_ref[...] = q
        in0 = _first_head_lanes()
        not0 = jnp.logical_not(in0)
        for pr in range(mw // PAIR):
            sl = slice(pr * PAIR, (pr + 1) * PAIR)
            q2, k2, v2 = q[:, sl], k_ref[:, sl], v_ref[:, sl]
            oa = [_dot(_xattn_softmax(_zero_other(mine, q2), k2).astype(BF16), v2) for mine in (in0, not0)]
            o_ref[:, sl] = _pick(in0, oa[0], oa[1]).astype(BF16)
        y = _dot(o_ref[...], wo_ref[...])
        y2_ref[...] = y
        x2_ref[...] = x + _rms_fwd(y, go_ref[...])

    return _call(
        body, name="xattn_fwd", grid=(s_len // tm,),
        in_specs=[_rows(tm, d), _resident((1, d)), _resident(w_xq.shape), _resident((n_mem, mw)), _resident((n_mem, mw)),
                  _resident(w_xo.shape), _resident((1, d))],
        out_specs=[_rows(tm, d), _rows(tm, d), _rows(tm, d), _rows(tm, mw), _rows(tm, mw)],
        out_shape=[_sds((s_len, d), F32), _sds((s_len, d), F32), _sds((s_len, d), BF16), _sds((s_len, mw), BF16),
                   _sds((s_len, mw), BF16)],
    )(x1, g_pre, w_xq, kx, vx, w_xo, g_post)


def _ffn_up(x2, g_pre, w_gate, w_up):
    s_len, d = x2.shape
    dff = w_gate.shape[1]
    tm = ROW_TILE

    def body(x_ref, g_ref, wg_ref, wu_ref, h_ref, a_ref, u_ref, z_ref):
        h = _rms_fwd(x_ref[...], g_ref[...]).astype(BF16)
        h_ref[...] = h
        a = _dot(h, wg_ref[...])
        u = _dot(h, wu_ref[...])
        a_ref[...] = a.astype(BF16)
        u_ref[...] = u.astype(BF16)
        z_ref[...] = ((a * jax.nn.sigmoid(a)) * u).astype(BF16)

    return _call(
        body, name="ffn_up", grid=(s_len // tm,),
        in_specs=[_rows(tm, d), _resident((1, d)), _resident(w_gate.shape), _resident(w_up.shape)],
        out_specs=[_rows(tm, d), _rows(tm, dff), _rows(tm, dff), _rows(tm, dff)],
        out_shape=[_sds((s_len, d), BF16)] + [_sds((s_len, dff), BF16)] * 3,
    )(x2, g_pre, w_gate, w_up)


def _ffn_down_loss(z, w_down, x2, g_post, target):
    s_len, d = x2.shape
    dff = z.shape[1]
    tm = ROW_TILE

    def body(z_ref, w_ref, x_ref, g_ref, t_ref, y_ref, dx_ref, sq_ref):
        @pl.when(pl.program_id(0) == 0)
        def _():
            sq_ref[...] = jnp.zeros_like(sq_ref)

        y = _dot(z_ref[...], w_ref[...])
        y_ref[...] = y
        err = (x_ref[...] + _rms_fwd(y, g_ref[...])) - t_ref[...]
        sq_ref[...] += jnp.sum(err * err, axis=0, keepdims=True)
        dx_ref[...] = err * (1.0 / d)

    return _call(
        body, name="ffn_down_loss", grid=(s_len // tm,),
        in_specs=[_rows(tm, dff), _resident(w_down.shape), _rows(tm, d), _resident((1, d)), _rows(tm, d)],
        out_specs=[_rows(tm, d), _rows(tm, d), _acc_out((1, d))],
        out_shape=[_sds((s_len, d), F32), _sds((s_len, d), F32), _sds((1, d), F32)],
    )(z, w_down, x2, g_post, target)


def _weight_grad(a, b, name):
    s_len, k = a.shape
    n = b.shape[1]
    ts = 512 if s_len % 512 == 0 else s_len
    tn = n
    while k * tn * 4 > 8 * 2 ** 20 and tn % 256 == 0:
        tn //= 2

    def body(a_ref, b_ref, o_ref):
        @pl.when(pl.program_id(1) == 0)
        def _():
            o_ref[...] = jnp.zeros_like(o_ref)

        o_ref[...] += _dot_tn(a_ref[...].astype(BF16), b_ref[...].astype(BF16))

    return pl.pallas_call(
        body, name=name, grid=(n // tn, s_len // ts),
        in_specs=[pl.BlockSpec((ts, k), lambda j, i: (i, 0)), pl.BlockSpec((ts, tn), lambda j, i: (i, j))],
        out_specs=pl.BlockSpec((k, tn), lambda j, i: (0, j)),
        out_shape=_sds((k, n), F32),
        compiler_params=pltpu.CompilerParams(dimension_semantics=("arbitrary",) * 2, vmem_limit_bytes=V7X_VMEM_LIMIT_BYTES),
    )(a, b)


def _ffn_bwd_a(dx3, y3, g_post, w_down_t, a, u):
    s_len, d = dx3.shape
    dff = a.shape[1]
    tm = ROW_TILE

    def body(dx_ref, y_ref, g_ref, w_ref, a_ref, u_ref, dy_ref, da_ref, du_ref, dg_ref):
        @pl.when(pl.program_id(0) == 0)
        def _():
            dg_ref[...] = jnp.zeros_like(dg_ref)

        dy, dg = _rms_bwd(y_ref[...], g_ref[...], dx_ref[...])
        dg_ref[...] += dg
        dyb = dy.astype(BF16)
        dy_ref[...] = dyb
        dz = _dot(dyb, w_ref[...])
        av = a_ref[...].astype(F32)
        uv = u_ref[...].astype(F32)
        sg = jax.nn.sigmoid(av)
        da_ref[...] = (dz * uv * (sg * (1.0 + av * (1.0 - sg)))).astype(BF16)
        du_ref[...] = (dz * (av * sg)).astype(BF16)

    return _call(
        body, name="ffn_bwd_a", grid=(s_len // tm,),
        in_specs=[_rows(tm, d), _rows(tm, d), _resident((1, d)), _resident(w_down_t.shape), _rows(tm, dff), _rows(tm, dff)],
        out_specs=[_rows(tm, d), _rows(tm, dff), _rows(tm, dff), _acc_out((1, d))],
        out_shape=[_sds((s_len, d), BF16), _sds((s_len, dff), BF16), _sds((s_len, dff), BF16), _sds((1, d), F32)],
    )(dx3, y3, g_post, w_down_t, a, u)


def _ffn_bwd_b(da, du, w_gate_t, w_up_t, dx3, x2, g_pre):
    s_len, d = x2.shape
    dff = da.shape[1]
    tm = ROW_TILE

    def body(da_ref, du_ref, wg_ref, wu_ref, dx_ref, x_ref, g_ref, o_ref, dg_ref):
        @pl.when(pl.program_id(0) == 0)
        def _():
            dg_ref[...] = jnp.zeros_like(dg_ref)

        dh = _dot(da_ref[...], wg_ref[...]) + _dot(du_ref[...], wu_ref[...])
        dx, dg = _rms_bwd(x_ref[...], g_ref[...], dh)
        dg_ref[...] += dg
        o_ref[...] = dx_ref[...] + dx

    return _call(
        body, name="ffn_bwd_b", grid=(s_len // tm,),
        in_specs=[_rows(tm, dff), _rows(tm, dff), _resident(w_gate_t.shape), _resident(w_up_t.shape), _rows(tm, d),
                  _rows(tm, d), _resident((1, d))],
        out_specs=[_rows(tm, d), _acc_out((1, d))],
        out_shape=[_sds((s_len, d), F32), _sds((1, d), F32)],
    )(da, du, w_gate_t, w_up_t, dx3, x2, g_pre)


def _xattn_bwd(dx2, y2, g_post, w_xo_t, q, kx, vx, w_xq_t, x1, g_pre):
    s_len, d = x1.shape
    mw = q.shape[1]
    n_mem = kx.shape[0]
    tm = ROW_TILE

    def body(dx_ref, y_ref, go_ref, wo_ref, q_ref, k_ref, v_ref, wq_ref, x_ref, gp_ref,
             dx1_ref, dy_ref, dq_ref, dk_ref, dv_ref, dgo_ref, dgp_ref):
        @pl.when(pl.program_id(0) == 0)
        def _():
            dk_ref[...] = jnp.zeros_like(dk_ref)
            dv_ref[...] = jnp.zeros_like(dv_ref)
            dgo_ref[...] = jnp.zeros_like(dgo_ref)
            dgp_ref[...] = jnp.zeros_like(dgp_ref)

        dxin = dx_ref[...]
        dy, dgo = _rms_bwd(y_ref[...], go_ref[...], dxin)
        dgo_ref[...] += dgo
        dyb = dy.astype(BF16)
        dy_ref[...] = dyb
        do = _dot(dyb, wo_ref[...]).astype(BF16)
        in0 = _first_head_lanes()
        not0 = jnp.logical_not(in0)
        for pr in range(mw // PAIR):
            sl = slice(pr * PAIR, (pr + 1) * PAIR)
            q2, k2, v2, do2 = q_ref[:, sl], k_ref[:, sl], v_ref[:, sl], do[:, sl]
            dqs = []
            dk2 = jnp.zeros((n_mem, PAIR), F32)
            dv2 = jnp.zeros((n_mem, PAIR), F32)
            for mine in (in0, not0):
                qa = _zero_other(mine, q2)
                doa = _zero_other(mine, do2)
                p = _xattn_softmax(qa, k2)
                dp = _dot_nt(doa, v2)
                ds = (p * (dp - jnp.sum(p * dp, axis=1, keepdims=True))).astype(BF16)
                dqs.append(_dot(ds, k2))
                dk2 = dk2 + _dot_tn(ds, qa)
                dv2 = dv2 + _dot_tn(p.astype(BF16), doa)
            dq_ref[:, sl] = (_pick(in0, dqs[0], dqs[1]) * QK_SCALE).astype(BF16)
            dk_ref[:, sl] += dk2
            dv_ref[:, sl] += dv2
        dh = _dot(dq_ref[...], wq_ref[...])
        dx, dgp = _rms_bwd(x_ref[...], gp_ref[...], dh)
        dgp_ref[...] += dgp
        dx1_ref[...] = dxin + dx

    return _call(
        body, name="xattn_bwd", grid=(s_len // tm,),
        in_specs=[_rows(tm, d), _rows(tm, d), _resident((1, d)), _resident(w_xo_t.shape), _rows(tm, mw),
                  _resident((n_mem, mw)), _resident((n_mem, mw)), _resident(w_xq_t.shape), _rows(tm, d), _resident((1, d))],
        out_specs=[_rows(tm, d), _rows(tm, d), _rows(tm, mw), _acc_out((n_mem, mw)), _acc_out((n_mem, mw)),
                   _acc_out((1, d)), _acc_out((1, d))],
        out_shape=[_sds((s_len, d), F32), _sds((s_len, d), BF16), _sds((s_len, mw), BF16), _sds((n_mem, mw), F32),
                   _sds((n_mem, mw), F32), _sds((1, d), F32), _sds((1, d), F32)],
    )(dx2, y2, g_post, w_xo_t, q, kx, vx, w_xq_t, x1, g_pre)


def _mem_bwd(dk, dv, w_xk_t, w_xv_t, hm, mem, g_mem):
    n_mem, d = mem.shape
    mw = dk.shape[1]

    def body(dk_ref, dv_ref, wk_ref, wv_ref, hm_ref, mem_ref, g_ref, dwk_ref, dwv_ref, dg_ref):
        dkb = dk_ref[...].astype(BF16)
        dvb = dv_ref[...].astype(BF16)
        dhm = _dot(dkb, wk_ref[...]) + _dot(dvb, wv_ref[...])
        _, dg = _rms_bwd(mem_ref[...], g_ref[...], dhm)
        dg_ref[...] = dg
        dwk_ref[...] = _dot_tn(hm_ref[...], dkb)
        dwv_ref[...] = _dot_tn(hm_ref[...], dvb)

    return pl.pallas_call(
        body, name="mem_bwd",
        out_shape=[_sds((d, mw), F32), _sds((d, mw), F32), _sds((1, d), F32)],
    )(dk, dv, w_xk_t, w_xv_t, hm, mem, g_mem)


def _mix_out_bwd(dx1, y1, g_post, w_out_t, o_fox, o_dil):
    s_len, d = dx1.shape
    hw = HEAD_WIDTH
    tm = ROW_TILE
    head_of = np.arange(hw) // HEAD_DIM
    ones = jnp.asarray((head_of[:, None] == head_of[None, :]).astype(np.float32))

    def body(dx_ref, y_ref, g_ref, w_ref, of_ref, od_ref, ones_ref, dy_ref, dof_ref, dod_ref, dlf_ref, dld_ref, dg_ref):
        @pl.when(pl.program_id(0) == 0)
        def _():
            dg_ref[...] = jnp.zeros_like(dg_ref)

        dy, dg = _rms_bwd(y_ref[...], g_ref[...], dx_ref[...])
        dg_ref[...] += dg
        dyb = dy.astype(BF16)
        dy_ref[...] = dyb
        do = _dot(dyb, w_ref[...])
        dof_ref[...] = do[:, 0:hw].astype(BF16)
        dod_ref[...] = do[:, hw:2 * hw].astype(BF16)
        dlf_ref[...] = jnp.dot(do[:, 0:hw] * of_ref[...], ones_ref[...], precision=HIGHEST, preferred_element_type=F32)
        dld_ref[...] = jnp.dot(do[:, hw:2 * hw] * od_ref[...], ones_ref[...], precision=HIGHEST, preferred_element_type=F32)

    half = _rows(tm, hw)
    return _call(
        body, name="mix_out_bwd", grid=(s_len // tm,),
        in_specs=[_rows(tm, d), _rows(tm, d), _resident((1, d)), _resident(w_out_t.shape), half, half, _resident((hw, hw))],
        out_specs=[_rows(tm, d), half, half, half, half, _acc_out((1, d))],
        out_shape=[_sds((s_len, d), BF16), _sds((s_len, hw), BF16), _sds((s_len, hw), BF16), _sds((s_len, hw), F32),
                   _sds((s_len, hw), F32), _sds((1, d), F32)],
    )(dx1, y1, g_post, w_out_t, o_fox, o_dil, ones)


def _fox_bwd_prep(aq, lse, delta):
    s_len, hw = aq.shape
    tm = ROW_TILE

    def body(aq_ref, lse_ref, dl_ref, aql_ref, ad_ref):
        jj = _lane_in_head(hw)
        l3 = _split3(_swap_halves(lse_ref[...]))
        aql_ref[...] = _place3(jj, 6, [-p for p in l3], aq_ref[...])
        d3 = _split3(_swap_halves(dl_ref[...]))
        ad_ref[...] = _place3(jj, 0, [-p for p in d3], jnp.zeros((tm, hw), BF16))

    half = _rows(tm, hw)
    return _call(
        body, name="fox_bwd_prep", grid=(s_len // tm,),
        in_specs=[half, half, half], out_specs=[half, half],
        out_shape=[_sds((s_len, hw), BF16), _sds((s_len, hw), BF16)],
    )(aq, lse, delta)


def _ones_on_first3(shape):
    jj = lax.broadcasted_iota(jnp.int32, shape, 1) % HEAD_DIM
    return jnp.where(jj < 3, 1.0, 0.0).astype(BF16)


def _fox_bwd(fqkv, do, aql, ak, ad, stats):
    s_len = fqkv.shape[0]
    bq = bk = min(ATT_BLOCK, s_len)
    nq, nk = s_len // bq, s_len // bk

    def body(st_ref, q_ref, k_ref, v_ref, do_ref, aql_ref, ak_ref, ad_ref, dq_ref, rs_ref, dk_ref, dv_ref, dc_ref,
             ka_ref, va_ref, kone_ref, r_ref, dvacc_ref, dqacc_ref):
        pair, kj, qi = pl.program_id(0), pl.program_id(1), pl.program_id(2)
        in0 = _first_head_lanes()
        not0 = jnp.logical_not(in0)
        heads = ((0, in0), (1, not0))
        rows = pl.ds(pl.multiple_of(qi * bq, bq), bq)

        @pl.when(qi == 0)
        def _():
            k2, v2, a2 = k_ref[...], v_ref[...], ak_ref[...]
            one = jnp.ones_like(k2)
            one3 = _ones_on_first3(v2.shape)
            for a, mine in heads:
                ka_ref[a] = jnp.where(mine, k2, a2)
                va_ref[a] = jnp.where(mine, v2, one3)
                kone_ref[a] = jnp.where(mine, k2, one)
            r_ref[...] = jnp.zeros_like(r_ref)
            dvacc_ref[...] = jnp.zeros_like(dvacc_ref)

        @pl.when(kj == 0)
        def _():
            for a, _ in heads:
                dqacc_ref[a, rows, :] = jnp.zeros((bq, PAIR), F32)

        def step(masked):
            q2, do2, a2, d2 = q_ref[...], do_ref[...], aql_ref[...], ad_ref[...]
            one = jnp.ones_like(q2)
            for a, mine in heads:
                doa = jnp.where(mine, do2, d2)
                s = _dot_nt(jnp.where(mine, q2, a2), ka_ref[a])
                if masked:
                    s = jnp.where(_lower_triangle(bq), s, NEG)
                p = jnp.exp(s)
                ds = (p * _dot_nt(doa, va_ref[a])).astype(BF16)
                dvacc_ref[a] += _dot_tn(p.astype(BF16), doa)
                r_ref[a] += _dot_tn(ds, jnp.where(mine, q2, one))
                dqacc_ref[a, rows, :] += _dot(ds, kone_ref[a])

        @pl.when(jnp.logical_and(qi > kj, _tile_needed(st_ref, pair, qi, kj)))
        def _():
            step(False)

        @pl.when(qi == kj)
        def _():
            step(True)
            acc0, acc1 = dqacc_ref[0, rows, :], dqacc_ref[1, rows, :]
            dq_ref[...] = (_pick(in0, acc0, acc1) * QK_SCALE).astype(BF16)
            rs_ref[...] = pltpu.roll(_pick(in0, acc1, acc0), HEAD_DIM, 1)

        @pl.when(qi == nq - 1)
        def _():
            dk_ref[...] = _pick(in0, r_ref[0], r_ref[1]).astype(BF16)
            dv_ref[...] = _pick(in0, dvacc_ref[0], dvacc_ref[1]).astype(BF16)
            dc_ref[...] = -pltpu.roll(_pick(in0, r_ref[1], r_ref[0]), HEAD_DIM, 1)

    blk = lambda n: (n, PAIR)
    kvmap = lambda p, j, i, st: (j, p)

    def qmap(p, j, i, st):
        last = jnp.clip(st[FOX_LAST_QUERY_ROW + p, j].astype(jnp.int32), j, nq - 1)
        return (jnp.minimum(jnp.maximum(i, j), last), p)

    return pl.pallas_call(
        body, name="fox_bwd",
        grid_spec=pltpu.PrefetchScalarGridSpec(
            num_scalar_prefetch=1, grid=(N_PAIRS, nk, nq),
            in_specs=[
                pl.BlockSpec(blk(bq), qmap),
                pl.BlockSpec(blk(bk), lambda p, j, i, st: (j, N_PAIRS + p)),
                pl.BlockSpec(blk(bk), lambda p, j, i, st: (j, 2 * N_PAIRS + p)),
                pl.BlockSpec(blk(bq), qmap), pl.BlockSpec(blk(bq), qmap),
                pl.BlockSpec(blk(bk), kvmap),
                pl.BlockSpec(blk(bq), qmap),
            ],
            out_specs=[pl.BlockSpec(blk(bk), kvmap)] * 5,
            scratch_shapes=[pltpu.VMEM((2, bk, PAIR), BF16), pltpu.VMEM((2, bk, PAIR), BF16), pltpu.VMEM((2, bk, PAIR), BF16),
                            pltpu.VMEM((2, bk, PAIR), F32), pltpu.VMEM((2, bk, PAIR), F32), pltpu.VMEM((2, s_len, PAIR), F32)]),
        out_shape=[_sds((s_len, HEAD_WIDTH), BF16), _sds((s_len, HEAD_WIDTH), F32), _sds((s_len, HEAD_WIDTH), BF16),
                   _sds((s_len, HEAD_WIDTH), BF16), _sds((s_len, HEAD_WIDTH), F32)],
        compiler_params=pltpu.CompilerParams(dimension_semantics=("arbitrary",) * 3, vmem_limit_bytes=V7X_VMEM_LIMIT_BYTES),
    )(stats, fqkv, fqkv, fqkv, do, aql, ak, ad)


def _gate_bwd(rs, dc, gx, b_exp):
    s_len, hw = gx.shape
    t = min(SCAN_TILE, s_len)
    nt = s_len // t
    tri = jnp.asarray(np.triu(np.ones((t, t), np.float32)))

    def body(rs_ref, dc_ref, gx_ref, b_ref, tri_ref, dgx_ref, db_ref, carry):
        @pl.when(pl.program_id(0) == 0)
        def _():
            carry[...] = jnp.zeros_like(carry)
            db_ref[...] = jnp.zeros_like(db_ref)

        dlf = jnp.dot(tri_ref[...], rs_ref[...] + dc_ref[...], precision=HIGHEST, preferred_element_type=F32) + carry[...]
        carry[...] = dlf[0:1, :]
        dgate = dlf * jax.nn.sigmoid(-(gx_ref[...] + b_ref[...]))
        db_ref[...] += jnp.sum(dgate, axis=0, keepdims=True)
        lane = lax.broadcasted_iota(jnp.int32, (1, hw), 1)
        dgx_ref[...] = jnp.where(lane % HEAD_DIM == 0, dgate, 0.0).astype(BF16)

    rev = lambda i: (nt - 1 - i, 0)
    return _call(
        body, name="gate_bwd", grid=(nt,),
        in_specs=[pl.BlockSpec((t, hw), rev)] * 3 + [_resident((1, hw)), _resident((t, t))],
        out_specs=[pl.BlockSpec((t, hw), rev), _acc_out((1, hw))],
        out_shape=[_sds((s_len, hw), BF16), _sds((1, hw), F32)],
        scratch=[pltpu.VMEM((1, hw), F32)],
    )(rs, dc, gx, b_exp, tri)


def _dil_bwd(dqkv, do, lj, delta, bias, branch):
    dil = DILATIONS[branch]
    s_len = dqkv.shape[0]
    w = DIL_BLOCK
    hw = HEAD_WIDTH
    length = s_len // dil
    nb = length // w
    qkv_v = dqkv.reshape(length, dil * 3 * hw)
    do_v = do.reshape(length, dil * hw)
    dl_v = delta.reshape(length, dil * hw)
    lj_v = lj.reshape(length, dil * hw)

    def body(q0_ref, q1_ref, kp_ref, kc_ref, vp_ref, vc_ref, do0_ref, do1_ref, l0_ref, l1_ref, d0_ref, d1_ref, b_ref,
             dq_ref, dk_ref, dv_ref, dsum_ref):
        r, n = pl.program_id(0), pl.program_id(1)
        in0 = _first_head_lanes()
        not0 = jnp.logical_not(in0)
        first = n == 0
        last = n == nb - 1

        @pl.when(jnp.logical_and(r == 0, n == 0))
        def _():
            dsum_ref[...] = jnp.zeros_like(dsum_ref)

        pairs = [slice(pr * PAIR, (pr + 1) * PAIR) for pr in range(N_PAIRS)]

        def both_heads(ref, sl):
            v = ref[:, sl]
            return jnp.concatenate([_zero_other(in0, v), _zero_other(not0, v)], axis=0)

        def head_columns(ref):
            return jnp.concatenate([ref[:, h * HEAD_DIM:h * HEAD_DIM + 1] for h in range(N_HEADS)], axis=0)

        qq0 = [both_heads(q0_ref, sl) for sl in pairs]
        qq1 = [both_heads(q1_ref, sl) for sl in pairs]
        dd0 = [both_heads(do0_ref, sl) for sl in pairs]
        dd1 = [both_heads(do1_ref, sl) for sl in pairs]
        stack = lambda tiles: jnp.concatenate(tiles, axis=0)
        s_a = stack([_dot_nt(qq0[i], kp_ref[:, sl]) for i, sl in enumerate(pairs)])
        s_b = stack([_dot_nt(qq0[i], kc_ref[:, sl]) for i, sl in enumerate(pairs)])
        s_c = stack([_dot_nt(qq1[i], kc_ref[:, sl]) for i, sl in enumerate(pairs)])
        dp_a = stack([_dot_nt(dd0[i], vp_ref[:, sl]) for i, sl in enumerate(pairs)])
        dp_b = stack([_dot_nt(dd0[i], vc_ref[:, sl]) for i, sl in enumerate(pairs)])
        dp_c = stack([_dot_nt(dd1[i], vc_ref[:, sl]) for i, sl in enumerate(pairs)])
        bias2 = b_ref[...].reshape(N_HEADS * w, 2 * w)
        b_prev, b_cur = bias2[:, 0:w], bias2[:, w:2 * w]
        lse0, lse1 = head_columns(l0_ref), head_columns(l1_ref)
        dl0, dl1 = head_columns(d0_ref), head_columns(d1_ref)
        p_a = jnp.exp(jnp.where(first, NEG, s_a + b_prev) - lse0)
        p_b = jnp.exp((s_b + b_cur) - lse0)
        p_c = jnp.exp(jnp.where(last, NEG, s_c + b_prev) - lse1)
        ds_a = p_a * (dp_a - dl0)
        ds_b = p_b * (dp_b - dl0)
        ds_c = p_c * (dp_c - dl1)
        dsum_ref[...] += jnp.concatenate([ds_a, ds_b], axis=1).reshape(N_HEADS, w, 2 * w)
        ds_a, ds_b, ds_c = ds_a.astype(BF16), ds_b.astype(BF16), ds_c.astype(BF16)
        p_b, p_c = p_b.astype(BF16), p_c.astype(BF16)
        for i, sl in enumerate(pairs):
            rows = slice(2 * i * w, (2 * i + 2) * w)
            dq2 = _dot(ds_a[rows], kp_ref[:, sl]) + _dot(ds_b[rows], kc_ref[:, sl])
            dq_ref[:, sl] = _pick(in0, dq2[:w], dq2[w:])
            dk_ref[:, sl] = _dot_tn(ds_b[rows], qq0[i]) + _dot_tn(ds_c[rows], qq1[i])
            dv_ref[:, sl] = _dot_tn(p_b[rows], dd0[i]) + _dot_tn(p_c[rows], dd1[i])

    prev = lambda n: jnp.maximum(n - 1, 0)
    nxt = lambda n: jnp.minimum(n + 1, nb - 1)
    blk = (w, hw)
    outs = pl.pallas_call(
        body, name=f"dil_bwd_{dil}", grid=(dil, nb),
        in_specs=[
            pl.BlockSpec(blk, lambda r, n: (n, 3 * r)),
            pl.BlockSpec(blk, lambda r, n: (nxt(n), 3 * r)),
            pl.BlockSpec(blk, lambda r, n: (prev(n), 3 * r + 1)),
            pl.BlockSpec(blk, lambda r, n: (n, 3 * r + 1)),
            pl.BlockSpec(blk, lambda r, n: (prev(n), 3 * r + 2)),
            pl.BlockSpec(blk, lambda r, n: (n, 3 * r + 2)),
            pl.BlockSpec(blk, lambda r, n: (n, r)),
            pl.BlockSpec(blk, lambda r, n: (nxt(n), r)),
            pl.BlockSpec(blk, lambda r, n: (n, r)),
            pl.BlockSpec(blk, lambda r, n: (nxt(n), r)),
            pl.BlockSpec(blk, lambda r, n: (n, r)),
            pl.BlockSpec(blk, lambda r, n: (nxt(n), r)),
            pl.BlockSpec((None, N_HEADS, w, 2 * w), lambda r, n: (branch, 0, 0, 0)),
        ],
        out_specs=[pl.BlockSpec(blk, lambda r, n: (n, r))] * 3 + [pl.BlockSpec((N_HEADS, w, 2 * w), lambda r, n: (0, 0, 0))],
        out_shape=[_sds((length, dil * hw), F32)] * 3 + [_sds((N_HEADS, w, 2 * w), F32)],
        compiler_params=pltpu.CompilerParams(dimension_semantics=("arbitrary",) * 2, vmem_limit_bytes=V7X_VMEM_LIMIT_BYTES),
    )(qkv_v, qkv_v, qkv_v, qkv_v, qkv_v, qkv_v, do_v, do_v, lj_v, lj_v, dl_v, dl_v, bias)
    return [o.reshape(s_len, hw) for o in outs[:3]] + [outs[3]]


def _rel_bias_grad(dsum, buckets):
    w = DIL_BLOCK

    def body(ds_ref, bk_ref, o_ref):
        row = lax.broadcasted_iota(jnp.int32, (N_BUCKETS, PAIR), 0)
        lane = lax.broadcasted_iota(jnp.int32, (N_BUCKETS, PAIR), 1)

        def per_bucket(b, acc):
            for p in range(len(DILATIONS)):
                hit = bk_ref[p] == b
                for h in range(N_HEADS):
                    part = jnp.where(hit, ds_ref[p, h], 0.0)
                    tot = jnp.sum(jnp.sum(part, axis=1, keepdims=True), axis=0, keepdims=True)
                    acc = acc + jnp.where(jnp.logical_and(row == b, lane == h), tot, 0.0)
            return acc

        o_ref[...] = lax.fori_loop(0, N_BUCKETS, per_bucket, jnp.zeros((N_BUCKETS, PAIR), F32))

    return pl.pallas_call(body, name="rel_bias_grad", out_shape=_sds((N_BUCKETS, PAIR), F32))(dsum, buckets)


def _in_proj_bwd(dfq, dfk, dfv, dgx, ddq, ddk, ddv, w_in_t, dx1, x, g_pre):
    s_len, d = x.shape
    hw = HEAD_WIDTH
    tm = ROW_TILE

    def body(fq, fk, fv, gx, q1, q2, q3, k1, k2, k3, v1, v2, v3, w_ref, dx_ref, x_ref, g_ref, o_ref, dp_ref, dg_ref):
        @pl.when(pl.program_id(0) == 0)
        def _():
            dg_ref[...] = jnp.zeros_like(dg_ref)

        dp_ref[:, 0:hw] = fq[...]
        dp_ref[:, hw:2 * hw] = fk[...]
        dp_ref[:, 2 * hw:3 * hw] = fv[...]
        dp_ref[:, 3 * hw:4 * hw] = gx[...]
        dp_ref[:, 4 * hw:5 * hw] = (((q1[...] + q2[...]) + q3[...]) * QK_SCALE).astype(BF16)
        dp_ref[:, 5 * hw:6 * hw] = ((k1[...] + k2[...]) + k3[...]).astype(BF16)
        dp_ref[:, 6 * hw:7 * hw] = ((v1[...] + v2[...]) + v3[...]).astype(BF16)
        dh = _dot(dp_ref[...], w_ref[...])
        dx, dg = _rms_bwd(x_ref[...], g_ref[...], dh)
        dg_ref[...] += dg
        o_ref[...] = dx_ref[...] + dx

    half = _rows(tm, hw)
    return _call(
        body, name="in_proj_bwd", grid=(s_len // tm,),
        in_specs=[half] * 13 + [_resident(w_in_t.shape), _rows(tm, d), _rows(tm, d), _resident((1, d))],
        out_specs=[_rows(tm, d), _rows(tm, 7 * hw), _acc_out((1, d))],
        out_shape=[_sds((s_len, d), F32), _sds((s_len, 7 * hw), BF16), _sds((1, d), F32)],
    )(dfq, dfk, dfv, dgx, *ddq, *ddk, *ddv, w_in_t, dx1, x, g_pre)


def _expand_w_in(w_in):
    hw = HEAD_WIDTH
    gate = jnp.repeat(w_in[:, 3 * hw:3 * hw + N_HEADS], HEAD_DIM, axis=1)
    return jnp.concatenate([w_in[:, :3 * hw], gate, w_in[:, 3 * hw + N_HEADS:]], axis=1)


def _local_step(x, mem, target, g, w_bf):
    hw = HEAD_WIDTH
    w_in_e = _expand_w_in(w_bf["w_in"])
    b_exp = jnp.repeat(g["b_f"], HEAD_DIM, axis=1)
    buckets = jnp.asarray(_dil_buckets())

    h1, fqkv, gx, dqkv = _in_proj(x, g["g_mix_pre"], w_in_e)
    aq, ak, c = _gate_scan(gx, b_exp)
    stats = _fox_block_stats(fqkv, c)
    o_fox, lse_fox = _fox_fwd(fqkv, aq, ak, stats)
    bias = _dil_bias(g["rel_bias"], buckets)
    branches = [_dil_fwd(dqkv, bias, p) for p in range(len(DILATIONS))]
    x1, y1, o_dil, lj = _mix_out(o_fox, [b[0] for b in branches], [b[1] for b in branches], w_bf["w_out"], x, g["g_mix_post"])
    hm, kx, vx = _mem_fwd(mem, g["g_mem"], w_bf["w_xk"], w_bf["w_xv"])
    x2, y2, h2, qx, ox = _xattn_fwd(x1, g["g_xattn_pre"], w_bf["w_xq"], kx, vx, w_bf["w_xo"], g["g_xattn_post"])
    h3, a, u, z = _ffn_up(x2, g["g_ffn_pre"], w_bf["w_gate"], w_bf["w_up"])
    y3, dx3, sq = _ffn_down_loss(z, w_bf["w_down"], x2, g["g_ffn_post"], target)

    grads = {}
    dy3, da, du, grads["g_ffn_post"] = _ffn_bwd_a(dx3, y3, g["g_ffn_post"], w_bf["w_down"].T, a, u)
    dx2, grads["g_ffn_pre"] = _ffn_bwd_b(da, du, w_bf["w_gate"].T, w_bf["w_up"].T, dx3, x2, g["g_ffn_pre"])
    grads["w_down"] = _weight_grad(z, dy3, "dw_down")
    grads["w_gate"] = _weight_grad(h3, da, "dw_gate")
    grads["w_up"] = _weight_grad(h3, du, "dw_up")
    dx1, dy2, dqx, dkx, dvx, grads["g_xattn_post"], grads["g_xattn_pre"] = _xattn_bwd(
        dx2, y2, g["g_xattn_post"], w_bf["w_xo"].T, qx, kx, vx, w_bf["w_xq"].T, x1, g["g_xattn_pre"])
    grads["w_xo"] = _weight_grad(ox, dy2, "dw_xo")
    grads["w_xq"] = _weight_grad(h2, dqx, "dw_xq")
    grads["w_xk"], grads["w_xv"], grads["g_mem"] = _mem_bwd(dkx, dvx, w_bf["w_xk"].T, w_bf["w_xv"].T, hm, mem, g["g_mem"])
    dy1, do_fox, do_dil, delta_fox, delta_dil, grads["g_mix_post"] = _mix_out_bwd(
        dx1, y1, g["g_mix_post"], w_bf["w_out"].T, o_fox, o_dil)
    grads["w_out"] = jnp.concatenate([_weight_grad(o_fox, dy1, "dw_out_fox"), _weight_grad(o_dil, dy1, "dw_out_dil")], axis=0)
    aql, ad = _fox_bwd_prep(aq, lse_fox, delta_fox)
    dfq, rs, dfk, dfv, dc = _fox_bwd(fqkv, do_fox, aql, ak, ad, stats)
    dgx, db = _gate_bwd(rs, dc, gx, b_exp)
    grads["b_f"] = db[:, ::HEAD_DIM]
    dil = [_dil_bwd(dqkv, do_dil, lj, delta_dil, bias, p) for p in range(len(DILATIONS))]
    grads["rel_bias"] = _rel_bias_grad(jnp.stack([t[3] for t in dil]), buckets)[:, :N_HEADS]
    grad_x, dproj, grads["g_mix_pre"] = _in_proj_bwd(
        dfq, dfk, dfv, dgx, [t[0] for t in dil], [t[1] for t in dil], [t[2] for t in dil], w_in_e.T, dx1, x, g["g_mix_pre"])
    dw_in_e = _weight_grad(h1, dproj, "dw_in")
    grads["w_in"] = jnp.concatenate(
        [dw_in_e[:, :3 * hw], dw_in_e[:, 3 * hw:4 * hw:HEAD_DIM], dw_in_e[:, 4 * hw:]], axis=1)
    return sq, grad_x, grads


MESH = pl.DeviceIdType.MESH


def _my_place():
    return lax.axis_index("x"), lax.axis_index("y"), lax.axis_index("c")


def _all_gather(x, name):
    rows, lanes = x.shape

    def body(x_ref, out_ref, send_sems, recv_sems, local_sem):
        mx, my, mc = _my_place()
        me, sibling = (mx, my, mc), (mx, my, 1 - mc)
        chips = [(1 - mx, my), (mx, 1 - my), (1 - mx, 1 - my)]

        def slot(px, py, pc):
            return out_ref.at[4 * px + 2 * py + pc]

        def copy(k, block, to, src=None):
            return pltpu.make_async_remote_copy(
                src_ref=slot(*block) if src is None else src, dst_ref=slot(*block),
                send_sem=send_sems.at[k], recv_sem=recv_sems.at[k], device_id=to, device_id_type=MESH)

        mine = pltpu.make_async_copy(x_ref, slot(*me), local_sem)
        mine.start()
        first = [copy(0, me, sibling, src=x_ref)]
        first += [copy(1 + j, me, (*chip, mc), src=x_ref) for j, chip in enumerate(chips)]
        for cp in first:
            cp.start()
        passed = [copy(4 + j, (*chip, mc), sibling) for j, chip in enumerate(chips)]
        for j, chip in enumerate(chips):
            copy(1 + j, (*chip, mc), me).wait_recv()
            passed[j].start()
        copy(0, sibling, me).wait_recv()
        for j, chip in enumerate(chips):
            copy(4 + j, (*chip, 1 - mc), me).wait_recv()
        for cp in first + passed:
            cp.wait_send()
        mine.wait()

    return pl.pallas_call(
        body, name=name, out_shape=_sds((N_DEV, rows, lanes), x.dtype),
        in_specs=[pl.BlockSpec(memory_space=pl.ANY)], out_specs=pl.BlockSpec(memory_space=pl.ANY),
        scratch_shapes=[pltpu.SemaphoreType.DMA((N_DEV - 1,)), pltpu.SemaphoreType.DMA((N_DEV - 1,)), pltpu.SemaphoreType.DMA],
    )(x)


def _exchange(g, name):
    _, rows, lanes = g.shape

    def body(g_ref, land_ref, send_sems, recv_sems, local_sem):
        mx, my, mc = _my_place()
        me = 4 * mx + 2 * my + mc
        mine = pltpu.make_async_copy(g_ref.at[me], land_ref.at[me], local_sem)
        mine.start()
        sent, arriving = [], []
        for k in (1, 2, 3, 4, 5, 6, 7):
            px = 1 - mx if k & 4 else mx
            py = 1 - my if k & 2 else my
            pc = 1 - mc if k & 1 else mc
            peer = 4 * px + 2 * py + pc
            cp = pltpu.make_async_remote_copy(
                src_ref=g_ref.at[peer], dst_ref=land_ref.at[me], send_sem=send_sems.at[k - 1], recv_sem=recv_sems.at[k - 1],
                device_id=(px, py, pc), device_id_type=MESH)
            cp.start()
            sent.append(cp)
            arriving.append(pltpu.make_async_remote_copy(
                src_ref=g_ref.at[me], dst_ref=land_ref.at[peer], send_sem=send_sems.at[k - 1], recv_sem=recv_sems.at[k - 1],
                device_id=(px, py, pc), device_id_type=MESH))
        for cp in arriving:
            cp.wait_recv()
        for cp in sent:
            cp.wait_send()
        mine.wait()

    return pl.pallas_call(
        body, name=name, out_shape=_sds(g.shape, g.dtype),
        in_specs=[pl.BlockSpec(memory_space=pl.ANY)], out_specs=pl.BlockSpec(memory_space=pl.ANY),
        scratch_shapes=[pltpu.SemaphoreType.DMA((N_DEV - 1,)), pltpu.SemaphoreType.DMA((N_DEV - 1,)), pltpu.SemaphoreType.DMA],
    )(g)


def _sum_slots(parts, name):
    n, rows, lanes = parts.shape
    tr = 512 if rows % 512 == 0 else rows

    def body(p_ref, o_ref):
        acc = p_ref[0].astype(F32)
        for j in range(1, n):
            acc = acc + p_ref[j].astype(F32)
        o_ref[...] = acc

    return _call(
        body, name=name, grid=(rows // tr,),
        in_specs=[pl.BlockSpec((n, tr, lanes), lambda i: (0, i, 0))], out_specs=_rows(tr, lanes),
        out_shape=_sds((rows, lanes), F32),
    )(parts)


def _adamw(w, g, m, v, name):
    def body(w_ref, g_ref, m_ref, v_ref, d_ref, nm_ref, nv_ref):
        gv = g_ref[...]
        m_new = ADAM_B1 * m_ref[...] + (1.0 - ADAM_B1) * gv
        v_new = ADAM_B2 * v_ref[...] + (1.0 - ADAM_B2) * (gv * gv)
        nm_ref[...] = m_new
        nv_ref[...] = v_new
        m_hat = m_new / (1.0 - ADAM_B1 ** ADAM_STEP)
        v_hat = v_new / (1.0 - ADAM_B2 ** ADAM_STEP)
        d_ref[...] = -ADAM_LR * (m_hat / (jnp.sqrt(v_hat) + ADAM_EPS) + ADAM_WD * w_ref[...])

    out = _sds(w.shape, F32)
    return pl.pallas_call(
        body, name=name, out_shape=[out, out, out],
        compiler_params=pltpu.CompilerParams(vmem_limit_bytes=V7X_VMEM_LIMIT_BYTES),
    )(w, g, m, v)


def _loss_head(sq, d_model):
    def body(sq_ref, o_ref):
        tot = jnp.sum(jnp.sum(sq_ref[...], axis=1, keepdims=True), axis=0, keepdims=True)
        o_ref[...] = 0.5 * (tot / d_model)

    return pl.pallas_call(body, name="loss_head", out_shape=_sds((1, 1), F32))(sq)


_BIG = (("w_in", 1), ("w_out", 0), ("w_xq", 0), ("w_xk", 0), ("w_xv", 0), ("w_xo", 1), ("w_gate", 1), ("w_up", 1), ("w_down", 0))
_SMALL = ("g_mix_pre", "b_f", "rel_bias", "g_mix_post", "g_xattn_pre", "g_mem", "g_xattn_post", "g_ffn_pre", "g_ffn_post")
LANES = 128
BIG_ROW_ALIGN = 512


def _round_up(n, k):
    return -(-n // k) * k


def _pack_rows(flat_parts, row_align, dtype):
    starts, rows, padded = [], 0, []
    for p in flat_parts:
        n = _round_up(p.shape[0], LANES)
        starts.append(rows)
        rows += n // LANES
        padded.append(jnp.pad(p.astype(dtype), (0, n - p.shape[0])))
    total = _round_up(rows, row_align)
    padded.append(jnp.zeros(((total - rows) * LANES,), dtype))
    return jnp.concatenate(padded).reshape(total, LANES), starts


def _unpack_rows(buf, starts, shapes):
    lead = buf.shape[:-2]
    flat = buf.reshape(lead + (-1,))
    out = []
    for st, shp in zip(starts, shapes):
        n = int(np.prod(shp))
        out.append(flat[..., st * LANES:st * LANES + n].reshape(lead + tuple(shp)))
    return out


def kernel(x, mem, g_mix_pre, w_in, b_f, rel_bias, w_out, g_mix_post, g_xattn_pre, g_mem, w_xq, w_xk, w_xv, w_xo, g_xattn_post, g_ffn_pre, w_gate, w_up, w_down, g_ffn_post, loss_target, m_g_mix_pre, m_w_in, m_b_f, m_rel_bias, m_w_out, m_g_mix_post, m_g_xattn_pre, m_g_mem, m_w_xq, m_w_xk, m_w_xv, m_w_xo, m_g_xattn_post, m_g_ffn_pre, m_w_gate, m_w_up, m_w_down, m_g_ffn_post, v_g_mix_pre, v_w_in, v_b_f, v_rel_bias, v_w_out, v_g_mix_post, v_g_xattn_pre, v_g_mem, v_w_xq, v_w_xk, v_w_xv, v_w_xo, v_g_xattn_post, v_g_ffn_pre, v_w_gate, v_w_up, v_w_down, v_g_ffn_post):
    given = dict(locals())
    order = ("g_mix_pre", "w_in", "b_f", "rel_bias", "w_out", "g_mix_post", "g_xattn_pre", "g_mem", "w_xq", "w_xk", "w_xv",
             "w_xo", "g_xattn_post", "g_ffn_pre", "w_gate", "w_up", "w_down", "g_ffn_post")
    two_d = lambda a: a.reshape(a.shape[-2:])
    w_loc = {n: two_d(given[n]) for n in order}
    m_loc = {n: two_d(given["m_" + n]) for n in order}
    v_loc = {n: two_d(given["v_" + n]) for n in order}
    d_model = x.shape[-1]

    shard_shapes = [w_loc[n].shape for n, _ in _BIG]
    packed, starts = _pack_rows([w_loc[n].reshape(-1) for n, _ in _BIG], BIG_ROW_ALIGN, BF16)
    gathered = _all_gather(packed, "gather_weights")
    w_bf = {}
    for (n, axis), part in zip(_BIG, _unpack_rows(gathered, starts, shard_shapes)):
        r, c = part.shape[1:]
        w_bf[n] = part.reshape(N_DEV * r, c) if axis == 0 else part.transpose(1, 0, 2).reshape(r, N_DEV * c)

    small = {n: w_loc[n] for n in _SMALL}
    sq, grad_x, grads = _local_step(two_d(x), two_d(mem), two_d(loss_target), small, w_bf)

    per_owner = []
    for (n, axis), shp in zip(_BIG, shard_shapes):
        r, c = shp
        gfull = grads[n]
        per_owner.append(gfull.reshape(N_DEV, r * c) if axis == 0 else gfull.reshape(r, N_DEV, c).transpose(1, 0, 2).reshape(N_DEV, r * c))
    rows_big = packed.shape[0]
    slots = []
    for j in range(N_DEV):
        buf, _ = _pack_rows([p[j] for p in per_owner], BIG_ROW_ALIGN, BF16)
        slots.append(buf)
    landed = _exchange(jnp.stack(slots), "exchange_grads")
    g_big = dict(zip([n for n, _ in _BIG], _unpack_rows(_sum_slots(landed, "sum_grads"), starts, shard_shapes)))
    assert landed.shape[1] == rows_big

    small_parts = [grads[n].reshape(-1) for n in _SMALL] + [sq.reshape(-1)]
    small_shapes = [w_loc[n].shape for n in _SMALL] + [sq.shape]
    spacked, sstarts = _pack_rows(small_parts, 8, F32)
    ssum = _sum_slots(_all_gather(spacked, "gather_small"), "sum_small")
    g_small = dict(zip(_SMALL, _unpack_rows(ssum, sstarts, small_shapes)[:-1]))
    sq_rows = sq.size // LANES
    loss = _loss_head(ssum[sstarts[-1]:sstarts[-1] + sq_rows], d_model).reshape(())

    g_loc, delta, new_m, new_v = {}, {}, {}, {}
    for n, _ in _BIG:
        g_loc[n] = g_big[n]
        delta[n], new_m[n], new_v[n] = _adamw(w_loc[n], g_big[n], m_loc[n], v_loc[n], "adamw_" + n)
    pk = lambda d: _pack_rows([d[n].reshape(-1) for n in _SMALL], 8, F32)[0]
    pstarts = _pack_rows([w_loc[n].reshape(-1) for n in _SMALL], 8, F32)[1]
    d_s, m_s, v_s = _adamw(pk(w_loc), pk(g_small), pk(m_loc), pk(v_loc), "adamw_small")
    shapes_s = [w_loc[n].shape for n in _SMALL]
    for n, dd, mm, vv in zip(_SMALL, _unpack_rows(d_s, pstarts, shapes_s), _unpack_rows(m_s, pstarts, shapes_s),
                             _unpack_rows(v_s, pstarts, shapes_s)):
        g_loc[n], delta[n], new_m[n], new_v[n] = g_small[n], dd, mm, vv

    like = lambda d: [d[n].reshape(given[n].shape) for n in order]
    return (loss, grad_x.reshape(x.shape), *like(g_loc), *like(delta), *like(new_m), *like(new_v))
```

```python
import functools

import numpy as np
import jax
import jax.numpy as jnp
from jax import lax
from jax.experimental import pallas as pl
from jax.experimental.pallas import tpu as pltpu

F32 = jnp.float32
BF16 = jnp.bfloat16
HIGHEST = lax.Precision.HIGHEST

RMS_EPS = 1e-6
HEAD_DIM = 64
N_HEADS = 8
HEAD_WIDTH = N_HEADS * HEAD_DIM
PAIR = 2 * HEAD_DIM
N_PAIRS = N_HEADS // 2
DIL_BLOCK = 128
DILATIONS = (1, 4, 16)
N_BUCKETS = 32
MAX_DISTANCE = 2048
N_MEM_HEADS = 4
QK_SCALE = HEAD_DIM ** -0.5
NEG = -1e30
FOX_SKIP_MARGIN = 110.0
N_DEV = 8

ADAM_LR = 0.001
ADAM_B1 = 0.9
ADAM_B2 = 0.999
ADAM_EPS = 1e-08
ADAM_WD = 0.01
ADAM_STEP = 10

V7X_VMEM_LIMIT_BYTES = 56 * 2 ** 20
ROW_TILE = 256
ATT_BLOCK = 512
SCAN_TILE = 256


def _call(body, *, name, grid, in_specs, out_specs, out_shape, scratch=()):
    return pl.pallas_call(
        body, name=name, grid=grid, in_specs=in_specs, out_specs=out_specs, out_shape=out_shape,
        scratch_shapes=list(scratch),
        compiler_params=pltpu.CompilerParams(
            dimension_semantics=("arbitrary",) * len(grid), vmem_limit_bytes=V7X_VMEM_LIMIT_BYTES))


def _rows(tm, n):
    return pl.BlockSpec((tm, n), lambda i: (i, 0))


def _resident(shape):
    zeros = (0,) * len(shape)
    return pl.BlockSpec(shape, lambda i: zeros, pipeline_mode=pl.Buffered(1))


def _acc_out(shape):
    zeros = (0,) * len(shape)
    return pl.BlockSpec(shape, lambda i: zeros)


def _sds(shape, dtype):
    return jax.ShapeDtypeStruct(shape, dtype)


def _dot(a, b):
    return jnp.dot(a, b, preferred_element_type=F32)


def _dot_nt(a, b):
    return lax.dot_general(a, b, (((1,), (1,)), ((), ())), preferred_element_type=F32)


def _dot_tn(a, b):
    return lax.dot_general(a, b, (((0,), (0,)), ((), ())), preferred_element_type=F32)


def _rms_fwd(x, g):
    r = lax.rsqrt(jnp.mean(x * x, axis=-1, keepdims=True) + RMS_EPS)
    return (x * r) * g


def _rms_bwd(xin, g, dy):
    r = lax.rsqrt(jnp.mean(xin * xin, axis=-1, keepdims=True) + RMS_EPS)
    xhat = xin * r
    dg = jnp.sum(dy * xhat, axis=0, keepdims=True)
    dxh = dy * g
    dx = r * (dxh - xhat * jnp.mean(dxh * xhat, axis=-1, keepdims=True))
    return dx, dg


def _first_head_lanes():
    return lax.broadcasted_iota(jnp.int32, (1, PAIR), 1) < HEAD_DIM


def _pick(mask, a, b):
    return jnp.where(mask, a, b)


def _zero_other(mask, v):
    return jnp.where(mask, v, jnp.zeros_like(v))


def _store_lane_blocks(buf_ref, val):
    for cb in range(buf_ref.shape[0]):
        buf_ref[cb] = val[:, cb * PAIR:(cb + 1) * PAIR].astype(F32)


def _load_lane_blocks(buf_ref):
    return jnp.concatenate([buf_ref[cb] for cb in range(buf_ref.shape[0])], axis=1)


def _write_class_major(buf_ref, out_ref, dil):
    n, tile, _ = buf_ref.shape
    for r in range(dil):
        for cb in range(n):
            col = (r * n + cb) * PAIR
            out_ref[:, col:col + PAIR] = buf_ref.at[cb][pl.ds(r, tile // dil, stride=dil), :].astype(out_ref.dtype)


def _read_class_major(in_ref, buf_ref, dil):
    n, tile, _ = buf_ref.shape
    for r in range(dil):
        for cb in range(n):
            col = (r * n + cb) * PAIR
            buf_ref.at[cb][pl.ds(r, tile // dil, stride=dil), :] = in_ref[:, col:col + PAIR].astype(F32)
    return _load_lane_blocks(buf_ref)


def _class_rows(tm, width, dil):
    return _rows(tm // dil, dil * width)


def _in_proj(x, g, w):
    s_len, d = x.shape
    tm = ROW_TILE
    hw = HEAD_WIDTH

    def body(x_ref, g_ref, w_ref, h_ref, fqkv_ref, gx_ref, *rest):
        dil_refs, buf = rest[:-1], rest[-1]
        h = _rms_fwd(x_ref[...], g_ref[...]).astype(BF16)
        h_ref[...] = h
        proj = _dot(h, w_ref[...])
        fqkv_ref[:, 0:hw] = (proj[:, 0:hw] * QK_SCALE).astype(BF16)
        fqkv_ref[:, hw:3 * hw] = proj[:, hw:3 * hw].astype(BF16)
        gx_ref[...] = proj[:, 3 * hw:4 * hw]
        dqkv = jnp.concatenate([proj[:, 4 * hw:5 * hw] * QK_SCALE, proj[:, 5 * hw:7 * hw]], axis=1)
        _store_lane_blocks(buf, dqkv)
        for ref, dil in zip(dil_refs, DILATIONS):
            if dil == 1:
                ref[...] = dqkv.astype(BF16)
            else:
                _write_class_major(buf, ref, dil)

    return _call(
        body, name="in_proj", grid=(s_len // tm,),
        in_specs=[_rows(tm, d), _resident((1, d)), _resident(w.shape)],
        out_specs=[_rows(tm, d), _rows(tm, 3 * hw), _rows(tm, hw)] + [_class_rows(tm, 3 * hw, dil) for dil in DILATIONS],
        out_shape=[_sds((s_len, d), BF16), _sds((s_len, 3 * hw), BF16), _sds((s_len, hw), F32)]
        + [_sds((s_len // dil, dil * 3 * hw), BF16) for dil in DILATIONS],
        scratch=[pltpu.VMEM((3 * hw // PAIR, tm, PAIR), F32)],
    )(x, g, w)


def _swap_halves(x):
    return jnp.concatenate([pltpu.roll(x[:, i * PAIR:(i + 1) * PAIR], HEAD_DIM, 1) for i in range(x.shape[1] // PAIR)], axis=1)


def _split3(x):
    hi = x.astype(BF16)
    r = x - hi.astype(F32)
    mid = r.astype(BF16)
    lo = (r - mid.astype(F32)).astype(BF16)
    return hi, mid, lo


def _lane_in_head(width):
    return lax.broadcasted_iota(jnp.int32, (1, width), 1) % HEAD_DIM


def _place3(jj, first, pieces, base):
    out = base
    for i, p in enumerate(pieces):
        out = jnp.where(jj == first + i, p, out)
    return out


def _gate_scan(gx, b_exp):
    s_len, hw = gx.shape
    t = min(SCAN_TILE, s_len)
    tri = jnp.asarray(np.tril(np.ones((t, t), np.float32)))

    def body(gx_ref, b_ref, tri_ref, aq_ref, ak_ref, c_ref, carry):
        @pl.when(pl.program_id(0) == 0)
        def _():
            carry[...] = jnp.zeros_like(carry)

        z = gx_ref[...] + b_ref[...]
        lf = jnp.minimum(z, 0.0) - jnp.log1p(jnp.exp(-jnp.abs(z)))
        c = jnp.dot(tri_ref[...], lf, precision=HIGHEST, preferred_element_type=F32) + carry[...]
        carry[...] = c[t - 1:t, :]
        c_ref[...] = c
        hi, mid, lo = _split3(_swap_halves(c))
        jj = _lane_in_head(hw)
        zero = jnp.zeros_like(hi)
        one = jnp.ones_like(hi)
        aq_ref[...] = _place3(jj, 0, (hi, mid, lo), jnp.where(jj < 6, one, zero))
        ak_ref[...] = _place3(jj, 3, (-hi, -mid, -lo), jnp.where(jj < 9, one, zero))

    return _call(
        body, name="gate_scan", grid=(s_len // t,),
        in_specs=[_rows(t, hw), _resident((1, hw)), _resident((t, t))],
        out_specs=[_rows(t, hw), _rows(t, hw), _rows(t, hw)],
        out_shape=[_sds((s_len, hw), BF16), _sds((s_len, hw), BF16), _sds((s_len, hw), F32)],
        scratch=[pltpu.VMEM((1, hw), F32)],
    )(gx, b_exp, tri)


def _head_block_ones():
    head_of = np.arange(HEAD_WIDTH) // HEAD_DIM
    return jnp.asarray((head_of[:, None] == head_of[None, :]).astype(np.float32))


def _fox_block_stats(fqkv, c):
    s_len = fqkv.shape[0]
    hw = HEAD_WIDTH
    b = min(ATT_BLOCK, s_len)
    nb = s_len // b

    def body(q_ref, k_ref, c_ref, ones_ref, o_ref):
        q, k, cv = q_ref[...].astype(F32), k_ref[...].astype(F32), c_ref[...]
        seg = lambda x: jnp.dot(x, ones_ref[...], precision=HIGHEST, preferred_element_type=F32)
        col_max = lambda x: jnp.max(x, axis=0, keepdims=True)
        col_min = lambda x: jnp.min(x, axis=0, keepdims=True)
        o_ref[0] = jnp.concatenate(
            [jnp.sqrt(col_max(seg(q * q))), col_max(cv) - col_min(seg(q * k)), jnp.sqrt(col_max(seg(k * k))), col_min(cv),
             jnp.zeros((4, hw), F32)], axis=0)

    stats = _call(
        body, name="fox_block_stats", grid=(nb,),
        in_specs=[pl.BlockSpec((b, hw), lambda i: (i, 0)), pl.BlockSpec((b, hw), lambda i: (i, 1)), _rows(b, hw),
                  _resident((hw, hw))],
        out_specs=pl.BlockSpec((1, 8, hw), lambda i: (i, 0, 0)),
        out_shape=_sds((nb, 8, hw), F32),
    )(fqkv, fqkv, c, _head_block_ones())
    st = jnp.transpose(stats[:, :4, ::HEAD_DIM], (1, 2, 0))
    bound = st[0][:, :, None] * st[2][:, None, :] + st[1][:, :, None] - st[3][:, None, :]
    need_h = jnp.logical_not(bound < -FOX_SKIP_MARGIN)
    need = jnp.logical_or(need_h[0::2], need_h[1::2])
    ii = lax.broadcasted_iota(jnp.int32, (1, nb, nb), 1)
    jj = lax.broadcasted_iota(jnp.int32, (1, nb, nb), 2)
    first_needed = jnp.min(jnp.where(jnp.logical_or(jnp.logical_and(need, jj < ii), jj == ii), jj, nb), axis=2)
    window = ii[:, :, 0] - first_needed + 1
    in_window = jnp.logical_and(jj >= first_needed[:, :, None], jj <= ii)
    last_query = jnp.max(jnp.where(in_window, ii, 0), axis=1)
    return jnp.concatenate([st.reshape(4 * N_HEADS, nb), window.astype(F32), last_query.astype(F32)], axis=0)


FOX_WINDOW_ROW = 4 * N_HEADS
FOX_LAST_QUERY_ROW = 4 * N_HEADS + N_PAIRS


def _tile_needed(st_ref, pair, i, j):
    need = None
    for a in (0, 1):
        h = 2 * pair + a
        bound = st_ref[h, i] * st_ref[2 * N_HEADS + h, j] + st_ref[N_HEADS + h, i] - st_ref[3 * N_HEADS + h, j]
        need_a = jnp.logical_not(bound < -FOX_SKIP_MARGIN)
        need = need_a if need is None else jnp.logical_or(need, need_a)
    return need


def _lower_triangle(n):
    return lax.broadcasted_iota(jnp.int32, (n, n), 1) <= lax.broadcasted_iota(jnp.int32, (n, n), 0)


def _fox_fwd(fqkv, aq, ak, stats):
    s_len = fqkv.shape[0]
    bq = bk = min(ATT_BLOCK, s_len)
    nq, nk = s_len // bq, s_len // bk

    def body(st_ref, q_ref, k_ref, v_ref, aq_ref, ak_ref, o_ref, lse_ref, qa_ref, m_ref, acc_ref):
        pair, qi, back = pl.program_id(0), pl.program_id(1), pl.program_id(2)
        kj = jnp.maximum(qi - back, 0)
        in0 = _first_head_lanes()
        not0 = jnp.logical_not(in0)

        @pl.when(back == 0)
        def _():
            q2, a2 = q_ref[...], aq_ref[...]
            qa_ref[0] = jnp.where(in0, q2, a2)
            qa_ref[1] = jnp.where(in0, a2, q2)
            m_ref[...] = jnp.full_like(m_ref, NEG)
            acc_ref[...] = jnp.zeros_like(acc_ref)

        def step(masked):
            k2, v2, a2 = k_ref[...], v_ref[...], ak_ref[...]
            one = jnp.ones_like(v2)
            for a, mine in ((0, in0), (1, not0)):
                s = _dot_nt(qa_ref[a], jnp.where(mine, k2, a2))
                if masked:
                    s = jnp.where(_lower_triangle(bq), s, NEG)
                m_old = m_ref[a]
                m_new = jnp.maximum(m_old, jnp.max(s, axis=1, keepdims=True))
                p = jnp.exp(s - jnp.tile(m_new, (1, bk // PAIR))).astype(BF16)
                acc_ref[a] = jnp.exp(m_old - m_new) * acc_ref[a] + _dot(p, jnp.where(mine, v2, one))
                m_ref[a] = m_new

        @pl.when(back == 0)
        def _():
            step(True)

        @pl.when(jnp.logical_and(jnp.logical_and(back > 0, back <= qi), _tile_needed(st_ref, pair, qi, kj)))
        def _():
            step(False)

        @pl.when(back == qi)
        def _():
            acc0, acc1 = acc_ref[0], acc_ref[1]
            l2 = pltpu.roll(_pick(in0, acc1, acc0), HEAD_DIM, 1)
            o_ref[...] = _pick(in0, acc0, acc1) / l2
            lse_ref[...] = _pick(in0, m_ref[0], m_ref[1]) + jnp.log(l2)

    blk = lambda rows: (rows, PAIR)
    qmap = lambda p, i, b, st: (i, p)

    def key_block(p, i, b, st):
        window = jnp.clip(st[FOX_WINDOW_ROW + p, i].astype(jnp.int32), 1, i + 1)
        return i - jnp.minimum(b, window - 1)

    return pl.pallas_call(
        body, name="fox_fwd",
        grid_spec=pltpu.PrefetchScalarGridSpec(
            num_scalar_prefetch=1, grid=(N_PAIRS, nq, nk),
            in_specs=[
                pl.BlockSpec(blk(bq), qmap),
                pl.BlockSpec(blk(bk), lambda p, i, b, st: (key_block(p, i, b, st), N_PAIRS + p)),
                pl.BlockSpec(blk(bk), lambda p, i, b, st: (key_block(p, i, b, st), 2 * N_PAIRS + p)),
                pl.BlockSpec(blk(bq), qmap),
                pl.BlockSpec(blk(bk), lambda p, i, b, st: (key_block(p, i, b, st), p)),
            ],
            out_specs=[pl.BlockSpec(blk(bq), qmap), pl.BlockSpec(blk(bq), qmap)],
            scratch_shapes=[pltpu.VMEM((2, bq, PAIR), BF16), pltpu.VMEM((2, bq, PAIR), F32), pltpu.VMEM((2, bq, PAIR), F32)]),
        out_shape=[_sds((s_len, HEAD_WIDTH), F32), _sds((s_len, HEAD_WIDTH), F32)],
        compiler_params=pltpu.CompilerParams(dimension_semantics=("arbitrary",) * 3, vmem_limit_bytes=V7X_VMEM_LIMIT_BYTES),
    )(stats, fqkv, fqkv, fqkv, aq, ak)


def _t5_bucket(dist):
    max_exact = N_BUCKETS // 2
    d = np.maximum(dist, 1).astype(np.float32)
    large = max_exact + (np.log(d / max_exact) / np.log(MAX_DISTANCE / max_exact) * (N_BUCKETS - max_exact)).astype(np.int32)
    large = np.minimum(large, N_BUCKETS - 1)
    return np.where(dist < max_exact, dist, large).astype(np.int32)


def _dil_buckets():
    w = DIL_BLOCK
    qi = np.arange(w)[:, None]
    kj = np.arange(2 * w)[None, :]
    sub = qi + w - kj
    band = (sub >= 0) & (sub <= w)
    out = [np.where(band, _t5_bucket(np.clip(sub, 0, w) * dil), -1) for dil in DILATIONS]
    return np.stack(out).astype(np.int32)


def _dil_bias(rel_bias, buckets):
    w = DIL_BLOCK

    def body(rb_ref, bk_ref, o_ref):
        for p in range(len(DILATIONS)):
            bk = bk_ref[p]
            for h in range(N_HEADS):
                def add(b, acc):
                    return acc + jnp.where(bk == b, rb_ref[b, h], 0.0)
                acc = lax.fori_loop(0, N_BUCKETS, add, jnp.zeros((w, 2 * w), F32))
                o_ref[p, h] = jnp.where(bk < 0, NEG, acc)

    return pl.pallas_call(
        body, name="dil_bias",
        in_specs=[pl.BlockSpec(memory_space=pltpu.SMEM), pl.BlockSpec(memory_space=pltpu.VMEM)],
        out_specs=pl.BlockSpec(memory_space=pltpu.VMEM),
        out_shape=_sds((len(DILATIONS), N_HEADS, w, 2 * w), F32),
    )(rel_bias, buckets)


def _dil_fwd(view, bias, branch):
    dil = DILATIONS[branch]
    w = DIL_BLOCK
    hw = HEAD_WIDTH
    length = view.shape[0]
    nb = length // w

    def body(q_ref, kc_ref, kp_ref, vc_ref, vp_ref, b_ref, o_ref, lse_ref):
        n = pl.program_id(1)
        in0 = _first_head_lanes()
        not0 = jnp.logical_not(in0)
        pairs = [slice(pr * PAIR, (pr + 1) * PAIR) for pr in range(N_PAIRS)]
        tiles = []
        for sl in pairs:
            q2 = q_ref[:, sl]
            qq = jnp.concatenate([_zero_other(in0, q2), _zero_other(not0, q2)], axis=0)
            tiles.append(jnp.concatenate([_dot_nt(qq, kp_ref[:, sl]), _dot_nt(qq, kc_ref[:, sl])], axis=1))
        s = jnp.concatenate(tiles, axis=0) + b_ref[...].reshape(N_HEADS * w, 2 * w)
        prev_half = lax.broadcasted_iota(jnp.int32, (1, 2 * w), 1) < w
        s = jnp.where(jnp.logical_and(n == 0, prev_half), NEG, s)
        m = jnp.max(s, axis=1, keepdims=True)
        e = jnp.exp(s - m)
        l = jnp.sum(e, axis=1, keepdims=True)
        p = (e / l).astype(BF16)
        lse = m + jnp.log(l)
        for pr, sl in enumerate(pairs):
            pp = p[2 * pr * w:(2 * pr + 2) * w]
            o2 = _dot(pp[:, :w], vp_ref[:, sl]) + _dot(pp[:, w:], vc_ref[:, sl])
            o_ref[:, sl] = _pick(in0, o2[:w], o2[w:])
            lse_ref[:, sl] = _pick(in0, lse[2 * pr * w:(2 * pr + 1) * w], lse[(2 * pr + 1) * w:(2 * pr + 2) * w])

    prev = lambda n: jnp.maximum(n - 1, 0)
    out = pl.pallas_call(
        body, name=f"dil_fwd_{dil}", grid=(dil, nb),
        in_specs=[
            pl.BlockSpec((w, hw), lambda r, n: (n, 3 * r)),
            pl.BlockSpec((w, hw), lambda r, n: (n, 3 * r + 1)),
            pl.BlockSpec((w, hw), lambda r, n: (prev(n), 3 * r + 1)),
            pl.BlockSpec((w, hw), lambda r, n: (n, 3 * r + 2)),
            pl.BlockSpec((w, hw), lambda r, n: (prev(n), 3 * r + 2)),
            pl.BlockSpec((None, N_HEADS, w, 2 * w), lambda r, n: (branch, 0, 0, 0)),
        ],
        out_specs=[pl.BlockSpec((w, hw), lambda r, n: (n, r)), pl.BlockSpec((w, hw), lambda r, n: (n, r))],
        out_shape=[_sds((length, dil * hw), F32), _sds((length, dil * hw), F32)],
        compiler_params=pltpu.CompilerParams(dimension_semantics=("arbitrary",) * 2, vmem_limit_bytes=V7X_VMEM_LIMIT_BYTES),
    )(view, view, view, view, view, bias)
    return out[0], out[1]


def _mix_out(o_fox, o_br, lse_br, w_out, x, g_post):
    s_len, d = x.shape
    hw = HEAD_WIDTH
    tm = ROW_TILE

    def body(of_ref, o1, o2, o3, l1, l2, l3, w_ref, x_ref, g_ref, x1_ref, y1_ref, od_ref, lj1, lj2, lj3, buf):
        def natural(ref, dil):
            return ref[...] if dil == 1 else _read_class_major(ref, buf, dil)

        ob = [natural(r, dil) for r, dil in zip((o1, o2, o3), DILATIONS)]
        la, lb, lc = [natural(r, dil) for r, dil in zip((l1, l2, l3), DILATIONS)]
        m = jnp.maximum(jnp.maximum(la, lb), lc)
        ea, eb, ec = jnp.exp(la - m), jnp.exp(lb - m), jnp.exp(lc - m)
        tot = ea + eb + ec
        o_dil = (ea / tot) * ob[0] + (eb / tot) * ob[1] + (ec / tot) * ob[2]
        od_ref[...] = o_dil
        lj = m + jnp.log(tot)
        _store_lane_blocks(buf, lj)
        for ref, dil in zip((lj1, lj2, lj3), DILATIONS):
            if dil == 1:
                ref[...] = lj
            else:
                _write_class_major(buf, ref, dil)
        y = _dot(of_ref[...].astype(BF16), w_ref[0:hw, :]) + _dot(o_dil.astype(BF16), w_ref[hw:2 * hw, :])
        y1_ref[...] = y
        x1_ref[...] = x_ref[...] + _rms_fwd(y, g_ref[...])

    half = _rows(tm, hw)
    by_dil = [_class_rows(tm, hw, dil) for dil in DILATIONS]
    return _call(
        body, name="mix_out", grid=(s_len // tm,),
        in_specs=[half] + by_dil + by_dil + [_resident(w_out.shape), _rows(tm, d), _resident((1, d))],
        out_specs=[_rows(tm, d), _rows(tm, d), half] + by_dil,
        out_shape=[_sds((s_len, d), F32), _sds((s_len, d), F32), _sds((s_len, hw), F32)]
        + [_sds((s_len // dil, dil * hw), F32) for dil in DILATIONS],
        scratch=[pltpu.VMEM((hw // PAIR, tm, PAIR), F32)],
    )(o_fox, *o_br, *lse_br, w_out, x, g_post)


def _mem_fwd(mem, g_mem, w_xk, w_xv):
    n_mem, d = mem.shape
    mw = w_xk.shape[1]

    def body(mem_ref, g_ref, wk_ref, wv_ref, hm_ref, k_ref, v_ref):
        hm = _rms_fwd(mem_ref[...], g_ref[...]).astype(BF16)
        hm_ref[...] = hm
        k_ref[...] = _dot(hm, wk_ref[...]).astype(BF16)
        v_ref[...] = _dot(hm, wv_ref[...]).astype(BF16)

    return pl.pallas_call(
        body, name="mem_fwd",
        out_shape=[_sds((n_mem, d), BF16), _sds((n_mem, mw), BF16), _sds((n_mem, mw), BF16)],
    )(mem, g_mem, w_xk, w_xv)


def _xattn_softmax(qa, k2):
    s = _dot_nt(qa, k2)
    m = jnp.max(s, axis=1, keepdims=True)
    e = jnp.exp(s - m)
    return e / jnp.sum(e, axis=1, keepdims=True)


def _xattn_fwd(x1, g_pre, w_xq, kx, vx, w_xo, g_post):
    s_len, d = x1.shape
    mw = w_xq.shape[1]
    n_mem = kx.shape[0]
    tm = ROW_TILE

    def body(x_ref, gp_ref, wq_ref, k_ref, v_ref, wo_ref, go_ref, x2_ref, y2_ref, h2_ref, q_ref, o_ref):
        x = x_ref[...]
        h = _rms_fwd(x, gp_ref[...]).astype(BF16)
        h2_ref[...] = h
        q = (_dot(h, wq_ref[...]) * QK_SCALE).astype(BF16)
        q_ref[...] = q
        in0 = _first_head_lanes()
        not0 = jnp.logical_not(in0)
        for pr in range(mw // PAIR):
            sl = slice(pr * PAIR, (pr + 1) * PAIR)
            q2, k2, v2 = q[:, sl], k_ref[:, sl], v_ref[:, sl]
            oa = [_dot(_xattn_softmax(_zero_other(mine, q2), k2).astype(BF16), v2) for mine in (in0, not0)]
            o_ref[:, sl] = _pick(in0, oa[0], oa[1]).astype(BF16)
        y = _dot(o_ref[...], wo_ref[...])
        y2_ref[...] = y
        x2_ref[...] = x + _rms_fwd(y, go_ref[...])

    return _call(
        body, name="xattn_fwd", grid=(s_len // tm,),
        in_specs=[_rows(tm, d), _resident((1, d)), _resident(w_xq.shape), _resident((n_mem, mw)), _resident((n_mem, mw)),
                  _resident(w_xo.shape), _resident((1, d))],
        out_specs=[_rows(tm, d), _rows(tm, d), _rows(tm, d), _rows(tm, mw), _rows(tm, mw)],
        out_shape=[_sds((s_len, d), F32), _sds((s_len, d), F32), _sds((s_len, d), BF16), _sds((s_len, mw), BF16),
                   _sds((s_len, mw), BF16)],
    )(x1, g_pre, w_xq, kx, vx, w_xo, g_post)


def _ffn_up(x2, g_pre, w_gate, w_up):
    s_len, d = x2.shape
    dff = w_gate.shape[1]
    tm = ROW_TILE

    def body(x_ref, g_ref, wg_ref, wu_ref, h_ref, a_ref, u_ref, z_ref):
        h = _rms_fwd(x_ref[...], g_ref[...]).astype(BF16)
        h_ref[...] = h
        a = _dot(h, wg_ref[...])
        u = _dot(h, wu_ref[...])
        a_ref[...] = a.astype(BF16)
        u_ref[...] = u.astype(BF16)
        z_ref[...] = ((a * jax.nn.sigmoid(a)) * u).astype(BF16)

    return _call(
        body, name="ffn_up", grid=(s_len // tm,),
        in_specs=[_rows(tm, d), _resident((1, d)), _resident(w_gate.shape), _resident(w_up.shape)],
        out_specs=[_rows(tm, d), _rows(tm, dff), _rows(tm, dff), _rows(tm, dff)],
        out_shape=[_sds((s_len, d), BF16)] + [_sds((s_len, dff), BF16)] * 3,
    )(x2, g_pre, w_gate, w_up)


def _ffn_down_loss(z, w_down, x2, g_post, target):
    s_len, d = x2.shape
    dff = z.shape[1]
    tm = ROW_TILE

    def body(z_ref, w_ref, x_ref, g_ref, t_ref, y_ref, dx_ref, sq_ref):
        @pl.when(pl.program_id(0) == 0)
        def _():
            sq_ref[...] = jnp.zeros_like(sq_ref)

        y = _dot(z_ref[...], w_ref[...])
        y_ref[...] = y
        err = (x_ref[...] + _rms_fwd(y, g_ref[...])) - t_ref[...]
        sq_ref[...] += jnp.sum(err * err, axis=0, keepdims=True)
        dx_ref[...] = err * (1.0 / d)

    return _call(
        body, name="ffn_down_loss", grid=(s_len // tm,),
        in_specs=[_rows(tm, dff), _resident(w_down.shape), _rows(tm, d), _resident((1, d)), _rows(tm, d)],
        out_specs=[_rows(tm, d), _rows(tm, d), _acc_out((1, d))],
        out_shape=[_sds((s_len, d), F32), _sds((s_len, d), F32), _sds((1, d), F32)],
    )(z, w_down, x2, g_post, target)


def _weight_grad(a, b, name):
    s_len, k = a.shape
    n = b.shape[1]
    ts = 512 if s_len % 512 == 0 else s_len
    tn = n
    while k * tn * 4 > 8 * 2 ** 20 and tn % 256 == 0:
        tn //= 2

    def body(a_ref, b_ref, o_ref):
        @pl.when(pl.program_id(1) == 0)
        def _():
            o_ref[...] = jnp.zeros_like(o_ref)

        o_ref[...] += _dot_tn(a_ref[...].astype(BF16), b_ref[...].astype(BF16))

    return pl.pallas_call(
        body, name=name, grid=(n // tn, s_len // ts),
        in_specs=[pl.BlockSpec((ts, k), lambda j, i: (i, 0)), pl.BlockSpec((ts, tn), lambda j, i: (i, j))],
        out_specs=pl.BlockSpec((k, tn), lambda j, i: (0, j)),
        out_shape=_sds((k, n), F32),
        compiler_params=pltpu.CompilerParams(dimension_semantics=("arbitrary",) * 2, vmem_limit_bytes=V7X_VMEM_LIMIT_BYTES),
    )(a, b)


def _ffn_bwd_a(dx3, y3, g_post, w_down_t, a, u):
    s_len, d = dx3.shape
    dff = a.shape[1]
    tm = ROW_TILE

    def body(dx_ref, y_ref, g_ref, w_ref, a_ref, u_ref, dy_ref, da_ref, du_ref, dg_ref):
        @pl.when(pl.program_id(0) == 0)
        def _():
            dg_ref[...] = jnp.zeros_like(dg_ref)

        dy, dg = _rms_bwd(y_ref[...], g_ref[...], dx_ref[...])
        dg_ref[...] += dg
        dyb = dy.astype(BF16)
        dy_ref[...] = dyb
        dz = _dot(dyb, w_ref[...])
        av = a_ref[...].astype(F32)
        uv = u_ref[...].astype(F32)
        sg = jax.nn.sigmoid(av)
        da_ref[...] = (dz * uv * (sg * (1.0 + av * (1.0 - sg)))).astype(BF16)
        du_ref[...] = (dz * (av * sg)).astype(BF16)

    return _call(
        body, name="ffn_bwd_a", grid=(s_len // tm,),
        in_specs=[_rows(tm, d), _rows(tm, d), _resident((1, d)), _resident(w_down_t.shape), _rows(tm, dff), _rows(tm, dff)],
        out_specs=[_rows(tm, d), _rows(tm, dff), _rows(tm, dff), _acc_out((1, d))],
        out_shape=[_sds((s_len, d), BF16), _sds((s_len, dff), BF16), _sds((s_len, dff), BF16), _sds((1, d), F32)],
    )(dx3, y3, g_post, w_down_t, a, u)


def _ffn_bwd_b(da, du, w_gate_t, w_up_t, dx3, x2, g_pre):
    s_len, d = x2.shape
    dff = da.shape[1]
    tm = ROW_TILE

    def body(da_ref, du_ref, wg_ref, wu_ref, dx_ref, x_ref, g_ref, o_ref, dg_ref):
        @pl.when(pl.program_id(0) == 0)
        def _():
            dg_ref[...] = jnp.zeros_like(dg_ref)

        dh = _dot(da_ref[...], wg_ref[...]) + _dot(du_ref[...], wu_ref[...])
        dx, dg = _rms_bwd(x_ref[...], g_ref[...], dh)
        dg_ref[...] += dg
        o_ref[...] = dx_ref[...] + dx

    return _call(
        body, name="ffn_bwd_b", grid=(s_len // tm,),
        in_specs=[_rows(tm, dff), _rows(tm, dff), _resident(w_gate_t.shape), _resident(w_up_t.shape), _rows(tm, d),
                  _rows(tm, d), _resident((1, d))],
        out_specs=[_rows(tm, d), _acc_out((1, d))],
        out_shape=[_sds((s_len, d), F32), _sds((1, d), F32)],
    )(da, du, w_gate_t, w_up_t, dx3, x2, g_pre)


def _xattn_bwd(dx2, y2, g_post, w_xo_t, q, kx, vx, w_xq_t, x1, g_pre):
    s_len, d = x1.shape
    mw = q.shape[1]
    n_mem = kx.shape[0]
    tm = ROW_TILE

    def body(dx_ref, y_ref, go_ref, wo_ref, q_ref, k_ref, v_ref, wq_ref, x_ref, gp_ref,
             dx1_ref, dy_ref, dq_ref, dk_ref, dv_ref, dgo_ref, dgp_ref):
        @pl.when(pl.program_id(0) == 0)
        def _():
            dk_ref[...] = jnp.zeros_like(dk_ref)
            dv_ref[...] = jnp.zeros_like(dv_ref)
            dgo_ref[...] = jnp.zeros_like(dgo_ref)
            dgp_ref[...] = jnp.zeros_like(dgp_ref)

        dxin = dx_ref[...]
        dy, dgo = _rms_bwd(y_ref[...], go_ref[...], dxin)
        dgo_ref[...] += dgo
        dyb = dy.astype(BF16)
        dy_ref[...] = dyb
        do = _dot(dyb, wo_ref[...]).astype(BF16)
        in0 = _first_head_lanes()
        not0 = jnp.logical_not(in0)
        for pr in range(mw // PAIR):
            sl = slice(pr * PAIR, (pr + 1) * PAIR)
            q2, k2, v2, do2 = q_ref[:, sl], k_ref[:, sl], v_ref[:, sl], do[:, sl]
            dqs = []
            dk2 = jnp.zeros((n_mem, PAIR), F32)
            dv2 = jnp.zeros((n_mem, PAIR), F32)
            for mine in (in0, not0):
                qa = _zero_other(mine, q2)
                doa = _zero_other(mine, do2)
                p = _xattn_softmax(qa, k2)
                dp = _dot_nt(doa, v2)
                ds = (p * (dp - jnp.sum(p * dp, axis=1, keepdims=True))).astype(BF16)
                dqs.append(_dot(ds, k2))
                dk2 = dk2 + _dot_tn(ds, qa)
                dv2 = dv2 + _dot_tn(p.astype(BF16), doa)
            dq_ref[:, sl] = (_pick(in0, dqs[0], dqs[1]) * QK_SCALE).astype(BF16)
            dk_ref[:, sl] += dk2
            dv_ref[:, sl] += dv2
        dh = _dot(dq_ref[...], wq_ref[...])
        dx, dgp = _rms_bwd(x_ref[...], gp_ref[...], dh)
        dgp_ref[...] += dgp
        dx1_ref[...] = dxin + dx

    return _call(
        body, name="xattn_bwd", grid=(s_len // tm,),
        in_specs=[_rows(tm, d), _rows(tm, d), _resident((1, d)), _resident(w_xo_t.shape), _rows(tm, mw),
                  _resident((n_mem, mw)), _resident((n_mem, mw)), _resident(w_xq_t.shape), _rows(tm, d), _resident((1, d))],
        out_specs=[_rows(tm, d), _rows(tm, d), _rows(tm, mw), _acc_out((n_mem, mw)), _acc_out((n_mem, mw)),
                   _acc_out((1, d)), _acc_out((1, d))],
        out_shape=[_sds((s_len, d), F32), _sds((s_len, d), BF16), _sds((s_len, mw), BF16), _sds((n_mem, mw), F32),
                   _sds((n_mem, mw), F32), _sds((1, d), F32), _sds((1, d), F32)],
    )(dx2, y2, g_post, w_xo_t, q, kx, vx, w_xq_t, x1, g_pre)


def _mem_bwd(dk, dv, w_xk_t, w_xv_t, hm, mem, g_mem):
    n_mem, d = mem.shape
    mw = dk.shape[1]

    def body(dk_ref, dv_ref, wk_ref, wv_ref, hm_ref, mem_ref, g_ref, dwk_ref, dwv_ref, dg_ref):
        dkb = dk_ref[...].astype(BF16)
        dvb = dv_ref[...].astype(BF16)
        dhm = _dot(dkb, wk_ref[...]) + _dot(dvb, wv_ref[...])
        _, dg = _rms_bwd(mem_ref[...], g_ref[...], dhm)
        dg_ref[...] = dg
        dwk_ref[...] = _dot_tn(hm_ref[...], dkb)
        dwv_ref[...] = _dot_tn(hm_ref[...], dvb)

    return pl.pallas_call(
        body, name="mem_bwd",
        out_shape=[_sds((d, mw), F32), _sds((d, mw), F32), _sds((1, d), F32)],
    )(dk, dv, w_xk_t, w_xv_t, hm, mem, g_mem)


def _mix_out_bwd(dx1, y1, g_post, w_out_t, o_fox, o_dil):
    s_len, d = dx1.shape
    hw = HEAD_WIDTH
    tm = ROW_TILE
    head_of = np.arange(hw) // HEAD_DIM
    ones = jnp.asarray((head_of[:, None] == head_of[None, :]).astype(np.float32))

    def body(dx_ref, y_ref, g_ref, w_ref, of_ref, od_ref, ones_ref, dy_ref, dof_ref, dlf_ref, dg_ref,
             dod1, dod2, dod3, dld1, dld2, dld3, buf):
        @pl.when(pl.program_id(0) == 0)
        def _():
            dg_ref[...] = jnp.zeros_like(dg_ref)

        dy, dg = _rms_bwd(y_ref[...], g_ref[...], dx_ref[...])
        dg_ref[...] += dg
        dyb = dy.astype(BF16)
        dy_ref[...] = dyb
        do = _dot(dyb, w_ref[...])
        dof_ref[...] = do[:, 0:hw].astype(BF16)
        dlf_ref[...] = jnp.dot(do[:, 0:hw] * of_ref[...], ones_ref[...], precision=HIGHEST, preferred_element_type=F32)
        do_dil = do[:, hw:2 * hw]
        dl_dil = jnp.dot(do_dil * od_ref[...], ones_ref[...], precision=HIGHEST, preferred_element_type=F32)
        for val, refs in ((do_dil, (dod1, dod2, dod3)), (dl_dil, (dld1, dld2, dld3))):
            _store_lane_blocks(buf, val)
            for ref, dil in zip(refs, DILATIONS):
                if dil == 1:
                    ref[...] = val.astype(ref.dtype)
                else:
                    _write_class_major(buf, ref, dil)

    half = _rows(tm, hw)
    by_dil = [_class_rows(tm, hw, dil) for dil in DILATIONS]
    outs = _call(
        body, name="mix_out_bwd", grid=(s_len // tm,),
        in_specs=[_rows(tm, d), _rows(tm, d), _resident((1, d)), _resident(w_out_t.shape), half, half, _resident((hw, hw))],
        out_specs=[_rows(tm, d), half, half, _acc_out((1, d))] + by_dil + by_dil,
        out_shape=[_sds((s_len, d), BF16), _sds((s_len, hw), BF16), _sds((s_len, hw), F32), _sds((1, d), F32)]
        + [_sds((s_len // dil, dil * hw), BF16) for dil in DILATIONS]
        + [_sds((s_len // dil, dil * hw), F32) for dil in DILATIONS],
        scratch=[pltpu.VMEM((hw // PAIR, tm, PAIR), F32)],
    )(dx1, y1, g_post, w_out_t, o_fox, o_dil, ones)
    return outs[0], outs[1], outs[2], outs[3], outs[4:7], outs[7:10]


def _fox_bwd_prep(aq, lse, delta):
    s_len, hw = aq.shape
    tm = ROW_TILE

    def body(aq_ref, lse_ref, dl_ref, aql_ref, ad_ref):
        jj = _lane_in_head(hw)
        l3 = _split3(_swap_halves(lse_ref[...]))
        aql_ref[...] = _place3(jj, 6, [-p for p in l3], aq_ref[...])
        d3 = _split3(_swap_halves(dl_ref[...]))
        ad_ref[...] = _place3(jj, 0, [-p for p in d3], jnp.zeros((tm, hw), BF16))

    half = _rows(tm, hw)
    return _call(
        body, name="fox_bwd_prep", grid=(s_len // tm,),
        in_specs=[half, half, half], out_specs=[half, half],
        out_shape=[_sds((s_len, hw), BF16), _sds((s_len, hw), BF16)],
    )(aq, lse, delta)


def _ones_on_first3(shape):
    jj = lax.broadcasted_iota(jnp.int32, shape, 1) % HEAD_DIM
    return jnp.where(jj < 3, 1.0, 0.0).astype(BF16)


def _fox_bwd(fqkv, do, aql, ak, ad, stats):
    s_len = fqkv.shape[0]
    bq = bk = min(ATT_BLOCK, s_len)
    nq, nk = s_len // bq, s_len // bk

    def body(st_ref, q_ref, k_ref, v_ref, do_ref, aql_ref, ak_ref, ad_ref, dq_ref, rs_ref, dk_ref, dv_ref, dc_ref,
             ka_ref, va_ref, kone_ref, r_ref, dvacc_ref, dqacc_ref):
        pair, kj, qi = pl.program_id(0), pl.program_id(1), pl.program_id(2)
        in0 = _first_head_lanes()
        not0 = jnp.logical_not(in0)
        heads = ((0, in0), (1, not0))
        rows = pl.ds(pl.multiple_of(qi * bq, bq), bq)

        @pl.when(qi == 0)
        def _():
            k2, v2, a2 = k_ref[...], v_ref[...], ak_ref[...]
            one = jnp.ones_like(k2)
            one3 = _ones_on_first3(v2.shape)
            for a, mine in heads:
                ka_ref[a] = jnp.where(mine, k2, a2)
                va_ref[a] = jnp.where(mine, v2, one3)
                kone_ref[a] = jnp.where(mine, k2, one)
            r_ref[...] = jnp.zeros_like(r_ref)
            dvacc_ref[...] = jnp.zeros_like(dvacc_ref)

        @pl.when(kj == 0)
        def _():
            for a, _ in heads:
                dqacc_ref[a, rows, :] = jnp.zeros((bq, PAIR), F32)

        def step(masked):
            q2, do2, a2, d2 = q_ref[...], do_ref[...], aql_ref[...], ad_ref[...]
            one = jnp.ones_like(q2)
            for a, mine in heads:
                doa = jnp.where(mine, do2, d2)
                s = _dot_nt(jnp.where(mine, q2, a2), ka_ref[a])
                if masked:
                    s = jnp.where(_lower_triangle(bq), s, NEG)
                p = jnp.exp(s)
                ds = (p * _dot_nt(doa, va_ref[a])).astype(BF16)
                dvacc_ref[a] += _dot_tn(p.astype(BF16), doa)
                r_ref[a] += _dot_tn(ds, jnp.where(mine, q2, one))
                dqacc_ref[a, rows, :] += _dot(ds, kone_ref[a])

        @pl.when(jnp.logical_and(qi > kj, _tile_needed(st_ref, pair, qi, kj)))
        def _():
            step(False)

        @pl.when(qi == kj)
        def _():
            step(True)
            acc0, acc1 = dqacc_ref[0, rows, :], dqacc_ref[1, rows, :]
            dq_ref[...] = (_pick(in0, acc0, acc1) * QK_SCALE).astype(BF16)
            rs_ref[...] = pltpu.roll(_pick(in0, acc1, acc0), HEAD_DIM, 1)

        @pl.when(qi == nq - 1)
        def _():
            dk_ref[...] = _pick(in0, r_ref[0], r_ref[1]).astype(BF16)
            dv_ref[...] = _pick(in0, dvacc_ref[0], dvacc_ref[1]).astype(BF16)
            dc_ref[...] = -pltpu.roll(_pick(in0, r_ref[1], r_ref[0]), HEAD_DIM, 1)

    blk = lambda n: (n, PAIR)
    kvmap = lambda p, j, i, st: (j, p)

    def qmap(p, j, i, st):
        last = jnp.clip(st[FOX_LAST_QUERY_ROW + p, j].astype(jnp.int32), j, nq - 1)
        return (jnp.minimum(jnp.maximum(i, j), last), p)

    return pl.pallas_call(
        body, name="fox_bwd",
        grid_spec=pltpu.PrefetchScalarGridSpec(
            num_scalar_prefetch=1, grid=(N_PAIRS, nk, nq),
            in_specs=[
                pl.BlockSpec(blk(bq), qmap),
                pl.BlockSpec(blk(bk), lambda p, j, i, st: (j, N_PAIRS + p)),
                pl.BlockSpec(blk(bk), lambda p, j, i, st: (j, 2 * N_PAIRS + p)),
                pl.BlockSpec(blk(bq), qmap), pl.BlockSpec(blk(bq), qmap),
                pl.BlockSpec(blk(bk), kvmap),
                pl.BlockSpec(blk(bq), qmap),
            ],
            out_specs=[pl.BlockSpec(blk(bk), kvmap)] * 5,
            scratch_shapes=[pltpu.VMEM((2, bk, PAIR), BF16), pltpu.VMEM((2, bk, PAIR), BF16), pltpu.VMEM((2, bk, PAIR), BF16),
                            pltpu.VMEM((2, bk, PAIR), F32), pltpu.VMEM((2, bk, PAIR), F32), pltpu.VMEM((2, s_len, PAIR), F32)]),
        out_shape=[_sds((s_len, HEAD_WIDTH), BF16), _sds((s_len, HEAD_WIDTH), F32), _sds((s_len, HEAD_WIDTH), BF16),
                   _sds((s_len, HEAD_WIDTH), BF16), _sds((s_len, HEAD_WIDTH), F32)],
        compiler_params=pltpu.CompilerParams(dimension_semantics=("arbitrary",) * 3, vmem_limit_bytes=V7X_VMEM_LIMIT_BYTES),
    )(stats, fqkv, fqkv, fqkv, do, aql, ak, ad)


def _gate_bwd(rs, dc, gx, b_exp):
    s_len, hw = gx.shape
    t = min(SCAN_TILE, s_len)
    nt = s_len // t
    tri = jnp.asarray(np.triu(np.ones((t, t), np.float32)))

    def body(rs_ref, dc_ref, gx_ref, b_ref, tri_ref, dgx_ref, db_ref, carry):
        @pl.when(pl.program_id(0) == 0)
        def _():
            carry[...] = jnp.zeros_like(carry)
            db_ref[...] = jnp.zeros_like(db_ref)

        dlf = jnp.dot(tri_ref[...], rs_ref[...] + dc_ref[...], precision=HIGHEST, preferred_element_type=F32) + carry[...]
        carry[...] = dlf[0:1, :]
        dgate = dlf * jax.nn.sigmoid(-(gx_ref[...] + b_ref[...]))
        db_ref[...] += jnp.sum(dgate, axis=0, keepdims=True)
        lane = lax.broadcasted_iota(jnp.int32, (1, hw), 1)
        dgx_ref[...] = jnp.where(lane % HEAD_DIM == 0, dgate, 0.0).astype(BF16)

    rev = lambda i: (nt - 1 - i, 0)
    return _call(
        body, name="gate_bwd", grid=(nt,),
        in_specs=[pl.BlockSpec((t, hw), rev)] * 3 + [_resident((1, hw)), _resident((t, t))],
        out_specs=[pl.BlockSpec((t, hw), rev), _acc_out((1, hw))],
        out_shape=[_sds((s_len, hw), BF16), _sds((1, hw), F32)],
        scratch=[pltpu.VMEM((1, hw), F32)],
    )(rs, dc, gx, b_exp, tri)


def _dil_bwd(qkv_v, do_v, lj_v, dl_v, bias, branch):
    dil = DILATIONS[branch]
    w = DIL_BLOCK
    hw = HEAD_WIDTH
    length = qkv_v.shape[0]
    nb = length // w

    def body(q0_ref, q1_ref, kp_ref, kc_ref, vp_ref, vc_ref, do0_ref, do1_ref, l0_ref, l1_ref, d0_ref, d1_ref, b_ref,
             dq_ref, dk_ref, dv_ref, dsum_ref):
        r, n = pl.program_id(0), pl.program_id(1)
        in0 = _first_head_lanes()
        not0 = jnp.logical_not(in0)
        first = n == 0
        last = n == nb - 1

        @pl.when(jnp.logical_and(r == 0, n == 0))
        def _():
            dsum_ref[...] = jnp.zeros_like(dsum_ref)

        pairs = [slice(pr * PAIR, (pr + 1) * PAIR) for pr in range(N_PAIRS)]

        def both_heads(ref, sl):
            v = ref[:, sl]
            return jnp.concatenate([_zero_other(in0, v), _zero_other(not0, v)], axis=0)

        def head_columns(ref):
            return jnp.concatenate([ref[:, h * HEAD_DIM:h * HEAD_DIM + 1] for h in range(N_HEADS)], axis=0)

        qq0 = [both_heads(q0_ref, sl) for sl in pairs]
        qq1 = [both_heads(q1_ref, sl) for sl in pairs]
        dd0 = [both_heads(do0_ref, sl) for sl in pairs]
        dd1 = [both_heads(do1_ref, sl) for sl in pairs]
        stack = lambda tiles: jnp.concatenate(tiles, axis=0)
        s_a = stack([_dot_nt(qq0[i], kp_ref[:, sl]) for i, sl in enumerate(pairs)])
        s_b = stack([_dot_nt(qq0[i], kc_ref[:, sl]) for i, sl in enumerate(pairs)])
        s_c = stack([_dot_nt(qq1[i], kc_ref[:, sl]) for i, sl in enumerate(pairs)])
        dp_a = stack([_dot_nt(dd0[i], vp_ref[:, sl]) for i, sl in enumerate(pairs)])
        dp_b = stack([_dot_nt(dd0[i], vc_ref[:, sl]) for i, sl in enumerate(pairs)])
        dp_c = stack([_dot_nt(dd1[i], vc_ref[:, sl]) for i, sl in enumerate(pairs)])
        bias2 = b_ref[...].reshape(N_HEADS * w, 2 * w)
        b_prev, b_cur = bias2[:, 0:w], bias2[:, w:2 * w]
        lse0, lse1 = head_columns(l0_ref), head_columns(l1_ref)
        dl0, dl1 = head_columns(d0_ref), head_columns(d1_ref)
        p_a = jnp.exp(jnp.where(first, NEG, s_a + b_prev) - lse0)
        p_b = jnp.exp((s_b + b_cur) - lse0)
        p_c = jnp.exp(jnp.where(last, NEG, s_c + b_prev) - lse1)
        ds_a = p_a * (dp_a - dl0)
        ds_b = p_b * (dp_b - dl0)
        ds_c = p_c * (dp_c - dl1)
        dsum_ref[...] += jnp.concatenate([ds_a, ds_b], axis=1).reshape(N_HEADS, w, 2 * w)
        ds_a, ds_b, ds_c = ds_a.astype(BF16), ds_b.astype(BF16), ds_c.astype(BF16)
        p_b, p_c = p_b.astype(BF16), p_c.astype(BF16)
        for i, sl in enumerate(pairs):
            rows = slice(2 * i * w, (2 * i + 2) * w)
            dq2 = _dot(ds_a[rows], kp_ref[:, sl]) + _dot(ds_b[rows], kc_ref[:, sl])
            dq_ref[:, sl] = _pick(in0, dq2[:w], dq2[w:])
            dk_ref[:, sl] = _dot_tn(ds_b[rows], qq0[i]) + _dot_tn(ds_c[rows], qq1[i])
            dv_ref[:, sl] = _dot_tn(p_b[rows], dd0[i]) + _dot_tn(p_c[rows], dd1[i])

    prev = lambda n: jnp.maximum(n - 1, 0)
    nxt = lambda n: jnp.minimum(n + 1, nb - 1)
    blk = (w, hw)
    outs = pl.pallas_call(
        body, name=f"dil_bwd_{dil}", grid=(dil, nb),
        in_specs=[
            pl.BlockSpec(blk, lambda r, n: (n, 3 * r)),
            pl.BlockSpec(blk, lambda r, n: (nxt(n), 3 * r)),
            pl.BlockSpec(blk, lambda r, n: (prev(n), 3 * r + 1)),
            pl.BlockSpec(blk, lambda r, n: (n, 3 * r + 1)),
            pl.BlockSpec(blk, lambda r, n: (prev(n), 3 * r + 2)),
            pl.BlockSpec(blk, lambda r, n: (n, 3 * r + 2)),
            pl.BlockSpec(blk, lambda r, n: (n, r)),
            pl.BlockSpec(blk, lambda r, n: (nxt(n), r)),
            pl.BlockSpec(blk, lambda r, n: (n, r)),
            pl.BlockSpec(blk, lambda r, n: (nxt(n), r)),
            pl.BlockSpec(blk, lambda r, n: (n, r)),
            pl.BlockSpec(blk, lambda r, n: (nxt(n), r)),
            pl.BlockSpec((None, N_HEADS, w, 2 * w), lambda r, n: (branch, 0, 0, 0)),
        ],
        out_specs=[pl.BlockSpec(blk, lambda r, n: (n, r))] * 3 + [pl.BlockSpec((N_HEADS, w, 2 * w), lambda r, n: (0, 0, 0))],
        out_shape=[_sds((length, dil * hw), F32)] * 3 + [_sds((N_HEADS, w, 2 * w), F32)],
        compiler_params=pltpu.CompilerParams(dimension_semantics=("arbitrary",) * 2, vmem_limit_bytes=V7X_VMEM_LIMIT_BYTES),
    )(qkv_v, qkv_v, qkv_v, qkv_v, qkv_v, qkv_v, do_v, do_v, lj_v, lj_v, dl_v, dl_v, bias)
    return list(outs)


def _rel_bias_grad(dsum, buckets):
    w = DIL_BLOCK

    def body(ds_ref, bk_ref, o_ref):
        row = lax.broadcasted_iota(jnp.int32, (N_BUCKETS, PAIR), 0)
        lane = lax.broadcasted_iota(jnp.int32, (N_BUCKETS, PAIR), 1)

        def per_bucket(b, acc):
            for p in range(len(DILATIONS)):
                hit = bk_ref[p] == b
                for h in range(N_HEADS):
                    part = jnp.where(hit, ds_ref[p, h], 0.0)
                    tot = jnp.sum(jnp.sum(part, axis=1, keepdims=True), axis=0, keepdims=True)
                    acc = acc + jnp.where(jnp.logical_and(row == b, lane == h), tot, 0.0)
            return acc

        o_ref[...] = lax.fori_loop(0, N_BUCKETS, per_bucket, jnp.zeros((N_BUCKETS, PAIR), F32))

    return pl.pallas_call(body, name="rel_bias_grad", out_shape=_sds((N_BUCKETS, PAIR), F32))(dsum, buckets)


def _in_proj_bwd(dfq, dfk, dfv, dgx, ddq, ddk, ddv, w_in_t, dx1, x, g_pre):
    s_len, d = x.shape
    hw = HEAD_WIDTH
    tm = ROW_TILE

    def body(fq, fk, fv, gx, q1, q2, q3, k1, k2, k3, v1, v2, v3, w_ref, dx_ref, x_ref, g_ref, o_ref, dp_ref, dg_ref, buf):
        @pl.when(pl.program_id(0) == 0)
        def _():
            dg_ref[...] = jnp.zeros_like(dg_ref)

        def branch_sum(refs):
            a, b, c = [r[...] if dil == 1 else _read_class_major(r, buf, dil) for r, dil in zip(refs, DILATIONS)]
            return (a + b) + c

        dp_ref[:, 0:hw] = fq[...]
        dp_ref[:, hw:2 * hw] = fk[...]
        dp_ref[:, 2 * hw:3 * hw] = fv[...]
        dp_ref[:, 3 * hw:4 * hw] = gx[...]
        dp_ref[:, 4 * hw:5 * hw] = (branch_sum((q1, q2, q3)) * QK_SCALE).astype(BF16)
        dp_ref[:, 5 * hw:6 * hw] = branch_sum((k1, k2, k3)).astype(BF16)
        dp_ref[:, 6 * hw:7 * hw] = branch_sum((v1, v2, v3)).astype(BF16)
        dh = _dot(dp_ref[...], w_ref[...])
        dx, dg = _rms_bwd(x_ref[...], g_ref[...], dh)
        dg_ref[...] += dg
        o_ref[...] = dx_ref[...] + dx

    half = _rows(tm, hw)
    by_dil = [_class_rows(tm, hw, dil) for dil in DILATIONS]
    return _call(
        body, name="in_proj_bwd", grid=(s_len // tm,),
        in_specs=[half] * 4 + by_dil * 3 + [_resident(w_in_t.shape), _rows(tm, d), _rows(tm, d), _resident((1, d))],
        out_specs=[_rows(tm, d), _rows(tm, 7 * hw), _acc_out((1, d))],
        out_shape=[_sds((s_len, d), F32), _sds((s_len, 7 * hw), BF16), _sds((1, d), F32)],
        scratch=[pltpu.VMEM((hw // PAIR, tm, PAIR), F32)],
    )(dfq, dfk, dfv, dgx, *ddq, *ddk, *ddv, w_in_t, dx1, x, g_pre)


def _expand_w_in(w_in):
    hw = HEAD_WIDTH
    gate = jnp.repeat(w_in[:, 3 * hw:3 * hw + N_HEADS], HEAD_DIM, axis=1)
    return jnp.concatenate([w_in[:, :3 * hw], gate, w_in[:, 3 * hw + N_HEADS:]], axis=1)


def _local_step(x, mem, target, g, w_bf):
    hw = HEAD_WIDTH
    w_in_e = _expand_w_in(w_bf["w_in"])
    b_exp = jnp.repeat(g["b_f"], HEAD_DIM, axis=1)
    buckets = jnp.asarray(_dil_buckets())

    n_dil = len(DILATIONS)
    h1, fqkv, gx, *dqkv = _in_proj(x, g["g_mix_pre"], w_in_e)
    aq, ak, c = _gate_scan(gx, b_exp)
    stats = _fox_block_stats(fqkv, c)
    o_fox, lse_fox = _fox_fwd(fqkv, aq, ak, stats)
    bias = _dil_bias(g["rel_bias"], buckets)
    branches = [_dil_fwd(dqkv[p], bias, p) for p in range(n_dil)]
    x1, y1, o_dil, *lj = _mix_out(o_fox, [b[0] for b in branches], [b[1] for b in branches], w_bf["w_out"], x, g["g_mix_post"])
    hm, kx, vx = _mem_fwd(mem, g["g_mem"], w_bf["w_xk"], w_bf["w_xv"])
    x2, y2, h2, qx, ox = _xattn_fwd(x1, g["g_xattn_pre"], w_bf["w_xq"], kx, vx, w_bf["w_xo"], g["g_xattn_post"])
    h3, a, u, z = _ffn_up(x2, g["g_ffn_pre"], w_bf["w_gate"], w_bf["w_up"])
    y3, dx3, sq = _ffn_down_loss(z, w_bf["w_down"], x2, g["g_ffn_post"], target)

    grads = {}
    dy3, da, du, grads["g_ffn_post"] = _ffn_bwd_a(dx3, y3, g["g_ffn_post"], w_bf["w_down"].T, a, u)
    dx2, grads["g_ffn_pre"] = _ffn_bwd_b(da, du, w_bf["w_gate"].T, w_bf["w_up"].T, dx3, x2, g["g_ffn_pre"])
    grads["w_down"] = _weight_grad(z, dy3, "dw_down")
    grads["w_gate"] = _weight_grad(h3, da, "dw_gate")
    grads["w_up"] = _weight_grad(h3, du, "dw_up")
    dx1, dy2, dqx, dkx, dvx, grads["g_xattn_post"], grads["g_xattn_pre"] = _xattn_bwd(
        dx2, y2, g["g_xattn_post"], w_bf["w_xo"].T, qx, kx, vx, w_bf["w_xq"].T, x1, g["g_xattn_pre"])
    grads["w_xo"] = _weight_grad(ox, dy2, "dw_xo")
    grads["w_xq"] = _weight_grad(h2, dqx, "dw_xq")
    grads["w_xk"], grads["w_xv"], grads["g_mem"] = _mem_bwd(dkx, dvx, w_bf["w_xk"].T, w_bf["w_xv"].T, hm, mem, g["g_mem"])
    dy1, do_fox, delta_fox, grads["g_mix_post"], do_dil, delta_dil = _mix_out_bwd(
        dx1, y1, g["g_mix_post"], w_bf["w_out"].T, o_fox, o_dil)
    grads["w_out"] = jnp.concatenate([_weight_grad(o_fox, dy1, "dw_out_fox"), _weight_grad(o_dil, dy1, "dw_out_dil")], axis=0)
    aql, ad = _fox_bwd_prep(aq, lse_fox, delta_fox)
    dfq, rs, dfk, dfv, dc = _fox_bwd(fqkv, do_fox, aql, ak, ad, stats)
    dgx, db = _gate_bwd(rs, dc, gx, b_exp)
    grads["b_f"] = db[:, ::HEAD_DIM]
    dil = [_dil_bwd(dqkv[p], do_dil[p], lj[p], delta_dil[p], bias, p) for p in range(n_dil)]
    grads["rel_bias"] = _rel_bias_grad(jnp.stack([t[3] for t in dil]), buckets)[:, :N_HEADS]
    grad_x, dproj, grads["g_mix_pre"] = _in_proj_bwd(
        dfq, dfk, dfv, dgx, [t[0] for t in dil], [t[1] for t in dil], [t[2] for t in dil], w_in_e.T, dx1, x, g["g_mix_pre"])
    dw_in_e = _weight_grad(h1, dproj, "dw_in")
    grads["w_in"] = jnp.concatenate(
        [dw_in_e[:, :3 * hw], dw_in_e[:, 3 * hw:4 * hw:HEAD_DIM], dw_in_e[:, 4 * hw:]], axis=1)
    return sq, grad_x, grads


MESH = pl.DeviceIdType.MESH


def _my_place():
    return lax.axis_index("x"), lax.axis_index("y"), lax.axis_index("c")


def _all_gather(x, name):
    rows, lanes = x.shape

    def body(x_ref, out_ref, send_sems, recv_sems, local_sem):
        mx, my, mc = _my_place()
        me, sibling = (mx, my, mc), (mx, my, 1 - mc)
        chips = [(1 - mx, my), (mx, 1 - my), (1 - mx, 1 - my)]

        def slot(px, py, pc):
            return out_ref.at[4 * px + 2 * py + pc]

        def copy(k, block, to, src=None):
            return pltpu.make_async_remote_copy(
                src_ref=slot(*block) if src is None else src, dst_ref=slot(*block),
                send_sem=send_sems.at[k], recv_sem=recv_sems.at[k], device_id=to, device_id_type=MESH)

        mine = pltpu.make_async_copy(x_ref, slot(*me), local_sem)
        mine.start()
        first = [copy(0, me, sibling, src=x_ref)]
        first += [copy(1 + j, me, (*chip, mc), src=x_ref) for j, chip in enumerate(chips)]
        for cp in first:
            cp.start()
        passed = [copy(4 + j, (*chip, mc), sibling) for j, chip in enumerate(chips)]
        for j, chip in enumerate(chips):
            copy(1 + j, (*chip, mc), me).wait_recv()
            passed[j].start()
        copy(0, sibling, me).wait_recv()
        for j, chip in enumerate(chips):
            copy(4 + j, (*chip, 1 - mc), me).wait_recv()
        for cp in first + passed:
            cp.wait_send()
        mine.wait()

    return pl.pallas_call(
        body, name=name, out_shape=_sds((N_DEV, rows, lanes), x.dtype),
        in_specs=[pl.BlockSpec(memory_space=pl.ANY)], out_specs=pl.BlockSpec(memory_space=pl.ANY),
        scratch_shapes=[pltpu.SemaphoreType.DMA((N_DEV - 1,)), pltpu.SemaphoreType.DMA((N_DEV - 1,)), pltpu.SemaphoreType.DMA],
    )(x)


def _exchange(g, name):
    _, rows, lanes = g.shape

    def body(g_ref, land_ref, send_sems, recv_sems, local_sem):
        mx, my, mc = _my_place()
        me = 4 * mx + 2 * my + mc
        mine = pltpu.make_async_copy(g_ref.at[me], land_ref.at[me], local_sem)
        mine.start()
        sent, arriving = [], []
        for k in (1, 2, 3, 4, 5, 6, 7):
            px = 1 - mx if k & 4 else mx
            py = 1 - my if k & 2 else my
            pc = 1 - mc if k & 1 else mc
            peer = 4 * px + 2 * py + pc
            cp = pltpu.make_async_remote_copy(
                src_ref=g_ref.at[peer], dst_ref=land_ref.at[me], send_sem=send_sems.at[k - 1], recv_sem=recv_sems.at[k - 1],
                device_id=(px, py, pc), device_id_type=MESH)
            cp.start()
            sent.append(cp)
            arriving.append(pltpu.make_async_remote_copy(
                src_ref=g_ref.at[me], dst_ref=land_ref.at[peer], send_sem=send_sems.at[k - 1], recv_sem=recv_sems.at[k - 1],
                device_id=(px, py, pc), device_id_type=MESH))
        for cp in arriving:
            cp.wait_recv()
        for cp in sent:
            cp.wait_send()
        mine.wait()

    return pl.pallas_call(
        body, name=name, out_shape=_sds(g.shape, g.dtype),
        in_specs=[pl.BlockSpec(memory_space=pl.ANY)], out_specs=pl.BlockSpec(memory_space=pl.ANY),
        scratch_shapes=[pltpu.SemaphoreType.DMA((N_DEV - 1,)), pltpu.SemaphoreType.DMA((N_DEV - 1,)), pltpu.SemaphoreType.DMA],
    )(g)


def _sum_slots(parts, name):
    n, rows, lanes = parts.shape
    tr = 512 if rows % 512 == 0 else rows

    def body(p_ref, o_ref):
        acc = p_ref[0].astype(F32)
        for j in range(1, n):
            acc = acc + p_ref[j].astype(F32)
        o_ref[...] = acc

    return _call(
        body, name=name, grid=(rows // tr,),
        in_specs=[pl.BlockSpec((n, tr, lanes), lambda i: (0, i, 0))], out_specs=_rows(tr, lanes),
        out_shape=_sds((rows, lanes), F32),
    )(parts)


def _adamw(w, g, m, v, name):
    def body(w_ref, g_ref, m_ref, v_ref, d_ref, nm_ref, nv_ref):
        gv = g_ref[...]
        m_new = ADAM_B1 * m_ref[...] + (1.0 - ADAM_B1) * gv
        v_new = ADAM_B2 * v_ref[...] + (1.0 - ADAM_B2) * (gv * gv)
        nm_ref[...] = m_new
        nv_ref[...] = v_new
        m_hat = m_new / (1.0 - ADAM_B1 ** ADAM_STEP)
        v_hat = v_new / (1.0 - ADAM_B2 ** ADAM_STEP)
        d_ref[...] = -ADAM_LR * (m_hat / (jnp.sqrt(v_hat) + ADAM_EPS) + ADAM_WD * w_ref[...])

    out = _sds(w.shape, F32)
    return pl.pallas_call(
        body, name=name, out_shape=[out, out, out],
        compiler_params=pltpu.CompilerParams(vmem_limit_bytes=V7X_VMEM_LIMIT_BYTES),
    )(w, g, m, v)


def _loss_head(sq, d_model):
    def body(sq_ref, o_ref):
        tot = jnp.sum(jnp.sum(sq_ref[...], axis=1, keepdims=True), axis=0, keepdims=True)
        o_ref[...] = 0.5 * (tot / d_model)

    return pl.pallas_call(body, name="loss_head", out_shape=_sds((1, 1), F32))(sq)


_BIG = (("w_in", 1), ("w_out", 0), ("w_xq", 0), ("w_xk", 0), ("w_xv", 0), ("w_xo", 1), ("w_gate", 1), ("w_up", 1), ("w_down", 0))
_SMALL = ("g_mix_pre", "b_f", "rel_bias", "g_mix_post", "g_xattn_pre", "g_mem", "g_xattn_post", "g_ffn_pre", "g_ffn_post")
LANES = 128
BIG_ROW_ALIGN = 512


def _round_up(n, k):
    return -(-n // k) * k


def _pack_rows(flat_parts, row_align, dtype):
    starts, rows, padded = [], 0, []
    for p in flat_parts:
        n = _round_up(p.shape[0], LANES)
        starts.append(rows)
        rows += n // LANES
        padded.append(jnp.pad(p.astype(dtype), (0, n - p.shape[0])))
    total = _round_up(rows, row_align)
    padded.append(jnp.zeros(((total - rows) * LANES,), dtype))
    return jnp.concatenate(padded).reshape(total, LANES), starts


def _unpack_rows(buf, starts, shapes):
    lead = buf.shape[:-2]
    flat = buf.reshape(lead + (-1,))
    out = []
    for st, shp in zip(starts, shapes):
        n = int(np.prod(shp))
        out.append(flat[..., st * LANES:st * LANES + n].reshape(lead + tuple(shp)))
    return out


def kernel(x, mem, g_mix_pre, w_in, b_f, rel_bias, w_out, g_mix_post, g_xattn_pre, g_mem, w_xq, w_xk, w_xv, w_xo, g_xattn_post, g_ffn_pre, w_gate, w_up, w_down, g_ffn_post, loss_target, m_g_mix_pre, m_w_in, m_b_f, m_rel_bias, m_w_out, m_g_mix_post, m_g_xattn_pre, m_g_mem, m_w_xq, m_w_xk, m_w_xv, m_w_xo, m_g_xattn_post, m_g_ffn_pre, m_w_gate, m_w_up, m_w_down, m_g_ffn_post, v_g_mix_pre, v_w_in, v_b_f, v_rel_bias, v_w_out, v_g_mix_post, v_g_xattn_pre, v_g_mem, v_w_xq, v_w_xk, v_w_xv, v_w_xo, v_g_xattn_post, v_g_ffn_pre, v_w_gate, v_w_up, v_w_down, v_g_ffn_post):
    given = dict(locals())
    order = ("g_mix_pre", "w_in", "b_f", "rel_bias", "w_out", "g_mix_post", "g_xattn_pre", "g_mem", "w_xq", "w_xk", "w_xv",
             "w_xo", "g_xattn_post", "g_ffn_pre", "w_gate", "w_up", "w_down", "g_ffn_post")
    two_d = lambda a: a.reshape(a.shape[-2:])
    w_loc = {n: two_d(given[n]) for n in order}
    m_loc = {n: two_d(given["m_" + n]) for n in order}
    v_loc = {n: two_d(given["v_" + n]) for n in order}
    d_model = x.shape[-1]

    shard_shapes = [w_loc[n].shape for n, _ in _BIG]
    packed, starts = _pack_rows([w_loc[n].reshape(-1) for n, _ in _BIG], BIG_ROW_ALIGN, BF16)
    gathered = _all_gather(packed, "gather_weights")
    w_bf = {}
    for (n, axis), part in zip(_BIG, _unpack_rows(gathered, starts, shard_shapes)):
        r, c = part.shape[1:]
        w_bf[n] = part.reshape(N_DEV * r, c) if axis == 0 else part.transpose(1, 0, 2).reshape(r, N_DEV * c)

    small = {n: w_loc[n] for n in _SMALL}
    sq, grad_x, grads = _local_step(two_d(x), two_d(mem), two_d(loss_target), small, w_bf)

    per_owner = []
    for (n, axis), shp in zip(_BIG, shard_shapes):
        r, c = shp
        gfull = grads[n]
        per_owner.append(gfull.reshape(N_DEV, r * c) if axis == 0 else gfull.reshape(r, N_DEV, c).transpose(1, 0, 2).reshape(N_DEV, r * c))
    rows_big = packed.shape[0]
    slots = []
    for j in range(N_DEV):
        buf, _ = _pack_rows([p[j] for p in per_owner], BIG_ROW_ALIGN, BF16)
        slots.append(buf)
    landed = _exchange(jnp.stack(slots), "exchange_grads")
    g_big = dict(zip([n for n, _ in _BIG], _unpack_rows(_sum_slots(landed, "sum_grads"), starts, shard_shapes)))
    assert landed.shape[1] == rows_big

    small_parts = [grads[n].reshape(-1) for n in _SMALL] + [sq.reshape(-1)]
    small_shapes = [w_loc[n].shape for n in _SMALL] + [sq.shape]
    spacked, sstarts = _pack_rows(small_parts, 8, F32)
    ssum = _sum_slots(_all_gather(spacked, "gather_small"), "sum_small")
    g_small = dict(zip(_SMALL, _unpack_rows(ssum, sstarts, small_shapes)[:-1]))
    sq_rows = sq.size // LANES
    loss = _loss_head(ssum[sstarts[-1]:sstarts[-1] + sq_rows], d_model).reshape(())

    g_loc, delta, new_m, new_v = {}, {}, {}, {}
    for n, _ in _BIG:
        g_loc[n] = g_big[n]
        delta[n], new_m[n], new_v[n] = _adamw(w_loc[n], g_big[n], m_loc[n], v_loc[n], "adamw_" + n)
    pk = lambda d: _pack_rows([d[n].reshape(-1) for n in _SMALL], 8, F32)[0]
    pstarts = _pack_rows([w_loc[n].reshape(-1) for n in _SMALL], 8, F32)[1]
    d_s, m_s, v_s = _adamw(pk(w_loc), pk(g_small), pk(m_loc), pk(v_loc), "adamw_small")
    shapes_s = [w_loc[n].shape for n in _SMALL]
    for n, dd, mm, vv in zip(_SMALL, _unpack_rows(d_s, pstarts, shapes_s), _unpack_rows(m_s, pstarts, shapes_s),
                             _unpack_rows(v_s, pstarts, shapes_s)):
        g_loc[n], delta[n], new_m[n], new_v[n] = g_small[n], dd, mm, vv

    like = lambda d: [d[n].reshape(given[n].shape) for n in order]
    return (loss, grad_x.reshape(x.shape), *like(g_loc), *like(delta), *like(new_m), *like(new_v))
```

```python
import functools

import numpy as np
import jax
import jax.numpy as jnp
from jax import lax
from jax.experimental import pallas as pl
from jax.experimental.pallas import tpu as pltpu

F32 = jnp.float32
BF16 = jnp.bfloat16
HIGHEST = lax.Precision.HIGHEST

RMS_EPS = 1e-6
HEAD_DIM = 64
N_HEADS = 8
HEAD_WIDTH = N_HEADS * HEAD_DIM
PAIR = 2 * HEAD_DIM
N_PAIRS = N_HEADS // 2
DIL_BLOCK = 128
DILATIONS = (1, 4, 16)
N_BUCKETS = 32
MAX_DISTANCE = 2048
N_MEM_HEADS = 4
QK_SCALE = HEAD_DIM ** -0.5
NEG = -1e30
FOX_SKIP_MARGIN = 110.0
N_DEV = 8

ADAM_LR = 0.001
ADAM_B1 = 0.9
ADAM_B2 = 0.999
ADAM_EPS = 1e-08
ADAM_WD = 0.01
ADAM_STEP = 10

V7X_VMEM_LIMIT_BYTES = 56 * 2 ** 20
ROW_TILE = 256
ATT_BLOCK = 512
SCAN_TILE = 256


def _call(body, *, name, grid, in_specs, out_specs, out_shape, scratch=()):
    return pl.pallas_call(
        body, name=name, grid=grid, in_specs=in_specs, out_specs=out_specs, out_shape=out_shape,
        scratch_shapes=list(scratch),
        compiler_params=pltpu.CompilerParams(
            dimension_semantics=("arbitrary",) * len(grid), vmem_limit_bytes=V7X_VMEM_LIMIT_BYTES))


def _rows(tm, n):
    return pl.BlockSpec((tm, n), lambda i: (i, 0))


def _resident(shape):
    zeros = (0,) * len(shape)
    return pl.BlockSpec(shape, lambda i: zeros, pipeline_mode=pl.Buffered(1))


def _acc_out(shape):
    zeros = (0,) * len(shape)
    return pl.BlockSpec(shape, lambda i: zeros)


def _sds(shape, dtype):
    return jax.ShapeDtypeStruct(shape, dtype)


def _dot(a, b):
    return jnp.dot(a, b, preferred_element_type=F32)


def _dot_nt(a, b):
    return lax.dot_general(a, b, (((1,), (1,)), ((), ())), preferred_element_type=F32)


def _dot_tn(a, b):
    return lax.dot_general(a, b, (((0,), (0,)), ((), ())), preferred_element_type=F32)


def _rms_fwd(x, g):
    r = lax.rsqrt(jnp.mean(x * x, axis=-1, keepdims=True) + RMS_EPS)
    return (x * r) * g


def _rms_bwd(xin, g, dy):
    r = lax.rsqrt(jnp.mean(xin * xin, axis=-1, keepdims=True) + RMS_EPS)
    xhat = xin * r
    dg = jnp.sum(dy * xhat, axis=0, keepdims=True)
    dxh = dy * g
    dx = r * (dxh - xhat * jnp.mean(dxh * xhat, axis=-1, keepdims=True))
    return dx, dg


def _first_head_lanes():
    return lax.broadcasted_iota(jnp.int32, (1, PAIR), 1) < HEAD_DIM


def _pick(mask, a, b):
    return jnp.where(mask, a, b)


def _zero_other(mask, v):
    return jnp.where(mask, v, jnp.zeros_like(v))


def _store_lane_blocks(buf_ref, val):
    for cb in range(buf_ref.shape[0]):
        buf_ref[cb] = val[:, cb * PAIR:(cb + 1) * PAIR].astype(F32)


def _load_lane_blocks(buf_ref):
    return jnp.concatenate([buf_ref[cb] for cb in range(buf_ref.shape[0])], axis=1)


def _write_class_major(buf_ref, out_ref, dil):
    n, tile, _ = buf_ref.shape
    for r in range(dil):
        for cb in range(n):
            col = (r * n + cb) * PAIR
            out_ref[:, col:col + PAIR] = buf_ref.at[cb][pl.ds(r, tile // dil, stride=dil), :].astype(out_ref.dtype)


def _read_class_major(in_ref, buf_ref, dil):
    n, tile, _ = buf_ref.shape
    for r in range(dil):
        for cb in range(n):
            col = (r * n + cb) * PAIR
            buf_ref.at[cb][pl.ds(r, tile // dil, stride=dil), :] = in_ref[:, col:col + PAIR].astype(F32)
    return _load_lane_blocks(buf_ref)


def _class_rows(tm, width, dil):
    return _rows(tm // dil, dil * width)


def _in_proj(x, g, w):
    s_len, d = x.shape
    tm = ROW_TILE
    hw = HEAD_WIDTH

    def body(x_ref, g_ref, w_ref, h_ref, fqkv_ref, gx_ref, *rest):
        dil_refs, buf = rest[:-1], rest[-1]
        h = _rms_fwd(x_ref[...], g_ref[...]).astype(BF16)
        h_ref[...] = h
        proj = _dot(h, w_ref[...])
        fqkv_ref[:, 0:hw] = (proj[:, 0:hw] * QK_SCALE).astype(BF16)
        fqkv_ref[:, hw:3 * hw] = proj[:, hw:3 * hw].astype(BF16)
        gx_ref[...] = proj[:, 3 * hw:4 * hw]
        dqkv = jnp.concatenate([proj[:, 4 * hw:5 * hw] * QK_SCALE, proj[:, 5 * hw:7 * hw]], axis=1)
        _store_lane_blocks(buf, dqkv)
        for ref, dil in zip(dil_refs, DILATIONS):
            if dil == 1:
                ref[...] = dqkv.astype(BF16)
            else:
                _write_class_major(buf, ref, dil)

    return _call(
        body, name="in_proj", grid=(s_len // tm,),
        in_specs=[_rows(tm, d), _resident((1, d)), _resident(w.shape)],
        out_specs=[_rows(tm, d), _rows(tm, 3 * hw), _rows(tm, hw)] + [_class_rows(tm, 3 * hw, dil) for dil in DILATIONS],
        out_shape=[_sds((s_len, d), BF16), _sds((s_len, 3 * hw), BF16), _sds((s_len, hw), F32)]
        + [_sds((s_len // dil, dil * 3 * hw), BF16) for dil in DILATIONS],
        scratch=[pltpu.VMEM((3 * hw // PAIR, tm, PAIR), F32)],
    )(x, g, w)


def _swap_halves(x):
    return jnp.concatenate([pltpu.roll(x[:, i * PAIR:(i + 1) * PAIR], HEAD_DIM, 1) for i in range(x.shape[1] // PAIR)], axis=1)


def _split3(x):
    hi = x.astype(BF16)
    r = x - hi.astype(F32)
    mid = r.astype(BF16)
    lo = (r - mid.astype(F32)).astype(BF16)
    return hi, mid, lo


def _lane_in_head(width):
    return lax.broadcasted_iota(jnp.int32, (1, width), 1) % HEAD_DIM


def _place3(jj, first, pieces, base):
    out = base
    for i, p in enumerate(pieces):
        out = jnp.where(jj == first + i, p, out)
    return out


def _gate_scan(gx, b_exp):
    s_len, hw = gx.shape
    t = min(SCAN_TILE, s_len)
    tri = jnp.asarray(np.tril(np.ones((t, t), np.float32)))

    def body(gx_ref, b_ref, tri_ref, aq_ref, ak_ref, c_ref, carry):
        @pl.when(pl.program_id(0) == 0)
        def _():
            carry[...] = jnp.zeros_like(carry)

        z = gx_ref[...] + b_ref[...]
        lf = jnp.minimum(z, 0.0) - jnp.log1p(jnp.exp(-jnp.abs(z)))
        c = jnp.dot(tri_ref[...], lf, precision=HIGHEST, preferred_element_type=F32) + carry[...]
        carry[...] = c[t - 1:t, :]
        c_ref[...] = c
        hi, mid, lo = _split3(_swap_halves(c))
        jj = _lane_in_head(hw)
        zero = jnp.zeros_like(hi)
        one = jnp.ones_like(hi)
        aq_ref[...] = _place3(jj, 0, (hi, mid, lo), jnp.where(jj < 6, one, zero))
        ak_ref[...] = _place3(jj, 3, (-hi, -mid, -lo), jnp.where(jj < 9, one, zero))

    return _call(
        body, name="gate_scan", grid=(s_len // t,),
        in_specs=[_rows(t, hw), _resident((1, hw)), _resident((t, t))],
        out_specs=[_rows(t, hw), _rows(t, hw), _rows(t, hw)],
        out_shape=[_sds((s_len, hw), BF16), _sds((s_len, hw), BF16), _sds((s_len, hw), F32)],
        scratch=[pltpu.VMEM((1, hw), F32)],
    )(gx, b_exp, tri)


def _head_block_ones():
    head_of = np.arange(HEAD_WIDTH) // HEAD_DIM
    return jnp.asarray((head_of[:, None] == head_of[None, :]).astype(np.float32))


def _fox_block_stats(fqkv, c):
    s_len = fqkv.shape[0]
    hw = HEAD_WIDTH
    b = min(ATT_BLOCK, s_len)
    nb = s_len // b

    def body(q_ref, k_ref, c_ref, ones_ref, o_ref):
        q, k, cv = q_ref[...].astype(F32), k_ref[...].astype(F32), c_ref[...]
        seg = lambda x: _dot(x.astype(BF16), ones_ref[...])
        col_max = lambda x: jnp.max(x, axis=0, keepdims=True)
        col_min = lambda x: jnp.min(x, axis=0, keepdims=True)
        qn = 1.01 * jnp.sqrt(col_max(seg(q * q)))
        kn = 1.01 * jnp.sqrt(col_max(seg(k * k)))
        dmin = col_min(seg(q * k)) - (2.0 ** -8) * qn * kn
        o_ref[0] = jnp.concatenate([qn, col_max(cv) - dmin, kn, col_min(cv), jnp.zeros((4, hw), F32)], axis=0)

    stats = _call(
        body, name="fox_block_stats", grid=(nb,),
        in_specs=[pl.BlockSpec((b, hw), lambda i: (i, 0)), pl.BlockSpec((b, hw), lambda i: (i, 1)), _rows(b, hw),
                  _resident((hw, hw))],
        out_specs=pl.BlockSpec((1, 8, hw), lambda i: (i, 0, 0)),
        out_shape=_sds((nb, 8, hw), F32),
    )(fqkv, fqkv, c, _head_block_ones().astype(BF16))
    st = jnp.transpose(stats[:, :4, ::HEAD_DIM], (1, 2, 0))
    bound = st[0][:, :, None] * st[2][:, None, :] + st[1][:, :, None] - st[3][:, None, :]
    need_h = jnp.logical_not(bound < -FOX_SKIP_MARGIN)
    need = jnp.logical_or(need_h[0::2], need_h[1::2])
    ii = lax.broadcasted_iota(jnp.int32, (1, nb, nb), 1)
    jj = lax.broadcasted_iota(jnp.int32, (1, nb, nb), 2)
    first_needed = jnp.min(jnp.where(jnp.logical_or(jnp.logical_and(need, jj < ii), jj == ii), jj, nb), axis=2)
    window = ii[:, :, 0] - first_needed + 1
    in_window = jnp.logical_and(jj >= first_needed[:, :, None], jj <= ii)
    last_query = jnp.max(jnp.where(in_window, ii, 0), axis=1)
    return jnp.concatenate([st.reshape(4 * N_HEADS, nb), window.astype(F32), last_query.astype(F32)], axis=0)


FOX_WINDOW_ROW = 4 * N_HEADS
FOX_LAST_QUERY_ROW = 4 * N_HEADS + N_PAIRS


def _head_needed(st_ref, h, i, j):
    bound = st_ref[h, i] * st_ref[2 * N_HEADS + h, j] + st_ref[N_HEADS + h, i] - st_ref[3 * N_HEADS + h, j]
    return jnp.logical_not(bound < -FOX_SKIP_MARGIN)


def _lower_triangle(n):
    return lax.broadcasted_iota(jnp.int32, (n, n), 1) <= lax.broadcasted_iota(jnp.int32, (n, n), 0)


def _fox_fwd(fqkv, aq, ak, stats):
    s_len = fqkv.shape[0]
    bq = bk = min(ATT_BLOCK, s_len)
    nq, nk = s_len // bq, s_len // bk

    def body(st_ref, q_ref, k_ref, v_ref, aq_ref, ak_ref, o_ref, lse_ref, qa_ref, m_ref, acc_ref):
        pair, qi, back = pl.program_id(0), pl.program_id(1), pl.program_id(2)
        kj = jnp.maximum(qi - back, 0)
        in0 = _first_head_lanes()
        not0 = jnp.logical_not(in0)

        @pl.when(back == 0)
        def _():
            q2, a2 = q_ref[...], aq_ref[...]
            qa_ref[0] = jnp.where(in0, q2, a2)
            qa_ref[1] = jnp.where(in0, a2, q2)
            m_ref[...] = jnp.full_like(m_ref, NEG)
            acc_ref[...] = jnp.zeros_like(acc_ref)

        def head_step(a, mine, masked):
            k2, v2 = k_ref[...], v_ref[...]
            s = _dot_nt(qa_ref[a], jnp.where(mine, k2, ak_ref[...]))
            if masked:
                s = jnp.where(_lower_triangle(bq), s, NEG)
            m_old = m_ref[a]
            m_new = jnp.maximum(m_old, jnp.max(s, axis=1, keepdims=True))
            p = jnp.exp(s - jnp.tile(m_new, (1, bk // PAIR))).astype(BF16)
            acc_ref[a] = jnp.exp(m_old - m_new) * acc_ref[a] + _dot(p, jnp.where(mine, v2, jnp.ones_like(v2)))
            m_ref[a] = m_new

        @pl.when(back == 0)
        def _():
            head_step(0, in0, True)
            head_step(1, not0, True)

        below = jnp.logical_and(back > 0, back <= qi)
        for a, mine in ((0, in0), (1, not0)):
            @pl.when(jnp.logical_and(below, _head_needed(st_ref, 2 * pair + a, qi, kj)))
            def _():
                head_step(a, mine, False)

        @pl.when(back == qi)
        def _():
            acc0, acc1 = acc_ref[0], acc_ref[1]
            l2 = pltpu.roll(_pick(in0, acc1, acc0), HEAD_DIM, 1)
            o_ref[...] = _pick(in0, acc0, acc1) / l2
            lse_ref[...] = _pick(in0, m_ref[0], m_ref[1]) + jnp.log(l2)

    blk = lambda rows: (rows, PAIR)
    qmap = lambda p, i, b, st: (i, p)

    def key_block(p, i, b, st):
        window = jnp.clip(st[FOX_WINDOW_ROW + p, i].astype(jnp.int32), 1, i + 1)
        return i - jnp.minimum(b, window - 1)

    return pl.pallas_call(
        body, name="fox_fwd",
        grid_spec=pltpu.PrefetchScalarGridSpec(
            num_scalar_prefetch=1, grid=(N_PAIRS, nq, nk),
            in_specs=[
                pl.BlockSpec(blk(bq), qmap),
                pl.BlockSpec(blk(bk), lambda p, i, b, st: (key_block(p, i, b, st), N_PAIRS + p)),
                pl.BlockSpec(blk(bk), lambda p, i, b, st: (key_block(p, i, b, st), 2 * N_PAIRS + p)),
                pl.BlockSpec(blk(bq), qmap),
                pl.BlockSpec(blk(bk), lambda p, i, b, st: (key_block(p, i, b, st), p)),
            ],
            out_specs=[pl.BlockSpec(blk(bq), qmap), pl.BlockSpec(blk(bq), qmap)],
            scratch_shapes=[pltpu.VMEM((2, bq, PAIR), BF16), pltpu.VMEM((2, bq, PAIR), F32), pltpu.VMEM((2, bq, PAIR), F32)]),
        out_shape=[_sds((s_len, HEAD_WIDTH), F32), _sds((s_len, HEAD_WIDTH), F32)],
        compiler_params=pltpu.CompilerParams(dimension_semantics=("arbitrary",) * 3, vmem_limit_bytes=V7X_VMEM_LIMIT_BYTES),
    )(stats, fqkv, fqkv, fqkv, aq, ak)


def _t5_bucket(dist):
    max_exact = N_BUCKETS // 2
    d = np.maximum(dist, 1).astype(np.float32)
    large = max_exact + (np.log(d / max_exact) / np.log(MAX_DISTANCE / max_exact) * (N_BUCKETS - max_exact)).astype(np.int32)
    large = np.minimum(large, N_BUCKETS - 1)
    return np.where(dist < max_exact, dist, large).astype(np.int32)


def _dil_buckets():
    w = DIL_BLOCK
    qi = np.arange(w)[:, None]
    kj = np.arange(2 * w)[None, :]
    sub = qi + w - kj
    band = (sub >= 0) & (sub <= w)
    out = [np.where(band, _t5_bucket(np.clip(sub, 0, w) * dil), -1) for dil in DILATIONS]
    return np.stack(out).astype(np.int32)


def _dil_bias(rel_bias, buckets):
    w = DIL_BLOCK

    def body(rb_ref, bk_ref, o_ref):
        for p in range(len(DILATIONS)):
            bk = bk_ref[p]
            for h in range(N_HEADS):
                def add(b, acc):
                    return acc + jnp.where(bk == b, rb_ref[b, h], 0.0)
                acc = lax.fori_loop(0, N_BUCKETS, add, jnp.zeros((w, 2 * w), F32))
                o_ref[p, h] = jnp.where(bk < 0, NEG, acc)

    return pl.pallas_call(
        body, name="dil_bias",
        in_specs=[pl.BlockSpec(memory_space=pltpu.SMEM), pl.BlockSpec(memory_space=pltpu.VMEM)],
        out_specs=pl.BlockSpec(memory_space=pltpu.VMEM),
        out_shape=_sds((len(DILATIONS), N_HEADS, w, 2 * w), F32),
    )(rel_bias, buckets)


def _dil_fwd(view, bias, branch):
    dil = DILATIONS[branch]
    w = DIL_BLOCK
    hw = HEAD_WIDTH
    length = view.shape[0]
    nb = length // w

    def body(q_ref, kc_ref, kp_ref, vc_ref, vp_ref, b_ref, o_ref, lse_ref):
        n = pl.program_id(1)
        in0 = _first_head_lanes()
        not0 = jnp.logical_not(in0)
        pairs = [slice(pr * PAIR, (pr + 1) * PAIR) for pr in range(N_PAIRS)]
        tiles = []
        for sl in pairs:
            q2 = q_ref[:, sl]
            qq = jnp.concatenate([_zero_other(in0, q2), _zero_other(not0, q2)], axis=0)
            tiles.append(jnp.concatenate([_dot_nt(qq, kp_ref[:, sl]), _dot_nt(qq, kc_ref[:, sl])], axis=1))
        s = jnp.concatenate(tiles, axis=0) + b_ref[...].reshape(N_HEADS * w, 2 * w)
        prev_half = lax.broadcasted_iota(jnp.int32, (1, 2 * w), 1) < w
        s = jnp.where(jnp.logical_and(n == 0, prev_half), NEG, s)
        m = jnp.max(s, axis=1, keepdims=True)
        e = jnp.exp(s - m)
        l = jnp.sum(e, axis=1, keepdims=True)
        p = (e / l).astype(BF16)
        lse = m + jnp.log(l)
        for pr, sl in enumerate(pairs):
            pp = p[2 * pr * w:(2 * pr + 2) * w]
            o2 = _dot(pp[:, :w], vp_ref[:, sl]) + _dot(pp[:, w:], vc_ref[:, sl])
            o_ref[:, sl] = _pick(in0, o2[:w], o2[w:])
            lse_ref[:, sl] = _pick(in0, lse[2 * pr * w:(2 * pr + 1) * w], lse[(2 * pr + 1) * w:(2 * pr + 2) * w])

    prev = lambda n: jnp.maximum(n - 1, 0)
    out = pl.pallas_call(
        body, name=f"dil_fwd_{dil}", grid=(dil, nb),
        in_specs=[
            pl.BlockSpec((w, hw), lambda r, n: (n, 3 * r)),
            pl.BlockSpec((w, hw), lambda r, n: (n, 3 * r + 1)),
            pl.BlockSpec((w, hw), lambda r, n: (prev(n), 3 * r + 1)),
            pl.BlockSpec((w, hw), lambda r, n: (n, 3 * r + 2)),
            pl.BlockSpec((w, hw), lambda r, n: (prev(n), 3 * r + 2)),
            pl.BlockSpec((None, N_HEADS, w, 2 * w), lambda r, n: (branch, 0, 0, 0)),
        ],
        out_specs=[pl.BlockSpec((w, hw), lambda r, n: (n, r)), pl.BlockSpec((w, hw), lambda r, n: (n, r))],
        out_shape=[_sds((length, dil * hw), F32), _sds((length, dil * hw), F32)],
        compiler_params=pltpu.CompilerParams(dimension_semantics=("arbitrary",) * 2, vmem_limit_bytes=V7X_VMEM_LIMIT_BYTES),
    )(view, view, view, view, view, bias)
    return out[0], out[1]


def _mix_out(o_fox, o_br, lse_br, w_out, x, g_post):
    s_len, d = x.shape
    hw = HEAD_WIDTH
    tm = ROW_TILE

    def body(of_ref, o1, o2, o3, l1, l2, l3, w_ref, x_ref, g_ref, x1_ref, y1_ref, od_ref, lj1, lj2, lj3, buf):
        def natural(ref, dil):
            return ref[...] if dil == 1 else _read_class_major(ref, buf, dil)

        ob = [natural(r, dil) for r, dil in zip((o1, o2, o3), DILATIONS)]
        la, lb, lc = [natural(r, dil) for r, dil in zip((l1, l2, l3), DILATIONS)]
        m = jnp.maximum(jnp.maximum(la, lb), lc)
        ea, eb, ec = jnp.exp(la - m), jnp.exp(lb - m), jnp.exp(lc - m)
        tot = ea + eb + ec
        o_dil = (ea / tot) * ob[0] + (eb / tot) * ob[1] + (ec / tot) * ob[2]
        od_ref[...] = o_dil
        lj = m + jnp.log(tot)
        _store_lane_blocks(buf, lj)
        for ref, dil in zip((lj1, lj2, lj3), DILATIONS):
            if dil == 1:
                ref[...] = lj
            else:
                _write_class_major(buf, ref, dil)
        y = _dot(of_ref[...].astype(BF16), w_ref[0:hw, :]) + _dot(o_dil.astype(BF16), w_ref[hw:2 * hw, :])
        y1_ref[...] = y
        x1_ref[...] = x_ref[...] + _rms_fwd(y, g_ref[...])

    half = _rows(tm, hw)
    by_dil = [_class_rows(tm, hw, dil) for dil in DILATIONS]
    return _call(
        body, name="mix_out", grid=(s_len // tm,),
        in_specs=[half] + by_dil + by_dil + [_resident(w_out.shape), _rows(tm, d), _resident((1, d))],
        out_specs=[_rows(tm, d), _rows(tm, d), half] + by_dil,
        out_shape=[_sds((s_len, d), F32), _sds((s_len, d), F32), _sds((s_len, hw), F32)]
        + [_sds((s_len // dil, dil * hw), F32) for dil in DILATIONS],
        scratch=[pltpu.VMEM((hw // PAIR, tm, PAIR), F32)],
    )(o_fox, *o_br, *lse_br, w_out, x, g_post)


def _mem_fwd(mem, g_mem, w_xk, w_xv):
    n_mem, d = mem.shape
    mw = w_xk.shape[1]

    def body(mem_ref, g_ref, wk_ref, wv_ref, hm_ref, k_ref, v_ref):
        hm = _rms_fwd(mem_ref[...], g_ref[...]).astype(BF16)
        hm_ref[...] = hm
        k_ref[...] = _dot(hm, wk_ref[...]).astype(BF16)
        v_ref[...] = _dot(hm, wv_ref[...]).astype(BF16)

    return pl.pallas_call(
        body, name="mem_fwd",
        out_shape=[_sds((n_mem, d), BF16), _sds((n_mem, mw), BF16), _sds((n_mem, mw), BF16)],
    )(mem, g_mem, w_xk, w_xv)


def _xattn_softmax(qa, k2):
    s = _dot_nt(qa, k2)
    m = jnp.max(s, axis=1, keepdims=True)
    e = jnp.exp(s - m)
    return e / jnp.sum(e, axis=1, keepdims=True)


def _xattn_fwd(x1, g_pre, w_xq, kx, vx, w_xo, g_post):
    s_len, d = x1.shape
    mw = w_xq.shape[1]
    n_mem = kx.shape[0]
    tm = ROW_TILE

    def body(x_ref, gp_ref, wq_ref, k_ref, v_ref, wo_ref, go_ref, x2_ref, y2_ref, h2_ref, q_ref, o_ref):
        x = x_ref[...]
        h = _rms_fwd(x, gp_ref[...]).astype(BF16)
        h2_ref[...] = h
        q = (_dot(h, wq_ref[...]) * QK_SCALE).astype(BF16)
        q_ref[...] = q
        in0 = _first_head_lanes()
        not0 = jnp.logical_not(in0)
        for pr in range(mw // PAIR):
            sl = slice(pr * PAIR, (pr + 1) * PAIR)
            q2, k2, v2 = q[:, sl], k_ref[:, sl], v_ref[:, sl]
            oa = [_dot(_xattn_softmax(_zero_other(mine, q2), k2).astype(BF16), v2) for mine in (in0, not0)]
            o_ref[:, sl] = _pick(in0, oa[0], oa[1]).astype(BF16)
        y = _dot(o_ref[...], wo_ref[...])
        y2_ref[...] = y
        x2_ref[...] = x + _rms_fwd(y, go_ref[...])

    return _call(
        body, name="xattn_fwd", grid=(s_len // tm,),
        in_specs=[_rows(tm, d), _resident((1, d)), _resident(w_xq.shape), _resident((n_mem, mw)), _resident((n_mem, mw)),
                  _resident(w_xo.shape), _resident((1, d))],
        out_specs=[_rows(tm, d), _rows(tm, d), _rows(tm, d), _rows(tm, mw), _rows(tm, mw)],
        out_shape=[_sds((s_len, d), F32), _sds((s_len, d), F32), _sds((s_len, d), BF16), _sds((s_len, mw), BF16),
                   _sds((s_len, mw), BF16)],
    )(x1, g_pre, w_xq, kx, vx, w_xo, g_post)


def _ffn_up(x2, g_pre, w_gate, w_up):
    s_len, d = x2.shape
    dff = w_gate.shape[1]
    tm = ROW_TILE

    def body(x_ref, g_ref, wg_ref, wu_ref, h_ref, a_ref, u_ref, z_ref):
        h = _rms_fwd(x_ref[...], g_ref[...]).astype(BF16)
        h_ref[...] = h
        a = _dot(h, wg_ref[...])
        u = _dot(h, wu_ref[...])
        a_ref[...] = a.astype(BF16)
        u_ref[...] = u.astype(BF16)
        z_ref[...] = ((a * jax.nn.sigmoid(a)) * u).astype(BF16)

    return _call(
        body, name="ffn_up", grid=(s_len // tm,),
        in_specs=[_rows(tm, d), _resident((1, d)), _resident(w_gate.shape), _resident(w_up.shape)],
        out_specs=[_rows(tm, d), _rows(tm, dff), _rows(tm, dff), _rows(tm, dff)],
        out_shape=[_sds((s_len, d), BF16)] + [_sds((s_len, dff), BF16)] * 3,
    )(x2, g_pre, w_gate, w_up)


def _ffn_down_loss(z, w_down, x2, g_post, target):
    s_len, d = x2.shape
    dff = z.shape[1]
    tm = ROW_TILE

    def body(z_ref, w_ref, x_ref, g_ref, t_ref, y_ref, dx_ref, sq_ref):
        @pl.when(pl.program_id(0) == 0)
        def _():
            sq_ref[...] = jnp.zeros_like(sq_ref)

        y = _dot(z_ref[...], w_ref[...])
        y_ref[...] = y
        err = (x_ref[...] + _rms_fwd(y, g_ref[...])) - t_ref[...]
        sq_ref[...] += jnp.sum(err * err, axis=0, keepdims=True)
        dx_ref[...] = err * (1.0 / d)

    return _call(
        body, name="ffn_down_loss", grid=(s_len // tm,),
        in_specs=[_rows(tm, dff), _resident(w_down.shape), _rows(tm, d), _resident((1, d)), _rows(tm, d)],
        out_specs=[_rows(tm, d), _rows(tm, d), _acc_out((1, d))],
        out_shape=[_sds((s_len, d), F32), _sds((s_len, d), F32), _sds((1, d), F32)],
    )(z, w_down, x2, g_post, target)


def _weight_grad(a, b, name):
    s_len, k = a.shape
    n = b.shape[1]
    ts = 512 if s_len % 512 == 0 else s_len
    tn = n
    while k * tn * 4 > 8 * 2 ** 20 and tn % 256 == 0:
        tn //= 2

    def body(a_ref, b_ref, o_ref):
        @pl.when(pl.program_id(1) == 0)
        def _():
            o_ref[...] = jnp.zeros_like(o_ref)

        o_ref[...] += _dot_tn(a_ref[...].astype(BF16), b_ref[...].astype(BF16))

    return pl.pallas_call(
        body, name=name, grid=(n // tn, s_len // ts),
        in_specs=[pl.BlockSpec((ts, k), lambda j, i: (i, 0)), pl.BlockSpec((ts, tn), lambda j, i: (i, j))],
        out_specs=pl.BlockSpec((k, tn), lambda j, i: (0, j)),
        out_shape=_sds((k, n), F32),
        compiler_params=pltpu.CompilerParams(dimension_semantics=("arbitrary",) * 2, vmem_limit_bytes=V7X_VMEM_LIMIT_BYTES),
    )(a, b)


def _ffn_bwd_a(dx3, y3, g_post, w_down_t, a, u):
    s_len, d = dx3.shape
    dff = a.shape[1]
    tm = ROW_TILE

    def body(dx_ref, y_ref, g_ref, w_ref, a_ref, u_ref, dy_ref, da_ref, du_ref, dg_ref):
        @pl.when(pl.program_id(0) == 0)
        def _():
            dg_ref[...] = jnp.zeros_like(dg_ref)

        dy, dg = _rms_bwd(y_ref[...], g_ref[...], dx_ref[...])
        dg_ref[...] += dg
        dyb = dy.astype(BF16)
        dy_ref[...] = dyb
        dz = _dot(dyb, w_ref[...])
        av = a_ref[...].astype(F32)
        uv = u_ref[...].astype(F32)
        sg = jax.nn.sigmoid(av)
        da_ref[...] = (dz * uv * (sg * (1.0 + av * (1.0 - sg)))).astype(BF16)
        du_ref[...] = (dz * (av * sg)).astype(BF16)

    return _call(
        body, name="ffn_bwd_a", grid=(s_len // tm,),
        in_specs=[_rows(tm, d), _rows(tm, d), _resident((1, d)), _resident(w_down_t.shape), _rows(tm, dff), _rows(tm, dff)],
        out_specs=[_rows(tm, d), _rows(tm, dff), _rows(tm, dff), _acc_out((1, d))],
        out_shape=[_sds((s_len, d), BF16), _sds((s_len, dff), BF16), _sds((s_len, dff), BF16), _sds((1, d), F32)],
    )(dx3, y3, g_post, w_down_t, a, u)


def _ffn_bwd_b(da, du, w_gate_t, w_up_t, dx3, x2, g_pre):
    s_len, d = x2.shape
    dff = da.shape[1]
    tm = ROW_TILE

    def body(da_ref, du_ref, wg_ref, wu_ref, dx_ref, x_ref, g_ref, o_ref, dg_ref):
        @pl.when(pl.program_id(0) == 0)
        def _():
            dg_ref[...] = jnp.zeros_like(dg_ref)

        dh = _dot(da_ref[...], wg_ref[...]) + _dot(du_ref[...], wu_ref[...])
        dx, dg = _rms_bwd(x_ref[...], g_ref[...], dh)
        dg_ref[...] += dg
        o_ref[...] = dx_ref[...] + dx

    return _call(
        body, name="ffn_bwd_b", grid=(s_len // tm,),
        in_specs=[_rows(tm, dff), _rows(tm, dff), _resident(w_gate_t.shape), _resident(w_up_t.shape), _rows(tm, d),
                  _rows(tm, d), _resident((1, d))],
        out_specs=[_rows(tm, d), _acc_out((1, d))],
        out_shape=[_sds((s_len, d), F32), _sds((1, d), F32)],
    )(da, du, w_gate_t, w_up_t, dx3, x2, g_pre)


def _xattn_bwd(dx2, y2, g_post, w_xo_t, q, kx, vx, w_xq_t, x1, g_pre):
    s_len, d = x1.shape
    mw = q.shape[1]
    n_mem = kx.shape[0]
    tm = ROW_TILE

    def body(dx_ref, y_ref, go_ref, wo_ref, q_ref, k_ref, v_ref, wq_ref, x_ref, gp_ref,
             dx1_ref, dy_ref, dq_ref, dk_ref, dv_ref, dgo_ref, dgp_ref):
        @pl.when(pl.program_id(0) == 0)
        def _():
            dk_ref[...] = jnp.zeros_like(dk_ref)
            dv_ref[...] = jnp.zeros_like(dv_ref)
            dgo_ref[...] = jnp.zeros_like(dgo_ref)
            dgp_ref[...] = jnp.zeros_like(dgp_ref)

        dxin = dx_ref[...]
        dy, dgo = _rms_bwd(y_ref[...], go_ref[...], dxin)
        dgo_ref[...] += dgo
        dyb = dy.astype(BF16)
        dy_ref[...] = dyb
        do = _dot(dyb, wo_ref[...]).astype(BF16)
        in0 = _first_head_lanes()
        not0 = jnp.logical_not(in0)
        for pr in range(mw // PAIR):
            sl = slice(pr * PAIR, (pr + 1) * PAIR)
            q2, k2, v2, do2 = q_ref[:, sl], k_ref[:, sl], v_ref[:, sl], do[:, sl]
            dqs = []
            dk2 = jnp.zeros((n_mem, PAIR), F32)
            dv2 = jnp.zeros((n_mem, PAIR), F32)
            for mine in (in0, not0):
                qa = _zero_other(mine, q2)
                doa = _zero_other(mine, do2)
                p = _xattn_softmax(qa, k2)
                dp = _dot_nt(doa, v2)
                ds = (p * (dp - jnp.sum(p * dp, axis=1, keepdims=True))).astype(BF16)
                dqs.append(_dot(ds, k2))
                dk2 = dk2 + _dot_tn(ds, qa)
                dv2 = dv2 + _dot_tn(p.astype(BF16), doa)
            dq_ref[:, sl] = (_pick(in0, dqs[0], dqs[1]) * QK_SCALE).astype(BF16)
            dk_ref[:, sl] += dk2
            dv_ref[:, sl] += dv2
        dh = _dot(dq_ref[...], wq_ref[...])
        dx, dgp = _rms_bwd(x_ref[...], gp_ref[...], dh)
        dgp_ref[...] += dgp
        dx1_ref[...] = dxin + dx

    return _call(
        body, name="xattn_bwd", grid=(s_len // tm,),
        in_specs=[_rows(tm, d), _rows(tm, d), _resident((1, d)), _resident(w_xo_t.shape), _rows(tm, mw),
                  _resident((n_mem, mw)), _resident((n_mem, mw)), _resident(w_xq_t.shape), _rows(tm, d), _resident((1, d))],
        out_specs=[_rows(tm, d), _rows(tm, d), _rows(tm, mw), _acc_out((n_mem, mw)), _acc_out((n_mem, mw)),
                   _acc_out((1, d)), _acc_out((1, d))],
        out_shape=[_sds((s_len, d), F32), _sds((s_len, d), BF16), _sds((s_len, mw), BF16), _sds((n_mem, mw), F32),
                   _sds((n_mem, mw), F32), _sds((1, d), F32), _sds((1, d), F32)],
    )(dx2, y2, g_post, w_xo_t, q, kx, vx, w_xq_t, x1, g_pre)


def _mem_bwd(dk, dv, w_xk_t, w_xv_t, hm, mem, g_mem):
    n_mem, d = mem.shape
    mw = dk.shape[1]

    def body(dk_ref, dv_ref, wk_ref, wv_ref, hm_ref, mem_ref, g_ref, dwk_ref, dwv_ref, dg_ref):
        dkb = dk_ref[...].astype(BF16)
        dvb = dv_ref[...].astype(BF16)
        dhm = _dot(dkb, wk_ref[...]) + _dot(dvb, wv_ref[...])
        _, dg = _rms_bwd(mem_ref[...], g_ref[...], dhm)
        dg_ref[...] = dg
        dwk_ref[...] = _dot_tn(hm_ref[...], dkb)
        dwv_ref[...] = _dot_tn(hm_ref[...], dvb)

    return pl.pallas_call(
        body, name="mem_bwd",
        out_shape=[_sds((d, mw), F32), _sds((d, mw), F32), _sds((1, d), F32)],
    )(dk, dv, w_xk_t, w_xv_t, hm, mem, g_mem)


def _mix_out_bwd(dx1, y1, g_post, w_out_t, o_fox, o_dil):
    s_len, d = dx1.shape
    hw = HEAD_WIDTH
    tm = ROW_TILE
    ones = _head_block_ones().astype(BF16)

    def body(dx_ref, y_ref, g_ref, w_ref, of_ref, od_ref, ones_ref, dy_ref, dof_ref, dlf_ref, dg_ref,
             dod1, dod2, dod3, dld1, dld2, dld3, buf):
        @pl.when(pl.program_id(0) == 0)
        def _():
            dg_ref[...] = jnp.zeros_like(dg_ref)

        dy, dg = _rms_bwd(y_ref[...], g_ref[...], dx_ref[...])
        dg_ref[...] += dg
        dyb = dy.astype(BF16)
        dy_ref[...] = dyb
        do = _dot(dyb, w_ref[...])
        def head_sums(x):
            hi = x.astype(BF16)
            lo = (x - hi.astype(F32)).astype(BF16)
            return _dot(hi, ones_ref[...]) + _dot(lo, ones_ref[...])

        dof_ref[...] = do[:, 0:hw].astype(BF16)
        dlf_ref[...] = head_sums(do[:, 0:hw] * of_ref[...])
        do_dil = do[:, hw:2 * hw]
        dl_dil = head_sums(do_dil * od_ref[...])
        for val, refs in ((do_dil, (dod1, dod2, dod3)), (dl_dil, (dld1, dld2, dld3))):
            _store_lane_blocks(buf, val)
            for ref, dil in zip(refs, DILATIONS):
                if dil == 1:
                    ref[...] = val.astype(ref.dtype)
                else:
                    _write_class_major(buf, ref, dil)

    half = _rows(tm, hw)
    by_dil = [_class_rows(tm, hw, dil) for dil in DILATIONS]
    outs = _call(
        body, name="mix_out_bwd", grid=(s_len // tm,),
        in_specs=[_rows(tm, d), _rows(tm, d), _resident((1, d)), _resident(w_out_t.shape), half, half, _resident((hw, hw))],
        out_specs=[_rows(tm, d), half, half, _acc_out((1, d))] + by_dil + by_dil,
        out_shape=[_sds((s_len, d), BF16), _sds((s_len, hw), BF16), _sds((s_len, hw), F32), _sds((1, d), F32)]
        + [_sds((s_len // dil, dil * hw), BF16) for dil in DILATIONS]
        + [_sds((s_len // dil, dil * hw), F32) for dil in DILATIONS],
        scratch=[pltpu.VMEM((hw // PAIR, tm, PAIR), F32)],
    )(dx1, y1, g_post, w_out_t, o_fox, o_dil, ones)
    return outs[0], outs[1], outs[2], outs[3], outs[4:7], outs[7:10]


def _fox_bwd_prep(aq, lse, delta):
    s_len, hw = aq.shape
    tm = ROW_TILE

    def body(aq_ref, lse_ref, dl_ref, aql_ref, ad_ref):
        jj = _lane_in_head(hw)
        l3 = _split3(_swap_halves(lse_ref[...]))
        aql_ref[...] = _place3(jj, 6, [-p for p in l3], aq_ref[...])
        d3 = _split3(_swap_halves(dl_ref[...]))
        ad_ref[...] = _place3(jj, 0, [-p for p in d3], jnp.zeros((tm, hw), BF16))

    half = _rows(tm, hw)
    return _call(
        body, name="fox_bwd_prep", grid=(s_len // tm,),
        in_specs=[half, half, half], out_specs=[half, half],
        out_shape=[_sds((s_len, hw), BF16), _sds((s_len, hw), BF16)],
    )(aq, lse, delta)


def _ones_on_first3(shape):
    jj = lax.broadcasted_iota(jnp.int32, shape, 1) % HEAD_DIM
    return jnp.where(jj < 3, 1.0, 0.0).astype(BF16)


def _fox_bwd(fqkv, do, aql, ak, ad, stats):
    s_len = fqkv.shape[0]
    bq = bk = min(ATT_BLOCK, s_len)
    nq, nk = s_len // bq, s_len // bk

    def body(st_ref, q_ref, k_ref, v_ref, do_ref, aql_ref, ak_ref, ad_ref, dq_ref, rs_ref, dk_ref, dv_ref, dc_ref,
             ka_ref, va_ref, kone_ref, r_ref, dvacc_ref, dqacc_ref):
        pair, kj, qi = pl.program_id(0), pl.program_id(1), pl.program_id(2)
        in0 = _first_head_lanes()
        not0 = jnp.logical_not(in0)
        heads = ((0, in0), (1, not0))
        rows = pl.ds(pl.multiple_of(qi * bq, bq), bq)

        @pl.when(qi == 0)
        def _():
            k2, v2, a2 = k_ref[...], v_ref[...], ak_ref[...]
            one = jnp.ones_like(k2)
            one3 = _ones_on_first3(v2.shape)
            for a, mine in heads:
                ka_ref[a] = jnp.where(mine, k2, a2)
                va_ref[a] = jnp.where(mine, v2, one3)
                kone_ref[a] = jnp.where(mine, k2, one)
            r_ref[...] = jnp.zeros_like(r_ref)
            dvacc_ref[...] = jnp.zeros_like(dvacc_ref)

        @pl.when(kj == 0)
        def _():
            for a, _ in heads:
                dqacc_ref[a, rows, :] = jnp.zeros((bq, PAIR), F32)

        def head_step(a, mine, masked):
            q2 = q_ref[...]
            doa = jnp.where(mine, do_ref[...], ad_ref[...])
            s = _dot_nt(jnp.where(mine, q2, aql_ref[...]), ka_ref[a])
            if masked:
                s = jnp.where(_lower_triangle(bq), s, NEG)
            p = jnp.exp(s)
            ds = (p * _dot_nt(doa, va_ref[a])).astype(BF16)
            dvacc_ref[a] += _dot_tn(p.astype(BF16), doa)
            r_ref[a] += _dot_tn(ds, jnp.where(mine, q2, jnp.ones_like(q2)))
            dqacc_ref[a, rows, :] += _dot(ds, kone_ref[a])

        for a, mine in heads:
            @pl.when(jnp.logical_and(qi > kj, _head_needed(st_ref, 2 * pair + a, qi, kj)))
            def _():
                head_step(a, mine, False)

        @pl.when(qi == kj)
        def _():
            for a, mine in heads:
                head_step(a, mine, True)
            acc0, acc1 = dqacc_ref[0, rows, :], dqacc_ref[1, rows, :]
            dq_ref[...] = (_pick(in0, acc0, acc1) * QK_SCALE).astype(BF16)
            rs_ref[...] = pltpu.roll(_pick(in0, acc1, acc0), HEAD_DIM, 1)

        @pl.when(qi == nq - 1)
        def _():
            dk_ref[...] = _pick(in0, r_ref[0], r_ref[1]).astype(BF16)
            dv_ref[...] = _pick(in0, dvacc_ref[0], dvacc_ref[1]).astype(BF16)
            dc_ref[...] = -pltpu.roll(_pick(in0, r_ref[1], r_ref[0]), HEAD_DIM, 1)

    blk = lambda n: (n, PAIR)
    kvmap = lambda p, j, i, st: (j, p)

    def qmap(p, j, i, st):
        last = jnp.clip(st[FOX_LAST_QUERY_ROW + p, j].astype(jnp.int32), j, nq - 1)
        return (jnp.minimum(jnp.maximum(i, j), last), p)

    return pl.pallas_call(
        body, name="fox_bwd",
        grid_spec=pltpu.PrefetchScalarGridSpec(
            num_scalar_prefetch=1, grid=(N_PAIRS, nk, nq),
            in_specs=[
                pl.BlockSpec(blk(bq), qmap),
                pl.BlockSpec(blk(bk), lambda p, j, i, st: (j, N_PAIRS + p)),
                pl.BlockSpec(blk(bk), lambda p, j, i, st: (j, 2 * N_PAIRS + p)),
                pl.BlockSpec(blk(bq), qmap), pl.BlockSpec(blk(bq), qmap),
                pl.BlockSpec(blk(bk), kvmap),
                pl.BlockSpec(blk(bq), qmap),
            ],
            out_specs=[pl.BlockSpec(blk(bk), kvmap)] * 5,
            scratch_shapes=[pltpu.VMEM((2, bk, PAIR), BF16), pltpu.VMEM((2, bk, PAIR), BF16), pltpu.VMEM((2, bk, PAIR), BF16),
                            pltpu.VMEM((2, bk, PAIR), F32), pltpu.VMEM((2, bk, PAIR), F32), pltpu.VMEM((2, s_len, PAIR), F32)]),
        out_shape=[_sds((s_len, HEAD_WIDTH), BF16), _sds((s_len, HEAD_WIDTH), F32), _sds((s_len, HEAD_WIDTH), BF16),
                   _sds((s_len, HEAD_WIDTH), BF16), _sds((s_len, HEAD_WIDTH), F32)],
        compiler_params=pltpu.CompilerParams(dimension_semantics=("arbitrary",) * 3, vmem_limit_bytes=V7X_VMEM_LIMIT_BYTES),
    )(stats, fqkv, fqkv, fqkv, do, aql, ak, ad)


def _gate_bwd(rs, dc, gx, b_exp):
    s_len, hw = gx.shape
    t = min(SCAN_TILE, s_len)
    nt = s_len // t
    tri = jnp.asarray(np.triu(np.ones((t, t), np.float32)))

    def body(rs_ref, dc_ref, gx_ref, b_ref, tri_ref, dgx_ref, db_ref, carry):
        @pl.when(pl.program_id(0) == 0)
        def _():
            carry[...] = jnp.zeros_like(carry)
            db_ref[...] = jnp.zeros_like(db_ref)

        dlf = jnp.dot(tri_ref[...], rs_ref[...] + dc_ref[...], precision=HIGHEST, preferred_element_type=F32) + carry[...]
        carry[...] = dlf[0:1, :]
        dgate = dlf * jax.nn.sigmoid(-(gx_ref[...] + b_ref[...]))
        db_ref[...] += jnp.sum(dgate, axis=0, keepdims=True)
        lane = lax.broadcasted_iota(jnp.int32, (1, hw), 1)
        dgx_ref[...] = jnp.where(lane % HEAD_DIM == 0, dgate, 0.0).astype(BF16)

    rev = lambda i: (nt - 1 - i, 0)
    return _call(
        body, name="gate_bwd", grid=(nt,),
        in_specs=[pl.BlockSpec((t, hw), rev)] * 3 + [_resident((1, hw)), _resident((t, t))],
        out_specs=[pl.BlockSpec((t, hw), rev), _acc_out((1, hw))],
        out_shape=[_sds((s_len, hw), BF16), _sds((1, hw), F32)],
        scratch=[pltpu.VMEM((1, hw), F32)],
    )(rs, dc, gx, b_exp, tri)


def _dil_bwd(qkv_v, do_v, lj_v, dl_v, bias, branch):
    dil = DILATIONS[branch]
    w = DIL_BLOCK
    hw = HEAD_WIDTH
    length = qkv_v.shape[0]
    nb = length // w

    def body(q0_ref, q1_ref, kp_ref, kc_ref, vp_ref, vc_ref, do0_ref, do1_ref, l0_ref, l1_ref, d0_ref, d1_ref, b_ref,
             dq_ref, dk_ref, dv_ref, dsum_ref):
        r, n = pl.program_id(0), pl.program_id(1)
        in0 = _first_head_lanes()
        not0 = jnp.logical_not(in0)
        first = n == 0
        last = n == nb - 1

        @pl.when(jnp.logical_and(r == 0, n == 0))
        def _():
            dsum_ref[...] = jnp.zeros_like(dsum_ref)

        pairs = [slice(pr * PAIR, (pr + 1) * PAIR) for pr in range(N_PAIRS)]

        def both_heads(ref, sl):
            v = ref[:, sl]
            return jnp.concatenate([_zero_other(in0, v), _zero_other(not0, v)], axis=0)

        def head_columns(ref):
            return jnp.concatenate([ref[:, h * HEAD_DIM:h * HEAD_DIM + 1] for h in range(N_HEADS)], axis=0)

        qq0 = [both_heads(q0_ref, sl) for sl in pairs]
        qq1 = [both_heads(q1_ref, sl) for sl in pairs]
        dd0 = [both_heads(do0_ref, sl) for sl in pairs]
        dd1 = [both_heads(do1_ref, sl) for sl in pairs]
        stack = lambda tiles: jnp.concatenate(tiles, axis=0)
        s_a = stack([_dot_nt(qq0[i], kp_ref[:, sl]) for i, sl in enumerate(pairs)])
        s_b = stack([_dot_nt(qq0[i], kc_ref[:, sl]) for i, sl in enumerate(pairs)])
        s_c = stack([_dot_nt(qq1[i], kc_ref[:, sl]) for i, sl in enumerate(pairs)])
        dp_a = stack([_dot_nt(dd0[i], vp_ref[:, sl]) for i, sl in enumerate(pairs)])
        dp_b = stack([_dot_nt(dd0[i], vc_ref[:, sl]) for i, sl in enumerate(pairs)])
        dp_c = stack([_dot_nt(dd1[i], vc_ref[:, sl]) for i, sl in enumerate(pairs)])
        bias2 = b_ref[...].reshape(N_HEADS * w, 2 * w)
        b_prev, b_cur = bias2[:, 0:w], bias2[:, w:2 * w]
        lse0, lse1 = head_columns(l0_ref), head_columns(l1_ref)
        dl0, dl1 = head_columns(d0_ref), head_columns(d1_ref)
        p_a = jnp.exp(jnp.where(first, NEG, s_a + b_prev) - lse0)
        p_b = jnp.exp((s_b + b_cur) - lse0)
        p_c = jnp.exp(jnp.where(last, NEG, s_c + b_prev) - lse1)
        ds_a = p_a * (dp_a - dl0)
        ds_b = p_b * (dp_b - dl0)
        ds_c = p_c * (dp_c - dl1)
        dsum_ref[...] += jnp.concatenate([ds_a, ds_b], axis=1).reshape(N_HEADS, w, 2 * w)
        ds_a, ds_b, ds_c = ds_a.astype(BF16), ds_b.astype(BF16), ds_c.astype(BF16)
        p_b, p_c = p_b.astype(BF16), p_c.astype(BF16)
        for i, sl in enumerate(pairs):
            rows = slice(2 * i * w, (2 * i + 2) * w)
            dq2 = _dot(ds_a[rows], kp_ref[:, sl]) + _dot(ds_b[rows], kc_ref[:, sl])
            dq_ref[:, sl] = _pick(in0, dq2[:w], dq2[w:])
            dk_ref[:, sl] = _dot_tn(ds_b[rows], qq0[i]) + _dot_tn(ds_c[rows], qq1[i])
            dv_ref[:, sl] = _dot_tn(p_b[rows], dd0[i]) + _dot_tn(p_c[rows], dd1[i])

    prev = lambda n: jnp.maximum(n - 1, 0)
    nxt = lambda n: jnp.minimum(n + 1, nb - 1)
    blk = (w, hw)
    outs = pl.pallas_call(
        body, name=f"dil_bwd_{dil}", grid=(dil, nb),
        in_specs=[
            pl.BlockSpec(blk, lambda r, n: (n, 3 * r)),
            pl.BlockSpec(blk, lambda r, n: (nxt(n), 3 * r)),
            pl.BlockSpec(blk, lambda r, n: (prev(n), 3 * r + 1)),
            pl.BlockSpec(blk, lambda r, n: (n, 3 * r + 1)),
            pl.BlockSpec(blk, lambda r, n: (prev(n), 3 * r + 2)),
            pl.BlockSpec(blk, lambda r, n: (n, 3 * r + 2)),
            pl.BlockSpec(blk, lambda r, n: (n, r)),
            pl.BlockSpec(blk, lambda r, n: (nxt(n), r)),
            pl.BlockSpec(blk, lambda r, n: (n, r)),
            pl.BlockSpec(blk, lambda r, n: (nxt(n), r)),
            pl.BlockSpec(blk, lambda r, n: (n, r)),
            pl.BlockSpec(blk, lambda r, n: (nxt(n), r)),
            pl.BlockSpec((None, N_HEADS, w, 2 * w), lambda r, n: (branch, 0, 0, 0)),
        ],
        out_specs=[pl.BlockSpec(blk, lambda r, n: (n, r))] * 3 + [pl.BlockSpec((N_HEADS, w, 2 * w), lambda r, n: (0, 0, 0))],
        out_shape=[_sds((length, dil * hw), F32)] * 3 + [_sds((N_HEADS, w, 2 * w), F32)],
        compiler_params=pltpu.CompilerParams(dimension_semantics=("arbitrary",) * 2, vmem_limit_bytes=V7X_VMEM_LIMIT_BYTES),
    )(qkv_v, qkv_v, qkv_v, qkv_v, qkv_v, qkv_v, do_v, do_v, lj_v, lj_v, dl_v, dl_v, bias)
    return list(outs)


def _rel_bias_grad(dsum, buckets):
    w = DIL_BLOCK

    def body(ds_ref, bk_ref, o_ref):
        row = lax.broadcasted_iota(jnp.int32, (N_BUCKETS, PAIR), 0)
        lane = lax.broadcasted_iota(jnp.int32, (N_BUCKETS, PAIR), 1)

        def per_bucket(b, acc):
            for p in range(len(DILATIONS)):
                hit = bk_ref[p] == b
                for h in range(N_HEADS):
                    part = jnp.where(hit, ds_ref[p, h], 0.0)
                    tot = jnp.sum(jnp.sum(part, axis=1, keepdims=True), axis=0, keepdims=True)
                    acc = acc + jnp.where(jnp.logical_and(row == b, lane == h), tot, 0.0)
            return acc

        o_ref[...] = lax.fori_loop(0, N_BUCKETS, per_bucket, jnp.zeros((N_BUCKETS, PAIR), F32))

    return pl.pallas_call(body, name="rel_bias_grad", out_shape=_sds((N_BUCKETS, PAIR), F32))(dsum, buckets)


def _in_proj_bwd(dfq, dfk, dfv, dgx, ddq, ddk, ddv, w_in_t, dx1, x, g_pre):
    s_len, d = x.shape
    hw = HEAD_WIDTH
    tm = ROW_TILE

    def body(fq, fk, fv, gx, q1, q2, q3, k1, k2, k3, v1, v2, v3, w_ref, dx_ref, x_ref, g_ref, o_ref, dp_ref, dg_ref, buf):
        @pl.when(pl.program_id(0) == 0)
        def _():
            dg_ref[...] = jnp.zeros_like(dg_ref)

        def branch_sum(refs):
            a, b, c = [r[...] if dil == 1 else _read_class_major(r, buf, dil) for r, dil in zip(refs, DILATIONS)]
            return (a + b) + c

        dp_ref[:, 0:hw] = fq[...]
        dp_ref[:, hw:2 * hw] = fk[...]
        dp_ref[:, 2 * hw:3 * hw] = fv[...]
        dp_ref[:, 3 * hw:4 * hw] = gx[...]
        dp_ref[:, 4 * hw:5 * hw] = (branch_sum((q1, q2, q3)) * QK_SCALE).astype(BF16)
        dp_ref[:, 5 * hw:6 * hw] = branch_sum((k1, k2, k3)).astype(BF16)
        dp_ref[:, 6 * hw:7 * hw] = branch_sum((v1, v2, v3)).astype(BF16)
        dh = _dot(dp_ref[...], w_ref[...])
        dx, dg = _rms_bwd(x_ref[...], g_ref[...], dh)
        dg_ref[...] += dg
        o_ref[...] = dx_ref[...] + dx

    half = _rows(tm, hw)
    by_dil = [_class_rows(tm, hw, dil) for dil in DILATIONS]
    return _call(
        body, name="in_proj_bwd", grid=(s_len // tm,),
        in_specs=[half] * 4 + by_dil * 3 + [_resident(w_in_t.shape), _rows(tm, d), _rows(tm, d), _resident((1, d))],
        out_specs=[_rows(tm, d), _rows(tm, 7 * hw), _acc_out((1, d))],
        out_shape=[_sds((s_len, d), F32), _sds((s_len, 7 * hw), BF16), _sds((1, d), F32)],
        scratch=[pltpu.VMEM((hw // PAIR, tm, PAIR), F32)],
    )(dfq, dfk, dfv, dgx, *ddq, *ddk, *ddv, w_in_t, dx1, x, g_pre)


def _expand_w_in(w_in):
    hw = HEAD_WIDTH
    gate = jnp.repeat(w_in[:, 3 * hw:3 * hw + N_HEADS], HEAD_DIM, axis=1)
    return jnp.concatenate([w_in[:, :3 * hw], gate, w_in[:, 3 * hw + N_HEADS:]], axis=1)


def _local_step(x, mem, target, g, w_bf):
    hw = HEAD_WIDTH
    w_in_e = _expand_w_in(w_bf["w_in"])
    b_exp = jnp.repeat(g["b_f"], HEAD_DIM, axis=1)
    buckets = jnp.asarray(_dil_buckets())

    n_dil = len(DILATIONS)
    h1, fqkv, gx, *dqkv = _in_proj(x, g["g_mix_pre"], w_in_e)
    aq, ak, c = _gate_scan(gx, b_exp)
    stats = _fox_block_stats(fqkv, c)
    o_fox, lse_fox = _fox_fwd(fqkv, aq, ak, stats)
    bias = _dil_bias(g["rel_bias"], buckets)
    branches = [_dil_fwd(dqkv[p], bias, p) for p in range(n_dil)]
    x1, y1, o_dil, *lj = _mix_out(o_fox, [b[0] for b in branches], [b[1] for b in branches], w_bf["w_out"], x, g["g_mix_post"])
    hm, kx, vx = _mem_fwd(mem, g["g_mem"], w_bf["w_xk"], w_bf["w_xv"])
    x2, y2, h2, qx, ox = _xattn_fwd(x1, g["g_xattn_pre"], w_bf["w_xq"], kx, vx, w_bf["w_xo"], g["g_xattn_post"])
    h3, a, u, z = _ffn_up(x2, g["g_ffn_pre"], w_bf["w_gate"], w_bf["w_up"])
    y3, dx3, sq = _ffn_down_loss(z, w_bf["w_down"], x2, g["g_ffn_post"], target)

    grads = {}
    dy3, da, du, grads["g_ffn_post"] = _ffn_bwd_a(dx3, y3, g["g_ffn_post"], w_bf["w_down"].T, a, u)
    dx2, grads["g_ffn_pre"] = _ffn_bwd_b(da, du, w_bf["w_gate"].T, w_bf["w_up"].T, dx3, x2, g["g_ffn_pre"])
    grads["w_down"] = _weight_grad(z, dy3, "dw_down")
    grads["w_gate"] = _weight_grad(h3, da, "dw_gate")
    grads["w_up"] = _weight_grad(h3, du, "dw_up")
    dx1, dy2, dqx, dkx, dvx, grads["g_xattn_post"], grads["g_xattn_pre"] = _xattn_bwd(
        dx2, y2, g["g_xattn_post"], w_bf["w_xo"].T, qx, kx, vx, w_bf["w_xq"].T, x1, g["g_xattn_pre"])
    grads["w_xo"] = _weight_grad(ox, dy2, "dw_xo")
    grads["w_xq"] = _weight_grad(h2, dqx, "dw_xq")
    grads["w_xk"], grads["w_xv"], grads["g_mem"] = _mem_bwd(dkx, dvx, w_bf["w_xk"].T, w_bf["w_xv"].T, hm, mem, g["g_mem"])
    dy1, do_fox, delta_fox, grads["g_mix_post"], do_dil, delta_dil = _mix_out_bwd(
        dx1, y1, g["g_mix_post"], w_bf["w_out"].T, o_fox, o_dil)
    grads["w_out"] = jnp.concatenate([_weight_grad(o_fox, dy1, "dw_out_fox"), _weight_grad(o_dil, dy1, "dw_out_dil")], axis=0)
    aql, ad = _fox_bwd_prep(aq, lse_fox, delta_fox)
    dfq, rs, dfk, dfv, dc = _fox_bwd(fqkv, do_fox, aql, ak, ad, stats)
    dgx, db = _gate_bwd(rs, dc, gx, b_exp)
    grads["b_f"] = db[:, ::HEAD_DIM]
    dil = [_dil_bwd(dqkv[p], do_dil[p], lj[p], delta_dil[p], bias, p) for p in range(n_dil)]
    grads["rel_bias"] = _rel_bias_grad(jnp.stack([t[3] for t in dil]), buckets)[:, :N_HEADS]
    grad_x, dproj, grads["g_mix_pre"] = _in_proj_bwd(
        dfq, dfk, dfv, dgx, [t[0] for t in dil], [t[1] for t in dil], [t[2] for t in dil], w_in_e.T, dx1, x, g["g_mix_pre"])
    dw_in_e = _weight_grad(h1, dproj, "dw_in")
    grads["w_in"] = jnp.concatenate(
        [dw_in_e[:, :3 * hw], dw_in_e[:, 3 * hw:4 * hw:HEAD_DIM], dw_in_e[:, 4 * hw:]], axis=1)
    return sq, grad_x, grads


MESH = pl.DeviceIdType.MESH


def _my_place():
    return lax.axis_index("x"), lax.axis_index("y"), lax.axis_index("c")


def _all_gather(x, name):
    rows, lanes = x.shape

    def body(x_ref, out_ref, send_sems, recv_sems, local_sem):
        mx, my, mc = _my_place()
        me, sibling = (mx, my, mc), (mx, my, 1 - mc)
        chips = [(1 - mx, my), (mx, 1 - my), (1 - mx, 1 - my)]

        def slot(px, py, pc):
            return out_ref.at[4 * px + 2 * py + pc]

        def copy(k, block, to, src=None):
            return pltpu.make_async_remote_copy(
                src_ref=slot(*block) if src is None else src, dst_ref=slot(*block),
                send_sem=send_sems.at[k], recv_sem=recv_sems.at[k], device_id=to, device_id_type=MESH)

        mine = pltpu.make_async_copy(x_ref, slot(*me), local_sem)
        mine.start()
        first = [copy(0, me, sibling, src=x_ref)]
        first += [copy(1 + j, me, (*chip, mc), src=x_ref) for j, chip in enumerate(chips)]
        for cp in first:
            cp.start()
        passed = [copy(4 + j, (*chip, mc), sibling) for j, chip in enumerate(chips)]
        for j, chip in enumerate(chips):
            copy(1 + j, (*chip, mc), me).wait_recv()
            passed[j].start()
        copy(0, sibling, me).wait_recv()
        for j, chip in enumerate(chips):
            copy(4 + j, (*chip, 1 - mc), me).wait_recv()
        for cp in first + passed:
            cp.wait_send()
        mine.wait()

    return pl.pallas_call(
        body, name=name, out_shape=_sds((N_DEV, rows, lanes), x.dtype),
        in_specs=[pl.BlockSpec(memory_space=pl.ANY)], out_specs=pl.BlockSpec(memory_space=pl.ANY),
        scratch_shapes=[pltpu.SemaphoreType.DMA((N_DEV - 1,)), pltpu.SemaphoreType.DMA((N_DEV - 1,)), pltpu.SemaphoreType.DMA],
    )(x)


def _exchange(g, name):
    _, rows, lanes = g.shape

    def body(g_ref, land_ref, send_sems, recv_sems, local_sem):
        mx, my, mc = _my_place()
        me = 4 * mx + 2 * my + mc
        mine = pltpu.make_async_copy(g_ref.at[me], land_ref.at[me], local_sem)
        mine.start()
        sent, arriving = [], []
        for k in (1, 2, 3, 4, 5, 6, 7):
            px = 1 - mx if k & 4 else mx
            py = 1 - my if k & 2 else my
            pc = 1 - mc if k & 1 else mc
            peer = 4 * px + 2 * py + pc
            cp = pltpu.make_async_remote_copy(
                src_ref=g_ref.at[peer], dst_ref=land_ref.at[me], send_sem=send_sems.at[k - 1], recv_sem=recv_sems.at[k - 1],
                device_id=(px, py, pc), device_id_type=MESH)
            cp.start()
            sent.append(cp)
            arriving.append(pltpu.make_async_remote_copy(
                src_ref=g_ref.at[me], dst_ref=land_ref.at[peer], send_sem=send_sems.at[k - 1], recv_sem=recv_sems.at[k - 1],
                device_id=(px, py, pc), device_id_type=MESH))
        for cp in arriving:
            cp.wait_recv()
        for cp in sent:
            cp.wait_send()
        mine.wait()

    return pl.pallas_call(
        body, name=name, out_shape=_sds(g.shape, g.dtype),
        in_specs=[pl.BlockSpec(memory_space=pl.ANY)], out_specs=pl.BlockSpec(memory_space=pl.ANY),
        scratch_shapes=[pltpu.SemaphoreType.DMA((N_DEV - 1,)), pltpu.SemaphoreType.DMA((N_DEV - 1,)), pltpu.SemaphoreType.DMA],
    )(g)


def _sum_slots(parts, name):
    n, rows, lanes = parts.shape
    tr = 512 if rows % 512 == 0 else rows

    def body(p_ref, o_ref):
        acc = p_ref[0].astype(F32)
        for j in range(1, n):
            acc = acc + p_ref[j].astype(F32)
        o_ref[...] = acc

    return _call(
        body, name=name, grid=(rows // tr,),
        in_specs=[pl.BlockSpec((n, tr, lanes), lambda i: (0, i, 0))], out_specs=_rows(tr, lanes),
        out_shape=_sds((rows, lanes), F32),
    )(parts)


def _adamw(w, g, m, v, name):
    def body(w_ref, g_ref, m_ref, v_ref, d_ref, nm_ref, nv_ref):
        gv = g_ref[...]
        m_new = ADAM_B1 * m_ref[...] + (1.0 - ADAM_B1) * gv
        v_new = ADAM_B2 * v_ref[...] + (1.0 - ADAM_B2) * (gv * gv)
        nm_ref[...] = m_new
        nv_ref[...] = v_new
        m_hat = m_new / (1.0 - ADAM_B1 ** ADAM_STEP)
        v_hat = v_new / (1.0 - ADAM_B2 ** ADAM_STEP)
        d_ref[...] = -ADAM_LR * (m_hat / (jnp.sqrt(v_hat) + ADAM_EPS) + ADAM_WD * w_ref[...])

    out = _sds(w.shape, F32)
    return pl.pallas_call(
        body, name=name, out_shape=[out, out, out],
        compiler_params=pltpu.CompilerParams(vmem_limit_bytes=V7X_VMEM_LIMIT_BYTES),
    )(w, g, m, v)


def _loss_head(sq, d_model):
    def body(sq_ref, o_ref):
        tot = jnp.sum(jnp.sum(sq_ref[...], axis=1, keepdims=True), axis=0, keepdims=True)
        o_ref[...] = 0.5 * (tot / d_model)

    return pl.pallas_call(body, name="loss_head", out_shape=_sds((1, 1), F32))(sq)


_BIG = (("w_in", 1), ("w_out", 0), ("w_xq", 0), ("w_xk", 0), ("w_xv", 0), ("w_xo", 1), ("w_gate", 1), ("w_up", 1), ("w_down", 0))
_SMALL = ("g_mix_pre", "b_f", "rel_bias", "g_mix_post", "g_xattn_pre", "g_mem", "g_xattn_post", "g_ffn_pre", "g_ffn_post")
LANES = 128
BIG_ROW_ALIGN = 512


def _round_up(n, k):
    return -(-n // k) * k


def _pack_rows(flat_parts, row_align, dtype):
    starts, rows, padded = [], 0, []
    for p in flat_parts:
        n = _round_up(p.shape[0], LANES)
        starts.append(rows)
        rows += n // LANES
        padded.append(jnp.pad(p.astype(dtype), (0, n - p.shape[0])))
    total = _round_up(rows, row_align)
    padded.append(jnp.zeros(((total - rows) * LANES,), dtype))
    return jnp.concatenate(padded).reshape(total, LANES), starts


def _unpack_rows(buf, starts, shapes):
    lead = buf.shape[:-2]
    flat = buf.reshape(lead + (-1,))
    out = []
    for st, shp in zip(starts, shapes):
        n = int(np.prod(shp))
        out.append(flat[..., st * LANES:st * LANES + n].reshape(lead + tuple(shp)))
    return out


def kernel(x, mem, g_mix_pre, w_in, b_f, rel_bias, w_out, g_mix_post, g_xattn_pre, g_mem, w_xq, w_xk, w_xv, w_xo, g_xattn_post, g_ffn_pre, w_gate, w_up, w_down, g_ffn_post, loss_target, m_g_mix_pre, m_w_in, m_b_f, m_rel_bias, m_w_out, m_g_mix_post, m_g_xattn_pre, m_g_mem, m_w_xq, m_w_xk, m_w_xv, m_w_xo, m_g_xattn_post, m_g_ffn_pre, m_w_gate, m_w_up, m_w_down, m_g_ffn_post, v_g_mix_pre, v_w_in, v_b_f, v_rel_bias, v_w_out, v_g_mix_post, v_g_xattn_pre, v_g_mem, v_w_xq, v_w_xk, v_w_xv, v_w_xo, v_g_xattn_post, v_g_ffn_pre, v_w_gate, v_w_up, v_w_down, v_g_ffn_post):
    given = dict(locals())
    order = ("g_mix_pre", "w_in", "b_f", "rel_bias", "w_out", "g_mix_post", "g_xattn_pre", "g_mem", "w_xq", "w_xk", "w_xv",
             "w_xo", "g_xattn_post", "g_ffn_pre", "w_gate", "w_up", "w_down", "g_ffn_post")
    two_d = lambda a: a.reshape(a.shape[-2:])
    w_loc = {n: two_d(given[n]) for n in order}
    m_loc = {n: two_d(given["m_" + n]) for n in order}
    v_loc = {n: two_d(given["v_" + n]) for n in order}
    d_model = x.shape[-1]

    shard_shapes = [w_loc[n].shape for n, _ in _BIG]
    packed, starts = _pack_rows([w_loc[n].reshape(-1) for n, _ in _BIG], BIG_ROW_ALIGN, BF16)
    gathered = _all_gather(packed, "gather_weights")
    w_bf = {}
    for (n, axis), part in zip(_BIG, _unpack_rows(gathered, starts, shard_shapes)):
        r, c = part.shape[1:]
        w_bf[n] = part.reshape(N_DEV * r, c) if axis == 0 else part.transpose(1, 0, 2).reshape(r, N_DEV * c)

    small = {n: w_loc[n] for n in _SMALL}
    sq, grad_x, grads = _local_step(two_d(x), two_d(mem), two_d(loss_target), small, w_bf)

    per_owner = []
    for (n, axis), shp in zip(_BIG, shard_shapes):
        r, c = shp
        gfull = grads[n]
        per_owner.append(gfull.reshape(N_DEV, r * c) if axis == 0 else gfull.reshape(r, N_DEV, c).transpose(1, 0, 2).reshape(N_DEV, r * c))
    rows_big = packed.shape[0]
    slots = []
    for j in range(N_DEV):
        buf, _ = _pack_rows([p[j] for p in per_owner], BIG_ROW_ALIGN, BF16)
        slots.append(buf)
    landed = _exchange(jnp.stack(slots), "exchange_grads")
    g_big = dict(zip([n for n, _ in _BIG], _unpack_rows(_sum_slots(landed, "sum_grads"), starts, shard_shapes)))
    assert landed.shape[1] == rows_big

    small_parts = [grads[n].reshape(-1) for n in _SMALL] + [sq.reshape(-1)]
    small_shapes = [w_loc[n].shape for n in _SMALL] + [sq.shape]
    spacked, sstarts = _pack_rows(small_parts, 8, F32)
    ssum = _sum_slots(_all_gather(spacked, "gather_small"), "sum_small")
    g_small = dict(zip(_SMALL, _unpack_rows(ssum, sstarts, small_shapes)[:-1]))
    sq_rows = sq.size // LANES
    loss = _loss_head(ssum[sstarts[-1]:sstarts[-1] + sq_rows], d_model).reshape(())

    g_loc, delta, new_m, new_v = {}, {}, {}, {}
    for n, _ in _BIG:
        g_loc[n] = g_big[n]
        delta[n], new_m[n], new_v[n] = _adamw(w_loc[n], g_big[n], m_loc[n], v_loc[n], "adamw_" + n)
    pk = lambda d: _pack_rows([d[n].reshape(-1) for n in _SMALL], 8, F32)[0]
    pstarts = _pack_rows([w_loc[n].reshape(-1) for n in _SMALL], 8, F32)[1]
    d_s, m_s, v_s = _adamw(pk(w_loc), pk(g_small), pk(m_loc), pk(v_loc), "adamw_small")
    shapes_s = [w_loc[n].shape for n in _SMALL]
    for n, dd, mm, vv in zip(_SMALL, _unpack_rows(d_s, pstarts, shapes_s), _unpack_rows(m_s, pstarts, shapes_s),
                             _unpack_rows(v_s, pstarts, shapes_s)):
        g_loc[n], delta[n], new_m[n], new_v[n] = g_small[n], dd, mm, vv

    like = lambda d: [d[n].reshape(given[n].shape) for n in order]
    return (loss, grad_x.reshape(x.shape), *like(g_loc), *like(delta), *like(new_m), *like(new_v))
```

```python
import functools

import numpy as np
import jax
import jax.numpy as jnp
from jax import lax
from jax.experimental import pallas as pl
from jax.experimental.pallas import tpu as pltpu

F32 = jnp.float32
BF16 = jnp.bfloat16
HIGHEST = lax.Precision.HIGHEST

RMS_EPS = 1e-6
HEAD_DIM = 64
N_HEADS = 8
HEAD_WIDTH = N_HEADS * HEAD_DIM
PAIR = 2 * HEAD_DIM
N_PAIRS = N_HEADS // 2
DIL_BLOCK = 128
DILATIONS = (1, 4, 16)
N_BUCKETS = 32
MAX_DISTANCE = 2048
N_MEM_HEADS = 4
QK_SCALE = HEAD_DIM ** -0.5
NEG = -1e30
FOX_SKIP_MARGIN = 110.0
FOX_SHORT_SPANS = (4, 8, 16)
N_DEV = 8

ADAM_LR = 0.001
ADAM_B1 = 0.9
ADAM_B2 = 0.999
ADAM_EPS = 1e-08
ADAM_WD = 0.01
ADAM_STEP = 10

V7X_VMEM_LIMIT_BYTES = 56 * 2 ** 20
ROW_TILE = 256
ATT_BLOCK = 512
SCAN_TILE = 256


def _call(body, *, name, grid, in_specs, out_specs, out_shape, scratch=()):
    return pl.pallas_call(
        body, name=name, grid=grid, in_specs=in_specs, out_specs=out_specs, out_shape=out_shape,
        scratch_shapes=list(scratch),
        compiler_params=pltpu.CompilerParams(
            dimension_semantics=("arbitrary",) * len(grid), vmem_limit_bytes=V7X_VMEM_LIMIT_BYTES))


def _rows(tm, n):
    return pl.BlockSpec((tm, n), lambda i: (i, 0))


def _resident(shape):
    zeros = (0,) * len(shape)
    return pl.BlockSpec(shape, lambda i: zeros, pipeline_mode=pl.Buffered(1))


def _acc_out(shape):
    zeros = (0,) * len(shape)
    return pl.BlockSpec(shape, lambda i: zeros)


def _sds(shape, dtype):
    return jax.ShapeDtypeStruct(shape, dtype)


def _dot(a, b):
    return jnp.dot(a, b, preferred_element_type=F32)


def _dot_nt(a, b):
    return lax.dot_general(a, b, (((1,), (1,)), ((), ())), preferred_element_type=F32)


def _dot_tn(a, b):
    return lax.dot_general(a, b, (((0,), (0,)), ((), ())), preferred_element_type=F32)


def _rms_fwd(x, g):
    r = lax.rsqrt(jnp.mean(x * x, axis=-1, keepdims=True) + RMS_EPS)
    return (x * r) * g


def _rms_bwd(xin, g, dy):
    r = lax.rsqrt(jnp.mean(xin * xin, axis=-1, keepdims=True) + RMS_EPS)
    xhat = xin * r
    dg = jnp.sum(dy * xhat, axis=0, keepdims=True)
    dxh = dy * g
    dx = r * (dxh - xhat * jnp.mean(dxh * xhat, axis=-1, keepdims=True))
    return dx, dg


def _first_head_lanes():
    return lax.broadcasted_iota(jnp.int32, (1, PAIR), 1) < HEAD_DIM


def _pick(mask, a, b):
    return jnp.where(mask, a, b)


def _zero_other(mask, v):
    return jnp.where(mask, v, jnp.zeros_like(v))


def _store_lane_blocks(buf_ref, val):
    for cb in range(buf_ref.shape[0]):
        buf_ref[cb] = val[:, cb * PAIR:(cb + 1) * PAIR].astype(F32)


def _load_lane_blocks(buf_ref):
    return jnp.concatenate([buf_ref[cb] for cb in range(buf_ref.shape[0])], axis=1)


def _write_class_major(buf_ref, out_ref, dil):
    n, tile, _ = buf_ref.shape
    for r in range(dil):
        for cb in range(n):
            col = (r * n + cb) * PAIR
            out_ref[:, col:col + PAIR] = buf_ref.at[cb][pl.ds(r, tile // dil, stride=dil), :].astype(out_ref.dtype)


def _read_class_major(in_ref, buf_ref, dil):
    n, tile, _ = buf_ref.shape
    for r in range(dil):
        for cb in range(n):
            col = (r * n + cb) * PAIR
            buf_ref.at[cb][pl.ds(r, tile // dil, stride=dil), :] = in_ref[:, col:col + PAIR].astype(F32)
    return _load_lane_blocks(buf_ref)


def _class_rows(tm, width, dil):
    return _rows(tm // dil, dil * width)


def _in_proj(x, g, w):
    s_len, d = x.shape
    tm = ROW_TILE
    hw = HEAD_WIDTH

    def body(x_ref, g_ref, w_ref, h_ref, fqkv_ref, gx_ref, *rest):
        dil_refs, buf = rest[:-1], rest[-1]
        h = _rms_fwd(x_ref[...], g_ref[...]).astype(BF16)
        h_ref[...] = h
        proj = _dot(h, w_ref[...])
        fqkv_ref[:, 0:hw] = (proj[:, 0:hw] * QK_SCALE).astype(BF16)
        fqkv_ref[:, hw:3 * hw] = proj[:, hw:3 * hw].astype(BF16)
        gx_ref[...] = proj[:, 3 * hw:4 * hw]
        dqkv = jnp.concatenate([proj[:, 4 * hw:5 * hw] * QK_SCALE, proj[:, 5 * hw:7 * hw]], axis=1)
        _store_lane_blocks(buf, dqkv)
        for ref, dil in zip(dil_refs, DILATIONS):
            if dil == 1:
                ref[...] = dqkv.astype(BF16)
            else:
                _write_class_major(buf, ref, dil)

    return _call(
        body, name="in_proj", grid=(s_len // tm,),
        in_specs=[_rows(tm, d), _resident((1, d)), _resident(w.shape)],
        out_specs=[_rows(tm, d), _rows(tm, 3 * hw), _rows(tm, hw)] + [_class_rows(tm, 3 * hw, dil) for dil in DILATIONS],
        out_shape=[_sds((s_len, d), BF16), _sds((s_len, 3 * hw), BF16), _sds((s_len, hw), F32)]
        + [_sds((s_len // dil, dil * 3 * hw), BF16) for dil in DILATIONS],
        scratch=[pltpu.VMEM((3 * hw // PAIR, tm, PAIR), F32)],
    )(x, g, w)


def _swap_halves(x):
    return jnp.concatenate([pltpu.roll(x[:, i * PAIR:(i + 1) * PAIR], HEAD_DIM, 1) for i in range(x.shape[1] // PAIR)], axis=1)


def _split3(x):
    hi = x.astype(BF16)
    r = x - hi.astype(F32)
    mid = r.astype(BF16)
    lo = (r - mid.astype(F32)).astype(BF16)
    return hi, mid, lo


def _lane_in_head(width):
    return lax.broadcasted_iota(jnp.int32, (1, width), 1) % HEAD_DIM


def _place3(jj, first, pieces, base):
    out = base
    for i, p in enumerate(pieces):
        out = jnp.where(jj == first + i, p, out)
    return out


def _gate_scan(gx, b_exp):
    s_len, hw = gx.shape
    t = min(SCAN_TILE, s_len)
    tri = jnp.asarray(np.tril(np.ones((t, t), np.float32)))

    def body(gx_ref, b_ref, tri_ref, aq_ref, ak_ref, c_ref, carry):
        @pl.when(pl.program_id(0) == 0)
        def _():
            carry[...] = jnp.zeros_like(carry)

        z = gx_ref[...] + b_ref[...]
        lf = jnp.minimum(z, 0.0) - jnp.log1p(jnp.exp(-jnp.abs(z)))
        c = jnp.dot(tri_ref[...], lf, precision=HIGHEST, preferred_element_type=F32) + carry[...]
        carry[...] = c[t - 1:t, :]
        c_ref[...] = c
        hi, mid, lo = _split3(_swap_halves(c))
        jj = _lane_in_head(hw)
        zero = jnp.zeros_like(hi)
        one = jnp.ones_like(hi)
        aq_ref[...] = _place3(jj, 0, (hi, mid, lo), jnp.where(jj < 6, one, zero))
        ak_ref[...] = _place3(jj, 3, (-hi, -mid, -lo), jnp.where(jj < 9, one, zero))

    return _call(
        body, name="gate_scan", grid=(s_len // t,),
        in_specs=[_rows(t, hw), _resident((1, hw)), _resident((t, t))],
        out_specs=[_rows(t, hw), _rows(t, hw), _rows(t, hw)],
        out_shape=[_sds((s_len, hw), BF16), _sds((s_len, hw), BF16), _sds((s_len, hw), F32)],
        scratch=[pltpu.VMEM((1, hw), F32)],
    )(gx, b_exp, tri)


def _head_block_ones():
    head_of = np.arange(HEAD_WIDTH) // HEAD_DIM
    return jnp.asarray((head_of[:, None] == head_of[None, :]).astype(np.float32))


def _fox_block_stats(fqkv, c):
    s_len = fqkv.shape[0]
    hw = HEAD_WIDTH
    b = min(ATT_BLOCK, s_len)
    nb = s_len // b

    def body(q_ref, k_ref, c_ref, ones_ref, o_ref):
        q, k, cv = q_ref[...].astype(F32), k_ref[...].astype(F32), c_ref[...]
        seg = lambda x: _dot(x.astype(BF16), ones_ref[...])
        col_max = lambda x: jnp.max(x, axis=0, keepdims=True)
        col_min = lambda x: jnp.min(x, axis=0, keepdims=True)
        qn = 1.01 * jnp.sqrt(col_max(seg(q * q)))
        kn = 1.01 * jnp.sqrt(col_max(seg(k * k)))
        dmin = col_min(seg(q * k)) - (2.0 ** -8) * qn * kn
        o_ref[0] = jnp.concatenate([qn, col_max(cv) - dmin, kn, col_min(cv), jnp.zeros((4, hw), F32)], axis=0)

    stats = _call(
        body, name="fox_block_stats", grid=(nb,),
        in_specs=[pl.BlockSpec((b, hw), lambda i: (i, 0)), pl.BlockSpec((b, hw), lambda i: (i, 1)), _rows(b, hw),
                  _resident((hw, hw))],
        out_specs=pl.BlockSpec((1, 8, hw), lambda i: (i, 0, 0)),
        out_shape=_sds((nb, 8, hw), F32),
    )(fqkv, fqkv, c, _head_block_ones().astype(BF16))
    st = jnp.transpose(stats[:, :4, ::HEAD_DIM], (1, 2, 0))
    bound = st[0][:, :, None] * st[2][:, None, :] + st[1][:, :, None] - st[3][:, None, :]
    need_h = jnp.logical_not(bound < -FOX_SKIP_MARGIN)
    need = jnp.logical_or(need_h[0::2], need_h[1::2])
    ii = lax.broadcasted_iota(jnp.int32, (1, nb, nb), 1)
    jj = lax.broadcasted_iota(jnp.int32, (1, nb, nb), 2)
    first_needed = jnp.min(jnp.where(jnp.logical_or(jnp.logical_and(need, jj < ii), jj == ii), jj, nb), axis=2)
    window = ii[:, :, 0] - first_needed + 1
    in_window = jnp.logical_and(jj >= first_needed[:, :, None], jj <= ii)
    last_query = jnp.max(jnp.where(in_window, ii, 0), axis=1)
    return jnp.concatenate([st.reshape(4 * N_HEADS, nb), window.astype(F32), last_query.astype(F32)], axis=0)


FOX_WINDOW_ROW = 4 * N_HEADS
FOX_LAST_QUERY_ROW = 4 * N_HEADS + N_PAIRS


def _head_needed(st_ref, h, i, j):
    bound = st_ref[h, i] * st_ref[2 * N_HEADS + h, j] + st_ref[N_HEADS + h, i] - st_ref[3 * N_HEADS + h, j]
    return jnp.logical_not(bound < -FOX_SKIP_MARGIN)


def _lower_triangle(n):
    return lax.broadcasted_iota(jnp.int32, (n, n), 1) <= lax.broadcasted_iota(jnp.int32, (n, n), 0)


def _fox_spans(s_len):
    nq = s_len // min(ATT_BLOCK, s_len)
    return [s for s in FOX_SHORT_SPANS if s < nq] + [nq]


def _fox_span_index(stats, s_len):
    longest = jnp.max(stats[FOX_WINDOW_ROW:FOX_WINDOW_ROW + N_PAIRS])
    idx = jnp.int32(0)
    for s in _fox_spans(s_len)[:-1]:
        idx = idx + (longest > s).astype(jnp.int32)
    return idx


def _fox_fwd(span, fqkv, aq, ak, stats):
    s_len = fqkv.shape[0]
    bq = bk = min(ATT_BLOCK, s_len)
    nq = s_len // bq

    def body(st_ref, q_ref, k_ref, v_ref, aq_ref, ak_ref, o_ref, lse_ref, qa_ref, m_ref, acc_ref):
        pair, qi, back = pl.program_id(0), pl.program_id(1), pl.program_id(2)
        kj = jnp.maximum(qi - back, 0)
        in0 = _first_head_lanes()
        not0 = jnp.logical_not(in0)

        @pl.when(back == 0)
        def _():
            q2, a2 = q_ref[...], aq_ref[...]
            qa_ref[0] = jnp.where(in0, q2, a2)
            qa_ref[1] = jnp.where(in0, a2, q2)
            m_ref[...] = jnp.full_like(m_ref, NEG)
            acc_ref[...] = jnp.zeros_like(acc_ref)

        def head_step(a, mine, masked):
            k2, v2 = k_ref[...], v_ref[...]
            s = _dot_nt(qa_ref[a], jnp.where(mine, k2, ak_ref[...]))
            if masked:
                s = jnp.where(_lower_triangle(bq), s, NEG)
            m_old = m_ref[a]
            m_new = jnp.maximum(m_old, jnp.max(s, axis=1, keepdims=True))
            p = jnp.exp(s - jnp.tile(m_new, (1, bk // PAIR))).astype(BF16)
            acc_ref[a] = jnp.exp(m_old - m_new) * acc_ref[a] + _dot(p, jnp.where(mine, v2, jnp.ones_like(v2)))
            m_ref[a] = m_new

        @pl.when(back == 0)
        def _():
            head_step(0, in0, True)
            head_step(1, not0, True)

        below = jnp.logical_and(back > 0, back <= qi)
        for a, mine in ((0, in0), (1, not0)):
            @pl.when(jnp.logical_and(below, _head_needed(st_ref, 2 * pair + a, qi, kj)))
            def _():
                head_step(a, mine, False)

        @pl.when(back == jnp.minimum(qi, span - 1))
        def _():
            acc0, acc1 = acc_ref[0], acc_ref[1]
            l2 = pltpu.roll(_pick(in0, acc1, acc0), HEAD_DIM, 1)
            o_ref[...] = _pick(in0, acc0, acc1) / l2
            lse_ref[...] = _pick(in0, m_ref[0], m_ref[1]) + jnp.log(l2)

    blk = lambda rows: (rows, PAIR)
    qmap = lambda p, i, b, st: (i, p)

    def key_block(p, i, b, st):
        window = jnp.clip(st[FOX_WINDOW_ROW + p, i].astype(jnp.int32), 1, i + 1)
        return i - jnp.minimum(b, window - 1)

    return pl.pallas_call(
        body, name=f"fox_fwd_{span}",
        grid_spec=pltpu.PrefetchScalarGridSpec(
            num_scalar_prefetch=1, grid=(N_PAIRS, nq, span),
            in_specs=[
                pl.BlockSpec(blk(bq), qmap),
                pl.BlockSpec(blk(bk), lambda p, i, b, st: (key_block(p, i, b, st), N_PAIRS + p)),
                pl.BlockSpec(blk(bk), lambda p, i, b, st: (key_block(p, i, b, st), 2 * N_PAIRS + p)),
                pl.BlockSpec(blk(bq), qmap),
                pl.BlockSpec(blk(bk), lambda p, i, b, st: (key_block(p, i, b, st), p)),
            ],
            out_specs=[pl.BlockSpec(blk(bq), qmap), pl.BlockSpec(blk(bq), qmap)],
            scratch_shapes=[pltpu.VMEM((2, bq, PAIR), BF16), pltpu.VMEM((2, bq, PAIR), F32), pltpu.VMEM((2, bq, PAIR), F32)]),
        out_shape=[_sds((s_len, HEAD_WIDTH), F32), _sds((s_len, HEAD_WIDTH), F32)],
        compiler_params=pltpu.CompilerParams(dimension_semantics=("arbitrary",) * 3, vmem_limit_bytes=V7X_VMEM_LIMIT_BYTES),
    )(stats, fqkv, fqkv, fqkv, aq, ak)


def _t5_bucket(dist):
    max_exact = N_BUCKETS // 2
    d = np.maximum(dist, 1).astype(np.float32)
    large = max_exact + (np.log(d / max_exact) / np.log(MAX_DISTANCE / max_exact) * (N_BUCKETS - max_exact)).astype(np.int32)
    large = np.minimum(large, N_BUCKETS - 1)
    return np.where(dist < max_exact, dist, large).astype(np.int32)


def _dil_buckets():
    w = DIL_BLOCK
    qi = np.arange(w)[:, None]
    kj = np.arange(2 * w)[None, :]
    sub = qi + w - kj
    band = (sub >= 0) & (sub <= w)
    out = [np.where(band, _t5_bucket(np.clip(sub, 0, w) * dil), -1) for dil in DILATIONS]
    return np.stack(out).astype(np.int32)


def _dil_bias(rel_bias, buckets):
    w = DIL_BLOCK

    def body(rb_ref, bk_ref, o_ref):
        for p in range(len(DILATIONS)):
            bk = bk_ref[p]
            for h in range(N_HEADS):
                def add(b, acc):
                    return acc + jnp.where(bk == b, rb_ref[b, h], 0.0)
                acc = lax.fori_loop(0, N_BUCKETS, add, jnp.zeros((w, 2 * w), F32))
                o_ref[p, h] = jnp.where(bk < 0, NEG, acc)

    return pl.pallas_call(
        body, name="dil_bias",
        in_specs=[pl.BlockSpec(memory_space=pltpu.SMEM), pl.BlockSpec(memory_space=pltpu.VMEM)],
        out_specs=pl.BlockSpec(memory_space=pltpu.VMEM),
        out_shape=_sds((len(DILATIONS), N_HEADS, w, 2 * w), F32),
    )(rel_bias, buckets)


def _dil_fwd(view, bias, branch):
    dil = DILATIONS[branch]
    w = DIL_BLOCK
    hw = HEAD_WIDTH
    length = view.shape[0]
    nb = length // w

    def body(q_ref, kc_ref, kp_ref, vc_ref, vp_ref, b_ref, o_ref, lse_ref):
        n = pl.program_id(1)
        in0 = _first_head_lanes()
        not0 = jnp.logical_not(in0)
        pairs = [slice(pr * PAIR, (pr + 1) * PAIR) for pr in range(N_PAIRS)]
        tiles = []
        for sl in pairs:
            q2 = q_ref[:, sl]
            qq = jnp.concatenate([_zero_other(in0, q2), _zero_other(not0, q2)], axis=0)
            tiles.append(jnp.concatenate([_dot_nt(qq, kp_ref[:, sl]), _dot_nt(qq, kc_ref[:, sl])], axis=1))
        s = jnp.concatenate(tiles, axis=0) + b_ref[...].reshape(N_HEADS * w, 2 * w)
        prev_half = lax.broadcasted_iota(jnp.int32, (1, 2 * w), 1) < w
        s = jnp.where(jnp.logical_and(n == 0, prev_half), NEG, s)
        m = jnp.max(s, axis=1, keepdims=True)
        e = jnp.exp(s - m)
        l = jnp.sum(e, axis=1, keepdims=True)
        p = (e / l).astype(BF16)
        lse = m + jnp.log(l)
        for pr, sl in enumerate(pairs):
            pp = p[2 * pr * w:(2 * pr + 2) * w]
            o2 = _dot(pp[:, :w], vp_ref[:, sl]) + _dot(pp[:, w:], vc_ref[:, sl])
            o_ref[:, sl] = _pick(in0, o2[:w], o2[w:])
            lse_ref[:, sl] = _pick(in0, lse[2 * pr * w:(2 * pr + 1) * w], lse[(2 * pr + 1) * w:(2 * pr + 2) * w])

    prev = lambda n: jnp.maximum(n - 1, 0)
    out = pl.pallas_call(
        body, name=f"dil_fwd_{dil}", grid=(dil, nb),
        in_specs=[
            pl.BlockSpec((w, hw), lambda r, n: (n, 3 * r)),
            pl.BlockSpec((w, hw), lambda r, n: (n, 3 * r + 1)),
            pl.BlockSpec((w, hw), lambda r, n: (prev(n), 3 * r + 1)),
            pl.BlockSpec((w, hw), lambda r, n: (n, 3 * r + 2)),
            pl.BlockSpec((w, hw), lambda r, n: (prev(n), 3 * r + 2)),
            pl.BlockSpec((None, N_HEADS, w, 2 * w), lambda r, n: (branch, 0, 0, 0)),
        ],
        out_specs=[pl.BlockSpec((w, hw), lambda r, n: (n, r)), pl.BlockSpec((w, hw), lambda r, n: (n, r))],
        out_shape=[_sds((length, dil * hw), F32), _sds((length, dil * hw), F32)],
        compiler_params=pltpu.CompilerParams(dimension_semantics=("arbitrary",) * 2, vmem_limit_bytes=V7X_VMEM_LIMIT_BYTES),
    )(view, view, view, view, view, bias)
    return out[0], out[1]


def _mix_out(o_fox, o_br, lse_br, w_out, x, g_post):
    s_len, d = x.shape
    hw = HEAD_WIDTH
    tm = ROW_TILE

    def body(of_ref, o1, o2, o3, l1, l2, l3, w_ref, x_ref, g_ref, x1_ref, y1_ref, od_ref, lj1, lj2, lj3, buf):
        def natural(ref, dil):
            return ref[...] if dil == 1 else _read_class_major(ref, buf, dil)

        ob = [natural(r, dil) for r, dil in zip((o1, o2, o3), DILATIONS)]
        la, lb, lc = [natural(r, dil) for r, dil in zip((l1, l2, l3), DILATIONS)]
        m = jnp.maximum(jnp.maximum(la, lb), lc)
        ea, eb, ec = jnp.exp(la - m), jnp.exp(lb - m), jnp.exp(lc - m)
        tot = ea + eb + ec
        o_dil = (ea / tot) * ob[0] + (eb / tot) * ob[1] + (ec / tot) * ob[2]
        od_ref[...] = o_dil
        lj = m + jnp.log(tot)
        _store_lane_blocks(buf, lj)
        for ref, dil in zip((lj1, lj2, lj3), DILATIONS):
            if dil == 1:
                ref[...] = lj
            else:
                _write_class_major(buf, ref, dil)
        y = _dot(of_ref[...].astype(BF16), w_ref[0:hw, :]) + _dot(o_dil.astype(BF16), w_ref[hw:2 * hw, :])
        y1_ref[...] = y
        x1_ref[...] = x_ref[...] + _rms_fwd(y, g_ref[...])

    half = _rows(tm, hw)
    by_dil = [_class_rows(tm, hw, dil) for dil in DILATIONS]
    return _call(
        body, name="mix_out", grid=(s_len // tm,),
        in_specs=[half] + by_dil + by_dil + [_resident(w_out.shape), _rows(tm, d), _resident((1, d))],
        out_specs=[_rows(tm, d), _rows(tm, d), half] + by_dil,
        out_shape=[_sds((s_len, d), F32), _sds((s_len, d), F32), _sds((s_len, hw), F32)]
        + [_sds((s_len // dil, dil * hw), F32) for dil in DILATIONS],
        scratch=[pltpu.VMEM((hw // PAIR, tm, PAIR), F32)],
    )(o_fox, *o_br, *lse_br, w_out, x, g_post)


def _mem_fwd(mem, g_mem, w_xk, w_xv):
    n_mem, d = mem.shape
    mw = w_xk.shape[1]

    def body(mem_ref, g_ref, wk_ref, wv_ref, hm_ref, k_ref, v_ref):
        hm = _rms_fwd(mem_ref[...], g_ref[...]).astype(BF16)
        hm_ref[...] = hm
        k_ref[...] = _dot(hm, wk_ref[...]).astype(BF16)
        v_ref[...] = _dot(hm, wv_ref[...]).astype(BF16)

    return pl.pallas_call(
        body, name="mem_fwd",
        out_shape=[_sds((n_mem, d), BF16), _sds((n_mem, mw), BF16), _sds((n_mem, mw), BF16)],
    )(mem, g_mem, w_xk, w_xv)


def _xattn_softmax(qa, k2):
    s = _dot_nt(qa, k2)
    m = jnp.max(s, axis=1, keepdims=True)
    e = jnp.exp(s - m)
    return e / jnp.sum(e, axis=1, keepdims=True)


def _xattn_fwd(x1, g_pre, w_xq, kx, vx, w_xo, g_post):
    s_len, d = x1.shape
    mw = w_xq.shape[1]
    n_mem = kx.shape[0]
    tm = ROW_TILE

    def body(x_ref, gp_ref, wq_ref, k_ref, v_ref, wo_ref, go_ref, x2_ref, y2_ref, h2_ref, q_ref, o_ref):
        x = x_ref[...]
        h = _rms_fwd(x, gp_ref[...]).astype(BF16)
        h2_ref[...] = h
        q = (_dot(h, wq_ref[...]) * QK_SCALE).astype(BF16)
        q_ref[...] = q
        in0 = _first_head_lanes()
        not0 = jnp.logical_not(in0)
        for pr in range(mw // PAIR):
            sl = slice(pr * PAIR, (pr + 1) * PAIR)
            q2, k2, v2 = q[:, sl], k_ref[:, sl], v_ref[:, sl]
            oa = [_dot(_xattn_softmax(_zero_other(mine, q2), k2).astype(BF16), v2) for mine in (in0, not0)]
            o_ref[:, sl] = _pick(in0, oa[0], oa[1]).astype(BF16)
        y = _dot(o_ref[...], wo_ref[...])
        y2_ref[...] = y
        x2_ref[...] = x + _rms_fwd(y, go_ref[...])

    return _call(
        body, name="xattn_fwd", grid=(s_len // tm,),
        in_specs=[_rows(tm, d), _resident((1, d)), _resident(w_xq.shape), _resident((n_mem, mw)), _resident((n_mem, mw)),
                  _resident(w_xo.shape), _resident((1, d))],
        out_specs=[_rows(tm, d), _rows(tm, d), _rows(tm, d), _rows(tm, mw), _rows(tm, mw)],
        out_shape=[_sds((s_len, d), F32), _sds((s_len, d), F32), _sds((s_len, d), BF16), _sds((s_len, mw), BF16),
                   _sds((s_len, mw), BF16)],
    )(x1, g_pre, w_xq, kx, vx, w_xo, g_post)


def _ffn_up(x2, g_pre, w_gate, w_up):
    s_len, d = x2.shape
    dff = w_gate.shape[1]
    tm = ROW_TILE

    def body(x_ref, g_ref, wg_ref, wu_ref, h_ref, a_ref, u_ref, z_ref):
        h = _rms_fwd(x_ref[...], g_ref[...]).astype(BF16)
        h_ref[...] = h
        a = _dot(h, wg_ref[...])
        u = _dot(h, wu_ref[...])
        a_ref[...] = a.astype(BF16)
        u_ref[...] = u.astype(BF16)
        z_ref[...] = ((a * jax.nn.sigmoid(a)) * u).astype(BF16)

    return _call(
        body, name="ffn_up", grid=(s_len // tm,),
        in_specs=[_rows(tm, d), _resident((1, d)), _resident(w_gate.shape), _resident(w_up.shape)],
        out_specs=[_rows(tm, d), _rows(tm, dff), _rows(tm, dff), _rows(tm, dff)],
        out_shape=[_sds((s_len, d), BF16)] + [_sds((s_len, dff), BF16)] * 3,
    )(x2, g_pre, w_gate, w_up)


def _ffn_down_loss(z, w_down, x2, g_post, target):
    s_len, d = x2.shape
    dff = z.shape[1]
    tm = ROW_TILE

    def body(z_ref, w_ref, x_ref, g_ref, t_ref, y_ref, dx_ref, sq_ref):
        @pl.when(pl.program_id(0) == 0)
        def _():
            sq_ref[...] = jnp.zeros_like(sq_ref)

        y = _dot(z_ref[...], w_ref[...])
        y_ref[...] = y
        err = (x_ref[...] + _rms_fwd(y, g_ref[...])) - t_ref[...]
        sq_ref[...] += jnp.sum(err * err, axis=0, keepdims=True)
        dx_ref[...] = err * (1.0 / d)

    return _call(
        body, name="ffn_down_loss", grid=(s_len // tm,),
        in_specs=[_rows(tm, dff), _resident(w_down.shape), _rows(tm, d), _resident((1, d)), _rows(tm, d)],
        out_specs=[_rows(tm, d), _rows(tm, d), _acc_out((1, d))],
        out_shape=[_sds((s_len, d), F32), _sds((s_len, d), F32), _sds((1, d), F32)],
    )(z, w_down, x2, g_post, target)


def _weight_grad(a, b, name):
    s_len, k = a.shape
    n = b.shape[1]
    ts = 512 if s_len % 512 == 0 else s_len
    tn = n
    while k * tn * 4 > 8 * 2 ** 20 and tn % 256 == 0:
        tn //= 2

    def body(a_ref, b_ref, o_ref):
        @pl.when(pl.program_id(1) == 0)
        def _():
            o_ref[...] = jnp.zeros_like(o_ref)

        o_ref[...] += _dot_tn(a_ref[...].astype(BF16), b_ref[...].astype(BF16))

    return pl.pallas_call(
        body, name=name, grid=(n // tn, s_len // ts),
        in_specs=[pl.BlockSpec((ts, k), lambda j, i: (i, 0)), pl.BlockSpec((ts, tn), lambda j, i: (i, j))],
        out_specs=pl.BlockSpec((k, tn), lambda j, i: (0, j)),
        out_shape=_sds((k, n), F32),
        compiler_params=pltpu.CompilerParams(dimension_semantics=("arbitrary",) * 2, vmem_limit_bytes=V7X_VMEM_LIMIT_BYTES),
    )(a, b)


def _ffn_bwd_a(dx3, y3, g_post, w_down_t, a, u):
    s_len, d = dx3.shape
    dff = a.shape[1]
    tm = ROW_TILE

    def body(dx_ref, y_ref, g_ref, w_ref, a_ref, u_ref, dy_ref, da_ref, du_ref, dg_ref):
        @pl.when(pl.program_id(0) == 0)
        def _():
            dg_ref[...] = jnp.zeros_like(dg_ref)

        dy, dg = _rms_bwd(y_ref[...], g_ref[...], dx_ref[...])
        dg_ref[...] += dg
        dyb = dy.astype(BF16)
        dy_ref[...] = dyb
        dz = _dot(dyb, w_ref[...])
        av = a_ref[...].astype(F32)
        uv = u_ref[...].astype(F32)
        sg = jax.nn.sigmoid(av)
        da_ref[...] = (dz * uv * (sg * (1.0 + av * (1.0 - sg)))).astype(BF16)
        du_ref[...] = (dz * (av * sg)).astype(BF16)

    return _call(
        body, name="ffn_bwd_a", grid=(s_len // tm,),
        in_specs=[_rows(tm, d), _rows(tm, d), _resident((1, d)), _resident(w_down_t.shape), _rows(tm, dff), _rows(tm, dff)],
        out_specs=[_rows(tm, d), _rows(tm, dff), _rows(tm, dff), _acc_out((1, d))],
        out_shape=[_sds((s_len, d), BF16), _sds((s_len, dff), BF16), _sds((s_len, dff), BF16), _sds((1, d), F32)],
    )(dx3, y3, g_post, w_down_t, a, u)


def _ffn_bwd_b(da, du, w_gate_t, w_up_t, dx3, x2, g_pre):
    s_len, d = x2.shape
    dff = da.shape[1]
    tm = ROW_TILE

    def body(da_ref, du_ref, wg_ref, wu_ref, dx_ref, x_ref, g_ref, o_ref, dg_ref):
        @pl.when(pl.program_id(0) == 0)
        def _():
            dg_ref[...] = jnp.zeros_like(dg_ref)

        dh = _dot(da_ref[...], wg_ref[...]) + _dot(du_ref[...], wu_ref[...])
        dx, dg = _rms_bwd(x_ref[...], g_ref[...], dh)
        dg_ref[...] += dg
        o_ref[...] = dx_ref[...] + dx

    return _call(
        body, name="ffn_bwd_b", grid=(s_len // tm,),
        in_specs=[_rows(tm, dff), _rows(tm, dff), _resident(w_gate_t.shape), _resident(w_up_t.shape), _rows(tm, d),
                  _rows(tm, d), _resident((1, d))],
        out_specs=[_rows(tm, d), _acc_out((1, d))],
        out_shape=[_sds((s_len, d), F32), _sds((1, d), F32)],
    )(da, du, w_gate_t, w_up_t, dx3, x2, g_pre)


def _xattn_bwd(dx2, y2, g_post, w_xo_t, q, kx, vx, w_xq_t, x1, g_pre):
    s_len, d = x1.shape
    mw = q.shape[1]
    n_mem = kx.shape[0]
    tm = ROW_TILE

    def body(dx_ref, y_ref, go_ref, wo_ref, q_ref, k_ref, v_ref, wq_ref, x_ref, gp_ref,
             dx1_ref, dy_ref, dq_ref, dk_ref, dv_ref, dgo_ref, dgp_ref):
        @pl.when(pl.program_id(0) == 0)
        def _():
            dk_ref[...] = jnp.zeros_like(dk_ref)
            dv_ref[...] = jnp.zeros_like(dv_ref)
            dgo_ref[...] = jnp.zeros_like(dgo_ref)
            dgp_ref[...] = jnp.zeros_like(dgp_ref)

        dxin = dx_ref[...]
        dy, dgo = _rms_bwd(y_ref[...], go_ref[...], dxin)
        dgo_ref[...] += dgo
        dyb = dy.astype(BF16)
        dy_ref[...] = dyb
        do = _dot(dyb, wo_ref[...]).astype(BF16)
        in0 = _first_head_lanes()
        not0 = jnp.logical_not(in0)
        for pr in range(mw // PAIR):
            sl = slice(pr * PAIR, (pr + 1) * PAIR)
            q2, k2, v2, do2 = q_ref[:, sl], k_ref[:, sl], v_ref[:, sl], do[:, sl]
            dqs = []
            dk2 = jnp.zeros((n_mem, PAIR), F32)
            dv2 = jnp.zeros((n_mem, PAIR), F32)
            for mine in (in0, not0):
                qa = _zero_other(mine, q2)
                doa = _zero_other(mine, do2)
                p = _xattn_softmax(qa, k2)
                dp = _dot_nt(doa, v2)
                ds = (p * (dp - jnp.sum(p * dp, axis=1, keepdims=True))).astype(BF16)
                dqs.append(_dot(ds, k2))
                dk2 = dk2 + _dot_tn(ds, qa)
                dv2 = dv2 + _dot_tn(p.astype(BF16), doa)
            dq_ref[:, sl] = (_pick(in0, dqs[0], dqs[1]) * QK_SCALE).astype(BF16)
            dk_ref[:, sl] += dk2
            dv_ref[:, sl] += dv2
        dh = _dot(dq_ref[...], wq_ref[...])
        dx, dgp = _rms_bwd(x_ref[...], gp_ref[...], dh)
        dgp_ref[...] += dgp
        dx1_ref[...] = dxin + dx

    return _call(
        body, name="xattn_bwd", grid=(s_len // tm,),
        in_specs=[_rows(tm, d), _rows(tm, d), _resident((1, d)), _resident(w_xo_t.shape), _rows(tm, mw),
                  _resident((n_mem, mw)), _resident((n_mem, mw)), _resident(w_xq_t.shape), _rows(tm, d), _resident((1, d))],
        out_specs=[_rows(tm, d), _rows(tm, d), _rows(tm, mw), _acc_out((n_mem, mw)), _acc_out((n_mem, mw)),
                   _acc_out((1, d)), _acc_out((1, d))],
        out_shape=[_sds((s_len, d), F32), _sds((s_len, d), BF16), _sds((s_len, mw), BF16), _sds((n_mem, mw), F32),
                   _sds((n_mem, mw), F32), _sds((1, d), F32), _sds((1, d), F32)],
    )(dx2, y2, g_post, w_xo_t, q, kx, vx, w_xq_t, x1, g_pre)


def _mem_bwd(dk, dv, w_xk_t, w_xv_t, hm, mem, g_mem):
    n_mem, d = mem.shape
    mw = dk.shape[1]

    def body(dk_ref, dv_ref, wk_ref, wv_ref, hm_ref, mem_ref, g_ref, dwk_ref, dwv_ref, dg_ref):
        dkb = dk_ref[...].astype(BF16)
        dvb = dv_ref[...].astype(BF16)
        dhm = _dot(dkb, wk_ref[...]) + _dot(dvb, wv_ref[...])
        _, dg = _rms_bwd(mem_ref[...], g_ref[...], dhm)
        dg_ref[...] = dg
        dwk_ref[...] = _dot_tn(hm_ref[...], dkb)
        dwv_ref[...] = _dot_tn(hm_ref[...], dvb)

    return pl.pallas_call(
        body, name="mem_bwd",
        out_shape=[_sds((d, mw), F32), _sds((d, mw), F32), _sds((1, d), F32)],
    )(dk, dv, w_xk_t, w_xv_t, hm, mem, g_mem)


def _mix_out_bwd(dx1, y1, g_post, w_out_t, o_fox, o_dil):
    s_len, d = dx1.shape
    hw = HEAD_WIDTH
    tm = ROW_TILE
    ones = _head_block_ones().astype(BF16)

    def body(dx_ref, y_ref, g_ref, w_ref, of_ref, od_ref, ones_ref, dy_ref, dof_ref, dlf_ref, dg_ref,
             dod1, dod2, dod3, dld1, dld2, dld3, buf):
        @pl.when(pl.program_id(0) == 0)
        def _():
            dg_ref[...] = jnp.zeros_like(dg_ref)

        dy, dg = _rms_bwd(y_ref[...], g_ref[...], dx_ref[...])
        dg_ref[...] += dg
        dyb = dy.astype(BF16)
        dy_ref[...] = dyb
        do = _dot(dyb, w_ref[...])
        def head_sums(x):
            hi = x.astype(BF16)
            lo = (x - hi.astype(F32)).astype(BF16)
            return _dot(hi, ones_ref[...]) + _dot(lo, ones_ref[...])

        dof_ref[...] = do[:, 0:hw].astype(BF16)
        dlf_ref[...] = head_sums(do[:, 0:hw] * of_ref[...])
        do_dil = do[:, hw:2 * hw]
        dl_dil = head_sums(do_dil * od_ref[...])
        for val, refs in ((do_dil, (dod1, dod2, dod3)), (dl_dil, (dld1, dld2, dld3))):
            _store_lane_blocks(buf, val)
            for ref, dil in zip(refs, DILATIONS):
                if dil == 1:
                    ref[...] = val.astype(ref.dtype)
                else:
                    _write_class_major(buf, ref, dil)

    half = _rows(tm, hw)
    by_dil = [_class_rows(tm, hw, dil) for dil in DILATIONS]
    outs = _call(
        body, name="mix_out_bwd", grid=(s_len // tm,),
        in_specs=[_rows(tm, d), _rows(tm, d), _resident((1, d)), _resident(w_out_t.shape), half, half, _resident((hw, hw))],
        out_specs=[_rows(tm, d), half, half, _acc_out((1, d))] + by_dil + by_dil,
        out_shape=[_sds((s_len, d), BF16), _sds((s_len, hw), BF16), _sds((s_len, hw), F32), _sds((1, d), F32)]
        + [_sds((s_len // dil, dil * hw), BF16) for dil in DILATIONS]
        + [_sds((s_len // dil, dil * hw), F32) for dil in DILATIONS],
        scratch=[pltpu.VMEM((hw // PAIR, tm, PAIR), F32)],
    )(dx1, y1, g_post, w_out_t, o_fox, o_dil, ones)
    return outs[0], outs[1], outs[2], outs[3], outs[4:7], outs[7:10]


def _fox_bwd_prep(aq, lse, delta):
    s_len, hw = aq.shape
    tm = ROW_TILE

    def body(aq_ref, lse_ref, dl_ref, aql_ref, ad_ref):
        jj = _lane_in_head(hw)
        l3 = _split3(_swap_halves(lse_ref[...]))
        aql_ref[...] = _place3(jj, 6, [-p for p in l3], aq_ref[...])
        d3 = _split3(_swap_halves(dl_ref[...]))
        ad_ref[...] = _place3(jj, 0, [-p for p in d3], jnp.zeros((tm, hw), BF16))

    half = _rows(tm, hw)
    return _call(
        body, name="fox_bwd_prep", grid=(s_len // tm,),
        in_specs=[half, half, half], out_specs=[half, half],
        out_shape=[_sds((s_len, hw), BF16), _sds((s_len, hw), BF16)],
    )(aq, lse, delta)


def _ones_on_first3(shape):
    jj = lax.broadcasted_iota(jnp.int32, shape, 1) % HEAD_DIM
    return jnp.where(jj < 3, 1.0, 0.0).astype(BF16)


def _fox_bwd(span, fqkv, do, aql, ak, ad, stats):
    s_len = fqkv.shape[0]
    bq = bk = min(ATT_BLOCK, s_len)
    nq = nk = s_len // bq

    def body(st_ref, q_ref, k_ref, v_ref, do_ref, aql_ref, ak_ref, ad_ref, dq_ref, rs_ref, dk_ref, dv_ref, dc_ref,
             ka_ref, va_ref, kone_ref, r_ref, dvacc_ref, dqacc_ref):
        pair, kj, ahead = pl.program_id(0), pl.program_id(1), pl.program_id(2)
        valid = kj + ahead < nq
        qi = jnp.minimum(kj + ahead, nq - 1)
        in0 = _first_head_lanes()
        not0 = jnp.logical_not(in0)
        heads = ((0, in0), (1, not0))
        rows = pl.ds(pl.multiple_of(qi * bq, bq), bq)

        @pl.when(ahead == 0)
        def _():
            k2, v2, a2 = k_ref[...], v_ref[...], ak_ref[...]
            one = jnp.ones_like(k2)
            one3 = _ones_on_first3(v2.shape)
            for a, mine in heads:
                ka_ref[a] = jnp.where(mine, k2, a2)
                va_ref[a] = jnp.where(mine, v2, one3)
                kone_ref[a] = jnp.where(mine, k2, one)
            r_ref[...] = jnp.zeros_like(r_ref)
            dvacc_ref[...] = jnp.zeros_like(dvacc_ref)

        @pl.when(jnp.logical_and(valid, jnp.logical_or(kj == 0, ahead == span - 1)))
        def _():
            for a, _ in heads:
                dqacc_ref[a, rows, :] = jnp.zeros((bq, PAIR), F32)

        def head_step(a, mine, masked):
            q2 = q_ref[...]
            doa = jnp.where(mine, do_ref[...], ad_ref[...])
            s = _dot_nt(jnp.where(mine, q2, aql_ref[...]), ka_ref[a])
            if masked:
                s = jnp.where(_lower_triangle(bq), s, NEG)
            p = jnp.exp(s)
            ds = (p * _dot_nt(doa, va_ref[a])).astype(BF16)
            dvacc_ref[a] += _dot_tn(p.astype(BF16), doa)
            r_ref[a] += _dot_tn(ds, jnp.where(mine, q2, jnp.ones_like(q2)))
            dqacc_ref[a, rows, :] += _dot(ds, kone_ref[a])

        for a, mine in heads:
            @pl.when(jnp.logical_and(jnp.logical_and(valid, ahead > 0), _head_needed(st_ref, 2 * pair + a, qi, kj)))
            def _():
                head_step(a, mine, False)

        @pl.when(ahead == 0)
        def _():
            for a, mine in heads:
                head_step(a, mine, True)
            acc0, acc1 = dqacc_ref[0, rows, :], dqacc_ref[1, rows, :]
            dq_ref[...] = (_pick(in0, acc0, acc1) * QK_SCALE).astype(BF16)
            rs_ref[...] = pltpu.roll(_pick(in0, acc1, acc0), HEAD_DIM, 1)

        @pl.when(ahead == span - 1)
        def _():
            dk_ref[...] = _pick(in0, r_ref[0], r_ref[1]).astype(BF16)
            dv_ref[...] = _pick(in0, dvacc_ref[0], dvacc_ref[1]).astype(BF16)
            dc_ref[...] = -pltpu.roll(_pick(in0, r_ref[1], r_ref[0]), HEAD_DIM, 1)

    blk = lambda n: (n, PAIR)
    kvmap = lambda p, j, t, st: (j, p)

    def qmap(p, j, t, st):
        last = jnp.clip(st[FOX_LAST_QUERY_ROW + p, j].astype(jnp.int32), j, nq - 1)
        return (jnp.minimum(j + t, last), p)

    return pl.pallas_call(
        body, name=f"fox_bwd_{span}",
        grid_spec=pltpu.PrefetchScalarGridSpec(
            num_scalar_prefetch=1, grid=(N_PAIRS, nk, span),
            in_specs=[
                pl.BlockSpec(blk(bq), qmap),
                pl.BlockSpec(blk(bk), lambda p, j, t, st: (j, N_PAIRS + p)),
                pl.BlockSpec(blk(bk), lambda p, j, t, st: (j, 2 * N_PAIRS + p)),
                pl.BlockSpec(blk(bq), qmap), pl.BlockSpec(blk(bq), qmap),
                pl.BlockSpec(blk(bk), kvmap),
                pl.BlockSpec(blk(bq), qmap),
            ],
            out_specs=[pl.BlockSpec(blk(bk), kvmap)] * 5,
            scratch_shapes=[pltpu.VMEM((2, bk, PAIR), BF16), pltpu.VMEM((2, bk, PAIR), BF16), pltpu.VMEM((2, bk, PAIR), BF16),
                            pltpu.VMEM((2, bk, PAIR), F32), pltpu.VMEM((2, bk, PAIR), F32), pltpu.VMEM((2, s_len, PAIR), F32)]),
        out_shape=[_sds((s_len, HEAD_WIDTH), BF16), _sds((s_len, HEAD_WIDTH), F32), _sds((s_len, HEAD_WIDTH), BF16),
                   _sds((s_len, HEAD_WIDTH), BF16), _sds((s_len, HEAD_WIDTH), F32)],
        compiler_params=pltpu.CompilerParams(dimension_semantics=("arbitrary",) * 3, vmem_limit_bytes=V7X_VMEM_LIMIT_BYTES),
    )(stats, fqkv, fqkv, fqkv, do, aql, ak, ad)


def _gate_bwd(rs, dc, gx, b_exp):
    s_len, hw = gx.shape
    t = min(SCAN_TILE, s_len)
    nt = s_len // t
    tri = jnp.asarray(np.triu(np.ones((t, t), np.float32)))

    def body(rs_ref, dc_ref, gx_ref, b_ref, tri_ref, dgx_ref, db_ref, carry):
        @pl.when(pl.program_id(0) == 0)
        def _():
            carry[...] = jnp.zeros_like(carry)
            db_ref[...] = jnp.zeros_like(db_ref)

        dlf = jnp.dot(tri_ref[...], rs_ref[...] + dc_ref[...], precision=HIGHEST, preferred_element_type=F32) + carry[...]
        carry[...] = dlf[0:1, :]
        dgate = dlf * jax.nn.sigmoid(-(gx_ref[...] + b_ref[...]))
        db_ref[...] += jnp.sum(dgate, axis=0, keepdims=True)
        lane = lax.broadcasted_iota(jnp.int32, (1, hw), 1)
        dgx_ref[...] = jnp.where(lane % HEAD_DIM == 0, dgate, 0.0).astype(BF16)

    rev = lambda i: (nt - 1 - i, 0)
    return _call(
        body, name="gate_bwd", grid=(nt,),
        in_specs=[pl.BlockSpec((t, hw), rev)] * 3 + [_resident((1, hw)), _resident((t, t))],
        out_specs=[pl.BlockSpec((t, hw), rev), _acc_out((1, hw))],
        out_shape=[_sds((s_len, hw), BF16), _sds((1, hw), F32)],
        scratch=[pltpu.VMEM((1, hw), F32)],
    )(rs, dc, gx, b_exp, tri)


def _dil_bwd(qkv_v, do_v, lj_v, dl_v, bias, branch):
    dil = DILATIONS[branch]
    w = DIL_BLOCK
    hw = HEAD_WIDTH
    length = qkv_v.shape[0]
    nb = length // w

    def body(q0_ref, q1_ref, kp_ref, kc_ref, vp_ref, vc_ref, do0_ref, do1_ref, l0_ref, l1_ref, d0_ref, d1_ref, b_ref,
             dq_ref, dk_ref, dv_ref, dsum_ref):
        r, n = pl.program_id(0), pl.program_id(1)
        in0 = _first_head_lanes()
        not0 = jnp.logical_not(in0)
        first = n == 0
        last = n == nb - 1

        @pl.when(jnp.logical_and(r == 0, n == 0))
        def _():
            dsum_ref[...] = jnp.zeros_like(dsum_ref)

        pairs = [slice(pr * PAIR, (pr + 1) * PAIR) for pr in range(N_PAIRS)]

        def both_heads(ref, sl):
            v = ref[:, sl]
            return jnp.concatenate([_zero_other(in0, v), _zero_other(not0, v)], axis=0)

        def head_columns(ref):
            return jnp.concatenate([ref[:, h * HEAD_DIM:h * HEAD_DIM + 1] for h in range(N_HEADS)], axis=0)

        qq0 = [both_heads(q0_ref, sl) for sl in pairs]
        qq1 = [both_heads(q1_ref, sl) for sl in pairs]
        dd0 = [both_heads(do0_ref, sl) for sl in pairs]
        dd1 = [both_heads(do1_ref, sl) for sl in pairs]
        stack = lambda tiles: jnp.concatenate(tiles, axis=0)
        s_a = stack([_dot_nt(qq0[i], kp_ref[:, sl]) for i, sl in enumerate(pairs)])
        s_b = stack([_dot_nt(qq0[i], kc_ref[:, sl]) for i, sl in enumerate(pairs)])
        s_c = stack([_dot_nt(qq1[i], kc_ref[:, sl]) for i, sl in enumerate(pairs)])
        dp_a = stack([_dot_nt(dd0[i], vp_ref[:, sl]) for i, sl in enumerate(pairs)])
        dp_b = stack([_dot_nt(dd0[i], vc_ref[:, sl]) for i, sl in enumerate(pairs)])
        dp_c = stack([_dot_nt(dd1[i], vc_ref[:, sl]) for i, sl in enumerate(pairs)])
        bias2 = b_ref[...].reshape(N_HEADS * w, 2 * w)
        b_prev, b_cur = bias2[:, 0:w], bias2[:, w:2 * w]
        lse0, lse1 = head_columns(l0_ref), head_columns(l1_ref)
        dl0, dl1 = head_columns(d0_ref), head_columns(d1_ref)
        p_a = jnp.exp(jnp.where(first, NEG, s_a + b_prev) - lse0)
        p_b = jnp.exp((s_b + b_cur) - lse0)
        p_c = jnp.exp(jnp.where(last, NEG, s_c + b_prev) - lse1)
        ds_a = p_a * (dp_a - dl0)
        ds_b = p_b * (dp_b - dl0)
        ds_c = p_c * (dp_c - dl1)
        dsum_ref[...] += jnp.concatenate([ds_a, ds_b], axis=1).reshape(N_HEADS, w, 2 * w)
        ds_a, ds_b, ds_c = ds_a.astype(BF16), ds_b.astype(BF16), ds_c.astype(BF16)
        p_b, p_c = p_b.astype(BF16), p_c.astype(BF16)
        for i, sl in enumerate(pairs):
            rows = slice(2 * i * w, (2 * i + 2) * w)
            dq2 = _dot(ds_a[rows], kp_ref[:, sl]) + _dot(ds_b[rows], kc_ref[:, sl])
            dq_ref[:, sl] = _pick(in0, dq2[:w], dq2[w:])
            dk_ref[:, sl] = _dot_tn(ds_b[rows], qq0[i]) + _dot_tn(ds_c[rows], qq1[i])
            dv_ref[:, sl] = _dot_tn(p_b[rows], dd0[i]) + _dot_tn(p_c[rows], dd1[i])

    prev = lambda n: jnp.maximum(n - 1, 0)
    nxt = lambda n: jnp.minimum(n + 1, nb - 1)
    blk = (w, hw)
    outs = pl.pallas_call(
        body, name=f"dil_bwd_{dil}", grid=(dil, nb),
        in_specs=[
            pl.BlockSpec(blk, lambda r, n: (n, 3 * r)),
            pl.BlockSpec(blk, lambda r, n: (nxt(n), 3 * r)),
            pl.BlockSpec(blk, lambda r, n: (prev(n), 3 * r + 1)),
            pl.BlockSpec(blk, lambda r, n: (n, 3 * r + 1)),
            pl.BlockSpec(blk, lambda r, n: (prev(n), 3 * r + 2)),
            pl.BlockSpec(blk, lambda r, n: (n, 3 * r + 2)),
            pl.BlockSpec(blk, lambda r, n: (n, r)),
            pl.BlockSpec(blk, lambda r, n: (nxt(n), r)),
            pl.BlockSpec(blk, lambda r, n: (n, r)),
            pl.BlockSpec(blk, lambda r, n: (nxt(n), r)),
            pl.BlockSpec(blk, lambda r, n: (n, r)),
            pl.BlockSpec(blk, lambda r, n: (nxt(n), r)),
            pl.BlockSpec((None, N_HEADS, w, 2 * w), lambda r, n: (branch, 0, 0, 0)),
        ],
        out_specs=[pl.BlockSpec(blk, lambda r, n: (n, r))] * 3 + [pl.BlockSpec((N_HEADS, w, 2 * w), lambda r, n: (0, 0, 0))],
        out_shape=[_sds((length, dil * hw), F32)] * 3 + [_sds((N_HEADS, w, 2 * w), F32)],
        compiler_params=pltpu.CompilerParams(dimension_semantics=("arbitrary",) * 2, vmem_limit_bytes=V7X_VMEM_LIMIT_BYTES),
    )(qkv_v, qkv_v, qkv_v, qkv_v, qkv_v, qkv_v, do_v, do_v, lj_v, lj_v, dl_v, dl_v, bias)
    return list(outs)


def _rel_bias_grad(dsum, buckets):
    w = DIL_BLOCK

    def body(ds_ref, bk_ref, o_ref):
        row = lax.broadcasted_iota(jnp.int32, (N_BUCKETS, PAIR), 0)
        lane = lax.broadcasted_iota(jnp.int32, (N_BUCKETS, PAIR), 1)

        def per_bucket(b, acc):
            for p in range(len(DILATIONS)):
                hit = bk_ref[p] == b
                for h in range(N_HEADS):
                    part = jnp.where(hit, ds_ref[p, h], 0.0)
                    tot = jnp.sum(jnp.sum(part, axis=1, keepdims=True), axis=0, keepdims=True)
                    acc = acc + jnp.where(jnp.logical_and(row == b, lane == h), tot, 0.0)
            return acc

        o_ref[...] = lax.fori_loop(0, N_BUCKETS, per_bucket, jnp.zeros((N_BUCKETS, PAIR), F32))

    return pl.pallas_call(body, name="rel_bias_grad", out_shape=_sds((N_BUCKETS, PAIR), F32))(dsum, buckets)


def _in_proj_bwd(dfq, dfk, dfv, dgx, ddq, ddk, ddv, w_in_t, dx1, x, g_pre):
    s_len, d = x.shape
    hw = HEAD_WIDTH
    tm = ROW_TILE

    def body(fq, fk, fv, gx, q1, q2, q3, k1, k2, k3, v1, v2, v3, w_ref, dx_ref, x_ref, g_ref, o_ref, dp_ref, dg_ref, buf):
        @pl.when(pl.program_id(0) == 0)
        def _():
            dg_ref[...] = jnp.zeros_like(dg_ref)

        def branch_sum(refs):
            a, b, c = [r[...] if dil == 1 else _read_class_major(r, buf, dil) for r, dil in zip(refs, DILATIONS)]
            return (a + b) + c

        dp_ref[:, 0:hw] = fq[...]
        dp_ref[:, hw:2 * hw] = fk[...]
        dp_ref[:, 2 * hw:3 * hw] = fv[...]
        dp_ref[:, 3 * hw:4 * hw] = gx[...]
        dp_ref[:, 4 * hw:5 * hw] = (branch_sum((q1, q2, q3)) * QK_SCALE).astype(BF16)
        dp_ref[:, 5 * hw:6 * hw] = branch_sum((k1, k2, k3)).astype(BF16)
        dp_ref[:, 6 * hw:7 * hw] = branch_sum((v1, v2, v3)).astype(BF16)
        dh = _dot(dp_ref[...], w_ref[...])
        dx, dg = _rms_bwd(x_ref[...], g_ref[...], dh)
        dg_ref[...] += dg
        o_ref[...] = dx_ref[...] + dx

    half = _rows(tm, hw)
    by_dil = [_class_rows(tm, hw, dil) for dil in DILATIONS]
    return _call(
        body, name="in_proj_bwd", grid=(s_len // tm,),
        in_specs=[half] * 4 + by_dil * 3 + [_resident(w_in_t.shape), _rows(tm, d), _rows(tm, d), _resident((1, d))],
        out_specs=[_rows(tm, d), _rows(tm, 7 * hw), _acc_out((1, d))],
        out_shape=[_sds((s_len, d), F32), _sds((s_len, 7 * hw), BF16), _sds((1, d), F32)],
        scratch=[pltpu.VMEM((hw // PAIR, tm, PAIR), F32)],
    )(dfq, dfk, dfv, dgx, *ddq, *ddk, *ddv, w_in_t, dx1, x, g_pre)


def _expand_w_in(w_in):
    hw = HEAD_WIDTH
    gate = jnp.repeat(w_in[:, 3 * hw:3 * hw + N_HEADS], HEAD_DIM, axis=1)
    return jnp.concatenate([w_in[:, :3 * hw], gate, w_in[:, 3 * hw + N_HEADS:]], axis=1)


def _local_step(x, mem, target, g, w_bf):
    hw = HEAD_WIDTH
    w_in_e = _expand_w_in(w_bf["w_in"])
    b_exp = jnp.repeat(g["b_f"], HEAD_DIM, axis=1)
    buckets = jnp.asarray(_dil_buckets())

    n_dil = len(DILATIONS)
    h1, fqkv, gx, *dqkv = _in_proj(x, g["g_mix_pre"], w_in_e)
    aq, ak, c = _gate_scan(gx, b_exp)
    stats = _fox_block_stats(fqkv, c)
    spans = _fox_spans(x.shape[0])
    span_idx = _fox_span_index(stats, x.shape[0])
    o_fox, lse_fox = lax.switch(span_idx, [functools.partial(_fox_fwd, s) for s in spans], fqkv, aq, ak, stats)
    bias = _dil_bias(g["rel_bias"], buckets)
    branches = [_dil_fwd(dqkv[p], bias, p) for p in range(n_dil)]
    x1, y1, o_dil, *lj = _mix_out(o_fox, [b[0] for b in branches], [b[1] for b in branches], w_bf["w_out"], x, g["g_mix_post"])
    hm, kx, vx = _mem_fwd(mem, g["g_mem"], w_bf["w_xk"], w_bf["w_xv"])
    x2, y2, h2, qx, ox = _xattn_fwd(x1, g["g_xattn_pre"], w_bf["w_xq"], kx, vx, w_bf["w_xo"], g["g_xattn_post"])
    h3, a, u, z = _ffn_up(x2, g["g_ffn_pre"], w_bf["w_gate"], w_bf["w_up"])
    y3, dx3, sq = _ffn_down_loss(z, w_bf["w_down"], x2, g["g_ffn_post"], target)

    grads = {}
    dy3, da, du, grads["g_ffn_post"] = _ffn_bwd_a(dx3, y3, g["g_ffn_post"], w_bf["w_down"].T, a, u)
    dx2, grads["g_ffn_pre"] = _ffn_bwd_b(da, du, w_bf["w_gate"].T, w_bf["w_up"].T, dx3, x2, g["g_ffn_pre"])
    grads["w_down"] = _weight_grad(z, dy3, "dw_down")
    grads["w_gate"] = _weight_grad(h3, da, "dw_gate")
    grads["w_up"] = _weight_grad(h3, du, "dw_up")
    dx1, dy2, dqx, dkx, dvx, grads["g_xattn_post"], grads["g_xattn_pre"] = _xattn_bwd(
        dx2, y2, g["g_xattn_post"], w_bf["w_xo"].T, qx, kx, vx, w_bf["w_xq"].T, x1, g["g_xattn_pre"])
    grads["w_xo"] = _weight_grad(ox, dy2, "dw_xo")
    grads["w_xq"] = _weight_grad(h2, dqx, "dw_xq")
    grads["w_xk"], grads["w_xv"], grads["g_mem"] = _mem_bwd(dkx, dvx, w_bf["w_xk"].T, w_bf["w_xv"].T, hm, mem, g["g_mem"])
    dy1, do_fox, delta_fox, grads["g_mix_post"], do_dil, delta_dil = _mix_out_bwd(
        dx1, y1, g["g_mix_post"], w_bf["w_out"].T, o_fox, o_dil)
    grads["w_out"] = jnp.concatenate([_weight_grad(o_fox, dy1, "dw_out_fox"), _weight_grad(o_dil, dy1, "dw_out_dil")], axis=0)
    aql, ad = _fox_bwd_prep(aq, lse_fox, delta_fox)
    dfq, rs, dfk, dfv, dc = lax.switch(
        span_idx, [functools.partial(_fox_bwd, s) for s in spans], fqkv, do_fox, aql, ak, ad, stats)
    dgx, db = _gate_bwd(rs, dc, gx, b_exp)
    grads["b_f"] = db[:, ::HEAD_DIM]
    dil = [_dil_bwd(dqkv[p], do_dil[p], lj[p], delta_dil[p], bias, p) for p in range(n_dil)]
    grads["rel_bias"] = _rel_bias_grad(jnp.stack([t[3] for t in dil]), buckets)[:, :N_HEADS]
    grad_x, dproj, grads["g_mix_pre"] = _in_proj_bwd(
        dfq, dfk, dfv, dgx, [t[0] for t in dil], [t[1] for t in dil], [t[2] for t in dil], w_in_e.T, dx1, x, g["g_mix_pre"])
    dw_in_e = _weight_grad(h1, dproj, "dw_in")
    grads["w_in"] = jnp.concatenate(
        [dw_in_e[:, :3 * hw], dw_in_e[:, 3 * hw:4 * hw:HEAD_DIM], dw_in_e[:, 4 * hw:]], axis=1)
    return sq, grad_x, grads


MESH = pl.DeviceIdType.MESH


def _my_place():
    return lax.axis_index("x"), lax.axis_index("y"), lax.axis_index("c")


def _all_gather(x, name):
    rows, lanes = x.shape

    def body(x_ref, out_ref, send_sems, recv_sems, local_sem):
        mx, my, mc = _my_place()
        me, sibling = (mx, my, mc), (mx, my, 1 - mc)
        chips = [(1 - mx, my), (mx, 1 - my), (1 - mx, 1 - my)]

        def slot(px, py, pc):
            return out_ref.at[4 * px + 2 * py + pc]

        def copy(k, block, to, src=None):
            return pltpu.make_async_remote_copy(
                src_ref=slot(*block) if src is None else src, dst_ref=slot(*block),
                send_sem=send_sems.at[k], recv_sem=recv_sems.at[k], device_id=to, device_id_type=MESH)

        mine = pltpu.make_async_copy(x_ref, slot(*me), local_sem)
        mine.start()
        first = [copy(0, me, sibling, src=x_ref)]
        first += [copy(1 + j, me, (*chip, mc), src=x_ref) for j, chip in enumerate(chips)]
        for cp in first:
            cp.start()
        passed = [copy(4 + j, (*chip, mc), sibling) for j, chip in enumerate(chips)]
        for j, chip in enumerate(chips):
            copy(1 + j, (*chip, mc), me).wait_recv()
            passed[j].start()
        copy(0, sibling, me).wait_recv()
        for j, chip in enumerate(chips):
            copy(4 + j, (*chip, 1 - mc), me).wait_recv()
        for cp in first + passed:
            cp.wait_send()
        mine.wait()

    return pl.pallas_call(
        body, name=name, out_shape=_sds((N_DEV, rows, lanes), x.dtype),
        in_specs=[pl.BlockSpec(memory_space=pl.ANY)], out_specs=pl.BlockSpec(memory_space=pl.ANY),
        scratch_shapes=[pltpu.SemaphoreType.DMA((N_DEV - 1,)), pltpu.SemaphoreType.DMA((N_DEV - 1,)), pltpu.SemaphoreType.DMA],
    )(x)


def _exchange(g, name):
    _, rows, lanes = g.shape

    def body(g_ref, land_ref, send_sems, recv_sems, local_sem):
        mx, my, mc = _my_place()
        me = 4 * mx + 2 * my + mc
        mine = pltpu.make_async_copy(g_ref.at[me], land_ref.at[me], local_sem)
        mine.start()
        sent, arriving = [], []
        for k in (1, 2, 3, 4, 5, 6, 7):
            px = 1 - mx if k & 4 else mx
            py = 1 - my if k & 2 else my
            pc = 1 - mc if k & 1 else mc
            peer = 4 * px + 2 * py + pc
            cp = pltpu.make_async_remote_copy(
                src_ref=g_ref.at[peer], dst_ref=land_ref.at[me], send_sem=send_sems.at[k - 1], recv_sem=recv_sems.at[k - 1],
                device_id=(px, py, pc), device_id_type=MESH)
            cp.start()
            sent.append(cp)
            arriving.append(pltpu.make_async_remote_copy(
                src_ref=g_ref.at[me], dst_ref=land_ref.at[peer], send_sem=send_sems.at[k - 1], recv_sem=recv_sems.at[k - 1],
                device_id=(px, py, pc), device_id_type=MESH))
        for cp in arriving:
            cp.wait_recv()
        for cp in sent:
            cp.wait_send()
        mine.wait()

    return pl.pallas_call(
        body, name=name, out_shape=_sds(g.shape, g.dtype),
        in_specs=[pl.BlockSpec(memory_space=pl.ANY)], out_specs=pl.BlockSpec(memory_space=pl.ANY),
        scratch_shapes=[pltpu.SemaphoreType.DMA((N_DEV - 1,)), pltpu.SemaphoreType.DMA((N_DEV - 1,)), pltpu.SemaphoreType.DMA],
    )(g)


def _sum_slots(parts, name):
    n, rows, lanes = parts.shape
    tr = 512 if rows % 512 == 0 else rows

    def body(p_ref, o_ref):
        acc = p_ref[0].astype(F32)
        for j in range(1, n):
            acc = acc + p_ref[j].astype(F32)
        o_ref[...] = acc

    return _call(
        body, name=name, grid=(rows // tr,),
        in_specs=[pl.BlockSpec((n, tr, lanes), lambda i: (0, i, 0))], out_specs=_rows(tr, lanes),
        out_shape=_sds((rows, lanes), F32),
    )(parts)


def _adamw(w, g, m, v, name):
    def body(w_ref, g_ref, m_ref, v_ref, d_ref, nm_ref, nv_ref):
        gv = g_ref[...]
        m_new = ADAM_B1 * m_ref[...] + (1.0 - ADAM_B1) * gv
        v_new = ADAM_B2 * v_ref[...] + (1.0 - ADAM_B2) * (gv * gv)
        nm_ref[...] = m_new
        nv_ref[...] = v_new
        m_hat = m_new / (1.0 - ADAM_B1 ** ADAM_STEP)
        v_hat = v_new / (1.0 - ADAM_B2 ** ADAM_STEP)
        d_ref[...] = -ADAM_LR * (m_hat / (jnp.sqrt(v_hat) + ADAM_EPS) + ADAM_WD * w_ref[...])

    out = _sds(w.shape, F32)
    return pl.pallas_call(
        body, name=name, out_shape=[out, out, out],
        compiler_params=pltpu.CompilerParams(vmem_limit_bytes=V7X_VMEM_LIMIT_BYTES),
    )(w, g, m, v)


def _loss_head(sq, d_model):
    def body(sq_ref, o_ref):
        tot = jnp.sum(jnp.sum(sq_ref[...], axis=1, keepdims=True), axis=0, keepdims=True)
        o_ref[...] = 0.5 * (tot / d_model)

    return pl.pallas_call(body, name="loss_head", out_shape=_sds((1, 1), F32))(sq)


_BIG = (("w_in", 1), ("w_out", 0), ("w_xq", 0), ("w_xk", 0), ("w_xv", 0), ("w_xo", 1), ("w_gate", 1), ("w_up", 1), ("w_down", 0))
_SMALL = ("g_mix_pre", "b_f", "rel_bias", "g_mix_post", "g_xattn_pre", "g_mem", "g_xattn_post", "g_ffn_pre", "g_ffn_post")
LANES = 128
BIG_ROW_ALIGN = 512


def _round_up(n, k):
    return -(-n // k) * k


def _pack_rows(flat_parts, row_align, dtype):
    starts, rows, padded = [], 0, []
    for p in flat_parts:
        n = _round_up(p.shape[0], LANES)
        starts.append(rows)
        rows += n // LANES
        padded.append(jnp.pad(p.astype(dtype), (0, n - p.shape[0])))
    total = _round_up(rows, row_align)
    padded.append(jnp.zeros(((total - rows) * LANES,), dtype))
    return jnp.concatenate(padded).reshape(total, LANES), starts


def _unpack_rows(buf, starts, shapes):
    lead = buf.shape[:-2]
    flat = buf.reshape(lead + (-1,))
    out = []
    for st, shp in zip(starts, shapes):
        n = int(np.prod(shp))
        out.append(flat[..., st * LANES:st * LANES + n].reshape(lead + tuple(shp)))
    return out


def kernel(x, mem, g_mix_pre, w_in, b_f, rel_bias, w_out, g_mix_post, g_xattn_pre, g_mem, w_xq, w_xk, w_xv, w_xo, g_xattn_post, g_ffn_pre, w_gate, w_up, w_down, g_ffn_post, loss_target, m_g_mix_pre, m_w_in, m_b_f, m_rel_bias, m_w_out, m_g_mix_post, m_g_xattn_pre, m_g_mem, m_w_xq, m_w_xk, m_w_xv, m_w_xo, m_g_xattn_post, m_g_ffn_pre, m_w_gate, m_w_up, m_w_down, m_g_ffn_post, v_g_mix_pre, v_w_in, v_b_f, v_rel_bias, v_w_out, v_g_mix_post, v_g_xattn_pre, v_g_mem, v_w_xq, v_w_xk, v_w_xv, v_w_xo, v_g_xattn_post, v_g_ffn_pre, v_w_gate, v_w_up, v_w_down, v_g_ffn_post):
    given = dict(locals())
    order = ("g_mix_pre", "w_in", "b_f", "rel_bias", "w_out", "g_mix_post", "g_xattn_pre", "g_mem", "w_xq", "w_xk", "w_xv",
             "w_xo", "g_xattn_post", "g_ffn_pre", "w_gate", "w_up", "w_down", "g_ffn_post")
    two_d = lambda a: a.reshape(a.shape[-2:])
    w_loc = {n: two_d(given[n]) for n in order}
    m_loc = {n: two_d(given["m_" + n]) for n in order}
    v_loc = {n: two_d(given["v_" + n]) for n in order}
    d_model = x.shape[-1]

    shard_shapes = [w_loc[n].shape for n, _ in _BIG]
    packed, starts = _pack_rows([w_loc[n].reshape(-1) for n, _ in _BIG], BIG_ROW_ALIGN, BF16)
    gathered = _all_gather(packed, "gather_weights")
    w_bf = {}
    for (n, axis), part in zip(_BIG, _unpack_rows(gathered, starts, shard_shapes)):
        r, c = part.shape[1:]
        w_bf[n] = part.reshape(N_DEV * r, c) if axis == 0 else part.transpose(1, 0, 2).reshape(r, N_DEV * c)

    small = {n: w_loc[n] for n in _SMALL}
    sq, grad_x, grads = _local_step(two_d(x), two_d(mem), two_d(loss_target), small, w_bf)

    per_owner = []
    for (n, axis), shp in zip(_BIG, shard_shapes):
        r, c = shp
        gfull = grads[n]
        per_owner.append(gfull.reshape(N_DEV, r * c) if axis == 0 else gfull.reshape(r, N_DEV, c).transpose(1, 0, 2).reshape(N_DEV, r * c))
    rows_big = packed.shape[0]
    slots = []
    for j in range(N_DEV):
        buf, _ = _pack_rows([p[j] for p in per_owner], BIG_ROW_ALIGN, BF16)
        slots.append(buf)
    landed = _exchange(jnp.stack(slots), "exchange_grads")
    g_big = dict(zip([n for n, _ in _BIG], _unpack_rows(_sum_slots(landed, "sum_grads"), starts, shard_shapes)))
    assert landed.shape[1] == rows_big

    small_parts = [grads[n].reshape(-1) for n in _SMALL] + [sq.reshape(-1)]
    small_shapes = [w_loc[n].shape for n in _SMALL] + [sq.shape]
    spacked, sstarts = _pack_rows(small_parts, 8, F32)
    ssum = _sum_slots(_all_gather(spacked, "gather_small"), "sum_small")
    g_small = dict(zip(_SMALL, _unpack_rows(ssum, sstarts, small_shapes)[:-1]))
    sq_rows = sq.size // LANES
    loss = _loss_head(ssum[sstarts[-1]:sstarts[-1] + sq_rows], d_model).reshape(())

    g_loc, delta, new_m, new_v = {}, {}, {}, {}
    for n, _ in _BIG:
        g_loc[n] = g_big[n]
        delta[n], new_m[n], new_v[n] = _adamw(w_loc[n], g_big[n], m_loc[n], v_loc[n], "adamw_" + n)
    pk = lambda d: _pack_rows([d[n].reshape(-1) for n in _SMALL], 8, F32)[0]
    pstarts = _pack_rows([w_loc[n].reshape(-1) for n in _SMALL], 8, F32)[1]
    d_s, m_s, v_s = _adamw(pk(w_loc), pk(g_small), pk(m_loc), pk(v_loc), "adamw_small")
    shapes_s = [w_loc[n].shape for n in _SMALL]
    for n, dd, mm, vv in zip(_SMALL, _unpack_rows(d_s, pstarts, shapes_s), _unpack_rows(m_s, pstarts, shapes_s),
                             _unpack_rows(v_s, pstarts, shapes_s)):
        g_loc[n], delta[n], new_m[n], new_v[n] = g_small[n], dd, mm, vv

    like = lambda d: [d[n].reshape(given[n].shape) for n in order]
    return (loss, grad_x.reshape(x.shape), *like(g_loc), *like(delta), *like(new_m), *like(new_v))
```

```python
import functools

import numpy as np
import jax
import jax.numpy as jnp
from jax import lax
from jax.experimental import pallas as pl
from jax.experimental.pallas import tpu as pltpu

F32 = jnp.float32
BF16 = jnp.bfloat16

RMS_EPS = 1e-6
HEAD_DIM = 64
N_HEADS = 8
HEAD_WIDTH = N_HEADS * HEAD_DIM
PAIR = 2 * HEAD_DIM
N_PAIRS = N_HEADS // 2
DIL_BLOCK = 128
DILATIONS = (1, 4, 16)
N_BUCKETS = 32
MAX_DISTANCE = 2048
N_MEM_HEADS = 4
QK_SCALE = HEAD_DIM ** -0.5
NEG = -1e30
FOX_SKIP_MARGIN = 110.0
FOX_SHORT_SPANS = (4, 8, 16)
N_DEV = 8

ADAM_LR = 0.001
ADAM_B1 = 0.9
ADAM_B2 = 0.999
ADAM_EPS = 1e-08
ADAM_WD = 0.01
ADAM_STEP = 10

V7X_VMEM_LIMIT_BYTES = 56 * 2 ** 20
ROW_TILE = 256
ATT_BLOCK = 512
SCAN_TILE = 256


def _call(body, *, name, grid, in_specs, out_specs, out_shape, scratch=()):
    return pl.pallas_call(
        body, name=name, grid=grid, in_specs=in_specs, out_specs=out_specs, out_shape=out_shape,
        scratch_shapes=list(scratch),
        compiler_params=pltpu.CompilerParams(
            dimension_semantics=("arbitrary",) * len(grid), vmem_limit_bytes=V7X_VMEM_LIMIT_BYTES))


def _rows(tm, n):
    return pl.BlockSpec((tm, n), lambda i: (i, 0))


def _resident(shape):
    zeros = (0,) * len(shape)
    return pl.BlockSpec(shape, lambda i: zeros, pipeline_mode=pl.Buffered(1))


def _acc_out(shape):
    zeros = (0,) * len(shape)
    return pl.BlockSpec(shape, lambda i: zeros)


def _sds(shape, dtype):
    return jax.ShapeDtypeStruct(shape, dtype)


def _dot(a, b):
    return jnp.dot(a, b, preferred_element_type=F32)


def _dot_nt(a, b):
    return lax.dot_general(a, b, (((1,), (1,)), ((), ())), preferred_element_type=F32)


def _dot_tn(a, b):
    return lax.dot_general(a, b, (((0,), (0,)), ((), ())), preferred_element_type=F32)


def _rms_fwd(x, g):
    r = lax.rsqrt(jnp.mean(x * x, axis=-1, keepdims=True) + RMS_EPS)
    return (x * r) * g


def _rms_bwd(xin, g, dy):
    r = lax.rsqrt(jnp.mean(xin * xin, axis=-1, keepdims=True) + RMS_EPS)
    xhat = xin * r
    dg = jnp.sum(dy * xhat, axis=0, keepdims=True)
    dxh = dy * g
    dx = r * (dxh - xhat * jnp.mean(dxh * xhat, axis=-1, keepdims=True))
    return dx, dg


def _first_head_lanes():
    return lax.broadcasted_iota(jnp.int32, (1, PAIR), 1) < HEAD_DIM


def _pick(mask, a, b):
    return jnp.where(mask, a, b)


def _zero_other(mask, v):
    return jnp.where(mask, v, jnp.zeros_like(v))


def _store_lane_blocks(buf_ref, val):
    for cb in range(buf_ref.shape[0]):
        buf_ref[cb] = val[:, cb * PAIR:(cb + 1) * PAIR].astype(F32)


def _load_lane_blocks(buf_ref):
    return jnp.concatenate([buf_ref[cb] for cb in range(buf_ref.shape[0])], axis=1)


def _write_class_major(buf_ref, out_ref, dil):
    n, tile, _ = buf_ref.shape
    for r in range(dil):
        for cb in range(n):
            col = (r * n + cb) * PAIR
            out_ref[:, col:col + PAIR] = buf_ref.at[cb][pl.ds(r, tile // dil, stride=dil), :].astype(out_ref.dtype)


def _read_class_major(in_ref, buf_ref, dil):
    n, tile, _ = buf_ref.shape
    for r in range(dil):
        for cb in range(n):
            col = (r * n + cb) * PAIR
            buf_ref.at[cb][pl.ds(r, tile // dil, stride=dil), :] = in_ref[:, col:col + PAIR].astype(F32)
    return _load_lane_blocks(buf_ref)


def _class_rows(tm, width, dil):
    return _rows(tm // dil, dil * width)


def _in_proj(x, g, w):
    s_len, d = x.shape
    tm = ROW_TILE
    hw = HEAD_WIDTH

    def body(x_ref, g_ref, w_ref, h_ref, fqkv_ref, gx_ref, *rest):
        dil_refs, buf = rest[:-1], rest[-1]
        h = _rms_fwd(x_ref[...], g_ref[...]).astype(BF16)
        h_ref[...] = h
        proj = _dot(h, w_ref[...])
        fqkv_ref[:, 0:hw] = (proj[:, 0:hw] * QK_SCALE).astype(BF16)
        fqkv_ref[:, hw:3 * hw] = proj[:, hw:3 * hw].astype(BF16)
        gx_ref[...] = proj[:, 3 * hw:4 * hw]
        dqkv = jnp.concatenate([proj[:, 4 * hw:5 * hw] * QK_SCALE, proj[:, 5 * hw:7 * hw]], axis=1)
        _store_lane_blocks(buf, dqkv)
        for ref, dil in zip(dil_refs, DILATIONS):
            if dil == 1:
                ref[...] = dqkv.astype(BF16)
            else:
                _write_class_major(buf, ref, dil)

    return _call(
        body, name="in_proj", grid=(s_len // tm,),
        in_specs=[_rows(tm, d), _resident((1, d)), _resident(w.shape)],
        out_specs=[_rows(tm, d), _rows(tm, 3 * hw), _rows(tm, hw)] + [_class_rows(tm, 3 * hw, dil) for dil in DILATIONS],
        out_shape=[_sds((s_len, d), BF16), _sds((s_len, 3 * hw), BF16), _sds((s_len, hw), F32)]
        + [_sds((s_len // dil, dil * 3 * hw), BF16) for dil in DILATIONS],
        scratch=[pltpu.VMEM((3 * hw // PAIR, tm, PAIR), F32)],
    )(x, g, w)


def _swap_halves(x):
    return jnp.concatenate([pltpu.roll(x[:, i * PAIR:(i + 1) * PAIR], HEAD_DIM, 1) for i in range(x.shape[1] // PAIR)], axis=1)


def _split3(x):
    hi = x.astype(BF16)
    r = x - hi.astype(F32)
    mid = r.astype(BF16)
    lo = (r - mid.astype(F32)).astype(BF16)
    return hi, mid, lo


def _tri_dot(tri, x):
    return sum(_dot(tri, piece) for piece in _split3(x))


def _lane_in_head(width):
    return lax.broadcasted_iota(jnp.int32, (1, width), 1) % HEAD_DIM


def _place3(jj, first, pieces, base):
    out = base
    for i, p in enumerate(pieces):
        out = jnp.where(jj == first + i, p, out)
    return out


def _gate_scan(gx, b_exp):
    s_len, hw = gx.shape
    t = min(SCAN_TILE, s_len)
    tri = jnp.asarray(np.tril(np.ones((t, t), np.float32))).astype(BF16)

    def body(gx_ref, b_ref, tri_ref, aq_ref, ak_ref, c_ref, carry):
        @pl.when(pl.program_id(0) == 0)
        def _():
            carry[...] = jnp.zeros_like(carry)

        z = gx_ref[...] + b_ref[...]
        lf = jnp.minimum(z, 0.0) - jnp.log1p(jnp.exp(-jnp.abs(z)))
        c = _tri_dot(tri_ref[...], lf) + carry[...]
        carry[...] = c[t - 1:t, :]
        c_ref[...] = c
        hi, mid, lo = _split3(_swap_halves(c))
        jj = _lane_in_head(hw)
        zero = jnp.zeros_like(hi)
        one = jnp.ones_like(hi)
        aq_ref[...] = _place3(jj, 0, (hi, mid, lo), jnp.where(jj < 6, one, zero))
        ak_ref[...] = _place3(jj, 3, (-hi, -mid, -lo), jnp.where(jj < 9, one, zero))

    return _call(
        body, name="gate_scan", grid=(s_len // t,),
        in_specs=[_rows(t, hw), _resident((1, hw)), _resident((t, t))],
        out_specs=[_rows(t, hw), _rows(t, hw), _rows(t, hw)],
        out_shape=[_sds((s_len, hw), BF16), _sds((s_len, hw), BF16), _sds((s_len, hw), F32)],
        scratch=[pltpu.VMEM((1, hw), F32)],
    )(gx, b_exp, tri)


def _head_block_ones():
    head_of = np.arange(HEAD_WIDTH) // HEAD_DIM
    return jnp.asarray((head_of[:, None] == head_of[None, :]).astype(np.float32))


def _fox_block_stats(fqkv, c):
    s_len = fqkv.shape[0]
    hw = HEAD_WIDTH
    b = min(ATT_BLOCK, s_len)
    nb = s_len // b

    def body(q_ref, k_ref, c_ref, ones_ref, o_ref):
        q, k, cv = q_ref[...].astype(F32), k_ref[...].astype(F32), c_ref[...]
        seg = lambda x: _dot(x.astype(BF16), ones_ref[...])
        col_max = lambda x: jnp.max(x, axis=0, keepdims=True)
        col_min = lambda x: jnp.min(x, axis=0, keepdims=True)
        qn = 1.01 * jnp.sqrt(col_max(seg(q * q)))
        kn = 1.01 * jnp.sqrt(col_max(seg(k * k)))
        dmin = col_min(seg(q * k)) - (2.0 ** -8) * qn * kn
        o_ref[0] = jnp.concatenate([qn, col_max(cv) - dmin, kn, col_min(cv), jnp.zeros((4, hw), F32)], axis=0)

    stats = _call(
        body, name="fox_block_stats", grid=(nb,),
        in_specs=[pl.BlockSpec((b, hw), lambda i: (i, 0)), pl.BlockSpec((b, hw), lambda i: (i, 1)), _rows(b, hw),
                  _resident((hw, hw))],
        out_specs=pl.BlockSpec((1, 8, hw), lambda i: (i, 0, 0)),
        out_shape=_sds((nb, 8, hw), F32),
    )(fqkv, fqkv, c, _head_block_ones().astype(BF16))
    st = jnp.transpose(stats[:, :4, ::HEAD_DIM], (1, 2, 0))
    bound = st[0][:, :, None] * st[2][:, None, :] + st[1][:, :, None] - st[3][:, None, :]
    need_h = jnp.logical_not(bound < -FOX_SKIP_MARGIN)
    need = jnp.logical_or(need_h[0::2], need_h[1::2])
    ii = lax.broadcasted_iota(jnp.int32, (1, nb, nb), 1)
    jj = lax.broadcasted_iota(jnp.int32, (1, nb, nb), 2)
    first_needed = jnp.min(jnp.where(jnp.logical_or(jnp.logical_and(need, jj < ii), jj == ii), jj, nb), axis=2)
    window = ii[:, :, 0] - first_needed + 1
    in_window = jnp.logical_and(jj >= first_needed[:, :, None], jj <= ii)
    last_query = jnp.max(jnp.where(in_window, ii, 0), axis=1)
    return jnp.concatenate([st.reshape(4 * N_HEADS, nb), window.astype(F32), last_query.astype(F32)], axis=0)


FOX_WINDOW_ROW = 4 * N_HEADS
FOX_LAST_QUERY_ROW = 4 * N_HEADS + N_PAIRS


def _head_needed(st_ref, h, i, j):
    bound = st_ref[h, i] * st_ref[2 * N_HEADS + h, j] + st_ref[N_HEADS + h, i] - st_ref[3 * N_HEADS + h, j]
    return jnp.logical_not(bound < -FOX_SKIP_MARGIN)


def _lower_triangle(n):
    return lax.broadcasted_iota(jnp.int32, (n, n), 1) <= lax.broadcasted_iota(jnp.int32, (n, n), 0)


def _fox_spans(s_len):
    nq = s_len // min(ATT_BLOCK, s_len)
    return [s for s in FOX_SHORT_SPANS if s < nq] + [nq]


def _fox_span_index(stats, s_len):
    longest = jnp.max(stats[FOX_WINDOW_ROW:FOX_WINDOW_ROW + N_PAIRS])
    idx = jnp.int32(0)
    for s in _fox_spans(s_len)[:-1]:
        idx = idx + (longest > s).astype(jnp.int32)
    return idx


def _fox_fwd(span, fqkv, aq, ak, stats):
    s_len = fqkv.shape[0]
    bq = bk = min(ATT_BLOCK, s_len)
    nq = s_len // bq

    def body(st_ref, q_ref, k_ref, v_ref, aq_ref, ak_ref, o_ref, lse_ref, qa_ref, m_ref, acc_ref):
        pair, qi, back = pl.program_id(0), pl.program_id(1), pl.program_id(2)
        kj = jnp.maximum(qi - back, 0)
        in0 = _first_head_lanes()
        not0 = jnp.logical_not(in0)

        @pl.when(back == 0)
        def _():
            q2, a2 = q_ref[...], aq_ref[...]
            qa_ref[0] = jnp.where(in0, q2, a2)
            qa_ref[1] = jnp.where(in0, a2, q2)
            m_ref[...] = jnp.full_like(m_ref, NEG)
            acc_ref[...] = jnp.zeros_like(acc_ref)

        def head_step(a, mine, masked):
            k2, v2 = k_ref[...], v_ref[...]
            s = _dot_nt(qa_ref[a], jnp.where(mine, k2, ak_ref[...]))
            if masked:
                s = jnp.where(_lower_triangle(bq), s, NEG)
            m_old = m_ref[a]
            m_new = jnp.maximum(m_old, jnp.max(s, axis=1, keepdims=True))
            p = jnp.exp(s - jnp.tile(m_new, (1, bk // PAIR))).astype(BF16)
            acc_ref[a] = jnp.exp(m_old - m_new) * acc_ref[a] + _dot(p, jnp.where(mine, v2, jnp.ones_like(v2)))
            m_ref[a] = m_new

        @pl.when(back == 0)
        def _():
            head_step(0, in0, True)
            head_step(1, not0, True)

        below = jnp.logical_and(back > 0, back <= qi)
        for a, mine in ((0, in0), (1, not0)):
            @pl.when(jnp.logical_and(below, _head_needed(st_ref, 2 * pair + a, qi, kj)))
            def _():
                head_step(a, mine, False)

        @pl.when(back == jnp.minimum(qi, span - 1))
        def _():
            acc0, acc1 = acc_ref[0], acc_ref[1]
            l2 = pltpu.roll(_pick(in0, acc1, acc0), HEAD_DIM, 1)
            o_ref[...] = _pick(in0, acc0, acc1) / l2
            lse_ref[...] = _pick(in0, m_ref[0], m_ref[1]) + jnp.log(l2)

    blk = lambda rows: (rows, PAIR)
    qmap = lambda p, i, b, st: (i, p)

    def key_block(p, i, b, st):
        window = jnp.clip(st[FOX_WINDOW_ROW + p, i].astype(jnp.int32), 1, i + 1)
        return i - jnp.minimum(b, window - 1)

    return pl.pallas_call(
        body, name=f"fox_fwd_{span}",
        grid_spec=pltpu.PrefetchScalarGridSpec(
            num_scalar_prefetch=1, grid=(N_PAIRS, nq, span),
            in_specs=[
                pl.BlockSpec(blk(bq), qmap),
                pl.BlockSpec(blk(bk), lambda p, i, b, st: (key_block(p, i, b, st), N_PAIRS + p)),
                pl.BlockSpec(blk(bk), lambda p, i, b, st: (key_block(p, i, b, st), 2 * N_PAIRS + p)),
                pl.BlockSpec(blk(bq), qmap),
                pl.BlockSpec(blk(bk), lambda p, i, b, st: (key_block(p, i, b, st), p)),
            ],
            out_specs=[pl.BlockSpec(blk(bq), qmap), pl.BlockSpec(blk(bq), qmap)],
            scratch_shapes=[pltpu.VMEM((2, bq, PAIR), BF16), pltpu.VMEM((2, bq, PAIR), F32), pltpu.VMEM((2, bq, PAIR), F32)]),
        out_shape=[_sds((s_len, HEAD_WIDTH), F32), _sds((s_len, HEAD_WIDTH), F32)],
        compiler_params=pltpu.CompilerParams(dimension_semantics=("arbitrary",) * 3, vmem_limit_bytes=V7X_VMEM_LIMIT_BYTES),
    )(stats, fqkv, fqkv, fqkv, aq, ak)


def _t5_bucket(dist):
    max_exact = N_BUCKETS // 2
    d = np.maximum(dist, 1).astype(np.float32)
    large = max_exact + (np.log(d / max_exact) / np.log(MAX_DISTANCE / max_exact) * (N_BUCKETS - max_exact)).astype(np.int32)
    large = np.minimum(large, N_BUCKETS - 1)
    return np.where(dist < max_exact, dist, large).astype(np.int32)


def _dil_buckets():
    w = DIL_BLOCK
    qi = np.arange(w)[:, None]
    kj = np.arange(2 * w)[None, :]
    sub = qi + w - kj
    band = (sub >= 0) & (sub <= w)
    out = [np.where(band, _t5_bucket(np.clip(sub, 0, w) * dil), -1) for dil in DILATIONS]
    return np.stack(out).astype(np.int32)


def _dil_bias(rel_bias, buckets):
    w = DIL_BLOCK

    def body(rb_ref, bk_ref, o_ref):
        for p in range(len(DILATIONS)):
            bk = bk_ref[p]
            for h in range(N_HEADS):
                def add(b, acc):
                    return acc + jnp.where(bk == b, rb_ref[b, h], 0.0)
                acc = lax.fori_loop(0, N_BUCKETS, add, jnp.zeros((w, 2 * w), F32))
                o_ref[p, h] = jnp.where(bk < 0, NEG, acc)

    return pl.pallas_call(
        body, name="dil_bias",
        in_specs=[pl.BlockSpec(memory_space=pltpu.SMEM), pl.BlockSpec(memory_space=pltpu.VMEM)],
        out_specs=pl.BlockSpec(memory_space=pltpu.VMEM),
        out_shape=_sds((len(DILATIONS), N_HEADS, w, 2 * w), F32),
    )(rel_bias, buckets)


def _dil_fwd(view, bias, branch):
    dil = DILATIONS[branch]
    w = DIL_BLOCK
    hw = HEAD_WIDTH
    length = view.shape[0]
    nb = length // w

    def body(q_ref, kc_ref, kp_ref, vc_ref, vp_ref, b_ref, o_ref, lse_ref):
        n = pl.program_id(1)
        in0 = _first_head_lanes()
        not0 = jnp.logical_not(in0)
        pairs = [slice(pr * PAIR, (pr + 1) * PAIR) for pr in range(N_PAIRS)]
        tiles = []
        for sl in pairs:
            q2 = q_ref[:, sl]
            qq = jnp.concatenate([_zero_other(in0, q2), _zero_other(not0, q2)], axis=0)
            tiles.append(jnp.concatenate([_dot_nt(qq, kp_ref[:, sl]), _dot_nt(qq, kc_ref[:, sl])], axis=1))
        s = jnp.concatenate(tiles, axis=0) + b_ref[...].reshape(N_HEADS * w, 2 * w)
        prev_half = lax.broadcasted_iota(jnp.int32, (1, 2 * w), 1) < w
        s = jnp.where(jnp.logical_and(n == 0, prev_half), NEG, s)
        m = jnp.max(s, axis=1, keepdims=True)
        e = jnp.exp(s - m)
        l = jnp.sum(e, axis=1, keepdims=True)
        p = (e / l).astype(BF16)
        lse = m + jnp.log(l)
        for pr, sl in enumerate(pairs):
            pp = p[2 * pr * w:(2 * pr + 2) * w]
            o2 = _dot(pp[:, :w], vp_ref[:, sl]) + _dot(pp[:, w:], vc_ref[:, sl])
            o_ref[:, sl] = _pick(in0, o2[:w], o2[w:])
            lse_ref[:, sl] = _pick(in0, lse[2 * pr * w:(2 * pr + 1) * w], lse[(2 * pr + 1) * w:(2 * pr + 2) * w])

    prev = lambda n: jnp.maximum(n - 1, 0)
    out = pl.pallas_call(
        body, name=f"dil_fwd_{dil}", grid=(dil, nb),
        in_specs=[
            pl.BlockSpec((w, hw), lambda r, n: (n, 3 * r)),
            pl.BlockSpec((w, hw), lambda r, n: (n, 3 * r + 1)),
            pl.BlockSpec((w, hw), lambda r, n: (prev(n), 3 * r + 1)),
            pl.BlockSpec((w, hw), lambda r, n: (n, 3 * r + 2)),
            pl.BlockSpec((w, hw), lambda r, n: (prev(n), 3 * r + 2)),
            pl.BlockSpec((None, N_HEADS, w, 2 * w), lambda r, n: (branch, 0, 0, 0)),
        ],
        out_specs=[pl.BlockSpec((w, hw), lambda r, n: (n, r)), pl.BlockSpec((w, hw), lambda r, n: (n, r))],
        out_shape=[_sds((length, dil * hw), F32), _sds((length, dil * hw), F32)],
        compiler_params=pltpu.CompilerParams(dimension_semantics=("arbitrary",) * 2, vmem_limit_bytes=V7X_VMEM_LIMIT_BYTES),
    )(view, view, view, view, view, bias)
    return out[0], out[1]


def _mix_out(o_fox, o_br, lse_br, w_out, x, g_post):
    s_len, d = x.shape
    hw = HEAD_WIDTH
    tm = ROW_TILE

    def body(of_ref, o1, o2, o3, l1, l2, l3, w_ref, x_ref, g_ref, x1_ref, y1_ref, od_ref, lj1, lj2, lj3, buf):
        def natural(ref, dil):
            return ref[...] if dil == 1 else _read_class_major(ref, buf, dil)

        ob = [natural(r, dil) for r, dil in zip((o1, o2, o3), DILATIONS)]
        la, lb, lc = [natural(r, dil) for r, dil in zip((l1, l2, l3), DILATIONS)]
        m = jnp.maximum(jnp.maximum(la, lb), lc)
        ea, eb, ec = jnp.exp(la - m), jnp.exp(lb - m), jnp.exp(lc - m)
        tot = ea + eb + ec
        o_dil = (ea / tot) * ob[0] + (eb / tot) * ob[1] + (ec / tot) * ob[2]
        od_ref[...] = o_dil
        lj = m + jnp.log(tot)
        _store_lane_blocks(buf, lj)
        for ref, dil in zip((lj1, lj2, lj3), DILATIONS):
            if dil == 1:
                ref[...] = lj
            else:
                _write_class_major(buf, ref, dil)
        y = _dot(of_ref[...].astype(BF16), w_ref[0:hw, :]) + _dot(o_dil.astype(BF16), w_ref[hw:2 * hw, :])
        y1_ref[...] = y
        x1_ref[...] = x_ref[...] + _rms_fwd(y, g_ref[...])

    half = _rows(tm, hw)
    by_dil = [_class_rows(tm, hw, dil) for dil in DILATIONS]
    return _call(
        body, name="mix_out", grid=(s_len // tm,),
        in_specs=[half] + by_dil + by_dil + [_resident(w_out.shape), _rows(tm, d), _resident((1, d))],
        out_specs=[_rows(tm, d), _rows(tm, d), half] + by_dil,
        out_shape=[_sds((s_len, d), F32), _sds((s_len, d), F32), _sds((s_len, hw), F32)]
        + [_sds((s_len // dil, dil * hw), F32) for dil in DILATIONS],
        scratch=[pltpu.VMEM((hw // PAIR, tm, PAIR), F32)],
    )(o_fox, *o_br, *lse_br, w_out, x, g_post)


def _mem_fwd(mem, g_mem, w_xk, w_xv):
    n_mem, d = mem.shape
    mw = w_xk.shape[1]

    def body(mem_ref, g_ref, wk_ref, wv_ref, hm_ref, k_ref, v_ref):
        hm = _rms_fwd(mem_ref[...], g_ref[...]).astype(BF16)
        hm_ref[...] = hm
        k_ref[...] = _dot(hm, wk_ref[...]).astype(BF16)
        v_ref[...] = _dot(hm, wv_ref[...]).astype(BF16)

    return pl.pallas_call(
        body, name="mem_fwd",
        out_shape=[_sds((n_mem, d), BF16), _sds((n_mem, mw), BF16), _sds((n_mem, mw), BF16)],
    )(mem, g_mem, w_xk, w_xv)


def _xattn_softmax(qa, k2):
    s = _dot_nt(qa, k2)
    m = jnp.max(s, axis=1, keepdims=True)
    e = jnp.exp(s - m)
    return e / jnp.sum(e, axis=1, keepdims=True)


def _xattn_fwd(x1, g_pre, w_xq, kx, vx, w_xo, g_post):
    s_len, d = x1.shape
    mw = w_xq.shape[1]
    n_mem = kx.shape[0]
    tm = ROW_TILE

    def body(x_ref, gp_ref, wq_ref, k_ref, v_ref, wo_ref, go_ref, x2_ref, y2_ref, h2_ref, q_ref, o_ref):
        x = x_ref[...]
        h = _rms_fwd(x, gp_ref[...]).astype(BF16)
        h2_ref[...] = h
        q = (_dot(h, wq_ref[...]) * QK_SCALE).astype(BF16)
        q_ref[...] = q
        in0 = _first_head_lanes()
        not0 = jnp.logical_not(in0)
        for pr in range(mw // PAIR):
            sl = slice(pr * PAIR, (pr + 1) * PAIR)
            q2, k2, v2 = q[:, sl], k_ref[:, sl], v_ref[:, sl]
            oa = [_dot(_xattn_softmax(_zero_other(mine, q2), k2).astype(BF16), v2) for mine in (in0, not0)]
            o_ref[:, sl] = _pick(in0, oa[0], oa[1]).astype(BF16)
        y = _dot(o_ref[...], wo_ref[...])
        y2_ref[...] = y
        x2_ref[...] = x + _rms_fwd(y, go_ref[...])

    return _call(
        body, name="xattn_fwd", grid=(s_len // tm,),
        in_specs=[_rows(tm, d), _resident((1, d)), _resident(w_xq.shape), _resident((n_mem, mw)), _resident((n_mem, mw)),
                  _resident(w_xo.shape), _resident((1, d))],
        out_specs=[_rows(tm, d), _rows(tm, d), _rows(tm, d), _rows(tm, mw), _rows(tm, mw)],
        out_shape=[_sds((s_len, d), F32), _sds((s_len, d), F32), _sds((s_len, d), BF16), _sds((s_len, mw), BF16),
                   _sds((s_len, mw), BF16)],
    )(x1, g_pre, w_xq, kx, vx, w_xo, g_post)


def _ffn_up(x2, g_pre, w_gate, w_up):
    s_len, d = x2.shape
    dff = w_gate.shape[1]
    tm = ROW_TILE

    def body(x_ref, g_ref, wg_ref, wu_ref, h_ref, a_ref, u_ref, z_ref):
        h = _rms_fwd(x_ref[...], g_ref[...]).astype(BF16)
        h_ref[...] = h
        a = _dot(h, wg_ref[...])
        u = _dot(h, wu_ref[...])
        a_ref[...] = a.astype(BF16)
        u_ref[...] = u.astype(BF16)
        z_ref[...] = ((a * jax.nn.sigmoid(a)) * u).astype(BF16)

    return _call(
        body, name="ffn_up", grid=(s_len // tm,),
        in_specs=[_rows(tm, d), _resident((1, d)), _resident(w_gate.shape), _resident(w_up.shape)],
        out_specs=[_rows(tm, d), _rows(tm, dff), _rows(tm, dff), _rows(tm, dff)],
        out_shape=[_sds((s_len, d), BF16)] + [_sds((s_len, dff), BF16)] * 3,
    )(x2, g_pre, w_gate, w_up)


def _ffn_down_loss(z, w_down, x2, g_post, target):
    s_len, d = x2.shape
    dff = z.shape[1]
    tm = ROW_TILE

    def body(z_ref, w_ref, x_ref, g_ref, t_ref, y_ref, dx_ref, sq_ref):
        @pl.when(pl.program_id(0) == 0)
        def _():
            sq_ref[...] = jnp.zeros_like(sq_ref)

        y = _dot(z_ref[...], w_ref[...])
        y_ref[...] = y
        err = (x_ref[...] + _rms_fwd(y, g_ref[...])) - t_ref[...]
        sq_ref[...] += jnp.sum(err * err, axis=0, keepdims=True)
        dx_ref[...] = err * (1.0 / d)

    return _call(
        body, name="ffn_down_loss", grid=(s_len // tm,),
        in_specs=[_rows(tm, dff), _resident(w_down.shape), _rows(tm, d), _resident((1, d)), _rows(tm, d)],
        out_specs=[_rows(tm, d), _rows(tm, d), _acc_out((1, d))],
        out_shape=[_sds((s_len, d), F32), _sds((s_len, d), F32), _sds((1, d), F32)],
    )(z, w_down, x2, g_post, target)


def _weight_grad(a, b, name):
    s_len, k = a.shape
    n = b.shape[1]
    ts = 512 if s_len % 512 == 0 else s_len
    tn = n
    while k * tn * 4 > 8 * 2 ** 20 and tn % 256 == 0:
        tn //= 2

    def body(a_ref, b_ref, o_ref):
        @pl.when(pl.program_id(1) == 0)
        def _():
            o_ref[...] = jnp.zeros_like(o_ref)

        o_ref[...] += _dot_tn(a_ref[...].astype(BF16), b_ref[...].astype(BF16))

    return pl.pallas_call(
        body, name=name, grid=(n // tn, s_len // ts),
        in_specs=[pl.BlockSpec((ts, k), lambda j, i: (i, 0)), pl.BlockSpec((ts, tn), lambda j, i: (i, j))],
        out_specs=pl.BlockSpec((k, tn), lambda j, i: (0, j)),
        out_shape=_sds((k, n), F32),
        compiler_params=pltpu.CompilerParams(dimension_semantics=("arbitrary",) * 2, vmem_limit_bytes=V7X_VMEM_LIMIT_BYTES),
    )(a, b)


def _ffn_bwd_a(dx3, y3, g_post, w_down_t, a, u):
    s_len, d = dx3.shape
    dff = a.shape[1]
    tm = ROW_TILE

    def body(dx_ref, y_ref, g_ref, w_ref, a_ref, u_ref, dy_ref, da_ref, du_ref, dg_ref):
        @pl.when(pl.program_id(0) == 0)
        def _():
            dg_ref[...] = jnp.zeros_like(dg_ref)

        dy, dg = _rms_bwd(y_ref[...], g_ref[...], dx_ref[...])
        dg_ref[...] += dg
        dyb = dy.astype(BF16)
        dy_ref[...] = dyb
        dz = _dot(dyb, w_ref[...])
        av = a_ref[...].astype(F32)
        uv = u_ref[...].astype(F32)
        sg = jax.nn.sigmoid(av)
        da_ref[...] = (dz * uv * (sg * (1.0 + av * (1.0 - sg)))).astype(BF16)
        du_ref[...] = (dz * (av * sg)).astype(BF16)

    return _call(
        body, name="ffn_bwd_a", grid=(s_len // tm,),
        in_specs=[_rows(tm, d), _rows(tm, d), _resident((1, d)), _resident(w_down_t.shape), _rows(tm, dff), _rows(tm, dff)],
        out_specs=[_rows(tm, d), _rows(tm, dff), _rows(tm, dff), _acc_out((1, d))],
        out_shape=[_sds((s_len, d), BF16), _sds((s_len, dff), BF16), _sds((s_len, dff), BF16), _sds((1, d), F32)],
    )(dx3, y3, g_post, w_down_t, a, u)


def _ffn_bwd_b(da, du, w_gate_t, w_up_t, dx3, x2, g_pre):
    s_len, d = x2.shape
    dff = da.shape[1]
    tm = ROW_TILE

    def body(da_ref, du_ref, wg_ref, wu_ref, dx_ref, x_ref, g_ref, o_ref, dg_ref):
        @pl.when(pl.program_id(0) == 0)
        def _():
            dg_ref[...] = jnp.zeros_like(dg_ref)

        dh = _dot(da_ref[...], wg_ref[...]) + _dot(du_ref[...], wu_ref[...])
        dx, dg = _rms_bwd(x_ref[...], g_ref[...], dh)
        dg_ref[...] += dg
        o_ref[...] = dx_ref[...] + dx

    return _call(
        body, name="ffn_bwd_b", grid=(s_len // tm,),
        in_specs=[_rows(tm, dff), _rows(tm, dff), _resident(w_gate_t.shape), _resident(w_up_t.shape), _rows(tm, d),
                  _rows(tm, d), _resident((1, d))],
        out_specs=[_rows(tm, d), _acc_out((1, d))],
        out_shape=[_sds((s_len, d), F32), _sds((1, d), F32)],
    )(da, du, w_gate_t, w_up_t, dx3, x2, g_pre)


def _xattn_bwd(dx2, y2, g_post, w_xo_t, q, kx, vx, w_xq_t, x1, g_pre):
    s_len, d = x1.shape
    mw = q.shape[1]
    n_mem = kx.shape[0]
    tm = ROW_TILE

    def body(dx_ref, y_ref, go_ref, wo_ref, q_ref, k_ref, v_ref, wq_ref, x_ref, gp_ref,
             dx1_ref, dy_ref, dq_ref, dk_ref, dv_ref, dgo_ref, dgp_ref):
        @pl.when(pl.program_id(0) == 0)
        def _():
            dk_ref[...] = jnp.zeros_like(dk_ref)
            dv_ref[...] = jnp.zeros_like(dv_ref)
            dgo_ref[...] = jnp.zeros_like(dgo_ref)
            dgp_ref[...] = jnp.zeros_like(dgp_ref)

        dxin = dx_ref[...]
        dy, dgo = _rms_bwd(y_ref[...], go_ref[...], dxin)
        dgo_ref[...] += dgo
        dyb = dy.astype(BF16)
        dy_ref[...] = dyb
        do = _dot(dyb, wo_ref[...]).astype(BF16)
        in0 = _first_head_lanes()
        not0 = jnp.logical_not(in0)
        for pr in range(mw // PAIR):
            sl = slice(pr * PAIR, (pr + 1) * PAIR)
            q2, k2, v2, do2 = q_ref[:, sl], k_ref[:, sl], v_ref[:, sl], do[:, sl]
            dqs = []
            dk2 = jnp.zeros((n_mem, PAIR), F32)
            dv2 = jnp.zeros((n_mem, PAIR), F32)
            for mine in (in0, not0):
                qa = _zero_other(mine, q2)
                doa = _zero_other(mine, do2)
                p = _xattn_softmax(qa, k2)
                dp = _dot_nt(doa, v2)
                ds = (p * (dp - jnp.sum(p * dp, axis=1, keepdims=True))).astype(BF16)
                dqs.append(_dot(ds, k2))
                dk2 = dk2 + _dot_tn(ds, qa)
                dv2 = dv2 + _dot_tn(p.astype(BF16), doa)
            dq_ref[:, sl] = (_pick(in0, dqs[0], dqs[1]) * QK_SCALE).astype(BF16)
            dk_ref[:, sl] += dk2
            dv_ref[:, sl] += dv2
        dh = _dot(dq_ref[...], wq_ref[...])
        dx, dgp = _rms_bwd(x_ref[...], gp_ref[...], dh)
        dgp_ref[...] += dgp
        dx1_ref[...] = dxin + dx

    return _call(
        body, name="xattn_bwd", grid=(s_len // tm,),
        in_specs=[_rows(tm, d), _rows(tm, d), _resident((1, d)), _resident(w_xo_t.shape), _rows(tm, mw),
                  _resident((n_mem, mw)), _resident((n_mem, mw)), _resident(w_xq_t.shape), _rows(tm, d), _resident((1, d))],
        out_specs=[_rows(tm, d), _rows(tm, d), _rows(tm, mw), _acc_out((n_mem, mw)), _acc_out((n_mem, mw)),
                   _acc_out((1, d)), _acc_out((1, d))],
        out_shape=[_sds((s_len, d), F32), _sds((s_len, d), BF16), _sds((s_len, mw), BF16), _sds((n_mem, mw), F32),
                   _sds((n_mem, mw), F32), _sds((1, d), F32), _sds((1, d), F32)],
    )(dx2, y2, g_post, w_xo_t, q, kx, vx, w_xq_t, x1, g_pre)


def _mem_bwd(dk, dv, w_xk_t, w_xv_t, hm, mem, g_mem):
    n_mem, d = mem.shape
    mw = dk.shape[1]

    def body(dk_ref, dv_ref, wk_ref, wv_ref, hm_ref, mem_ref, g_ref, dwk_ref, dwv_ref, dg_ref):
        dkb = dk_ref[...].astype(BF16)
        dvb = dv_ref[...].astype(BF16)
        dhm = _dot(dkb, wk_ref[...]) + _dot(dvb, wv_ref[...])
        _, dg = _rms_bwd(mem_ref[...], g_ref[...], dhm)
        dg_ref[...] = dg
        dwk_ref[...] = _dot_tn(hm_ref[...], dkb)
        dwv_ref[...] = _dot_tn(hm_ref[...], dvb)

    return pl.pallas_call(
        body, name="mem_bwd",
        out_shape=[_sds((d, mw), F32), _sds((d, mw), F32), _sds((1, d), F32)],
    )(dk, dv, w_xk_t, w_xv_t, hm, mem, g_mem)


def _mix_out_bwd(dx1, y1, g_post, w_out_t, o_fox, o_dil):
    s_len, d = dx1.shape
    hw = HEAD_WIDTH
    tm = ROW_TILE
    ones = _head_block_ones().astype(BF16)

    def body(dx_ref, y_ref, g_ref, w_ref, of_ref, od_ref, ones_ref, dy_ref, dof_ref, dlf_ref, dg_ref,
             dod1, dod2, dod3, dld1, dld2, dld3, buf):
        @pl.when(pl.program_id(0) == 0)
        def _():
            dg_ref[...] = jnp.zeros_like(dg_ref)

        dy, dg = _rms_bwd(y_ref[...], g_ref[...], dx_ref[...])
        dg_ref[...] += dg
        dyb = dy.astype(BF16)
        dy_ref[...] = dyb
        do = _dot(dyb, w_ref[...])
        def head_sums(x):
            hi = x.astype(BF16)
            lo = (x - hi.astype(F32)).astype(BF16)
            return _dot(hi, ones_ref[...]) + _dot(lo, ones_ref[...])

        dof_ref[...] = do[:, 0:hw].astype(BF16)
        dlf_ref[...] = head_sums(do[:, 0:hw] * of_ref[...])
        do_dil = do[:, hw:2 * hw]
        dl_dil = head_sums(do_dil * od_ref[...])
        for val, refs in ((do_dil, (dod1, dod2, dod3)), (dl_dil, (dld1, dld2, dld3))):
            _store_lane_blocks(buf, val)
            for ref, dil in zip(refs, DILATIONS):
                if dil == 1:
                    ref[...] = val.astype(ref.dtype)
                else:
                    _write_class_major(buf, ref, dil)

    half = _rows(tm, hw)
    by_dil = [_class_rows(tm, hw, dil) for dil in DILATIONS]
    outs = _call(
        body, name="mix_out_bwd", grid=(s_len // tm,),
        in_specs=[_rows(tm, d), _rows(tm, d), _resident((1, d)), _resident(w_out_t.shape), half, half, _resident((hw, hw))],
        out_specs=[_rows(tm, d), half, half, _acc_out((1, d))] + by_dil + by_dil,
        out_shape=[_sds((s_len, d), BF16), _sds((s_len, hw), BF16), _sds((s_len, hw), F32), _sds((1, d), F32)]
        + [_sds((s_len // dil, dil * hw), BF16) for dil in DILATIONS]
        + [_sds((s_len // dil, dil * hw), F32) for dil in DILATIONS],
        scratch=[pltpu.VMEM((hw // PAIR, tm, PAIR), F32)],
    )(dx1, y1, g_post, w_out_t, o_fox, o_dil, ones)
    return outs[0], outs[1], outs[2], outs[3], outs[4:7], outs[7:10]


def _fox_bwd_prep(aq, lse, delta):
    s_len, hw = aq.shape
    tm = ROW_TILE

    def body(aq_ref, lse_ref, dl_ref, aql_ref, ad_ref):
        jj = _lane_in_head(hw)
        l3 = _split3(_swap_halves(lse_ref[...]))
        aql_ref[...] = _place3(jj, 6, [-p for p in l3], aq_ref[...])
        d3 = _split3(_swap_halves(dl_ref[...]))
        ad_ref[...] = _place3(jj, 0, [-p for p in d3], jnp.zeros((tm, hw), BF16))

    half = _rows(tm, hw)
    return _call(
        body, name="fox_bwd_prep", grid=(s_len // tm,),
        in_specs=[half, half, half], out_specs=[half, half],
        out_shape=[_sds((s_len, hw), BF16), _sds((s_len, hw), BF16)],
    )(aq, lse, delta)


def _ones_on_first3(shape):
    jj = lax.broadcasted_iota(jnp.int32, shape, 1) % HEAD_DIM
    return jnp.where(jj < 3, 1.0, 0.0).astype(BF16)


def _fox_bwd(span, fqkv, do, aql, ak, ad, stats):
    s_len = fqkv.shape[0]
    bq = bk = min(ATT_BLOCK, s_len)
    nq = nk = s_len // bq

    def body(st_ref, q_ref, k_ref, v_ref, do_ref, aql_ref, ak_ref, ad_ref, dq_ref, rs_ref, dk_ref, dv_ref, dc_ref,
             ka_ref, va_ref, kone_ref, r_ref, dvacc_ref, dqacc_ref):
        pair, kj, ahead = pl.program_id(0), pl.program_id(1), pl.program_id(2)
        valid = kj + ahead < nq
        qi = jnp.minimum(kj + ahead, nq - 1)
        in0 = _first_head_lanes()
        not0 = jnp.logical_not(in0)
        heads = ((0, in0), (1, not0))
        rows = pl.ds(pl.multiple_of(qi * bq, bq), bq)

        @pl.when(ahead == 0)
        def _():
            k2, v2, a2 = k_ref[...], v_ref[...], ak_ref[...]
            one = jnp.ones_like(k2)
            one3 = _ones_on_first3(v2.shape)
            for a, mine in heads:
                ka_ref[a] = jnp.where(mine, k2, a2)
                va_ref[a] = jnp.where(mine, v2, one3)
                kone_ref[a] = jnp.where(mine, k2, one)
            r_ref[...] = jnp.zeros_like(r_ref)
            dvacc_ref[...] = jnp.zeros_like(dvacc_ref)

        @pl.when(jnp.logical_and(valid, jnp.logical_or(kj == 0, ahead == span - 1)))
        def _():
            for a, _ in heads:
                dqacc_ref[a, rows, :] = jnp.zeros((bq, PAIR), F32)

        def head_step(a, mine, masked):
            q2 = q_ref[...]
            doa = jnp.where(mine, do_ref[...], ad_ref[...])
            s = _dot_nt(jnp.where(mine, q2, aql_ref[...]), ka_ref[a])
            if masked:
                s = jnp.where(_lower_triangle(bq), s, NEG)
            p = jnp.exp(s)
            ds = (p * _dot_nt(doa, va_ref[a])).astype(BF16)
            dvacc_ref[a] += _dot_tn(p.astype(BF16), doa)
            r_ref[a] += _dot_tn(ds, jnp.where(mine, q2, jnp.ones_like(q2)))
            dqacc_ref[a, rows, :] += _dot(ds, kone_ref[a])

        for a, mine in heads:
            @pl.when(jnp.logical_and(jnp.logical_and(valid, ahead > 0), _head_needed(st_ref, 2 * pair + a, qi, kj)))
            def _():
                head_step(a, mine, False)

        @pl.when(ahead == 0)
        def _():
            for a, mine in heads:
                head_step(a, mine, True)
            acc0, acc1 = dqacc_ref[0, rows, :], dqacc_ref[1, rows, :]
            dq_ref[...] = (_pick(in0, acc0, acc1) * QK_SCALE).astype(BF16)
            rs_ref[...] = pltpu.roll(_pick(in0, acc1, acc0), HEAD_DIM, 1)

        @pl.when(ahead == span - 1)
        def _():
            dk_ref[...] = _pick(in0, r_ref[0], r_ref[1]).astype(BF16)
            dv_ref[...] = _pick(in0, dvacc_ref[0], dvacc_ref[1]).astype(BF16)
            dc_ref[...] = -pltpu.roll(_pick(in0, r_ref[1], r_ref[0]), HEAD_DIM, 1)

    blk = lambda n: (n, PAIR)
    kvmap = lambda p, j, t, st: (j, p)

    def qmap(p, j, t, st):
        last = jnp.clip(st[FOX_LAST_QUERY_ROW + p, j].astype(jnp.int32), j, nq - 1)
        return (jnp.minimum(j + t, last), p)

    return pl.pallas_call(
        body, name=f"fox_bwd_{span}",
        grid_spec=pltpu.PrefetchScalarGridSpec(
            num_scalar_prefetch=1, grid=(N_PAIRS, nk, span),
            in_specs=[
                pl.BlockSpec(blk(bq), qmap),
                pl.BlockSpec(blk(bk), lambda p, j, t, st: (j, N_PAIRS + p)),
                pl.BlockSpec(blk(bk), lambda p, j, t, st: (j, 2 * N_PAIRS + p)),
                pl.BlockSpec(blk(bq), qmap), pl.BlockSpec(blk(bq), qmap),
                pl.BlockSpec(blk(bk), kvmap),
                pl.BlockSpec(blk(bq), qmap),
            ],
            out_specs=[pl.BlockSpec(blk(bk), kvmap)] * 5,
            scratch_shapes=[pltpu.VMEM((2, bk, PAIR), BF16), pltpu.VMEM((2, bk, PAIR), BF16), pltpu.VMEM((2, bk, PAIR), BF16),
                            pltpu.VMEM((2, bk, PAIR), F32), pltpu.VMEM((2, bk, PAIR), F32), pltpu.VMEM((2, s_len, PAIR), F32)]),
        out_shape=[_sds((s_len, HEAD_WIDTH), BF16), _sds((s_len, HEAD_WIDTH), F32), _sds((s_len, HEAD_WIDTH), BF16),
                   _sds((s_len, HEAD_WIDTH), BF16), _sds((s_len, HEAD_WIDTH), F32)],
        compiler_params=pltpu.CompilerParams(dimension_semantics=("arbitrary",) * 3, vmem_limit_bytes=V7X_VMEM_LIMIT_BYTES),
    )(stats, fqkv, fqkv, fqkv, do, aql, ak, ad)


def _gate_bwd(rs, dc, gx, b_exp):
    s_len, hw = gx.shape
    t = min(SCAN_TILE, s_len)
    nt = s_len // t
    tri = jnp.asarray(np.triu(np.ones((t, t), np.float32))).astype(BF16)

    def body(rs_ref, dc_ref, gx_ref, b_ref, tri_ref, dgx_ref, db_ref, carry):
        @pl.when(pl.program_id(0) == 0)
        def _():
            carry[...] = jnp.zeros_like(carry)
            db_ref[...] = jnp.zeros_like(db_ref)

        dlf = _tri_dot(tri_ref[...], rs_ref[...] + dc_ref[...]) + carry[...]
        carry[...] = dlf[0:1, :]
        dgate = dlf * jax.nn.sigmoid(-(gx_ref[...] + b_ref[...]))
        db_ref[...] += jnp.sum(dgate, axis=0, keepdims=True)
        lane = lax.broadcasted_iota(jnp.int32, (1, hw), 1)
        dgx_ref[...] = jnp.where(lane % HEAD_DIM == 0, dgate, 0.0).astype(BF16)

    rev = lambda i: (nt - 1 - i, 0)
    return _call(
        body, name="gate_bwd", grid=(nt,),
        in_specs=[pl.BlockSpec((t, hw), rev)] * 3 + [_resident((1, hw)), _resident((t, t))],
        out_specs=[pl.BlockSpec((t, hw), rev), _acc_out((1, hw))],
        out_shape=[_sds((s_len, hw), BF16), _sds((1, hw), F32)],
        scratch=[pltpu.VMEM((1, hw), F32)],
    )(rs, dc, gx, b_exp, tri)


def _dil_bwd(qkv_v, do_v, lj_v, dl_v, bias, branch):
    dil = DILATIONS[branch]
    w = DIL_BLOCK
    hw = HEAD_WIDTH
    length = qkv_v.shape[0]
    nb = length // w

    def body(q0_ref, q1_ref, kp_ref, kc_ref, vp_ref, vc_ref, do0_ref, do1_ref, l0_ref, l1_ref, d0_ref, d1_ref, b_ref,
             dq_ref, dk_ref, dv_ref, dsum_ref):
        r, n = pl.program_id(0), pl.program_id(1)
        in0 = _first_head_lanes()
        not0 = jnp.logical_not(in0)
        first = n == 0
        last = n == nb - 1

        @pl.when(jnp.logical_and(r == 0, n == 0))
        def _():
            dsum_ref[...] = jnp.zeros_like(dsum_ref)

        pairs = [slice(pr * PAIR, (pr + 1) * PAIR) for pr in range(N_PAIRS)]

        def both_heads(ref, sl):
            v = ref[:, sl]
            return jnp.concatenate([_zero_other(in0, v), _zero_other(not0, v)], axis=0)

        def head_columns(ref):
            return jnp.concatenate([ref[:, h * HEAD_DIM:h * HEAD_DIM + 1] for h in range(N_HEADS)], axis=0)

        qq0 = [both_heads(q0_ref, sl) for sl in pairs]
        qq1 = [both_heads(q1_ref, sl) for sl in pairs]
        dd0 = [both_heads(do0_ref, sl) for sl in pairs]
        dd1 = [both_heads(do1_ref, sl) for sl in pairs]
        stack = lambda tiles: jnp.concatenate(tiles, axis=0)
        s_a = stack([_dot_nt(qq0[i], kp_ref[:, sl]) for i, sl in enumerate(pairs)])
        s_b = stack([_dot_nt(qq0[i], kc_ref[:, sl]) for i, sl in enumerate(pairs)])
        s_c = stack([_dot_nt(qq1[i], kc_ref[:, sl]) for i, sl in enumerate(pairs)])
        dp_a = stack([_dot_nt(dd0[i], vp_ref[:, sl]) for i, sl in enumerate(pairs)])
        dp_b = stack([_dot_nt(dd0[i], vc_ref[:, sl]) for i, sl in enumerate(pairs)])
        dp_c = stack([_dot_nt(dd1[i], vc_ref[:, sl]) for i, sl in enumerate(pairs)])
        bias2 = b_ref[...].reshape(N_HEADS * w, 2 * w)
        b_prev, b_cur = bias2[:, 0:w], bias2[:, w:2 * w]
        lse0, lse1 = head_columns(l0_ref), head_columns(l1_ref)
        dl0, dl1 = head_columns(d0_ref), head_columns(d1_ref)
        p_a = jnp.exp(jnp.where(first, NEG, s_a + b_prev) - lse0)
        p_b = jnp.exp((s_b + b_cur) - lse0)
        p_c = jnp.exp(jnp.where(last, NEG, s_c + b_prev) - lse1)
        ds_a = p_a * (dp_a - dl0)
        ds_b = p_b * (dp_b - dl0)
        ds_c = p_c * (dp_c - dl1)
        dsum_ref[...] += jnp.concatenate([ds_a, ds_b], axis=1).reshape(N_HEADS, w, 2 * w)
        ds_a, ds_b, ds_c = ds_a.astype(BF16), ds_b.astype(BF16), ds_c.astype(BF16)
        p_b, p_c = p_b.astype(BF16), p_c.astype(BF16)
        for i, sl in enumerate(pairs):
            rows = slice(2 * i * w, (2 * i + 2) * w)
            dq2 = _dot(ds_a[rows], kp_ref[:, sl]) + _dot(ds_b[rows], kc_ref[:, sl])
            dq_ref[:, sl] = _pick(in0, dq2[:w], dq2[w:]).astype(BF16)
            dk_ref[:, sl] = (_dot_tn(ds_b[rows], qq0[i]) + _dot_tn(ds_c[rows], qq1[i])).astype(BF16)
            dv_ref[:, sl] = (_dot_tn(p_b[rows], dd0[i]) + _dot_tn(p_c[rows], dd1[i])).astype(BF16)

    prev = lambda n: jnp.maximum(n - 1, 0)
    nxt = lambda n: jnp.minimum(n + 1, nb - 1)
    blk = (w, hw)
    outs = pl.pallas_call(
        body, name=f"dil_bwd_{dil}", grid=(dil, nb),
        in_specs=[
            pl.BlockSpec(blk, lambda r, n: (n, 3 * r)),
            pl.BlockSpec(blk, lambda r, n: (nxt(n), 3 * r)),
            pl.BlockSpec(blk, lambda r, n: (prev(n), 3 * r + 1)),
            pl.BlockSpec(blk, lambda r, n: (n, 3 * r + 1)),
            pl.BlockSpec(blk, lambda r, n: (prev(n), 3 * r + 2)),
            pl.BlockSpec(blk, lambda r, n: (n, 3 * r + 2)),
            pl.BlockSpec(blk, lambda r, n: (n, r)),
            pl.BlockSpec(blk, lambda r, n: (nxt(n), r)),
            pl.BlockSpec(blk, lambda r, n: (n, r)),
            pl.BlockSpec(blk, lambda r, n: (nxt(n), r)),
            pl.BlockSpec(blk, lambda r, n: (n, r)),
            pl.BlockSpec(blk, lambda r, n: (nxt(n), r)),
            pl.BlockSpec((None, N_HEADS, w, 2 * w), lambda r, n: (branch, 0, 0, 0)),
        ],
        out_specs=[pl.BlockSpec(blk, lambda r, n: (n, r))] * 3 + [pl.BlockSpec((N_HEADS, w, 2 * w), lambda r, n: (0, 0, 0))],
        out_shape=[_sds((length, dil * hw), BF16)] * 3 + [_sds((N_HEADS, w, 2 * w), F32)],
        compiler_params=pltpu.CompilerParams(dimension_semantics=("arbitrary",) * 2, vmem_limit_bytes=V7X_VMEM_LIMIT_BYTES),
    )(qkv_v, qkv_v, qkv_v, qkv_v, qkv_v, qkv_v, do_v, do_v, lj_v, lj_v, dl_v, dl_v, bias)
    return list(outs)


def _rel_bias_grad(dsum, buckets):
    w = DIL_BLOCK

    def body(ds_ref, bk_ref, o_ref):
        row = lax.broadcasted_iota(jnp.int32, (N_BUCKETS, PAIR), 0)
        lane = lax.broadcasted_iota(jnp.int32, (N_BUCKETS, PAIR), 1)

        def per_bucket(b, acc):
            for p in range(len(DILATIONS)):
                hit = bk_ref[p] == b
                for h in range(N_HEADS):
                    part = jnp.where(hit, ds_ref[p, h], 0.0)
                    tot = jnp.sum(jnp.sum(part, axis=1, keepdims=True), axis=0, keepdims=True)
                    acc = acc + jnp.where(jnp.logical_and(row == b, lane == h), tot, 0.0)
            return acc

        o_ref[...] = lax.fori_loop(0, N_BUCKETS, per_bucket, jnp.zeros((N_BUCKETS, PAIR), F32))

    return pl.pallas_call(body, name="rel_bias_grad", out_shape=_sds((N_BUCKETS, PAIR), F32))(dsum, buckets)


def _in_proj_bwd(dfq, dfk, dfv, dgx, ddq, ddk, ddv, w_in_t, dx1, x, g_pre):
    s_len, d = x.shape
    hw = HEAD_WIDTH
    tm = ROW_TILE

    def body(fq, fk, fv, gx, q1, q2, q3, k1, k2, k3, v1, v2, v3, w_ref, dx_ref, x_ref, g_ref, o_ref, dp_ref, dg_ref, buf):
        @pl.when(pl.program_id(0) == 0)
        def _():
            dg_ref[...] = jnp.zeros_like(dg_ref)

        def branch_sum(refs):
            a, b, c = [r[...].astype(F32) if dil == 1 else _read_class_major(r, buf, dil) for r, dil in zip(refs, DILATIONS)]
            return (a + b) + c

        dp_ref[:, 0:hw] = fq[...]
        dp_ref[:, hw:2 * hw] = fk[...]
        dp_ref[:, 2 * hw:3 * hw] = fv[...]
        dp_ref[:, 3 * hw:4 * hw] = gx[...]
        dp_ref[:, 4 * hw:5 * hw] = (branch_sum((q1, q2, q3)) * QK_SCALE).astype(BF16)
        dp_ref[:, 5 * hw:6 * hw] = branch_sum((k1, k2, k3)).astype(BF16)
        dp_ref[:, 6 * hw:7 * hw] = branch_sum((v1, v2, v3)).astype(BF16)
        dh = _dot(dp_ref[...], w_ref[...])
        dx, dg = _rms_bwd(x_ref[...], g_ref[...], dh)
        dg_ref[...] += dg
        o_ref[...] = dx_ref[...] + dx

    half = _rows(tm, hw)
    by_dil = [_class_rows(tm, hw, dil) for dil in DILATIONS]
    return _call(
        body, name="in_proj_bwd", grid=(s_len // tm,),
        in_specs=[half] * 4 + by_dil * 3 + [_resident(w_in_t.shape), _rows(tm, d), _rows(tm, d), _resident((1, d))],
        out_specs=[_rows(tm, d), _rows(tm, 7 * hw), _acc_out((1, d))],
        out_shape=[_sds((s_len, d), F32), _sds((s_len, 7 * hw), BF16), _sds((1, d), F32)],
        scratch=[pltpu.VMEM((hw // PAIR, tm, PAIR), F32)],
    )(dfq, dfk, dfv, dgx, *ddq, *ddk, *ddv, w_in_t, dx1, x, g_pre)


def _expand_w_in(w_in):
    hw = HEAD_WIDTH
    gate = jnp.repeat(w_in[:, 3 * hw:3 * hw + N_HEADS], HEAD_DIM, axis=1)
    return jnp.concatenate([w_in[:, :3 * hw], gate, w_in[:, 3 * hw + N_HEADS:]], axis=1)


def _local_step(x, mem, target, g, w_bf):
    hw = HEAD_WIDTH
    w_in_e = _expand_w_in(w_bf["w_in"])
    b_exp = jnp.repeat(g["b_f"], HEAD_DIM, axis=1)
    buckets = jnp.asarray(_dil_buckets())

    n_dil = len(DILATIONS)
    h1, fqkv, gx, *dqkv = _in_proj(x, g["g_mix_pre"], w_in_e)
    aq, ak, c = _gate_scan(gx, b_exp)
    stats = _fox_block_stats(fqkv, c)
    spans = _fox_spans(x.shape[0])
    span_idx = _fox_span_index(stats, x.shape[0])
    o_fox, lse_fox = lax.switch(span_idx, [functools.partial(_fox_fwd, s) for s in spans], fqkv, aq, ak, stats)
    bias = _dil_bias(g["rel_bias"], buckets)
    branches = [_dil_fwd(dqkv[p], bias, p) for p in range(n_dil)]
    x1, y1, o_dil, *lj = _mix_out(o_fox, [b[0] for b in branches], [b[1] for b in branches], w_bf["w_out"], x, g["g_mix_post"])
    hm, kx, vx = _mem_fwd(mem, g["g_mem"], w_bf["w_xk"], w_bf["w_xv"])
    x2, y2, h2, qx, ox = _xattn_fwd(x1, g["g_xattn_pre"], w_bf["w_xq"], kx, vx, w_bf["w_xo"], g["g_xattn_post"])
    h3, a, u, z = _ffn_up(x2, g["g_ffn_pre"], w_bf["w_gate"], w_bf["w_up"])
    y3, dx3, sq = _ffn_down_loss(z, w_bf["w_down"], x2, g["g_ffn_post"], target)

    grads = {}
    dy3, da, du, grads["g_ffn_post"] = _ffn_bwd_a(dx3, y3, g["g_ffn_post"], w_bf["w_down"].T, a, u)
    dx2, grads["g_ffn_pre"] = _ffn_bwd_b(da, du, w_bf["w_gate"].T, w_bf["w_up"].T, dx3, x2, g["g_ffn_pre"])
    grads["w_down"] = _weight_grad(z, dy3, "dw_down")
    grads["w_gate"] = _weight_grad(h3, da, "dw_gate")
    grads["w_up"] = _weight_grad(h3, du, "dw_up")
    dx1, dy2, dqx, dkx, dvx, grads["g_xattn_post"], grads["g_xattn_pre"] = _xattn_bwd(
        dx2, y2, g["g_xattn_post"], w_bf["w_xo"].T, qx, kx, vx, w_bf["w_xq"].T, x1, g["g_xattn_pre"])
    grads["w_xo"] = _weight_grad(ox, dy2, "dw_xo")
    grads["w_xq"] = _weight_grad(h2, dqx, "dw_xq")
    grads["w_xk"], grads["w_xv"], grads["g_mem"] = _mem_bwd(dkx, dvx, w_bf["w_xk"].T, w_bf["w_xv"].T, hm, mem, g["g_mem"])
    dy1, do_fox, delta_fox, grads["g_mix_post"], do_dil, delta_dil = _mix_out_bwd(
        dx1, y1, g["g_mix_post"], w_bf["w_out"].T, o_fox, o_dil)
    grads["w_out"] = jnp.concatenate([_weight_grad(o_fox, dy1, "dw_out_fox"), _weight_grad(o_dil, dy1, "dw_out_dil")], axis=0)
    aql, ad = _fox_bwd_prep(aq, lse_fox, delta_fox)
    dfq, rs, dfk, dfv, dc = lax.switch(
        span_idx, [functools.partial(_fox_bwd, s) for s in spans], fqkv, do_fox, aql, ak, ad, stats)
    dgx, db = _gate_bwd(rs, dc, gx, b_exp)
    grads["b_f"] = db[:, ::HEAD_DIM]
    dil = [_dil_bwd(dqkv[p], do_dil[p], lj[p], delta_dil[p], bias, p) for p in range(n_dil)]
    grads["rel_bias"] = _rel_bias_grad(jnp.stack([t[3] for t in dil]), buckets)[:, :N_HEADS]
    grad_x, dproj, grads["g_mix_pre"] = _in_proj_bwd(
        dfq, dfk, dfv, dgx, [t[0] for t in dil], [t[1] for t in dil], [t[2] for t in dil], w_in_e.T, dx1, x, g["g_mix_pre"])
    dw_in_e = _weight_grad(h1, dproj, "dw_in")
    grads["w_in"] = jnp.concatenate(
        [dw_in_e[:, :3 * hw], dw_in_e[:, 3 * hw:4 * hw:HEAD_DIM], dw_in_e[:, 4 * hw:]], axis=1)
    return sq, grad_x, grads


MESH = pl.DeviceIdType.MESH


def _my_place():
    return lax.axis_index("x"), lax.axis_index("y"), lax.axis_index("c")


def _all_gather(x, name):
    rows, lanes = x.shape

    def body(x_ref, out_ref, send_sems, recv_sems, local_sem):
        mx, my, mc = _my_place()
        me, sibling = (mx, my, mc), (mx, my, 1 - mc)
        chips = [(1 - mx, my), (mx, 1 - my), (1 - mx, 1 - my)]

        def slot(px, py, pc):
            return out_ref.at[4 * px + 2 * py + pc]

        def copy(k, block, to, src=None):
            return pltpu.make_async_remote_copy(
                src_ref=slot(*block) if src is None else src, dst_ref=slot(*block),
                send_sem=send_sems.at[k], recv_sem=recv_sems.at[k], device_id=to, device_id_type=MESH)

        mine = pltpu.make_async_copy(x_ref, slot(*me), local_sem)
        mine.start()
        first = [copy(0, me, sibling, src=x_ref)]
        first += [copy(1 + j, me, (*chip, mc), src=x_ref) for j, chip in enumerate(chips)]
        for cp in first:
            cp.start()
        passed = [copy(4 + j, (*chip, mc), sibling) for j, chip in enumerate(chips)]
        for j, chip in enumerate(chips):
            copy(1 + j, (*chip, mc), me).wait_recv()
            passed[j].start()
        copy(0, sibling, me).wait_recv()
        for j, chip in enumerate(chips):
            copy(4 + j, (*chip, 1 - mc), me).wait_recv()
        for cp in first + passed:
            cp.wait_send()
        mine.wait()

    return pl.pallas_call(
        body, name=name, out_shape=_sds((N_DEV, rows, lanes), x.dtype),
        in_specs=[pl.BlockSpec(memory_space=pl.ANY)], out_specs=pl.BlockSpec(memory_space=pl.ANY),
        scratch_shapes=[pltpu.SemaphoreType.DMA((N_DEV - 1,)), pltpu.SemaphoreType.DMA((N_DEV - 1,)), pltpu.SemaphoreType.DMA],
    )(x)


def _exchange(g, name):
    _, rows, lanes = g.shape

    def body(g_ref, land_ref, send_sems, recv_sems, local_sem):
        mx, my, mc = _my_place()
        me = 4 * mx + 2 * my + mc
        mine = pltpu.make_async_copy(g_ref.at[me], land_ref.at[me], local_sem)
        mine.start()
        sent, arriving = [], []
        for k in (1, 2, 3, 4, 5, 6, 7):
            px = 1 - mx if k & 4 else mx
            py = 1 - my if k & 2 else my
            pc = 1 - mc if k & 1 else mc
            peer = 4 * px + 2 * py + pc
            cp = pltpu.make_async_remote_copy(
                src_ref=g_ref.at[peer], dst_ref=land_ref.at[me], send_sem=send_sems.at[k - 1], recv_sem=recv_sems.at[k - 1],
                device_id=(px, py, pc), device_id_type=MESH)
            cp.start()
            sent.append(cp)
            arriving.append(pltpu.make_async_remote_copy(
                src_ref=g_ref.at[me], dst_ref=land_ref.at[peer], send_sem=send_sems.at[k - 1], recv_sem=recv_sems.at[k - 1],
                device_id=(px, py, pc), device_id_type=MESH))
        for cp in arriving:
            cp.wait_recv()
        for cp in sent:
            cp.wait_send()
        mine.wait()

    return pl.pallas_call(
        body, name=name, out_shape=_sds(g.shape, g.dtype),
        in_specs=[pl.BlockSpec(memory_space=pl.ANY)], out_specs=pl.BlockSpec(memory_space=pl.ANY),
        scratch_shapes=[pltpu.SemaphoreType.DMA((N_DEV - 1,)), pltpu.SemaphoreType.DMA((N_DEV - 1,)), pltpu.SemaphoreType.DMA],
    )(g)


def _sum_slots(parts, name):
    n, rows, lanes = parts.shape
    tr = 512 if rows % 512 == 0 else rows

    def body(p_ref, o_ref):
        acc = p_ref[0].astype(F32)
        for j in range(1, n):
            acc = acc + p_ref[j].astype(F32)
        o_ref[...] = acc

    return _call(
        body, name=name, grid=(rows // tr,),
        in_specs=[pl.BlockSpec((n, tr, lanes), lambda i: (0, i, 0))], out_specs=_rows(tr, lanes),
        out_shape=_sds((rows, lanes), F32),
    )(parts)


def _adamw(w, g, m, v, name):
    def body(w_ref, g_ref, m_ref, v_ref, d_ref, nm_ref, nv_ref):
        gv = g_ref[...]
        m_new = ADAM_B1 * m_ref[...] + (1.0 - ADAM_B1) * gv
        v_new = ADAM_B2 * v_ref[...] + (1.0 - ADAM_B2) * (gv * gv)
        nm_ref[...] = m_new
        nv_ref[...] = v_new
        m_hat = m_new / (1.0 - ADAM_B1 ** ADAM_STEP)
        v_hat = v_new / (1.0 - ADAM_B2 ** ADAM_STEP)
        d_ref[...] = -ADAM_LR * (m_hat / (jnp.sqrt(v_hat) + ADAM_EPS) + ADAM_WD * w_ref[...])

    out = _sds(w.shape, F32)
    return pl.pallas_call(
        body, name=name, out_shape=[out, out, out],
        compiler_params=pltpu.CompilerParams(vmem_limit_bytes=V7X_VMEM_LIMIT_BYTES),
    )(w, g, m, v)


def _loss_head(sq, d_model):
    def body(sq_ref, o_ref):
        tot = jnp.sum(jnp.sum(sq_ref[...], axis=1, keepdims=True), axis=0, keepdims=True)
        o_ref[...] = 0.5 * (tot / d_model)

    return pl.pallas_call(body, name="loss_head", out_shape=_sds((1, 1), F32))(sq)


_BIG = (("w_in", 1), ("w_out", 0), ("w_xq", 0), ("w_xk", 0), ("w_xv", 0), ("w_xo", 1), ("w_gate", 1), ("w_up", 1), ("w_down", 0))
_SMALL = ("g_mix_pre", "b_f", "rel_bias", "g_mix_post", "g_xattn_pre", "g_mem", "g_xattn_post", "g_ffn_pre", "g_ffn_post")
LANES = 128
BIG_ROW_ALIGN = 512


def _round_up(n, k):
    return -(-n // k) * k


def _pack_rows(flat_parts, row_align, dtype):
    lead = flat_parts[0].shape[:-1]
    starts, rows, padded = [], 0, []
    for p in flat_parts:
        n = _round_up(p.shape[-1], LANES)
        starts.append(rows)
        rows += n // LANES
        padded.append(jnp.pad(p.astype(dtype), [(0, 0)] * len(lead) + [(0, n - p.shape[-1])]))
    total = _round_up(rows, row_align)
    padded.append(jnp.zeros(lead + ((total - rows) * LANES,), dtype))
    return jnp.concatenate(padded, axis=-1).reshape(lead + (total, LANES)), starts


def _unpack_rows(buf, starts, shapes):
    lead = buf.shape[:-2]
    flat = buf.reshape(lead + (-1,))
    out = []
    for st, shp in zip(starts, shapes):
        n = int(np.prod(shp))
        out.append(flat[..., st * LANES:st * LANES + n].reshape(lead + tuple(shp)))
    return out


def kernel(x, mem, g_mix_pre, w_in, b_f, rel_bias, w_out, g_mix_post, g_xattn_pre, g_mem, w_xq, w_xk, w_xv, w_xo, g_xattn_post, g_ffn_pre, w_gate, w_up, w_down, g_ffn_post, loss_target, m_g_mix_pre, m_w_in, m_b_f, m_rel_bias, m_w_out, m_g_mix_post, m_g_xattn_pre, m_g_mem, m_w_xq, m_w_xk, m_w_xv, m_w_xo, m_g_xattn_post, m_g_ffn_pre, m_w_gate, m_w_up, m_w_down, m_g_ffn_post, v_g_mix_pre, v_w_in, v_b_f, v_rel_bias, v_w_out, v_g_mix_post, v_g_xattn_pre, v_g_mem, v_w_xq, v_w_xk, v_w_xv, v_w_xo, v_g_xattn_post, v_g_ffn_pre, v_w_gate, v_w_up, v_w_down, v_g_ffn_post):
    given = dict(locals())
    order = ("g_mix_pre", "w_in", "b_f", "rel_bias", "w_out", "g_mix_post", "g_xattn_pre", "g_mem", "w_xq", "w_xk", "w_xv",
             "w_xo", "g_xattn_post", "g_ffn_pre", "w_gate", "w_up", "w_down", "g_ffn_post")
    two_d = lambda a: a.reshape(a.shape[-2:])
    w_loc = {n: two_d(given[n]) for n in order}
    m_loc = {n: two_d(given["m_" + n]) for n in order}
    v_loc = {n: two_d(given["v_" + n]) for n in order}
    d_model = x.shape[-1]

    shard_shapes = [w_loc[n].shape for n, _ in _BIG]
    packed, starts = _pack_rows([w_loc[n].reshape(-1) for n, _ in _BIG], BIG_ROW_ALIGN, BF16)
    gathered = _all_gather(packed, "gather_weights")
    w_bf = {}
    for (n, axis), part in zip(_BIG, _unpack_rows(gathered, starts, shard_shapes)):
        r, c = part.shape[1:]
        w_bf[n] = part.reshape(N_DEV * r, c) if axis == 0 else part.transpose(1, 0, 2).reshape(r, N_DEV * c)

    small = {n: w_loc[n] for n in _SMALL}
    sq, grad_x, grads = _local_step(two_d(x), two_d(mem), two_d(loss_target), small, w_bf)

    per_owner = []
    for (n, axis), shp in zip(_BIG, shard_shapes):
        r, c = shp
        gfull = grads[n]
        per_owner.append(gfull.reshape(N_DEV, r * c) if axis == 0 else gfull.reshape(r, N_DEV, c).transpose(1, 0, 2).reshape(N_DEV, r * c))
    rows_big = packed.shape[0]
    slots, _ = _pack_rows(per_owner, BIG_ROW_ALIGN, BF16)
    landed = _exchange(slots, "exchange_grads")
    g_big = dict(zip([n for n, _ in _BIG], _unpack_rows(_sum_slots(landed, "sum_grads"), starts, shard_shapes)))
    assert landed.shape[1] == rows_big

    small_parts = [grads[n].reshape(-1) for n in _SMALL] + [sq.reshape(-1)]
    small_shapes = [w_loc[n].shape for n in _SMALL] + [sq.shape]
    spacked, sstarts = _pack_rows(small_parts, 8, F32)
    ssum = _sum_slots(_all_gather(spacked, "gather_small"), "sum_small")
    g_small = dict(zip(_SMALL, _unpack_rows(ssum, sstarts, small_shapes)[:-1]))
    sq_rows = sq.size // LANES
    loss = _loss_head(ssum[sstarts[-1]:sstarts[-1] + sq_rows], d_model).reshape(())

    g_loc, delta, new_m, new_v = {}, {}, {}, {}
    for n, _ in _BIG:
        g_loc[n] = g_big[n]
        delta[n], new_m[n], new_v[n] = _adamw(w_loc[n], g_big[n], m_loc[n], v_loc[n], "adamw_" + n)
    pk = lambda d: _pack_rows([d[n].reshape(-1) for n in _SMALL], 8, F32)[0]
    pstarts = _pack_rows([w_loc[n].reshape(-1) for n in _SMALL], 8, F32)[1]
    d_s, m_s, v_s = _adamw(pk(w_loc), pk(g_small), pk(m_loc), pk(v_loc), "adamw_small")
    shapes_s = [w_loc[n].shape for n in _SMALL]
    for n, dd, mm, vv in zip(_SMALL, _unpack_rows(d_s, pstarts, shapes_s), _unpack_rows(m_s, pstarts, shapes_s),
                             _unpack_rows(v_s, pstarts, shapes_s)):
        g_loc[n], delta[n], new_m[n], new_v[n] = g_small[n], dd, mm, vv

    like = lambda d: [d[n].reshape(given[n].shape) for n in order]
    return (loss, grad_x.reshape(x.shape), *like(g_loc), *like(delta), *like(new_m), *like(new_v))
```

```python
import functools

import numpy as np
import jax
import jax.numpy as jnp
from jax import lax
from jax.experimental import pallas as pl
from jax.experimental.pallas import tpu as pltpu

F32 = jnp.float32
BF16 = jnp.bfloat16

RMS_EPS = 1e-6
HEAD_DIM = 64
N_HEADS = 8
HEAD_WIDTH = N_HEADS * HEAD_DIM
PAIR = 2 * HEAD_DIM
N_PAIRS = N_HEADS // 2
DIL_BLOCK = 128
DILATIONS = (1, 4, 16)
N_BUCKETS = 32
MAX_DISTANCE = 2048
N_MEM_HEADS = 4
QK_SCALE = HEAD_DIM ** -0.5
NEG = -1e30
FOX_SKIP_MARGIN = 110.0
FOX_SHORT_SPANS = (4, 8, 16)
N_DEV = 8

ADAM_LR = 0.001
ADAM_B1 = 0.9
ADAM_B2 = 0.999
ADAM_EPS = 1e-08
ADAM_WD = 0.01
ADAM_STEP = 10

V7X_VMEM_LIMIT_BYTES = 56 * 2 ** 20
ROW_TILE = 256
ATT_BLOCK = 512
SCAN_TILE = 256


def _call(body, *, name, grid, in_specs, out_specs, out_shape, scratch=()):
    return pl.pallas_call(
        body, name=name, grid=grid, in_specs=in_specs, out_specs=out_specs, out_shape=out_shape,
        scratch_shapes=list(scratch),
        compiler_params=pltpu.CompilerParams(
            dimension_semantics=("arbitrary",) * len(grid), vmem_limit_bytes=V7X_VMEM_LIMIT_BYTES))


def _rows(tm, n):
    return pl.BlockSpec((tm, n), lambda i: (i, 0))


def _resident(shape):
    zeros = (0,) * len(shape)
    return pl.BlockSpec(shape, lambda i: zeros, pipeline_mode=pl.Buffered(1))


def _acc_out(shape):
    zeros = (0,) * len(shape)
    return pl.BlockSpec(shape, lambda i: zeros)


def _sds(shape, dtype):
    return jax.ShapeDtypeStruct(shape, dtype)


def _dot(a, b):
    return jnp.dot(a, b, preferred_element_type=F32)


def _dot_nt(a, b):
    return lax.dot_general(a, b, (((1,), (1,)), ((), ())), preferred_element_type=F32)


def _dot_tn(a, b):
    return lax.dot_general(a, b, (((0,), (0,)), ((), ())), preferred_element_type=F32)


def _rms_fwd(x, g):
    r = lax.rsqrt(jnp.mean(x * x, axis=-1, keepdims=True) + RMS_EPS)
    return (x * r) * g


def _rms_bwd(xin, g, dy):
    r = lax.rsqrt(jnp.mean(xin * xin, axis=-1, keepdims=True) + RMS_EPS)
    xhat = xin * r
    dg = jnp.sum(dy * xhat, axis=0, keepdims=True)
    dxh = dy * g
    dx = r * (dxh - xhat * jnp.mean(dxh * xhat, axis=-1, keepdims=True))
    return dx, dg


def _first_head_lanes():
    return lax.broadcasted_iota(jnp.int32, (1, PAIR), 1) < HEAD_DIM


def _pick(mask, a, b):
    return jnp.where(mask, a, b)


def _zero_other(mask, v):
    return jnp.where(mask, v, jnp.zeros_like(v))


def _store_lane_blocks(buf_ref, val):
    for cb in range(buf_ref.shape[0]):
        buf_ref[cb] = val[:, cb * PAIR:(cb + 1) * PAIR].astype(F32)


def _load_lane_blocks(buf_ref):
    return jnp.concatenate([buf_ref[cb] for cb in range(buf_ref.shape[0])], axis=1)


def _write_class_major(buf_ref, out_ref, dil):
    n, tile, _ = buf_ref.shape
    for r in range(dil):
        for cb in range(n):
            col = (r * n + cb) * PAIR
            out_ref[:, col:col + PAIR] = buf_ref.at[cb][pl.ds(r, tile // dil, stride=dil), :].astype(out_ref.dtype)


def _read_class_major(in_ref, buf_ref, dil):
    n, tile, _ = buf_ref.shape
    for r in range(dil):
        for cb in range(n):
            col = (r * n + cb) * PAIR
            buf_ref.at[cb][pl.ds(r, tile // dil, stride=dil), :] = in_ref[:, col:col + PAIR].astype(F32)
    return _load_lane_blocks(buf_ref)


def _class_rows(tm, width, dil):
    return _rows(tm // dil, dil * width)


def _in_proj(x, g, w):
    s_len, d = x.shape
    tm = ROW_TILE
    hw = HEAD_WIDTH

    def body(x_ref, g_ref, w_ref, h_ref, fqkv_ref, gx_ref, *rest):
        dil_refs, buf = rest[:-1], rest[-1]
        h = _rms_fwd(x_ref[...], g_ref[...]).astype(BF16)
        h_ref[...] = h
        proj = _dot(h, w_ref[...])
        fqkv_ref[:, 0:hw] = (proj[:, 0:hw] * QK_SCALE).astype(BF16)
        fqkv_ref[:, hw:3 * hw] = proj[:, hw:3 * hw].astype(BF16)
        gx_ref[...] = proj[:, 3 * hw:4 * hw]
        dqkv = jnp.concatenate([proj[:, 4 * hw:5 * hw] * QK_SCALE, proj[:, 5 * hw:7 * hw]], axis=1)
        _store_lane_blocks(buf, dqkv)
        for ref, dil in zip(dil_refs, DILATIONS):
            if dil == 1:
                ref[...] = dqkv.astype(BF16)
            else:
                _write_class_major(buf, ref, dil)

    return _call(
        body, name="in_proj", grid=(s_len // tm,),
        in_specs=[_rows(tm, d), _resident((1, d)), _resident(w.shape)],
        out_specs=[_rows(tm, d), _rows(tm, 3 * hw), _rows(tm, hw)] + [_class_rows(tm, 3 * hw, dil) for dil in DILATIONS],
        out_shape=[_sds((s_len, d), BF16), _sds((s_len, 3 * hw), BF16), _sds((s_len, hw), F32)]
        + [_sds((s_len // dil, dil * 3 * hw), BF16) for dil in DILATIONS],
        scratch=[pltpu.VMEM((3 * hw // PAIR, tm, PAIR), F32)],
    )(x, g, w)


def _swap_halves(x):
    return jnp.concatenate([pltpu.roll(x[:, i * PAIR:(i + 1) * PAIR], HEAD_DIM, 1) for i in range(x.shape[1] // PAIR)], axis=1)


def _split3(x):
    hi = x.astype(BF16)
    r = x - hi.astype(F32)
    mid = r.astype(BF16)
    lo = (r - mid.astype(F32)).astype(BF16)
    return hi, mid, lo


def _tri_dot(tri, x):
    return sum(_dot(tri, piece) for piece in _split3(x))


def _lane_in_head(width):
    return lax.broadcasted_iota(jnp.int32, (1, width), 1) % HEAD_DIM


def _place3(jj, first, pieces, base):
    out = base
    for i, p in enumerate(pieces):
        out = jnp.where(jj == first + i, p, out)
    return out


def _gate_scan(gx, b_exp):
    s_len, hw = gx.shape
    t = min(SCAN_TILE, s_len)
    tri = jnp.asarray(np.tril(np.ones((t, t), np.float32))).astype(BF16)

    def body(gx_ref, b_ref, tri_ref, aq_ref, ak_ref, c_ref, carry):
        @pl.when(pl.program_id(0) == 0)
        def _():
            carry[...] = jnp.zeros_like(carry)

        z = gx_ref[...] + b_ref[...]
        lf = jnp.minimum(z, 0.0) - jnp.log1p(jnp.exp(-jnp.abs(z)))
        c = _tri_dot(tri_ref[...], lf) + carry[...]
        carry[...] = c[t - 1:t, :]
        c_ref[...] = c
        hi, mid, lo = _split3(_swap_halves(c))
        jj = _lane_in_head(hw)
        zero = jnp.zeros_like(hi)
        one = jnp.ones_like(hi)
        aq_ref[...] = _place3(jj, 0, (hi, mid, lo), jnp.where(jj < 6, one, zero))
        ak_ref[...] = _place3(jj, 3, (-hi, -mid, -lo), jnp.where(jj < 9, one, zero))

    return _call(
        body, name="gate_scan", grid=(s_len // t,),
        in_specs=[_rows(t, hw), _resident((1, hw)), _resident((t, t))],
        out_specs=[_rows(t, hw), _rows(t, hw), _rows(t, hw)],
        out_shape=[_sds((s_len, hw), BF16), _sds((s_len, hw), BF16), _sds((s_len, hw), F32)],
        scratch=[pltpu.VMEM((1, hw), F32)],
    )(gx, b_exp, tri)


def _head_block_ones():
    head_of = np.arange(HEAD_WIDTH) // HEAD_DIM
    return jnp.asarray((head_of[:, None] == head_of[None, :]).astype(np.float32))


def _fox_block_stats(fqkv, c):
    s_len = fqkv.shape[0]
    hw = HEAD_WIDTH
    b = min(ATT_BLOCK, s_len)
    nb = s_len // b

    def body(q_ref, k_ref, c_ref, ones_ref, o_ref):
        q, k, cv = q_ref[...].astype(F32), k_ref[...].astype(F32), c_ref[...]
        seg = lambda x: _dot(x.astype(BF16), ones_ref[...])
        col_max = lambda x: jnp.max(x, axis=0, keepdims=True)
        col_min = lambda x: jnp.min(x, axis=0, keepdims=True)
        qn = 1.01 * jnp.sqrt(col_max(seg(q * q)))
        kn = 1.01 * jnp.sqrt(col_max(seg(k * k)))
        dmin = col_min(seg(q * k)) - (2.0 ** -8) * qn * kn
        o_ref[0] = jnp.concatenate([qn, col_max(cv) - dmin, kn, col_min(cv), jnp.zeros((4, hw), F32)], axis=0)

    stats = _call(
        body, name="fox_block_stats", grid=(nb,),
        in_specs=[pl.BlockSpec((b, hw), lambda i: (i, 0)), pl.BlockSpec((b, hw), lambda i: (i, 1)), _rows(b, hw),
                  _resident((hw, hw))],
        out_specs=pl.BlockSpec((1, 8, hw), lambda i: (i, 0, 0)),
        out_shape=_sds((nb, 8, hw), F32),
    )(fqkv, fqkv, c, _head_block_ones().astype(BF16))
    st = jnp.transpose(stats[:, :4, ::HEAD_DIM], (1, 2, 0))
    bound = st[0][:, :, None] * st[2][:, None, :] + st[1][:, :, None] - st[3][:, None, :]
    need_h = jnp.logical_not(bound < -FOX_SKIP_MARGIN)
    need = jnp.logical_or(need_h[0::2], need_h[1::2])
    ii = lax.broadcasted_iota(jnp.int32, (1, nb, nb), 1)
    jj = lax.broadcasted_iota(jnp.int32, (1, nb, nb), 2)
    first_needed = jnp.min(jnp.where(jnp.logical_or(jnp.logical_and(need, jj < ii), jj == ii), jj, nb), axis=2)
    window = ii[:, :, 0] - first_needed + 1
    in_window = jnp.logical_and(jj >= first_needed[:, :, None], jj <= ii)
    last_query = jnp.max(jnp.where(in_window, ii, 0), axis=1)
    return jnp.concatenate([st.reshape(4 * N_HEADS, nb), window.astype(F32), last_query.astype(F32)], axis=0)


FOX_WINDOW_ROW = 4 * N_HEADS
FOX_LAST_QUERY_ROW = 4 * N_HEADS + N_PAIRS


def _head_needed(st_ref, h, i, j):
    bound = st_ref[h, i] * st_ref[2 * N_HEADS + h, j] + st_ref[N_HEADS + h, i] - st_ref[3 * N_HEADS + h, j]
    return jnp.logical_not(bound < -FOX_SKIP_MARGIN)


def _lower_triangle(n):
    return lax.broadcasted_iota(jnp.int32, (n, n), 1) <= lax.broadcasted_iota(jnp.int32, (n, n), 0)


def _fox_spans(s_len):
    nq = s_len // min(ATT_BLOCK, s_len)
    return [s for s in FOX_SHORT_SPANS if s < nq] + [nq]


def _fox_span_index(stats, s_len):
    longest = jnp.max(stats[FOX_WINDOW_ROW:FOX_WINDOW_ROW + N_PAIRS])
    idx = jnp.int32(0)
    for s in _fox_spans(s_len)[:-1]:
        idx = idx + (longest > s).astype(jnp.int32)
    return idx


def _fox_fwd(span, fqkv, aq, ak, stats):
    s_len = fqkv.shape[0]
    bq = bk = min(ATT_BLOCK, s_len)
    nq = s_len // bq

    def body(st_ref, q_ref, k_ref, v_ref, aq_ref, ak_ref, o_ref, lse_ref, qa_ref, m_ref, acc_ref):
        pair, qi, back = pl.program_id(0), pl.program_id(1), pl.program_id(2)
        kj = jnp.maximum(qi - back, 0)
        in0 = _first_head_lanes()
        not0 = jnp.logical_not(in0)

        @pl.when(back == 0)
        def _():
            q2, a2 = q_ref[...], aq_ref[...]
            qa_ref[0] = jnp.where(in0, q2, a2)
            qa_ref[1] = jnp.where(in0, a2, q2)
            m_ref[...] = jnp.full_like(m_ref, NEG)
            acc_ref[...] = jnp.zeros_like(acc_ref)

        def head_step(a, mine, masked):
            k2, v2 = k_ref[...], v_ref[...]
            s = _dot_nt(qa_ref[a], jnp.where(mine, k2, ak_ref[...]))
            if masked:
                s = jnp.where(_lower_triangle(bq), s, NEG)
            m_old = m_ref[a]
            m_new = jnp.maximum(m_old, jnp.max(s, axis=1, keepdims=True))
            p = jnp.exp(s - jnp.tile(m_new, (1, bk // PAIR))).astype(BF16)
            acc_ref[a] = jnp.exp(m_old - m_new) * acc_ref[a] + _dot(p, jnp.where(mine, v2, jnp.ones_like(v2)))
            m_ref[a] = m_new

        @pl.when(back == 0)
        def _():
            head_step(0, in0, True)
            head_step(1, not0, True)

        below = jnp.logical_and(back > 0, back <= qi)
        for a, mine in ((0, in0), (1, not0)):
            @pl.when(jnp.logical_and(below, _head_needed(st_ref, 2 * pair + a, qi, kj)))
            def _():
                head_step(a, mine, False)

        @pl.when(back == jnp.minimum(qi, span - 1))
        def _():
            acc0, acc1 = acc_ref[0], acc_ref[1]
            l2 = pltpu.roll(_pick(in0, acc1, acc0), HEAD_DIM, 1)
            o_ref[...] = _pick(in0, acc0, acc1) / l2
            lse_ref[...] = _pick(in0, m_ref[0], m_ref[1]) + jnp.log(l2)

    blk = lambda rows: (rows, PAIR)
    qmap = lambda p, i, b, st: (i, p)

    def key_block(p, i, b, st):
        window = jnp.clip(st[FOX_WINDOW_ROW + p, i].astype(jnp.int32), 1, i + 1)
        return i - jnp.minimum(b, window - 1)

    return pl.pallas_call(
        body, name=f"fox_fwd_{span}",
        grid_spec=pltpu.PrefetchScalarGridSpec(
            num_scalar_prefetch=1, grid=(N_PAIRS, nq, span),
            in_specs=[
                pl.BlockSpec(blk(bq), qmap),
                pl.BlockSpec(blk(bk), lambda p, i, b, st: (key_block(p, i, b, st), N_PAIRS + p)),
                pl.BlockSpec(blk(bk), lambda p, i, b, st: (key_block(p, i, b, st), 2 * N_PAIRS + p)),
                pl.BlockSpec(blk(bq), qmap),
                pl.BlockSpec(blk(bk), lambda p, i, b, st: (key_block(p, i, b, st), p)),
            ],
            out_specs=[pl.BlockSpec(blk(bq), qmap), pl.BlockSpec(blk(bq), qmap)],
            scratch_shapes=[pltpu.VMEM((2, bq, PAIR), BF16), pltpu.VMEM((2, bq, PAIR), F32), pltpu.VMEM((2, bq, PAIR), F32)]),
        out_shape=[_sds((s_len, HEAD_WIDTH), F32), _sds((s_len, HEAD_WIDTH), F32)],
        compiler_params=pltpu.CompilerParams(dimension_semantics=("arbitrary",) * 3, vmem_limit_bytes=V7X_VMEM_LIMIT_BYTES),
    )(stats, fqkv, fqkv, fqkv, aq, ak)


def _t5_bucket(dist):
    max_exact = N_BUCKETS // 2
    d = np.maximum(dist, 1).astype(np.float32)
    large = max_exact + (np.log(d / max_exact) / np.log(MAX_DISTANCE / max_exact) * (N_BUCKETS - max_exact)).astype(np.int32)
    large = np.minimum(large, N_BUCKETS - 1)
    return np.where(dist < max_exact, dist, large).astype(np.int32)


def _dil_buckets():
    w = DIL_BLOCK
    qi = np.arange(w)[:, None]
    kj = np.arange(2 * w)[None, :]
    sub = qi + w - kj
    band = (sub >= 0) & (sub <= w)
    out = [np.where(band, _t5_bucket(np.clip(sub, 0, w) * dil), -1) for dil in DILATIONS]
    return np.stack(out).astype(np.int32)


def _dil_bias(rel_bias, buckets):
    w = DIL_BLOCK

    def body(rb_ref, bk_ref, o_ref):
        for p in range(len(DILATIONS)):
            bk = bk_ref[p]
            for h in range(N_HEADS):
                def add(b, acc):
                    return acc + jnp.where(bk == b, rb_ref[b, h], 0.0)
                acc = lax.fori_loop(0, N_BUCKETS, add, jnp.zeros((w, 2 * w), F32))
                o_ref[p, h] = jnp.where(bk < 0, NEG, acc)

    return pl.pallas_call(
        body, name="dil_bias",
        in_specs=[pl.BlockSpec(memory_space=pltpu.SMEM), pl.BlockSpec(memory_space=pltpu.VMEM)],
        out_specs=pl.BlockSpec(memory_space=pltpu.VMEM),
        out_shape=_sds((len(DILATIONS), N_HEADS, w, 2 * w), F32),
    )(rel_bias, buckets)


def _dil_fwd(view, bias, branch):
    dil = DILATIONS[branch]
    w = DIL_BLOCK
    hw = HEAD_WIDTH
    length = view.shape[0]
    nb = length // w

    def body(q_ref, kc_ref, kp_ref, vc_ref, vp_ref, b_ref, o_ref, lse_ref):
        n = pl.program_id(1)
        in0 = _first_head_lanes()
        not0 = jnp.logical_not(in0)
        pairs = [slice(pr * PAIR, (pr + 1) * PAIR) for pr in range(N_PAIRS)]
        tiles = []
        for sl in pairs:
            q2 = q_ref[:, sl]
            qq = jnp.concatenate([_zero_other(in0, q2), _zero_other(not0, q2)], axis=0)
            tiles.append(jnp.concatenate([_dot_nt(qq, kp_ref[:, sl]), _dot_nt(qq, kc_ref[:, sl])], axis=1))
        s = jnp.concatenate(tiles, axis=0) + b_ref[...].reshape(N_HEADS * w, 2 * w)
        prev_half = lax.broadcasted_iota(jnp.int32, (1, 2 * w), 1) < w
        s = jnp.where(jnp.logical_and(n == 0, prev_half), NEG, s)
        m = jnp.max(s, axis=1, keepdims=True)
        e = jnp.exp(s - m)
        l = jnp.sum(e, axis=1, keepdims=True)
        p = (e / l).astype(BF16)
        lse = m + jnp.log(l)
        for pr, sl in enumerate(pairs):
            pp = p[2 * pr * w:(2 * pr + 2) * w]
            o2 = _dot(pp[:, :w], vp_ref[:, sl]) + _dot(pp[:, w:], vc_ref[:, sl])
            o_ref[:, sl] = _pick(in0, o2[:w], o2[w:])
            lse_ref[:, sl] = _pick(in0, lse[2 * pr * w:(2 * pr + 1) * w], lse[(2 * pr + 1) * w:(2 * pr + 2) * w])

    prev = lambda n: jnp.maximum(n - 1, 0)
    out = pl.pallas_call(
        body, name=f"dil_fwd_{dil}", grid=(dil, nb),
        in_specs=[
            pl.BlockSpec((w, hw), lambda r, n: (n, 3 * r)),
            pl.BlockSpec((w, hw), lambda r, n: (n, 3 * r + 1)),
            pl.BlockSpec((w, hw), lambda r, n: (prev(n), 3 * r + 1)),
            pl.BlockSpec((w, hw), lambda r, n: (n, 3 * r + 2)),
            pl.BlockSpec((w, hw), lambda r, n: (prev(n), 3 * r + 2)),
            pl.BlockSpec((None, N_HEADS, w, 2 * w), lambda r, n: (branch, 0, 0, 0)),
        ],
        out_specs=[pl.BlockSpec((w, hw), lambda r, n: (n, r)), pl.BlockSpec((w, hw), lambda r, n: (n, r))],
        out_shape=[_sds((length, dil * hw), F32), _sds((length, dil * hw), F32)],
        compiler_params=pltpu.CompilerParams(dimension_semantics=("arbitrary",) * 2, vmem_limit_bytes=V7X_VMEM_LIMIT_BYTES),
    )(view, view, view, view, view, bias)
    return out[0], out[1]


def _mix_out(o_fox, o_br, lse_br, w_out, x, g_post):
    s_len, d = x.shape
    hw = HEAD_WIDTH
    tm = ROW_TILE

    def body(of_ref, o1, o2, o3, l1, l2, l3, w_ref, x_ref, g_ref, x1_ref, y1_ref, od_ref, lj1, lj2, lj3, buf):
        def natural(ref, dil):
            return ref[...] if dil == 1 else _read_class_major(ref, buf, dil)

        ob = [natural(r, dil) for r, dil in zip((o1, o2, o3), DILATIONS)]
        la, lb, lc = [natural(r, dil) for r, dil in zip((l1, l2, l3), DILATIONS)]
        m = jnp.maximum(jnp.maximum(la, lb), lc)
        ea, eb, ec = jnp.exp(la - m), jnp.exp(lb - m), jnp.exp(lc - m)
        tot = ea + eb + ec
        o_dil = (ea / tot) * ob[0] + (eb / tot) * ob[1] + (ec / tot) * ob[2]
        od_ref[...] = o_dil
        lj = m + jnp.log(tot)
        _store_lane_blocks(buf, lj)
        for ref, dil in zip((lj1, lj2, lj3), DILATIONS):
            if dil == 1:
                ref[...] = lj
            else:
                _write_class_major(buf, ref, dil)
        y = _dot(of_ref[...].astype(BF16), w_ref[0:hw, :]) + _dot(o_dil.astype(BF16), w_ref[hw:2 * hw, :])
        y1_ref[...] = y
        x1_ref[...] = x_ref[...] + _rms_fwd(y, g_ref[...])

    half = _rows(tm, hw)
    by_dil = [_class_rows(tm, hw, dil) for dil in DILATIONS]
    return _call(
        body, name="mix_out", grid=(s_len // tm,),
        in_specs=[half] + by_dil + by_dil + [_resident(w_out.shape), _rows(tm, d), _resident((1, d))],
        out_specs=[_rows(tm, d), _rows(tm, d), half] + by_dil,
        out_shape=[_sds((s_len, d), F32), _sds((s_len, d), F32), _sds((s_len, hw), F32)]
        + [_sds((s_len // dil, dil * hw), F32) for dil in DILATIONS],
        scratch=[pltpu.VMEM((hw // PAIR, tm, PAIR), F32)],
    )(o_fox, *o_br, *lse_br, w_out, x, g_post)


def _mem_fwd(mem, g_mem, w_xk, w_xv):
    n_mem, d = mem.shape
    mw = w_xk.shape[1]

    def body(mem_ref, g_ref, wk_ref, wv_ref, hm_ref, k_ref, v_ref):
        hm = _rms_fwd(mem_ref[...], g_ref[...]).astype(BF16)
        hm_ref[...] = hm
        k_ref[...] = _dot(hm, wk_ref[...]).astype(BF16)
        v_ref[...] = _dot(hm, wv_ref[...]).astype(BF16)

    return pl.pallas_call(
        body, name="mem_fwd",
        out_shape=[_sds((n_mem, d), BF16), _sds((n_mem, mw), BF16), _sds((n_mem, mw), BF16)],
    )(mem, g_mem, w_xk, w_xv)


def _xattn_softmax(qa, k2):
    s = _dot_nt(qa, k2)
    m = jnp.max(s, axis=1, keepdims=True)
    e = jnp.exp(s - m)
    return e / jnp.sum(e, axis=1, keepdims=True)


def _xattn_fwd(x1, g_pre, w_xq, kx, vx, w_xo, g_post):
    s_len, d = x1.shape
    mw = w_xq.shape[1]
    n_mem = kx.shape[0]
    tm = ROW_TILE

    def body(x_ref, gp_ref, wq_ref, k_ref, v_ref, wo_ref, go_ref, x2_ref, y2_ref, h2_ref, q_ref, o_ref):
        x = x_ref[...]
        h = _rms_fwd(x, gp_ref[...]).astype(BF16)
        h2_ref[...] = h
        q = (_dot(h, wq_ref[...]) * QK_SCALE).astype(BF16)
        q_ref[...] = q
        in0 = _first_head_lanes()
        not0 = jnp.logical_not(in0)
        for pr in range(mw // PAIR):
            sl = slice(pr * PAIR, (pr + 1) * PAIR)
            q2, k2, v2 = q[:, sl], k_ref[:, sl], v_ref[:, sl]
            oa = [_dot(_xattn_softmax(_zero_other(mine, q2), k2).astype(BF16), v2) for mine in (in0, not0)]
            o_ref[:, sl] = _pick(in0, oa[0], oa[1]).astype(BF16)
        y = _dot(o_ref[...], wo_ref[...])
        y2_ref[...] = y
        x2_ref[...] = x + _rms_fwd(y, go_ref[...])

    return _call(
        body, name="xattn_fwd", grid=(s_len // tm,),
        in_specs=[_rows(tm, d), _resident((1, d)), _resident(w_xq.shape), _resident((n_mem, mw)), _resident((n_mem, mw)),
                  _resident(w_xo.shape), _resident((1, d))],
        out_specs=[_rows(tm, d), _rows(tm, d), _rows(tm, d), _rows(tm, mw), _rows(tm, mw)],
        out_shape=[_sds((s_len, d), F32), _sds((s_len, d), F32), _sds((s_len, d), BF16), _sds((s_len, mw), BF16),
                   _sds((s_len, mw), BF16)],
    )(x1, g_pre, w_xq, kx, vx, w_xo, g_post)


def _ffn_up(x2, g_pre, w_gate, w_up):
    s_len, d = x2.shape
    dff = w_gate.shape[1]
    tm = ROW_TILE

    def body(x_ref, g_ref, wg_ref, wu_ref, h_ref, a_ref, u_ref, z_ref):
        h = _rms_fwd(x_ref[...], g_ref[...]).astype(BF16)
        h_ref[...] = h
        a = _dot(h, wg_ref[...])
        u = _dot(h, wu_ref[...])
        a_ref[...] = a.astype(BF16)
        u_ref[...] = u.astype(BF16)
        z_ref[...] = ((a * jax.nn.sigmoid(a)) * u).astype(BF16)

    return _call(
        body, name="ffn_up", grid=(s_len // tm,),
        in_specs=[_rows(tm, d), _resident((1, d)), _resident(w_gate.shape), _resident(w_up.shape)],
        out_specs=[_rows(tm, d), _rows(tm, dff), _rows(tm, dff), _rows(tm, dff)],
        out_shape=[_sds((s_len, d), BF16)] + [_sds((s_len, dff), BF16)] * 3,
    )(x2, g_pre, w_gate, w_up)


def _ffn_down_loss(z, w_down, x2, g_post, target):
    s_len, d = x2.shape
    dff = z.shape[1]
    tm = ROW_TILE

    def body(z_ref, w_ref, x_ref, g_ref, t_ref, y_ref, dx_ref, sq_ref):
        @pl.when(pl.program_id(0) == 0)
        def _():
            sq_ref[...] = jnp.zeros_like(sq_ref)

        y = _dot(z_ref[...], w_ref[...])
        y_ref[...] = y
        err = (x_ref[...] + _rms_fwd(y, g_ref[...])) - t_ref[...]
        sq_ref[...] += jnp.sum(err * err, axis=0, keepdims=True)
        dx_ref[...] = err * (1.0 / d)

    return _call(
        body, name="ffn_down_loss", grid=(s_len // tm,),
        in_specs=[_rows(tm, dff), _resident(w_down.shape), _rows(tm, d), _resident((1, d)), _rows(tm, d)],
        out_specs=[_rows(tm, d), _rows(tm, d), _acc_out((1, d))],
        out_shape=[_sds((s_len, d), F32), _sds((s_len, d), F32), _sds((1, d), F32)],
    )(z, w_down, x2, g_post, target)


def _weight_grad(a, b, name):
    s_len, k = a.shape
    n = b.shape[1]
    ts = 512 if s_len % 512 == 0 else s_len
    tn = n
    while k * tn * 4 > 8 * 2 ** 20 and tn % 256 == 0:
        tn //= 2

    def body(a_ref, b_ref, o_ref):
        @pl.when(pl.program_id(1) == 0)
        def _():
            o_ref[...] = jnp.zeros_like(o_ref)

        o_ref[...] += _dot_tn(a_ref[...].astype(BF16), b_ref[...].astype(BF16))

    return pl.pallas_call(
        body, name=name, grid=(n // tn, s_len // ts),
        in_specs=[pl.BlockSpec((ts, k), lambda j, i: (i, 0)), pl.BlockSpec((ts, tn), lambda j, i: (i, j))],
        out_specs=pl.BlockSpec((k, tn), lambda j, i: (0, j)),
        out_shape=_sds((k, n), F32),
        compiler_params=pltpu.CompilerParams(dimension_semantics=("arbitrary",) * 2, vmem_limit_bytes=V7X_VMEM_LIMIT_BYTES),
    )(a, b)


def _ffn_bwd_a(dx3, y3, g_post, w_down_t, a, u):
    s_len, d = dx3.shape
    dff = a.shape[1]
    tm = ROW_TILE

    def body(dx_ref, y_ref, g_ref, w_ref, a_ref, u_ref, dy_ref, da_ref, du_ref, dg_ref):
        @pl.when(pl.program_id(0) == 0)
        def _():
            dg_ref[...] = jnp.zeros_like(dg_ref)

        dy, dg = _rms_bwd(y_ref[...], g_ref[...], dx_ref[...])
        dg_ref[...] += dg
        dyb = dy.astype(BF16)
        dy_ref[...] = dyb
        dz = _dot(dyb, w_ref[...])
        av = a_ref[...].astype(F32)
        uv = u_ref[...].astype(F32)
        sg = jax.nn.sigmoid(av)
        da_ref[...] = (dz * uv * (sg * (1.0 + av * (1.0 - sg)))).astype(BF16)
        du_ref[...] = (dz * (av * sg)).astype(BF16)

    return _call(
        body, name="ffn_bwd_a", grid=(s_len // tm,),
        in_specs=[_rows(tm, d), _rows(tm, d), _resident((1, d)), _resident(w_down_t.shape), _rows(tm, dff), _rows(tm, dff)],
        out_specs=[_rows(tm, d), _rows(tm, dff), _rows(tm, dff), _acc_out((1, d))],
        out_shape=[_sds((s_len, d), BF16), _sds((s_len, dff), BF16), _sds((s_len, dff), BF16), _sds((1, d), F32)],
    )(dx3, y3, g_post, w_down_t, a, u)


def _ffn_bwd_b(da, du, w_gate_t, w_up_t, dx3, x2, g_pre):
    s_len, d = x2.shape
    dff = da.shape[1]
    tm = ROW_TILE

    def body(da_ref, du_ref, wg_ref, wu_ref, dx_ref, x_ref, g_ref, o_ref, dg_ref):
        @pl.when(pl.program_id(0) == 0)
        def _():
            dg_ref[...] = jnp.zeros_like(dg_ref)

        dh = _dot(da_ref[...], wg_ref[...]) + _dot(du_ref[...], wu_ref[...])
        dx, dg = _rms_bwd(x_ref[...], g_ref[...], dh)
        dg_ref[...] += dg
        o_ref[...] = dx_ref[...] + dx

    return _call(
        body, name="ffn_bwd_b", grid=(s_len // tm,),
        in_specs=[_rows(tm, dff), _rows(tm, dff), _resident(w_gate_t.shape), _resident(w_up_t.shape), _rows(tm, d),
                  _rows(tm, d), _resident((1, d))],
        out_specs=[_rows(tm, d), _acc_out((1, d))],
        out_shape=[_sds((s_len, d), F32), _sds((1, d), F32)],
    )(da, du, w_gate_t, w_up_t, dx3, x2, g_pre)


def _xattn_bwd(dx2, y2, g_post, w_xo_t, q, kx, vx, w_xq_t, x1, g_pre):
    s_len, d = x1.shape
    mw = q.shape[1]
    n_mem = kx.shape[0]
    tm = ROW_TILE

    def body(dx_ref, y_ref, go_ref, wo_ref, q_ref, k_ref, v_ref, wq_ref, x_ref, gp_ref,
             dx1_ref, dy_ref, dq_ref, dk_ref, dv_ref, dgo_ref, dgp_ref):
        @pl.when(pl.program_id(0) == 0)
        def _():
            dk_ref[...] = jnp.zeros_like(dk_ref)
            dv_ref[...] = jnp.zeros_like(dv_ref)
            dgo_ref[...] = jnp.zeros_like(dgo_ref)
            dgp_ref[...] = jnp.zeros_like(dgp_ref)

        dxin = dx_ref[...]
        dy, dgo = _rms_bwd(y_ref[...], go_ref[...], dxin)
        dgo_ref[...] += dgo
        dyb = dy.astype(BF16)
        dy_ref[...] = dyb
        do = _dot(dyb, wo_ref[...]).astype(BF16)
        in0 = _first_head_lanes()
        not0 = jnp.logical_not(in0)
        for pr in range(mw // PAIR):
            sl = slice(pr * PAIR, (pr + 1) * PAIR)
            q2, k2, v2, do2 = q_ref[:, sl], k_ref[:, sl], v_ref[:, sl], do[:, sl]
            dqs = []
            dk2 = jnp.zeros((n_mem, PAIR), F32)
            dv2 = jnp.zeros((n_mem, PAIR), F32)
            for mine in (in0, not0):
                qa = _zero_other(mine, q2)
                doa = _zero_other(mine, do2)
                p = _xattn_softmax(qa, k2)
                dp = _dot_nt(doa, v2)
                ds = (p * (dp - jnp.sum(p * dp, axis=1, keepdims=True))).astype(BF16)
                dqs.append(_dot(ds, k2))
                dk2 = dk2 + _dot_tn(ds, qa)
                dv2 = dv2 + _dot_tn(p.astype(BF16), doa)
            dq_ref[:, sl] = (_pick(in0, dqs[0], dqs[1]) * QK_SCALE).astype(BF16)
            dk_ref[:, sl] += dk2
            dv_ref[:, sl] += dv2
        dh = _dot(dq_ref[...], wq_ref[...])
        dx, dgp = _rms_bwd(x_ref[...], gp_ref[...], dh)
        dgp_ref[...] += dgp
        dx1_ref[...] = dxin + dx

    return _call(
        body, name="xattn_bwd", grid=(s_len // tm,),
        in_specs=[_rows(tm, d), _rows(tm, d), _resident((1, d)), _resident(w_xo_t.shape), _rows(tm, mw),
                  _resident((n_mem, mw)), _resident((n_mem, mw)), _resident(w_xq_t.shape), _rows(tm, d), _resident((1, d))],
        out_specs=[_rows(tm, d), _rows(tm, d), _rows(tm, mw), _acc_out((n_mem, mw)), _acc_out((n_mem, mw)),
                   _acc_out((1, d)), _acc_out((1, d))],
        out_shape=[_sds((s_len, d), F32), _sds((s_len, d), BF16), _sds((s_len, mw), BF16), _sds((n_mem, mw), F32),
                   _sds((n_mem, mw), F32), _sds((1, d), F32), _sds((1, d), F32)],
    )(dx2, y2, g_post, w_xo_t, q, kx, vx, w_xq_t, x1, g_pre)


def _mem_bwd(dk, dv, w_xk_t, w_xv_t, hm, mem, g_mem):
    n_mem, d = mem.shape
    mw = dk.shape[1]

    def body(dk_ref, dv_ref, wk_ref, wv_ref, hm_ref, mem_ref, g_ref, dwk_ref, dwv_ref, dg_ref):
        dkb = dk_ref[...].astype(BF16)
        dvb = dv_ref[...].astype(BF16)
        dhm = _dot(dkb, wk_ref[...]) + _dot(dvb, wv_ref[...])
        _, dg = _rms_bwd(mem_ref[...], g_ref[...], dhm)
        dg_ref[...] = dg
        dwk_ref[...] = _dot_tn(hm_ref[...], dkb)
        dwv_ref[...] = _dot_tn(hm_ref[...], dvb)

    return pl.pallas_call(
        body, name="mem_bwd",
        out_shape=[_sds((d, mw), F32), _sds((d, mw), F32), _sds((1, d), F32)],
    )(dk, dv, w_xk_t, w_xv_t, hm, mem, g_mem)


def _mix_out_bwd(dx1, y1, g_post, w_out_t, o_fox, o_dil):
    s_len, d = dx1.shape
    hw = HEAD_WIDTH
    tm = ROW_TILE
    ones = _head_block_ones().astype(BF16)

    def body(dx_ref, y_ref, g_ref, w_ref, of_ref, od_ref, ones_ref, dy_ref, dof_ref, dlf_ref, dg_ref,
             dod1, dod2, dod3, dld1, dld2, dld3, buf):
        @pl.when(pl.program_id(0) == 0)
        def _():
            dg_ref[...] = jnp.zeros_like(dg_ref)

        dy, dg = _rms_bwd(y_ref[...], g_ref[...], dx_ref[...])
        dg_ref[...] += dg
        dyb = dy.astype(BF16)
        dy_ref[...] = dyb
        do = _dot(dyb, w_ref[...])
        def head_sums(x):
            hi = x.astype(BF16)
            lo = (x - hi.astype(F32)).astype(BF16)
            return _dot(hi, ones_ref[...]) + _dot(lo, ones_ref[...])

        dof_ref[...] = do[:, 0:hw].astype(BF16)
        dlf_ref[...] = head_sums(do[:, 0:hw] * of_ref[...])
        do_dil = do[:, hw:2 * hw]
        dl_dil = head_sums(do_dil * od_ref[...])
        for val, refs in ((do_dil, (dod1, dod2, dod3)), (dl_dil, (dld1, dld2, dld3))):
            _store_lane_blocks(buf, val)
            for ref, dil in zip(refs, DILATIONS):
                if dil == 1:
                    ref[...] = val.astype(ref.dtype)
                else:
                    _write_class_major(buf, ref, dil)

    half = _rows(tm, hw)
    by_dil = [_class_rows(tm, hw, dil) for dil in DILATIONS]
    outs = _call(
        body, name="mix_out_bwd", grid=(s_len // tm,),
        in_specs=[_rows(tm, d), _rows(tm, d), _resident((1, d)), _resident(w_out_t.shape), half, half, _resident((hw, hw))],
        out_specs=[_rows(tm, d), half, half, _acc_out((1, d))] + by_dil + by_dil,
        out_shape=[_sds((s_len, d), BF16), _sds((s_len, hw), BF16), _sds((s_len, hw), F32), _sds((1, d), F32)]
        + [_sds((s_len // dil, dil * hw), BF16) for dil in DILATIONS]
        + [_sds((s_len // dil, dil * hw), F32) for dil in DILATIONS],
        scratch=[pltpu.VMEM((hw // PAIR, tm, PAIR), F32)],
    )(dx1, y1, g_post, w_out_t, o_fox, o_dil, ones)
    return outs[0], outs[1], outs[2], outs[3], outs[4:7], outs[7:10]


def _fox_bwd_prep(aq, lse, delta):
    s_len, hw = aq.shape
    tm = ROW_TILE

    def body(aq_ref, lse_ref, dl_ref, aql_ref, ad_ref):
        jj = _lane_in_head(hw)
        l3 = _split3(_swap_halves(lse_ref[...]))
        aql_ref[...] = _place3(jj, 6, [-p for p in l3], aq_ref[...])
        d3 = _split3(_swap_halves(dl_ref[...]))
        ad_ref[...] = _place3(jj, 0, [-p for p in d3], jnp.zeros((tm, hw), BF16))

    half = _rows(tm, hw)
    return _call(
        body, name="fox_bwd_prep", grid=(s_len // tm,),
        in_specs=[half, half, half], out_specs=[half, half],
        out_shape=[_sds((s_len, hw), BF16), _sds((s_len, hw), BF16)],
    )(aq, lse, delta)


def _ones_on_first3(shape):
    jj = lax.broadcasted_iota(jnp.int32, shape, 1) % HEAD_DIM
    return jnp.where(jj < 3, 1.0, 0.0).astype(BF16)


def _fox_bwd(span, fqkv, do, aql, ak, ad, stats):
    s_len = fqkv.shape[0]
    bq = bk = min(ATT_BLOCK, s_len)
    nq = nk = s_len // bq

    def body(st_ref, q_ref, k_ref, v_ref, do_ref, aql_ref, ak_ref, ad_ref, dq_ref, rs_ref, dk_ref, dv_ref, dc_ref,
             ka_ref, va_ref, kone_ref, r_ref, dvacc_ref, dqacc_ref):
        pair, kj, ahead = pl.program_id(0), pl.program_id(1), pl.program_id(2)
        valid = kj + ahead < nq
        qi = jnp.minimum(kj + ahead, nq - 1)
        in0 = _first_head_lanes()
        not0 = jnp.logical_not(in0)
        heads = ((0, in0), (1, not0))
        rows = pl.ds(pl.multiple_of(qi * bq, bq), bq)

        @pl.when(ahead == 0)
        def _():
            k2, v2, a2 = k_ref[...], v_ref[...], ak_ref[...]
            one = jnp.ones_like(k2)
            one3 = _ones_on_first3(v2.shape)
            for a, mine in heads:
                ka_ref[a] = jnp.where(mine, k2, a2)
                va_ref[a] = jnp.where(mine, v2, one3)
                kone_ref[a] = jnp.where(mine, k2, one)
            r_ref[...] = jnp.zeros_like(r_ref)
            dvacc_ref[...] = jnp.zeros_like(dvacc_ref)

        @pl.when(jnp.logical_and(valid, jnp.logical_or(kj == 0, ahead == span - 1)))
        def _():
            for a, _ in heads:
                dqacc_ref[a, rows, :] = jnp.zeros((bq, PAIR), F32)

        def head_step(a, mine, masked):
            q2 = q_ref[...]
            doa = jnp.where(mine, do_ref[...], ad_ref[...])
            s = _dot_nt(jnp.where(mine, q2, aql_ref[...]), ka_ref[a])
            if masked:
                s = jnp.where(_lower_triangle(bq), s, NEG)
            p = jnp.exp(s)
            ds = (p * _dot_nt(doa, va_ref[a])).astype(BF16)
            dvacc_ref[a] += _dot_tn(p.astype(BF16), doa)
            r_ref[a] += _dot_tn(ds, jnp.where(mine, q2, jnp.ones_like(q2)))
            dqacc_ref[a, rows, :] += _dot(ds, kone_ref[a])

        for a, mine in heads:
            @pl.when(jnp.logical_and(jnp.logical_and(valid, ahead > 0), _head_needed(st_ref, 2 * pair + a, qi, kj)))
            def _():
                head_step(a, mine, False)

        @pl.when(ahead == 0)
        def _():
            for a, mine in heads:
                head_step(a, mine, True)
            acc0, acc1 = dqacc_ref[0, rows, :], dqacc_ref[1, rows, :]
            dq_ref[...] = (_pick(in0, acc0, acc1) * QK_SCALE).astype(BF16)
            rs_ref[...] = pltpu.roll(_pick(in0, acc1, acc0), HEAD_DIM, 1)

        @pl.when(ahead == span - 1)
        def _():
            dk_ref[...] = _pick(in0, r_ref[0], r_ref[1]).astype(BF16)
            dv_ref[...] = _pick(in0, dvacc_ref[0], dvacc_ref[1]).astype(BF16)
            dc_ref[...] = -pltpu.roll(_pick(in0, r_ref[1], r_ref[0]), HEAD_DIM, 1)

    blk = lambda n: (n, PAIR)
    kvmap = lambda p, j, t, st: (j, p)

    def qmap(p, j, t, st):
        last = jnp.clip(st[FOX_LAST_QUERY_ROW + p, j].astype(jnp.int32), j, nq - 1)
        return (jnp.minimum(j + t, last), p)

    return pl.pallas_call(
        body, name=f"fox_bwd_{span}",
        grid_spec=pltpu.PrefetchScalarGridSpec(
            num_scalar_prefetch=1, grid=(N_PAIRS, nk, span),
            in_specs=[
                pl.BlockSpec(blk(bq), qmap),
                pl.BlockSpec(blk(bk), lambda p, j, t, st: (j, N_PAIRS + p)),
                pl.BlockSpec(blk(bk), lambda p, j, t, st: (j, 2 * N_PAIRS + p)),
                pl.BlockSpec(blk(bq), qmap), pl.BlockSpec(blk(bq), qmap),
                pl.BlockSpec(blk(bk), kvmap),
                pl.BlockSpec(blk(bq), qmap),
            ],
            out_specs=[pl.BlockSpec(blk(bk), kvmap)] * 5,
            scratch_shapes=[pltpu.VMEM((2, bk, PAIR), BF16), pltpu.VMEM((2, bk, PAIR), BF16), pltpu.VMEM((2, bk, PAIR), BF16),
                            pltpu.VMEM((2, bk, PAIR), F32), pltpu.VMEM((2, bk, PAIR), F32), pltpu.VMEM((2, s_len, PAIR), F32)]),
        out_shape=[_sds((s_len, HEAD_WIDTH), BF16), _sds((s_len, HEAD_WIDTH), F32), _sds((s_len, HEAD_WIDTH), BF16),
                   _sds((s_len, HEAD_WIDTH), BF16), _sds((s_len, HEAD_WIDTH), F32)],
        compiler_params=pltpu.CompilerParams(dimension_semantics=("arbitrary",) * 3, vmem_limit_bytes=V7X_VMEM_LIMIT_BYTES),
    )(stats, fqkv, fqkv, fqkv, do, aql, ak, ad)


def _gate_bwd(rs, dc, gx, b_exp):
    s_len, hw = gx.shape
    t = min(SCAN_TILE, s_len)
    nt = s_len // t
    tri = jnp.asarray(np.triu(np.ones((t, t), np.float32))).astype(BF16)

    def body(rs_ref, dc_ref, gx_ref, b_ref, tri_ref, dgx_ref, db_ref, carry):
        @pl.when(pl.program_id(0) == 0)
        def _():
            carry[...] = jnp.zeros_like(carry)
            db_ref[...] = jnp.zeros_like(db_ref)

        dlf = _tri_dot(tri_ref[...], rs_ref[...] + dc_ref[...]) + carry[...]
        carry[...] = dlf[0:1, :]
        dgate = dlf * jax.nn.sigmoid(-(gx_ref[...] + b_ref[...]))
        db_ref[...] += jnp.sum(dgate, axis=0, keepdims=True)
        lane = lax.broadcasted_iota(jnp.int32, (1, hw), 1)
        dgx_ref[...] = jnp.where(lane % HEAD_DIM == 0, dgate, 0.0).astype(BF16)

    rev = lambda i: (nt - 1 - i, 0)
    return _call(
        body, name="gate_bwd", grid=(nt,),
        in_specs=[pl.BlockSpec((t, hw), rev)] * 3 + [_resident((1, hw)), _resident((t, t))],
        out_specs=[pl.BlockSpec((t, hw), rev), _acc_out((1, hw))],
        out_shape=[_sds((s_len, hw), BF16), _sds((1, hw), F32)],
        scratch=[pltpu.VMEM((1, hw), F32)],
    )(rs, dc, gx, b_exp, tri)


def _dil_bwd(qkv_v, do_v, lj_v, dl_v, bias, branch, riding=None):
    dil = DILATIONS[branch]
    w = DIL_BLOCK
    hw = HEAD_WIDTH
    length = qkv_v.shape[0]
    nb = length // w

    def body(q0_ref, q1_ref, kp_ref, kc_ref, vp_ref, vc_ref, do0_ref, do1_ref, l0_ref, l1_ref, d0_ref, d1_ref, b_ref, *rest):
        if riding is None:
            dq_ref, dk_ref, dv_ref, dsum_ref = rest
        else:
            g_ref, dq_ref, dk_ref, dv_ref, dsum_ref, land_ref, send_sems, recv_sems, local_sem = rest
            exchange = (g_ref, land_ref, send_sems, recv_sems, local_sem)
        r, n = pl.program_id(0), pl.program_id(1)
        in0 = _first_head_lanes()
        not0 = jnp.logical_not(in0)
        first = n == 0
        last = n == nb - 1

        @pl.when(jnp.logical_and(r == 0, n == 0))
        def _():
            dsum_ref[...] = jnp.zeros_like(dsum_ref)
            if riding is not None:
                _exchange_start(*exchange)

        pairs = [slice(pr * PAIR, (pr + 1) * PAIR) for pr in range(N_PAIRS)]

        def both_heads(ref, sl):
            v = ref[:, sl]
            return jnp.concatenate([_zero_other(in0, v), _zero_other(not0, v)], axis=0)

        def head_columns(ref):
            return jnp.concatenate([ref[:, h * HEAD_DIM:h * HEAD_DIM + 1] for h in range(N_HEADS)], axis=0)

        qq0 = [both_heads(q0_ref, sl) for sl in pairs]
        qq1 = [both_heads(q1_ref, sl) for sl in pairs]
        dd0 = [both_heads(do0_ref, sl) for sl in pairs]
        dd1 = [both_heads(do1_ref, sl) for sl in pairs]
        stack = lambda tiles: jnp.concatenate(tiles, axis=0)
        s_a = stack([_dot_nt(qq0[i], kp_ref[:, sl]) for i, sl in enumerate(pairs)])
        s_b = stack([_dot_nt(qq0[i], kc_ref[:, sl]) for i, sl in enumerate(pairs)])
        s_c = stack([_dot_nt(qq1[i], kc_ref[:, sl]) for i, sl in enumerate(pairs)])
        dp_a = stack([_dot_nt(dd0[i], vp_ref[:, sl]) for i, sl in enumerate(pairs)])
        dp_b = stack([_dot_nt(dd0[i], vc_ref[:, sl]) for i, sl in enumerate(pairs)])
        dp_c = stack([_dot_nt(dd1[i], vc_ref[:, sl]) for i, sl in enumerate(pairs)])
        bias2 = b_ref[...].reshape(N_HEADS * w, 2 * w)
        b_prev, b_cur = bias2[:, 0:w], bias2[:, w:2 * w]
        lse0, lse1 = head_columns(l0_ref), head_columns(l1_ref)
        dl0, dl1 = head_columns(d0_ref), head_columns(d1_ref)
        p_a = jnp.exp(jnp.where(first, NEG, s_a + b_prev) - lse0)
        p_b = jnp.exp((s_b + b_cur) - lse0)
        p_c = jnp.exp(jnp.where(last, NEG, s_c + b_prev) - lse1)
        ds_a = p_a * (dp_a - dl0)
        ds_b = p_b * (dp_b - dl0)
        ds_c = p_c * (dp_c - dl1)
        dsum_ref[...] += jnp.concatenate([ds_a, ds_b], axis=1).reshape(N_HEADS, w, 2 * w)
        ds_a, ds_b, ds_c = ds_a.astype(BF16), ds_b.astype(BF16), ds_c.astype(BF16)
        p_b, p_c = p_b.astype(BF16), p_c.astype(BF16)
        for i, sl in enumerate(pairs):
            rows = slice(2 * i * w, (2 * i + 2) * w)
            dq2 = _dot(ds_a[rows], kp_ref[:, sl]) + _dot(ds_b[rows], kc_ref[:, sl])
            dq_ref[:, sl] = _pick(in0, dq2[:w], dq2[w:]).astype(BF16)
            dk_ref[:, sl] = (_dot_tn(ds_b[rows], qq0[i]) + _dot_tn(ds_c[rows], qq1[i])).astype(BF16)
            dv_ref[:, sl] = (_dot_tn(p_b[rows], dd0[i]) + _dot_tn(p_c[rows], dd1[i])).astype(BF16)

        if riding is not None:
            @pl.when(jnp.logical_and(r == dil - 1, n == nb - 1))
            def _():
                _exchange_wait(*exchange)

    prev = lambda n: jnp.maximum(n - 1, 0)
    nxt = lambda n: jnp.minimum(n + 1, nb - 1)
    blk = (w, hw)
    anywhere = pl.BlockSpec(memory_space=pl.ANY)
    extra_in, extra_out, extra_shape, extra_scratch, extra_args = [], [], [], [], []
    if riding is not None:
        extra_in, extra_out, extra_shape = [anywhere], [anywhere], [_sds(riding.shape, riding.dtype)]
        extra_scratch, extra_args = list(EXCHANGE_SEMAPHORES), [riding]
    outs = pl.pallas_call(
        body, name=f"dil_bwd_{dil}", grid=(dil, nb),
        in_specs=[
            pl.BlockSpec(blk, lambda r, n: (n, 3 * r)),
            pl.BlockSpec(blk, lambda r, n: (nxt(n), 3 * r)),
            pl.BlockSpec(blk, lambda r, n: (prev(n), 3 * r + 1)),
            pl.BlockSpec(blk, lambda r, n: (n, 3 * r + 1)),
            pl.BlockSpec(blk, lambda r, n: (prev(n), 3 * r + 2)),
            pl.BlockSpec(blk, lambda r, n: (n, 3 * r + 2)),
            pl.BlockSpec(blk, lambda r, n: (n, r)),
            pl.BlockSpec(blk, lambda r, n: (nxt(n), r)),
            pl.BlockSpec(blk, lambda r, n: (n, r)),
            pl.BlockSpec(blk, lambda r, n: (nxt(n), r)),
            pl.BlockSpec(blk, lambda r, n: (n, r)),
            pl.BlockSpec(blk, lambda r, n: (nxt(n), r)),
            pl.BlockSpec((None, N_HEADS, w, 2 * w), lambda r, n: (branch, 0, 0, 0)),
        ] + extra_in,
        out_specs=[pl.BlockSpec(blk, lambda r, n: (n, r))] * 3 + [pl.BlockSpec((N_HEADS, w, 2 * w), lambda r, n: (0, 0, 0))]
        + extra_out,
        out_shape=[_sds((length, dil * hw), BF16)] * 3 + [_sds((N_HEADS, w, 2 * w), F32)] + extra_shape,
        scratch_shapes=extra_scratch,
        compiler_params=pltpu.CompilerParams(dimension_semantics=("arbitrary",) * 2, vmem_limit_bytes=V7X_VMEM_LIMIT_BYTES),
    )(qkv_v, qkv_v, qkv_v, qkv_v, qkv_v, qkv_v, do_v, do_v, lj_v, lj_v, dl_v, dl_v, bias, *extra_args)
    return list(outs)


def _rel_bias_grad(dsum, buckets):
    w = DIL_BLOCK

    def body(ds_ref, bk_ref, o_ref):
        row = lax.broadcasted_iota(jnp.int32, (N_BUCKETS, PAIR), 0)
        lane = lax.broadcasted_iota(jnp.int32, (N_BUCKETS, PAIR), 1)

        def per_bucket(b, acc):
            for p in range(len(DILATIONS)):
                hit = bk_ref[p] == b
                for h in range(N_HEADS):
                    part = jnp.where(hit, ds_ref[p, h], 0.0)
                    tot = jnp.sum(jnp.sum(part, axis=1, keepdims=True), axis=0, keepdims=True)
                    acc = acc + jnp.where(jnp.logical_and(row == b, lane == h), tot, 0.0)
            return acc

        o_ref[...] = lax.fori_loop(0, N_BUCKETS, per_bucket, jnp.zeros((N_BUCKETS, PAIR), F32))

    return pl.pallas_call(body, name="rel_bias_grad", out_shape=_sds((N_BUCKETS, PAIR), F32))(dsum, buckets)


def _in_proj_bwd(dfq, dfk, dfv, dgx, ddq, ddk, ddv, w_in_t, dx1, x, g_pre):
    s_len, d = x.shape
    hw = HEAD_WIDTH
    tm = ROW_TILE

    def body(fq, fk, fv, gx, q1, q2, q3, k1, k2, k3, v1, v2, v3, w_ref, dx_ref, x_ref, g_ref, o_ref, dp_ref, dg_ref, buf):
        @pl.when(pl.program_id(0) == 0)
        def _():
            dg_ref[...] = jnp.zeros_like(dg_ref)

        def branch_sum(refs):
            a, b, c = [r[...].astype(F32) if dil == 1 else _read_class_major(r, buf, dil) for r, dil in zip(refs, DILATIONS)]
            return (a + b) + c

        dp_ref[:, 0:hw] = fq[...]
        dp_ref[:, hw:2 * hw] = fk[...]
        dp_ref[:, 2 * hw:3 * hw] = fv[...]
        dp_ref[:, 3 * hw:4 * hw] = gx[...]
        dp_ref[:, 4 * hw:5 * hw] = (branch_sum((q1, q2, q3)) * QK_SCALE).astype(BF16)
        dp_ref[:, 5 * hw:6 * hw] = branch_sum((k1, k2, k3)).astype(BF16)
        dp_ref[:, 6 * hw:7 * hw] = branch_sum((v1, v2, v3)).astype(BF16)
        dh = _dot(dp_ref[...], w_ref[...])
        dx, dg = _rms_bwd(x_ref[...], g_ref[...], dh)
        dg_ref[...] += dg
        o_ref[...] = dx_ref[...] + dx

    half = _rows(tm, hw)
    by_dil = [_class_rows(tm, hw, dil) for dil in DILATIONS]
    return _call(
        body, name="in_proj_bwd", grid=(s_len // tm,),
        in_specs=[half] * 4 + by_dil * 3 + [_resident(w_in_t.shape), _rows(tm, d), _rows(tm, d), _resident((1, d))],
        out_specs=[_rows(tm, d), _rows(tm, 7 * hw), _acc_out((1, d))],
        out_shape=[_sds((s_len, d), F32), _sds((s_len, 7 * hw), BF16), _sds((1, d), F32)],
        scratch=[pltpu.VMEM((hw // PAIR, tm, PAIR), F32)],
    )(dfq, dfk, dfv, dgx, *ddq, *ddk, *ddv, w_in_t, dx1, x, g_pre)


def _expand_w_in(w_in):
    hw = HEAD_WIDTH
    gate = jnp.repeat(w_in[:, 3 * hw:3 * hw + N_HEADS], HEAD_DIM, axis=1)
    return jnp.concatenate([w_in[:, :3 * hw], gate, w_in[:, 3 * hw + N_HEADS:]], axis=1)


def _local_step(x, mem, target, g, w_bf, pack_early=None):
    hw = HEAD_WIDTH
    w_in_e = _expand_w_in(w_bf["w_in"])
    b_exp = jnp.repeat(g["b_f"], HEAD_DIM, axis=1)
    buckets = jnp.asarray(_dil_buckets())

    n_dil = len(DILATIONS)
    h1, fqkv, gx, *dqkv = _in_proj(x, g["g_mix_pre"], w_in_e)
    aq, ak, c = _gate_scan(gx, b_exp)
    stats = _fox_block_stats(fqkv, c)
    spans = _fox_spans(x.shape[0])
    span_idx = _fox_span_index(stats, x.shape[0])
    o_fox, lse_fox = lax.switch(span_idx, [functools.partial(_fox_fwd, s) for s in spans], fqkv, aq, ak, stats)
    bias = _dil_bias(g["rel_bias"], buckets)
    branches = [_dil_fwd(dqkv[p], bias, p) for p in range(n_dil)]
    x1, y1, o_dil, *lj = _mix_out(o_fox, [b[0] for b in branches], [b[1] for b in branches], w_bf["w_out"], x, g["g_mix_post"])
    hm, kx, vx = _mem_fwd(mem, g["g_mem"], w_bf["w_xk"], w_bf["w_xv"])
    x2, y2, h2, qx, ox = _xattn_fwd(x1, g["g_xattn_pre"], w_bf["w_xq"], kx, vx, w_bf["w_xo"], g["g_xattn_post"])
    h3, a, u, z = _ffn_up(x2, g["g_ffn_pre"], w_bf["w_gate"], w_bf["w_up"])
    y3, dx3, sq = _ffn_down_loss(z, w_bf["w_down"], x2, g["g_ffn_post"], target)

    grads = {}
    dy3, da, du, grads["g_ffn_post"] = _ffn_bwd_a(dx3, y3, g["g_ffn_post"], w_bf["w_down"].T, a, u)
    dx2, grads["g_ffn_pre"] = _ffn_bwd_b(da, du, w_bf["w_gate"].T, w_bf["w_up"].T, dx3, x2, g["g_ffn_pre"])
    grads["w_down"] = _weight_grad(z, dy3, "dw_down")
    grads["w_gate"] = _weight_grad(h3, da, "dw_gate")
    grads["w_up"] = _weight_grad(h3, du, "dw_up")
    dx1, dy2, dqx, dkx, dvx, grads["g_xattn_post"], grads["g_xattn_pre"] = _xattn_bwd(
        dx2, y2, g["g_xattn_post"], w_bf["w_xo"].T, qx, kx, vx, w_bf["w_xq"].T, x1, g["g_xattn_pre"])
    grads["w_xo"] = _weight_grad(ox, dy2, "dw_xo")
    grads["w_xq"] = _weight_grad(h2, dqx, "dw_xq")
    grads["w_xk"], grads["w_xv"], grads["g_mem"] = _mem_bwd(dkx, dvx, w_bf["w_xk"].T, w_bf["w_xv"].T, hm, mem, g["g_mem"])
    dy1, do_fox, delta_fox, grads["g_mix_post"], do_dil, delta_dil = _mix_out_bwd(
        dx1, y1, g["g_mix_post"], w_bf["w_out"].T, o_fox, o_dil)
    grads["w_out"] = jnp.concatenate([_weight_grad(o_fox, dy1, "dw_out_fox"), _weight_grad(o_dil, dy1, "dw_out_dil")], axis=0)
    aql, ad = _fox_bwd_prep(aq, lse_fox, delta_fox)
    dfq, rs, dfk, dfv, dc = lax.switch(
        span_idx, [functools.partial(_fox_bwd, s) for s in spans], fqkv, do_fox, aql, ak, ad, stats)
    dgx, db = _gate_bwd(rs, dc, gx, b_exp)
    grads["b_f"] = db[:, ::HEAD_DIM]
    riding = None if pack_early is None else pack_early(grads)
    dil = [_dil_bwd(dqkv[p], do_dil[p], lj[p], delta_dil[p], bias, p, riding if p == 0 else None) for p in range(n_dil)]
    landed_early = dil[0][4] if riding is not None else None
    grads["rel_bias"] = _rel_bias_grad(jnp.stack([t[3] for t in dil]), buckets)[:, :N_HEADS]
    grad_x, dproj, grads["g_mix_pre"] = _in_proj_bwd(
        dfq, dfk, dfv, dgx, [t[0] for t in dil], [t[1] for t in dil], [t[2] for t in dil], w_in_e.T, dx1, x, g["g_mix_pre"])
    dw_in_e = _weight_grad(h1, dproj, "dw_in")
    grads["w_in"] = jnp.concatenate(
        [dw_in_e[:, :3 * hw], dw_in_e[:, 3 * hw:4 * hw:HEAD_DIM], dw_in_e[:, 4 * hw:]], axis=1)
    return sq, grad_x, grads, landed_early


MESH = pl.DeviceIdType.MESH


def _my_place():
    return lax.axis_index("x"), lax.axis_index("y"), lax.axis_index("c")


def _all_gather(x, name):
    rows, lanes = x.shape

    def body(x_ref, out_ref, send_sems, recv_sems, local_sem):
        mx, my, mc = _my_place()
        me, sibling = (mx, my, mc), (mx, my, 1 - mc)
        chips = [(1 - mx, my), (mx, 1 - my), (1 - mx, 1 - my)]

        def slot(px, py, pc):
            return out_ref.at[4 * px + 2 * py + pc]

        def copy(k, block, to, src=None):
            return pltpu.make_async_remote_copy(
                src_ref=slot(*block) if src is None else src, dst_ref=slot(*block),
                send_sem=send_sems.at[k], recv_sem=recv_sems.at[k], device_id=to, device_id_type=MESH)

        mine = pltpu.make_async_copy(x_ref, slot(*me), local_sem)
        mine.start()
        first = [copy(0, me, sibling, src=x_ref)]
        first += [copy(1 + j, me, (*chip, mc), src=x_ref) for j, chip in enumerate(chips)]
        for cp in first:
            cp.start()
        passed = [copy(4 + j, (*chip, mc), sibling) for j, chip in enumerate(chips)]
        for j, chip in enumerate(chips):
            copy(1 + j, (*chip, mc), me).wait_recv()
            passed[j].start()
        copy(0, sibling, me).wait_recv()
        for j, chip in enumerate(chips):
            copy(4 + j, (*chip, 1 - mc), me).wait_recv()
        for cp in first + passed:
            cp.wait_send()
        mine.wait()

    return pl.pallas_call(
        body, name=name, out_shape=_sds((N_DEV, rows, lanes), x.dtype),
        in_specs=[pl.BlockSpec(memory_space=pl.ANY)], out_specs=pl.BlockSpec(memory_space=pl.ANY),
        scratch_shapes=[pltpu.SemaphoreType.DMA((N_DEV - 1,)), pltpu.SemaphoreType.DMA((N_DEV - 1,)), pltpu.SemaphoreType.DMA],
    )(x)


def _exchange_copies(g_ref, land_ref, send_sems, recv_sems, local_sem):
    mx, my, mc = _my_place()
    me = 4 * mx + 2 * my + mc
    mine = pltpu.make_async_copy(g_ref.at[me], land_ref.at[me], local_sem)
    sent, arriving = [], []
    for k in (1, 2, 3, 4, 5, 6, 7):
        px = 1 - mx if k & 4 else mx
        py = 1 - my if k & 2 else my
        pc = 1 - mc if k & 1 else mc
        peer = 4 * px + 2 * py + pc
        sent.append(pltpu.make_async_remote_copy(
            src_ref=g_ref.at[peer], dst_ref=land_ref.at[me], send_sem=send_sems.at[k - 1], recv_sem=recv_sems.at[k - 1],
            device_id=(px, py, pc), device_id_type=MESH))
        arriving.append(pltpu.make_async_remote_copy(
            src_ref=g_ref.at[me], dst_ref=land_ref.at[peer], send_sem=send_sems.at[k - 1], recv_sem=recv_sems.at[k - 1],
            device_id=(px, py, pc), device_id_type=MESH))
    return mine, sent, arriving


def _exchange_start(*refs):
    mine, sent, _ = _exchange_copies(*refs)
    mine.start()
    for cp in sent:
        cp.start()


def _exchange_wait(*refs):
    mine, sent, arriving = _exchange_copies(*refs)
    for cp in arriving:
        cp.wait_recv()
    for cp in sent:
        cp.wait_send()
    mine.wait()


EXCHANGE_SEMAPHORES = (pltpu.SemaphoreType.DMA((N_DEV - 1,)), pltpu.SemaphoreType.DMA((N_DEV - 1,)), pltpu.SemaphoreType.DMA)


def _exchange(g, name):
    def body(g_ref, land_ref, send_sems, recv_sems, local_sem):
        _exchange_start(g_ref, land_ref, send_sems, recv_sems, local_sem)
        _exchange_wait(g_ref, land_ref, send_sems, recv_sems, local_sem)

    return pl.pallas_call(
        body, name=name, out_shape=_sds(g.shape, g.dtype),
        in_specs=[pl.BlockSpec(memory_space=pl.ANY)], out_specs=pl.BlockSpec(memory_space=pl.ANY),
        scratch_shapes=list(EXCHANGE_SEMAPHORES),
    )(g)


def _sum_slots(parts, name):
    n, rows, lanes = parts.shape
    tr = 512 if rows % 512 == 0 else rows

    def body(p_ref, o_ref):
        acc = p_ref[0].astype(F32)
        for j in range(1, n):
            acc = acc + p_ref[j].astype(F32)
        o_ref[...] = acc

    return _call(
        body, name=name, grid=(rows // tr,),
        in_specs=[pl.BlockSpec((n, tr, lanes), lambda i: (0, i, 0))], out_specs=_rows(tr, lanes),
        out_shape=_sds((rows, lanes), F32),
    )(parts)


def _adamw(w, g, m, v, name):
    def body(w_ref, g_ref, m_ref, v_ref, d_ref, nm_ref, nv_ref):
        gv = g_ref[...]
        m_new = ADAM_B1 * m_ref[...] + (1.0 - ADAM_B1) * gv
        v_new = ADAM_B2 * v_ref[...] + (1.0 - ADAM_B2) * (gv * gv)
        nm_ref[...] = m_new
        nv_ref[...] = v_new
        m_hat = m_new / (1.0 - ADAM_B1 ** ADAM_STEP)
        v_hat = v_new / (1.0 - ADAM_B2 ** ADAM_STEP)
        d_ref[...] = -ADAM_LR * (m_hat / (jnp.sqrt(v_hat) + ADAM_EPS) + ADAM_WD * w_ref[...])

    out = _sds(w.shape, F32)
    return pl.pallas_call(
        body, name=name, out_shape=[out, out, out],
        compiler_params=pltpu.CompilerParams(vmem_limit_bytes=V7X_VMEM_LIMIT_BYTES),
    )(w, g, m, v)


def _loss_head(sq, d_model):
    def body(sq_ref, o_ref):
        tot = jnp.sum(jnp.sum(sq_ref[...], axis=1, keepdims=True), axis=0, keepdims=True)
        o_ref[...] = 0.5 * (tot / d_model)

    return pl.pallas_call(body, name="loss_head", out_shape=_sds((1, 1), F32))(sq)


_BIG = (("w_in", 1), ("w_out", 0), ("w_xq", 0), ("w_xk", 0), ("w_xv", 0), ("w_xo", 1), ("w_gate", 1), ("w_up", 1), ("w_down", 0))
_EARLY = ("w_xq", "w_xk", "w_xv", "w_xo", "w_gate", "w_up", "w_down")
_LATE = ("w_in", "w_out")
_SMALL = ("g_mix_pre", "b_f", "rel_bias", "g_mix_post", "g_xattn_pre", "g_mem", "g_xattn_post", "g_ffn_pre", "g_ffn_post")
LANES = 128
BIG_ROW_ALIGN = 512


def _round_up(n, k):
    return -(-n // k) * k


def _row_starts(sizes):
    starts, rows = [], 0
    for n in sizes:
        starts.append(rows)
        rows += _round_up(n, LANES) // LANES
    return starts


def _pack_rows(flat_parts, row_align, dtype):
    lead = flat_parts[0].shape[:-1]
    starts, rows, padded = [], 0, []
    for p in flat_parts:
        n = _round_up(p.shape[-1], LANES)
        starts.append(rows)
        rows += n // LANES
        padded.append(jnp.pad(p.astype(dtype), [(0, 0)] * len(lead) + [(0, n - p.shape[-1])]))
    total = _round_up(rows, row_align)
    padded.append(jnp.zeros(lead + ((total - rows) * LANES,), dtype))
    return jnp.concatenate(padded, axis=-1).reshape(lead + (total, LANES)), starts


def _unpack_rows(buf, starts, shapes):
    lead = buf.shape[:-2]
    flat = buf.reshape(lead + (-1,))
    out = []
    for st, shp in zip(starts, shapes):
        n = int(np.prod(shp))
        out.append(flat[..., st * LANES:st * LANES + n].reshape(lead + tuple(shp)))
    return out


def kernel(x, mem, g_mix_pre, w_in, b_f, rel_bias, w_out, g_mix_post, g_xattn_pre, g_mem, w_xq, w_xk, w_xv, w_xo, g_xattn_post, g_ffn_pre, w_gate, w_up, w_down, g_ffn_post, loss_target, m_g_mix_pre, m_w_in, m_b_f, m_rel_bias, m_w_out, m_g_mix_post, m_g_xattn_pre, m_g_mem, m_w_xq, m_w_xk, m_w_xv, m_w_xo, m_g_xattn_post, m_g_ffn_pre, m_w_gate, m_w_up, m_w_down, m_g_ffn_post, v_g_mix_pre, v_w_in, v_b_f, v_rel_bias, v_w_out, v_g_mix_post, v_g_xattn_pre, v_g_mem, v_w_xq, v_w_xk, v_w_xv, v_w_xo, v_g_xattn_post, v_g_ffn_pre, v_w_gate, v_w_up, v_w_down, v_g_ffn_post):
    given = dict(locals())
    order = ("g_mix_pre", "w_in", "b_f", "rel_bias", "w_out", "g_mix_post", "g_xattn_pre", "g_mem", "w_xq", "w_xk", "w_xv",
             "w_xo", "g_xattn_post", "g_ffn_pre", "w_gate", "w_up", "w_down", "g_ffn_post")
    two_d = lambda a: a.reshape(a.shape[-2:])
    w_loc = {n: two_d(given[n]) for n in order}
    m_loc = {n: two_d(given["m_" + n]) for n in order}
    v_loc = {n: two_d(given["v_" + n]) for n in order}
    d_model = x.shape[-1]

    shard_shapes = [w_loc[n].shape for n, _ in _BIG]
    packed, starts = _pack_rows([w_loc[n].reshape(-1) for n, _ in _BIG], BIG_ROW_ALIGN, BF16)
    gathered = _all_gather(packed, "gather_weights")
    w_bf = {}
    for (n, axis), part in zip(_BIG, _unpack_rows(gathered, starts, shard_shapes)):
        r, c = part.shape[1:]
        w_bf[n] = part.reshape(N_DEV * r, c) if axis == 0 else part.transpose(1, 0, 2).reshape(r, N_DEV * c)

    shape_of = dict(zip([n for n, _ in _BIG], shard_shapes))
    axis_of = dict(_BIG)

    def owner_slots(grads, names):
        per_owner = []
        for n in names:
            r, c = shape_of[n]
            gfull = grads[n]
            per_owner.append(gfull.reshape(N_DEV, r * c) if axis_of[n] == 0
                             else gfull.reshape(r, N_DEV, c).transpose(1, 0, 2).reshape(N_DEV, r * c))
        return _pack_rows(per_owner, BIG_ROW_ALIGN, BF16)

    small = {n: w_loc[n] for n in _SMALL}
    sq, grad_x, grads, landed_early = _local_step(
        two_d(x), two_d(mem), two_d(loss_target), small, w_bf, pack_early=lambda gr: owner_slots(gr, _EARLY)[0])
    late_slots, late_starts = owner_slots(grads, _LATE)
    landed_late = _exchange(late_slots, "exchange_grads")
    early_starts = _row_starts([int(np.prod(shape_of[n])) for n in _EARLY])
    g_big = {}
    for names, landed, row_starts, label in ((_EARLY, landed_early, early_starts, "sum_grads_early"),
                                             (_LATE, landed_late, late_starts, "sum_grads_late")):
        g_big.update(zip(names, _unpack_rows(_sum_slots(landed, label), row_starts, [shape_of[n] for n in names])))

    small_parts = [grads[n].reshape(-1) for n in _SMALL] + [sq.reshape(-1)]
    small_shapes = [w_loc[n].shape for n in _SMALL] + [sq.shape]
    spacked, sstarts = _pack_rows(small_parts, 8, F32)
    ssum = _sum_slots(_all_gather(spacked, "gather_small"), "sum_small")
    g_small = dict(zip(_SMALL, _unpack_rows(ssum, sstarts, small_shapes)[:-1]))
    sq_rows = sq.size // LANES
    loss = _loss_head(ssum[sstarts[-1]:sstarts[-1] + sq_rows], d_model).reshape(())

    g_loc, delta, new_m, new_v = {}, {}, {}, {}
    for n, _ in _BIG:
        g_loc[n] = g_big[n]
        delta[n], new_m[n], new_v[n] = _adamw(w_loc[n], g_big[n], m_loc[n], v_loc[n], "adamw_" + n)
    pk = lambda d: _pack_rows([d[n].reshape(-1) for n in _SMALL], 8, F32)[0]
    pstarts = _pack_rows([w_loc[n].reshape(-1) for n in _SMALL], 8, F32)[1]
    d_s, m_s, v_s = _adamw(pk(w_loc), pk(g_small), pk(m_loc), pk(v_loc), "adamw_small")
    shapes_s = [w_loc[n].shape for n in _SMALL]
    for n, dd, mm, vv in zip(_SMALL, _unpack_rows(d_s, pstarts, shapes_s), _unpack_rows(m_s, pstarts, shapes_s),
                             _unpack_rows(v_s, pstarts, shapes_s)):
        g_loc[n], delta[n], new_m[n], new_v[n] = g_small[n], dd, mm, vv

    like = lambda d: [d[n].reshape(given[n].shape) for n in order]
    return (loss, grad_x.reshape(x.shape), *like(g_loc), *like(delta), *like(new_m), *like(new_v))
```

```python
import functools

import numpy as np
import jax
import jax.numpy as jnp
from jax import lax
from jax.experimental import pallas as pl
from jax.experimental.pallas import tpu as pltpu

F32 = jnp.float32
BF16 = jnp.bfloat16

RMS_EPS = 1e-6
HEAD_DIM = 64
N_HEADS = 8
HEAD_WIDTH = N_HEADS * HEAD_DIM
PAIR = 2 * HEAD_DIM
N_PAIRS = N_HEADS // 2
DIL_BLOCK = 128
DILATIONS = (1, 4, 16)
N_BUCKETS = 32
MAX_DISTANCE = 2048
N_MEM_HEADS = 4
QK_SCALE = HEAD_DIM ** -0.5
NEG = -1e30
FOX_SKIP_MARGIN = 110.0
FOX_SHORT_SPANS = (4, 8, 16)
N_DEV = 8

ADAM_LR = 0.001
ADAM_B1 = 0.9
ADAM_B2 = 0.999
ADAM_EPS = 1e-08
ADAM_WD = 0.01
ADAM_STEP = 10

V7X_VMEM_LIMIT_BYTES = 56 * 2 ** 20
ROW_TILE = 256
ATT_BLOCK = 512
SCAN_TILE = 256


def _call(body, *, name, grid, in_specs, out_specs, out_shape, scratch=()):
    return pl.pallas_call(
        body, name=name, grid=grid, in_specs=in_specs, out_specs=out_specs, out_shape=out_shape,
        scratch_shapes=list(scratch),
        compiler_params=pltpu.CompilerParams(
            dimension_semantics=("arbitrary",) * len(grid), vmem_limit_bytes=V7X_VMEM_LIMIT_BYTES))


def _rows(tm, n):
    return pl.BlockSpec((tm, n), lambda i: (i, 0))


def _resident(shape):
    zeros = (0,) * len(shape)
    return pl.BlockSpec(shape, lambda i: zeros, pipeline_mode=pl.Buffered(1))


def _acc_out(shape):
    zeros = (0,) * len(shape)
    return pl.BlockSpec(shape, lambda i: zeros)


def _sds(shape, dtype):
    return jax.ShapeDtypeStruct(shape, dtype)


def _dot(a, b):
    return jnp.dot(a, b, preferred_element_type=F32)


def _dot_nt(a, b):
    return lax.dot_general(a, b, (((1,), (1,)), ((), ())), preferred_element_type=F32)


def _dot_tn(a, b):
    return lax.dot_general(a, b, (((0,), (0,)), ((), ())), preferred_element_type=F32)


def _rms_fwd(x, g):
    r = lax.rsqrt(jnp.mean(x * x, axis=-1, keepdims=True) + RMS_EPS)
    return (x * r) * g


def _rms_bwd(xin, g, dy):
    r = lax.rsqrt(jnp.mean(xin * xin, axis=-1, keepdims=True) + RMS_EPS)
    xhat = xin * r
    dg = jnp.sum(dy * xhat, axis=0, keepdims=True)
    dxh = dy * g
    dx = r * (dxh - xhat * jnp.mean(dxh * xhat, axis=-1, keepdims=True))
    return dx, dg


def _first_head_lanes():
    return lax.broadcasted_iota(jnp.int32, (1, PAIR), 1) < HEAD_DIM


def _pick(mask, a, b):
    return jnp.where(mask, a, b)


def _zero_other(mask, v):
    return jnp.where(mask, v, jnp.zeros_like(v))


def _store_lane_blocks(buf_ref, val):
    for cb in range(buf_ref.shape[0]):
        buf_ref[cb] = val[:, cb * PAIR:(cb + 1) * PAIR].astype(F32)


def _load_lane_blocks(buf_ref):
    return jnp.concatenate([buf_ref[cb] for cb in range(buf_ref.shape[0])], axis=1)


def _write_class_major(buf_ref, out_ref, dil):
    n, tile, _ = buf_ref.shape
    for r in range(dil):
        for cb in range(n):
            col = (r * n + cb) * PAIR
            out_ref[:, col:col + PAIR] = buf_ref.at[cb][pl.ds(r, tile // dil, stride=dil), :].astype(out_ref.dtype)


def _read_class_major(in_ref, buf_ref, dil):
    n, tile, _ = buf_ref.shape
    for r in range(dil):
        for cb in range(n):
            col = (r * n + cb) * PAIR
            buf_ref.at[cb][pl.ds(r, tile // dil, stride=dil), :] = in_ref[:, col:col + PAIR].astype(F32)
    return _load_lane_blocks(buf_ref)


def _class_rows(tm, width, dil):
    return _rows(tm // dil, dil * width)


def _in_proj(x, g, w, riding=None):
    s_len, d = x.shape
    tm = ROW_TILE
    hw = HEAD_WIDTH
    n_steps = s_len // tm

    def body(x_ref, g_ref, w_ref, *rest):
        if riding is None:
            h_ref, fqkv_ref, gx_ref, *dil_refs, buf = rest
        else:
            part_ref, h_ref, fqkv_ref, gx_ref, *dil_refs, all_ref, buf, send_sems, recv_sems, local_sem = rest
            gather = (part_ref, all_ref, send_sems, recv_sems, local_sem)

            @pl.when(pl.program_id(0) == 0)
            def _():
                _gather_start(*gather)

            @pl.when(pl.program_id(0) == (3 * n_steps) // 4)
            def _():
                _gather_forward(*gather)

        h = _rms_fwd(x_ref[...], g_ref[...]).astype(BF16)
        h_ref[...] = h
        proj = _dot(h, w_ref[...])
        fqkv_ref[:, 0:hw] = (proj[:, 0:hw] * QK_SCALE).astype(BF16)
        fqkv_ref[:, hw:3 * hw] = proj[:, hw:3 * hw].astype(BF16)
        gx_ref[...] = proj[:, 3 * hw:4 * hw]
        dqkv = jnp.concatenate([proj[:, 4 * hw:5 * hw] * QK_SCALE, proj[:, 5 * hw:7 * hw]], axis=1)
        _store_lane_blocks(buf, dqkv)
        for ref, dil in zip(dil_refs, DILATIONS):
            if dil == 1:
                ref[...] = dqkv.astype(BF16)
            else:
                _write_class_major(buf, ref, dil)

        if riding is not None:
            @pl.when(pl.program_id(0) == n_steps - 1)
            def _():
                _gather_finish(*gather)

    anywhere = pl.BlockSpec(memory_space=pl.ANY)
    extra_in, extra_out, extra_shape, extra_scratch, extra_args = [], [], [], [], []
    if riding is not None:
        extra_in, extra_out = [anywhere], [anywhere]
        extra_shape = [_sds((N_DEV,) + riding.shape, riding.dtype)]
        extra_scratch, extra_args = list(EXCHANGE_SEMAPHORES), [riding]
    return _call(
        body, name="in_proj", grid=(n_steps,),
        in_specs=[_rows(tm, d), _resident((1, d)), _resident(w.shape)] + extra_in,
        out_specs=[_rows(tm, d), _rows(tm, 3 * hw), _rows(tm, hw)] + [_class_rows(tm, 3 * hw, dil) for dil in DILATIONS]
        + extra_out,
        out_shape=[_sds((s_len, d), BF16), _sds((s_len, 3 * hw), BF16), _sds((s_len, hw), F32)]
        + [_sds((s_len // dil, dil * 3 * hw), BF16) for dil in DILATIONS] + extra_shape,
        scratch=[pltpu.VMEM((3 * hw // PAIR, tm, PAIR), F32)] + extra_scratch,
    )(x, g, w, *extra_args)


def _swap_halves(x):
    return jnp.concatenate([pltpu.roll(x[:, i * PAIR:(i + 1) * PAIR], HEAD_DIM, 1) for i in range(x.shape[1] // PAIR)], axis=1)


def _split3(x):
    hi = x.astype(BF16)
    r = x - hi.astype(F32)
    mid = r.astype(BF16)
    lo = (r - mid.astype(F32)).astype(BF16)
    return hi, mid, lo


def _tri_dot(tri, x):
    return sum(_dot(tri, piece) for piece in _split3(x))


def _lane_in_head(width):
    return lax.broadcasted_iota(jnp.int32, (1, width), 1) % HEAD_DIM


def _place3(jj, first, pieces, base):
    out = base
    for i, p in enumerate(pieces):
        out = jnp.where(jj == first + i, p, out)
    return out


def _gate_scan(gx, b_exp):
    s_len, hw = gx.shape
    t = min(SCAN_TILE, s_len)
    tri = jnp.asarray(np.tril(np.ones((t, t), np.float32))).astype(BF16)

    def body(gx_ref, b_ref, tri_ref, aq_ref, ak_ref, c_ref, carry):
        @pl.when(pl.program_id(0) == 0)
        def _():
            carry[...] = jnp.zeros_like(carry)

        z = gx_ref[...] + b_ref[...]
        lf = jnp.minimum(z, 0.0) - jnp.log1p(jnp.exp(-jnp.abs(z)))
        c = _tri_dot(tri_ref[...], lf) + carry[...]
        carry[...] = c[t - 1:t, :]
        c_ref[...] = c
        hi, mid, lo = _split3(_swap_halves(c))
        jj = _lane_in_head(hw)
        zero = jnp.zeros_like(hi)
        one = jnp.ones_like(hi)
        aq_ref[...] = _place3(jj, 0, (hi, mid, lo), jnp.where(jj < 6, one, zero))
        ak_ref[...] = _place3(jj, 3, (-hi, -mid, -lo), jnp.where(jj < 9, one, zero))

    return _call(
        body, name="gate_scan", grid=(s_len // t,),
        in_specs=[_rows(t, hw), _resident((1, hw)), _resident((t, t))],
        out_specs=[_rows(t, hw), _rows(t, hw), _rows(t, hw)],
        out_shape=[_sds((s_len, hw), BF16), _sds((s_len, hw), BF16), _sds((s_len, hw), F32)],
        scratch=[pltpu.VMEM((1, hw), F32)],
    )(gx, b_exp, tri)


def _head_block_ones():
    head_of = np.arange(HEAD_WIDTH) // HEAD_DIM
    return jnp.asarray((head_of[:, None] == head_of[None, :]).astype(np.float32))


def _fox_block_stats(fqkv, c):
    s_len = fqkv.shape[0]
    hw = HEAD_WIDTH
    b = min(ATT_BLOCK, s_len)
    nb = s_len // b

    def body(q_ref, k_ref, c_ref, ones_ref, o_ref):
        q, k, cv = q_ref[...].astype(F32), k_ref[...].astype(F32), c_ref[...]
        seg = lambda x: _dot(x.astype(BF16), ones_ref[...])
        col_max = lambda x: jnp.max(x, axis=0, keepdims=True)
        col_min = lambda x: jnp.min(x, axis=0, keepdims=True)
        qn = 1.01 * jnp.sqrt(col_max(seg(q * q)))
        kn = 1.01 * jnp.sqrt(col_max(seg(k * k)))
        dmin = col_min(seg(q * k)) - (2.0 ** -8) * qn * kn
        o_ref[0] = jnp.concatenate([qn, col_max(cv) - dmin, kn, col_min(cv), jnp.zeros((4, hw), F32)], axis=0)

    stats = _call(
        body, name="fox_block_stats", grid=(nb,),
        in_specs=[pl.BlockSpec((b, hw), lambda i: (i, 0)), pl.BlockSpec((b, hw), lambda i: (i, 1)), _rows(b, hw),
                  _resident((hw, hw))],
        out_specs=pl.BlockSpec((1, 8, hw), lambda i: (i, 0, 0)),
        out_shape=_sds((nb, 8, hw), F32),
    )(fqkv, fqkv, c, _head_block_ones().astype(BF16))
    st = jnp.transpose(stats[:, :4, ::HEAD_DIM], (1, 2, 0))
    bound = st[0][:, :, None] * st[2][:, None, :] + st[1][:, :, None] - st[3][:, None, :]
    need_h = jnp.logical_not(bound < -FOX_SKIP_MARGIN)
    need = jnp.logical_or(need_h[0::2], need_h[1::2])
    ii = lax.broadcasted_iota(jnp.int32, (1, nb, nb), 1)
    jj = lax.broadcasted_iota(jnp.int32, (1, nb, nb), 2)
    first_needed = jnp.min(jnp.where(jnp.logical_or(jnp.logical_and(need, jj < ii), jj == ii), jj, nb), axis=2)
    window = ii[:, :, 0] - first_needed + 1
    in_window = jnp.logical_and(jj >= first_needed[:, :, None], jj <= ii)
    last_query = jnp.max(jnp.where(in_window, ii, 0), axis=1)
    return jnp.concatenate([st.reshape(4 * N_HEADS, nb), window.astype(F32), last_query.astype(F32)], axis=0)


FOX_WINDOW_ROW = 4 * N_HEADS
FOX_LAST_QUERY_ROW = 4 * N_HEADS + N_PAIRS


def _head_needed(st_ref, h, i, j):
    bound = st_ref[h, i] * st_ref[2 * N_HEADS + h, j] + st_ref[N_HEADS + h, i] - st_ref[3 * N_HEADS + h, j]
    return jnp.logical_not(bound < -FOX_SKIP_MARGIN)


def _lower_triangle(n):
    return lax.broadcasted_iota(jnp.int32, (n, n), 1) <= lax.broadcasted_iota(jnp.int32, (n, n), 0)


def _fox_spans(s_len):
    nq = s_len // min(ATT_BLOCK, s_len)
    return [s for s in FOX_SHORT_SPANS if s < nq] + [nq]


def _fox_span_index(stats, s_len):
    longest = jnp.max(stats[FOX_WINDOW_ROW:FOX_WINDOW_ROW + N_PAIRS])
    idx = jnp.int32(0)
    for s in _fox_spans(s_len)[:-1]:
        idx = idx + (longest > s).astype(jnp.int32)
    return idx


def _fox_fwd(span, fqkv, aq, ak, stats):
    s_len = fqkv.shape[0]
    bq = bk = min(ATT_BLOCK, s_len)
    nq = s_len // bq

    def body(st_ref, q_ref, k_ref, v_ref, aq_ref, ak_ref, o_ref, lse_ref, qa_ref, m_ref, acc_ref):
        pair, qi, back = pl.program_id(0), pl.program_id(1), pl.program_id(2)
        kj = jnp.maximum(qi - back, 0)
        in0 = _first_head_lanes()
        not0 = jnp.logical_not(in0)

        @pl.when(back == 0)
        def _():
            q2, a2 = q_ref[...], aq_ref[...]
            qa_ref[0] = jnp.where(in0, q2, a2)
            qa_ref[1] = jnp.where(in0, a2, q2)
            m_ref[...] = jnp.full_like(m_ref, NEG)
            acc_ref[...] = jnp.zeros_like(acc_ref)

        def head_step(a, mine, masked):
            k2, v2 = k_ref[...], v_ref[...]
            s = _dot_nt(qa_ref[a], jnp.where(mine, k2, ak_ref[...]))
            if masked:
                s = jnp.where(_lower_triangle(bq), s, NEG)
            m_old = m_ref[a]
            m_new = jnp.maximum(m_old, jnp.max(s, axis=1, keepdims=True))
            p = jnp.exp(s - jnp.tile(m_new, (1, bk // PAIR))).astype(BF16)
            acc_ref[a] = jnp.exp(m_old - m_new) * acc_ref[a] + _dot(p, jnp.where(mine, v2, jnp.ones_like(v2)))
            m_ref[a] = m_new

        @pl.when(back == 0)
        def _():
            head_step(0, in0, True)
            head_step(1, not0, True)

        below = jnp.logical_and(back > 0, back <= qi)
        for a, mine in ((0, in0), (1, not0)):
            @pl.when(jnp.logical_and(below, _head_needed(st_ref, 2 * pair + a, qi, kj)))
            def _():
                head_step(a, mine, False)

        @pl.when(back == jnp.minimum(qi, span - 1))
        def _():
            acc0, acc1 = acc_ref[0], acc_ref[1]
            l2 = pltpu.roll(_pick(in0, acc1, acc0), HEAD_DIM, 1)
            o_ref[...] = _pick(in0, acc0, acc1) / l2
            lse_ref[...] = _pick(in0, m_ref[0], m_ref[1]) + jnp.log(l2)

    blk = lambda rows: (rows, PAIR)
    qmap = lambda p, i, b, st: (i, p)

    def key_block(p, i, b, st):
        window = jnp.clip(st[FOX_WINDOW_ROW + p, i].astype(jnp.int32), 1, i + 1)
        return i - jnp.minimum(b, window - 1)

    return pl.pallas_call(
        body, name=f"fox_fwd_{span}",
        grid_spec=pltpu.PrefetchScalarGridSpec(
            num_scalar_prefetch=1, grid=(N_PAIRS, nq, span),
            in_specs=[
                pl.BlockSpec(blk(bq), qmap),
                pl.BlockSpec(blk(bk), lambda p, i, b, st: (key_block(p, i, b, st), N_PAIRS + p)),
                pl.BlockSpec(blk(bk), lambda p, i, b, st: (key_block(p, i, b, st), 2 * N_PAIRS + p)),
                pl.BlockSpec(blk(bq), qmap),
                pl.BlockSpec(blk(bk), lambda p, i, b, st: (key_block(p, i, b, st), p)),
            ],
            out_specs=[pl.BlockSpec(blk(bq), qmap), pl.BlockSpec(blk(bq), qmap)],
            scratch_shapes=[pltpu.VMEM((2, bq, PAIR), BF16), pltpu.VMEM((2, bq, PAIR), F32), pltpu.VMEM((2, bq, PAIR), F32)]),
        out_shape=[_sds((s_len, HEAD_WIDTH), F32), _sds((s_len, HEAD_WIDTH), F32)],
        compiler_params=pltpu.CompilerParams(dimension_semantics=("arbitrary",) * 3, vmem_limit_bytes=V7X_VMEM_LIMIT_BYTES),
    )(stats, fqkv, fqkv, fqkv, aq, ak)


def _t5_bucket(dist):
    max_exact = N_BUCKETS // 2
    d = np.maximum(dist, 1).astype(np.float32)
    large = max_exact + (np.log(d / max_exact) / np.log(MAX_DISTANCE / max_exact) * (N_BUCKETS - max_exact)).astype(np.int32)
    large = np.minimum(large, N_BUCKETS - 1)
    return np.where(dist < max_exact, dist, large).astype(np.int32)


def _dil_buckets():
    w = DIL_BLOCK
    qi = np.arange(w)[:, None]
    kj = np.arange(2 * w)[None, :]
    sub = qi + w - kj
    band = (sub >= 0) & (sub <= w)
    out = [np.where(band, _t5_bucket(np.clip(sub, 0, w) * dil), -1) for dil in DILATIONS]
    return np.stack(out).astype(np.int32)


def _dil_bias(rel_bias, buckets):
    w = DIL_BLOCK

    def body(rb_ref, bk_ref, o_ref):
        for p in range(len(DILATIONS)):
            bk = bk_ref[p]
            for h in range(N_HEADS):
                def add(b, acc):
                    return acc + jnp.where(bk == b, rb_ref[b, h], 0.0)
                acc = lax.fori_loop(0, N_BUCKETS, add, jnp.zeros((w, 2 * w), F32))
                o_ref[p, h] = jnp.where(bk < 0, NEG, acc)

    return pl.pallas_call(
        body, name="dil_bias",
        in_specs=[pl.BlockSpec(memory_space=pltpu.SMEM), pl.BlockSpec(memory_space=pltpu.VMEM)],
        out_specs=pl.BlockSpec(memory_space=pltpu.VMEM),
        out_shape=_sds((len(DILATIONS), N_HEADS, w, 2 * w), F32),
    )(rel_bias, buckets)


def _dil_fwd(view, bias, branch):
    dil = DILATIONS[branch]
    w = DIL_BLOCK
    hw = HEAD_WIDTH
    length = view.shape[0]
    nb = length // w

    def body(q_ref, kc_ref, kp_ref, vc_ref, vp_ref, b_ref, o_ref, lse_ref):
        n = pl.program_id(1)
        in0 = _first_head_lanes()
        not0 = jnp.logical_not(in0)
        pairs = [slice(pr * PAIR, (pr + 1) * PAIR) for pr in range(N_PAIRS)]
        tiles = []
        for sl in pairs:
            q2 = q_ref[:, sl]
            qq = jnp.concatenate([_zero_other(in0, q2), _zero_other(not0, q2)], axis=0)
            tiles.append(jnp.concatenate([_dot_nt(qq, kp_ref[:, sl]), _dot_nt(qq, kc_ref[:, sl])], axis=1))
        s = jnp.concatenate(tiles, axis=0) + b_ref[...].reshape(N_HEADS * w, 2 * w)
        prev_half = lax.broadcasted_iota(jnp.int32, (1, 2 * w), 1) < w
        s = jnp.where(jnp.logical_and(n == 0, prev_half), NEG, s)
        m = jnp.max(s, axis=1, keepdims=True)
        e = jnp.exp(s - m)
        l = jnp.sum(e, axis=1, keepdims=True)
        p = (e / l).astype(BF16)
        lse = m + jnp.log(l)
        for pr, sl in enumerate(pairs):
            pp = p[2 * pr * w:(2 * pr + 2) * w]
            o2 = _dot(pp[:, :w], vp_ref[:, sl]) + _dot(pp[:, w:], vc_ref[:, sl])
            o_ref[:, sl] = _pick(in0, o2[:w], o2[w:])
            lse_ref[:, sl] = _pick(in0, lse[2 * pr * w:(2 * pr + 1) * w], lse[(2 * pr + 1) * w:(2 * pr + 2) * w])

    prev = lambda n: jnp.maximum(n - 1, 0)
    out = pl.pallas_call(
        body, name=f"dil_fwd_{dil}", grid=(dil, nb),
        in_specs=[
            pl.BlockSpec((w, hw), lambda r, n: (n, 3 * r)),
            pl.BlockSpec((w, hw), lambda r, n: (n, 3 * r + 1)),
            pl.BlockSpec((w, hw), lambda r, n: (prev(n), 3 * r + 1)),
            pl.BlockSpec((w, hw), lambda r, n: (n, 3 * r + 2)),
            pl.BlockSpec((w, hw), lambda r, n: (prev(n), 3 * r + 2)),
            pl.BlockSpec((None, N_HEADS, w, 2 * w), lambda r, n: (branch, 0, 0, 0)),
        ],
        out_specs=[pl.BlockSpec((w, hw), lambda r, n: (n, r)), pl.BlockSpec((w, hw), lambda r, n: (n, r))],
        out_shape=[_sds((length, dil * hw), F32), _sds((length, dil * hw), F32)],
        compiler_params=pltpu.CompilerParams(dimension_semantics=("arbitrary",) * 2, vmem_limit_bytes=V7X_VMEM_LIMIT_BYTES),
    )(view, view, view, view, view, bias)
    return out[0], out[1]


def _mix_out(o_fox, o_br, lse_br, w_out, x, g_post):
    s_len, d = x.shape
    hw = HEAD_WIDTH
    tm = ROW_TILE

    def body(of_ref, o1, o2, o3, l1, l2, l3, w_ref, x_ref, g_ref, x1_ref, y1_ref, od_ref, lj1, lj2, lj3, buf):
        def natural(ref, dil):
            return ref[...] if dil == 1 else _read_class_major(ref, buf, dil)

        ob = [natural(r, dil) for r, dil in zip((o1, o2, o3), DILATIONS)]
        la, lb, lc = [natural(r, dil) for r, dil in zip((l1, l2, l3), DILATIONS)]
        m = jnp.maximum(jnp.maximum(la, lb), lc)
        ea, eb, ec = jnp.exp(la - m), jnp.exp(lb - m), jnp.exp(lc - m)
        tot = ea + eb + ec
        o_dil = (ea / tot) * ob[0] + (eb / tot) * ob[1] + (ec / tot) * ob[2]
        od_ref[...] = o_dil
        lj = m + jnp.log(tot)
        _store_lane_blocks(buf, lj)
        for ref, dil in zip((lj1, lj2, lj3), DILATIONS):
            if dil == 1:
                ref[...] = lj
            else:
                _write_class_major(buf, ref, dil)
        y = _dot(of_ref[...].astype(BF16), w_ref[0:hw, :]) + _dot(o_dil.astype(BF16), w_ref[hw:2 * hw, :])
        y1_ref[...] = y
        x1_ref[...] = x_ref[...] + _rms_fwd(y, g_ref[...])

    half = _rows(tm, hw)
    by_dil = [_class_rows(tm, hw, dil) for dil in DILATIONS]
    return _call(
        body, name="mix_out", grid=(s_len // tm,),
        in_specs=[half] + by_dil + by_dil + [_resident(w_out.shape), _rows(tm, d), _resident((1, d))],
        out_specs=[_rows(tm, d), _rows(tm, d), half] + by_dil,
        out_shape=[_sds((s_len, d), F32), _sds((s_len, d), F32), _sds((s_len, hw), F32)]
        + [_sds((s_len // dil, dil * hw), F32) for dil in DILATIONS],
        scratch=[pltpu.VMEM((hw // PAIR, tm, PAIR), F32)],
    )(o_fox, *o_br, *lse_br, w_out, x, g_post)


def _mem_fwd(mem, g_mem, w_xk, w_xv):
    n_mem, d = mem.shape
    mw = w_xk.shape[1]

    def body(mem_ref, g_ref, wk_ref, wv_ref, hm_ref, k_ref, v_ref):
        hm = _rms_fwd(mem_ref[...], g_ref[...]).astype(BF16)
        hm_ref[...] = hm
        k_ref[...] = _dot(hm, wk_ref[...]).astype(BF16)
        v_ref[...] = _dot(hm, wv_ref[...]).astype(BF16)

    return pl.pallas_call(
        body, name="mem_fwd",
        out_shape=[_sds((n_mem, d), BF16), _sds((n_mem, mw), BF16), _sds((n_mem, mw), BF16)],
    )(mem, g_mem, w_xk, w_xv)


def _xattn_softmax(qa, k2):
    s = _dot_nt(qa, k2)
    m = jnp.max(s, axis=1, keepdims=True)
    e = jnp.exp(s - m)
    return e / jnp.sum(e, axis=1, keepdims=True)


def _xattn_fwd(x1, g_pre, w_xq, kx, vx, w_xo, g_post):
    s_len, d = x1.shape
    mw = w_xq.shape[1]
    n_mem = kx.shape[0]
    tm = ROW_TILE

    def body(x_ref, gp_ref, wq_ref, k_ref, v_ref, wo_ref, go_ref, x2_ref, y2_ref, h2_ref, q_ref, o_ref):
        x = x_ref[...]
        h = _rms_fwd(x, gp_ref[...]).astype(BF16)
        h2_ref[...] = h
        q = (_dot(h, wq_ref[...]) * QK_SCALE).astype(BF16)
        q_ref[...] = q
        in0 = _first_head_lanes()
        not0 = jnp.logical_not(in0)
        for pr in range(mw // PAIR):
            sl = slice(pr * PAIR, (pr + 1) * PAIR)
            q2, k2, v2 = q[:, sl], k_ref[:, sl], v_ref[:, sl]
            oa = [_dot(_xattn_softmax(_zero_other(mine, q2), k2).astype(BF16), v2) for mine in (in0, not0)]
            o_ref[:, sl] = _pick(in0, oa[0], oa[1]).astype(BF16)
        y = _dot(o_ref[...], wo_ref[...])
        y2_ref[...] = y
        x2_ref[...] = x + _rms_fwd(y, go_ref[...])

    return _call(
        body, name="xattn_fwd", grid=(s_len // tm,),
        in_specs=[_rows(tm, d), _resident((1, d)), _resident(w_xq.shape), _resident((n_mem, mw)), _resident((n_mem, mw)),
                  _resident(w_xo.shape), _resident((1, d))],
        out_specs=[_rows(tm, d), _rows(tm, d), _rows(tm, d), _rows(tm, mw), _rows(tm, mw)],
        out_shape=[_sds((s_len, d), F32), _sds((s_len, d), F32), _sds((s_len, d), BF16), _sds((s_len, mw), BF16),
                   _sds((s_len, mw), BF16)],
    )(x1, g_pre, w_xq, kx, vx, w_xo, g_post)


def _ffn_up(x2, g_pre, w_gate, w_up):
    s_len, d = x2.shape
    dff = w_gate.shape[1]
    tm = ROW_TILE

    def body(x_ref, g_ref, wg_ref, wu_ref, h_ref, a_ref, u_ref, z_ref):
        h = _rms_fwd(x_ref[...], g_ref[...]).astype(BF16)
        h_ref[...] = h
        a = _dot(h, wg_ref[...])
        u = _dot(h, wu_ref[...])
        a_ref[...] = a.astype(BF16)
        u_ref[...] = u.astype(BF16)
        z_ref[...] = ((a * jax.nn.sigmoid(a)) * u).astype(BF16)

    return _call(
        body, name="ffn_up", grid=(s_len // tm,),
        in_specs=[_rows(tm, d), _resident((1, d)), _resident(w_gate.shape), _resident(w_up.shape)],
        out_specs=[_rows(tm, d), _rows(tm, dff), _rows(tm, dff), _rows(tm, dff)],
        out_shape=[_sds((s_len, d), BF16)] + [_sds((s_len, dff), BF16)] * 3,
    )(x2, g_pre, w_gate, w_up)


def _ffn_down_loss(z, w_down, x2, g_post, target):
    s_len, d = x2.shape
    dff = z.shape[1]
    tm = ROW_TILE

    def body(z_ref, w_ref, x_ref, g_ref, t_ref, y_ref, dx_ref, sq_ref):
        @pl.when(pl.program_id(0) == 0)
        def _():
            sq_ref[...] = jnp.zeros_like(sq_ref)

        y = _dot(z_ref[...], w_ref[...])
        y_ref[...] = y
        err = (x_ref[...] + _rms_fwd(y, g_ref[...])) - t_ref[...]
        sq_ref[...] += jnp.sum(err * err, axis=0, keepdims=True)
        dx_ref[...] = err * (1.0 / d)

    return _call(
        body, name="ffn_down_loss", grid=(s_len // tm,),
        in_specs=[_rows(tm, dff), _resident(w_down.shape), _rows(tm, d), _resident((1, d)), _rows(tm, d)],
        out_specs=[_rows(tm, d), _rows(tm, d), _acc_out((1, d))],
        out_shape=[_sds((s_len, d), F32), _sds((s_len, d), F32), _sds((1, d), F32)],
    )(z, w_down, x2, g_post, target)


def _weight_grad(a, b, name):
    s_len, k = a.shape
    n = b.shape[1]
    ts = 512 if s_len % 512 == 0 else s_len
    tn = n
    while k * tn * 4 > 8 * 2 ** 20 and tn % 256 == 0:
        tn //= 2

    def body(a_ref, b_ref, o_ref):
        @pl.when(pl.program_id(1) == 0)
        def _():
            o_ref[...] = jnp.zeros_like(o_ref)

        o_ref[...] += _dot_tn(a_ref[...].astype(BF16), b_ref[...].astype(BF16))

    return pl.pallas_call(
        body, name=name, grid=(n // tn, s_len // ts),
        in_specs=[pl.BlockSpec((ts, k), lambda j, i: (i, 0)), pl.BlockSpec((ts, tn), lambda j, i: (i, j))],
        out_specs=pl.BlockSpec((k, tn), lambda j, i: (0, j)),
        out_shape=_sds((k, n), F32),
        compiler_params=pltpu.CompilerParams(dimension_semantics=("arbitrary",) * 2, vmem_limit_bytes=V7X_VMEM_LIMIT_BYTES),
    )(a, b)


def _ffn_bwd_a(dx3, y3, g_post, w_down_t, a, u):
    s_len, d = dx3.shape
    dff = a.shape[1]
    tm = ROW_TILE

    def body(dx_ref, y_ref, g_ref, w_ref, a_ref, u_ref, dy_ref, da_ref, du_ref, dg_ref):
        @pl.when(pl.program_id(0) == 0)
        def _():
            dg_ref[...] = jnp.zeros_like(dg_ref)

        dy, dg = _rms_bwd(y_ref[...], g_ref[...], dx_ref[...])
        dg_ref[...] += dg
        dyb = dy.astype(BF16)
        dy_ref[...] = dyb
        dz = _dot(dyb, w_ref[...])
        av = a_ref[...].astype(F32)
        uv = u_ref[...].astype(F32)
        sg = jax.nn.sigmoid(av)
        da_ref[...] = (dz * uv * (sg * (1.0 + av * (1.0 - sg)))).astype(BF16)
        du_ref[...] = (dz * (av * sg)).astype(BF16)

    return _call(
        body, name="ffn_bwd_a", grid=(s_len // tm,),
        in_specs=[_rows(tm, d), _rows(tm, d), _resident((1, d)), _resident(w_down_t.shape), _rows(tm, dff), _rows(tm, dff)],
        out_specs=[_rows(tm, d), _rows(tm, dff), _rows(tm, dff), _acc_out((1, d))],
        out_shape=[_sds((s_len, d), BF16), _sds((s_len, dff), BF16), _sds((s_len, dff), BF16), _sds((1, d), F32)],
    )(dx3, y3, g_post, w_down_t, a, u)


def _ffn_bwd_b(da, du, w_gate_t, w_up_t, dx3, x2, g_pre):
    s_len, d = x2.shape
    dff = da.shape[1]
    tm = ROW_TILE

    def body(da_ref, du_ref, wg_ref, wu_ref, dx_ref, x_ref, g_ref, o_ref, dg_ref):
        @pl.when(pl.program_id(0) == 0)
        def _():
            dg_ref[...] = jnp.zeros_like(dg_ref)

        dh = _dot(da_ref[...], wg_ref[...]) + _dot(du_ref[...], wu_ref[...])
        dx, dg = _rms_bwd(x_ref[...], g_ref[...], dh)
        dg_ref[...] += dg
        o_ref[...] = dx_ref[...] + dx

    return _call(
        body, name="ffn_bwd_b", grid=(s_len // tm,),
        in_specs=[_rows(tm, dff), _rows(tm, dff), _resident(w_gate_t.shape), _resident(w_up_t.shape), _rows(tm, d),
                  _rows(tm, d), _resident((1, d))],
        out_specs=[_rows(tm, d), _acc_out((1, d))],
        out_shape=[_sds((s_len, d), F32), _sds((1, d), F32)],
    )(da, du, w_gate_t, w_up_t, dx3, x2, g_pre)


def _xattn_bwd(dx2, y2, g_post, w_xo_t, q, kx, vx, w_xq_t, x1, g_pre):
    s_len, d = x1.shape
    mw = q.shape[1]
    n_mem = kx.shape[0]
    tm = ROW_TILE

    def body(dx_ref, y_ref, go_ref, wo_ref, q_ref, k_ref, v_ref, wq_ref, x_ref, gp_ref,
             dx1_ref, dy_ref, dq_ref, dk_ref, dv_ref, dgo_ref, dgp_ref):
        @pl.when(pl.program_id(0) == 0)
        def _():
            dk_ref[...] = jnp.zeros_like(dk_ref)
            dv_ref[...] = jnp.zeros_like(dv_ref)
            dgo_ref[...] = jnp.zeros_like(dgo_ref)
            dgp_ref[...] = jnp.zeros_like(dgp_ref)

        dxin = dx_ref[...]
        dy, dgo = _rms_bwd(y_ref[...], go_ref[...], dxin)
        dgo_ref[...] += dgo
        dyb = dy.astype(BF16)
        dy_ref[...] = dyb
        do = _dot(dyb, wo_ref[...]).astype(BF16)
        in0 = _first_head_lanes()
        not0 = jnp.logical_not(in0)
        for pr in range(mw // PAIR):
            sl = slice(pr * PAIR, (pr + 1) * PAIR)
            q2, k2, v2, do2 = q_ref[:, sl], k_ref[:, sl], v_ref[:, sl], do[:, sl]
            dqs = []
            dk2 = jnp.zeros((n_mem, PAIR), F32)
            dv2 = jnp.zeros((n_mem, PAIR), F32)
            for mine in (in0, not0):
                qa = _zero_other(mine, q2)
                doa = _zero_other(mine, do2)
                p = _xattn_softmax(qa, k2)
                dp = _dot_nt(doa, v2)
                ds = (p * (dp - jnp.sum(p * dp, axis=1, keepdims=True))).astype(BF16)
                dqs.append(_dot(ds, k2))
                dk2 = dk2 + _dot_tn(ds, qa)
                dv2 = dv2 + _dot_tn(p.astype(BF16), doa)
            dq_ref[:, sl] = (_pick(in0, dqs[0], dqs[1]) * QK_SCALE).astype(BF16)
            dk_ref[:, sl] += dk2
            dv_ref[:, sl] += dv2
        dh = _dot(dq_ref[...], wq_ref[...])
        dx, dgp = _rms_bwd(x_ref[...], gp_ref[...], dh)
        dgp_ref[...] += dgp
        dx1_ref[...] = dxin + dx

    return _call(
        body, name="xattn_bwd", grid=(s_len // tm,),
        in_specs=[_rows(tm, d), _rows(tm, d), _resident((1, d)), _resident(w_xo_t.shape), _rows(tm, mw),
                  _resident((n_mem, mw)), _resident((n_mem, mw)), _resident(w_xq_t.shape), _rows(tm, d), _resident((1, d))],
        out_specs=[_rows(tm, d), _rows(tm, d), _rows(tm, mw), _acc_out((n_mem, mw)), _acc_out((n_mem, mw)),
                   _acc_out((1, d)), _acc_out((1, d))],
        out_shape=[_sds((s_len, d), F32), _sds((s_len, d), BF16), _sds((s_len, mw), BF16), _sds((n_mem, mw), F32),
                   _sds((n_mem, mw), F32), _sds((1, d), F32), _sds((1, d), F32)],
    )(dx2, y2, g_post, w_xo_t, q, kx, vx, w_xq_t, x1, g_pre)


def _mem_bwd(dk, dv, w_xk_t, w_xv_t, hm, mem, g_mem):
    n_mem, d = mem.shape
    mw = dk.shape[1]

    def body(dk_ref, dv_ref, wk_ref, wv_ref, hm_ref, mem_ref, g_ref, dwk_ref, dwv_ref, dg_ref):
        dkb = dk_ref[...].astype(BF16)
        dvb = dv_ref[...].astype(BF16)
        dhm = _dot(dkb, wk_ref[...]) + _dot(dvb, wv_ref[...])
        _, dg = _rms_bwd(mem_ref[...], g_ref[...], dhm)
        dg_ref[...] = dg
        dwk_ref[...] = _dot_tn(hm_ref[...], dkb)
        dwv_ref[...] = _dot_tn(hm_ref[...], dvb)

    return pl.pallas_call(
        body, name="mem_bwd",
        out_shape=[_sds((d, mw), F32), _sds((d, mw), F32), _sds((1, d), F32)],
    )(dk, dv, w_xk_t, w_xv_t, hm, mem, g_mem)


def _mix_out_bwd(dx1, y1, g_post, w_out_t, o_fox, o_dil):
    s_len, d = dx1.shape
    hw = HEAD_WIDTH
    tm = ROW_TILE
    ones = _head_block_ones().astype(BF16)

    def body(dx_ref, y_ref, g_ref, w_ref, of_ref, od_ref, ones_ref, dy_ref, dof_ref, dlf_ref, dg_ref,
             dod1, dod2, dod3, dld1, dld2, dld3, buf):
        @pl.when(pl.program_id(0) == 0)
        def _():
            dg_ref[...] = jnp.zeros_like(dg_ref)

        dy, dg = _rms_bwd(y_ref[...], g_ref[...], dx_ref[...])
        dg_ref[...] += dg
        dyb = dy.astype(BF16)
        dy_ref[...] = dyb
        do = _dot(dyb, w_ref[...])
        def head_sums(x):
            hi = x.astype(BF16)
            lo = (x - hi.astype(F32)).astype(BF16)
            return _dot(hi, ones_ref[...]) + _dot(lo, ones_ref[...])

        dof_ref[...] = do[:, 0:hw].astype(BF16)
        dlf_ref[...] = head_sums(do[:, 0:hw] * of_ref[...])
        do_dil = do[:, hw:2 * hw]
        dl_dil = head_sums(do_dil * od_ref[...])
        for val, refs in ((do_dil, (dod1, dod2, dod3)), (dl_dil, (dld1, dld2, dld3))):
            _store_lane_blocks(buf, val)
            for ref, dil in zip(refs, DILATIONS):
                if dil == 1:
                    ref[...] = val.astype(ref.dtype)
                else:
                    _write_class_major(buf, ref, dil)

    half = _rows(tm, hw)
    by_dil = [_class_rows(tm, hw, dil) for dil in DILATIONS]
    outs = _call(
        body, name="mix_out_bwd", grid=(s_len // tm,),
        in_specs=[_rows(tm, d), _rows(tm, d), _resident((1, d)), _resident(w_out_t.shape), half, half, _resident((hw, hw))],
        out_specs=[_rows(tm, d), half, half, _acc_out((1, d))] + by_dil + by_dil,
        out_shape=[_sds((s_len, d), BF16), _sds((s_len, hw), BF16), _sds((s_len, hw), F32), _sds((1, d), F32)]
        + [_sds((s_len // dil, dil * hw), BF16) for dil in DILATIONS]
        + [_sds((s_len // dil, dil * hw), F32) for dil in DILATIONS],
        scratch=[pltpu.VMEM((hw // PAIR, tm, PAIR), F32)],
    )(dx1, y1, g_post, w_out_t, o_fox, o_dil, ones)
    return outs[0], outs[1], outs[2], outs[3], outs[4:7], outs[7:10]


def _fox_bwd_prep(aq, lse, delta):
    s_len, hw = aq.shape
    tm = ROW_TILE

    def body(aq_ref, lse_ref, dl_ref, aql_ref, ad_ref):
        jj = _lane_in_head(hw)
        l3 = _split3(_swap_halves(lse_ref[...]))
        aql_ref[...] = _place3(jj, 6, [-p for p in l3], aq_ref[...])
        d3 = _split3(_swap_halves(dl_ref[...]))
        ad_ref[...] = _place3(jj, 0, [-p for p in d3], jnp.zeros((tm, hw), BF16))

    half = _rows(tm, hw)
    return _call(
        body, name="fox_bwd_prep", grid=(s_len // tm,),
        in_specs=[half, half, half], out_specs=[half, half],
        out_shape=[_sds((s_len, hw), BF16), _sds((s_len, hw), BF16)],
    )(aq, lse, delta)


def _ones_on_first3(shape):
    jj = lax.broadcasted_iota(jnp.int32, shape, 1) % HEAD_DIM
    return jnp.where(jj < 3, 1.0, 0.0).astype(BF16)


def _fox_bwd(span, fqkv, do, aql, ak, ad, stats):
    s_len = fqkv.shape[0]
    bq = bk = min(ATT_BLOCK, s_len)
    nq = nk = s_len // bq

    def body(st_ref, q_ref, k_ref, v_ref, do_ref, aql_ref, ak_ref, ad_ref, dq_ref, rs_ref, dk_ref, dv_ref, dc_ref,
             ka_ref, va_ref, kone_ref, r_ref, dvacc_ref, dqacc_ref):
        pair, kj, ahead = pl.program_id(0), pl.program_id(1), pl.program_id(2)
        valid = kj + ahead < nq
        qi = jnp.minimum(kj + ahead, nq - 1)
        in0 = _first_head_lanes()
        not0 = jnp.logical_not(in0)
        heads = ((0, in0), (1, not0))
        rows = pl.ds(pl.multiple_of(qi * bq, bq), bq)

        @pl.when(ahead == 0)
        def _():
            k2, v2, a2 = k_ref[...], v_ref[...], ak_ref[...]
            one = jnp.ones_like(k2)
            one3 = _ones_on_first3(v2.shape)
            for a, mine in heads:
                ka_ref[a] = jnp.where(mine, k2, a2)
                va_ref[a] = jnp.where(mine, v2, one3)
                kone_ref[a] = jnp.where(mine, k2, one)
            r_ref[...] = jnp.zeros_like(r_ref)
            dvacc_ref[...] = jnp.zeros_like(dvacc_ref)

        @pl.when(jnp.logical_and(valid, jnp.logical_or(kj == 0, ahead == span - 1)))
        def _():
            for a, _ in heads:
                dqacc_ref[a, rows, :] = jnp.zeros((bq, PAIR), F32)

        def head_step(a, mine, masked):
            q2 = q_ref[...]
            doa = jnp.where(mine, do_ref[...], ad_ref[...])
            s = _dot_nt(jnp.where(mine, q2, aql_ref[...]), ka_ref[a])
            if masked:
                s = jnp.where(_lower_triangle(bq), s, NEG)
            p = jnp.exp(s)
            ds = (p * _dot_nt(doa, va_ref[a])).astype(BF16)
            dvacc_ref[a] += _dot_tn(p.astype(BF16), doa)
            r_ref[a] += _dot_tn(ds, jnp.where(mine, q2, jnp.ones_like(q2)))
            dqacc_ref[a, rows, :] += _dot(ds, kone_ref[a])

        for a, mine in heads:
            @pl.when(jnp.logical_and(jnp.logical_and(valid, ahead > 0), _head_needed(st_ref, 2 * pair + a, qi, kj)))
            def _():
                head_step(a, mine, False)

        @pl.when(ahead == 0)
        def _():
            for a, mine in heads:
                head_step(a, mine, True)
            acc0, acc1 = dqacc_ref[0, rows, :], dqacc_ref[1, rows, :]
            dq_ref[...] = (_pick(in0, acc0, acc1) * QK_SCALE).astype(BF16)
            rs_ref[...] = pltpu.roll(_pick(in0, acc1, acc0), HEAD_DIM, 1)

        @pl.when(ahead == span - 1)
        def _():
            dk_ref[...] = _pick(in0, r_ref[0], r_ref[1]).astype(BF16)
            dv_ref[...] = _pick(in0, dvacc_ref[0], dvacc_ref[1]).astype(BF16)
            dc_ref[...] = -pltpu.roll(_pick(in0, r_ref[1], r_ref[0]), HEAD_DIM, 1)

    blk = lambda n: (n, PAIR)
    kvmap = lambda p, j, t, st: (j, p)

    def qmap(p, j, t, st):
        last = jnp.clip(st[FOX_LAST_QUERY_ROW + p, j].astype(jnp.int32), j, nq - 1)
        return (jnp.minimum(j + t, last), p)

    return pl.pallas_call(
        body, name=f"fox_bwd_{span}",
        grid_spec=pltpu.PrefetchScalarGridSpec(
            num_scalar_prefetch=1, grid=(N_PAIRS, nk, span),
            in_specs=[
                pl.BlockSpec(blk(bq), qmap),
                pl.BlockSpec(blk(bk), lambda p, j, t, st: (j, N_PAIRS + p)),
                pl.BlockSpec(blk(bk), lambda p, j, t, st: (j, 2 * N_PAIRS + p)),
                pl.BlockSpec(blk(bq), qmap), pl.BlockSpec(blk(bq), qmap),
                pl.BlockSpec(blk(bk), kvmap),
                pl.BlockSpec(blk(bq), qmap),
            ],
            out_specs=[pl.BlockSpec(blk(bk), kvmap)] * 5,
            scratch_shapes=[pltpu.VMEM((2, bk, PAIR), BF16), pltpu.VMEM((2, bk, PAIR), BF16), pltpu.VMEM((2, bk, PAIR), BF16),
                            pltpu.VMEM((2, bk, PAIR), F32), pltpu.VMEM((2, bk, PAIR), F32), pltpu.VMEM((2, s_len, PAIR), F32)]),
        out_shape=[_sds((s_len, HEAD_WIDTH), BF16), _sds((s_len, HEAD_WIDTH), F32), _sds((s_len, HEAD_WIDTH), BF16),
                   _sds((s_len, HEAD_WIDTH), BF16), _sds((s_len, HEAD_WIDTH), F32)],
        compiler_params=pltpu.CompilerParams(dimension_semantics=("arbitrary",) * 3, vmem_limit_bytes=V7X_VMEM_LIMIT_BYTES),
    )(stats, fqkv, fqkv, fqkv, do, aql, ak, ad)


def _gate_bwd(rs, dc, gx, b_exp):
    s_len, hw = gx.shape
    t = min(SCAN_TILE, s_len)
    nt = s_len // t
    tri = jnp.asarray(np.triu(np.ones((t, t), np.float32))).astype(BF16)

    def body(rs_ref, dc_ref, gx_ref, b_ref, tri_ref, dgx_ref, db_ref, carry):
        @pl.when(pl.program_id(0) == 0)
        def _():
            carry[...] = jnp.zeros_like(carry)
            db_ref[...] = jnp.zeros_like(db_ref)

        dlf = _tri_dot(tri_ref[...], rs_ref[...] + dc_ref[...]) + carry[...]
        carry[...] = dlf[0:1, :]
        dgate = dlf * jax.nn.sigmoid(-(gx_ref[...] + b_ref[...]))
        db_ref[...] += jnp.sum(dgate, axis=0, keepdims=True)
        lane = lax.broadcasted_iota(jnp.int32, (1, hw), 1)
        dgx_ref[...] = jnp.where(lane % HEAD_DIM == 0, dgate, 0.0).astype(BF16)

    rev = lambda i: (nt - 1 - i, 0)
    return _call(
        body, name="gate_bwd", grid=(nt,),
        in_specs=[pl.BlockSpec((t, hw), rev)] * 3 + [_resident((1, hw)), _resident((t, t))],
        out_specs=[pl.BlockSpec((t, hw), rev), _acc_out((1, hw))],
        out_shape=[_sds((s_len, hw), BF16), _sds((1, hw), F32)],
        scratch=[pltpu.VMEM((1, hw), F32)],
    )(rs, dc, gx, b_exp, tri)


def _dil_bwd(qkv_v, do_v, lj_v, dl_v, bias, branch, riding=None):
    dil = DILATIONS[branch]
    w = DIL_BLOCK
    hw = HEAD_WIDTH
    length = qkv_v.shape[0]
    nb = length // w

    def body(q0_ref, q1_ref, kp_ref, kc_ref, vp_ref, vc_ref, do0_ref, do1_ref, l0_ref, l1_ref, d0_ref, d1_ref, b_ref, *rest):
        if riding is None:
            dq_ref, dk_ref, dv_ref, dsum_ref = rest
        else:
            g_ref, dq_ref, dk_ref, dv_ref, dsum_ref, land_ref, send_sems, recv_sems, local_sem = rest
            exchange = (g_ref, land_ref, send_sems, recv_sems, local_sem)
        r, n = pl.program_id(0), pl.program_id(1)
        in0 = _first_head_lanes()
        not0 = jnp.logical_not(in0)
        first = n == 0
        last = n == nb - 1

        @pl.when(jnp.logical_and(r == 0, n == 0))
        def _():
            dsum_ref[...] = jnp.zeros_like(dsum_ref)
            if riding is not None:
                _exchange_start(*exchange)

        pairs = [slice(pr * PAIR, (pr + 1) * PAIR) for pr in range(N_PAIRS)]

        def both_heads(ref, sl):
            v = ref[:, sl]
            return jnp.concatenate([_zero_other(in0, v), _zero_other(not0, v)], axis=0)

        def head_columns(ref):
            return jnp.concatenate([ref[:, h * HEAD_DIM:h * HEAD_DIM + 1] for h in range(N_HEADS)], axis=0)

        qq0 = [both_heads(q0_ref, sl) for sl in pairs]
        qq1 = [both_heads(q1_ref, sl) for sl in pairs]
        dd0 = [both_heads(do0_ref, sl) for sl in pairs]
        dd1 = [both_heads(do1_ref, sl) for sl in pairs]
        stack = lambda tiles: jnp.concatenate(tiles, axis=0)
        s_a = stack([_dot_nt(qq0[i], kp_ref[:, sl]) for i, sl in enumerate(pairs)])
        s_b = stack([_dot_nt(qq0[i], kc_ref[:, sl]) for i, sl in enumerate(pairs)])
        s_c = stack([_dot_nt(qq1[i], kc_ref[:, sl]) for i, sl in enumerate(pairs)])
        dp_a = stack([_dot_nt(dd0[i], vp_ref[:, sl]) for i, sl in enumerate(pairs)])
        dp_b = stack([_dot_nt(dd0[i], vc_ref[:, sl]) for i, sl in enumerate(pairs)])
        dp_c = stack([_dot_nt(dd1[i], vc_ref[:, sl]) for i, sl in enumerate(pairs)])
        bias2 = b_ref[...].reshape(N_HEADS * w, 2 * w)
        b_prev, b_cur = bias2[:, 0:w], bias2[:, w:2 * w]
        lse0, lse1 = head_columns(l0_ref), head_columns(l1_ref)
        dl0, dl1 = head_columns(d0_ref), head_columns(d1_ref)
        p_a = jnp.exp(jnp.where(first, NEG, s_a + b_prev) - lse0)
        p_b = jnp.exp((s_b + b_cur) - lse0)
        p_c = jnp.exp(jnp.where(last, NEG, s_c + b_prev) - lse1)
        ds_a = p_a * (dp_a - dl0)
        ds_b = p_b * (dp_b - dl0)
        ds_c = p_c * (dp_c - dl1)
        dsum_ref[...] += jnp.concatenate([ds_a, ds_b], axis=1).reshape(N_HEADS, w, 2 * w)
        ds_a, ds_b, ds_c = ds_a.astype(BF16), ds_b.astype(BF16), ds_c.astype(BF16)
        p_b, p_c = p_b.astype(BF16), p_c.astype(BF16)
        for i, sl in enumerate(pairs):
            rows = slice(2 * i * w, (2 * i + 2) * w)
            dq2 = _dot(ds_a[rows], kp_ref[:, sl]) + _dot(ds_b[rows], kc_ref[:, sl])
            dq_ref[:, sl] = _pick(in0, dq2[:w], dq2[w:]).astype(BF16)
            dk_ref[:, sl] = (_dot_tn(ds_b[rows], qq0[i]) + _dot_tn(ds_c[rows], qq1[i])).astype(BF16)
            dv_ref[:, sl] = (_dot_tn(p_b[rows], dd0[i]) + _dot_tn(p_c[rows], dd1[i])).astype(BF16)

        if riding is not None:
            @pl.when(jnp.logical_and(r == dil - 1, n == nb - 1))
            def _():
                _exchange_wait(*exchange)

    prev = lambda n: jnp.maximum(n - 1, 0)
    nxt = lambda n: jnp.minimum(n + 1, nb - 1)
    blk = (w, hw)
    anywhere = pl.BlockSpec(memory_space=pl.ANY)
    extra_in, extra_out, extra_shape, extra_scratch, extra_args = [], [], [], [], []
    if riding is not None:
        extra_in, extra_out, extra_shape = [anywhere], [anywhere], [_sds(riding.shape, riding.dtype)]
        extra_scratch, extra_args = list(EXCHANGE_SEMAPHORES), [riding]
    outs = pl.pallas_call(
        body, name=f"dil_bwd_{dil}", grid=(dil, nb),
        in_specs=[
            pl.BlockSpec(blk, lambda r, n: (n, 3 * r)),
            pl.BlockSpec(blk, lambda r, n: (nxt(n), 3 * r)),
            pl.BlockSpec(blk, lambda r, n: (prev(n), 3 * r + 1)),
            pl.BlockSpec(blk, lambda r, n: (n, 3 * r + 1)),
            pl.BlockSpec(blk, lambda r, n: (prev(n), 3 * r + 2)),
            pl.BlockSpec(blk, lambda r, n: (n, 3 * r + 2)),
            pl.BlockSpec(blk, lambda r, n: (n, r)),
            pl.BlockSpec(blk, lambda r, n: (nxt(n), r)),
            pl.BlockSpec(blk, lambda r, n: (n, r)),
            pl.BlockSpec(blk, lambda r, n: (nxt(n), r)),
            pl.BlockSpec(blk, lambda r, n: (n, r)),
            pl.BlockSpec(blk, lambda r, n: (nxt(n), r)),
            pl.BlockSpec((None, N_HEADS, w, 2 * w), lambda r, n: (branch, 0, 0, 0)),
        ] + extra_in,
        out_specs=[pl.BlockSpec(blk, lambda r, n: (n, r))] * 3 + [pl.BlockSpec((N_HEADS, w, 2 * w), lambda r, n: (0, 0, 0))]
        + extra_out,
        out_shape=[_sds((length, dil * hw), BF16)] * 3 + [_sds((N_HEADS, w, 2 * w), F32)] + extra_shape,
        scratch_shapes=extra_scratch,
        compiler_params=pltpu.CompilerParams(dimension_semantics=("arbitrary",) * 2, vmem_limit_bytes=V7X_VMEM_LIMIT_BYTES),
    )(qkv_v, qkv_v, qkv_v, qkv_v, qkv_v, qkv_v, do_v, do_v, lj_v, lj_v, dl_v, dl_v, bias, *extra_args)
    return list(outs)


def _rel_bias_grad(dsum, buckets):
    w = DIL_BLOCK

    def body(ds_ref, bk_ref, o_ref):
        row = lax.broadcasted_iota(jnp.int32, (N_BUCKETS, PAIR), 0)
        lane = lax.broadcasted_iota(jnp.int32, (N_BUCKETS, PAIR), 1)

        def per_bucket(b, acc):
            for p in range(len(DILATIONS)):
                hit = bk_ref[p] == b
                for h in range(N_HEADS):
                    part = jnp.where(hit, ds_ref[p, h], 0.0)
                    tot = jnp.sum(jnp.sum(part, axis=1, keepdims=True), axis=0, keepdims=True)
                    acc = acc + jnp.where(jnp.logical_and(row == b, lane == h), tot, 0.0)
            return acc

        o_ref[...] = lax.fori_loop(0, N_BUCKETS, per_bucket, jnp.zeros((N_BUCKETS, PAIR), F32))

    return pl.pallas_call(body, name="rel_bias_grad", out_shape=_sds((N_BUCKETS, PAIR), F32))(dsum, buckets)


def _in_proj_bwd(dfq, dfk, dfv, dgx, ddq, ddk, ddv, w_in_t, dx1, x, g_pre):
    s_len, d = x.shape
    hw = HEAD_WIDTH
    tm = ROW_TILE

    def body(fq, fk, fv, gx, q1, q2, q3, k1, k2, k3, v1, v2, v3, w_ref, dx_ref, x_ref, g_ref, o_ref, dp_ref, dg_ref, buf):
        @pl.when(pl.program_id(0) == 0)
        def _():
            dg_ref[...] = jnp.zeros_like(dg_ref)

        def branch_sum(refs):
            a, b, c = [r[...].astype(F32) if dil == 1 else _read_class_major(r, buf, dil) for r, dil in zip(refs, DILATIONS)]
            return (a + b) + c

        dp_ref[:, 0:hw] = fq[...]
        dp_ref[:, hw:2 * hw] = fk[...]
        dp_ref[:, 2 * hw:3 * hw] = fv[...]
        dp_ref[:, 3 * hw:4 * hw] = gx[...]
        dp_ref[:, 4 * hw:5 * hw] = (branch_sum((q1, q2, q3)) * QK_SCALE).astype(BF16)
        dp_ref[:, 5 * hw:6 * hw] = branch_sum((k1, k2, k3)).astype(BF16)
        dp_ref[:, 6 * hw:7 * hw] = branch_sum((v1, v2, v3)).astype(BF16)
        dh = _dot(dp_ref[...], w_ref[...])
        dx, dg = _rms_bwd(x_ref[...], g_ref[...], dh)
        dg_ref[...] += dg
        o_ref[...] = dx_ref[...] + dx

    half = _rows(tm, hw)
    by_dil = [_class_rows(tm, hw, dil) for dil in DILATIONS]
    return _call(
        body, name="in_proj_bwd", grid=(s_len // tm,),
        in_specs=[half] * 4 + by_dil * 3 + [_resident(w_in_t.shape), _rows(tm, d), _rows(tm, d), _resident((1, d))],
        out_specs=[_rows(tm, d), _rows(tm, 7 * hw), _acc_out((1, d))],
        out_shape=[_sds((s_len, d), F32), _sds((s_len, 7 * hw), BF16), _sds((1, d), F32)],
        scratch=[pltpu.VMEM((hw // PAIR, tm, PAIR), F32)],
    )(dfq, dfk, dfv, dgx, *ddq, *ddk, *ddv, w_in_t, dx1, x, g_pre)


def _expand_w_in(w_in):
    hw = HEAD_WIDTH
    gate = jnp.repeat(w_in[:, 3 * hw:3 * hw + N_HEADS], HEAD_DIM, axis=1)
    return jnp.concatenate([w_in[:, :3 * hw], gate, w_in[:, 3 * hw + N_HEADS:]], axis=1)


def _local_step(x, mem, target, g, w_bf, pack_early=None, ride_gather=None):
    hw = HEAD_WIDTH
    w_in_e = _expand_w_in(w_bf["w_in"])
    b_exp = jnp.repeat(g["b_f"], HEAD_DIM, axis=1)
    buckets = jnp.asarray(_dil_buckets())

    n_dil = len(DILATIONS)
    if ride_gather is None:
        h1, fqkv, gx, *dqkv = _in_proj(x, g["g_mix_pre"], w_in_e)
    else:
        h1, fqkv, gx, *dqkv, gathered = _in_proj(x, g["g_mix_pre"], w_in_e, riding=ride_gather[0])
        w_bf = {**w_bf, **ride_gather[1](gathered)}
    aq, ak, c = _gate_scan(gx, b_exp)
    stats = _fox_block_stats(fqkv, c)
    spans = _fox_spans(x.shape[0])
    span_idx = _fox_span_index(stats, x.shape[0])
    o_fox, lse_fox = lax.switch(span_idx, [functools.partial(_fox_fwd, s) for s in spans], fqkv, aq, ak, stats)
    bias = _dil_bias(g["rel_bias"], buckets)
    branches = [_dil_fwd(dqkv[p], bias, p) for p in range(n_dil)]
    x1, y1, o_dil, *lj = _mix_out(o_fox, [b[0] for b in branches], [b[1] for b in branches], w_bf["w_out"], x, g["g_mix_post"])
    hm, kx, vx = _mem_fwd(mem, g["g_mem"], w_bf["w_xk"], w_bf["w_xv"])
    x2, y2, h2, qx, ox = _xattn_fwd(x1, g["g_xattn_pre"], w_bf["w_xq"], kx, vx, w_bf["w_xo"], g["g_xattn_post"])
    h3, a, u, z = _ffn_up(x2, g["g_ffn_pre"], w_bf["w_gate"], w_bf["w_up"])
    y3, dx3, sq = _ffn_down_loss(z, w_bf["w_down"], x2, g["g_ffn_post"], target)

    grads = {}
    dy3, da, du, grads["g_ffn_post"] = _ffn_bwd_a(dx3, y3, g["g_ffn_post"], w_bf["w_down"].T, a, u)
    dx2, grads["g_ffn_pre"] = _ffn_bwd_b(da, du, w_bf["w_gate"].T, w_bf["w_up"].T, dx3, x2, g["g_ffn_pre"])
    grads["w_down"] = _weight_grad(z, dy3, "dw_down")
    grads["w_gate"] = _weight_grad(h3, da, "dw_gate")
    grads["w_up"] = _weight_grad(h3, du, "dw_up")
    dx1, dy2, dqx, dkx, dvx, grads["g_xattn_post"], grads["g_xattn_pre"] = _xattn_bwd(
        dx2, y2, g["g_xattn_post"], w_bf["w_xo"].T, qx, kx, vx, w_bf["w_xq"].T, x1, g["g_xattn_pre"])
    grads["w_xo"] = _weight_grad(ox, dy2, "dw_xo")
    grads["w_xq"] = _weight_grad(h2, dqx, "dw_xq")
    grads["w_xk"], grads["w_xv"], grads["g_mem"] = _mem_bwd(dkx, dvx, w_bf["w_xk"].T, w_bf["w_xv"].T, hm, mem, g["g_mem"])
    dy1, do_fox, delta_fox, grads["g_mix_post"], do_dil, delta_dil = _mix_out_bwd(
        dx1, y1, g["g_mix_post"], w_bf["w_out"].T, o_fox, o_dil)
    grads["w_out"] = jnp.concatenate([_weight_grad(o_fox, dy1, "dw_out_fox"), _weight_grad(o_dil, dy1, "dw_out_dil")], axis=0)
    aql, ad = _fox_bwd_prep(aq, lse_fox, delta_fox)
    dfq, rs, dfk, dfv, dc = lax.switch(
        span_idx, [functools.partial(_fox_bwd, s) for s in spans], fqkv, do_fox, aql, ak, ad, stats)
    dgx, db = _gate_bwd(rs, dc, gx, b_exp)
    grads["b_f"] = db[:, ::HEAD_DIM]
    riding = None if pack_early is None else pack_early(grads)
    dil = [_dil_bwd(dqkv[p], do_dil[p], lj[p], delta_dil[p], bias, p, riding if p == 0 else None) for p in range(n_dil)]
    landed_early = dil[0][4] if riding is not None else None
    grads["rel_bias"] = _rel_bias_grad(jnp.stack([t[3] for t in dil]), buckets)[:, :N_HEADS]
    grad_x, dproj, grads["g_mix_pre"] = _in_proj_bwd(
        dfq, dfk, dfv, dgx, [t[0] for t in dil], [t[1] for t in dil], [t[2] for t in dil], w_in_e.T, dx1, x, g["g_mix_pre"])
    dw_in_e = _weight_grad(h1, dproj, "dw_in")
    grads["w_in"] = jnp.concatenate(
        [dw_in_e[:, :3 * hw], dw_in_e[:, 3 * hw:4 * hw:HEAD_DIM], dw_in_e[:, 4 * hw:]], axis=1)
    return sq, grad_x, grads, landed_early


MESH = pl.DeviceIdType.MESH


def _my_place():
    return lax.axis_index("x"), lax.axis_index("y"), lax.axis_index("c")


def _gather_copies(x_ref, out_ref, send_sems, recv_sems, local_sem):
    mx, my, mc = _my_place()
    me, sibling = (mx, my, mc), (mx, my, 1 - mc)
    chips = [(1 - mx, my), (mx, 1 - my), (1 - mx, 1 - my)]

    def slot(px, py, pc):
        return out_ref.at[4 * px + 2 * py + pc]

    def copy(k, block, to, src=None):
        return pltpu.make_async_remote_copy(
            src_ref=slot(*block) if src is None else src, dst_ref=slot(*block),
            send_sem=send_sems.at[k], recv_sem=recv_sems.at[k], device_id=to, device_id_type=MESH)

    mine = pltpu.make_async_copy(x_ref, slot(*me), local_sem)
    first = [copy(0, me, sibling, src=x_ref)] + [copy(1 + j, me, (*chip, mc), src=x_ref) for j, chip in enumerate(chips)]
    passed = [copy(4 + j, (*chip, mc), sibling) for j, chip in enumerate(chips)]
    over_ici = [copy(1 + j, (*chip, mc), me) for j, chip in enumerate(chips)]
    from_sibling = [copy(0, sibling, me)] + [copy(4 + j, (*chip, 1 - mc), me) for j, chip in enumerate(chips)]
    return mine, first, passed, over_ici, from_sibling


def _gather_start(*refs):
    mine, first, _, _, _ = _gather_copies(*refs)
    mine.start()
    for cp in first:
        cp.start()


def _gather_forward(*refs):
    _, _, passed, over_ici, _ = _gather_copies(*refs)
    for arrival, forward in zip(over_ici, passed):
        arrival.wait_recv()
        forward.start()


def _gather_finish(*refs):
    mine, first, passed, _, from_sibling = _gather_copies(*refs)
    for arrival in from_sibling:
        arrival.wait_recv()
    for cp in first + passed:
        cp.wait_send()
    mine.wait()


def _all_gather(x, name):
    rows, lanes = x.shape

    def body(x_ref, out_ref, send_sems, recv_sems, local_sem):
        _gather_start(x_ref, out_ref, send_sems, recv_sems, local_sem)
        _gather_forward(x_ref, out_ref, send_sems, recv_sems, local_sem)
        _gather_finish(x_ref, out_ref, send_sems, recv_sems, local_sem)

    return pl.pallas_call(
        body, name=name, out_shape=_sds((N_DEV, rows, lanes), x.dtype),
        in_specs=[pl.BlockSpec(memory_space=pl.ANY)], out_specs=pl.BlockSpec(memory_space=pl.ANY),
        scratch_shapes=[pltpu.SemaphoreType.DMA((N_DEV - 1,)), pltpu.SemaphoreType.DMA((N_DEV - 1,)), pltpu.SemaphoreType.DMA],
    )(x)


def _exchange_copies(g_ref, land_ref, send_sems, recv_sems, local_sem):
    mx, my, mc = _my_place()
    me = 4 * mx + 2 * my + mc
    mine = pltpu.make_async_copy(g_ref.at[me], land_ref.at[me], local_sem)
    sent, arriving = [], []
    for k in (1, 2, 3, 4, 5, 6, 7):
        px = 1 - mx if k & 4 else mx
        py = 1 - my if k & 2 else my
        pc = 1 - mc if k & 1 else mc
        peer = 4 * px + 2 * py + pc
        sent.append(pltpu.make_async_remote_copy(
            src_ref=g_ref.at[peer], dst_ref=land_ref.at[me], send_sem=send_sems.at[k - 1], recv_sem=recv_sems.at[k - 1],
            device_id=(px, py, pc), device_id_type=MESH))
        arriving.append(pltpu.make_async_remote_copy(
            src_ref=g_ref.at[me], dst_ref=land_ref.at[peer], send_sem=send_sems.at[k - 1], recv_sem=recv_sems.at[k - 1],
            device_id=(px, py, pc), device_id_type=MESH))
    return mine, sent, arriving


def _exchange_start(*refs):
    mine, sent, _ = _exchange_copies(*refs)
    mine.start()
    for cp in sent:
        cp.start()


def _exchange_wait(*refs):
    mine, sent, arriving = _exchange_copies(*refs)
    for cp in arriving:
        cp.wait_recv()
    for cp in sent:
        cp.wait_send()
    mine.wait()


EXCHANGE_SEMAPHORES = (pltpu.SemaphoreType.DMA((N_DEV - 1,)), pltpu.SemaphoreType.DMA((N_DEV - 1,)), pltpu.SemaphoreType.DMA)


def _exchange(g, name):
    def body(g_ref, land_ref, send_sems, recv_sems, local_sem):
        _exchange_start(g_ref, land_ref, send_sems, recv_sems, local_sem)
        _exchange_wait(g_ref, land_ref, send_sems, recv_sems, local_sem)

    return pl.pallas_call(
        body, name=name, out_shape=_sds(g.shape, g.dtype),
        in_specs=[pl.BlockSpec(memory_space=pl.ANY)], out_specs=pl.BlockSpec(memory_space=pl.ANY),
        scratch_shapes=list(EXCHANGE_SEMAPHORES),
    )(g)


def _sum_slots(parts, name):
    n, rows, lanes = parts.shape
    tr = 512 if rows % 512 == 0 else rows

    def body(p_ref, o_ref):
        acc = p_ref[0].astype(F32)
        for j in range(1, n):
            acc = acc + p_ref[j].astype(F32)
        o_ref[...] = acc

    return _call(
        body, name=name, grid=(rows // tr,),
        in_specs=[pl.BlockSpec((n, tr, lanes), lambda i: (0, i, 0))], out_specs=_rows(tr, lanes),
        out_shape=_sds((rows, lanes), F32),
    )(parts)


def _adamw(w, g, m, v, name):
    def body(w_ref, g_ref, m_ref, v_ref, d_ref, nm_ref, nv_ref):
        gv = g_ref[...]
        m_new = ADAM_B1 * m_ref[...] + (1.0 - ADAM_B1) * gv
        v_new = ADAM_B2 * v_ref[...] + (1.0 - ADAM_B2) * (gv * gv)
        nm_ref[...] = m_new
        nv_ref[...] = v_new
        m_hat = m_new / (1.0 - ADAM_B1 ** ADAM_STEP)
        v_hat = v_new / (1.0 - ADAM_B2 ** ADAM_STEP)
        d_ref[...] = -ADAM_LR * (m_hat / (jnp.sqrt(v_hat) + ADAM_EPS) + ADAM_WD * w_ref[...])

    out = _sds(w.shape, F32)
    return pl.pallas_call(
        body, name=name, out_shape=[out, out, out],
        compiler_params=pltpu.CompilerParams(vmem_limit_bytes=V7X_VMEM_LIMIT_BYTES),
    )(w, g, m, v)


def _loss_head(sq, d_model):
    def body(sq_ref, o_ref):
        tot = jnp.sum(jnp.sum(sq_ref[...], axis=1, keepdims=True), axis=0, keepdims=True)
        o_ref[...] = 0.5 * (tot / d_model)

    return pl.pallas_call(body, name="loss_head", out_shape=_sds((1, 1), F32))(sq)


_BIG = (("w_in", 1), ("w_out", 0), ("w_xq", 0), ("w_xk", 0), ("w_xv", 0), ("w_xo", 1), ("w_gate", 1), ("w_up", 1), ("w_down", 0))
_EARLY = ("w_xq", "w_xk", "w_xv", "w_xo", "w_gate", "w_up", "w_down")
_LATE = ("w_in", "w_out")
_SMALL = ("g_mix_pre", "b_f", "rel_bias", "g_mix_post", "g_xattn_pre", "g_mem", "g_xattn_post", "g_ffn_pre", "g_ffn_post")
LANES = 128
BF16_ROW_TILE = 16
BIG_ROW_ALIGN = 512


def _round_up(n, k):
    return -(-n // k) * k


def _row_starts(sizes):
    starts, rows = [], 0
    for n in sizes:
        starts.append(rows)
        rows += _round_up(n, LANES) // LANES
    return starts


def _pack_rows(flat_parts, row_align, dtype):
    lead = flat_parts[0].shape[:-1]
    starts, rows, padded = [], 0, []
    for p in flat_parts:
        n = _round_up(p.shape[-1], LANES)
        starts.append(rows)
        rows += n // LANES
        padded.append(jnp.pad(p.astype(dtype), [(0, 0)] * len(lead) + [(0, n - p.shape[-1])]))
    total = _round_up(rows, row_align)
    padded.append(jnp.zeros(lead + ((total - rows) * LANES,), dtype))
    return jnp.concatenate(padded, axis=-1).reshape(lead + (total, LANES)), starts


def _unpack_rows(buf, starts, shapes):
    lead = buf.shape[:-2]
    flat = buf.reshape(lead + (-1,))
    out = []
    for st, shp in zip(starts, shapes):
        n = int(np.prod(shp))
        out.append(flat[..., st * LANES:st * LANES + n].reshape(lead + tuple(shp)))
    return out


def kernel(x, mem, g_mix_pre, w_in, b_f, rel_bias, w_out, g_mix_post, g_xattn_pre, g_mem, w_xq, w_xk, w_xv, w_xo, g_xattn_post, g_ffn_pre, w_gate, w_up, w_down, g_ffn_post, loss_target, m_g_mix_pre, m_w_in, m_b_f, m_rel_bias, m_w_out, m_g_mix_post, m_g_xattn_pre, m_g_mem, m_w_xq, m_w_xk, m_w_xv, m_w_xo, m_g_xattn_post, m_g_ffn_pre, m_w_gate, m_w_up, m_w_down, m_g_ffn_post, v_g_mix_pre, v_w_in, v_b_f, v_rel_bias, v_w_out, v_g_mix_post, v_g_xattn_pre, v_g_mem, v_w_xq, v_w_xk, v_w_xv, v_w_xo, v_g_xattn_post, v_g_ffn_pre, v_w_gate, v_w_up, v_w_down, v_g_ffn_post):
    given = dict(locals())
    order = ("g_mix_pre", "w_in", "b_f", "rel_bias", "w_out", "g_mix_post", "g_xattn_pre", "g_mem", "w_xq", "w_xk", "w_xv",
             "w_xo", "g_xattn_post", "g_ffn_pre", "w_gate", "w_up", "w_down", "g_ffn_post")
    two_d = lambda a: a.reshape(a.shape[-2:])
    w_loc = {n: two_d(given[n]) for n in order}
    m_loc = {n: two_d(given["m_" + n]) for n in order}
    v_loc = {n: two_d(given["v_" + n]) for n in order}
    d_model = x.shape[-1]

    shard_shapes = [w_loc[n].shape for n, _ in _BIG]
    axis_of = dict(_BIG)

    def gathered_weights(gathered, names, row_starts):
        out = {}
        for n, part in zip(names, _unpack_rows(gathered, row_starts, [w_loc[n].shape for n in names])):
            r, c = part.shape[1:]
            out[n] = part.reshape(N_DEV * r, c) if axis_of[n] == 0 else part.transpose(1, 0, 2).reshape(r, N_DEV * c)
        return out

    first_packed, first_starts = _pack_rows([w_loc["w_in"].reshape(-1)], BF16_ROW_TILE, BF16)
    w_bf = gathered_weights(_all_gather(first_packed, "gather_w_in"), ["w_in"], first_starts)
    later = [n for n, _ in _BIG if n != "w_in"]
    later_packed, later_starts = _pack_rows([w_loc[n].reshape(-1) for n in later], BF16_ROW_TILE, BF16)
    ride_gather = (later_packed, lambda gathered: gathered_weights(gathered, later, later_starts))

    shape_of = dict(zip([n for n, _ in _BIG], shard_shapes))

    def owner_slots(grads, names):
        per_owner = []
        for n in names:
            r, c = shape_of[n]
            gfull = grads[n]
            per_owner.append(gfull.reshape(N_DEV, r * c) if axis_of[n] == 0
                             else gfull.reshape(r, N_DEV, c).transpose(1, 0, 2).reshape(N_DEV, r * c))
        return _pack_rows(per_owner, BIG_ROW_ALIGN, BF16)

    small = {n: w_loc[n] for n in _SMALL}
    sq, grad_x, grads, landed_early = _local_step(
        two_d(x), two_d(mem), two_d(loss_target), small, w_bf, pack_early=lambda gr: owner_slots(gr, _EARLY)[0],
        ride_gather=ride_gather)
    late_slots, late_starts = owner_slots(grads, _LATE)
    landed_late = _exchange(late_slots, "exchange_grads")
    early_starts = _row_starts([int(np.prod(shape_of[n])) for n in _EARLY])
    g_big = {}
    for names, landed, row_starts, label in ((_EARLY, landed_early, early_starts, "sum_grads_early"),
                                             (_LATE, landed_late, late_starts, "sum_grads_late")):
        g_big.update(zip(names, _unpack_rows(_sum_slots(landed, label), row_starts, [shape_of[n] for n in names])))

    small_parts = [grads[n].reshape(-1) for n in _SMALL] + [sq.reshape(-1)]
    small_shapes = [w_loc[n].shape for n in _SMALL] + [sq.shape]
    spacked, sstarts = _pack_rows(small_parts, 8, F32)
    ssum = _sum_slots(_all_gather(spacked, "gather_small"), "sum_small")
    g_small = dict(zip(_SMALL, _unpack_rows(ssum, sstarts, small_shapes)[:-1]))
    sq_rows = sq.size // LANES
    loss = _loss_head(ssum[sstarts[-1]:sstarts[-1] + sq_rows], d_model).reshape(())

    g_loc, delta, new_m, new_v = {}, {}, {}, {}
    for n, _ in _BIG:
        g_loc[n] = g_big[n]
        delta[n], new_m[n], new_v[n] = _adamw(w_loc[n], g_big[n], m_loc[n], v_loc[n], "adamw_" + n)
    pk = lambda d: _pack_rows([d[n].reshape(-1) for n in _SMALL], 8, F32)[0]
    pstarts = _pack_rows([w_loc[n].reshape(-1) for n in _SMALL], 8, F32)[1]
    d_s, m_s, v_s = _adamw(pk(w_loc), pk(g_small), pk(m_loc), pk(v_loc), "adamw_small")
    shapes_s = [w_loc[n].shape for n in _SMALL]
    for n, dd, mm, vv in zip(_SMALL, _unpack_rows(d_s, pstarts, shapes_s), _unpack_rows(m_s, pstarts, shapes_s),
                             _unpack_rows(v_s, pstarts, shapes_s)):
        g_loc[n], delta[n], new_m[n], new_v[n] = g_small[n], dd, mm, vv

    like = lambda d: [d[n].reshape(given[n].shape) for n in order]
    return (loss, grad_x.reshape(x.shape), *like(g_loc), *like(delta), *like(new_m), *like(new_v))
```

```python
import functools

import numpy as np
import jax
import jax.numpy as jnp
from jax import lax
from jax.experimental import pallas as pl
from jax.experimental.pallas import tpu as pltpu

F32 = jnp.float32
BF16 = jnp.bfloat16

RMS_EPS = 1e-6
HEAD_DIM = 64
N_HEADS = 8
HEAD_WIDTH = N_HEADS * HEAD_DIM
PAIR = 2 * HEAD_DIM
N_PAIRS = N_HEADS // 2
DIL_BLOCK = 128
DILATIONS = (1, 4, 16)
N_BUCKETS = 32
MAX_DISTANCE = 2048
N_MEM_HEADS = 4
QK_SCALE = HEAD_DIM ** -0.5
NEG = -1e30
FOX_SKIP_MARGIN = 110.0
FOX_SHORT_SPANS = (4, 8, 16)
N_DEV = 8

ADAM_LR = 0.001
ADAM_B1 = 0.9
ADAM_B2 = 0.999
ADAM_EPS = 1e-08
ADAM_WD = 0.01
ADAM_STEP = 10

V7X_VMEM_LIMIT_BYTES = 56 * 2 ** 20
ROW_TILE = 256
ATT_BLOCK = 512
SCAN_TILE = 256


def _call(body, *, name, grid, in_specs, out_specs, out_shape, scratch=()):
    return pl.pallas_call(
        body, name=name, grid=grid, in_specs=in_specs, out_specs=out_specs, out_shape=out_shape,
        scratch_shapes=list(scratch),
        compiler_params=pltpu.CompilerParams(
            dimension_semantics=("arbitrary",) * len(grid), vmem_limit_bytes=V7X_VMEM_LIMIT_BYTES))


def _rows(tm, n):
    return pl.BlockSpec((tm, n), lambda i: (i, 0))


def _resident(shape):
    zeros = (0,) * len(shape)
    return pl.BlockSpec(shape, lambda i: zeros, pipeline_mode=pl.Buffered(1))


def _acc_out(shape):
    zeros = (0,) * len(shape)
    return pl.BlockSpec(shape, lambda i: zeros)


def _sds(shape, dtype):
    return jax.ShapeDtypeStruct(shape, dtype)


def _dot(a, b):
    return jnp.dot(a, b, preferred_element_type=F32)


def _dot_nt(a, b):
    return lax.dot_general(a, b, (((1,), (1,)), ((), ())), preferred_element_type=F32)


def _dot_tn(a, b):
    return lax.dot_general(a, b, (((0,), (0,)), ((), ())), preferred_element_type=F32)


def _rms_fwd(x, g):
    r = lax.rsqrt(jnp.mean(x * x, axis=-1, keepdims=True) + RMS_EPS)
    return (x * r) * g


def _rms_bwd(xin, g, dy):
    r = lax.rsqrt(jnp.mean(xin * xin, axis=-1, keepdims=True) + RMS_EPS)
    xhat = xin * r
    dg = jnp.sum(dy * xhat, axis=0, keepdims=True)
    dxh = dy * g
    dx = r * (dxh - xhat * jnp.mean(dxh * xhat, axis=-1, keepdims=True))
    return dx, dg


def _first_head_lanes():
    return lax.broadcasted_iota(jnp.int32, (1, PAIR), 1) < HEAD_DIM


def _pick(mask, a, b):
    return jnp.where(mask, a, b)


def _zero_other(mask, v):
    return jnp.where(mask, v, jnp.zeros_like(v))


def _store_lane_blocks(buf_ref, val):
    for cb in range(buf_ref.shape[0]):
        buf_ref[cb] = val[:, cb * PAIR:(cb + 1) * PAIR].astype(F32)


def _load_lane_blocks(buf_ref):
    return jnp.concatenate([buf_ref[cb] for cb in range(buf_ref.shape[0])], axis=1)


def _write_class_major(buf_ref, out_ref, dil):
    n, tile, _ = buf_ref.shape
    for r in range(dil):
        for cb in range(n):
            col = (r * n + cb) * PAIR
            out_ref[:, col:col + PAIR] = buf_ref.at[cb][pl.ds(r, tile // dil, stride=dil), :].astype(out_ref.dtype)


def _read_class_major(in_ref, buf_ref, dil):
    n, tile, _ = buf_ref.shape
    for r in range(dil):
        for cb in range(n):
            col = (r * n + cb) * PAIR
            buf_ref.at[cb][pl.ds(r, tile // dil, stride=dil), :] = in_ref[:, col:col + PAIR].astype(F32)
    return _load_lane_blocks(buf_ref)


def _class_rows(tm, width, dil):
    return _rows(tm // dil, dil * width)


def _in_proj(x, g, w, riding=None):
    s_len, d = x.shape
    tm = ROW_TILE
    hw = HEAD_WIDTH
    n_steps = s_len // tm

    def body(x_ref, g_ref, w_ref, *rest):
        if riding is None:
            h_ref, fqkv_ref, gx_ref, *dil_refs, buf = rest
        else:
            part_ref, h_ref, fqkv_ref, gx_ref, *dil_refs, all_ref, buf, send_sems, recv_sems, local_sem = rest
            gather = (part_ref, all_ref, send_sems, recv_sems, local_sem)

            @pl.when(pl.program_id(0) == 0)
            def _():
                _gather_start(*gather)

            @pl.when(pl.program_id(0) == (3 * n_steps) // 4)
            def _():
                _gather_forward(*gather)

        h = _rms_fwd(x_ref[...], g_ref[...]).astype(BF16)
        h_ref[...] = h
        proj = _dot(h, w_ref[...])
        fqkv_ref[:, 0:hw] = (proj[:, 0:hw] * QK_SCALE).astype(BF16)
        fqkv_ref[:, hw:3 * hw] = proj[:, hw:3 * hw].astype(BF16)
        gx_ref[...] = proj[:, 3 * hw:4 * hw]
        dqkv = jnp.concatenate([proj[:, 4 * hw:5 * hw] * QK_SCALE, proj[:, 5 * hw:7 * hw]], axis=1)
        _store_lane_blocks(buf, dqkv)
        for ref, dil in zip(dil_refs, DILATIONS):
            if dil == 1:
                ref[...] = dqkv.astype(BF16)
            else:
                _write_class_major(buf, ref, dil)

        if riding is not None:
            @pl.when(pl.program_id(0) == n_steps - 1)
            def _():
                _gather_finish(*gather)

    anywhere = pl.BlockSpec(memory_space=pl.ANY)
    extra_in, extra_out, extra_shape, extra_scratch, extra_args = [], [], [], [], []
    if riding is not None:
        extra_in, extra_out = [anywhere], [anywhere]
        extra_shape = [_sds((N_DEV,) + riding.shape, riding.dtype)]
        extra_scratch, extra_args = list(EXCHANGE_SEMAPHORES), [riding]
    return _call(
        body, name="in_proj", grid=(n_steps,),
        in_specs=[_rows(tm, d), _resident((1, d)), _resident(w.shape)] + extra_in,
        out_specs=[_rows(tm, d), _rows(tm, 3 * hw), _rows(tm, hw)] + [_class_rows(tm, 3 * hw, dil) for dil in DILATIONS]
        + extra_out,
        out_shape=[_sds((s_len, d), BF16), _sds((s_len, 3 * hw), BF16), _sds((s_len, hw), F32)]
        + [_sds((s_len // dil, dil * 3 * hw), BF16) for dil in DILATIONS] + extra_shape,
        scratch=[pltpu.VMEM((3 * hw // PAIR, tm, PAIR), F32)] + extra_scratch,
    )(x, g, w, *extra_args)


def _swap_halves(x):
    return jnp.concatenate([pltpu.roll(x[:, i * PAIR:(i + 1) * PAIR], HEAD_DIM, 1) for i in range(x.shape[1] // PAIR)], axis=1)


def _split3(x):
    hi = x.astype(BF16)
    r = x - hi.astype(F32)
    mid = r.astype(BF16)
    lo = (r - mid.astype(F32)).astype(BF16)
    return hi, mid, lo


def _tri_dot(tri, x):
    return sum(_dot(tri, piece) for piece in _split3(x))


def _lane_in_head(width):
    return lax.broadcasted_iota(jnp.int32, (1, width), 1) % HEAD_DIM


def _place3(jj, first, pieces, base):
    out = base
    for i, p in enumerate(pieces):
        out = jnp.where(jj == first + i, p, out)
    return out


def _gate_scan(gx, b_exp):
    s_len, hw = gx.shape
    t = min(SCAN_TILE, s_len)
    tri = jnp.asarray(np.tril(np.ones((t, t), np.float32))).astype(BF16)

    def body(gx_ref, b_ref, tri_ref, aq_ref, ak_ref, c_ref, carry):
        @pl.when(pl.program_id(0) == 0)
        def _():
            carry[...] = jnp.zeros_like(carry)

        z = gx_ref[...] + b_ref[...]
        lf = jnp.minimum(z, 0.0) - jnp.log1p(jnp.exp(-jnp.abs(z)))
        c = _tri_dot(tri_ref[...], lf) + carry[...]
        carry[...] = c[t - 1:t, :]
        c_ref[...] = c
        hi, mid, lo = _split3(_swap_halves(c))
        jj = _lane_in_head(hw)
        zero = jnp.zeros_like(hi)
        one = jnp.ones_like(hi)
        aq_ref[...] = _place3(jj, 0, (hi, mid, lo), jnp.where(jj < 6, one, zero))
        ak_ref[...] = _place3(jj, 3, (-hi, -mid, -lo), jnp.where(jj < 9, one, zero))

    return _call(
        body, name="gate_scan", grid=(s_len // t,),
        in_specs=[_rows(t, hw), _resident((1, hw)), _resident((t, t))],
        out_specs=[_rows(t, hw), _rows(t, hw), _rows(t, hw)],
        out_shape=[_sds((s_len, hw), BF16), _sds((s_len, hw), BF16), _sds((s_len, hw), F32)],
        scratch=[pltpu.VMEM((1, hw), F32)],
    )(gx, b_exp, tri)


def _head_block_ones():
    head_of = np.arange(HEAD_WIDTH) // HEAD_DIM
    return jnp.asarray((head_of[:, None] == head_of[None, :]).astype(np.float32))


def _fox_block_stats(fqkv, c):
    s_len = fqkv.shape[0]
    hw = HEAD_WIDTH
    b = min(ATT_BLOCK, s_len)
    nb = s_len // b

    def body(q_ref, k_ref, c_ref, ones_ref, o_ref):
        q, k, cv = q_ref[...].astype(F32), k_ref[...].astype(F32), c_ref[...]
        seg = lambda x: _dot(x.astype(BF16), ones_ref[...])
        col_max = lambda x: jnp.max(x, axis=0, keepdims=True)
        col_min = lambda x: jnp.min(x, axis=0, keepdims=True)
        qn = 1.01 * jnp.sqrt(col_max(seg(q * q)))
        kn = 1.01 * jnp.sqrt(col_max(seg(k * k)))
        dmin = col_min(seg(q * k)) - (2.0 ** -8) * qn * kn
        o_ref[0] = jnp.concatenate([qn, col_max(cv) - dmin, kn, col_min(cv), jnp.zeros((4, hw), F32)], axis=0)

    stats = _call(
        body, name="fox_block_stats", grid=(nb,),
        in_specs=[pl.BlockSpec((b, hw), lambda i: (i, 0)), pl.BlockSpec((b, hw), lambda i: (i, 1)), _rows(b, hw),
                  _resident((hw, hw))],
        out_specs=pl.BlockSpec((1, 8, hw), lambda i: (i, 0, 0)),
        out_shape=_sds((nb, 8, hw), F32),
    )(fqkv, fqkv, c, _head_block_ones().astype(BF16))
    st = jnp.transpose(stats[:, :4, ::HEAD_DIM], (1, 2, 0))
    bound = st[0][:, :, None] * st[2][:, None, :] + st[1][:, :, None] - st[3][:, None, :]
    need_h = jnp.logical_not(bound < -FOX_SKIP_MARGIN)
    need = jnp.logical_or(need_h[0::2], need_h[1::2])
    ii = lax.broadcasted_iota(jnp.int32, (1, nb, nb), 1)
    jj = lax.broadcasted_iota(jnp.int32, (1, nb, nb), 2)
    first_needed = jnp.min(jnp.where(jnp.logical_or(jnp.logical_and(need, jj < ii), jj == ii), jj, nb), axis=2)
    window = ii[:, :, 0] - first_needed + 1
    in_window = jnp.logical_and(jj >= first_needed[:, :, None], jj <= ii)
    last_query = jnp.max(jnp.where(in_window, ii, 0), axis=1)
    return jnp.concatenate([st.reshape(4 * N_HEADS, nb), window.astype(F32), last_query.astype(F32)], axis=0)


FOX_WINDOW_ROW = 4 * N_HEADS
FOX_LAST_QUERY_ROW = 4 * N_HEADS + N_PAIRS


def _head_needed(st_ref, h, i, j):
    bound = st_ref[h, i] * st_ref[2 * N_HEADS + h, j] + st_ref[N_HEADS + h, i] - st_ref[3 * N_HEADS + h, j]
    return jnp.logical_not(bound < -FOX_SKIP_MARGIN)


def _lower_triangle(n):
    return lax.broadcasted_iota(jnp.int32, (n, n), 1) <= lax.broadcasted_iota(jnp.int32, (n, n), 0)


def _fox_spans(s_len):
    nq = s_len // min(ATT_BLOCK, s_len)
    return [s for s in FOX_SHORT_SPANS if s < nq] + [nq]


def _fox_span_index(stats, s_len):
    longest = jnp.max(stats[FOX_WINDOW_ROW:FOX_WINDOW_ROW + N_PAIRS])
    idx = jnp.int32(0)
    for s in _fox_spans(s_len)[:-1]:
        idx = idx + (longest > s).astype(jnp.int32)
    return idx


def _fox_fwd(span, fqkv, aq, ak, stats):
    s_len = fqkv.shape[0]
    bq = bk = min(ATT_BLOCK, s_len)
    nq = s_len // bq

    def body(st_ref, q_ref, k_ref, v_ref, aq_ref, ak_ref, o_ref, lse_ref, qa_ref, m_ref, acc_ref):
        pair, qi, back = pl.program_id(0), pl.program_id(1), pl.program_id(2)
        kj = jnp.maximum(qi - back, 0)
        in0 = _first_head_lanes()
        not0 = jnp.logical_not(in0)

        @pl.when(back == 0)
        def _():
            q2, a2 = q_ref[...], aq_ref[...]
            qa_ref[0] = jnp.where(in0, q2, a2)
            qa_ref[1] = jnp.where(in0, a2, q2)
            m_ref[...] = jnp.full_like(m_ref, NEG)
            acc_ref[...] = jnp.zeros_like(acc_ref)

        def head_step(a, mine, masked):
            k2, v2 = k_ref[...], v_ref[...]
            s = _dot_nt(qa_ref[a], jnp.where(mine, k2, ak_ref[...]))
            if masked:
                s = jnp.where(_lower_triangle(bq), s, NEG)
            m_old = m_ref[a]
            m_new = jnp.maximum(m_old, jnp.max(s, axis=1, keepdims=True))
            p = jnp.exp(s - jnp.tile(m_new, (1, bk // PAIR))).astype(BF16)
            acc_ref[a] = jnp.exp(m_old - m_new) * acc_ref[a] + _dot(p, jnp.where(mine, v2, jnp.ones_like(v2)))
            m_ref[a] = m_new

        @pl.when(back == 0)
        def _():
            head_step(0, in0, True)
            head_step(1, not0, True)

        below = jnp.logical_and(back > 0, back <= qi)
        for a, mine in ((0, in0), (1, not0)):
            @pl.when(jnp.logical_and(below, _head_needed(st_ref, 2 * pair + a, qi, kj)))
            def _():
                head_step(a, mine, False)

        @pl.when(back == jnp.minimum(qi, span - 1))
        def _():
            acc0, acc1 = acc_ref[0], acc_ref[1]
            l2 = pltpu.roll(_pick(in0, acc1, acc0), HEAD_DIM, 1)
            o_ref[...] = _pick(in0, acc0, acc1) / l2
            lse_ref[...] = _pick(in0, m_ref[0], m_ref[1]) + jnp.log(l2)

    blk = lambda rows: (rows, PAIR)
    qmap = lambda p, i, b, st: (i, p)

    def key_block(p, i, b, st):
        window = jnp.clip(st[FOX_WINDOW_ROW + p, i].astype(jnp.int32), 1, i + 1)
        return i - jnp.minimum(b, window - 1)

    return pl.pallas_call(
        body, name=f"fox_fwd_{span}",
        grid_spec=pltpu.PrefetchScalarGridSpec(
            num_scalar_prefetch=1, grid=(N_PAIRS, nq, span),
            in_specs=[
                pl.BlockSpec(blk(bq), qmap),
                pl.BlockSpec(blk(bk), lambda p, i, b, st: (key_block(p, i, b, st), N_PAIRS + p)),
                pl.BlockSpec(blk(bk), lambda p, i, b, st: (key_block(p, i, b, st), 2 * N_PAIRS + p)),
                pl.BlockSpec(blk(bq), qmap),
                pl.BlockSpec(blk(bk), lambda p, i, b, st: (key_block(p, i, b, st), p)),
            ],
            out_specs=[pl.BlockSpec(blk(bq), qmap), pl.BlockSpec(blk(bq), qmap)],
            scratch_shapes=[pltpu.VMEM((2, bq, PAIR), BF16), pltpu.VMEM((2, bq, PAIR), F32), pltpu.VMEM((2, bq, PAIR), F32)]),
        out_shape=[_sds((s_len, HEAD_WIDTH), F32), _sds((s_len, HEAD_WIDTH), F32)],
        compiler_params=pltpu.CompilerParams(dimension_semantics=("arbitrary",) * 3, vmem_limit_bytes=V7X_VMEM_LIMIT_BYTES),
    )(stats, fqkv, fqkv, fqkv, aq, ak)


def _t5_bucket(dist):
    max_exact = N_BUCKETS // 2
    d = np.maximum(dist, 1).astype(np.float32)
    large = max_exact + (np.log(d / max_exact) / np.log(MAX_DISTANCE / max_exact) * (N_BUCKETS - max_exact)).astype(np.int32)
    large = np.minimum(large, N_BUCKETS - 1)
    return np.where(dist < max_exact, dist, large).astype(np.int32)


def _dil_buckets():
    w = DIL_BLOCK
    qi = np.arange(w)[:, None]
    kj = np.arange(2 * w)[None, :]
    sub = qi + w - kj
    band = (sub >= 0) & (sub <= w)
    out = [np.where(band, _t5_bucket(np.clip(sub, 0, w) * dil), -1) for dil in DILATIONS]
    return np.stack(out).astype(np.int32)


def _dil_bias(rel_bias, buckets):
    w = DIL_BLOCK

    def body(rb_ref, bk_ref, o_ref):
        for p in range(len(DILATIONS)):
            bk = bk_ref[p]
            for h in range(N_HEADS):
                def add(b, acc):
                    return acc + jnp.where(bk == b, rb_ref[b, h], 0.0)
                acc = lax.fori_loop(0, N_BUCKETS, add, jnp.zeros((w, 2 * w), F32))
                o_ref[p, h] = jnp.where(bk < 0, NEG, acc)

    return pl.pallas_call(
        body, name="dil_bias",
        in_specs=[pl.BlockSpec(memory_space=pltpu.SMEM), pl.BlockSpec(memory_space=pltpu.VMEM)],
        out_specs=pl.BlockSpec(memory_space=pltpu.VMEM),
        out_shape=_sds((len(DILATIONS), N_HEADS, w, 2 * w), F32),
    )(rel_bias, buckets)


def _dil_fwd(view, bias, branch):
    dil = DILATIONS[branch]
    w = DIL_BLOCK
    hw = HEAD_WIDTH
    length = view.shape[0]
    nb = length // w

    def body(q_ref, kc_ref, kp_ref, vc_ref, vp_ref, b_ref, o_ref, lse_ref):
        n = pl.program_id(1)
        in0 = _first_head_lanes()
        not0 = jnp.logical_not(in0)
        pairs = [slice(pr * PAIR, (pr + 1) * PAIR) for pr in range(N_PAIRS)]
        tiles = []
        for sl in pairs:
            q2 = q_ref[:, sl]
            qq = jnp.concatenate([_zero_other(in0, q2), _zero_other(not0, q2)], axis=0)
            tiles.append(jnp.concatenate([_dot_nt(qq, kp_ref[:, sl]), _dot_nt(qq, kc_ref[:, sl])], axis=1))
        s = jnp.concatenate(tiles, axis=0) + b_ref[...].reshape(N_HEADS * w, 2 * w)
        prev_half = lax.broadcasted_iota(jnp.int32, (1, 2 * w), 1) < w
        s = jnp.where(jnp.logical_and(n == 0, prev_half), NEG, s)
        m = jnp.max(s, axis=1, keepdims=True)
        e = jnp.exp(s - m)
        l = jnp.sum(e, axis=1, keepdims=True)
        p = (e / l).astype(BF16)
        lse = m + jnp.log(l)
        for pr, sl in enumerate(pairs):
            pp = p[2 * pr * w:(2 * pr + 2) * w]
            o2 = _dot(pp[:, :w], vp_ref[:, sl]) + _dot(pp[:, w:], vc_ref[:, sl])
            o_ref[:, sl] = _pick(in0, o2[:w], o2[w:])
            lse_ref[:, sl] = _pick(in0, lse[2 * pr * w:(2 * pr + 1) * w], lse[(2 * pr + 1) * w:(2 * pr + 2) * w])

    prev = lambda n: jnp.maximum(n - 1, 0)
    out = pl.pallas_call(
        body, name=f"dil_fwd_{dil}", grid=(dil, nb),
        in_specs=[
            pl.BlockSpec((w, hw), lambda r, n: (n, 3 * r)),
            pl.BlockSpec((w, hw), lambda r, n: (n, 3 * r + 1)),
            pl.BlockSpec((w, hw), lambda r, n: (prev(n), 3 * r + 1)),
            pl.BlockSpec((w, hw), lambda r, n: (n, 3 * r + 2)),
            pl.BlockSpec((w, hw), lambda r, n: (prev(n), 3 * r + 2)),
            pl.BlockSpec((None, N_HEADS, w, 2 * w), lambda r, n: (branch, 0, 0, 0)),
        ],
        out_specs=[pl.BlockSpec((w, hw), lambda r, n: (n, r)), pl.BlockSpec((w, hw), lambda r, n: (n, r))],
        out_shape=[_sds((length, dil * hw), F32), _sds((length, dil * hw), F32)],
        compiler_params=pltpu.CompilerParams(dimension_semantics=("arbitrary",) * 2, vmem_limit_bytes=V7X_VMEM_LIMIT_BYTES),
    )(view, view, view, view, view, bias)
    return out[0], out[1]


def _mix_out(o_fox, o_br, lse_br, w_out, x, g_post):
    s_len, d = x.shape
    hw = HEAD_WIDTH
    tm = ROW_TILE

    def body(of_ref, o1, o2, o3, l1, l2, l3, w_ref, x_ref, g_ref, x1_ref, y1_ref, od_ref, lj1, lj2, lj3, buf):
        def natural(ref, dil):
            return ref[...] if dil == 1 else _read_class_major(ref, buf, dil)

        ob = [natural(r, dil) for r, dil in zip((o1, o2, o3), DILATIONS)]
        la, lb, lc = [natural(r, dil) for r, dil in zip((l1, l2, l3), DILATIONS)]
        m = jnp.maximum(jnp.maximum(la, lb), lc)
        ea, eb, ec = jnp.exp(la - m), jnp.exp(lb - m), jnp.exp(lc - m)
        tot = ea + eb + ec
        o_dil = (ea / tot) * ob[0] + (eb / tot) * ob[1] + (ec / tot) * ob[2]
        od_ref[...] = o_dil
        lj = m + jnp.log(tot)
        _store_lane_blocks(buf, lj)
        for ref, dil in zip((lj1, lj2, lj3), DILATIONS):
            if dil == 1:
                ref[...] = lj
            else:
                _write_class_major(buf, ref, dil)
        y = _dot(of_ref[...].astype(BF16), w_ref[0:hw, :]) + _dot(o_dil.astype(BF16), w_ref[hw:2 * hw, :])
        y1_ref[...] = y
        x1_ref[...] = x_ref[...] + _rms_fwd(y, g_ref[...])

    half = _rows(tm, hw)
    by_dil = [_class_rows(tm, hw, dil) for dil in DILATIONS]
    return _call(
        body, name="mix_out", grid=(s_len // tm,),
        in_specs=[half] + by_dil + by_dil + [_resident(w_out.shape), _rows(tm, d), _resident((1, d))],
        out_specs=[_rows(tm, d), _rows(tm, d), half] + by_dil,
        out_shape=[_sds((s_len, d), F32), _sds((s_len, d), F32), _sds((s_len, hw), F32)]
        + [_sds((s_len // dil, dil * hw), F32) for dil in DILATIONS],
        scratch=[pltpu.VMEM((hw // PAIR, tm, PAIR), F32)],
    )(o_fox, *o_br, *lse_br, w_out, x, g_post)


def _mem_fwd(mem, g_mem, w_xk, w_xv):
    n_mem, d = mem.shape
    mw = w_xk.shape[1]

    def body(mem_ref, g_ref, wk_ref, wv_ref, hm_ref, k_ref, v_ref):
        hm = _rms_fwd(mem_ref[...], g_ref[...]).astype(BF16)
        hm_ref[...] = hm
        k_ref[...] = _dot(hm, wk_ref[...]).astype(BF16)
        v_ref[...] = _dot(hm, wv_ref[...]).astype(BF16)

    return pl.pallas_call(
        body, name="mem_fwd",
        out_shape=[_sds((n_mem, d), BF16), _sds((n_mem, mw), BF16), _sds((n_mem, mw), BF16)],
    )(mem, g_mem, w_xk, w_xv)


def _xattn_softmax(qa, k2):
    s = _dot_nt(qa, k2)
    m = jnp.max(s, axis=1, keepdims=True)
    e = jnp.exp(s - m)
    return e / jnp.sum(e, axis=1, keepdims=True)


def _xattn_fwd(x1, g_pre, w_xq, kx, vx, w_xo, g_post):
    s_len, d = x1.shape
    mw = w_xq.shape[1]
    n_mem = kx.shape[0]
    tm = ROW_TILE

    def body(x_ref, gp_ref, wq_ref, k_ref, v_ref, wo_ref, go_ref, x2_ref, y2_ref, h2_ref, q_ref, o_ref):
        x = x_ref[...]
        h = _rms_fwd(x, gp_ref[...]).astype(BF16)
        h2_ref[...] = h
        q = (_dot(h, wq_ref[...]) * QK_SCALE).astype(BF16)
        q_ref[...] = q
        in0 = _first_head_lanes()
        not0 = jnp.logical_not(in0)
        for pr in range(mw // PAIR):
            sl = slice(pr * PAIR, (pr + 1) * PAIR)
            q2, k2, v2 = q[:, sl], k_ref[:, sl], v_ref[:, sl]
            oa = [_dot(_xattn_softmax(_zero_other(mine, q2), k2).astype(BF16), v2) for mine in (in0, not0)]
            o_ref[:, sl] = _pick(in0, oa[0], oa[1]).astype(BF16)
        y = _dot(o_ref[...], wo_ref[...])
        y2_ref[...] = y
        x2_ref[...] = x + _rms_fwd(y, go_ref[...])

    return _call(
        body, name="xattn_fwd", grid=(s_len // tm,),
        in_specs=[_rows(tm, d), _resident((1, d)), _resident(w_xq.shape), _resident((n_mem, mw)), _resident((n_mem, mw)),
                  _resident(w_xo.shape), _resident((1, d))],
        out_specs=[_rows(tm, d), _rows(tm, d), _rows(tm, d), _rows(tm, mw), _rows(tm, mw)],
        out_shape=[_sds((s_len, d), F32), _sds((s_len, d), F32), _sds((s_len, d), BF16), _sds((s_len, mw), BF16),
                   _sds((s_len, mw), BF16)],
    )(x1, g_pre, w_xq, kx, vx, w_xo, g_post)


def _ffn_up(x2, g_pre, w_gate, w_up):
    s_len, d = x2.shape
    dff = w_gate.shape[1]
    tm = ROW_TILE

    def body(x_ref, g_ref, wg_ref, wu_ref, h_ref, a_ref, u_ref, z_ref):
        h = _rms_fwd(x_ref[...], g_ref[...]).astype(BF16)
        h_ref[...] = h
        a = _dot(h, wg_ref[...])
        u = _dot(h, wu_ref[...])
        a_ref[...] = a.astype(BF16)
        u_ref[...] = u.astype(BF16)
        z_ref[...] = ((a * jax.nn.sigmoid(a)) * u).astype(BF16)

    return _call(
        body, name="ffn_up", grid=(s_len // tm,),
        in_specs=[_rows(tm, d), _resident((1, d)), _resident(w_gate.shape), _resident(w_up.shape)],
        out_specs=[_rows(tm, d), _rows(tm, dff), _rows(tm, dff), _rows(tm, dff)],
        out_shape=[_sds((s_len, d), BF16)] + [_sds((s_len, dff), BF16)] * 3,
    )(x2, g_pre, w_gate, w_up)


def _ffn_down_loss(z, w_down, x2, g_post, target):
    s_len, d = x2.shape
    dff = z.shape[1]
    tm = ROW_TILE

    def body(z_ref, w_ref, x_ref, g_ref, t_ref, y_ref, dx_ref, sq_ref):
        @pl.when(pl.program_id(0) == 0)
        def _():
            sq_ref[...] = jnp.zeros_like(sq_ref)

        y = _dot(z_ref[...], w_ref[...])
        y_ref[...] = y
        err = (x_ref[...] + _rms_fwd(y, g_ref[...])) - t_ref[...]
        sq_ref[...] += jnp.sum(err * err, axis=0, keepdims=True)
        dx_ref[...] = err * (1.0 / d)

    return _call(
        body, name="ffn_down_loss", grid=(s_len // tm,),
        in_specs=[_rows(tm, dff), _resident(w_down.shape), _rows(tm, d), _resident((1, d)), _rows(tm, d)],
        out_specs=[_rows(tm, d), _rows(tm, d), _acc_out((1, d))],
        out_shape=[_sds((s_len, d), F32), _sds((s_len, d), F32), _sds((1, d), F32)],
    )(z, w_down, x2, g_post, target)


def _weight_grad(a, b, name):
    s_len, k = a.shape
    n = b.shape[1]
    ts = 512 if s_len % 512 == 0 else s_len
    tn = n
    while k * tn * 4 > 8 * 2 ** 20 and tn % 256 == 0:
        tn //= 2

    def body(a_ref, b_ref, o_ref):
        @pl.when(pl.program_id(1) == 0)
        def _():
            o_ref[...] = jnp.zeros_like(o_ref)

        o_ref[...] += _dot_tn(a_ref[...].astype(BF16), b_ref[...].astype(BF16))

    return pl.pallas_call(
        body, name=name, grid=(n // tn, s_len // ts),
        in_specs=[pl.BlockSpec((ts, k), lambda j, i: (i, 0)), pl.BlockSpec((ts, tn), lambda j, i: (i, j))],
        out_specs=pl.BlockSpec((k, tn), lambda j, i: (0, j)),
        out_shape=_sds((k, n), F32),
        compiler_params=pltpu.CompilerParams(dimension_semantics=("arbitrary",) * 2, vmem_limit_bytes=V7X_VMEM_LIMIT_BYTES),
    )(a, b)


def _ffn_bwd_a(dx3, y3, g_post, w_down, a, u):
    s_len, d = dx3.shape
    dff = a.shape[1]
    tm = ROW_TILE

    def body(dx_ref, y_ref, g_ref, w_ref, a_ref, u_ref, dy_ref, da_ref, du_ref, dg_ref):
        @pl.when(pl.program_id(0) == 0)
        def _():
            dg_ref[...] = jnp.zeros_like(dg_ref)

        dy, dg = _rms_bwd(y_ref[...], g_ref[...], dx_ref[...])
        dg_ref[...] += dg
        dyb = dy.astype(BF16)
        dy_ref[...] = dyb
        dz = _dot_nt(dyb, w_ref[...])
        av = a_ref[...].astype(F32)
        uv = u_ref[...].astype(F32)
        sg = jax.nn.sigmoid(av)
        da_ref[...] = (dz * uv * (sg * (1.0 + av * (1.0 - sg)))).astype(BF16)
        du_ref[...] = (dz * (av * sg)).astype(BF16)

    return _call(
        body, name="ffn_bwd_a", grid=(s_len // tm,),
        in_specs=[_rows(tm, d), _rows(tm, d), _resident((1, d)), _resident(w_down.shape), _rows(tm, dff), _rows(tm, dff)],
        out_specs=[_rows(tm, d), _rows(tm, dff), _rows(tm, dff), _acc_out((1, d))],
        out_shape=[_sds((s_len, d), BF16), _sds((s_len, dff), BF16), _sds((s_len, dff), BF16), _sds((1, d), F32)],
    )(dx3, y3, g_post, w_down, a, u)


def _ffn_bwd_b(da, du, w_gate, w_up, dx3, x2, g_pre):
    s_len, d = x2.shape
    dff = da.shape[1]
    tm = ROW_TILE

    def body(da_ref, du_ref, wg_ref, wu_ref, dx_ref, x_ref, g_ref, o_ref, dg_ref):
        @pl.when(pl.program_id(0) == 0)
        def _():
            dg_ref[...] = jnp.zeros_like(dg_ref)

        dh = _dot_nt(da_ref[...], wg_ref[...]) + _dot_nt(du_ref[...], wu_ref[...])
        dx, dg = _rms_bwd(x_ref[...], g_ref[...], dh)
        dg_ref[...] += dg
        o_ref[...] = dx_ref[...] + dx

    return _call(
        body, name="ffn_bwd_b", grid=(s_len // tm,),
        in_specs=[_rows(tm, dff), _rows(tm, dff), _resident(w_gate.shape), _resident(w_up.shape), _rows(tm, d),
                  _rows(tm, d), _resident((1, d))],
        out_specs=[_rows(tm, d), _acc_out((1, d))],
        out_shape=[_sds((s_len, d), F32), _sds((1, d), F32)],
    )(da, du, w_gate, w_up, dx3, x2, g_pre)


def _xattn_bwd(dx2, y2, g_post, w_xo, q, kx, vx, w_xq, x1, g_pre):
    s_len, d = x1.shape
    mw = q.shape[1]
    n_mem = kx.shape[0]
    tm = ROW_TILE

    def body(dx_ref, y_ref, go_ref, wo_ref, q_ref, k_ref, v_ref, wq_ref, x_ref, gp_ref,
             dx1_ref, dy_ref, dq_ref, dk_ref, dv_ref, dgo_ref, dgp_ref):
        @pl.when(pl.program_id(0) == 0)
        def _():
            dk_ref[...] = jnp.zeros_like(dk_ref)
            dv_ref[...] = jnp.zeros_like(dv_ref)
            dgo_ref[...] = jnp.zeros_like(dgo_ref)
            dgp_ref[...] = jnp.zeros_like(dgp_ref)

        dxin = dx_ref[...]
        dy, dgo = _rms_bwd(y_ref[...], go_ref[...], dxin)
        dgo_ref[...] += dgo
        dyb = dy.astype(BF16)
        dy_ref[...] = dyb
        do = _dot_nt(dyb, wo_ref[...]).astype(BF16)
        in0 = _first_head_lanes()
        not0 = jnp.logical_not(in0)
        for pr in range(mw // PAIR):
            sl = slice(pr * PAIR, (pr + 1) * PAIR)
            q2, k2, v2, do2 = q_ref[:, sl], k_ref[:, sl], v_ref[:, sl], do[:, sl]
            dqs = []
            dk2 = jnp.zeros((n_mem, PAIR), F32)
            dv2 = jnp.zeros((n_mem, PAIR), F32)
            for mine in (in0, not0):
                qa = _zero_other(mine, q2)
                doa = _zero_other(mine, do2)
                p = _xattn_softmax(qa, k2)
                dp = _dot_nt(doa, v2)
                ds = (p * (dp - jnp.sum(p * dp, axis=1, keepdims=True))).astype(BF16)
                dqs.append(_dot(ds, k2))
                dk2 = dk2 + _dot_tn(ds, qa)
                dv2 = dv2 + _dot_tn(p.astype(BF16), doa)
            dq_ref[:, sl] = (_pick(in0, dqs[0], dqs[1]) * QK_SCALE).astype(BF16)
            dk_ref[:, sl] += dk2
            dv_ref[:, sl] += dv2
        dh = _dot_nt(dq_ref[...], wq_ref[...])
        dx, dgp = _rms_bwd(x_ref[...], gp_ref[...], dh)
        dgp_ref[...] += dgp
        dx1_ref[...] = dxin + dx

    return _call(
        body, name="xattn_bwd", grid=(s_len // tm,),
        in_specs=[_rows(tm, d), _rows(tm, d), _resident((1, d)), _resident(w_xo.shape), _rows(tm, mw),
                  _resident((n_mem, mw)), _resident((n_mem, mw)), _resident(w_xq.shape), _rows(tm, d), _resident((1, d))],
        out_specs=[_rows(tm, d), _rows(tm, d), _rows(tm, mw), _acc_out((n_mem, mw)), _acc_out((n_mem, mw)),
                   _acc_out((1, d)), _acc_out((1, d))],
        out_shape=[_sds((s_len, d), F32), _sds((s_len, d), BF16), _sds((s_len, mw), BF16), _sds((n_mem, mw), F32),
                   _sds((n_mem, mw), F32), _sds((1, d), F32), _sds((1, d), F32)],
    )(dx2, y2, g_post, w_xo, q, kx, vx, w_xq, x1, g_pre)


def _mem_bwd(dk, dv, w_xk, w_xv, hm, mem, g_mem):
    n_mem, d = mem.shape
    mw = dk.shape[1]

    def body(dk_ref, dv_ref, wk_ref, wv_ref, hm_ref, mem_ref, g_ref, dwk_ref, dwv_ref, dg_ref):
        dkb = dk_ref[...].astype(BF16)
        dvb = dv_ref[...].astype(BF16)
        dhm = _dot_nt(dkb, wk_ref[...]) + _dot_nt(dvb, wv_ref[...])
        _, dg = _rms_bwd(mem_ref[...], g_ref[...], dhm)
        dg_ref[...] = dg
        dwk_ref[...] = _dot_tn(hm_ref[...], dkb)
        dwv_ref[...] = _dot_tn(hm_ref[...], dvb)

    return pl.pallas_call(
        body, name="mem_bwd",
        out_shape=[_sds((d, mw), F32), _sds((d, mw), F32), _sds((1, d), F32)],
    )(dk, dv, w_xk, w_xv, hm, mem, g_mem)


def _mix_out_bwd(dx1, y1, g_post, w_out, o_fox, o_dil):
    s_len, d = dx1.shape
    hw = HEAD_WIDTH
    tm = ROW_TILE
    ones = _head_block_ones().astype(BF16)

    def body(dx_ref, y_ref, g_ref, w_ref, of_ref, od_ref, ones_ref, dy_ref, dof_ref, dlf_ref, dg_ref,
             dod1, dod2, dod3, dld1, dld2, dld3, buf):
        @pl.when(pl.program_id(0) == 0)
        def _():
            dg_ref[...] = jnp.zeros_like(dg_ref)

        dy, dg = _rms_bwd(y_ref[...], g_ref[...], dx_ref[...])
        dg_ref[...] += dg
        dyb = dy.astype(BF16)
        dy_ref[...] = dyb
        do = _dot_nt(dyb, w_ref[...])
        def head_sums(x):
            hi = x.astype(BF16)
            lo = (x - hi.astype(F32)).astype(BF16)
            return _dot(hi, ones_ref[...]) + _dot(lo, ones_ref[...])

        dof_ref[...] = do[:, 0:hw].astype(BF16)
        dlf_ref[...] = head_sums(do[:, 0:hw] * of_ref[...])
        do_dil = do[:, hw:2 * hw]
        dl_dil = head_sums(do_dil * od_ref[...])
        for val, refs in ((do_dil, (dod1, dod2, dod3)), (dl_dil, (dld1, dld2, dld3))):
            _store_lane_blocks(buf, val)
            for ref, dil in zip(refs, DILATIONS):
                if dil == 1:
                    ref[...] = val.astype(ref.dtype)
                else:
                    _write_class_major(buf, ref, dil)

    half = _rows(tm, hw)
    by_dil = [_class_rows(tm, hw, dil) for dil in DILATIONS]
    outs = _call(
        body, name="mix_out_bwd", grid=(s_len // tm,),
        in_specs=[_rows(tm, d), _rows(tm, d), _resident((1, d)), _resident(w_out.shape), half, half, _resident((hw, hw))],
        out_specs=[_rows(tm, d), half, half, _acc_out((1, d))] + by_dil + by_dil,
        out_shape=[_sds((s_len, d), BF16), _sds((s_len, hw), BF16), _sds((s_len, hw), F32), _sds((1, d), F32)]
        + [_sds((s_len // dil, dil * hw), BF16) for dil in DILATIONS]
        + [_sds((s_len // dil, dil * hw), F32) for dil in DILATIONS],
        scratch=[pltpu.VMEM((hw // PAIR, tm, PAIR), F32)],
    )(dx1, y1, g_post, w_out, o_fox, o_dil, ones)
    return outs[0], outs[1], outs[2], outs[3], outs[4:7], outs[7:10]


def _fox_bwd_prep(aq, lse, delta):
    s_len, hw = aq.shape
    tm = ROW_TILE

    def body(aq_ref, lse_ref, dl_ref, aql_ref, ad_ref):
        jj = _lane_in_head(hw)
        l3 = _split3(_swap_halves(lse_ref[...]))
        aql_ref[...] = _place3(jj, 6, [-p for p in l3], aq_ref[...])
        d3 = _split3(_swap_halves(dl_ref[...]))
        ad_ref[...] = _place3(jj, 0, [-p for p in d3], jnp.zeros((tm, hw), BF16))

    half = _rows(tm, hw)
    return _call(
        body, name="fox_bwd_prep", grid=(s_len // tm,),
        in_specs=[half, half, half], out_specs=[half, half],
        out_shape=[_sds((s_len, hw), BF16), _sds((s_len, hw), BF16)],
    )(aq, lse, delta)


def _ones_on_first3(shape):
    jj = lax.broadcasted_iota(jnp.int32, shape, 1) % HEAD_DIM
    return jnp.where(jj < 3, 1.0, 0.0).astype(BF16)


def _fox_bwd(span, fqkv, do, aql, ak, ad, stats):
    s_len = fqkv.shape[0]
    bq = bk = min(ATT_BLOCK, s_len)
    nq = nk = s_len // bq

    def body(st_ref, q_ref, k_ref, v_ref, do_ref, aql_ref, ak_ref, ad_ref, dq_ref, rs_ref, dk_ref, dv_ref, dc_ref,
             ka_ref, va_ref, kone_ref, r_ref, dvacc_ref, dqacc_ref):
        pair, kj, ahead = pl.program_id(0), pl.program_id(1), pl.program_id(2)
        valid = kj + ahead < nq
        qi = jnp.minimum(kj + ahead, nq - 1)
        in0 = _first_head_lanes()
        not0 = jnp.logical_not(in0)
        heads = ((0, in0), (1, not0))
        rows = pl.ds(pl.multiple_of(qi * bq, bq), bq)

        @pl.when(ahead == 0)
        def _():
            k2, v2, a2 = k_ref[...], v_ref[...], ak_ref[...]
            one = jnp.ones_like(k2)
            one3 = _ones_on_first3(v2.shape)
            for a, mine in heads:
                ka_ref[a] = jnp.where(mine, k2, a2)
                va_ref[a] = jnp.where(mine, v2, one3)
                kone_ref[a] = jnp.where(mine, k2, one)
            r_ref[...] = jnp.zeros_like(r_ref)
            dvacc_ref[...] = jnp.zeros_like(dvacc_ref)

        @pl.when(jnp.logical_and(valid, jnp.logical_or(kj == 0, ahead == span - 1)))
        def _():
            for a, _ in heads:
                dqacc_ref[a, rows, :] = jnp.zeros((bq, PAIR), F32)

        def head_step(a, mine, masked):
            q2 = q_ref[...]
            doa = jnp.where(mine, do_ref[...], ad_ref[...])
            s = _dot_nt(jnp.where(mine, q2, aql_ref[...]), ka_ref[a])
            if masked:
                s = jnp.where(_lower_triangle(bq), s, NEG)
            p = jnp.exp(s)
            ds = (p * _dot_nt(doa, va_ref[a])).astype(BF16)
            dvacc_ref[a] += _dot_tn(p.astype(BF16), doa)
            r_ref[a] += _dot_tn(ds, jnp.where(mine, q2, jnp.ones_like(q2)))
            dqacc_ref[a, rows, :] += _dot(ds, kone_ref[a])

        for a, mine in heads:
            @pl.when(jnp.logical_and(jnp.logical_and(valid, ahead > 0), _head_needed(st_ref, 2 * pair + a, qi, kj)))
            def _():
                head_step(a, mine, False)

        @pl.when(ahead == 0)
        def _():
            for a, mine in heads:
                head_step(a, mine, True)
            acc0, acc1 = dqacc_ref[0, rows, :], dqacc_ref[1, rows, :]
            dq_ref[...] = (_pick(in0, acc0, acc1) * QK_SCALE).astype(BF16)
            rs_ref[...] = pltpu.roll(_pick(in0, acc1, acc0), HEAD_DIM, 1)

        @pl.when(ahead == span - 1)
        def _():
            dk_ref[...] = _pick(in0, r_ref[0], r_ref[1]).astype(BF16)
            dv_ref[...] = _pick(in0, dvacc_ref[0], dvacc_ref[1]).astype(BF16)
            dc_ref[...] = -pltpu.roll(_pick(in0, r_ref[1], r_ref[0]), HEAD_DIM, 1)

    blk = lambda n: (n, PAIR)
    kvmap = lambda p, j, t, st: (j, p)

    def qmap(p, j, t, st):
        last = jnp.clip(st[FOX_LAST_QUERY_ROW + p, j].astype(jnp.int32), j, nq - 1)
        return (jnp.minimum(j + t, last), p)

    return pl.pallas_call(
        body, name=f"fox_bwd_{span}",
        grid_spec=pltpu.PrefetchScalarGridSpec(
            num_scalar_prefetch=1, grid=(N_PAIRS, nk, span),
            in_specs=[
                pl.BlockSpec(blk(bq), qmap),
                pl.BlockSpec(blk(bk), lambda p, j, t, st: (j, N_PAIRS + p)),
                pl.BlockSpec(blk(bk), lambda p, j, t, st: (j, 2 * N_PAIRS + p)),
                pl.BlockSpec(blk(bq), qmap), pl.BlockSpec(blk(bq), qmap),
                pl.BlockSpec(blk(bk), kvmap),
                pl.BlockSpec(blk(bq), qmap),
            ],
            out_specs=[pl.BlockSpec(blk(bk), kvmap)] * 5,
            scratch_shapes=[pltpu.VMEM((2, bk, PAIR), BF16), pltpu.VMEM((2, bk, PAIR), BF16), pltpu.VMEM((2, bk, PAIR), BF16),
                            pltpu.VMEM((2, bk, PAIR), F32), pltpu.VMEM((2, bk, PAIR), F32), pltpu.VMEM((2, s_len, PAIR), F32)]),
        out_shape=[_sds((s_len, HEAD_WIDTH), BF16), _sds((s_len, HEAD_WIDTH), F32), _sds((s_len, HEAD_WIDTH), BF16),
                   _sds((s_len, HEAD_WIDTH), BF16), _sds((s_len, HEAD_WIDTH), F32)],
        compiler_params=pltpu.CompilerParams(dimension_semantics=("arbitrary",) * 3, vmem_limit_bytes=V7X_VMEM_LIMIT_BYTES),
    )(stats, fqkv, fqkv, fqkv, do, aql, ak, ad)


def _gate_bwd(rs, dc, gx, b_exp):
    s_len, hw = gx.shape
    t = min(SCAN_TILE, s_len)
    nt = s_len // t
    tri = jnp.asarray(np.triu(np.ones((t, t), np.float32))).astype(BF16)

    def body(rs_ref, dc_ref, gx_ref, b_ref, tri_ref, dgx_ref, db_ref, carry):
        @pl.when(pl.program_id(0) == 0)
        def _():
            carry[...] = jnp.zeros_like(carry)
            db_ref[...] = jnp.zeros_like(db_ref)

        dlf = _tri_dot(tri_ref[...], rs_ref[...] + dc_ref[...]) + carry[...]
        carry[...] = dlf[0:1, :]
        dgate = dlf * jax.nn.sigmoid(-(gx_ref[...] + b_ref[...]))
        db_ref[...] += jnp.sum(dgate, axis=0, keepdims=True)
        lane = lax.broadcasted_iota(jnp.int32, (1, hw), 1)
        dgx_ref[...] = jnp.where(lane % HEAD_DIM == 0, dgate, 0.0).astype(BF16)

    rev = lambda i: (nt - 1 - i, 0)
    return _call(
        body, name="gate_bwd", grid=(nt,),
        in_specs=[pl.BlockSpec((t, hw), rev)] * 3 + [_resident((1, hw)), _resident((t, t))],
        out_specs=[pl.BlockSpec((t, hw), rev), _acc_out((1, hw))],
        out_shape=[_sds((s_len, hw), BF16), _sds((1, hw), F32)],
        scratch=[pltpu.VMEM((1, hw), F32)],
    )(rs, dc, gx, b_exp, tri)


def _dil_bwd(qkv_v, do_v, lj_v, dl_v, bias, branch, riding=None):
    dil = DILATIONS[branch]
    w = DIL_BLOCK
    hw = HEAD_WIDTH
    length = qkv_v.shape[0]
    nb = length // w

    def body(q0_ref, q1_ref, kp_ref, kc_ref, vp_ref, vc_ref, do0_ref, do1_ref, l0_ref, l1_ref, d0_ref, d1_ref, b_ref, *rest):
        if riding is None:
            dq_ref, dk_ref, dv_ref, dsum_ref = rest
        else:
            g_ref, dq_ref, dk_ref, dv_ref, dsum_ref, land_ref, send_sems, recv_sems, local_sem = rest
            exchange = (g_ref, land_ref, send_sems, recv_sems, local_sem)
        r, n = pl.program_id(0), pl.program_id(1)
        in0 = _first_head_lanes()
        not0 = jnp.logical_not(in0)
        first = n == 0
        last = n == nb - 1

        @pl.when(jnp.logical_and(r == 0, n == 0))
        def _():
            dsum_ref[...] = jnp.zeros_like(dsum_ref)
            if riding is not None:
                _exchange_start(*exchange)

        pairs = [slice(pr * PAIR, (pr + 1) * PAIR) for pr in range(N_PAIRS)]

        def both_heads(ref, sl):
            v = ref[:, sl]
            return jnp.concatenate([_zero_other(in0, v), _zero_other(not0, v)], axis=0)

        def head_columns(ref):
            return jnp.concatenate([ref[:, h * HEAD_DIM:h * HEAD_DIM + 1] for h in range(N_HEADS)], axis=0)

        qq0 = [both_heads(q0_ref, sl) for sl in pairs]
        qq1 = [both_heads(q1_ref, sl) for sl in pairs]
        dd0 = [both_heads(do0_ref, sl) for sl in pairs]
        dd1 = [both_heads(do1_ref, sl) for sl in pairs]
        stack = lambda tiles: jnp.concatenate(tiles, axis=0)
        s_a = stack([_dot_nt(qq0[i], kp_ref[:, sl]) for i, sl in enumerate(pairs)])
        s_b = stack([_dot_nt(qq0[i], kc_ref[:, sl]) for i, sl in enumerate(pairs)])
        s_c = stack([_dot_nt(qq1[i], kc_ref[:, sl]) for i, sl in enumerate(pairs)])
        dp_a = stack([_dot_nt(dd0[i], vp_ref[:, sl]) for i, sl in enumerate(pairs)])
        dp_b = stack([_dot_nt(dd0[i], vc_ref[:, sl]) for i, sl in enumerate(pairs)])
        dp_c = stack([_dot_nt(dd1[i], vc_ref[:, sl]) for i, sl in enumerate(pairs)])
        bias2 = b_ref[...].reshape(N_HEADS * w, 2 * w)
        b_prev, b_cur = bias2[:, 0:w], bias2[:, w:2 * w]
        lse0, lse1 = head_columns(l0_ref), head_columns(l1_ref)
        dl0, dl1 = head_columns(d0_ref), head_columns(d1_ref)
        p_a = jnp.exp(jnp.where(first, NEG, s_a + b_prev) - lse0)
        p_b = jnp.exp((s_b + b_cur) - lse0)
        p_c = jnp.exp(jnp.where(last, NEG, s_c + b_prev) - lse1)
        ds_a = p_a * (dp_a - dl0)
        ds_b = p_b * (dp_b - dl0)
        ds_c = p_c * (dp_c - dl1)
        dsum_ref[...] += jnp.concatenate([ds_a, ds_b], axis=1).reshape(N_HEADS, w, 2 * w)
        ds_a, ds_b, ds_c = ds_a.astype(BF16), ds_b.astype(BF16), ds_c.astype(BF16)
        p_b, p_c = p_b.astype(BF16), p_c.astype(BF16)
        for i, sl in enumerate(pairs):
            rows = slice(2 * i * w, (2 * i + 2) * w)
            dq2 = _dot(ds_a[rows], kp_ref[:, sl]) + _dot(ds_b[rows], kc_ref[:, sl])
            dq_ref[:, sl] = _pick(in0, dq2[:w], dq2[w:]).astype(BF16)
            dk_ref[:, sl] = (_dot_tn(ds_b[rows], qq0[i]) + _dot_tn(ds_c[rows], qq1[i])).astype(BF16)
            dv_ref[:, sl] = (_dot_tn(p_b[rows], dd0[i]) + _dot_tn(p_c[rows], dd1[i])).astype(BF16)

        if riding is not None:
            @pl.when(jnp.logical_and(r == dil - 1, n == nb - 1))
            def _():
                _exchange_wait(*exchange)

    prev = lambda n: jnp.maximum(n - 1, 0)
    nxt = lambda n: jnp.minimum(n + 1, nb - 1)
    blk = (w, hw)
    anywhere = pl.BlockSpec(memory_space=pl.ANY)
    extra_in, extra_out, extra_shape, extra_scratch, extra_args = [], [], [], [], []
    if riding is not None:
        extra_in, extra_out, extra_shape = [anywhere], [anywhere], [_sds(riding.shape, riding.dtype)]
        extra_scratch, extra_args = list(EXCHANGE_SEMAPHORES), [riding]
    outs = pl.pallas_call(
        body, name=f"dil_bwd_{dil}", grid=(dil, nb),
        in_specs=[
            pl.BlockSpec(blk, lambda r, n: (n, 3 * r)),
            pl.BlockSpec(blk, lambda r, n: (nxt(n), 3 * r)),
            pl.BlockSpec(blk, lambda r, n: (prev(n), 3 * r + 1)),
            pl.BlockSpec(blk, lambda r, n: (n, 3 * r + 1)),
            pl.BlockSpec(blk, lambda r, n: (prev(n), 3 * r + 2)),
            pl.BlockSpec(blk, lambda r, n: (n, 3 * r + 2)),
            pl.BlockSpec(blk, lambda r, n: (n, r)),
            pl.BlockSpec(blk, lambda r, n: (nxt(n), r)),
            pl.BlockSpec(blk, lambda r, n: (n, r)),
            pl.BlockSpec(blk, lambda r, n: (nxt(n), r)),
            pl.BlockSpec(blk, lambda r, n: (n, r)),
            pl.BlockSpec(blk, lambda r, n: (nxt(n), r)),
            pl.BlockSpec((None, N_HEADS, w, 2 * w), lambda r, n: (branch, 0, 0, 0)),
        ] + extra_in,
        out_specs=[pl.BlockSpec(blk, lambda r, n: (n, r))] * 3 + [pl.BlockSpec((N_HEADS, w, 2 * w), lambda r, n: (0, 0, 0))]
        + extra_out,
        out_shape=[_sds((length, dil * hw), BF16)] * 3 + [_sds((N_HEADS, w, 2 * w), F32)] + extra_shape,
        scratch_shapes=extra_scratch,
        compiler_params=pltpu.CompilerParams(dimension_semantics=("arbitrary",) * 2, vmem_limit_bytes=V7X_VMEM_LIMIT_BYTES),
    )(qkv_v, qkv_v, qkv_v, qkv_v, qkv_v, qkv_v, do_v, do_v, lj_v, lj_v, dl_v, dl_v, bias, *extra_args)
    return list(outs)


def _rel_bias_grad(dsum, buckets):
    w = DIL_BLOCK

    def body(ds_ref, bk_ref, o_ref):
        row = lax.broadcasted_iota(jnp.int32, (N_BUCKETS, PAIR), 0)
        lane = lax.broadcasted_iota(jnp.int32, (N_BUCKETS, PAIR), 1)

        def per_bucket(b, acc):
            for p in range(len(DILATIONS)):
                hit = bk_ref[p] == b
                for h in range(N_HEADS):
                    part = jnp.where(hit, ds_ref[p, h], 0.0)
                    tot = jnp.sum(jnp.sum(part, axis=1, keepdims=True), axis=0, keepdims=True)
                    acc = acc + jnp.where(jnp.logical_and(row == b, lane == h), tot, 0.0)
            return acc

        o_ref[...] = lax.fori_loop(0, N_BUCKETS, per_bucket, jnp.zeros((N_BUCKETS, PAIR), F32))

    return pl.pallas_call(body, name="rel_bias_grad", out_shape=_sds((N_BUCKETS, PAIR), F32))(dsum, buckets)


def _in_proj_bwd(dfq, dfk, dfv, dgx, ddq, ddk, ddv, w_in, dx1, x, g_pre):
    s_len, d = x.shape
    hw = HEAD_WIDTH
    tm = ROW_TILE

    def body(fq, fk, fv, gx, q1, q2, q3, k1, k2, k3, v1, v2, v3, w_ref, dx_ref, x_ref, g_ref, o_ref, dp_ref, dg_ref, buf):
        @pl.when(pl.program_id(0) == 0)
        def _():
            dg_ref[...] = jnp.zeros_like(dg_ref)

        def branch_sum(refs):
            a, b, c = [r[...].astype(F32) if dil == 1 else _read_class_major(r, buf, dil) for r, dil in zip(refs, DILATIONS)]
            return (a + b) + c

        dp_ref[:, 0:hw] = fq[...]
        dp_ref[:, hw:2 * hw] = fk[...]
        dp_ref[:, 2 * hw:3 * hw] = fv[...]
        dp_ref[:, 3 * hw:4 * hw] = gx[...]
        dp_ref[:, 4 * hw:5 * hw] = (branch_sum((q1, q2, q3)) * QK_SCALE).astype(BF16)
        dp_ref[:, 5 * hw:6 * hw] = branch_sum((k1, k2, k3)).astype(BF16)
        dp_ref[:, 6 * hw:7 * hw] = branch_sum((v1, v2, v3)).astype(BF16)
        dh = _dot_nt(dp_ref[...], w_ref[...])
        dx, dg = _rms_bwd(x_ref[...], g_ref[...], dh)
        dg_ref[...] += dg
        o_ref[...] = dx_ref[...] + dx

    half = _rows(tm, hw)
    by_dil = [_class_rows(tm, hw, dil) for dil in DILATIONS]
    return _call(
        body, name="in_proj_bwd", grid=(s_len // tm,),
        in_specs=[half] * 4 + by_dil * 3 + [_resident(w_in.shape), _rows(tm, d), _rows(tm, d), _resident((1, d))],
        out_specs=[_rows(tm, d), _rows(tm, 7 * hw), _acc_out((1, d))],
        out_shape=[_sds((s_len, d), F32), _sds((s_len, 7 * hw), BF16), _sds((1, d), F32)],
        scratch=[pltpu.VMEM((hw // PAIR, tm, PAIR), F32)],
    )(dfq, dfk, dfv, dgx, *ddq, *ddk, *ddv, w_in, dx1, x, g_pre)


def _expand_w_in(w_in):
    hw = HEAD_WIDTH
    gate = jnp.repeat(w_in[:, 3 * hw:3 * hw + N_HEADS], HEAD_DIM, axis=1)
    return jnp.concatenate([w_in[:, :3 * hw], gate, w_in[:, 3 * hw + N_HEADS:]], axis=1)


def _local_step(x, mem, target, g, w_bf, pack_early=None, ride_gather=None):
    hw = HEAD_WIDTH
    w_in_e = _expand_w_in(w_bf["w_in"])
    b_exp = jnp.repeat(g["b_f"], HEAD_DIM, axis=1)
    buckets = jnp.asarray(_dil_buckets())

    n_dil = len(DILATIONS)
    if ride_gather is None:
        h1, fqkv, gx, *dqkv = _in_proj(x, g["g_mix_pre"], w_in_e)
    else:
        h1, fqkv, gx, *dqkv, gathered = _in_proj(x, g["g_mix_pre"], w_in_e, riding=ride_gather[0])
        w_bf = {**w_bf, **ride_gather[1](gathered)}
    aq, ak, c = _gate_scan(gx, b_exp)
    stats = _fox_block_stats(fqkv, c)
    spans = _fox_spans(x.shape[0])
    span_idx = _fox_span_index(stats, x.shape[0])
    o_fox, lse_fox = lax.switch(span_idx, [functools.partial(_fox_fwd, s) for s in spans], fqkv, aq, ak, stats)
    bias = _dil_bias(g["rel_bias"], buckets)
    branches = [_dil_fwd(dqkv[p], bias, p) for p in range(n_dil)]
    x1, y1, o_dil, *lj = _mix_out(o_fox, [b[0] for b in branches], [b[1] for b in branches], w_bf["w_out"], x, g["g_mix_post"])
    hm, kx, vx = _mem_fwd(mem, g["g_mem"], w_bf["w_xk"], w_bf["w_xv"])
    x2, y2, h2, qx, ox = _xattn_fwd(x1, g["g_xattn_pre"], w_bf["w_xq"], kx, vx, w_bf["w_xo"], g["g_xattn_post"])
    h3, a, u, z = _ffn_up(x2, g["g_ffn_pre"], w_bf["w_gate"], w_bf["w_up"])
    y3, dx3, sq = _ffn_down_loss(z, w_bf["w_down"], x2, g["g_ffn_post"], target)

    grads = {}
    dy3, da, du, grads["g_ffn_post"] = _ffn_bwd_a(dx3, y3, g["g_ffn_post"], w_bf["w_down"], a, u)
    dx2, grads["g_ffn_pre"] = _ffn_bwd_b(da, du, w_bf["w_gate"], w_bf["w_up"], dx3, x2, g["g_ffn_pre"])
    grads["w_down"] = _weight_grad(z, dy3, "dw_down")
    grads["w_gate"] = _weight_grad(h3, da, "dw_gate")
    grads["w_up"] = _weight_grad(h3, du, "dw_up")
    dx1, dy2, dqx, dkx, dvx, grads["g_xattn_post"], grads["g_xattn_pre"] = _xattn_bwd(
        dx2, y2, g["g_xattn_post"], w_bf["w_xo"], qx, kx, vx, w_bf["w_xq"], x1, g["g_xattn_pre"])
    grads["w_xo"] = _weight_grad(ox, dy2, "dw_xo")
    grads["w_xq"] = _weight_grad(h2, dqx, "dw_xq")
    grads["w_xk"], grads["w_xv"], grads["g_mem"] = _mem_bwd(dkx, dvx, w_bf["w_xk"], w_bf["w_xv"], hm, mem, g["g_mem"])
    dy1, do_fox, delta_fox, grads["g_mix_post"], do_dil, delta_dil = _mix_out_bwd(
        dx1, y1, g["g_mix_post"], w_bf["w_out"], o_fox, o_dil)
    grads["w_out"] = jnp.concatenate([_weight_grad(o_fox, dy1, "dw_out_fox"), _weight_grad(o_dil, dy1, "dw_out_dil")], axis=0)
    aql, ad = _fox_bwd_prep(aq, lse_fox, delta_fox)
    dfq, rs, dfk, dfv, dc = lax.switch(
        span_idx, [functools.partial(_fox_bwd, s) for s in spans], fqkv, do_fox, aql, ak, ad, stats)
    dgx, db = _gate_bwd(rs, dc, gx, b_exp)
    grads["b_f"] = db[:, ::HEAD_DIM]
    riding = None if pack_early is None else pack_early(grads)
    dil = [_dil_bwd(dqkv[p], do_dil[p], lj[p], delta_dil[p], bias, p, riding if p == 0 else None) for p in range(n_dil)]
    landed_early = dil[0][4] if riding is not None else None
    grads["rel_bias"] = _rel_bias_grad(jnp.stack([t[3] for t in dil]), buckets)[:, :N_HEADS]
    grad_x, dproj, grads["g_mix_pre"] = _in_proj_bwd(
        dfq, dfk, dfv, dgx, [t[0] for t in dil], [t[1] for t in dil], [t[2] for t in dil], w_in_e, dx1, x, g["g_mix_pre"])
    dw_in_e = _weight_grad(h1, dproj, "dw_in")
    grads["w_in"] = jnp.concatenate(
        [dw_in_e[:, :3 * hw], dw_in_e[:, 3 * hw:4 * hw:HEAD_DIM], dw_in_e[:, 4 * hw:]], axis=1)
    return sq, grad_x, grads, landed_early


MESH = pl.DeviceIdType.MESH


def _my_place():
    return lax.axis_index("x"), lax.axis_index("y"), lax.axis_index("c")


def _gather_copies(x_ref, out_ref, send_sems, recv_sems, local_sem):
    mx, my, mc = _my_place()
    me, sibling = (mx, my, mc), (mx, my, 1 - mc)
    chips = [(1 - mx, my), (mx, 1 - my), (1 - mx, 1 - my)]

    def slot(px, py, pc):
        return out_ref.at[4 * px + 2 * py + pc]

    def copy(k, block, to, src=None):
        return pltpu.make_async_remote_copy(
            src_ref=slot(*block) if src is None else src, dst_ref=slot(*block),
            send_sem=send_sems.at[k], recv_sem=recv_sems.at[k], device_id=to, device_id_type=MESH)

    mine = pltpu.make_async_copy(x_ref, slot(*me), local_sem)
    first = [copy(0, me, sibling, src=x_ref)] + [copy(1 + j, me, (*chip, mc), src=x_ref) for j, chip in enumerate(chips)]
    passed = [copy(4 + j, (*chip, mc), sibling) for j, chip in enumerate(chips)]
    over_ici = [copy(1 + j, (*chip, mc), me) for j, chip in enumerate(chips)]
    from_sibling = [copy(0, sibling, me)] + [copy(4 + j, (*chip, 1 - mc), me) for j, chip in enumerate(chips)]
    return mine, first, passed, over_ici, from_sibling


def _gather_start(*refs):
    mine, first, _, _, _ = _gather_copies(*refs)
    mine.start()
    for cp in first:
        cp.start()


def _gather_forward(*refs):
    _, _, passed, over_ici, _ = _gather_copies(*refs)
    for arrival, forward in zip(over_ici, passed):
        arrival.wait_recv()
        forward.start()


def _gather_finish(*refs):
    mine, first, passed, _, from_sibling = _gather_copies(*refs)
    for arrival in from_sibling:
        arrival.wait_recv()
    for cp in first + passed:
        cp.wait_send()
    mine.wait()


def _all_gather(x, name):
    rows, lanes = x.shape

    def body(x_ref, out_ref, send_sems, recv_sems, local_sem):
        _gather_start(x_ref, out_ref, send_sems, recv_sems, local_sem)
        _gather_forward(x_ref, out_ref, send_sems, recv_sems, local_sem)
        _gather_finish(x_ref, out_ref, send_sems, recv_sems, local_sem)

    return pl.pallas_call(
        body, name=name, out_shape=_sds((N_DEV, rows, lanes), x.dtype),
        in_specs=[pl.BlockSpec(memory_space=pl.ANY)], out_specs=pl.BlockSpec(memory_space=pl.ANY),
        scratch_shapes=[pltpu.SemaphoreType.DMA((N_DEV - 1,)), pltpu.SemaphoreType.DMA((N_DEV - 1,)), pltpu.SemaphoreType.DMA],
    )(x)


def _exchange_copies(g_ref, land_ref, send_sems, recv_sems, local_sem):
    mx, my, mc = _my_place()
    me = 4 * mx + 2 * my + mc
    mine = pltpu.make_async_copy(g_ref.at[me], land_ref.at[me], local_sem)
    sent, arriving = [], []
    for k in (1, 2, 3, 4, 5, 6, 7):
        px = 1 - mx if k & 4 else mx
        py = 1 - my if k & 2 else my
        pc = 1 - mc if k & 1 else mc
        peer = 4 * px + 2 * py + pc
        sent.append(pltpu.make_async_remote_copy(
            src_ref=g_ref.at[peer], dst_ref=land_ref.at[me], send_sem=send_sems.at[k - 1], recv_sem=recv_sems.at[k - 1],
            device_id=(px, py, pc), device_id_type=MESH))
        arriving.append(pltpu.make_async_remote_copy(
            src_ref=g_ref.at[me], dst_ref=land_ref.at[peer], send_sem=send_sems.at[k - 1], recv_sem=recv_sems.at[k - 1],
            device_id=(px, py, pc), device_id_type=MESH))
    return mine, sent, arriving


def _exchange_start(*refs):
    mine, sent, _ = _exchange_copies(*refs)
    mine.start()
    for cp in sent:
        cp.start()


def _exchange_wait(*refs):
    mine, sent, arriving = _exchange_copies(*refs)
    for cp in arriving:
        cp.wait_recv()
    for cp in sent:
        cp.wait_send()
    mine.wait()


EXCHANGE_SEMAPHORES = (pltpu.SemaphoreType.DMA((N_DEV - 1,)), pltpu.SemaphoreType.DMA((N_DEV - 1,)), pltpu.SemaphoreType.DMA)


def _exchange(g, name):
    def body(g_ref, land_ref, send_sems, recv_sems, local_sem):
        _exchange_start(g_ref, land_ref, send_sems, recv_sems, local_sem)
        _exchange_wait(g_ref, land_ref, send_sems, recv_sems, local_sem)

    return pl.pallas_call(
        body, name=name, out_shape=_sds(g.shape, g.dtype),
        in_specs=[pl.BlockSpec(memory_space=pl.ANY)], out_specs=pl.BlockSpec(memory_space=pl.ANY),
        scratch_shapes=list(EXCHANGE_SEMAPHORES),
    )(g)


def _sum_slots(parts, name):
    n, rows, lanes = parts.shape
    tr = 512 if rows % 512 == 0 else rows

    def body(p_ref, o_ref):
        acc = p_ref[0].astype(F32)
        for j in range(1, n):
            acc = acc + p_ref[j].astype(F32)
        o_ref[...] = acc

    return _call(
        body, name=name, grid=(rows // tr,),
        in_specs=[pl.BlockSpec((n, tr, lanes), lambda i: (0, i, 0))], out_specs=_rows(tr, lanes),
        out_shape=_sds((rows, lanes), F32),
    )(parts)


def _adamw(w, g, m, v, name):
    def body(w_ref, g_ref, m_ref, v_ref, d_ref, nm_ref, nv_ref):
        gv = g_ref[...]
        m_new = ADAM_B1 * m_ref[...] + (1.0 - ADAM_B1) * gv
        v_new = ADAM_B2 * v_ref[...] + (1.0 - ADAM_B2) * (gv * gv)
        nm_ref[...] = m_new
        nv_ref[...] = v_new
        m_hat = m_new / (1.0 - ADAM_B1 ** ADAM_STEP)
        v_hat = v_new / (1.0 - ADAM_B2 ** ADAM_STEP)
        d_ref[...] = -ADAM_LR * (m_hat / (jnp.sqrt(v_hat) + ADAM_EPS) + ADAM_WD * w_ref[...])

    out = _sds(w.shape, F32)
    return pl.pallas_call(
        body, name=name, out_shape=[out, out, out],
        compiler_params=pltpu.CompilerParams(vmem_limit_bytes=V7X_VMEM_LIMIT_BYTES),
    )(w, g, m, v)


def _loss_head(sq, d_model):
    def body(sq_ref, o_ref):
        tot = jnp.sum(jnp.sum(sq_ref[...], axis=1, keepdims=True), axis=0, keepdims=True)
        o_ref[...] = 0.5 * (tot / d_model)

    return pl.pallas_call(body, name="loss_head", out_shape=_sds((1, 1), F32))(sq)


_BIG = (("w_in", 1), ("w_out", 0), ("w_xq", 0), ("w_xk", 0), ("w_xv", 0), ("w_xo", 1), ("w_gate", 1), ("w_up", 1), ("w_down", 0))
_EARLY = ("w_xq", "w_xk", "w_xv", "w_xo", "w_gate", "w_up", "w_down")
_LATE = ("w_in", "w_out")
_SMALL = ("g_mix_pre", "b_f", "rel_bias", "g_mix_post", "g_xattn_pre", "g_mem", "g_xattn_post", "g_ffn_pre", "g_ffn_post")
LANES = 128
BF16_ROW_TILE = 16
BIG_ROW_ALIGN = 512


def _round_up(n, k):
    return -(-n // k) * k


def _row_starts(sizes):
    starts, rows = [], 0
    for n in sizes:
        starts.append(rows)
        rows += _round_up(n, LANES) // LANES
    return starts


def _pack_rows(flat_parts, row_align, dtype):
    lead = flat_parts[0].shape[:-1]
    starts, rows, padded = [], 0, []
    for p in flat_parts:
        n = _round_up(p.shape[-1], LANES)
        starts.append(rows)
        rows += n // LANES
        padded.append(jnp.pad(p.astype(dtype), [(0, 0)] * len(lead) + [(0, n - p.shape[-1])]))
    total = _round_up(rows, row_align)
    padded.append(jnp.zeros(lead + ((total - rows) * LANES,), dtype))
    return jnp.concatenate(padded, axis=-1).reshape(lead + (total, LANES)), starts


def _unpack_rows(buf, starts, shapes):
    lead = buf.shape[:-2]
    flat = buf.reshape(lead + (-1,))
    out = []
    for st, shp in zip(starts, shapes):
        n = int(np.prod(shp))
        out.append(flat[..., st * LANES:st * LANES + n].reshape(lead + tuple(shp)))
    return out


def kernel(x, mem, g_mix_pre, w_in, b_f, rel_bias, w_out, g_mix_post, g_xattn_pre, g_mem, w_xq, w_xk, w_xv, w_xo, g_xattn_post, g_ffn_pre, w_gate, w_up, w_down, g_ffn_post, loss_target, m_g_mix_pre, m_w_in, m_b_f, m_rel_bias, m_w_out, m_g_mix_post, m_g_xattn_pre, m_g_mem, m_w_xq, m_w_xk, m_w_xv, m_w_xo, m_g_xattn_post, m_g_ffn_pre, m_w_gate, m_w_up, m_w_down, m_g_ffn_post, v_g_mix_pre, v_w_in, v_b_f, v_rel_bias, v_w_out, v_g_mix_post, v_g_xattn_pre, v_g_mem, v_w_xq, v_w_xk, v_w_xv, v_w_xo, v_g_xattn_post, v_g_ffn_pre, v_w_gate, v_w_up, v_w_down, v_g_ffn_post):
    given = dict(locals())
    order = ("g_mix_pre", "w_in", "b_f", "rel_bias", "w_out", "g_mix_post", "g_xattn_pre", "g_mem", "w_xq", "w_xk", "w_xv",
             "w_xo", "g_xattn_post", "g_ffn_pre", "w_gate", "w_up", "w_down", "g_ffn_post")
    two_d = lambda a: a.reshape(a.shape[-2:])
    w_loc = {n: two_d(given[n]) for n in order}
    m_loc = {n: two_d(given["m_" + n]) for n in order}
    v_loc = {n: two_d(given["v_" + n]) for n in order}
    d_model = x.shape[-1]

    shard_shapes = [w_loc[n].shape for n, _ in _BIG]
    axis_of = dict(_BIG)

    def gathered_weights(gathered, names, row_starts):
        out = {}
        for n, part in zip(names, _unpack_rows(gathered, row_starts, [w_loc[n].shape for n in names])):
            r, c = part.shape[1:]
            out[n] = part.reshape(N_DEV * r, c) if axis_of[n] == 0 else part.transpose(1, 0, 2).reshape(r, N_DEV * c)
        return out

    first_packed, first_starts = _pack_rows([w_loc["w_in"].reshape(-1)], BF16_ROW_TILE, BF16)
    w_bf = gathered_weights(_all_gather(first_packed, "gather_w_in"), ["w_in"], first_starts)
    later = [n for n, _ in _BIG if n != "w_in"]
    later_packed, later_starts = _pack_rows([w_loc[n].reshape(-1) for n in later], BF16_ROW_TILE, BF16)
    ride_gather = (later_packed, lambda gathered: gathered_weights(gathered, later, later_starts))

    shape_of = dict(zip([n for n, _ in _BIG], shard_shapes))

    def owner_slots(grads, names):
        per_owner = []
        for n in names:
            r, c = shape_of[n]
            gfull = grads[n]
            per_owner.append(gfull.reshape(N_DEV, r * c) if axis_of[n] == 0
                             else gfull.reshape(r, N_DEV, c).transpose(1, 0, 2).reshape(N_DEV, r * c))
        return _pack_rows(per_owner, BIG_ROW_ALIGN, BF16)

    small = {n: w_loc[n] for n in _SMALL}
    sq, grad_x, grads, landed_early = _local_step(
        two_d(x), two_d(mem), two_d(loss_target), small, w_bf, pack_early=lambda gr: owner_slots(gr, _EARLY)[0],
        ride_gather=ride_gather)
    late_slots, late_starts = owner_slots(grads, _LATE)
    landed_late = _exchange(late_slots, "exchange_grads")
    early_starts = _row_starts([int(np.prod(shape_of[n])) for n in _EARLY])
    g_big = {}
    for names, landed, row_starts, label in ((_EARLY, landed_early, early_starts, "sum_grads_early"),
                                             (_LATE, landed_late, late_starts, "sum_grads_late")):
        g_big.update(zip(names, _unpack_rows(_sum_slots(landed, label), row_starts, [shape_of[n] for n in names])))

    small_parts = [grads[n].reshape(-1) for n in _SMALL] + [sq.reshape(-1)]
    small_shapes = [w_loc[n].shape for n in _SMALL] + [sq.shape]
    spacked, sstarts = _pack_rows(small_parts, 8, F32)
    ssum = _sum_slots(_all_gather(spacked, "gather_small"), "sum_small")
    g_small = dict(zip(_SMALL, _unpack_rows(ssum, sstarts, small_shapes)[:-1]))
    sq_rows = sq.size // LANES
    loss = _loss_head(ssum[sstarts[-1]:sstarts[-1] + sq_rows], d_model).reshape(())

    g_loc, delta, new_m, new_v = {}, {}, {}, {}
    for n, _ in _BIG:
        g_loc[n] = g_big[n]
        delta[n], new_m[n], new_v[n] = _adamw(w_loc[n], g_big[n], m_loc[n], v_loc[n], "adamw_" + n)
    pk = lambda d: _pack_rows([d[n].reshape(-1) for n in _SMALL], 8, F32)[0]
    pstarts = _pack_rows([w_loc[n].reshape(-1) for n in _SMALL], 8, F32)[1]
    d_s, m_s, v_s = _adamw(pk(w_loc), pk(g_small), pk(m_loc), pk(v_loc), "adamw_small")
    shapes_s = [w_loc[n].shape for n in _SMALL]
    for n, dd, mm, vv in zip(_SMALL, _unpack_rows(d_s, pstarts, shapes_s), _unpack_rows(m_s, pstarts, shapes_s),
                             _unpack_rows(v_s, pstarts, shapes_s)):
        g_loc[n], delta[n], new_m[n], new_v[n] = g_small[n], dd, mm, vv

    like = lambda d: [d[n].reshape(given[n].shape) for n in order]
    return (loss, grad_x.reshape(x.shape), *like(g_loc), *like(delta), *like(new_m), *like(new_v))
```

```python
import functools

import numpy as np
import jax
import jax.numpy as jnp
from jax import lax
from jax.experimental import pallas as pl
from jax.experimental.pallas import tpu as pltpu

F32 = jnp.float32
BF16 = jnp.bfloat16

RMS_EPS = 1e-6
HEAD_DIM = 64
N_HEADS = 8
HEAD_WIDTH = N_HEADS * HEAD_DIM
PAIR = 2 * HEAD_DIM
N_PAIRS = N_HEADS // 2
DIL_BLOCK = 128
DILATIONS = (1, 4, 16)
N_BUCKETS = 32
MAX_DISTANCE = 2048
N_MEM_HEADS = 4
QK_SCALE = HEAD_DIM ** -0.5
NEG = -1e30
FOX_SKIP_MARGIN = 110.0
FOX_SHORT_SPANS = (4, 6, 8, 12, 16)
N_DEV = 8

ADAM_LR = 0.001
ADAM_B1 = 0.9
ADAM_B2 = 0.999
ADAM_EPS = 1e-08
ADAM_WD = 0.01
ADAM_STEP = 10

V7X_VMEM_LIMIT_BYTES = 56 * 2 ** 20
ROW_TILE = 256
ATT_BLOCK = 512
SCAN_TILE = 256


def _call(body, *, name, grid, in_specs, out_specs, out_shape, scratch=()):
    return pl.pallas_call(
        body, name=name, grid=grid, in_specs=in_specs, out_specs=out_specs, out_shape=out_shape,
        scratch_shapes=list(scratch),
        compiler_params=pltpu.CompilerParams(
            dimension_semantics=("arbitrary",) * len(grid), vmem_limit_bytes=V7X_VMEM_LIMIT_BYTES))


def _rows(tm, n):
    return pl.BlockSpec((tm, n), lambda i: (i, 0))


def _resident(shape):
    zeros = (0,) * len(shape)
    return pl.BlockSpec(shape, lambda i: zeros, pipeline_mode=pl.Buffered(1))


def _acc_out(shape):
    zeros = (0,) * len(shape)
    return pl.BlockSpec(shape, lambda i: zeros)


def _sds(shape, dtype):
    return jax.ShapeDtypeStruct(shape, dtype)


def _dot(a, b):
    return jnp.dot(a, b, preferred_element_type=F32)


def _dot_nt(a, b):
    return lax.dot_general(a, b, (((1,), (1,)), ((), ())), preferred_element_type=F32)


def _dot_tn(a, b):
    return lax.dot_general(a, b, (((0,), (0,)), ((), ())), preferred_element_type=F32)


def _rms_fwd(x, g):
    r = lax.rsqrt(jnp.mean(x * x, axis=-1, keepdims=True) + RMS_EPS)
    return (x * r) * g


def _rms_bwd(xin, g, dy):
    r = lax.rsqrt(jnp.mean(xin * xin, axis=-1, keepdims=True) + RMS_EPS)
    xhat = xin * r
    dg = jnp.sum(dy * xhat, axis=0, keepdims=True)
    dxh = dy * g
    dx = r * (dxh - xhat * jnp.mean(dxh * xhat, axis=-1, keepdims=True))
    return dx, dg


def _first_head_lanes():
    return lax.broadcasted_iota(jnp.int32, (1, PAIR), 1) < HEAD_DIM


def _pick(mask, a, b):
    return jnp.where(mask, a, b)


def _zero_other(mask, v):
    return jnp.where(mask, v, jnp.zeros_like(v))


def _store_lane_blocks(buf_ref, val):
    for cb in range(buf_ref.shape[0]):
        buf_ref[cb] = val[:, cb * PAIR:(cb + 1) * PAIR].astype(F32)


def _load_lane_blocks(buf_ref):
    return jnp.concatenate([buf_ref[cb] for cb in range(buf_ref.shape[0])], axis=1)


def _write_class_major(buf_ref, out_ref, dil):
    n, tile, _ = buf_ref.shape
    for r in range(dil):
        for cb in range(n):
            col = (r * n + cb) * PAIR
            out_ref[:, col:col + PAIR] = buf_ref.at[cb][pl.ds(r, tile // dil, stride=dil), :].astype(out_ref.dtype)


def _read_class_major(in_ref, buf_ref, dil):
    n, tile, _ = buf_ref.shape
    for r in range(dil):
        for cb in range(n):
            col = (r * n + cb) * PAIR
            buf_ref.at[cb][pl.ds(r, tile // dil, stride=dil), :] = in_ref[:, col:col + PAIR].astype(F32)
    return _load_lane_blocks(buf_ref)


def _class_rows(tm, width, dil):
    return _rows(tm // dil, dil * width)


def _in_proj(x, g, w, riding=None):
    s_len, d = x.shape
    tm = ROW_TILE
    hw = HEAD_WIDTH
    n_steps = s_len // tm

    def body(x_ref, g_ref, w_ref, *rest):
        if riding is None:
            h_ref, fqkv_ref, gx_ref, *dil_refs, buf = rest
        else:
            part_ref, h_ref, fqkv_ref, gx_ref, *dil_refs, all_ref, buf, send_sems, recv_sems, local_sem = rest
            gather = (part_ref, all_ref, send_sems, recv_sems, local_sem)

            @pl.when(pl.program_id(0) == 0)
            def _():
                _gather_start(*gather)

            @pl.when(pl.program_id(0) == (3 * n_steps) // 4)
            def _():
                _gather_forward(*gather)

        h = _rms_fwd(x_ref[...], g_ref[...]).astype(BF16)
        h_ref[...] = h
        proj = _dot(h, w_ref[...])
        fqkv_ref[:, 0:hw] = (proj[:, 0:hw] * QK_SCALE).astype(BF16)
        fqkv_ref[:, hw:3 * hw] = proj[:, hw:3 * hw].astype(BF16)
        gx_ref[...] = proj[:, 3 * hw:4 * hw]
        dqkv = jnp.concatenate([proj[:, 4 * hw:5 * hw] * QK_SCALE, proj[:, 5 * hw:7 * hw]], axis=1)
        _store_lane_blocks(buf, dqkv)
        for ref, dil in zip(dil_refs, DILATIONS):
            if dil == 1:
                ref[...] = dqkv.astype(BF16)
            else:
                _write_class_major(buf, ref, dil)

        if riding is not None:
            @pl.when(pl.program_id(0) == n_steps - 1)
            def _():
                _gather_finish(*gather)

    anywhere = pl.BlockSpec(memory_space=pl.ANY)
    extra_in, extra_out, extra_shape, extra_scratch, extra_args = [], [], [], [], []
    if riding is not None:
        extra_in, extra_out = [anywhere], [anywhere]
        extra_shape = [_sds((N_DEV,) + riding.shape, riding.dtype)]
        extra_scratch, extra_args = list(EXCHANGE_SEMAPHORES), [riding]
    return _call(
        body, name="in_proj", grid=(n_steps,),
        in_specs=[_rows(tm, d), _resident((1, d)), _resident(w.shape)] + extra_in,
        out_specs=[_rows(tm, d), _rows(tm, 3 * hw), _rows(tm, hw)] + [_class_rows(tm, 3 * hw, dil) for dil in DILATIONS]
        + extra_out,
        out_shape=[_sds((s_len, d), BF16), _sds((s_len, 3 * hw), BF16), _sds((s_len, hw), F32)]
        + [_sds((s_len // dil, dil * 3 * hw), BF16) for dil in DILATIONS] + extra_shape,
        scratch=[pltpu.VMEM((3 * hw // PAIR, tm, PAIR), F32)] + extra_scratch,
    )(x, g, w, *extra_args)


def _swap_halves(x):
    return jnp.concatenate([pltpu.roll(x[:, i * PAIR:(i + 1) * PAIR], HEAD_DIM, 1) for i in range(x.shape[1] // PAIR)], axis=1)


def _split3(x):
    hi = x.astype(BF16)
    r = x - hi.astype(F32)
    mid = r.astype(BF16)
    lo = (r - mid.astype(F32)).astype(BF16)
    return hi, mid, lo


def _tri_dot(tri, x):
    return sum(_dot(tri, piece) for piece in _split3(x))


def _lane_in_head(width):
    return lax.broadcasted_iota(jnp.int32, (1, width), 1) % HEAD_DIM


def _place3(jj, first, pieces, base):
    out = base
    for i, p in enumerate(pieces):
        out = jnp.where(jj == first + i, p, out)
    return out


def _gate_scan(gx, b_exp):
    s_len, hw = gx.shape
    t = min(SCAN_TILE, s_len)
    tri = jnp.asarray(np.tril(np.ones((t, t), np.float32))).astype(BF16)

    def body(gx_ref, b_ref, tri_ref, aq_ref, ak_ref, c_ref, carry):
        @pl.when(pl.program_id(0) == 0)
        def _():
            carry[...] = jnp.zeros_like(carry)

        z = gx_ref[...] + b_ref[...]
        lf = jnp.minimum(z, 0.0) - jnp.log1p(jnp.exp(-jnp.abs(z)))
        c = _tri_dot(tri_ref[...], lf) + carry[...]
        carry[...] = c[t - 1:t, :]
        c_ref[...] = c
        hi, mid, lo = _split3(_swap_halves(c))
        jj = _lane_in_head(hw)
        zero = jnp.zeros_like(hi)
        one = jnp.ones_like(hi)
        aq_ref[...] = _place3(jj, 0, (hi, mid, lo), jnp.where(jj < 6, one, zero))
        ak_ref[...] = _place3(jj, 3, (-hi, -mid, -lo), jnp.where(jj < 9, one, zero))

    return _call(
        body, name="gate_scan", grid=(s_len // t,),
        in_specs=[_rows(t, hw), _resident((1, hw)), _resident((t, t))],
        out_specs=[_rows(t, hw), _rows(t, hw), _rows(t, hw)],
        out_shape=[_sds((s_len, hw), BF16), _sds((s_len, hw), BF16), _sds((s_len, hw), F32)],
        scratch=[pltpu.VMEM((1, hw), F32)],
    )(gx, b_exp, tri)


def _head_block_ones():
    head_of = np.arange(HEAD_WIDTH) // HEAD_DIM
    return jnp.asarray((head_of[:, None] == head_of[None, :]).astype(np.float32))


def _fox_block_stats(fqkv, c):
    s_len = fqkv.shape[0]
    hw = HEAD_WIDTH
    b = min(ATT_BLOCK, s_len)
    nb = s_len // b

    def body(q_ref, k_ref, c_ref, ones_ref, o_ref):
        q, k, cv = q_ref[...].astype(F32), k_ref[...].astype(F32), c_ref[...]
        seg = lambda x: _dot(x.astype(BF16), ones_ref[...])
        col_max = lambda x: jnp.max(x, axis=0, keepdims=True)
        col_min = lambda x: jnp.min(x, axis=0, keepdims=True)
        qn = 1.01 * jnp.sqrt(col_max(seg(q * q)))
        kn = 1.01 * jnp.sqrt(col_max(seg(k * k)))
        dmin = col_min(seg(q * k)) - (2.0 ** -8) * qn * kn
        o_ref[0] = jnp.concatenate([qn, col_max(cv) - dmin, kn, col_min(cv), jnp.zeros((4, hw), F32)], axis=0)

    stats = _call(
        body, name="fox_block_stats", grid=(nb,),
        in_specs=[pl.BlockSpec((b, hw), lambda i: (i, 0)), pl.BlockSpec((b, hw), lambda i: (i, 1)), _rows(b, hw),
                  _resident((hw, hw))],
        out_specs=pl.BlockSpec((1, 8, hw), lambda i: (i, 0, 0)),
        out_shape=_sds((nb, 8, hw), F32),
    )(fqkv, fqkv, c, _head_block_ones().astype(BF16))
    st = jnp.transpose(stats[:, :4, ::HEAD_DIM], (1, 2, 0))
    bound = st[0][:, :, None] * st[2][:, None, :] + st[1][:, :, None] - st[3][:, None, :]
    need_h = jnp.logical_not(bound < -FOX_SKIP_MARGIN)
    need = jnp.logical_or(need_h[0::2], need_h[1::2])
    ii = lax.broadcasted_iota(jnp.int32, (1, nb, nb), 1)
    jj = lax.broadcasted_iota(jnp.int32, (1, nb, nb), 2)
    first_needed = jnp.min(jnp.where(jnp.logical_or(jnp.logical_and(need, jj < ii), jj == ii), jj, nb), axis=2)
    window = ii[:, :, 0] - first_needed + 1
    in_window = jnp.logical_and(jj >= first_needed[:, :, None], jj <= ii)
    last_query = jnp.max(jnp.where(in_window, ii, 0), axis=1)
    return jnp.concatenate([st.reshape(4 * N_HEADS, nb), window.astype(F32), last_query.astype(F32)], axis=0)


FOX_WINDOW_ROW = 4 * N_HEADS
FOX_LAST_QUERY_ROW = 4 * N_HEADS + N_PAIRS


def _head_needed(st_ref, h, i, j):
    bound = st_ref[h, i] * st_ref[2 * N_HEADS + h, j] + st_ref[N_HEADS + h, i] - st_ref[3 * N_HEADS + h, j]
    return jnp.logical_not(bound < -FOX_SKIP_MARGIN)


def _lower_triangle(n):
    return lax.broadcasted_iota(jnp.int32, (n, n), 1) <= lax.broadcasted_iota(jnp.int32, (n, n), 0)


def _fox_spans(s_len):
    nq = s_len // min(ATT_BLOCK, s_len)
    return [s for s in FOX_SHORT_SPANS if s < nq] + [nq]


def _fox_span_index(stats, s_len):
    longest = jnp.max(stats[FOX_WINDOW_ROW:FOX_WINDOW_ROW + N_PAIRS])
    idx = jnp.int32(0)
    for s in _fox_spans(s_len)[:-1]:
        idx = idx + (longest > s).astype(jnp.int32)
    return idx


def _fox_fwd(span, fqkv, aq, ak, stats):
    s_len = fqkv.shape[0]
    bq = bk = min(ATT_BLOCK, s_len)
    nq = s_len // bq

    def body(st_ref, q_ref, k_ref, v_ref, aq_ref, ak_ref, o_ref, lse_ref, qa_ref, m_ref, acc_ref):
        pair, qi, back = pl.program_id(0), pl.program_id(1), pl.program_id(2)
        kj = jnp.maximum(qi - back, 0)
        in0 = _first_head_lanes()
        not0 = jnp.logical_not(in0)

        @pl.when(back == 0)
        def _():
            q2, a2 = q_ref[...], aq_ref[...]
            qa_ref[0] = jnp.where(in0, q2, a2)
            qa_ref[1] = jnp.where(in0, a2, q2)
            m_ref[...] = jnp.full_like(m_ref, NEG)
            acc_ref[...] = jnp.zeros_like(acc_ref)

        def head_step(a, mine, masked):
            k2, v2 = k_ref[...], v_ref[...]
            s = _dot_nt(qa_ref[a], jnp.where(mine, k2, ak_ref[...]))
            if masked:
                s = jnp.where(_lower_triangle(bq), s, NEG)
            m_old = m_ref[a]
            m_new = jnp.maximum(m_old, jnp.max(s, axis=1, keepdims=True))
            p = jnp.exp(s - jnp.tile(m_new, (1, bk // PAIR))).astype(BF16)
            acc_ref[a] = jnp.exp(m_old - m_new) * acc_ref[a] + _dot(p, jnp.where(mine, v2, jnp.ones_like(v2)))
            m_ref[a] = m_new

        @pl.when(back == 0)
        def _():
            head_step(0, in0, True)
            head_step(1, not0, True)

        below = jnp.logical_and(back > 0, back <= qi)
        for a, mine in ((0, in0), (1, not0)):
            @pl.when(jnp.logical_and(below, _head_needed(st_ref, 2 * pair + a, qi, kj)))
            def _():
                head_step(a, mine, False)

        @pl.when(back == jnp.minimum(qi, span - 1))
        def _():
            acc0, acc1 = acc_ref[0], acc_ref[1]
            l2 = pltpu.roll(_pick(in0, acc1, acc0), HEAD_DIM, 1)
            o_ref[...] = _pick(in0, acc0, acc1) / l2
            lse_ref[...] = _pick(in0, m_ref[0], m_ref[1]) + jnp.log(l2)

    blk = lambda rows: (rows, PAIR)
    qmap = lambda p, i, b, st: (i, p)

    def key_block(p, i, b, st):
        window = jnp.clip(st[FOX_WINDOW_ROW + p, i].astype(jnp.int32), 1, i + 1)
        return i - jnp.minimum(b, window - 1)

    return pl.pallas_call(
        body, name=f"fox_fwd_{span}",
        grid_spec=pltpu.PrefetchScalarGridSpec(
            num_scalar_prefetch=1, grid=(N_PAIRS, nq, span),
            in_specs=[
                pl.BlockSpec(blk(bq), qmap),
                pl.BlockSpec(blk(bk), lambda p, i, b, st: (key_block(p, i, b, st), N_PAIRS + p)),
                pl.BlockSpec(blk(bk), lambda p, i, b, st: (key_block(p, i, b, st), 2 * N_PAIRS + p)),
                pl.BlockSpec(blk(bq), qmap),
                pl.BlockSpec(blk(bk), lambda p, i, b, st: (key_block(p, i, b, st), p)),
            ],
            out_specs=[pl.BlockSpec(blk(bq), qmap), pl.BlockSpec(blk(bq), qmap)],
            scratch_shapes=[pltpu.VMEM((2, bq, PAIR), BF16), pltpu.VMEM((2, bq, PAIR), F32), pltpu.VMEM((2, bq, PAIR), F32)]),
        out_shape=[_sds((s_len, HEAD_WIDTH), F32), _sds((s_len, HEAD_WIDTH), F32)],
        compiler_params=pltpu.CompilerParams(dimension_semantics=("arbitrary",) * 3, vmem_limit_bytes=V7X_VMEM_LIMIT_BYTES),
    )(stats, fqkv, fqkv, fqkv, aq, ak)


def _t5_bucket(dist):
    max_exact = N_BUCKETS // 2
    d = np.maximum(dist, 1).astype(np.float32)
    large = max_exact + (np.log(d / max_exact) / np.log(MAX_DISTANCE / max_exact) * (N_BUCKETS - max_exact)).astype(np.int32)
    large = np.minimum(large, N_BUCKETS - 1)
    return np.where(dist < max_exact, dist, large).astype(np.int32)


def _dil_buckets():
    w = DIL_BLOCK
    qi = np.arange(w)[:, None]
    kj = np.arange(2 * w)[None, :]
    sub = qi + w - kj
    band = (sub >= 0) & (sub <= w)
    out = [np.where(band, _t5_bucket(np.clip(sub, 0, w) * dil), -1) for dil in DILATIONS]
    return np.stack(out).astype(np.int32)


def _dil_bias(rel_bias, buckets):
    w = DIL_BLOCK

    def body(rb_ref, bk_ref, o_ref):
        for p in range(len(DILATIONS)):
            bk = bk_ref[p]
            for h in range(N_HEADS):
                def add(b, acc):
                    return acc + jnp.where(bk == b, rb_ref[b, h], 0.0)
                acc = lax.fori_loop(0, N_BUCKETS, add, jnp.zeros((w, 2 * w), F32))
                o_ref[p, h] = jnp.where(bk < 0, NEG, acc)

    return pl.pallas_call(
        body, name="dil_bias",
        in_specs=[pl.BlockSpec(memory_space=pltpu.SMEM), pl.BlockSpec(memory_space=pltpu.VMEM)],
        out_specs=pl.BlockSpec(memory_space=pltpu.VMEM),
        out_shape=_sds((len(DILATIONS), N_HEADS, w, 2 * w), F32),
    )(rel_bias, buckets)


def _dil_fwd(view, bias, branch):
    dil = DILATIONS[branch]
    w = DIL_BLOCK
    hw = HEAD_WIDTH
    length = view.shape[0]
    nb = length // w

    def body(q_ref, kc_ref, kp_ref, vc_ref, vp_ref, b_ref, o_ref, lse_ref):
        n = pl.program_id(1)
        in0 = _first_head_lanes()
        not0 = jnp.logical_not(in0)
        pairs = [slice(pr * PAIR, (pr + 1) * PAIR) for pr in range(N_PAIRS)]
        tiles = []
        for sl in pairs:
            q2 = q_ref[:, sl]
            qq = jnp.concatenate([_zero_other(in0, q2), _zero_other(not0, q2)], axis=0)
            tiles.append(jnp.concatenate([_dot_nt(qq, kp_ref[:, sl]), _dot_nt(qq, kc_ref[:, sl])], axis=1))
        s = jnp.concatenate(tiles, axis=0) + b_ref[...].reshape(N_HEADS * w, 2 * w)
        prev_half = lax.broadcasted_iota(jnp.int32, (1, 2 * w), 1) < w
        s = jnp.where(jnp.logical_and(n == 0, prev_half), NEG, s)
        m = jnp.max(s, axis=1, keepdims=True)
        e = jnp.exp(s - m)
        l = jnp.sum(e, axis=1, keepdims=True)
        p = (e / l).astype(BF16)
        lse = m + jnp.log(l)
        for pr, sl in enumerate(pairs):
            pp = p[2 * pr * w:(2 * pr + 2) * w]
            o2 = _dot(pp[:, :w], vp_ref[:, sl]) + _dot(pp[:, w:], vc_ref[:, sl])
            o_ref[:, sl] = _pick(in0, o2[:w], o2[w:])
            lse_ref[:, sl] = _pick(in0, lse[2 * pr * w:(2 * pr + 1) * w], lse[(2 * pr + 1) * w:(2 * pr + 2) * w])

    prev = lambda n: jnp.maximum(n - 1, 0)
    out = pl.pallas_call(
        body, name=f"dil_fwd_{dil}", grid=(dil, nb),
        in_specs=[
            pl.BlockSpec((w, hw), lambda r, n: (n, 3 * r)),
            pl.BlockSpec((w, hw), lambda r, n: (n, 3 * r + 1)),
            pl.BlockSpec((w, hw), lambda r, n: (prev(n), 3 * r + 1)),
            pl.BlockSpec((w, hw), lambda r, n: (n, 3 * r + 2)),
            pl.BlockSpec((w, hw), lambda r, n: (prev(n), 3 * r + 2)),
            pl.BlockSpec((None, N_HEADS, w, 2 * w), lambda r, n: (branch, 0, 0, 0)),
        ],
        out_specs=[pl.BlockSpec((w, hw), lambda r, n: (n, r)), pl.BlockSpec((w, hw), lambda r, n: (n, r))],
        out_shape=[_sds((length, dil * hw), F32), _sds((length, dil * hw), F32)],
        compiler_params=pltpu.CompilerParams(dimension_semantics=("arbitrary",) * 2, vmem_limit_bytes=V7X_VMEM_LIMIT_BYTES),
    )(view, view, view, view, view, bias)
    return out[0], out[1]


def _mix_out(o_fox, o_br, lse_br, w_out, x, g_post):
    s_len, d = x.shape
    hw = HEAD_WIDTH
    tm = ROW_TILE

    def body(of_ref, o1, o2, o3, l1, l2, l3, w_ref, x_ref, g_ref, x1_ref, y1_ref, od_ref, lj1, lj2, lj3, buf):
        def natural(ref, dil):
            return ref[...] if dil == 1 else _read_class_major(ref, buf, dil)

        ob = [natural(r, dil) for r, dil in zip((o1, o2, o3), DILATIONS)]
        la, lb, lc = [natural(r, dil) for r, dil in zip((l1, l2, l3), DILATIONS)]
        m = jnp.maximum(jnp.maximum(la, lb), lc)
        ea, eb, ec = jnp.exp(la - m), jnp.exp(lb - m), jnp.exp(lc - m)
        tot = ea + eb + ec
        o_dil = (ea / tot) * ob[0] + (eb / tot) * ob[1] + (ec / tot) * ob[2]
        od_ref[...] = o_dil
        lj = m + jnp.log(tot)
        _store_lane_blocks(buf, lj)
        for ref, dil in zip((lj1, lj2, lj3), DILATIONS):
            if dil == 1:
                ref[...] = lj
            else:
                _write_class_major(buf, ref, dil)
        y = _dot(of_ref[...].astype(BF16), w_ref[0:hw, :]) + _dot(o_dil.astype(BF16), w_ref[hw:2 * hw, :])
        y1_ref[...] = y
        x1_ref[...] = x_ref[...] + _rms_fwd(y, g_ref[...])

    half = _rows(tm, hw)
    by_dil = [_class_rows(tm, hw, dil) for dil in DILATIONS]
    return _call(
        body, name="mix_out", grid=(s_len // tm,),
        in_specs=[half] + by_dil + by_dil + [_resident(w_out.shape), _rows(tm, d), _resident((1, d))],
        out_specs=[_rows(tm, d), _rows(tm, d), half] + by_dil,
        out_shape=[_sds((s_len, d), F32), _sds((s_len, d), F32), _sds((s_len, hw), F32)]
        + [_sds((s_len // dil, dil * hw), F32) for dil in DILATIONS],
        scratch=[pltpu.VMEM((hw // PAIR, tm, PAIR), F32)],
    )(o_fox, *o_br, *lse_br, w_out, x, g_post)


def _mem_fwd(mem, g_mem, w_xk, w_xv):
    n_mem, d = mem.shape
    mw = w_xk.shape[1]

    def body(mem_ref, g_ref, wk_ref, wv_ref, hm_ref, k_ref, v_ref):
        hm = _rms_fwd(mem_ref[...], g_ref[...]).astype(BF16)
        hm_ref[...] = hm
        k_ref[...] = _dot(hm, wk_ref[...]).astype(BF16)
        v_ref[...] = _dot(hm, wv_ref[...]).astype(BF16)

    return pl.pallas_call(
        body, name="mem_fwd",
        out_shape=[_sds((n_mem, d), BF16), _sds((n_mem, mw), BF16), _sds((n_mem, mw), BF16)],
    )(mem, g_mem, w_xk, w_xv)


def _xattn_softmax(qa, k2):
    s = _dot_nt(qa, k2)
    m = jnp.max(s, axis=1, keepdims=True)
    e = jnp.exp(s - m)
    return e / jnp.sum(e, axis=1, keepdims=True)


def _xattn_fwd(x1, g_pre, w_xq, kx, vx, w_xo, g_post):
    s_len, d = x1.shape
    mw = w_xq.shape[1]
    n_mem = kx.shape[0]
    tm = ROW_TILE

    def body(x_ref, gp_ref, wq_ref, k_ref, v_ref, wo_ref, go_ref, x2_ref, y2_ref, h2_ref, q_ref, o_ref):
        x = x_ref[...]
        h = _rms_fwd(x, gp_ref[...]).astype(BF16)
        h2_ref[...] = h
        q = (_dot(h, wq_ref[...]) * QK_SCALE).astype(BF16)
        q_ref[...] = q
        in0 = _first_head_lanes()
        not0 = jnp.logical_not(in0)
        for pr in range(mw // PAIR):
            sl = slice(pr * PAIR, (pr + 1) * PAIR)
            q2, k2, v2 = q[:, sl], k_ref[:, sl], v_ref[:, sl]
            oa = [_dot(_xattn_softmax(_zero_other(mine, q2), k2).astype(BF16), v2) for mine in (in0, not0)]
            o_ref[:, sl] = _pick(in0, oa[0], oa[1]).astype(BF16)
        y = _dot(o_ref[...], wo_ref[...])
        y2_ref[...] = y
        x2_ref[...] = x + _rms_fwd(y, go_ref[...])

    return _call(
        body, name="xattn_fwd", grid=(s_len // tm,),
        in_specs=[_rows(tm, d), _resident((1, d)), _resident(w_xq.shape), _resident((n_mem, mw)), _resident((n_mem, mw)),
                  _resident(w_xo.shape), _resident((1, d))],
        out_specs=[_rows(tm, d), _rows(tm, d), _rows(tm, d), _rows(tm, mw), _rows(tm, mw)],
        out_shape=[_sds((s_len, d), F32), _sds((s_len, d), F32), _sds((s_len, d), BF16), _sds((s_len, mw), BF16),
                   _sds((s_len, mw), BF16)],
    )(x1, g_pre, w_xq, kx, vx, w_xo, g_post)


def _ffn_up(x2, g_pre, w_gate, w_up):
    s_len, d = x2.shape
    dff = w_gate.shape[1]
    tm = ROW_TILE

    def body(x_ref, g_ref, wg_ref, wu_ref, h_ref, a_ref, u_ref, z_ref):
        h = _rms_fwd(x_ref[...], g_ref[...]).astype(BF16)
        h_ref[...] = h
        a = _dot(h, wg_ref[...])
        u = _dot(h, wu_ref[...])
        a_ref[...] = a.astype(BF16)
        u_ref[...] = u.astype(BF16)
        z_ref[...] = ((a * jax.nn.sigmoid(a)) * u).astype(BF16)

    return _call(
        body, name="ffn_up", grid=(s_len // tm,),
        in_specs=[_rows(tm, d), _resident((1, d)), _resident(w_gate.shape), _resident(w_up.shape)],
        out_specs=[_rows(tm, d), _rows(tm, dff), _rows(tm, dff), _rows(tm, dff)],
        out_shape=[_sds((s_len, d), BF16)] + [_sds((s_len, dff), BF16)] * 3,
    )(x2, g_pre, w_gate, w_up)


def _ffn_down_loss(z, w_down, x2, g_post, target):
    s_len, d = x2.shape
    dff = z.shape[1]
    tm = ROW_TILE

    def body(z_ref, w_ref, x_ref, g_ref, t_ref, y_ref, dx_ref, sq_ref):
        @pl.when(pl.program_id(0) == 0)
        def _():
            sq_ref[...] = jnp.zeros_like(sq_ref)

        y = _dot(z_ref[...], w_ref[...])
        y_ref[...] = y
        err = (x_ref[...] + _rms_fwd(y, g_ref[...])) - t_ref[...]
        sq_ref[...] += jnp.sum(err * err, axis=0, keepdims=True)
        dx_ref[...] = err * (1.0 / d)

    return _call(
        body, name="ffn_down_loss", grid=(s_len // tm,),
        in_specs=[_rows(tm, dff), _resident(w_down.shape), _rows(tm, d), _resident((1, d)), _rows(tm, d)],
        out_specs=[_rows(tm, d), _rows(tm, d), _acc_out((1, d))],
        out_shape=[_sds((s_len, d), F32), _sds((s_len, d), F32), _sds((1, d), F32)],
    )(z, w_down, x2, g_post, target)


def _weight_grad(a, b, name):
    s_len, k = a.shape
    n = b.shape[1]
    ts = 512 if s_len % 512 == 0 else s_len
    tn = n
    while k * tn * 4 > 8 * 2 ** 20 and tn % 256 == 0:
        tn //= 2

    def body(a_ref, b_ref, o_ref):
        @pl.when(pl.program_id(1) == 0)
        def _():
            o_ref[...] = jnp.zeros_like(o_ref)

        o_ref[...] += _dot_tn(a_ref[...].astype(BF16), b_ref[...].astype(BF16))

    return pl.pallas_call(
        body, name=name, grid=(n // tn, s_len // ts),
        in_specs=[pl.BlockSpec((ts, k), lambda j, i: (i, 0)), pl.BlockSpec((ts, tn), lambda j, i: (i, j))],
        out_specs=pl.BlockSpec((k, tn), lambda j, i: (0, j)),
        out_shape=_sds((k, n), F32),
        compiler_params=pltpu.CompilerParams(dimension_semantics=("arbitrary",) * 2, vmem_limit_bytes=V7X_VMEM_LIMIT_BYTES),
    )(a, b)


def _ffn_bwd_a(dx3, y3, g_post, w_down_t, a, u):
    s_len, d = dx3.shape
    dff = a.shape[1]
    tm = ROW_TILE

    def body(dx_ref, y_ref, g_ref, w_ref, a_ref, u_ref, dy_ref, da_ref, du_ref, dg_ref):
        @pl.when(pl.program_id(0) == 0)
        def _():
            dg_ref[...] = jnp.zeros_like(dg_ref)

        dy, dg = _rms_bwd(y_ref[...], g_ref[...], dx_ref[...])
        dg_ref[...] += dg
        dyb = dy.astype(BF16)
        dy_ref[...] = dyb
        dz = _dot(dyb, w_ref[...])
        av = a_ref[...].astype(F32)
        uv = u_ref[...].astype(F32)
        sg = jax.nn.sigmoid(av)
        da_ref[...] = (dz * uv * (sg * (1.0 + av * (1.0 - sg)))).astype(BF16)
        du_ref[...] = (dz * (av * sg)).astype(BF16)

    return _call(
        body, name="ffn_bwd_a", grid=(s_len // tm,),
        in_specs=[_rows(tm, d), _rows(tm, d), _resident((1, d)), _resident(w_down_t.shape), _rows(tm, dff), _rows(tm, dff)],
        out_specs=[_rows(tm, d), _rows(tm, dff), _rows(tm, dff), _acc_out((1, d))],
        out_shape=[_sds((s_len, d), BF16), _sds((s_len, dff), BF16), _sds((s_len, dff), BF16), _sds((1, d), F32)],
    )(dx3, y3, g_post, w_down_t, a, u)


def _ffn_bwd_b(da, du, w_gate_t, w_up_t, dx3, x2, g_pre):
    s_len, d = x2.shape
    dff = da.shape[1]
    tm = ROW_TILE

    def body(da_ref, du_ref, wg_ref, wu_ref, dx_ref, x_ref, g_ref, o_ref, dg_ref):
        @pl.when(pl.program_id(0) == 0)
        def _():
            dg_ref[...] = jnp.zeros_like(dg_ref)

        dh = _dot(da_ref[...], wg_ref[...]) + _dot(du_ref[...], wu_ref[...])
        dx, dg = _rms_bwd(x_ref[...], g_ref[...], dh)
        dg_ref[...] += dg
        o_ref[...] = dx_ref[...] + dx

    return _call(
        body, name="ffn_bwd_b", grid=(s_len // tm,),
        in_specs=[_rows(tm, dff), _rows(tm, dff), _resident(w_gate_t.shape), _resident(w_up_t.shape), _rows(tm, d),
                  _rows(tm, d), _resident((1, d))],
        out_specs=[_rows(tm, d), _acc_out((1, d))],
        out_shape=[_sds((s_len, d), F32), _sds((1, d), F32)],
    )(da, du, w_gate_t, w_up_t, dx3, x2, g_pre)


def _xattn_bwd(dx2, y2, g_post, w_xo_t, q, kx, vx, w_xq_t, x1, g_pre):
    s_len, d = x1.shape
    mw = q.shape[1]
    n_mem = kx.shape[0]
    tm = ROW_TILE

    def body(dx_ref, y_ref, go_ref, wo_ref, q_ref, k_ref, v_ref, wq_ref, x_ref, gp_ref,
             dx1_ref, dy_ref, dq_ref, dk_ref, dv_ref, dgo_ref, dgp_ref):
        @pl.when(pl.program_id(0) == 0)
        def _():
            dk_ref[...] = jnp.zeros_like(dk_ref)
            dv_ref[...] = jnp.zeros_like(dv_ref)
            dgo_ref[...] = jnp.zeros_like(dgo_ref)
            dgp_ref[...] = jnp.zeros_like(dgp_ref)

        dxin = dx_ref[...]
        dy, dgo = _rms_bwd(y_ref[...], go_ref[...], dxin)
        dgo_ref[...] += dgo
        dyb = dy.astype(BF16)
        dy_ref[...] = dyb
        do = _dot(dyb, wo_ref[...]).astype(BF16)
        in0 = _first_head_lanes()
        not0 = jnp.logical_not(in0)
        for pr in range(mw // PAIR):
            sl = slice(pr * PAIR, (pr + 1) * PAIR)
            q2, k2, v2, do2 = q_ref[:, sl], k_ref[:, sl], v_ref[:, sl], do[:, sl]
            dqs = []
            dk2 = jnp.zeros((n_mem, PAIR), F32)
            dv2 = jnp.zeros((n_mem, PAIR), F32)
            for mine in (in0, not0):
                qa = _zero_other(mine, q2)
                doa = _zero_other(mine, do2)
                p = _xattn_softmax(qa, k2)
                dp = _dot_nt(doa, v2)
                ds = (p * (dp - jnp.sum(p * dp, axis=1, keepdims=True))).astype(BF16)
                dqs.append(_dot(ds, k2))
                dk2 = dk2 + _dot_tn(ds, qa)
                dv2 = dv2 + _dot_tn(p.astype(BF16), doa)
            dq_ref[:, sl] = (_pick(in0, dqs[0], dqs[1]) * QK_SCALE).astype(BF16)
            dk_ref[:, sl] += dk2
            dv_ref[:, sl] += dv2
        dh = _dot(dq_ref[...], wq_ref[...])
        dx, dgp = _rms_bwd(x_ref[...], gp_ref[...], dh)
        dgp_ref[...] += dgp
        dx1_ref[...] = dxin + dx

    return _call(
        body, name="xattn_bwd", grid=(s_len // tm,),
        in_specs=[_rows(tm, d), _rows(tm, d), _resident((1, d)), _resident(w_xo_t.shape), _rows(tm, mw),
                  _resident((n_mem, mw)), _resident((n_mem, mw)), _resident(w_xq_t.shape), _rows(tm, d), _resident((1, d))],
        out_specs=[_rows(tm, d), _rows(tm, d), _rows(tm, mw), _acc_out((n_mem, mw)), _acc_out((n_mem, mw)),
                   _acc_out((1, d)), _acc_out((1, d))],
        out_shape=[_sds((s_len, d), F32), _sds((s_len, d), BF16), _sds((s_len, mw), BF16), _sds((n_mem, mw), F32),
                   _sds((n_mem, mw), F32), _sds((1, d), F32), _sds((1, d), F32)],
    )(dx2, y2, g_post, w_xo_t, q, kx, vx, w_xq_t, x1, g_pre)


def _mem_bwd(dk, dv, w_xk_t, w_xv_t, hm, mem, g_mem):
    n_mem, d = mem.shape
    mw = dk.shape[1]

    def body(dk_ref, dv_ref, wk_ref, wv_ref, hm_ref, mem_ref, g_ref, dwk_ref, dwv_ref, dg_ref):
        dkb = dk_ref[...].astype(BF16)
        dvb = dv_ref[...].astype(BF16)
        dhm = _dot(dkb, wk_ref[...]) + _dot(dvb, wv_ref[...])
        _, dg = _rms_bwd(mem_ref[...], g_ref[...], dhm)
        dg_ref[...] = dg
        dwk_ref[...] = _dot_tn(hm_ref[...], dkb)
        dwv_ref[...] = _dot_tn(hm_ref[...], dvb)

    return pl.pallas_call(
        body, name="mem_bwd",
        out_shape=[_sds((d, mw), F32), _sds((d, mw), F32), _sds((1, d), F32)],
    )(dk, dv, w_xk_t, w_xv_t, hm, mem, g_mem)


def _mix_out_bwd(dx1, y1, g_post, w_out_t, o_fox, o_dil):
    s_len, d = dx1.shape
    hw = HEAD_WIDTH
    tm = ROW_TILE
    ones = _head_block_ones().astype(BF16)

    def body(dx_ref, y_ref, g_ref, w_ref, of_ref, od_ref, ones_ref, dy_ref, dof_ref, dlf_ref, dg_ref,
             dod1, dod2, dod3, dld1, dld2, dld3, buf):
        @pl.when(pl.program_id(0) == 0)
        def _():
            dg_ref[...] = jnp.zeros_like(dg_ref)

        dy, dg = _rms_bwd(y_ref[...], g_ref[...], dx_ref[...])
        dg_ref[...] += dg
        dyb = dy.astype(BF16)
        dy_ref[...] = dyb
        do = _dot(dyb, w_ref[...])
        def head_sums(x):
            hi = x.astype(BF16)
            lo = (x - hi.astype(F32)).astype(BF16)
            return _dot(hi, ones_ref[...]) + _dot(lo, ones_ref[...])

        dof_ref[...] = do[:, 0:hw].astype(BF16)
        dlf_ref[...] = head_sums(do[:, 0:hw] * of_ref[...])
        do_dil = do[:, hw:2 * hw]
        dl_dil = head_sums(do_dil * od_ref[...])
        for val, refs in ((do_dil, (dod1, dod2, dod3)), (dl_dil, (dld1, dld2, dld3))):
            _store_lane_blocks(buf, val)
            for ref, dil in zip(refs, DILATIONS):
                if dil == 1:
                    ref[...] = val.astype(ref.dtype)
                else:
                    _write_class_major(buf, ref, dil)

    half = _rows(tm, hw)
    by_dil = [_class_rows(tm, hw, dil) for dil in DILATIONS]
    outs = _call(
        body, name="mix_out_bwd", grid=(s_len // tm,),
        in_specs=[_rows(tm, d), _rows(tm, d), _resident((1, d)), _resident(w_out_t.shape), half, half, _resident((hw, hw))],
        out_specs=[_rows(tm, d), half, half, _acc_out((1, d))] + by_dil + by_dil,
        out_shape=[_sds((s_len, d), BF16), _sds((s_len, hw), BF16), _sds((s_len, hw), F32), _sds((1, d), F32)]
        + [_sds((s_len // dil, dil * hw), BF16) for dil in DILATIONS]
        + [_sds((s_len // dil, dil * hw), F32) for dil in DILATIONS],
        scratch=[pltpu.VMEM((hw // PAIR, tm, PAIR), F32)],
    )(dx1, y1, g_post, w_out_t, o_fox, o_dil, ones)
    return outs[0], outs[1], outs[2], outs[3], outs[4:7], outs[7:10]


def _fox_bwd_prep(aq, lse, delta):
    s_len, hw = aq.shape
    tm = ROW_TILE

    def body(aq_ref, lse_ref, dl_ref, aql_ref, ad_ref):
        jj = _lane_in_head(hw)
        l3 = _split3(_swap_halves(lse_ref[...]))
        aql_ref[...] = _place3(jj, 6, [-p for p in l3], aq_ref[...])
        d3 = _split3(_swap_halves(dl_ref[...]))
        ad_ref[...] = _place3(jj, 0, [-p for p in d3], jnp.zeros((tm, hw), BF16))

    half = _rows(tm, hw)
    return _call(
        body, name="fox_bwd_prep", grid=(s_len // tm,),
        in_specs=[half, half, half], out_specs=[half, half],
        out_shape=[_sds((s_len, hw), BF16), _sds((s_len, hw), BF16)],
    )(aq, lse, delta)


def _ones_on_first3(shape):
    jj = lax.broadcasted_iota(jnp.int32, shape, 1) % HEAD_DIM
    return jnp.where(jj < 3, 1.0, 0.0).astype(BF16)


def _fox_bwd(span, fqkv, do, aql, ak, ad, stats):
    s_len = fqkv.shape[0]
    bq = bk = min(ATT_BLOCK, s_len)
    nq = nk = s_len // bq

    def body(st_ref, q_ref, k_ref, v_ref, do_ref, aql_ref, ak_ref, ad_ref, dq_ref, rs_ref, dk_ref, dv_ref, dc_ref,
             ka_ref, va_ref, kone_ref, r_ref, dvacc_ref, dqacc_ref):
        pair, kj, ahead = pl.program_id(0), pl.program_id(1), pl.program_id(2)
        valid = kj + ahead < nq
        qi = jnp.minimum(kj + ahead, nq - 1)
        in0 = _first_head_lanes()
        not0 = jnp.logical_not(in0)
        heads = ((0, in0), (1, not0))
        rows = pl.ds(pl.multiple_of(qi * bq, bq), bq)

        @pl.when(ahead == 0)
        def _():
            k2, v2, a2 = k_ref[...], v_ref[...], ak_ref[...]
            one = jnp.ones_like(k2)
            one3 = _ones_on_first3(v2.shape)
            for a, mine in heads:
                ka_ref[a] = jnp.where(mine, k2, a2)
                va_ref[a] = jnp.where(mine, v2, one3)
                kone_ref[a] = jnp.where(mine, k2, one)
            r_ref[...] = jnp.zeros_like(r_ref)
            dvacc_ref[...] = jnp.zeros_like(dvacc_ref)

        @pl.when(jnp.logical_and(valid, jnp.logical_or(kj == 0, ahead == span - 1)))
        def _():
            for a, _ in heads:
                dqacc_ref[a, rows, :] = jnp.zeros((bq, PAIR), F32)

        def head_step(a, mine, masked):
            q2 = q_ref[...]
            doa = jnp.where(mine, do_ref[...], ad_ref[...])
            s = _dot_nt(jnp.where(mine, q2, aql_ref[...]), ka_ref[a])
            if masked:
                s = jnp.where(_lower_triangle(bq), s, NEG)
            p = jnp.exp(s)
            ds = (p * _dot_nt(doa, va_ref[a])).astype(BF16)
            dvacc_ref[a] += _dot_tn(p.astype(BF16), doa)
            r_ref[a] += _dot_tn(ds, jnp.where(mine, q2, jnp.ones_like(q2)))
            dqacc_ref[a, rows, :] += _dot(ds, kone_ref[a])

        for a, mine in heads:
            @pl.when(jnp.logical_and(jnp.logical_and(valid, ahead > 0), _head_needed(st_ref, 2 * pair + a, qi, kj)))
            def _():
                head_step(a, mine, False)

        @pl.when(ahead == 0)
        def _():
            for a, mine in heads:
                head_step(a, mine, True)
            acc0, acc1 = dqacc_ref[0, rows, :], dqacc_ref[1, rows, :]
            dq_ref[...] = (_pick(in0, acc0, acc1) * QK_SCALE).astype(BF16)
            rs_ref[...] = pltpu.roll(_pick(in0, acc1, acc0), HEAD_DIM, 1)

        @pl.when(ahead == span - 1)
        def _():
            dk_ref[...] = _pick(in0, r_ref[0], r_ref[1]).astype(BF16)
            dv_ref[...] = _pick(in0, dvacc_ref[0], dvacc_ref[1]).astype(BF16)
            dc_ref[...] = -pltpu.roll(_pick(in0, r_ref[1], r_ref[0]), HEAD_DIM, 1)

    blk = lambda n: (n, PAIR)
    kvmap = lambda p, j, t, st: (j, p)

    def qmap(p, j, t, st):
        last = jnp.clip(st[FOX_LAST_QUERY_ROW + p, j].astype(jnp.int32), j, nq - 1)
        return (jnp.minimum(j + t, last), p)

    return pl.pallas_call(
        body, name=f"fox_bwd_{span}",
        grid_spec=pltpu.PrefetchScalarGridSpec(
            num_scalar_prefetch=1, grid=(N_PAIRS, nk, span),
            in_specs=[
                pl.BlockSpec(blk(bq), qmap),
                pl.BlockSpec(blk(bk), lambda p, j, t, st: (j, N_PAIRS + p)),
                pl.BlockSpec(blk(bk), lambda p, j, t, st: (j, 2 * N_PAIRS + p)),
                pl.BlockSpec(blk(bq), qmap), pl.BlockSpec(blk(bq), qmap),
                pl.BlockSpec(blk(bk), kvmap),
                pl.BlockSpec(blk(bq), qmap),
            ],
            out_specs=[pl.BlockSpec(blk(bk), kvmap)] * 5,
            scratch_shapes=[pltpu.VMEM((2, bk, PAIR), BF16), pltpu.VMEM((2, bk, PAIR), BF16), pltpu.VMEM((2, bk, PAIR), BF16),
                            pltpu.VMEM((2, bk, PAIR), F32), pltpu.VMEM((2, bk, PAIR), F32), pltpu.VMEM((2, s_len, PAIR), F32)]),
        out_shape=[_sds((s_len, HEAD_WIDTH), BF16), _sds((s_len, HEAD_WIDTH), F32), _sds((s_len, HEAD_WIDTH), BF16),
                   _sds((s_len, HEAD_WIDTH), BF16), _sds((s_len, HEAD_WIDTH), F32)],
        compiler_params=pltpu.CompilerParams(dimension_semantics=("arbitrary",) * 3, vmem_limit_bytes=V7X_VMEM_LIMIT_BYTES),
    )(stats, fqkv, fqkv, fqkv, do, aql, ak, ad)


def _gate_bwd(rs, dc, gx, b_exp):
    s_len, hw = gx.shape
    t = min(SCAN_TILE, s_len)
    nt = s_len // t
    tri = jnp.asarray(np.triu(np.ones((t, t), np.float32))).astype(BF16)

    def body(rs_ref, dc_ref, gx_ref, b_ref, tri_ref, dgx_ref, db_ref, carry):
        @pl.when(pl.program_id(0) == 0)
        def _():
            carry[...] = jnp.zeros_like(carry)
            db_ref[...] = jnp.zeros_like(db_ref)

        dlf = _tri_dot(tri_ref[...], rs_ref[...] + dc_ref[...]) + carry[...]
        carry[...] = dlf[0:1, :]
        dgate = dlf * jax.nn.sigmoid(-(gx_ref[...] + b_ref[...]))
        db_ref[...] += jnp.sum(dgate, axis=0, keepdims=True)
        lane = lax.broadcasted_iota(jnp.int32, (1, hw), 1)
        dgx_ref[...] = jnp.where(lane % HEAD_DIM == 0, dgate, 0.0).astype(BF16)

    rev = lambda i: (nt - 1 - i, 0)
    return _call(
        body, name="gate_bwd", grid=(nt,),
        in_specs=[pl.BlockSpec((t, hw), rev)] * 3 + [_resident((1, hw)), _resident((t, t))],
        out_specs=[pl.BlockSpec((t, hw), rev), _acc_out((1, hw))],
        out_shape=[_sds((s_len, hw), BF16), _sds((1, hw), F32)],
        scratch=[pltpu.VMEM((1, hw), F32)],
    )(rs, dc, gx, b_exp, tri)


def _dil_bwd(qkv_v, do_v, lj_v, dl_v, bias, branch, riding=None):
    dil = DILATIONS[branch]
    w = DIL_BLOCK
    hw = HEAD_WIDTH
    length = qkv_v.shape[0]
    nb = length // w

    def body(q0_ref, q1_ref, kp_ref, kc_ref, vp_ref, vc_ref, do0_ref, do1_ref, l0_ref, l1_ref, d0_ref, d1_ref, b_ref, *rest):
        if riding is None:
            dq_ref, dk_ref, dv_ref, dsum_ref = rest
        else:
            g_ref, dq_ref, dk_ref, dv_ref, dsum_ref, land_ref, send_sems, recv_sems, local_sem = rest
            exchange = (g_ref, land_ref, send_sems, recv_sems, local_sem)
        r, n = pl.program_id(0), pl.program_id(1)
        in0 = _first_head_lanes()
        not0 = jnp.logical_not(in0)
        first = n == 0
        last = n == nb - 1

        @pl.when(jnp.logical_and(r == 0, n == 0))
        def _():
            dsum_ref[...] = jnp.zeros_like(dsum_ref)
            if riding is not None:
                _exchange_start(*exchange)

        pairs = [slice(pr * PAIR, (pr + 1) * PAIR) for pr in range(N_PAIRS)]

        def both_heads(ref, sl):
            v = ref[:, sl]
            return jnp.concatenate([_zero_other(in0, v), _zero_other(not0, v)], axis=0)

        def head_columns(ref):
            return jnp.concatenate([ref[:, h * HEAD_DIM:h * HEAD_DIM + 1] for h in range(N_HEADS)], axis=0)

        qq0 = [both_heads(q0_ref, sl) for sl in pairs]
        qq1 = [both_heads(q1_ref, sl) for sl in pairs]
        dd0 = [both_heads(do0_ref, sl) for sl in pairs]
        dd1 = [both_heads(do1_ref, sl) for sl in pairs]
        stack = lambda tiles: jnp.concatenate(tiles, axis=0)
        s_a = stack([_dot_nt(qq0[i], kp_ref[:, sl]) for i, sl in enumerate(pairs)])
        s_b = stack([_dot_nt(qq0[i], kc_ref[:, sl]) for i, sl in enumerate(pairs)])
        s_c = stack([_dot_nt(qq1[i], kc_ref[:, sl]) for i, sl in enumerate(pairs)])
        dp_a = stack([_dot_nt(dd0[i], vp_ref[:, sl]) for i, sl in enumerate(pairs)])
        dp_b = stack([_dot_nt(dd0[i], vc_ref[:, sl]) for i, sl in enumerate(pairs)])
        dp_c = stack([_dot_nt(dd1[i], vc_ref[:, sl]) for i, sl in enumerate(pairs)])
        bias2 = b_ref[...].reshape(N_HEADS * w, 2 * w)
        b_prev, b_cur = bias2[:, 0:w], bias2[:, w:2 * w]
        lse0, lse1 = head_columns(l0_ref), head_columns(l1_ref)
        dl0, dl1 = head_columns(d0_ref), head_columns(d1_ref)
        p_a = jnp.exp(jnp.where(first, NEG, s_a + b_prev) - lse0)
        p_b = jnp.exp((s_b + b_cur) - lse0)
        p_c = jnp.exp(jnp.where(last, NEG, s_c + b_prev) - lse1)
        ds_a = p_a * (dp_a - dl0)
        ds_b = p_b * (dp_b - dl0)
        ds_c = p_c * (dp_c - dl1)
        dsum_ref[...] += jnp.concatenate([ds_a, ds_b], axis=1).reshape(N_HEADS, w, 2 * w)
        ds_a, ds_b, ds_c = ds_a.astype(BF16), ds_b.astype(BF16), ds_c.astype(BF16)
        p_b, p_c = p_b.astype(BF16), p_c.astype(BF16)
        for i, sl in enumerate(pairs):
            rows = slice(2 * i * w, (2 * i + 2) * w)
            dq2 = _dot(ds_a[rows], kp_ref[:, sl]) + _dot(ds_b[rows], kc_ref[:, sl])
            dq_ref[:, sl] = _pick(in0, dq2[:w], dq2[w:]).astype(BF16)
            dk_ref[:, sl] = (_dot_tn(ds_b[rows], qq0[i]) + _dot_tn(ds_c[rows], qq1[i])).astype(BF16)
            dv_ref[:, sl] = (_dot_tn(p_b[rows], dd0[i]) + _dot_tn(p_c[rows], dd1[i])).astype(BF16)

        if riding is not None:
            @pl.when(jnp.logical_and(r == dil - 1, n == nb - 1))
            def _():
                _exchange_wait(*exchange)

    prev = lambda n: jnp.maximum(n - 1, 0)
    nxt = lambda n: jnp.minimum(n + 1, nb - 1)
    blk = (w, hw)
    anywhere = pl.BlockSpec(memory_space=pl.ANY)
    extra_in, extra_out, extra_shape, extra_scratch, extra_args = [], [], [], [], []
    if riding is not None:
        extra_in, extra_out, extra_shape = [anywhere], [anywhere], [_sds(riding.shape, riding.dtype)]
        extra_scratch, extra_args = list(EXCHANGE_SEMAPHORES), [riding]
    outs = pl.pallas_call(
        body, name=f"dil_bwd_{dil}", grid=(dil, nb),
        in_specs=[
            pl.BlockSpec(blk, lambda r, n: (n, 3 * r)),
            pl.BlockSpec(blk, lambda r, n: (nxt(n), 3 * r)),
            pl.BlockSpec(blk, lambda r, n: (prev(n), 3 * r + 1)),
            pl.BlockSpec(blk, lambda r, n: (n, 3 * r + 1)),
            pl.BlockSpec(blk, lambda r, n: (prev(n), 3 * r + 2)),
            pl.BlockSpec(blk, lambda r, n: (n, 3 * r + 2)),
            pl.BlockSpec(blk, lambda r, n: (n, r)),
            pl.BlockSpec(blk, lambda r, n: (nxt(n), r)),
            pl.BlockSpec(blk, lambda r, n: (n, r)),
            pl.BlockSpec(blk, lambda r, n: (nxt(n), r)),
            pl.BlockSpec(blk, lambda r, n: (n, r)),
            pl.BlockSpec(blk, lambda r, n: (nxt(n), r)),
            pl.BlockSpec((None, N_HEADS, w, 2 * w), lambda r, n: (branch, 0, 0, 0)),
        ] + extra_in,
        out_specs=[pl.BlockSpec(blk, lambda r, n: (n, r))] * 3 + [pl.BlockSpec((N_HEADS, w, 2 * w), lambda r, n: (0, 0, 0))]
        + extra_out,
        out_shape=[_sds((length, dil * hw), BF16)] * 3 + [_sds((N_HEADS, w, 2 * w), F32)] + extra_shape,
        scratch_shapes=extra_scratch,
        compiler_params=pltpu.CompilerParams(dimension_semantics=("arbitrary",) * 2, vmem_limit_bytes=V7X_VMEM_LIMIT_BYTES),
    )(qkv_v, qkv_v, qkv_v, qkv_v, qkv_v, qkv_v, do_v, do_v, lj_v, lj_v, dl_v, dl_v, bias, *extra_args)
    return list(outs)


def _rel_bias_grad(dsum, buckets):
    w = DIL_BLOCK

    def body(ds_ref, bk_ref, o_ref):
        row = lax.broadcasted_iota(jnp.int32, (N_BUCKETS, PAIR), 0)
        lane = lax.broadcasted_iota(jnp.int32, (N_BUCKETS, PAIR), 1)

        def per_bucket(b, acc):
            for p in range(len(DILATIONS)):
                hit = bk_ref[p] == b
                for h in range(N_HEADS):
                    part = jnp.where(hit, ds_ref[p, h], 0.0)
                    tot = jnp.sum(jnp.sum(part, axis=1, keepdims=True), axis=0, keepdims=True)
                    acc = acc + jnp.where(jnp.logical_and(row == b, lane == h), tot, 0.0)
            return acc

        o_ref[...] = lax.fori_loop(0, N_BUCKETS, per_bucket, jnp.zeros((N_BUCKETS, PAIR), F32))

    return pl.pallas_call(body, name="rel_bias_grad", out_shape=_sds((N_BUCKETS, PAIR), F32))(dsum, buckets)


def _in_proj_bwd(dfq, dfk, dfv, dgx, ddq, ddk, ddv, w_in_t, dx1, x, g_pre):
    s_len, d = x.shape
    hw = HEAD_WIDTH
    tm = ROW_TILE

    def body(fq, fk, fv, gx, q1, q2, q3, k1, k2, k3, v1, v2, v3, w_ref, dx_ref, x_ref, g_ref, o_ref, dp_ref, dg_ref, buf):
        @pl.when(pl.program_id(0) == 0)
        def _():
            dg_ref[...] = jnp.zeros_like(dg_ref)

        def branch_sum(refs):
            a, b, c = [r[...].astype(F32) if dil == 1 else _read_class_major(r, buf, dil) for r, dil in zip(refs, DILATIONS)]
            return (a + b) + c

        dp_ref[:, 0:hw] = fq[...]
        dp_ref[:, hw:2 * hw] = fk[...]
        dp_ref[:, 2 * hw:3 * hw] = fv[...]
        dp_ref[:, 3 * hw:4 * hw] = gx[...]
        dp_ref[:, 4 * hw:5 * hw] = (branch_sum((q1, q2, q3)) * QK_SCALE).astype(BF16)
        dp_ref[:, 5 * hw:6 * hw] = branch_sum((k1, k2, k3)).astype(BF16)
        dp_ref[:, 6 * hw:7 * hw] = branch_sum((v1, v2, v3)).astype(BF16)
        dh = _dot(dp_ref[...], w_ref[...])
        dx, dg = _rms_bwd(x_ref[...], g_ref[...], dh)
        dg_ref[...] += dg
        o_ref[...] = dx_ref[...] + dx

    half = _rows(tm, hw)
    by_dil = [_class_rows(tm, hw, dil) for dil in DILATIONS]
    return _call(
        body, name="in_proj_bwd", grid=(s_len // tm,),
        in_specs=[half] * 4 + by_dil * 3 + [_resident(w_in_t.shape), _rows(tm, d), _rows(tm, d), _resident((1, d))],
        out_specs=[_rows(tm, d), _rows(tm, 7 * hw), _acc_out((1, d))],
        out_shape=[_sds((s_len, d), F32), _sds((s_len, 7 * hw), BF16), _sds((1, d), F32)],
        scratch=[pltpu.VMEM((hw // PAIR, tm, PAIR), F32)],
    )(dfq, dfk, dfv, dgx, *ddq, *ddk, *ddv, w_in_t, dx1, x, g_pre)


def _expand_w_in(w_in):
    hw = HEAD_WIDTH
    gate = jnp.repeat(w_in[:, 3 * hw:3 * hw + N_HEADS], HEAD_DIM, axis=1)
    return jnp.concatenate([w_in[:, :3 * hw], gate, w_in[:, 3 * hw + N_HEADS:]], axis=1)


def _local_step(x, mem, target, g, w_bf, pack_early=None, ride_gather=None):
    hw = HEAD_WIDTH
    w_in_e = _expand_w_in(w_bf["w_in"])
    b_exp = jnp.repeat(g["b_f"], HEAD_DIM, axis=1)
    buckets = jnp.asarray(_dil_buckets())

    n_dil = len(DILATIONS)
    if ride_gather is None:
        h1, fqkv, gx, *dqkv = _in_proj(x, g["g_mix_pre"], w_in_e)
    else:
        h1, fqkv, gx, *dqkv, gathered = _in_proj(x, g["g_mix_pre"], w_in_e, riding=ride_gather[0])
        w_bf = {**w_bf, **ride_gather[1](gathered)}
    aq, ak, c = _gate_scan(gx, b_exp)
    stats = _fox_block_stats(fqkv, c)
    spans = _fox_spans(x.shape[0])
    span_idx = _fox_span_index(stats, x.shape[0])
    o_fox, lse_fox = lax.switch(span_idx, [functools.partial(_fox_fwd, s) for s in spans], fqkv, aq, ak, stats)
    bias = _dil_bias(g["rel_bias"], buckets)
    branches = [_dil_fwd(dqkv[p], bias, p) for p in range(n_dil)]
    x1, y1, o_dil, *lj = _mix_out(o_fox, [b[0] for b in branches], [b[1] for b in branches], w_bf["w_out"], x, g["g_mix_post"])
    hm, kx, vx = _mem_fwd(mem, g["g_mem"], w_bf["w_xk"], w_bf["w_xv"])
    x2, y2, h2, qx, ox = _xattn_fwd(x1, g["g_xattn_pre"], w_bf["w_xq"], kx, vx, w_bf["w_xo"], g["g_xattn_post"])
    h3, a, u, z = _ffn_up(x2, g["g_ffn_pre"], w_bf["w_gate"], w_bf["w_up"])
    y3, dx3, sq = _ffn_down_loss(z, w_bf["w_down"], x2, g["g_ffn_post"], target)

    grads = {}
    dy3, da, du, grads["g_ffn_post"] = _ffn_bwd_a(dx3, y3, g["g_ffn_post"], w_bf["w_down"].T, a, u)
    dx2, grads["g_ffn_pre"] = _ffn_bwd_b(da, du, w_bf["w_gate"].T, w_bf["w_up"].T, dx3, x2, g["g_ffn_pre"])
    grads["w_down"] = _weight_grad(z, dy3, "dw_down")
    grads["w_gate"] = _weight_grad(h3, da, "dw_gate")
    grads["w_up"] = _weight_grad(h3, du, "dw_up")
    dx1, dy2, dqx, dkx, dvx, grads["g_xattn_post"], grads["g_xattn_pre"] = _xattn_bwd(
        dx2, y2, g["g_xattn_post"], w_bf["w_xo"].T, qx, kx, vx, w_bf["w_xq"].T, x1, g["g_xattn_pre"])
    grads["w_xo"] = _weight_grad(ox, dy2, "dw_xo")
    grads["w_xq"] = _weight_grad(h2, dqx, "dw_xq")
    grads["w_xk"], grads["w_xv"], grads["g_mem"] = _mem_bwd(dkx, dvx, w_bf["w_xk"].T, w_bf["w_xv"].T, hm, mem, g["g_mem"])
    dy1, do_fox, delta_fox, grads["g_mix_post"], do_dil, delta_dil = _mix_out_bwd(
        dx1, y1, g["g_mix_post"], w_bf["w_out"].T, o_fox, o_dil)
    grads["w_out"] = jnp.concatenate([_weight_grad(o_fox, dy1, "dw_out_fox"), _weight_grad(o_dil, dy1, "dw_out_dil")], axis=0)
    aql, ad = _fox_bwd_prep(aq, lse_fox, delta_fox)
    dfq, rs, dfk, dfv, dc = lax.switch(
        span_idx, [functools.partial(_fox_bwd, s) for s in spans], fqkv, do_fox, aql, ak, ad, stats)
    dgx, db = _gate_bwd(rs, dc, gx, b_exp)
    grads["b_f"] = db[:, ::HEAD_DIM]
    riding = None if pack_early is None else pack_early(grads)
    dil = [_dil_bwd(dqkv[p], do_dil[p], lj[p], delta_dil[p], bias, p, riding if p == 0 else None) for p in range(n_dil)]
    landed_early = dil[0][4] if riding is not None else None
    grads["rel_bias"] = _rel_bias_grad(jnp.stack([t[3] for t in dil]), buckets)[:, :N_HEADS]
    grad_x, dproj, grads["g_mix_pre"] = _in_proj_bwd(
        dfq, dfk, dfv, dgx, [t[0] for t in dil], [t[1] for t in dil], [t[2] for t in dil], w_in_e.T, dx1, x, g["g_mix_pre"])
    dw_in_e = _weight_grad(h1, dproj, "dw_in")
    grads["w_in"] = jnp.concatenate(
        [dw_in_e[:, :3 * hw], dw_in_e[:, 3 * hw:4 * hw:HEAD_DIM], dw_in_e[:, 4 * hw:]], axis=1)
    return sq, grad_x, grads, landed_early


MESH = pl.DeviceIdType.MESH


def _my_place():
    return lax.axis_index("x"), lax.axis_index("y"), lax.axis_index("c")


def _gather_copies(x_ref, out_ref, send_sems, recv_sems, local_sem):
    mx, my, mc = _my_place()
    me, sibling = (mx, my, mc), (mx, my, 1 - mc)
    chips = [(1 - mx, my), (mx, 1 - my), (1 - mx, 1 - my)]

    def slot(px, py, pc):
        return out_ref.at[4 * px + 2 * py + pc]

    def copy(k, block, to, src=None):
        return pltpu.make_async_remote_copy(
            src_ref=slot(*block) if src is None else src, dst_ref=slot(*block),
            send_sem=send_sems.at[k], recv_sem=recv_sems.at[k], device_id=to, device_id_type=MESH)

    mine = pltpu.make_async_copy(x_ref, slot(*me), local_sem)
    first = [copy(0, me, sibling, src=x_ref)] + [copy(1 + j, me, (*chip, mc), src=x_ref) for j, chip in enumerate(chips)]
    passed = [copy(4 + j, (*chip, mc), sibling) for j, chip in enumerate(chips)]
    over_ici = [copy(1 + j, (*chip, mc), me) for j, chip in enumerate(chips)]
    from_sibling = [copy(0, sibling, me)] + [copy(4 + j, (*chip, 1 - mc), me) for j, chip in enumerate(chips)]
    return mine, first, passed, over_ici, from_sibling


def _gather_start(*refs):
    mine, first, _, _, _ = _gather_copies(*refs)
    mine.start()
    for cp in first:
        cp.start()


def _gather_forward(*refs):
    _, _, passed, over_ici, _ = _gather_copies(*refs)
    for arrival, forward in zip(over_ici, passed):
        arrival.wait_recv()
        forward.start()


def _gather_finish(*refs):
    mine, first, passed, _, from_sibling = _gather_copies(*refs)
    for arrival in from_sibling:
        arrival.wait_recv()
    for cp in first + passed:
        cp.wait_send()
    mine.wait()


def _all_gather(x, name):
    rows, lanes = x.shape

    def body(x_ref, out_ref, send_sems, recv_sems, local_sem):
        _gather_start(x_ref, out_ref, send_sems, recv_sems, local_sem)
        _gather_forward(x_ref, out_ref, send_sems, recv_sems, local_sem)
        _gather_finish(x_ref, out_ref, send_sems, recv_sems, local_sem)

    return pl.pallas_call(
        body, name=name, out_shape=_sds((N_DEV, rows, lanes), x.dtype),
        in_specs=[pl.BlockSpec(memory_space=pl.ANY)], out_specs=pl.BlockSpec(memory_space=pl.ANY),
        scratch_shapes=[pltpu.SemaphoreType.DMA((N_DEV - 1,)), pltpu.SemaphoreType.DMA((N_DEV - 1,)), pltpu.SemaphoreType.DMA],
    )(x)


def _exchange_copies(g_ref, land_ref, send_sems, recv_sems, local_sem):
    mx, my, mc = _my_place()
    me = 4 * mx + 2 * my + mc
    mine = pltpu.make_async_copy(g_ref.at[me], land_ref.at[me], local_sem)
    sent, arriving = [], []
    for k in (1, 2, 3, 4, 5, 6, 7):
        px = 1 - mx if k & 4 else mx
        py = 1 - my if k & 2 else my
        pc = 1 - mc if k & 1 else mc
        peer = 4 * px + 2 * py + pc
        sent.append(pltpu.make_async_remote_copy(
            src_ref=g_ref.at[peer], dst_ref=land_ref.at[me], send_sem=send_sems.at[k - 1], recv_sem=recv_sems.at[k - 1],
            device_id=(px, py, pc), device_id_type=MESH))
        arriving.append(pltpu.make_async_remote_copy(
            src_ref=g_ref.at[me], dst_ref=land_ref.at[peer], send_sem=send_sems.at[k - 1], recv_sem=recv_sems.at[k - 1],
            device_id=(px, py, pc), device_id_type=MESH))
    return mine, sent, arriving


def _exchange_start(*refs):
    mine, sent, _ = _exchange_copies(*refs)
    mine.start()
    for cp in sent:
        cp.start()


def _exchange_wait(*refs):
    mine, sent, arriving = _exchange_copies(*refs)
    for cp in arriving:
        cp.wait_recv()
    for cp in sent:
        cp.wait_send()
    mine.wait()


EXCHANGE_SEMAPHORES = (pltpu.SemaphoreType.DMA((N_DEV - 1,)), pltpu.SemaphoreType.DMA((N_DEV - 1,)), pltpu.SemaphoreType.DMA)


def _exchange(g, name):
    def body(g_ref, land_ref, send_sems, recv_sems, local_sem):
        _exchange_start(g_ref, land_ref, send_sems, recv_sems, local_sem)
        _exchange_wait(g_ref, land_ref, send_sems, recv_sems, local_sem)

    return pl.pallas_call(
        body, name=name, out_shape=_sds(g.shape, g.dtype),
        in_specs=[pl.BlockSpec(memory_space=pl.ANY)], out_specs=pl.BlockSpec(memory_space=pl.ANY),
        scratch_shapes=list(EXCHANGE_SEMAPHORES),
    )(g)


def _sum_slots(parts, name):
    n, rows, lanes = parts.shape
    tr = 512 if rows % 512 == 0 else rows

    def body(p_ref, o_ref):
        acc = p_ref[0].astype(F32)
        for j in range(1, n):
            acc = acc + p_ref[j].astype(F32)
        o_ref[...] = acc

    return _call(
        body, name=name, grid=(rows // tr,),
        in_specs=[pl.BlockSpec((n, tr, lanes), lambda i: (0, i, 0))], out_specs=_rows(tr, lanes),
        out_shape=_sds((rows, lanes), F32),
    )(parts)


def _adamw(w, g, m, v, name):
    def body(w_ref, g_ref, m_ref, v_ref, d_ref, nm_ref, nv_ref):
        gv = g_ref[...]
        m_new = ADAM_B1 * m_ref[...] + (1.0 - ADAM_B1) * gv
        v_new = ADAM_B2 * v_ref[...] + (1.0 - ADAM_B2) * (gv * gv)
        nm_ref[...] = m_new
        nv_ref[...] = v_new
        m_hat = m_new / (1.0 - ADAM_B1 ** ADAM_STEP)
        v_hat = v_new / (1.0 - ADAM_B2 ** ADAM_STEP)
        d_ref[...] = -ADAM_LR * (m_hat / (jnp.sqrt(v_hat) + ADAM_EPS) + ADAM_WD * w_ref[...])

    out = _sds(w.shape, F32)
    return pl.pallas_call(
        body, name=name, out_shape=[out, out, out],
        compiler_params=pltpu.CompilerParams(vmem_limit_bytes=V7X_VMEM_LIMIT_BYTES),
    )(w, g, m, v)


def _loss_head(sq, d_model):
    def body(sq_ref, o_ref):
        tot = jnp.sum(jnp.sum(sq_ref[...], axis=1, keepdims=True), axis=0, keepdims=True)
        o_ref[...] = 0.5 * (tot / d_model)

    return pl.pallas_call(body, name="loss_head", out_shape=_sds((1, 1), F32))(sq)


_BIG = (("w_in", 1), ("w_out", 0), ("w_xq", 0), ("w_xk", 0), ("w_xv", 0), ("w_xo", 1), ("w_gate", 1), ("w_up", 1), ("w_down", 0))
_EARLY = ("w_xq", "w_xk", "w_xv", "w_xo", "w_gate", "w_up", "w_down")
_LATE = ("w_in", "w_out")
_SMALL = ("g_mix_pre", "b_f", "rel_bias", "g_mix_post", "g_xattn_pre", "g_mem", "g_xattn_post", "g_ffn_pre", "g_ffn_post")
LANES = 128
BF16_ROW_TILE = 16
BIG_ROW_ALIGN = 512


def _round_up(n, k):
    return -(-n // k) * k


def _row_starts(sizes):
    starts, rows = [], 0
    for n in sizes:
        starts.append(rows)
        rows += _round_up(n, LANES) // LANES
    return starts


def _pack_rows(flat_parts, row_align, dtype):
    lead = flat_parts[0].shape[:-1]
    starts, rows, padded = [], 0, []
    for p in flat_parts:
        n = _round_up(p.shape[-1], LANES)
        starts.append(rows)
        rows += n // LANES
        padded.append(jnp.pad(p.astype(dtype), [(0, 0)] * len(lead) + [(0, n - p.shape[-1])]))
    total = _round_up(rows, row_align)
    padded.append(jnp.zeros(lead + ((total - rows) * LANES,), dtype))
    return jnp.concatenate(padded, axis=-1).reshape(lead + (total, LANES)), starts


def _unpack_rows(buf, starts, shapes):
    lead = buf.shape[:-2]
    flat = buf.reshape(lead + (-1,))
    out = []
    for st, shp in zip(starts, shapes):
        n = int(np.prod(shp))
        out.append(flat[..., st * LANES:st * LANES + n].reshape(lead + tuple(shp)))
    return out


def kernel(x, mem, g_mix_pre, w_in, b_f, rel_bias, w_out, g_mix_post, g_xattn_pre, g_mem, w_xq, w_xk, w_xv, w_xo, g_xattn_post, g_ffn_pre, w_gate, w_up, w_down, g_ffn_post, loss_target, m_g_mix_pre, m_w_in, m_b_f, m_rel_bias, m_w_out, m_g_mix_post, m_g_xattn_pre, m_g_mem, m_w_xq, m_w_xk, m_w_xv, m_w_xo, m_g_xattn_post, m_g_ffn_pre, m_w_gate, m_w_up, m_w_down, m_g_ffn_post, v_g_mix_pre, v_w_in, v_b_f, v_rel_bias, v_w_out, v_g_mix_post, v_g_xattn_pre, v_g_mem, v_w_xq, v_w_xk, v_w_xv, v_w_xo, v_g_xattn_post, v_g_ffn_pre, v_w_gate, v_w_up, v_w_down, v_g_ffn_post):
    given = dict(locals())
    order = ("g_mix_pre", "w_in", "b_f", "rel_bias", "w_out", "g_mix_post", "g_xattn_pre", "g_mem", "w_xq", "w_xk", "w_xv",
             "w_xo", "g_xattn_post", "g_ffn_pre", "w_gate", "w_up", "w_down", "g_ffn_post")
    two_d = lambda a: a.reshape(a.shape[-2:])
    w_loc = {n: two_d(given[n]) for n in order}
    m_loc = {n: two_d(given["m_" + n]) for n in order}
    v_loc = {n: two_d(given["v_" + n]) for n in order}
    d_model = x.shape[-1]

    shard_shapes = [w_loc[n].shape for n, _ in _BIG]
    axis_of = dict(_BIG)

    def gathered_weights(gathered, names, row_starts):
        out = {}
        for n, part in zip(names, _unpack_rows(gathered, row_starts, [w_loc[n].shape for n in names])):
            r, c = part.shape[1:]
            out[n] = part.reshape(N_DEV * r, c) if axis_of[n] == 0 else part.transpose(1, 0, 2).reshape(r, N_DEV * c)
        return out

    first_packed, first_starts = _pack_rows([w_loc["w_in"].reshape(-1)], BF16_ROW_TILE, BF16)
    w_bf = gathered_weights(_all_gather(first_packed, "gather_w_in"), ["w_in"], first_starts)
    later = [n for n, _ in _BIG if n != "w_in"]
    later_packed, later_starts = _pack_rows([w_loc[n].reshape(-1) for n in later], BF16_ROW_TILE, BF16)
    ride_gather = (later_packed, lambda gathered: gathered_weights(gathered, later, later_starts))

    shape_of = dict(zip([n for n, _ in _BIG], shard_shapes))

    def owner_slots(grads, names):
        per_owner = []
        for n in names:
            r, c = shape_of[n]
            gfull = grads[n]
            per_owner.append(gfull.reshape(N_DEV, r * c) if axis_of[n] == 0
                             else gfull.reshape(r, N_DEV, c).transpose(1, 0, 2).reshape(N_DEV, r * c))
        return _pack_rows(per_owner, BIG_ROW_ALIGN, BF16)

    small = {n: w_loc[n] for n in _SMALL}
    sq, grad_x, grads, landed_early = _local_step(
        two_d(x), two_d(mem), two_d(loss_target), small, w_bf, pack_early=lambda gr: owner_slots(gr, _EARLY)[0],
        ride_gather=ride_gather)
    late_slots, late_starts = owner_slots(grads, _LATE)
    landed_late = _exchange(late_slots, "exchange_grads")
    early_starts = _row_starts([int(np.prod(shape_of[n])) for n in _EARLY])
    g_big = {}
    for names, landed, row_starts, label in ((_EARLY, landed_early, early_starts, "sum_grads_early"),
                                             (_LATE, landed_late, late_starts, "sum_grads_late")):
        g_big.update(zip(names, _unpack_rows(_sum_slots(landed, label), row_starts, [shape_of[n] for n in names])))

    small_parts = [grads[n].reshape(-1) for n in _SMALL] + [sq.reshape(-1)]
    small_shapes = [w_loc[n].shape for n in _SMALL] + [sq.shape]
    spacked, sstarts = _pack_rows(small_parts, 8, F32)
    ssum = _sum_slots(_all_gather(spacked, "gather_small"), "sum_small")
    g_small = dict(zip(_SMALL, _unpack_rows(ssum, sstarts, small_shapes)[:-1]))
    sq_rows = sq.size // LANES
    loss = _loss_head(ssum[sstarts[-1]:sstarts[-1] + sq_rows], d_model).reshape(())

    g_loc, delta, new_m, new_v = {}, {}, {}, {}
    for n, _ in _BIG:
        g_loc[n] = g_big[n]
        delta[n], new_m[n], new_v[n] = _adamw(w_loc[n], g_big[n], m_loc[n], v_loc[n], "adamw_" + n)
    pk = lambda d: _pack_rows([d[n].reshape(-1) for n in _SMALL], 8, F32)[0]
    pstarts = _pack_rows([w_loc[n].reshape(-1) for n in _SMALL], 8, F32)[1]
    d_s, m_s, v_s = _adamw(pk(w_loc), pk(g_small), pk(m_loc), pk(v_loc), "adamw_small")
    shapes_s = [w_loc[n].shape for n in _SMALL]
    for n, dd, mm, vv in zip(_SMALL, _unpack_rows(d_s, pstarts, shapes_s), _unpack_rows(m_s, pstarts, shapes_s),
                             _unpack_rows(v_s, pstarts, shapes_s)):
        g_loc[n], delta[n], new_m[n], new_v[n] = g_small[n], dd, mm, vv

    like = lambda d: [d[n].reshape(given[n].shape) for n in order]
    return (loss, grad_x.reshape(x.shape), *like(g_loc), *like(delta), *like(new_m), *like(new_v))
```

```python
import functools

import numpy as np
import jax
import jax.numpy as jnp
from jax import lax
from jax.experimental import pallas as pl
from jax.experimental.pallas import tpu as pltpu

F32 = jnp.float32
BF16 = jnp.bfloat16

RMS_EPS = 1e-6
HEAD_DIM = 64
N_HEADS = 8
HEAD_WIDTH = N_HEADS * HEAD_DIM
PAIR = 2 * HEAD_DIM
N_PAIRS = N_HEADS // 2
DIL_BLOCK = 128
DILATIONS = (1, 4, 16)
N_BUCKETS = 32
MAX_DISTANCE = 2048
N_MEM_HEADS = 4
QK_SCALE = HEAD_DIM ** -0.5
NEG = -1e30
FOX_SKIP_MARGIN = 110.0
FOX_SHORT_SPANS = (4, 5, 6, 7, 8, 10, 12, 16)
N_DEV = 8

ADAM_LR = 0.001
ADAM_B1 = 0.9
ADAM_B2 = 0.999
ADAM_EPS = 1e-08
ADAM_WD = 0.01
ADAM_STEP = 10

V7X_VMEM_LIMIT_BYTES = 56 * 2 ** 20
ROW_TILE = 256
ATT_BLOCK = 512
SCAN_TILE = 256


def _call(body, *, name, grid, in_specs, out_specs, out_shape, scratch=()):
    return pl.pallas_call(
        body, name=name, grid=grid, in_specs=in_specs, out_specs=out_specs, out_shape=out_shape,
        scratch_shapes=list(scratch),
        compiler_params=pltpu.CompilerParams(
            dimension_semantics=("arbitrary",) * len(grid), vmem_limit_bytes=V7X_VMEM_LIMIT_BYTES))


def _rows(tm, n):
    return pl.BlockSpec((tm, n), lambda i: (i, 0))


def _resident(shape):
    zeros = (0,) * len(shape)
    return pl.BlockSpec(shape, lambda i: zeros, pipeline_mode=pl.Buffered(1))


def _acc_out(shape):
    zeros = (0,) * len(shape)
    return pl.BlockSpec(shape, lambda i: zeros)


def _sds(shape, dtype):
    return jax.ShapeDtypeStruct(shape, dtype)


def _dot(a, b):
    return jnp.dot(a, b, preferred_element_type=F32)


def _dot_nt(a, b):
    return lax.dot_general(a, b, (((1,), (1,)), ((), ())), preferred_element_type=F32)


def _dot_tn(a, b):
    return lax.dot_general(a, b, (((0,), (0,)), ((), ())), preferred_element_type=F32)


def _rms_fwd(x, g):
    r = lax.rsqrt(jnp.mean(x * x, axis=-1, keepdims=True) + RMS_EPS)
    return (x * r) * g


def _rms_bwd(xin, g, dy):
    r = lax.rsqrt(jnp.mean(xin * xin, axis=-1, keepdims=True) + RMS_EPS)
    xhat = xin * r
    dg = jnp.sum(dy * xhat, axis=0, keepdims=True)
    dxh = dy * g
    dx = r * (dxh - xhat * jnp.mean(dxh * xhat, axis=-1, keepdims=True))
    return dx, dg


def _first_head_lanes():
    return lax.broadcasted_iota(jnp.int32, (1, PAIR), 1) < HEAD_DIM


def _pick(mask, a, b):
    return jnp.where(mask, a, b)


def _zero_other(mask, v):
    return jnp.where(mask, v, jnp.zeros_like(v))


def _store_lane_blocks(buf_ref, val):
    for cb in range(buf_ref.shape[0]):
        buf_ref[cb] = val[:, cb * PAIR:(cb + 1) * PAIR].astype(F32)


def _load_lane_blocks(buf_ref):
    return jnp.concatenate([buf_ref[cb] for cb in range(buf_ref.shape[0])], axis=1)


def _write_class_major(buf_ref, out_ref, dil):
    n, tile, _ = buf_ref.shape
    for r in range(dil):
        for cb in range(n):
            col = (r * n + cb) * PAIR
            out_ref[:, col:col + PAIR] = buf_ref.at[cb][pl.ds(r, tile // dil, stride=dil), :].astype(out_ref.dtype)


def _read_class_major(in_ref, buf_ref, dil):
    n, tile, _ = buf_ref.shape
    for r in range(dil):
        for cb in range(n):
            col = (r * n + cb) * PAIR
            buf_ref.at[cb][pl.ds(r, tile // dil, stride=dil), :] = in_ref[:, col:col + PAIR].astype(F32)
    return _load_lane_blocks(buf_ref)


def _class_rows(tm, width, dil):
    return _rows(tm // dil, dil * width)


def _in_proj(x, g, w, riding=None):
    s_len, d = x.shape
    tm = ROW_TILE
    hw = HEAD_WIDTH
    n_steps = s_len // tm

    def body(x_ref, g_ref, w_ref, *rest):
        if riding is None:
            h_ref, fqkv_ref, gx_ref, *dil_refs, buf = rest
        else:
            part_ref, h_ref, fqkv_ref, gx_ref, *dil_refs, all_ref, buf, send_sems, recv_sems, local_sem = rest
            gather = (part_ref, all_ref, send_sems, recv_sems, local_sem)

            @pl.when(pl.program_id(0) == 0)
            def _():
                _gather_start(*gather)

            @pl.when(pl.program_id(0) == (3 * n_steps) // 4)
            def _():
                _gather_forward(*gather)

        h = _rms_fwd(x_ref[...], g_ref[...]).astype(BF16)
        h_ref[...] = h
        proj = _dot(h, w_ref[...])
        fqkv_ref[:, 0:hw] = (proj[:, 0:hw] * QK_SCALE).astype(BF16)
        fqkv_ref[:, hw:3 * hw] = proj[:, hw:3 * hw].astype(BF16)
        gx_ref[...] = proj[:, 3 * hw:4 * hw]
        dqkv = jnp.concatenate([proj[:, 4 * hw:5 * hw] * QK_SCALE, proj[:, 5 * hw:7 * hw]], axis=1)
        _store_lane_blocks(buf, dqkv)
        for ref, dil in zip(dil_refs, DILATIONS):
            if dil == 1:
                ref[...] = dqkv.astype(BF16)
            else:
                _write_class_major(buf, ref, dil)

        if riding is not None:
            @pl.when(pl.program_id(0) == n_steps - 1)
            def _():
                _gather_finish(*gather)

    anywhere = pl.BlockSpec(memory_space=pl.ANY)
    extra_in, extra_out, extra_shape, extra_scratch, extra_args = [], [], [], [], []
    if riding is not None:
        extra_in, extra_out = [anywhere], [anywhere]
        extra_shape = [_sds((N_DEV,) + riding.shape, riding.dtype)]
        extra_scratch, extra_args = list(EXCHANGE_SEMAPHORES), [riding]
    return _call(
        body, name="in_proj", grid=(n_steps,),
        in_specs=[_rows(tm, d), _resident((1, d)), _resident(w.shape)] + extra_in,
        out_specs=[_rows(tm, d), _rows(tm, 3 * hw), _rows(tm, hw)] + [_class_rows(tm, 3 * hw, dil) for dil in DILATIONS]
        + extra_out,
        out_shape=[_sds((s_len, d), BF16), _sds((s_len, 3 * hw), BF16), _sds((s_len, hw), F32)]
        + [_sds((s_len // dil, dil * 3 * hw), BF16) for dil in DILATIONS] + extra_shape,
        scratch=[pltpu.VMEM((3 * hw // PAIR, tm, PAIR), F32)] + extra_scratch,
    )(x, g, w, *extra_args)


def _swap_halves(x):
    return jnp.concatenate([pltpu.roll(x[:, i * PAIR:(i + 1) * PAIR], HEAD_DIM, 1) for i in range(x.shape[1] // PAIR)], axis=1)


def _split3(x):
    hi = x.astype(BF16)
    r = x - hi.astype(F32)
    mid = r.astype(BF16)
    lo = (r - mid.astype(F32)).astype(BF16)
    return hi, mid, lo


def _tri_dot(tri, x):
    return sum(_dot(tri, piece) for piece in _split3(x))


def _lane_in_head(width):
    return lax.broadcasted_iota(jnp.int32, (1, width), 1) % HEAD_DIM


def _place3(jj, first, pieces, base):
    out = base
    for i, p in enumerate(pieces):
        out = jnp.where(jj == first + i, p, out)
    return out


def _gate_scan(gx, b_exp):
    s_len, hw = gx.shape
    t = min(SCAN_TILE, s_len)
    tri = jnp.asarray(np.tril(np.ones((t, t), np.float32))).astype(BF16)

    def body(gx_ref, b_ref, tri_ref, aq_ref, ak_ref, c_ref, carry):
        @pl.when(pl.program_id(0) == 0)
        def _():
            carry[...] = jnp.zeros_like(carry)

        z = gx_ref[...] + b_ref[...]
        lf = jnp.minimum(z, 0.0) - jnp.log1p(jnp.exp(-jnp.abs(z)))
        c = _tri_dot(tri_ref[...], lf) + carry[...]
        carry[...] = c[t - 1:t, :]
        c_ref[...] = c
        hi, mid, lo = _split3(_swap_halves(c))
        jj = _lane_in_head(hw)
        zero = jnp.zeros_like(hi)
        one = jnp.ones_like(hi)
        aq_ref[...] = _place3(jj, 0, (hi, mid, lo), jnp.where(jj < 6, one, zero))
        ak_ref[...] = _place3(jj, 3, (-hi, -mid, -lo), jnp.where(jj < 9, one, zero))

    return _call(
        body, name="gate_scan", grid=(s_len // t,),
        in_specs=[_rows(t, hw), _resident((1, hw)), _resident((t, t))],
        out_specs=[_rows(t, hw), _rows(t, hw), _rows(t, hw)],
        out_shape=[_sds((s_len, hw), BF16), _sds((s_len, hw), BF16), _sds((s_len, hw), F32)],
        scratch=[pltpu.VMEM((1, hw), F32)],
    )(gx, b_exp, tri)


def _head_block_ones():
    head_of = np.arange(HEAD_WIDTH) // HEAD_DIM
    return jnp.asarray((head_of[:, None] == head_of[None, :]).astype(np.float32))


def _fox_block_stats(fqkv, c):
    s_len = fqkv.shape[0]
    hw = HEAD_WIDTH
    b = min(ATT_BLOCK, s_len)
    nb = s_len // b

    def body(q_ref, k_ref, c_ref, ones_ref, o_ref):
        q, k, cv = q_ref[...].astype(F32), k_ref[...].astype(F32), c_ref[...]
        seg = lambda x: _dot(x.astype(BF16), ones_ref[...])
        col_max = lambda x: jnp.max(x, axis=0, keepdims=True)
        col_min = lambda x: jnp.min(x, axis=0, keepdims=True)
        qn = 1.01 * jnp.sqrt(col_max(seg(q * q)))
        kn = 1.01 * jnp.sqrt(col_max(seg(k * k)))
        dmin = col_min(seg(q * k)) - (2.0 ** -8) * qn * kn
        o_ref[0] = jnp.concatenate([qn, col_max(cv) - dmin, kn, col_min(cv), jnp.zeros((4, hw), F32)], axis=0)

    stats = _call(
        body, name="fox_block_stats", grid=(nb,),
        in_specs=[pl.BlockSpec((b, hw), lambda i: (i, 0)), pl.BlockSpec((b, hw), lambda i: (i, 1)), _rows(b, hw),
                  _resident((hw, hw))],
        out_specs=pl.BlockSpec((1, 8, hw), lambda i: (i, 0, 0)),
        out_shape=_sds((nb, 8, hw), F32),
    )(fqkv, fqkv, c, _head_block_ones().astype(BF16))
    st = jnp.transpose(stats[:, :4, ::HEAD_DIM], (1, 2, 0))
    bound = st[0][:, :, None] * st[2][:, None, :] + st[1][:, :, None] - st[3][:, None, :]
    need_h = jnp.logical_not(bound < -FOX_SKIP_MARGIN)
    need = jnp.logical_or(need_h[0::2], need_h[1::2])
    ii = lax.broadcasted_iota(jnp.int32, (1, nb, nb), 1)
    jj = lax.broadcasted_iota(jnp.int32, (1, nb, nb), 2)
    first_needed = jnp.min(jnp.where(jnp.logical_or(jnp.logical_and(need, jj < ii), jj == ii), jj, nb), axis=2)
    window = ii[:, :, 0] - first_needed + 1
    in_window = jnp.logical_and(jj >= first_needed[:, :, None], jj <= ii)
    last_query = jnp.max(jnp.where(in_window, ii, 0), axis=1)
    return jnp.concatenate([st.reshape(4 * N_HEADS, nb), window.astype(F32), last_query.astype(F32)], axis=0)


FOX_WINDOW_ROW = 4 * N_HEADS
FOX_LAST_QUERY_ROW = 4 * N_HEADS + N_PAIRS


def _head_needed(st_ref, h, i, j):
    bound = st_ref[h, i] * st_ref[2 * N_HEADS + h, j] + st_ref[N_HEADS + h, i] - st_ref[3 * N_HEADS + h, j]
    return jnp.logical_not(bound < -FOX_SKIP_MARGIN)


def _lower_triangle(n):
    return lax.broadcasted_iota(jnp.int32, (n, n), 1) <= lax.broadcasted_iota(jnp.int32, (n, n), 0)


def _fox_spans(s_len):
    nq = s_len // min(ATT_BLOCK, s_len)
    return [s for s in FOX_SHORT_SPANS if s < nq] + [nq]


def _fox_span_index(stats, s_len):
    longest = jnp.max(stats[FOX_WINDOW_ROW:FOX_WINDOW_ROW + N_PAIRS])
    idx = jnp.int32(0)
    for s in _fox_spans(s_len)[:-1]:
        idx = idx + (longest > s).astype(jnp.int32)
    return idx


def _fox_fwd(span, fqkv, aq, ak, stats):
    s_len = fqkv.shape[0]
    bq = bk = min(ATT_BLOCK, s_len)
    nq = s_len // bq

    def body(st_ref, q_ref, k_ref, v_ref, aq_ref, ak_ref, o_ref, lse_ref, qa_ref, m_ref, acc_ref):
        pair, qi, back = pl.program_id(0), pl.program_id(1), pl.program_id(2)
        kj = jnp.maximum(qi - back, 0)
        in0 = _first_head_lanes()
        not0 = jnp.logical_not(in0)

        @pl.when(back == 0)
        def _():
            q2, a2 = q_ref[...], aq_ref[...]
            qa_ref[0] = jnp.where(in0, q2, a2)
            qa_ref[1] = jnp.where(in0, a2, q2)
            m_ref[...] = jnp.full_like(m_ref, NEG)
            acc_ref[...] = jnp.zeros_like(acc_ref)

        def head_step(a, mine, masked):
            k2, v2 = k_ref[...], v_ref[...]
            s = _dot_nt(qa_ref[a], jnp.where(mine, k2, ak_ref[...]))
            if masked:
                s = jnp.where(_lower_triangle(bq), s, NEG)
            m_old = m_ref[a]
            m_new = jnp.maximum(m_old, jnp.max(s, axis=1, keepdims=True))
            p = jnp.exp(s - jnp.tile(m_new, (1, bk // PAIR))).astype(BF16)
            acc_ref[a] = jnp.exp(m_old - m_new) * acc_ref[a] + _dot(p, jnp.where(mine, v2, jnp.ones_like(v2)))
            m_ref[a] = m_new

        @pl.when(back == 0)
        def _():
            head_step(0, in0, True)
            head_step(1, not0, True)

        below = jnp.logical_and(back > 0, back <= qi)
        for a, mine in ((0, in0), (1, not0)):
            @pl.when(jnp.logical_and(below, _head_needed(st_ref, 2 * pair + a, qi, kj)))
            def _():
                head_step(a, mine, False)

        @pl.when(back == jnp.minimum(qi, span - 1))
        def _():
            acc0, acc1 = acc_ref[0], acc_ref[1]
            l2 = pltpu.roll(_pick(in0, acc1, acc0), HEAD_DIM, 1)
            o_ref[...] = _pick(in0, acc0, acc1) / l2
            lse_ref[...] = _pick(in0, m_ref[0], m_ref[1]) + jnp.log(l2)

    blk = lambda rows: (rows, PAIR)
    qmap = lambda p, i, b, st: (i, p)

    def key_block(p, i, b, st):
        window = jnp.clip(st[FOX_WINDOW_ROW + p, i].astype(jnp.int32), 1, i + 1)
        return i - jnp.minimum(b, window - 1)

    return pl.pallas_call(
        body, name=f"fox_fwd_{span}",
        grid_spec=pltpu.PrefetchScalarGridSpec(
            num_scalar_prefetch=1, grid=(N_PAIRS, nq, span),
            in_specs=[
                pl.BlockSpec(blk(bq), qmap),
                pl.BlockSpec(blk(bk), lambda p, i, b, st: (key_block(p, i, b, st), N_PAIRS + p)),
                pl.BlockSpec(blk(bk), lambda p, i, b, st: (key_block(p, i, b, st), 2 * N_PAIRS + p)),
                pl.BlockSpec(blk(bq), qmap),
                pl.BlockSpec(blk(bk), lambda p, i, b, st: (key_block(p, i, b, st), p)),
            ],
            out_specs=[pl.BlockSpec(blk(bq), qmap), pl.BlockSpec(blk(bq), qmap)],
            scratch_shapes=[pltpu.VMEM((2, bq, PAIR), BF16), pltpu.VMEM((2, bq, PAIR), F32), pltpu.VMEM((2, bq, PAIR), F32)]),
        out_shape=[_sds((s_len, HEAD_WIDTH), F32), _sds((s_len, HEAD_WIDTH), F32)],
        compiler_params=pltpu.CompilerParams(dimension_semantics=("arbitrary",) * 3, vmem_limit_bytes=V7X_VMEM_LIMIT_BYTES),
    )(stats, fqkv, fqkv, fqkv, aq, ak)


def _t5_bucket(dist):
    max_exact = N_BUCKETS // 2
    d = np.maximum(dist, 1).astype(np.float32)
    large = max_exact + (np.log(d / max_exact) / np.log(MAX_DISTANCE / max_exact) * (N_BUCKETS - max_exact)).astype(np.int32)
    large = np.minimum(large, N_BUCKETS - 1)
    return np.where(dist < max_exact, dist, large).astype(np.int32)


def _dil_buckets():
    w = DIL_BLOCK
    qi = np.arange(w)[:, None]
    kj = np.arange(2 * w)[None, :]
    sub = qi + w - kj
    band = (sub >= 0) & (sub <= w)
    out = [np.where(band, _t5_bucket(np.clip(sub, 0, w) * dil), -1) for dil in DILATIONS]
    return np.stack(out).astype(np.int32)


def _dil_bias(rel_bias, buckets):
    w = DIL_BLOCK

    def body(rb_ref, bk_ref, o_ref):
        for p in range(len(DILATIONS)):
            bk = bk_ref[p]
            for h in range(N_HEADS):
                def add(b, acc):
                    return acc + jnp.where(bk == b, rb_ref[b, h], 0.0)
                acc = lax.fori_loop(0, N_BUCKETS, add, jnp.zeros((w, 2 * w), F32))
                o_ref[p, h] = jnp.where(bk < 0, NEG, acc)

    return pl.pallas_call(
        body, name="dil_bias",
        in_specs=[pl.BlockSpec(memory_space=pltpu.SMEM), pl.BlockSpec(memory_space=pltpu.VMEM)],
        out_specs=pl.BlockSpec(memory_space=pltpu.VMEM),
        out_shape=_sds((len(DILATIONS), N_HEADS, w, 2 * w), F32),
    )(rel_bias, buckets)


def _dil_fwd(view, bias, branch):
    dil = DILATIONS[branch]
    w = DIL_BLOCK
    hw = HEAD_WIDTH
    length = view.shape[0]
    nb = length // w

    def body(q_ref, kc_ref, kp_ref, vc_ref, vp_ref, b_ref, o_ref, lse_ref):
        n = pl.program_id(1)
        in0 = _first_head_lanes()
        not0 = jnp.logical_not(in0)
        pairs = [slice(pr * PAIR, (pr + 1) * PAIR) for pr in range(N_PAIRS)]
        tiles = []
        for sl in pairs:
            q2 = q_ref[:, sl]
            qq = jnp.concatenate([_zero_other(in0, q2), _zero_other(not0, q2)], axis=0)
            tiles.append(jnp.concatenate([_dot_nt(qq, kp_ref[:, sl]), _dot_nt(qq, kc_ref[:, sl])], axis=1))
        s = jnp.concatenate(tiles, axis=0) + b_ref[...].reshape(N_HEADS * w, 2 * w)
        prev_half = lax.broadcasted_iota(jnp.int32, (1, 2 * w), 1) < w
        s = jnp.where(jnp.logical_and(n == 0, prev_half), NEG, s)
        m = jnp.max(s, axis=1, keepdims=True)
        e = jnp.exp(s - m)
        l = jnp.sum(e, axis=1, keepdims=True)
        p = (e / l).astype(BF16)
        lse = m + jnp.log(l)
        for pr, sl in enumerate(pairs):
            pp = p[2 * pr * w:(2 * pr + 2) * w]
            o2 = _dot(pp[:, :w], vp_ref[:, sl]) + _dot(pp[:, w:], vc_ref[:, sl])
            o_ref[:, sl] = _pick(in0, o2[:w], o2[w:])
            lse_ref[:, sl] = _pick(in0, lse[2 * pr * w:(2 * pr + 1) * w], lse[(2 * pr + 1) * w:(2 * pr + 2) * w])

    prev = lambda n: jnp.maximum(n - 1, 0)
    out = pl.pallas_call(
        body, name=f"dil_fwd_{dil}", grid=(dil, nb),
        in_specs=[
            pl.BlockSpec((w, hw), lambda r, n: (n, 3 * r)),
            pl.BlockSpec((w, hw), lambda r, n: (n, 3 * r + 1)),
            pl.BlockSpec((w, hw), lambda r, n: (prev(n), 3 * r + 1)),
            pl.BlockSpec((w, hw), lambda r, n: (n, 3 * r + 2)),
            pl.BlockSpec((w, hw), lambda r, n: (prev(n), 3 * r + 2)),
            pl.BlockSpec((None, N_HEADS, w, 2 * w), lambda r, n: (branch, 0, 0, 0)),
        ],
        out_specs=[pl.BlockSpec((w, hw), lambda r, n: (n, r)), pl.BlockSpec((w, hw), lambda r, n: (n, r))],
        out_shape=[_sds((length, dil * hw), F32), _sds((length, dil * hw), F32)],
        compiler_params=pltpu.CompilerParams(dimension_semantics=("arbitrary",) * 2, vmem_limit_bytes=V7X_VMEM_LIMIT_BYTES),
    )(view, view, view, view, view, bias)
    return out[0], out[1]


def _mix_out(o_fox, o_br, lse_br, w_out, x, g_post):
    s_len, d = x.shape
    hw = HEAD_WIDTH
    tm = ROW_TILE

    def body(of_ref, o1, o2, o3, l1, l2, l3, w_ref, x_ref, g_ref, x1_ref, y1_ref, od_ref, lj1, lj2, lj3, buf):
        def natural(ref, dil):
            return ref[...] if dil == 1 else _read_class_major(ref, buf, dil)

        ob = [natural(r, dil) for r, dil in zip((o1, o2, o3), DILATIONS)]
        la, lb, lc = [natural(r, dil) for r, dil in zip((l1, l2, l3), DILATIONS)]
        m = jnp.maximum(jnp.maximum(la, lb), lc)
        ea, eb, ec = jnp.exp(la - m), jnp.exp(lb - m), jnp.exp(lc - m)
        tot = ea + eb + ec
        o_dil = (ea / tot) * ob[0] + (eb / tot) * ob[1] + (ec / tot) * ob[2]
        od_ref[...] = o_dil
        lj = m + jnp.log(tot)
        _store_lane_blocks(buf, lj)
        for ref, dil in zip((lj1, lj2, lj3), DILATIONS):
            if dil == 1:
                ref[...] = lj
            else:
                _write_class_major(buf, ref, dil)
        y = _dot(of_ref[...].astype(BF16), w_ref[0:hw, :]) + _dot(o_dil.astype(BF16), w_ref[hw:2 * hw, :])
        y1_ref[...] = y
        x1_ref[...] = x_ref[...] + _rms_fwd(y, g_ref[...])

    half = _rows(tm, hw)
    by_dil = [_class_rows(tm, hw, dil) for dil in DILATIONS]
    return _call(
        body, name="mix_out", grid=(s_len // tm,),
        in_specs=[half] + by_dil + by_dil + [_resident(w_out.shape), _rows(tm, d), _resident((1, d))],
        out_specs=[_rows(tm, d), _rows(tm, d), half] + by_dil,
        out_shape=[_sds((s_len, d), F32), _sds((s_len, d), F32), _sds((s_len, hw), F32)]
        + [_sds((s_len // dil, dil * hw), F32) for dil in DILATIONS],
        scratch=[pltpu.VMEM((hw // PAIR, tm, PAIR), F32)],
    )(o_fox, *o_br, *lse_br, w_out, x, g_post)


def _mem_fwd(mem, g_mem, w_xk, w_xv):
    n_mem, d = mem.shape
    mw = w_xk.shape[1]

    def body(mem_ref, g_ref, wk_ref, wv_ref, hm_ref, k_ref, v_ref):
        hm = _rms_fwd(mem_ref[...], g_ref[...]).astype(BF16)
        hm_ref[...] = hm
        k_ref[...] = _dot(hm, wk_ref[...]).astype(BF16)
        v_ref[...] = _dot(hm, wv_ref[...]).astype(BF16)

    return pl.pallas_call(
        body, name="mem_fwd",
        out_shape=[_sds((n_mem, d), BF16), _sds((n_mem, mw), BF16), _sds((n_mem, mw), BF16)],
    )(mem, g_mem, w_xk, w_xv)


def _xattn_softmax(qa, k2):
    s = _dot_nt(qa, k2)
    m = jnp.max(s, axis=1, keepdims=True)
    e = jnp.exp(s - m)
    return e / jnp.sum(e, axis=1, keepdims=True)


def _xattn_fwd(x1, g_pre, w_xq, kx, vx, w_xo, g_post):
    s_len, d = x1.shape
    mw = w_xq.shape[1]
    n_mem = kx.shape[0]
    tm = ROW_TILE

    def body(x_ref, gp_ref, wq_ref, k_ref, v_ref, wo_ref, go_ref, x2_ref, y2_ref, h2_ref, q_ref, o_ref):
        x = x_ref[...]
        h = _rms_fwd(x, gp_ref[...]).astype(BF16)
        h2_ref[...] = h
        q = (_dot(h, wq_ref[...]) * QK_SCALE).astype(BF16)
        q_ref[...] = q
        in0 = _first_head_lanes()
        not0 = jnp.logical_not(in0)
        for pr in range(mw // PAIR):
            sl = slice(pr * PAIR, (pr + 1) * PAIR)
            q2, k2, v2 = q[:, sl], k_ref[:, sl], v_ref[:, sl]
            oa = [_dot(_xattn_softmax(_zero_other(mine, q2), k2).astype(BF16), v2) for mine in (in0, not0)]
            o_ref[:, sl] = _pick(in0, oa[0], oa[1]).astype(BF16)
        y = _dot(o_ref[...], wo_ref[...])
        y2_ref[...] = y
        x2_ref[...] = x + _rms_fwd(y, go_ref[...])

    return _call(
        body, name="xattn_fwd", grid=(s_len // tm,),
        in_specs=[_rows(tm, d), _resident((1, d)), _resident(w_xq.shape), _resident((n_mem, mw)), _resident((n_mem, mw)),
                  _resident(w_xo.shape), _resident((1, d))],
        out_specs=[_rows(tm, d), _rows(tm, d), _rows(tm, d), _rows(tm, mw), _rows(tm, mw)],
        out_shape=[_sds((s_len, d), F32), _sds((s_len, d), F32), _sds((s_len, d), BF16), _sds((s_len, mw), BF16),
                   _sds((s_len, mw), BF16)],
    )(x1, g_pre, w_xq, kx, vx, w_xo, g_post)


def _ffn_up(x2, g_pre, w_gate, w_up):
    s_len, d = x2.shape
    dff = w_gate.shape[1]
    tm = ROW_TILE

    def body(x_ref, g_ref, wg_ref, wu_ref, h_ref, a_ref, u_ref, z_ref):
        h = _rms_fwd(x_ref[...], g_ref[...]).astype(BF16)
        h_ref[...] = h
        a = _dot(h, wg_ref[...])
        u = _dot(h, wu_ref[...])
        a_ref[...] = a.astype(BF16)
        u_ref[...] = u.astype(BF16)
        z_ref[...] = ((a * jax.nn.sigmoid(a)) * u).astype(BF16)

    return _call(
        body, name="ffn_up", grid=(s_len // tm,),
        in_specs=[_rows(tm, d), _resident((1, d)), _resident(w_gate.shape), _resident(w_up.shape)],
        out_specs=[_rows(tm, d), _rows(tm, dff), _rows(tm, dff), _rows(tm, dff)],
        out_shape=[_sds((s_len, d), BF16)] + [_sds((s_len, dff), BF16)] * 3,
    )(x2, g_pre, w_gate, w_up)


def _ffn_down_loss(z, w_down, x2, g_post, target):
    s_len, d = x2.shape
    dff = z.shape[1]
    tm = ROW_TILE

    def body(z_ref, w_ref, x_ref, g_ref, t_ref, y_ref, dx_ref, sq_ref):
        @pl.when(pl.program_id(0) == 0)
        def _():
            sq_ref[...] = jnp.zeros_like(sq_ref)

        y = _dot(z_ref[...], w_ref[...])
        y_ref[...] = y
        err = (x_ref[...] + _rms_fwd(y, g_ref[...])) - t_ref[...]
        sq_ref[...] += jnp.sum(err * err, axis=0, keepdims=True)
        dx_ref[...] = err * (1.0 / d)

    return _call(
        body, name="ffn_down_loss", grid=(s_len // tm,),
        in_specs=[_rows(tm, dff), _resident(w_down.shape), _rows(tm, d), _resident((1, d)), _rows(tm, d)],
        out_specs=[_rows(tm, d), _rows(tm, d), _acc_out((1, d))],
        out_shape=[_sds((s_len, d), F32), _sds((s_len, d), F32), _sds((1, d), F32)],
    )(z, w_down, x2, g_post, target)


def _weight_grad(a, b, name):
    s_len, k = a.shape
    n = b.shape[1]
    ts = 512 if s_len % 512 == 0 else s_len
    tn = n
    while k * tn * 4 > 8 * 2 ** 20 and tn % 256 == 0:
        tn //= 2

    def body(a_ref, b_ref, o_ref):
        @pl.when(pl.program_id(1) == 0)
        def _():
            o_ref[...] = jnp.zeros_like(o_ref)

        o_ref[...] += _dot_tn(a_ref[...].astype(BF16), b_ref[...].astype(BF16))

    return pl.pallas_call(
        body, name=name, grid=(n // tn, s_len // ts),
        in_specs=[pl.BlockSpec((ts, k), lambda j, i: (i, 0)), pl.BlockSpec((ts, tn), lambda j, i: (i, j))],
        out_specs=pl.BlockSpec((k, tn), lambda j, i: (0, j)),
        out_shape=_sds((k, n), F32),
        compiler_params=pltpu.CompilerParams(dimension_semantics=("arbitrary",) * 2, vmem_limit_bytes=V7X_VMEM_LIMIT_BYTES),
    )(a, b)


def _ffn_bwd_a(dx3, y3, g_post, w_down_t, a, u):
    s_len, d = dx3.shape
    dff = a.shape[1]
    tm = ROW_TILE

    def body(dx_ref, y_ref, g_ref, w_ref, a_ref, u_ref, dy_ref, da_ref, du_ref, dg_ref):
        @pl.when(pl.program_id(0) == 0)
        def _():
            dg_ref[...] = jnp.zeros_like(dg_ref)

        dy, dg = _rms_bwd(y_ref[...], g_ref[...], dx_ref[...])
        dg_ref[...] += dg
        dyb = dy.astype(BF16)
        dy_ref[...] = dyb
        dz = _dot(dyb, w_ref[...])
        av = a_ref[...].astype(F32)
        uv = u_ref[...].astype(F32)
        sg = jax.nn.sigmoid(av)
        da_ref[...] = (dz * uv * (sg * (1.0 + av * (1.0 - sg)))).astype(BF16)
        du_ref[...] = (dz * (av * sg)).astype(BF16)

    return _call(
        body, name="ffn_bwd_a", grid=(s_len // tm,),
        in_specs=[_rows(tm, d), _rows(tm, d), _resident((1, d)), _resident(w_down_t.shape), _rows(tm, dff), _rows(tm, dff)],
        out_specs=[_rows(tm, d), _rows(tm, dff), _rows(tm, dff), _acc_out((1, d))],
        out_shape=[_sds((s_len, d), BF16), _sds((s_len, dff), BF16), _sds((s_len, dff), BF16), _sds((1, d), F32)],
    )(dx3, y3, g_post, w_down_t, a, u)


def _ffn_bwd_b(da, du, w_gate_t, w_up_t, dx3, x2, g_pre):
    s_len, d = x2.shape
    dff = da.shape[1]
    tm = ROW_TILE

    def body(da_ref, du_ref, wg_ref, wu_ref, dx_ref, x_ref, g_ref, o_ref, dg_ref):
        @pl.when(pl.program_id(0) == 0)
        def _():
            dg_ref[...] = jnp.zeros_like(dg_ref)

        dh = _dot(da_ref[...], wg_ref[...]) + _dot(du_ref[...], wu_ref[...])
        dx, dg = _rms_bwd(x_ref[...], g_ref[...], dh)
        dg_ref[...] += dg
        o_ref[...] = dx_ref[...] + dx

    return _call(
        body, name="ffn_bwd_b", grid=(s_len // tm,),
        in_specs=[_rows(tm, dff), _rows(tm, dff), _resident(w_gate_t.shape), _resident(w_up_t.shape), _rows(tm, d),
                  _rows(tm, d), _resident((1, d))],
        out_specs=[_rows(tm, d), _acc_out((1, d))],
        out_shape=[_sds((s_len, d), F32), _sds((1, d), F32)],
    )(da, du, w_gate_t, w_up_t, dx3, x2, g_pre)


def _xattn_bwd(dx2, y2, g_post, w_xo_t, q, kx, vx, w_xq_t, x1, g_pre):
    s_len, d = x1.shape
    mw = q.shape[1]
    n_mem = kx.shape[0]
    tm = ROW_TILE

    def body(dx_ref, y_ref, go_ref, wo_ref, q_ref, k_ref, v_ref, wq_ref, x_ref, gp_ref,
             dx1_ref, dy_ref, dq_ref, dk_ref, dv_ref, dgo_ref, dgp_ref):
        @pl.when(pl.program_id(0) == 0)
        def _():
            dk_ref[...] = jnp.zeros_like(dk_ref)
            dv_ref[...] = jnp.zeros_like(dv_ref)
            dgo_ref[...] = jnp.zeros_like(dgo_ref)
            dgp_ref[...] = jnp.zeros_like(dgp_ref)

        dxin = dx_ref[...]
        dy, dgo = _rms_bwd(y_ref[...], go_ref[...], dxin)
        dgo_ref[...] += dgo
        dyb = dy.astype(BF16)
        dy_ref[...] = dyb
        do = _dot(dyb, wo_ref[...]).astype(BF16)
        in0 = _first_head_lanes()
        not0 = jnp.logical_not(in0)
        for pr in range(mw // PAIR):
            sl = slice(pr * PAIR, (pr + 1) * PAIR)
            q2, k2, v2, do2 = q_ref[:, sl], k_ref[:, sl], v_ref[:, sl], do[:, sl]
            dqs = []
            dk2 = jnp.zeros((n_mem, PAIR), F32)
            dv2 = jnp.zeros((n_mem, PAIR), F32)
            for mine in (in0, not0):
                qa = _zero_other(mine, q2)
                doa = _zero_other(mine, do2)
                p = _xattn_softmax(qa, k2)
                dp = _dot_nt(doa, v2)
                ds = (p * (dp - jnp.sum(p * dp, axis=1, keepdims=True))).astype(BF16)
                dqs.append(_dot(ds, k2))
                dk2 = dk2 + _dot_tn(ds, qa)
                dv2 = dv2 + _dot_tn(p.astype(BF16), doa)
            dq_ref[:, sl] = (_pick(in0, dqs[0], dqs[1]) * QK_SCALE).astype(BF16)
            dk_ref[:, sl] += dk2
            dv_ref[:, sl] += dv2
        dh = _dot(dq_ref[...], wq_ref[...])
        dx, dgp = _rms_bwd(x_ref[...], gp_ref[...], dh)
        dgp_ref[...] += dgp
        dx1_ref[...] = dxin + dx

    return _call(
        body, name="xattn_bwd", grid=(s_len // tm,),
        in_specs=[_rows(tm, d), _rows(tm, d), _resident((1, d)), _resident(w_xo_t.shape), _rows(tm, mw),
                  _resident((n_mem, mw)), _resident((n_mem, mw)), _resident(w_xq_t.shape), _rows(tm, d), _resident((1, d))],
        out_specs=[_rows(tm, d), _rows(tm, d), _rows(tm, mw), _acc_out((n_mem, mw)), _acc_out((n_mem, mw)),
                   _acc_out((1, d)), _acc_out((1, d))],
        out_shape=[_sds((s_len, d), F32), _sds((s_len, d), BF16), _sds((s_len, mw), BF16), _sds((n_mem, mw), F32),
                   _sds((n_mem, mw), F32), _sds((1, d), F32), _sds((1, d), F32)],
    )(dx2, y2, g_post, w_xo_t, q, kx, vx, w_xq_t, x1, g_pre)


def _mem_bwd(dk, dv, w_xk_t, w_xv_t, hm, mem, g_mem):
    n_mem, d = mem.shape
    mw = dk.shape[1]

    def body(dk_ref, dv_ref, wk_ref, wv_ref, hm_ref, mem_ref, g_ref, dwk_ref, dwv_ref, dg_ref):
        dkb = dk_ref[...].astype(BF16)
        dvb = dv_ref[...].astype(BF16)
        dhm = _dot(dkb, wk_ref[...]) + _dot(dvb, wv_ref[...])
        _, dg = _rms_bwd(mem_ref[...], g_ref[...], dhm)
        dg_ref[...] = dg
        dwk_ref[...] = _dot_tn(hm_ref[...], dkb)
        dwv_ref[...] = _dot_tn(hm_ref[...], dvb)

    return pl.pallas_call(
        body, name="mem_bwd",
        out_shape=[_sds((d, mw), F32), _sds((d, mw), F32), _sds((1, d), F32)],
    )(dk, dv, w_xk_t, w_xv_t, hm, mem, g_mem)


def _mix_out_bwd(dx1, y1, g_post, w_out_t, o_fox, o_dil):
    s_len, d = dx1.shape
    hw = HEAD_WIDTH
    tm = ROW_TILE
    ones = _head_block_ones().astype(BF16)

    def body(dx_ref, y_ref, g_ref, w_ref, of_ref, od_ref, ones_ref, dy_ref, dof_ref, dlf_ref, dg_ref,
             dod1, dod2, dod3, dld1, dld2, dld3, buf):
        @pl.when(pl.program_id(0) == 0)
        def _():
            dg_ref[...] = jnp.zeros_like(dg_ref)

        dy, dg = _rms_bwd(y_ref[...], g_ref[...], dx_ref[...])
        dg_ref[...] += dg
        dyb = dy.astype(BF16)
        dy_ref[...] = dyb
        do = _dot(dyb, w_ref[...])
        def head_sums(x):
            hi = x.astype(BF16)
            lo = (x - hi.astype(F32)).astype(BF16)
            return _dot(hi, ones_ref[...]) + _dot(lo, ones_ref[...])

        dof_ref[...] = do[:, 0:hw].astype(BF16)
        dlf_ref[...] = head_sums(do[:, 0:hw] * of_ref[...])
        do_dil = do[:, hw:2 * hw]
        dl_dil = head_sums(do_dil * od_ref[...])
        for val, refs in ((do_dil, (dod1, dod2, dod3)), (dl_dil, (dld1, dld2, dld3))):
            _store_lane_blocks(buf, val)
            for ref, dil in zip(refs, DILATIONS):
                if dil == 1:
                    ref[...] = val.astype(ref.dtype)
                else:
                    _write_class_major(buf, ref, dil)

    half = _rows(tm, hw)
    by_dil = [_class_rows(tm, hw, dil) for dil in DILATIONS]
    outs = _call(
        body, name="mix_out_bwd", grid=(s_len // tm,),
        in_specs=[_rows(tm, d), _rows(tm, d), _resident((1, d)), _resident(w_out_t.shape), half, half, _resident((hw, hw))],
        out_specs=[_rows(tm, d), half, half, _acc_out((1, d))] + by_dil + by_dil,
        out_shape=[_sds((s_len, d), BF16), _sds((s_len, hw), BF16), _sds((s_len, hw), F32), _sds((1, d), F32)]
        + [_sds((s_len // dil, dil * hw), BF16) for dil in DILATIONS]
        + [_sds((s_len // dil, dil * hw), F32) for dil in DILATIONS],
        scratch=[pltpu.VMEM((hw // PAIR, tm, PAIR), F32)],
    )(dx1, y1, g_post, w_out_t, o_fox, o_dil, ones)
    return outs[0], outs[1], outs[2], outs[3], outs[4:7], outs[7:10]


def _fox_bwd_prep(aq, lse, delta):
    s_len, hw = aq.shape
    tm = ROW_TILE

    def body(aq_ref, lse_ref, dl_ref, aql_ref, ad_ref):
        jj = _lane_in_head(hw)
        l3 = _split3(_swap_halves(lse_ref[...]))
        aql_ref[...] = _place3(jj, 6, [-p for p in l3], aq_ref[...])
        d3 = _split3(_swap_halves(dl_ref[...]))
        ad_ref[...] = _place3(jj, 0, [-p for p in d3], jnp.zeros((tm, hw), BF16))

    half = _rows(tm, hw)
    return _call(
        body, name="fox_bwd_prep", grid=(s_len // tm,),
        in_specs=[half, half, half], out_specs=[half, half],
        out_shape=[_sds((s_len, hw), BF16), _sds((s_len, hw), BF16)],
    )(aq, lse, delta)


def _ones_on_first3(shape):
    jj = lax.broadcasted_iota(jnp.int32, shape, 1) % HEAD_DIM
    return jnp.where(jj < 3, 1.0, 0.0).astype(BF16)


def _fox_bwd(span, fqkv, do, aql, ak, ad, stats):
    s_len = fqkv.shape[0]
    bq = bk = min(ATT_BLOCK, s_len)
    nq = nk = s_len // bq

    def body(st_ref, q_ref, k_ref, v_ref, do_ref, aql_ref, ak_ref, ad_ref, dq_ref, rs_ref, dk_ref, dv_ref, dc_ref,
             ka_ref, va_ref, kone_ref, r_ref, dvacc_ref, dqacc_ref):
        pair, kj, ahead = pl.program_id(0), pl.program_id(1), pl.program_id(2)
        valid = kj + ahead < nq
        qi = jnp.minimum(kj + ahead, nq - 1)
        in0 = _first_head_lanes()
        not0 = jnp.logical_not(in0)
        heads = ((0, in0), (1, not0))
        rows = pl.ds(pl.multiple_of(qi * bq, bq), bq)

        @pl.when(ahead == 0)
        def _():
            k2, v2, a2 = k_ref[...], v_ref[...], ak_ref[...]
            one = jnp.ones_like(k2)
            one3 = _ones_on_first3(v2.shape)
            for a, mine in heads:
                ka_ref[a] = jnp.where(mine, k2, a2)
                va_ref[a] = jnp.where(mine, v2, one3)
                kone_ref[a] = jnp.where(mine, k2, one)
            r_ref[...] = jnp.zeros_like(r_ref)
            dvacc_ref[...] = jnp.zeros_like(dvacc_ref)

        @pl.when(jnp.logical_and(valid, jnp.logical_or(kj == 0, ahead == span - 1)))
        def _():
            for a, _ in heads:
                dqacc_ref[a, rows, :] = jnp.zeros((bq, PAIR), F32)

        def head_step(a, mine, masked):
            q2 = q_ref[...]
            doa = jnp.where(mine, do_ref[...], ad_ref[...])
            s = _dot_nt(jnp.where(mine, q2, aql_ref[...]), ka_ref[a])
            if masked:
                s = jnp.where(_lower_triangle(bq), s, NEG)
            p = jnp.exp(s)
            ds = (p * _dot_nt(doa, va_ref[a])).astype(BF16)
            dvacc_ref[a] += _dot_tn(p.astype(BF16), doa)
            r_ref[a] += _dot_tn(ds, jnp.where(mine, q2, jnp.ones_like(q2)))
            dqacc_ref[a, rows, :] += _dot(ds, kone_ref[a])

        for a, mine in heads:
            @pl.when(jnp.logical_and(jnp.logical_and(valid, ahead > 0), _head_needed(st_ref, 2 * pair + a, qi, kj)))
            def _():
                head_step(a, mine, False)

        @pl.when(ahead == 0)
        def _():
            for a, mine in heads:
                head_step(a, mine, True)
            acc0, acc1 = dqacc_ref[0, rows, :], dqacc_ref[1, rows, :]
            dq_ref[...] = (_pick(in0, acc0, acc1) * QK_SCALE).astype(BF16)
            rs_ref[...] = pltpu.roll(_pick(in0, acc1, acc0), HEAD_DIM, 1)

        @pl.when(ahead == span - 1)
        def _():
            dk_ref[...] = _pick(in0, r_ref[0], r_ref[1]).astype(BF16)
            dv_ref[...] = _pick(in0, dvacc_ref[0], dvacc_ref[1]).astype(BF16)
            dc_ref[...] = -pltpu.roll(_pick(in0, r_ref[1], r_ref[0]), HEAD_DIM, 1)

    blk = lambda n: (n, PAIR)
    kvmap = lambda p, j, t, st: (j, p)

    def qmap(p, j, t, st):
        last = jnp.clip(st[FOX_LAST_QUERY_ROW + p, j].astype(jnp.int32), j, nq - 1)
        return (jnp.minimum(j + t, last), p)

    return pl.pallas_call(
        body, name=f"fox_bwd_{span}",
        grid_spec=pltpu.PrefetchScalarGridSpec(
            num_scalar_prefetch=1, grid=(N_PAIRS, nk, span),
            in_specs=[
                pl.BlockSpec(blk(bq), qmap),
                pl.BlockSpec(blk(bk), lambda p, j, t, st: (j, N_PAIRS + p)),
                pl.BlockSpec(blk(bk), lambda p, j, t, st: (j, 2 * N_PAIRS + p)),
                pl.BlockSpec(blk(bq), qmap), pl.BlockSpec(blk(bq), qmap),
                pl.BlockSpec(blk(bk), kvmap),
                pl.BlockSpec(blk(bq), qmap),
            ],
            out_specs=[pl.BlockSpec(blk(bk), kvmap)] * 5,
            scratch_shapes=[pltpu.VMEM((2, bk, PAIR), BF16), pltpu.VMEM((2, bk, PAIR), BF16), pltpu.VMEM((2, bk, PAIR), BF16),
                            pltpu.VMEM((2, bk, PAIR), F32), pltpu.VMEM((2, bk, PAIR), F32), pltpu.VMEM((2, s_len, PAIR), F32)]),
        out_shape=[_sds((s_len, HEAD_WIDTH), BF16), _sds((s_len, HEAD_WIDTH), F32), _sds((s_len, HEAD_WIDTH), BF16),
                   _sds((s_len, HEAD_WIDTH), BF16), _sds((s_len, HEAD_WIDTH), F32)],
        compiler_params=pltpu.CompilerParams(dimension_semantics=("arbitrary",) * 3, vmem_limit_bytes=V7X_VMEM_LIMIT_BYTES),
    )(stats, fqkv, fqkv, fqkv, do, aql, ak, ad)


def _gate_bwd(rs, dc, gx, b_exp):
    s_len, hw = gx.shape
    t = min(SCAN_TILE, s_len)
    nt = s_len // t
    tri = jnp.asarray(np.triu(np.ones((t, t), np.float32))).astype(BF16)

    def body(rs_ref, dc_ref, gx_ref, b_ref, tri_ref, dgx_ref, db_ref, carry):
        @pl.when(pl.program_id(0) == 0)
        def _():
            carry[...] = jnp.zeros_like(carry)
            db_ref[...] = jnp.zeros_like(db_ref)

        dlf = _tri_dot(tri_ref[...], rs_ref[...] + dc_ref[...]) + carry[...]
        carry[...] = dlf[0:1, :]
        dgate = dlf * jax.nn.sigmoid(-(gx_ref[...] + b_ref[...]))
        db_ref[...] += jnp.sum(dgate, axis=0, keepdims=True)
        lane = lax.broadcasted_iota(jnp.int32, (1, hw), 1)
        dgx_ref[...] = jnp.where(lane % HEAD_DIM == 0, dgate, 0.0).astype(BF16)

    rev = lambda i: (nt - 1 - i, 0)
    return _call(
        body, name="gate_bwd", grid=(nt,),
        in_specs=[pl.BlockSpec((t, hw), rev)] * 3 + [_resident((1, hw)), _resident((t, t))],
        out_specs=[pl.BlockSpec((t, hw), rev), _acc_out((1, hw))],
        out_shape=[_sds((s_len, hw), BF16), _sds((1, hw), F32)],
        scratch=[pltpu.VMEM((1, hw), F32)],
    )(rs, dc, gx, b_exp, tri)


def _dil_bwd(qkv_v, do_v, lj_v, dl_v, bias, branch, riding=None):
    dil = DILATIONS[branch]
    w = DIL_BLOCK
    hw = HEAD_WIDTH
    length = qkv_v.shape[0]
    nb = length // w

    def body(q0_ref, q1_ref, kp_ref, kc_ref, vp_ref, vc_ref, do0_ref, do1_ref, l0_ref, l1_ref, d0_ref, d1_ref, b_ref, *rest):
        if riding is None:
            dq_ref, dk_ref, dv_ref, dsum_ref = rest
        else:
            g_ref, dq_ref, dk_ref, dv_ref, dsum_ref, land_ref, send_sems, recv_sems, local_sem = rest
            exchange = (g_ref, land_ref, send_sems, recv_sems, local_sem)
        r, n = pl.program_id(0), pl.program_id(1)
        in0 = _first_head_lanes()
        not0 = jnp.logical_not(in0)
        first = n == 0
        last = n == nb - 1

        @pl.when(jnp.logical_and(r == 0, n == 0))
        def _():
            dsum_ref[...] = jnp.zeros_like(dsum_ref)
            if riding is not None:
                _exchange_start(*exchange)

        pairs = [slice(pr * PAIR, (pr + 1) * PAIR) for pr in range(N_PAIRS)]

        def both_heads(ref, sl):
            v = ref[:, sl]
            return jnp.concatenate([_zero_other(in0, v), _zero_other(not0, v)], axis=0)

        def head_columns(ref):
            return jnp.concatenate([ref[:, h * HEAD_DIM:h * HEAD_DIM + 1] for h in range(N_HEADS)], axis=0)

        qq0 = [both_heads(q0_ref, sl) for sl in pairs]
        qq1 = [both_heads(q1_ref, sl) for sl in pairs]
        dd0 = [both_heads(do0_ref, sl) for sl in pairs]
        dd1 = [both_heads(do1_ref, sl) for sl in pairs]
        stack = lambda tiles: jnp.concatenate(tiles, axis=0)
        s_a = stack([_dot_nt(qq0[i], kp_ref[:, sl]) for i, sl in enumerate(pairs)])
        s_b = stack([_dot_nt(qq0[i], kc_ref[:, sl]) for i, sl in enumerate(pairs)])
        s_c = stack([_dot_nt(qq1[i], kc_ref[:, sl]) for i, sl in enumerate(pairs)])
        dp_a = stack([_dot_nt(dd0[i], vp_ref[:, sl]) for i, sl in enumerate(pairs)])
        dp_b = stack([_dot_nt(dd0[i], vc_ref[:, sl]) for i, sl in enumerate(pairs)])
        dp_c = stack([_dot_nt(dd1[i], vc_ref[:, sl]) for i, sl in enumerate(pairs)])
        bias2 = b_ref[...].reshape(N_HEADS * w, 2 * w)
        b_prev, b_cur = bias2[:, 0:w], bias2[:, w:2 * w]
        lse0, lse1 = head_columns(l0_ref), head_columns(l1_ref)
        dl0, dl1 = head_columns(d0_ref), head_columns(d1_ref)
        p_a = jnp.exp(jnp.where(first, NEG, s_a + b_prev) - lse0)
        p_b = jnp.exp((s_b + b_cur) - lse0)
        p_c = jnp.exp(jnp.where(last, NEG, s_c + b_prev) - lse1)
        ds_a = p_a * (dp_a - dl0)
        ds_b = p_b * (dp_b - dl0)
        ds_c = p_c * (dp_c - dl1)
        dsum_ref[...] += jnp.concatenate([ds_a, ds_b], axis=1).reshape(N_HEADS, w, 2 * w)
        ds_a, ds_b, ds_c = ds_a.astype(BF16), ds_b.astype(BF16), ds_c.astype(BF16)
        p_b, p_c = p_b.astype(BF16), p_c.astype(BF16)
        for i, sl in enumerate(pairs):
            rows = slice(2 * i * w, (2 * i + 2) * w)
            dq2 = _dot(ds_a[rows], kp_ref[:, sl]) + _dot(ds_b[rows], kc_ref[:, sl])
            dq_ref[:, sl] = _pick(in0, dq2[:w], dq2[w:]).astype(BF16)
            dk_ref[:, sl] = (_dot_tn(ds_b[rows], qq0[i]) + _dot_tn(ds_c[rows], qq1[i])).astype(BF16)
            dv_ref[:, sl] = (_dot_tn(p_b[rows], dd0[i]) + _dot_tn(p_c[rows], dd1[i])).astype(BF16)

        if riding is not None:
            @pl.when(jnp.logical_and(r == dil - 1, n == nb - 1))
            def _():
                _exchange_wait(*exchange)

    prev = lambda n: jnp.maximum(n - 1, 0)
    nxt = lambda n: jnp.minimum(n + 1, nb - 1)
    blk = (w, hw)
    anywhere = pl.BlockSpec(memory_space=pl.ANY)
    extra_in, extra_out, extra_shape, extra_scratch, extra_args = [], [], [], [], []
    if riding is not None:
        extra_in, extra_out, extra_shape = [anywhere], [anywhere], [_sds(riding.shape, riding.dtype)]
        extra_scratch, extra_args = list(EXCHANGE_SEMAPHORES), [riding]
    outs = pl.pallas_call(
        body, name=f"dil_bwd_{dil}", grid=(dil, nb),
        in_specs=[
            pl.BlockSpec(blk, lambda r, n: (n, 3 * r)),
            pl.BlockSpec(blk, lambda r, n: (nxt(n), 3 * r)),
            pl.BlockSpec(blk, lambda r, n: (prev(n), 3 * r + 1)),
            pl.BlockSpec(blk, lambda r, n: (n, 3 * r + 1)),
            pl.BlockSpec(blk, lambda r, n: (prev(n), 3 * r + 2)),
            pl.BlockSpec(blk, lambda r, n: (n, 3 * r + 2)),
            pl.BlockSpec(blk, lambda r, n: (n, r)),
            pl.BlockSpec(blk, lambda r, n: (nxt(n), r)),
            pl.BlockSpec(blk, lambda r, n: (n, r)),
            pl.BlockSpec(blk, lambda r, n: (nxt(n), r)),
            pl.BlockSpec(blk, lambda r, n: (n, r)),
            pl.BlockSpec(blk, lambda r, n: (nxt(n), r)),
            pl.BlockSpec((None, N_HEADS, w, 2 * w), lambda r, n: (branch, 0, 0, 0)),
        ] + extra_in,
        out_specs=[pl.BlockSpec(blk, lambda r, n: (n, r))] * 3 + [pl.BlockSpec((N_HEADS, w, 2 * w), lambda r, n: (0, 0, 0))]
        + extra_out,
        out_shape=[_sds((length, dil * hw), BF16)] * 3 + [_sds((N_HEADS, w, 2 * w), F32)] + extra_shape,
        scratch_shapes=extra_scratch,
        compiler_params=pltpu.CompilerParams(dimension_semantics=("arbitrary",) * 2, vmem_limit_bytes=V7X_VMEM_LIMIT_BYTES),
    )(qkv_v, qkv_v, qkv_v, qkv_v, qkv_v, qkv_v, do_v, do_v, lj_v, lj_v, dl_v, dl_v, bias, *extra_args)
    return list(outs)


def _rel_bias_grad(dsum, buckets):
    w = DIL_BLOCK

    def body(ds_ref, bk_ref, o_ref):
        row = lax.broadcasted_iota(jnp.int32, (N_BUCKETS, PAIR), 0)
        lane = lax.broadcasted_iota(jnp.int32, (N_BUCKETS, PAIR), 1)

        def per_bucket(b, acc):
            for p in range(len(DILATIONS)):
                hit = bk_ref[p] == b
                for h in range(N_HEADS):
                    part = jnp.where(hit, ds_ref[p, h], 0.0)
                    tot = jnp.sum(jnp.sum(part, axis=1, keepdims=True), axis=0, keepdims=True)
                    acc = acc + jnp.where(jnp.logical_and(row == b, lane == h), tot, 0.0)
            return acc

        o_ref[...] = lax.fori_loop(0, N_BUCKETS, per_bucket, jnp.zeros((N_BUCKETS, PAIR), F32))

    return pl.pallas_call(body, name="rel_bias_grad", out_shape=_sds((N_BUCKETS, PAIR), F32))(dsum, buckets)


def _in_proj_bwd(dfq, dfk, dfv, dgx, ddq, ddk, ddv, w_in_t, dx1, x, g_pre):
    s_len, d = x.shape
    hw = HEAD_WIDTH
    tm = ROW_TILE

    def body(fq, fk, fv, gx, q1, q2, q3, k1, k2, k3, v1, v2, v3, w_ref, dx_ref, x_ref, g_ref, o_ref, dp_ref, dg_ref, buf):
        @pl.when(pl.program_id(0) == 0)
        def _():
            dg_ref[...] = jnp.zeros_like(dg_ref)

        def branch_sum(refs):
            a, b, c = [r[...].astype(F32) if dil == 1 else _read_class_major(r, buf, dil) for r, dil in zip(refs, DILATIONS)]
            return (a + b) + c

        dp_ref[:, 0:hw] = fq[...]
        dp_ref[:, hw:2 * hw] = fk[...]
        dp_ref[:, 2 * hw:3 * hw] = fv[...]
        dp_ref[:, 3 * hw:4 * hw] = gx[...]
        dp_ref[:, 4 * hw:5 * hw] = (branch_sum((q1, q2, q3)) * QK_SCALE).astype(BF16)
        dp_ref[:, 5 * hw:6 * hw] = branch_sum((k1, k2, k3)).astype(BF16)
        dp_ref[:, 6 * hw:7 * hw] = branch_sum((v1, v2, v3)).astype(BF16)
        dh = _dot(dp_ref[...], w_ref[...])
        dx, dg = _rms_bwd(x_ref[...], g_ref[...], dh)
        dg_ref[...] += dg
        o_ref[...] = dx_ref[...] + dx

    half = _rows(tm, hw)
    by_dil = [_class_rows(tm, hw, dil) for dil in DILATIONS]
    return _call(
        body, name="in_proj_bwd", grid=(s_len // tm,),
        in_specs=[half] * 4 + by_dil * 3 + [_resident(w_in_t.shape), _rows(tm, d), _rows(tm, d), _resident((1, d))],
        out_specs=[_rows(tm, d), _rows(tm, 7 * hw), _acc_out((1, d))],
        out_shape=[_sds((s_len, d), F32), _sds((s_len, 7 * hw), BF16), _sds((1, d), F32)],
        scratch=[pltpu.VMEM((hw // PAIR, tm, PAIR), F32)],
    )(dfq, dfk, dfv, dgx, *ddq, *ddk, *ddv, w_in_t, dx1, x, g_pre)


def _expand_w_in(w_in):
    hw = HEAD_WIDTH
    gate = jnp.repeat(w_in[:, 3 * hw:3 * hw + N_HEADS], HEAD_DIM, axis=1)
    return jnp.concatenate([w_in[:, :3 * hw], gate, w_in[:, 3 * hw + N_HEADS:]], axis=1)


def _local_step(x, mem, target, g, w_bf, pack_early=None, ride_gather=None):
    hw = HEAD_WIDTH
    w_in_e = _expand_w_in(w_bf["w_in"])
    b_exp = jnp.repeat(g["b_f"], HEAD_DIM, axis=1)
    buckets = jnp.asarray(_dil_buckets())

    n_dil = len(DILATIONS)
    if ride_gather is None:
        h1, fqkv, gx, *dqkv = _in_proj(x, g["g_mix_pre"], w_in_e)
    else:
        h1, fqkv, gx, *dqkv, gathered = _in_proj(x, g["g_mix_pre"], w_in_e, riding=ride_gather[0])
        w_bf = {**w_bf, **ride_gather[1](gathered)}
    aq, ak, c = _gate_scan(gx, b_exp)
    stats = _fox_block_stats(fqkv, c)
    spans = _fox_spans(x.shape[0])
    span_idx = _fox_span_index(stats, x.shape[0])
    o_fox, lse_fox = lax.switch(span_idx, [functools.partial(_fox_fwd, s) for s in spans], fqkv, aq, ak, stats)
    bias = _dil_bias(g["rel_bias"], buckets)
    branches = [_dil_fwd(dqkv[p], bias, p) for p in range(n_dil)]
    x1, y1, o_dil, *lj = _mix_out(o_fox, [b[0] for b in branches], [b[1] for b in branches], w_bf["w_out"], x, g["g_mix_post"])
    hm, kx, vx = _mem_fwd(mem, g["g_mem"], w_bf["w_xk"], w_bf["w_xv"])
    x2, y2, h2, qx, ox = _xattn_fwd(x1, g["g_xattn_pre"], w_bf["w_xq"], kx, vx, w_bf["w_xo"], g["g_xattn_post"])
    h3, a, u, z = _ffn_up(x2, g["g_ffn_pre"], w_bf["w_gate"], w_bf["w_up"])
    y3, dx3, sq = _ffn_down_loss(z, w_bf["w_down"], x2, g["g_ffn_post"], target)

    grads = {}
    dy3, da, du, grads["g_ffn_post"] = _ffn_bwd_a(dx3, y3, g["g_ffn_post"], w_bf["w_down"].T, a, u)
    dx2, grads["g_ffn_pre"] = _ffn_bwd_b(da, du, w_bf["w_gate"].T, w_bf["w_up"].T, dx3, x2, g["g_ffn_pre"])
    grads["w_down"] = _weight_grad(z, dy3, "dw_down")
    grads["w_gate"] = _weight_grad(h3, da, "dw_gate")
    grads["w_up"] = _weight_grad(h3, du, "dw_up")
    dx1, dy2, dqx, dkx, dvx, grads["g_xattn_post"], grads["g_xattn_pre"] = _xattn_bwd(
        dx2, y2, g["g_xattn_post"], w_bf["w_xo"].T, qx, kx, vx, w_bf["w_xq"].T, x1, g["g_xattn_pre"])
    grads["w_xo"] = _weight_grad(ox, dy2, "dw_xo")
    grads["w_xq"] = _weight_grad(h2, dqx, "dw_xq")
    grads["w_xk"], grads["w_xv"], grads["g_mem"] = _mem_bwd(dkx, dvx, w_bf["w_xk"].T, w_bf["w_xv"].T, hm, mem, g["g_mem"])
    dy1, do_fox, delta_fox, grads["g_mix_post"], do_dil, delta_dil = _mix_out_bwd(
        dx1, y1, g["g_mix_post"], w_bf["w_out"].T, o_fox, o_dil)
    grads["w_out"] = jnp.concatenate([_weight_grad(o_fox, dy1, "dw_out_fox"), _weight_grad(o_dil, dy1, "dw_out_dil")], axis=0)
    aql, ad = _fox_bwd_prep(aq, lse_fox, delta_fox)
    dfq, rs, dfk, dfv, dc = lax.switch(
        span_idx, [functools.partial(_fox_bwd, s) for s in spans], fqkv, do_fox, aql, ak, ad, stats)
    dgx, db = _gate_bwd(rs, dc, gx, b_exp)
    grads["b_f"] = db[:, ::HEAD_DIM]
    riding = None if pack_early is None else pack_early(grads)
    dil = [_dil_bwd(dqkv[p], do_dil[p], lj[p], delta_dil[p], bias, p, riding if p == 0 else None) for p in range(n_dil)]
    landed_early = dil[0][4] if riding is not None else None
    grads["rel_bias"] = _rel_bias_grad(jnp.stack([t[3] for t in dil]), buckets)[:, :N_HEADS]
    grad_x, dproj, grads["g_mix_pre"] = _in_proj_bwd(
        dfq, dfk, dfv, dgx, [t[0] for t in dil], [t[1] for t in dil], [t[2] for t in dil], w_in_e.T, dx1, x, g["g_mix_pre"])
    dw_in_e = _weight_grad(h1, dproj, "dw_in")
    grads["w_in"] = jnp.concatenate(
        [dw_in_e[:, :3 * hw], dw_in_e[:, 3 * hw:4 * hw:HEAD_DIM], dw_in_e[:, 4 * hw:]], axis=1)
    return sq, grad_x, grads, landed_early


MESH = pl.DeviceIdType.MESH


def _my_place():
    return lax.axis_index("x"), lax.axis_index("y"), lax.axis_index("c")


def _gather_copies(x_ref, out_ref, send_sems, recv_sems, local_sem):
    mx, my, mc = _my_place()
    me, sibling = (mx, my, mc), (mx, my, 1 - mc)
    chips = [(1 - mx, my), (mx, 1 - my), (1 - mx, 1 - my)]

    def slot(px, py, pc):
        return out_ref.at[4 * px + 2 * py + pc]

    def copy(k, block, to, src=None):
        return pltpu.make_async_remote_copy(
            src_ref=slot(*block) if src is None else src, dst_ref=slot(*block),
            send_sem=send_sems.at[k], recv_sem=recv_sems.at[k], device_id=to, device_id_type=MESH)

    mine = pltpu.make_async_copy(x_ref, slot(*me), local_sem)
    first = [copy(0, me, sibling, src=x_ref)] + [copy(1 + j, me, (*chip, mc), src=x_ref) for j, chip in enumerate(chips)]
    passed = [copy(4 + j, (*chip, mc), sibling) for j, chip in enumerate(chips)]
    over_ici = [copy(1 + j, (*chip, mc), me) for j, chip in enumerate(chips)]
    from_sibling = [copy(0, sibling, me)] + [copy(4 + j, (*chip, 1 - mc), me) for j, chip in enumerate(chips)]
    return mine, first, passed, over_ici, from_sibling


def _gather_start(*refs):
    mine, first, _, _, _ = _gather_copies(*refs)
    mine.start()
    for cp in first:
        cp.start()


def _gather_forward(*refs):
    _, _, passed, over_ici, _ = _gather_copies(*refs)
    for arrival, forward in zip(over_ici, passed):
        arrival.wait_recv()
        forward.start()


def _gather_finish(*refs):
    mine, first, passed, _, from_sibling = _gather_copies(*refs)
    for arrival in from_sibling:
        arrival.wait_recv()
    for cp in first + passed:
        cp.wait_send()
    mine.wait()


def _all_gather(x, name):
    rows, lanes = x.shape

    def body(x_ref, out_ref, send_sems, recv_sems, local_sem):
        _gather_start(x_ref, out_ref, send_sems, recv_sems, local_sem)
        _gather_forward(x_ref, out_ref, send_sems, recv_sems, local_sem)
        _gather_finish(x_ref, out_ref, send_sems, recv_sems, local_sem)

    return pl.pallas_call(
        body, name=name, out_shape=_sds((N_DEV, rows, lanes), x.dtype),
        in_specs=[pl.BlockSpec(memory_space=pl.ANY)], out_specs=pl.BlockSpec(memory_space=pl.ANY),
        scratch_shapes=[pltpu.SemaphoreType.DMA((N_DEV - 1,)), pltpu.SemaphoreType.DMA((N_DEV - 1,)), pltpu.SemaphoreType.DMA],
    )(x)


def _exchange_copies(g_ref, land_ref, send_sems, recv_sems, local_sem):
    mx, my, mc = _my_place()
    me = 4 * mx + 2 * my + mc
    mine = pltpu.make_async_copy(g_ref.at[me], land_ref.at[me], local_sem)
    sent, arriving = [], []
    for k in (1, 2, 3, 4, 5, 6, 7):
        px = 1 - mx if k & 4 else mx
        py = 1 - my if k & 2 else my
        pc = 1 - mc if k & 1 else mc
        peer = 4 * px + 2 * py + pc
        sent.append(pltpu.make_async_remote_copy(
            src_ref=g_ref.at[peer], dst_ref=land_ref.at[me], send_sem=send_sems.at[k - 1], recv_sem=recv_sems.at[k - 1],
            device_id=(px, py, pc), device_id_type=MESH))
        arriving.append(pltpu.make_async_remote_copy(
            src_ref=g_ref.at[me], dst_ref=land_ref.at[peer], send_sem=send_sems.at[k - 1], recv_sem=recv_sems.at[k - 1],
            device_id=(px, py, pc), device_id_type=MESH))
    return mine, sent, arriving


def _exchange_start(*refs):
    mine, sent, _ = _exchange_copies(*refs)
    mine.start()
    for cp in sent:
        cp.start()


def _exchange_wait(*refs):
    mine, sent, arriving = _exchange_copies(*refs)
    for cp in arriving:
        cp.wait_recv()
    for cp in sent:
        cp.wait_send()
    mine.wait()


EXCHANGE_SEMAPHORES = (pltpu.SemaphoreType.DMA((N_DEV - 1,)), pltpu.SemaphoreType.DMA((N_DEV - 1,)), pltpu.SemaphoreType.DMA)


def _exchange(g, name):
    def body(g_ref, land_ref, send_sems, recv_sems, local_sem):
        _exchange_start(g_ref, land_ref, send_sems, recv_sems, local_sem)
        _exchange_wait(g_ref, land_ref, send_sems, recv_sems, local_sem)

    return pl.pallas_call(
        body, name=name, out_shape=_sds(g.shape, g.dtype),
        in_specs=[pl.BlockSpec(memory_space=pl.ANY)], out_specs=pl.BlockSpec(memory_space=pl.ANY),
        scratch_shapes=list(EXCHANGE_SEMAPHORES),
    )(g)


def _sum_slots(parts, name):
    n, rows, lanes = parts.shape
    tr = 512 if rows % 512 == 0 else rows

    def body(p_ref, o_ref):
        acc = p_ref[0].astype(F32)
        for j in range(1, n):
            acc = acc + p_ref[j].astype(F32)
        o_ref[...] = acc

    return _call(
        body, name=name, grid=(rows // tr,),
        in_specs=[pl.BlockSpec((n, tr, lanes), lambda i: (0, i, 0))], out_specs=_rows(tr, lanes),
        out_shape=_sds((rows, lanes), F32),
    )(parts)


def _adamw(w, g, m, v, name):
    def body(w_ref, g_ref, m_ref, v_ref, d_ref, nm_ref, nv_ref):
        gv = g_ref[...]
        m_new = ADAM_B1 * m_ref[...] + (1.0 - ADAM_B1) * gv
        v_new = ADAM_B2 * v_ref[...] + (1.0 - ADAM_B2) * (gv * gv)
        nm_ref[...] = m_new
        nv_ref[...] = v_new
        m_hat = m_new / (1.0 - ADAM_B1 ** ADAM_STEP)
        v_hat = v_new / (1.0 - ADAM_B2 ** ADAM_STEP)
        d_ref[...] = -ADAM_LR * (m_hat / (jnp.sqrt(v_hat) + ADAM_EPS) + ADAM_WD * w_ref[...])

    out = _sds(w.shape, F32)
    return pl.pallas_call(
        body, name=name, out_shape=[out, out, out],
        compiler_params=pltpu.CompilerParams(vmem_limit_bytes=V7X_VMEM_LIMIT_BYTES),
    )(w, g, m, v)


def _loss_head(sq, d_model):
    def body(sq_ref, o_ref):
        tot = jnp.sum(jnp.sum(sq_ref[...], axis=1, keepdims=True), axis=0, keepdims=True)
        o_ref[...] = 0.5 * (tot / d_model)

    return pl.pallas_call(body, name="loss_head", out_shape=_sds((1, 1), F32))(sq)


_BIG = (("w_in", 1), ("w_out", 0), ("w_xq", 0), ("w_xk", 0), ("w_xv", 0), ("w_xo", 1), ("w_gate", 1), ("w_up", 1), ("w_down", 0))
_EARLY = ("w_xq", "w_xk", "w_xv", "w_xo", "w_gate", "w_up", "w_down")
_LATE = ("w_in", "w_out")
_SMALL = ("g_mix_pre", "b_f", "rel_bias", "g_mix_post", "g_xattn_pre", "g_mem", "g_xattn_post", "g_ffn_pre", "g_ffn_post")
LANES = 128
BF16_ROW_TILE = 16
BIG_ROW_ALIGN = 512


def _round_up(n, k):
    return -(-n // k) * k


def _row_starts(sizes):
    starts, rows = [], 0
    for n in sizes:
        starts.append(rows)
        rows += _round_up(n, LANES) // LANES
    return starts


def _pack_rows(flat_parts, row_align, dtype):
    lead = flat_parts[0].shape[:-1]
    starts, rows, padded = [], 0, []
    for p in flat_parts:
        n = _round_up(p.shape[-1], LANES)
        starts.append(rows)
        rows += n // LANES
        padded.append(jnp.pad(p.astype(dtype), [(0, 0)] * len(lead) + [(0, n - p.shape[-1])]))
    total = _round_up(rows, row_align)
    padded.append(jnp.zeros(lead + ((total - rows) * LANES,), dtype))
    return jnp.concatenate(padded, axis=-1).reshape(lead + (total, LANES)), starts


def _unpack_rows(buf, starts, shapes):
    lead = buf.shape[:-2]
    flat = buf.reshape(lead + (-1,))
    out = []
    for st, shp in zip(starts, shapes):
        n = int(np.prod(shp))
        out.append(flat[..., st * LANES:st * LANES + n].reshape(lead + tuple(shp)))
    return out


def kernel(x, mem, g_mix_pre, w_in, b_f, rel_bias, w_out, g_mix_post, g_xattn_pre, g_mem, w_xq, w_xk, w_xv, w_xo, g_xattn_post, g_ffn_pre, w_gate, w_up, w_down, g_ffn_post, loss_target, m_g_mix_pre, m_w_in, m_b_f, m_rel_bias, m_w_out, m_g_mix_post, m_g_xattn_pre, m_g_mem, m_w_xq, m_w_xk, m_w_xv, m_w_xo, m_g_xattn_post, m_g_ffn_pre, m_w_gate, m_w_up, m_w_down, m_g_ffn_post, v_g_mix_pre, v_w_in, v_b_f, v_rel_bias, v_w_out, v_g_mix_post, v_g_xattn_pre, v_g_mem, v_w_xq, v_w_xk, v_w_xv, v_w_xo, v_g_xattn_post, v_g_ffn_pre, v_w_gate, v_w_up, v_w_down, v_g_ffn_post):
    given = dict(locals())
    order = ("g_mix_pre", "w_in", "b_f", "rel_bias", "w_out", "g_mix_post", "g_xattn_pre", "g_mem", "w_xq", "w_xk", "w_xv",
             "w_xo", "g_xattn_post", "g_ffn_pre", "w_gate", "w_up", "w_down", "g_ffn_post")
    two_d = lambda a: a.reshape(a.shape[-2:])
    w_loc = {n: two_d(given[n]) for n in order}
    m_loc = {n: two_d(given["m_" + n]) for n in order}
    v_loc = {n: two_d(given["v_" + n]) for n in order}
    d_model = x.shape[-1]

    shard_shapes = [w_loc[n].shape for n, _ in _BIG]
    axis_of = dict(_BIG)

    def gathered_weights(gathered, names, row_starts):
        out = {}
        for n, part in zip(names, _unpack_rows(gathered, row_starts, [w_loc[n].shape for n in names])):
            r, c = part.shape[1:]
            out[n] = part.reshape(N_DEV * r, c) if axis_of[n] == 0 else part.transpose(1, 0, 2).reshape(r, N_DEV * c)
        return out

    first_packed, first_starts = _pack_rows([w_loc["w_in"].reshape(-1)], BF16_ROW_TILE, BF16)
    w_bf = gathered_weights(_all_gather(first_packed, "gather_w_in"), ["w_in"], first_starts)
    later = [n for n, _ in _BIG if n != "w_in"]
    later_packed, later_starts = _pack_rows([w_loc[n].reshape(-1) for n in later], BF16_ROW_TILE, BF16)
    ride_gather = (later_packed, lambda gathered: gathered_weights(gathered, later, later_starts))

    shape_of = dict(zip([n for n, _ in _BIG], shard_shapes))

    def owner_slots(grads, names):
        per_owner = []
        for n in names:
            r, c = shape_of[n]
            gfull = grads[n]
            per_owner.append(gfull.reshape(N_DEV, r * c) if axis_of[n] == 0
                             else gfull.reshape(r, N_DEV, c).transpose(1, 0, 2).reshape(N_DEV, r * c))
        return _pack_rows(per_owner, BIG_ROW_ALIGN, BF16)

    small = {n: w_loc[n] for n in _SMALL}
    sq, grad_x, grads, landed_early = _local_step(
        two_d(x), two_d(mem), two_d(loss_target), small, w_bf, pack_early=lambda gr: owner_slots(gr, _EARLY)[0],
        ride_gather=ride_gather)
    late_slots, late_starts = owner_slots(grads, _LATE)
    landed_late = _exchange(late_slots, "exchange_grads")
    early_starts = _row_starts([int(np.prod(shape_of[n])) for n in _EARLY])
    g_big = {}
    for names, landed, row_starts, label in ((_EARLY, landed_early, early_starts, "sum_grads_early"),
                                             (_LATE, landed_late, late_starts, "sum_grads_late")):
        g_big.update(zip(names, _unpack_rows(_sum_slots(landed, label), row_starts, [shape_of[n] for n in names])))

    small_parts = [grads[n].reshape(-1) for n in _SMALL] + [sq.reshape(-1)]
    small_shapes = [w_loc[n].shape for n in _SMALL] + [sq.shape]
    spacked, sstarts = _pack_rows(small_parts, 8, F32)
    ssum = _sum_slots(_all_gather(spacked, "gather_small"), "sum_small")
    g_small = dict(zip(_SMALL, _unpack_rows(ssum, sstarts, small_shapes)[:-1]))
    sq_rows = sq.size // LANES
    loss = _loss_head(ssum[sstarts[-1]:sstarts[-1] + sq_rows], d_model).reshape(())

    g_loc, delta, new_m, new_v = {}, {}, {}, {}
    for n, _ in _BIG:
        g_loc[n] = g_big[n]
        delta[n], new_m[n], new_v[n] = _adamw(w_loc[n], g_big[n], m_loc[n], v_loc[n], "adamw_" + n)
    pk = lambda d: _pack_rows([d[n].reshape(-1) for n in _SMALL], 8, F32)[0]
    pstarts = _pack_rows([w_loc[n].reshape(-1) for n in _SMALL], 8, F32)[1]
    d_s, m_s, v_s = _adamw(pk(w_loc), pk(g_small), pk(m_loc), pk(v_loc), "adamw_small")
    shapes_s = [w_loc[n].shape for n in _SMALL]
    for n, dd, mm, vv in zip(_SMALL, _unpack_rows(d_s, pstarts, shapes_s), _unpack_rows(m_s, pstarts, shapes_s),
                             _unpack_rows(v_s, pstarts, shapes_s)):
        g_loc[n], delta[n], new_m[n], new_v[n] = g_small[n], dd, mm, vv

    like = lambda d: [d[n].reshape(given[n].shape) for n in order]
    return (loss, grad_x.reshape(x.shape), *like(g_loc), *like(delta), *like(new_m), *like(new_v))
```
